```python
import jax, jax.numpy as jnp
from jax import lax
import numpy as np

D_MODEL = 1024
BATCH = 8
SEQ = 8192
DEPTH = 2

POOL_WIDTH = 256
POOL_GROUPS = 4
POOL_WINDOWS = (2, 4, 8, 16)
CONV_WIDTH = 256
CONV_KERNEL = 31
HEAD_DIM = 64
N_Q_HEADS = 8
N_KV_HEADS = 2
Q_PER_KV = N_Q_HEADS // N_KV_HEADS
ATTN_WIDTH = N_Q_HEADS * HEAD_DIM
KV_WIDTH = N_KV_HEADS * HEAD_DIM
WINDOW = 128
BLOCK = 128
D_MIX = POOL_WIDTH + CONV_WIDTH + ATTN_WIDTH
D_IN = 2 * POOL_WIDTH + 3 * CONV_WIDTH + 2 * ATTN_WIDTH + 2 * KV_WIDTH
EPS = 1e-6

kernel_name = "hybrid_pool_conv_swa_block"


def rms_norm(x, g):
    xf = x.astype(jnp.float32)
    y = xf * lax.rsqrt(jnp.mean(xf * xf, axis=-1, keepdims=True) + EPS)
    return (y * g.astype(jnp.float32)).astype(x.dtype)


def layer_norm(x, g, b):
    xf = x.astype(jnp.float32)
    mu = jnp.mean(xf, axis=-1, keepdims=True)
    var = jnp.mean(jnp.square(xf - mu), axis=-1, keepdims=True)
    y = (xf - mu) * lax.rsqrt(var + EPS)
    return (y * g.astype(jnp.float32) + b.astype(jnp.float32)).astype(x.dtype)


def alibi_slopes():
    return jnp.asarray([2.0 ** (-8.0 * (h + 1) / N_Q_HEADS) for h in range(N_Q_HEADS)], dtype=jnp.float32)


def pool_mixer(u, w, scale):
    B, S, _ = u.shape
    gw = POOL_WIDTH // POOL_GROUPS
    uf = u.astype(jnp.float32)
    c = jnp.cumsum(uf, axis=1)
    t = jnp.arange(S, dtype=jnp.float32)[:, None]
    outs = []
    for g, wnd in enumerate(POOL_WINDOWS):
        cg = c[..., g * gw:(g + 1) * gw]
        prev = jnp.pad(cg, ((0, 0), (wnd, 0), (0, 0)))[:, :S]
        cnt = jnp.minimum(t + 1.0, float(wnd))
        outs.append((cg - prev) / cnt)
    pooled = jnp.stack(outs, axis=2)
    diff = (pooled - uf.reshape(B, S, POOL_GROUPS, gw)).astype(u.dtype)
    y = jnp.einsum('bsgc,gcd->bsgd', diff, w).reshape(B, S, POOL_WIDTH)
    return y * scale


def conv_module(a, b, dw, dw_bias, ln_g, ln_b, pw):
    h = a * jax.nn.sigmoid(b)
    h = jnp.pad(h, ((0, 0), (CONV_KERNEL - 1, 0), (0, 0)))
    h = lax.conv_general_dilated(h, dw[:, None, :], window_strides=(1,), padding='VALID',
                                 dimension_numbers=('NWC', 'WIO', 'NWC'),
                                 feature_group_count=CONV_WIDTH) + dw_bias
    h = layer_norm(h, ln_g, ln_b)
    h = jax.nn.silu(h)
    return jnp.einsum('bsc,cd->bsd', h, pw)


def sliding_window_attention(q, k, v, sinks):
    B, S, _ = q.shape
    nb = S // BLOCK
    qb = q.reshape(B, nb, BLOCK, N_KV_HEADS, Q_PER_KV, HEAD_DIM) * (HEAD_DIM ** -0.5)
    kr = jnp.pad(k.reshape(B, nb, BLOCK, N_KV_HEADS, HEAD_DIM), ((0, 0), (1, 0), (0, 0), (0, 0), (0, 0)))
    vr = jnp.pad(v.reshape(B, nb, BLOCK, N_KV_HEADS, HEAD_DIM), ((0, 0), (1, 0), (0, 0), (0, 0), (0, 0)))
    kb = jnp.concatenate([kr[:, :-1], kr[:, 1:]], axis=2)
    vb = jnp.concatenate([vr[:, :-1], vr[:, 1:]], axis=2)
    scores = jnp.einsum('bnqkgd,bnskd->bnkgqs', qb, kb).astype(jnp.float32)
    i = jnp.arange(BLOCK)[:, None]
    j = jnp.arange(2 * BLOCK)[None, :]
    dist = BLOCK + i - j
    in_band = (dist >= 0) & (dist < WINDOW)
    key_exists = ~(((jnp.arange(nb) == 0)[:, None, None]) & (j < BLOCK)[None])
    valid = in_band[None] & key_exists
    bias = -alibi_slopes().reshape(N_KV_HEADS, Q_PER_KV)[:, :, None, None] * dist.astype(jnp.float32)
    scores = jnp.where(valid[None, :, None, None], scores + bias[None, None], -1e30)
    sink = jnp.broadcast_to(sinks.astype(jnp.float32).reshape(N_KV_HEADS, Q_PER_KV)[None, None, :, :, None, None],
                            scores.shape[:-1] + (1,))
    p = jax.nn.softmax(jnp.concatenate([scores, sink], axis=-1), axis=-1)[..., :-1]
    out = jnp.einsum('bnkgqs,bnskd->bnqkgd', p.astype(v.dtype), vb)
    return out.reshape(B, S, ATTN_WIDTH)


def split_columns(proj):
    sizes = (POOL_WIDTH, POOL_WIDTH, CONV_WIDTH, CONV_WIDTH, CONV_WIDTH,
             ATTN_WIDTH, KV_WIDTH, KV_WIDTH, ATTN_WIDTH)
    idx = [int(s) for s in np.cumsum(sizes)[:-1]]
    return jnp.split(proj, idx, axis=-1)


def _fwd_setup_inputs(seed: int = 0) -> dict:
    key = jax.random.key(seed)
    ks = jax.random.split(key, 14)
    f32 = jnp.float32
    gw = POOL_WIDTH // POOL_GROUPS
    x = jax.random.normal(ks[0], (BATCH, SEQ, D_MODEL), f32)
    ln_g = 1.0 + 0.05 * jax.random.normal(ks[1], (DEPTH, D_MODEL), f32)
    w_in = jax.random.normal(ks[2], (DEPTH, D_MODEL, D_IN), f32) * D_MODEL ** -0.5
    pool_w = jax.random.normal(ks[3], (DEPTH, POOL_GROUPS, gw, gw), f32) * gw ** -0.5
    pool_scale = 0.5 + 0.05 * jax.random.normal(ks[4], (DEPTH, POOL_WIDTH), f32)
    conv_dw = jax.random.normal(ks[5], (DEPTH, CONV_KERNEL, CONV_WIDTH), f32) * CONV_KERNEL ** -0.5
    conv_b = 0.02 * jax.random.normal(ks[6], (DEPTH, CONV_WIDTH), f32)
    conv_ln_g = 1.0 + 0.05 * jax.random.normal(ks[7], (DEPTH, CONV_WIDTH), f32)
    conv_ln_b = 0.02 * jax.random.normal(ks[8], (DEPTH, CONV_WIDTH), f32)
    conv_pw = jax.random.normal(ks[9], (DEPTH, CONV_WIDTH, CONV_WIDTH), f32) * CONV_WIDTH ** -0.5
    attn_sinks = 0.5 * jax.random.normal(ks[10], (DEPTH, N_Q_HEADS), f32)
    w_out = jax.random.normal(ks[11], (DEPTH, D_MIX, D_MODEL), f32) * D_MIX ** -0.5
    final_g = 1.0 + 0.05 * jax.random.normal(ks[12], (D_MODEL,), f32)
    return {"x": x, "ln_g": ln_g, "w_in": w_in, "pool_w": pool_w, "pool_scale": pool_scale,
            "conv_dw": conv_dw, "conv_b": conv_b, "conv_ln_g": conv_ln_g, "conv_ln_b": conv_ln_b,
            "conv_pw": conv_pw, "attn_sinks": attn_sinks, "w_out": w_out, "final_g": final_g}


def _fwd_reference(x, ln_g, w_in, pool_w, pool_scale, conv_dw, conv_b, conv_ln_g, conv_ln_b,
              conv_pw, attn_sinks, w_out, final_g):
    for l in range(DEPTH):
        h = rms_norm(x, ln_g[l])
        proj = jnp.einsum('bsd,de->bse', h, w_in[l])
        (u_pool, g_pool, c_a, c_b, g_conv, q, k, v, g_attn) = split_columns(proj)
        y_pool = pool_mixer(u_pool, pool_w[l], pool_scale[l])
        y_conv = conv_module(c_a, c_b, conv_dw[l], conv_b[l], conv_ln_g[l], conv_ln_b[l], conv_pw[l])
        y_attn = sliding_window_attention(q, k, v, attn_sinks[l])
        y = jnp.concatenate([y_pool * jax.nn.silu(g_pool),
                             y_conv * jax.nn.silu(g_conv),
                             y_attn * jax.nn.silu(g_attn)], axis=-1)
        x = x + jnp.einsum('bse,ed->bsd', y, w_out[l])
    return rms_norm(x, final_g)


import jax as _jax
import jax.numpy as _jnp

TWIN_FORMAT = 'train_step'
FWD_PARAMS = ['x', 'ln_g', 'w_in', 'pool_w', 'pool_scale', 'conv_dw', 'conv_b', 'conv_ln_g', 'conv_ln_b', 'conv_pw', 'attn_sinks', 'w_out', 'final_g']
TWIN_WEIGHTS = ['ln_g', 'w_in', 'pool_w', 'pool_scale', 'conv_dw', 'conv_b', 'conv_ln_g', 'conv_ln_b', 'conv_pw', 'attn_sinks', 'w_out', 'final_g']
TWIN_DIFF_INPUT = 'x'
TWIN_INPUTS = ['x', 'ln_g', 'w_in', 'pool_w', 'pool_scale', 'conv_dw', 'conv_b', 'conv_ln_g', 'conv_ln_b', 'conv_pw', 'attn_sinks', 'w_out', 'final_g', 'loss_target', 'm_ln_g', 'm_w_in', 'm_pool_w', 'm_pool_scale', 'm_conv_dw', 'm_conv_b', 'm_conv_ln_g', 'm_conv_ln_b', 'm_conv_pw', 'm_attn_sinks', 'm_w_out', 'm_final_g', 'v_ln_g', 'v_w_in', 'v_pool_w', 'v_pool_scale', 'v_conv_dw', 'v_conv_b', 'v_conv_ln_g', 'v_conv_ln_b', 'v_conv_pw', 'v_attn_sinks', 'v_w_out', 'v_final_g']
TWIN_OUTPUTS = ['loss', 'grad_x', 'grad_ln_g', 'grad_w_in', 'grad_pool_w', 'grad_pool_scale', 'grad_conv_dw', 'grad_conv_b', 'grad_conv_ln_g', 'grad_conv_ln_b', 'grad_conv_pw', 'grad_attn_sinks', 'grad_w_out', 'grad_final_g', 'delta_ln_g', 'delta_w_in', 'delta_pool_w', 'delta_pool_scale', 'delta_conv_dw', 'delta_conv_b', 'delta_conv_ln_g', 'delta_conv_ln_b', 'delta_conv_pw', 'delta_attn_sinks', 'delta_w_out', 'delta_final_g', 'new_m_ln_g', 'new_m_w_in', 'new_m_pool_w', 'new_m_pool_scale', 'new_m_conv_dw', 'new_m_conv_b', 'new_m_conv_ln_g', 'new_m_conv_ln_b', 'new_m_conv_pw', 'new_m_attn_sinks', 'new_m_w_out', 'new_m_final_g', 'new_v_ln_g', 'new_v_w_in', 'new_v_pool_w', 'new_v_pool_scale', 'new_v_conv_dw', 'new_v_conv_b', 'new_v_conv_ln_g', 'new_v_conv_ln_b', 'new_v_conv_pw', 'new_v_attn_sinks', 'new_v_w_out', 'new_v_final_g']
TWIN_LEAF_KINDS = {'loss': 'loss', 'grad_x': 'grad_x', 'grad_ln_g': 'grad_w', 'grad_w_in': 'grad_w', 'grad_pool_w': 'grad_w', 'grad_pool_scale': 'grad_w', 'grad_conv_dw': 'grad_w', 'grad_conv_b': 'grad_w', 'grad_conv_ln_g': 'grad_w', 'grad_conv_ln_b': 'grad_w', 'grad_conv_pw': 'grad_w', 'grad_attn_sinks': 'grad_w', 'grad_w_out': 'grad_w', 'grad_final_g': 'grad_w', 'delta_ln_g': 'delta_w', 'delta_w_in': 'delta_w', 'delta_pool_w': 'delta_w', 'delta_pool_scale': 'delta_w', 'delta_conv_dw': 'delta_w', 'delta_conv_b': 'delta_w', 'delta_conv_ln_g': 'delta_w', 'delta_conv_ln_b': 'delta_w', 'delta_conv_pw': 'delta_w', 'delta_attn_sinks': 'delta_w', 'delta_w_out': 'delta_w', 'delta_final_g': 'delta_w', 'new_m_ln_g': 'new_m', 'new_m_w_in': 'new_m', 'new_m_pool_w': 'new_m', 'new_m_pool_scale': 'new_m', 'new_m_conv_dw': 'new_m', 'new_m_conv_b': 'new_m', 'new_m_conv_ln_g': 'new_m', 'new_m_conv_ln_b': 'new_m', 'new_m_conv_pw': 'new_m', 'new_m_attn_sinks': 'new_m', 'new_m_w_out': 'new_m', 'new_m_final_g': 'new_m', 'new_v_ln_g': 'new_v', 'new_v_w_in': 'new_v', 'new_v_pool_w': 'new_v', 'new_v_pool_scale': 'new_v', 'new_v_conv_dw': 'new_v', 'new_v_conv_b': 'new_v', 'new_v_conv_ln_g': 'new_v', 'new_v_conv_ln_b': 'new_v', 'new_v_conv_pw': 'new_v', 'new_v_attn_sinks': 'new_v', 'new_v_w_out': 'new_v', 'new_v_final_g': 'new_v'}


def _forward(args):
    return _fwd_reference(*[args[k] for k in FWD_PARAMS])


def _output_shape():
    def fwd():
        inp = _fwd_setup_inputs(0)
        return _fwd_reference(*[inp[k] for k in FWD_PARAMS])
    out = _jax.eval_shape(fwd)
    return out.shape, out.dtype

N_MICROBATCH = 1
ADAM_LR = 0.001
ADAM_B1 = 0.9
ADAM_B2 = 0.999
ADAM_EPS = 1e-08
ADAM_WD = 0.01
ADAM_STEP = 10
PER_EXAMPLE_BATCH_AXIS = {'x': 0, 'loss_target': 0}
SHARED_INPUTS = []
_WEIGHT_DTYPES = {'ln_g': _jnp.float32, 'w_in': _jnp.float32, 'pool_w': _jnp.float32, 'pool_scale': _jnp.float32, 'conv_dw': _jnp.float32, 'conv_b': _jnp.float32, 'conv_ln_g': _jnp.float32, 'conv_ln_b': _jnp.float32, 'conv_pw': _jnp.float32, 'attn_sinks': _jnp.float32, 'w_out': _jnp.float32, 'final_g': _jnp.float32}
MOMENT_SCALE = {'ln_g': 1.002481e-01, 'w_in': 6.313156e-02, 'pool_w': 6.596839e-02, 'pool_scale': 1.313299e-01, 'conv_dw': 9.228993e-02, 'conv_b': 2.031972e-01, 'conv_ln_g': 1.110782e-01, 'conv_ln_b': 9.348347e-02, 'conv_pw': 8.840276e-02, 'attn_sinks': 4.940029e-02, 'w_out': 6.327953e-02, 'final_g': 6.426679e+01}


def _to_microbatches(a, axis):
    t = _jnp.moveaxis(a, axis, 0)
    t = t.reshape((N_MICROBATCH, t.shape[0] // N_MICROBATCH) + t.shape[1:])
    return _jnp.moveaxis(t, 1, axis + 1)


def setup_inputs(seed: int = 0) -> dict:
    inp = _fwd_setup_inputs(seed)
    key = _jax.random.fold_in(_jax.random.key(seed), 7919)
    shape, _ = _output_shape()
    out = dict(inp)
    out["loss_target"] = _jax.random.normal(_jax.random.fold_in(key, 0), shape, _jnp.float32)
    for i, name in enumerate(TWIN_WEIGHTS):
        w = inp[name].astype(_jnp.float32)
        if MOMENT_SCALE is None:
            s = _jnp.sqrt(_jnp.mean(_jnp.square(w)) + 1e-30)
        else:
            s = MOMENT_SCALE[name]
        km, kv = _jax.random.split(_jax.random.fold_in(key, i + 1))
        out[name] = w
        out["m_" + name] = s * _jax.random.normal(km, w.shape, _jnp.float32)
        out["v_" + name] = (s * s) * _jax.random.uniform(kv, w.shape, _jnp.float32, 0.5, 1.5)
    if N_MICROBATCH > 1:
        for name, axis in PER_EXAMPLE_BATCH_AXIS.items():
            out[name] = _to_microbatches(out[name], axis)
    return {'x': out['x'], 'ln_g': out['ln_g'], 'w_in': out['w_in'], 'pool_w': out['pool_w'], 'pool_scale': out['pool_scale'], 'conv_dw': out['conv_dw'], 'conv_b': out['conv_b'], 'conv_ln_g': out['conv_ln_g'], 'conv_ln_b': out['conv_ln_b'], 'conv_pw': out['conv_pw'], 'attn_sinks': out['attn_sinks'], 'w_out': out['w_out'], 'final_g': out['final_g'], 'loss_target': out['loss_target'], 'm_ln_g': out['m_ln_g'], 'm_w_in': out['m_w_in'], 'm_pool_w': out['m_pool_w'], 'm_pool_scale': out['m_pool_scale'], 'm_conv_dw': out['m_conv_dw'], 'm_conv_b': out['m_conv_b'], 'm_conv_ln_g': out['m_conv_ln_g'], 'm_conv_ln_b': out['m_conv_ln_b'], 'm_conv_pw': out['m_conv_pw'], 'm_attn_sinks': out['m_attn_sinks'], 'm_w_out': out['m_w_out'], 'm_final_g': out['m_final_g'], 'v_ln_g': out['v_ln_g'], 'v_w_in': out['v_w_in'], 'v_pool_w': out['v_pool_w'], 'v_pool_scale': out['v_pool_scale'], 'v_conv_dw': out['v_conv_dw'], 'v_conv_b': out['v_conv_b'], 'v_conv_ln_g': out['v_conv_ln_g'], 'v_conv_ln_b': out['v_conv_ln_b'], 'v_conv_pw': out['v_conv_pw'], 'v_attn_sinks': out['v_attn_sinks'], 'v_w_out': out['v_w_out'], 'v_final_g': out['v_final_g']}


def _loss(weights, diff, rest, loss_target):
    with _jax.named_scope("forward"):
        args = {**rest, TWIN_DIFF_INPUT: diff, **{k: w.astype(_WEIGHT_DTYPES[k]) for k, w in weights.items()}}
        y = _forward(args)
    with _jax.named_scope("loss_head"):
        err = _jnp.square(y.astype(_jnp.float32) - loss_target)
        return 0.5 * _jnp.sum(_jnp.mean(err, axis=-1)) if err.ndim else 0.5 * err


def _adamw(w, g, m, v):
    m = ADAM_B1 * m + (1.0 - ADAM_B1) * g
    v = ADAM_B2 * v + (1.0 - ADAM_B2) * _jnp.square(g)
    m_hat = m / (1.0 - ADAM_B1 ** ADAM_STEP)
    v_hat = v / (1.0 - ADAM_B2 ** ADAM_STEP)
    delta = -ADAM_LR * (m_hat / (_jnp.sqrt(v_hat) + ADAM_EPS) + ADAM_WD * w)
    return delta, m, v


def reference(x, ln_g, w_in, pool_w, pool_scale, conv_dw, conv_b, conv_ln_g, conv_ln_b, conv_pw, attn_sinks, w_out, final_g, loss_target, m_ln_g, m_w_in, m_pool_w, m_pool_scale, m_conv_dw, m_conv_b, m_conv_ln_g, m_conv_ln_b, m_conv_pw, m_attn_sinks, m_w_out, m_final_g, v_ln_g, v_w_in, v_pool_w, v_pool_scale, v_conv_dw, v_conv_b, v_conv_ln_g, v_conv_ln_b, v_conv_pw, v_attn_sinks, v_w_out, v_final_g):
    given = dict(x=x, ln_g=ln_g, w_in=w_in, pool_w=pool_w, pool_scale=pool_scale, conv_dw=conv_dw, conv_b=conv_b, conv_ln_g=conv_ln_g, conv_ln_b=conv_ln_b, conv_pw=conv_pw, attn_sinks=attn_sinks, w_out=w_out, final_g=final_g, loss_target=loss_target, m_ln_g=m_ln_g, m_w_in=m_w_in, m_pool_w=m_pool_w, m_pool_scale=m_pool_scale, m_conv_dw=m_conv_dw, m_conv_b=m_conv_b, m_conv_ln_g=m_conv_ln_g, m_conv_ln_b=m_conv_ln_b, m_conv_pw=m_conv_pw, m_attn_sinks=m_attn_sinks, m_w_out=m_w_out, m_final_g=m_final_g, v_ln_g=v_ln_g, v_w_in=v_w_in, v_pool_w=v_pool_w, v_pool_scale=v_pool_scale, v_conv_dw=v_conv_dw, v_conv_b=v_conv_b, v_conv_ln_g=v_conv_ln_g, v_conv_ln_b=v_conv_ln_b, v_conv_pw=v_conv_pw, v_attn_sinks=v_attn_sinks, v_w_out=v_w_out, v_final_g=v_final_g)
    weights = {n: given[n] for n in TWIN_WEIGHTS}
    shared = {n: given[n] for n in SHARED_INPUTS}
    per_example = {n: given[n] for n in ['x']}
    grad_fn = _jax.value_and_grad(_loss, argnums=(0, 1))

    def one_microbatch(ex, loss_target):
        ex = dict(ex)
        diff = ex.pop(TWIN_DIFF_INPUT)
        return grad_fn(weights, diff, {**shared, **ex}, loss_target)

    if N_MICROBATCH == 1:
        loss, (grad_w, grad_x) = one_microbatch(per_example, given["loss_target"])
    else:
        def body(carry, xs):
            loss_sum, grad_sum = carry
            l_k, (gw_k, gx_k) = one_microbatch(xs[0], xs[1])
            with _jax.named_scope("update"):
                return (loss_sum + l_k, _jax.tree.map(_jnp.add, grad_sum, gw_k)), gx_k

        init = (_jnp.zeros((), _jnp.float32), _jax.tree.map(_jnp.zeros_like, weights))
        (loss, grad_w), grad_x = _jax.lax.scan(body, init, (per_example, given["loss_target"]))
    with _jax.named_scope("update"):
        delta_w, new_m, new_v = {}, {}, {}
        for n in TWIN_WEIGHTS:
            delta_w[n], new_m[n], new_v[n] = _adamw(weights[n], grad_w[n], given["m_" + n], given["v_" + n])
    return (loss, grad_x, *[grad_w[n] for n in TWIN_WEIGHTS], *[delta_w[n] for n in TWIN_WEIGHTS],
            *[new_m[n] for n in TWIN_WEIGHTS], *[new_v[n] for n in TWIN_WEIGHTS])
```

```python
import jax
import jax.numpy as jnp
from jax import lax
from jax.experimental import pallas as pl
from jax.experimental.pallas import tpu as pltpu

F32 = jnp.float32
MXU_DTYPE = jnp.bfloat16

N_DEV = 8
DEPTH = 2
POOL_WIDTH = 256
POOL_GROUP = 64
CONV_WIDTH = 256
CONV_KERNEL = 31
CONV_TAPS_PAD = 32
HEAD_DIM = 64
N_KV_HEADS = 2
Q_PER_KV = 4
N_Q_HEADS = 8
ATTN_WIDTH = 512
BLOCK = 128
D_MIX = 1024
D_IN = 2560
HALF_IN = 1280
EPS = 1e-6
SCALE = HEAD_DIM ** -0.5
NEG = -1e30

ADAM_LR = 0.001
ADAM_B1 = 0.9
ADAM_B2 = 0.999
ADAM_EPS = 1e-08
ADAM_WD = 0.01
ADAM_STEP = 10

HALO = 32
ROW_TILE = 512
IN_BWD_TILE = 256
VMEM_LIMIT = 56 * 1024 * 1024

_NN = (((1,), (0,)), ((), ()))
_NT = (((1,), (1,)), ((), ()))
_TN = (((0,), (0,)), ((), ()))


def _mm(a, b, dims=_NN):
    return lax.dot_general(a.astype(MXU_DTYPE), b.astype(MXU_DTYPE), dims, preferred_element_type=F32)


def _sig(x):
    return 1.0 / (1.0 + jnp.exp(-x))


def _dsilu(z, s):
    return s * (1.0 + z * (1.0 - s))


def _params(n_grid):
    return pltpu.CompilerParams(dimension_semantics=("arbitrary",) * n_grid, vmem_limit_bytes=VMEM_LIMIT)


def _row_tile(rows, cap):
    t = min(rows, cap)
    while rows % t or t % 8:
        t -= 8
    return t


def _full(shape):
    return pl.BlockSpec(shape, lambda i: (0,) * len(shape))


def _by_group(lane, v2, v4, v8, v16):
    return jnp.where(lane < 64, v2, jnp.where(lane < 128, v4, jnp.where(lane < 192, v8, v16)))


def _pool_count(t0, n):
    lane = lax.broadcasted_iota(jnp.int32, (1, POOL_WIDTH), 1)
    t = (t0 + lax.broadcasted_iota(jnp.int32, (n, 1), 0)).astype(F32)
    wnd = _by_group(lane, 2.0, 4.0, 8.0, 16.0)
    return jnp.minimum(t + 1.0, wnd)


def _pool_diff(u_ext, t0, ts):
    lane = lax.broadcasted_iota(jnp.int32, (1, POOL_WIDTH), 1)
    s2 = u_ext + pltpu.roll(u_ext, 1, 0)
    s4 = s2 + pltpu.roll(s2, 2, 0)
    s8 = s4 + pltpu.roll(s4, 4, 0)
    s16 = s8 + pltpu.roll(s8, 8, 0)
    pooled = _by_group(lane, s2, s4, s8, s16)[HALO:]
    return pooled / _pool_count(t0, ts) - u_ext[HALO:]


def _pool_diff_bwd(w, ts):
    n = w.shape[0]
    lane = lax.broadcasted_iota(jnp.int32, (1, POOL_WIDTH), 1)
    f2 = w + pltpu.roll(w, n - 1, 0)
    f4 = f2 + pltpu.roll(f2, n - 2, 0)
    f8 = f4 + pltpu.roll(f4, n - 4, 0)
    f16 = f8 + pltpu.roll(f8, n - 8, 0)
    return _by_group(lane, f2, f4, f8, f16)[:ts]


def _shifted_down(x):
    return [x if r == 0 else pltpu.roll(x, r, 0) for r in range(8)]


def _causal_conv(hh_rolled, dw_ref, base, n_out):
    acc = None
    for r in range(8):
        for m in range(4):
            d = 8 * m + r
            if d >= CONV_KERNEL:
                continue
            term = dw_ref[pl.ds(CONV_KERNEL - 1 - d, 1), :] * hh_rolled[r][base - 8 * m: base - 8 * m + n_out]
            acc = term if acc is None else acc + term
    return acc


def _anticausal_conv(dcv, dw_ref, n_out):
    n = dcv.shape[0]
    acc = None
    for r in range(8):
        up = dcv if r == 0 else pltpu.roll(dcv, n - r, 0)
        for m in range(4):
            d = 8 * m + r
            if d >= CONV_KERNEL:
                continue
            term = dw_ref[pl.ds(CONV_KERNEL - 1 - d, 1), :] * up[8 * m: 8 * m + n_out]
            acc = term if acc is None else acc + term
    return acc


def _layer_norm(cv):
    mu = jnp.mean(cv, axis=-1, keepdims=True)
    xc = cv - mu
    var = jnp.mean(xc * xc, axis=-1, keepdims=True)
    rstd = lax.rsqrt(var + EPS)
    return xc * rstd, rstd


def _stack_heads(a, kh):
    return jnp.concatenate(
        [a[:, HEAD_DIM * (Q_PER_KV * kh + g): HEAD_DIM * (Q_PER_KV * kh + g + 1)] for g in range(Q_PER_KV)], axis=0)


def _unstack_heads(parts):
    return jnp.concatenate([p[BLOCK * g: BLOCK * (g + 1)] for p in parts for g in range(Q_PER_KV)], axis=-1)


def _per_head_rows(vals):
    g = lax.shift_right_logical(lax.broadcasted_iota(jnp.int32, (Q_PER_KV * BLOCK, 1), 0), 7)
    return jnp.where(g == 0, vals[0], jnp.where(g == 1, vals[1], jnp.where(g == 2, vals[2], vals[3])))


def _attn_probs(q4, k2, kh, first_block, sinks_ref):
    rows = lax.broadcasted_iota(jnp.int32, (Q_PER_KV * BLOCK, 2 * BLOCK), 0)
    cols = lax.broadcasted_iota(jnp.int32, (Q_PER_KV * BLOCK, 2 * BLOCK), 1)
    dist = BLOCK + (rows & (BLOCK - 1)) - cols
    valid = (dist >= 0) & (dist < BLOCK) & ((cols >= BLOCK) | jnp.logical_not(first_block))
    slope = _per_head_rows([2.0 ** -(Q_PER_KV * kh + g + 1) for g in range(Q_PER_KV)])
    sink = _per_head_rows([sinks_ref[Q_PER_KV * kh + g] for g in range(Q_PER_KV)])
    s = _mm(q4, k2, _NT) * SCALE
    s = jnp.where(valid, s - slope * dist.astype(F32), NEG)
    m = jnp.maximum(jnp.max(s, axis=-1, keepdims=True), sink)
    e = jnp.exp(s - m)
    es = jnp.exp(sink - m)
    inv = 1.0 / (jnp.sum(e, axis=-1, keepdims=True) + es)
    return e * inv, es * inv


def _in_proj(x, g, w_t):
    s_len, d = x.shape
    ts = _row_tile(s_len, ROW_TILE)

    def body(x_ref, g_ref, w_ref, o_ref):
        xv = x_ref[...]
        r = lax.rsqrt(jnp.mean(xv * xv, axis=-1, keepdims=True) + EPS)
        o_ref[...] = _mm(xv * r * g_ref[...], w_ref[...], _NT)

    return pl.pallas_call(
        body, name="in_proj", grid=(s_len // ts,),
        in_specs=[pl.BlockSpec((ts, d), lambda i: (i, 0)), _full((1, d)), _full((D_IN, d))],
        out_specs=pl.BlockSpec((ts, D_IN), lambda i: (i, 0)),
        out_shape=jax.ShapeDtypeStruct((s_len, D_IN), F32),
        compiler_params=_params(1),
    )(x, g, w_t)


def _poolconv_fwd(proj, wp, scale, dw, cb, lng, lnb, pw):
    s_len = proj.shape[0]
    ts = _row_tile(s_len, ROW_TILE)
    hb = ts // HALO

    def body(p_ref, ph_ref, wp_ref, sc_ref, dw_ref, cb_ref, lng_ref, lnb_ref, pw_ref, y_ref):
        i = pl.program_id(0)
        cur = p_ref[...]
        halo = jnp.where(i > 0, ph_ref[...], 0.0)
        ext = jnp.concatenate([halo, cur], axis=0)
        diff = _pool_diff(ext[:, 0:256], i * ts, ts)
        gp = cur[:, 256:512]
        y_pool = _mm(diff, wp_ref[...]) * sc_ref[...] * (gp * _sig(gp))
        hh = ext[:, 512:768] * _sig(ext[:, 768:1024])
        cv = _causal_conv(_shifted_down(hh), dw_ref, HALO, ts) + cb_ref[...]
        n, _ = _layer_norm(cv)
        z = n * lng_ref[...] + lnb_ref[...]
        gc = cur[:, 1024:1280]
        y_conv = _mm(z * _sig(z), pw_ref[...]) * (gc * _sig(gc))
        y_ref[...] = jnp.concatenate([y_pool, y_conv], axis=-1).astype(y_ref.dtype)

    vec = _full((1, 256))
    return pl.pallas_call(
        body, name="poolconv_fwd", grid=(s_len // ts,),
        in_specs=[pl.BlockSpec((ts, HALF_IN), lambda i: (i, 0)),
                  pl.BlockSpec((HALO, HALF_IN), lambda i: (jnp.maximum(i * hb - 1, 0), 0)),
                  _full((256, 256)), vec, _full((CONV_TAPS_PAD, 256)), vec, vec, vec, _full((256, 256))],
        out_specs=pl.BlockSpec((ts, 512), lambda i: (i, 0)),
        out_shape=jax.ShapeDtypeStruct((s_len, 512), MXU_DTYPE),
        compiler_params=_params(1),
    )(proj, proj, wp, scale, dw, cb, lng, lnb, pw)


def _attn_fwd(proj, sinks):
    s_len = proj.shape[0]

    def body(p_ref, kvp_ref, sinks_ref, y_ref):
        i = pl.program_id(0)
        pc = p_ref[...]
        kvp = kvp_ref[...]
        q, k, v, ga = pc[:, 0:512], pc[:, 512:640], pc[:, 640:768], pc[:, 768:1280]
        outs = []
        for kh in range(N_KV_HEADS):
            lo, hi = HEAD_DIM * kh, HEAD_DIM * (kh + 1)
            k2 = jnp.concatenate([kvp[:, lo:hi], k[:, lo:hi]], axis=0).astype(MXU_DTYPE)
            v2 = jnp.concatenate([kvp[:, 128 + lo:128 + hi], v[:, lo:hi]], axis=0).astype(MXU_DTYPE)
            q4 = _stack_heads(q, kh).astype(MXU_DTYPE)
            p, _ = _attn_probs(q4, k2, kh, i == 0, sinks_ref)
            outs.append(_mm(p, v2))
        y_ref[...] = (_unstack_heads(outs) * (ga * _sig(ga))).astype(y_ref.dtype)

    return pl.pallas_call(
        body, name="attn_fwd", grid=(s_len // BLOCK,),
        in_specs=[pl.BlockSpec((BLOCK, HALF_IN), lambda i: (i, 1)),
                  pl.BlockSpec((BLOCK, 256), lambda i: (jnp.maximum(i - 1, 0), 7)),
                  pl.BlockSpec(memory_space=pltpu.SMEM)],
        out_specs=pl.BlockSpec((BLOCK, 512), lambda i: (i, 0)),
        out_shape=jax.ShapeDtypeStruct((s_len, 512), MXU_DTYPE),
        compiler_params=_params(1),
    )(proj, proj, sinks)


def _out_proj(x, y_pc, y_at, w_out):
    s_len, d = x.shape
    ts = _row_tile(s_len, ROW_TILE)

    def body(x_ref, a_ref, b_ref, w_ref, o_ref):
        y = jnp.concatenate([a_ref[...], b_ref[...]], axis=-1)
        o_ref[...] = x_ref[...] + _mm(y, w_ref[...])

    return pl.pallas_call(
        body, name="out_proj", grid=(s_len // ts,),
        in_specs=[pl.BlockSpec((ts, d), lambda i: (i, 0)), pl.BlockSpec((ts, 512), lambda i: (i, 0)),
                  pl.BlockSpec((ts, 512), lambda i: (i, 0)), _full((D_MIX, d))],
        out_specs=pl.BlockSpec((ts, d), lambda i: (i, 0)),
        out_shape=jax.ShapeDtypeStruct((s_len, d), F32),
        compiler_params=_params(1),
    )(x, y_pc, y_at, w_out)


def _loss_bwd(x, target, g):
    s_len, d = x.shape
    ts = _row_tile(s_len, ROW_TILE)

    def body(x_ref, t_ref, g_ref, sq_ref, dg_ref, dx_ref):
        i = pl.program_id(0)
        xv = x_ref[...]
        gv = g_ref[...]
        r = lax.rsqrt(jnp.mean(xv * xv, axis=-1, keepdims=True) + EPS)
        xr = xv * r
        err = xr * gv - t_ref[...]
        dout = err * (1.0 / d)
        w = dout * gv
        dx_ref[...] = r * (w - xr * jnp.mean(w * xr, axis=-1, keepdims=True))

        @pl.when(i == 0)
        def _():
            sq_ref[...] = jnp.zeros_like(sq_ref)
            dg_ref[...] = jnp.zeros_like(dg_ref)

        sq = jnp.sum(jnp.sum(err * err, axis=-1, keepdims=True), axis=0, keepdims=True)
        sq_ref[...] += jnp.broadcast_to(sq, sq_ref.shape)
        dg_ref[...] += jnp.sum(dout * xr, axis=0, keepdims=True)

    return pl.pallas_call(
        body, name="loss_bwd", grid=(s_len // ts,),
        in_specs=[pl.BlockSpec((ts, d), lambda i: (i, 0)), pl.BlockSpec((ts, d), lambda i: (i, 0)), _full((1, d))],
        out_specs=[_full((1, 128)), _full((1, d)), pl.BlockSpec((ts, d), lambda i: (i, 0))],
        out_shape=[jax.ShapeDtypeStruct((1, 128), F32), jax.ShapeDtypeStruct((1, d), F32),
                   jax.ShapeDtypeStruct((s_len, d), F32)],
        compiler_params=_params(1),
    )(x, target, g)


def _out_bwd(dxo, y_pc, y_at, w_out):
    s_len, d = dxo.shape
    ts = _row_tile(s_len, ROW_TILE)

    def body(dx_ref, a_ref, b_ref, w_ref, dy_ref, gw_ref):
        i = pl.program_id(0)
        dxv = dx_ref[...].astype(MXU_DTYPE)
        dy_ref[...] = _mm(dxv, w_ref[...], _NT)

        @pl.when(i == 0)
        def _():
            gw_ref[...] = jnp.zeros_like(gw_ref)

        y = jnp.concatenate([a_ref[...], b_ref[...]], axis=-1)
        gw_ref[...] += _mm(y, dxv, _TN)

    return pl.pallas_call(
        body, name="out_bwd", grid=(s_len // ts,),
        in_specs=[pl.BlockSpec((ts, d), lambda i: (i, 0)), pl.BlockSpec((ts, 512), lambda i: (i, 0)),
                  pl.BlockSpec((ts, 512), lambda i: (i, 0)), _full((D_MIX, d))],
        out_specs=[pl.BlockSpec((ts, D_MIX), lambda i: (i, 0)), _full((D_MIX, d))],
        out_shape=[jax.ShapeDtypeStruct((s_len, D_MIX), F32), jax.ShapeDtypeStruct((D_MIX, d), F32)],
        compiler_params=_params(1),
    )(dxo, y_pc, y_at, w_out)


def _poolconv_bwd(proj, dy, wp, scale, dw, cb, lng, lnb, pw):
    s_len = proj.shape[0]
    ts = _row_tile(s_len, ROW_TILE)
    hb = ts // HALO
    nt = s_len // ts
    last_halo = s_len // HALO - 1
    n = ts + HALO

    def body(p_ref, ph_ref, pn_ref, dy_ref, dyn_ref, wp_ref, sc_ref, dw_ref, cb_ref, lng_ref, lnb_ref, pw_ref,
             da_ref, gwp_ref, gpw_ref, gdw_ref, gvec_ref):
        i = pl.program_id(0)
        cur = p_ref[...]
        prev = jnp.where(i > 0, ph_ref[...], 0.0)
        nxt = jnp.where(i < nt - 1, pn_ref[...], 0.0)
        dyx = jnp.concatenate([dy_ref[...], jnp.where(i < nt - 1, dyn_ref[...], 0.0)], axis=0)
        row = lax.broadcasted_iota(jnp.int32, (n, 1), 0)
        in_seq = (row < ts) | (i < nt - 1)
        scale_v = sc_ref[...]

        a3 = jnp.concatenate([prev[:, 512:768], cur[:, 512:768], nxt[:, 512:768]], axis=0)
        sb3 = _sig(jnp.concatenate([prev[:, 768:1024], cur[:, 768:1024], nxt[:, 768:1024]], axis=0))
        hh_rolled = _shifted_down(a3 * sb3)
        cv = _causal_conv(hh_rolled, dw_ref, HALO, n) + cb_ref[...]
        nrm, rstd = _layer_norm(cv)
        z = nrm * lng_ref[...] + lnb_ref[...]
        sz = _sig(z)
        sw = z * sz
        gc = jnp.concatenate([cur[:, 1024:1280], nxt[:, 1024:1280]], axis=0)
        sgc = _sig(gc)
        yc = _mm(sw, pw_ref[...])
        dyc = dyx[:, 256:512]
        d_yc = dyc * (gc * sgc)
        d_gc = (dyc * yc * _dsilu(gc, sgc))[:ts]
        d_z = _mm(d_yc, pw_ref[...], _NT) * _dsilu(z, sz)
        d_n = d_z * lng_ref[...]
        d_cv = rstd * (d_n - jnp.mean(d_n, axis=-1, keepdims=True)
                       - nrm * jnp.mean(d_n * nrm, axis=-1, keepdims=True))
        d_cv = jnp.where(in_seq, d_cv, 0.0)
        d_hh = _anticausal_conv(d_cv, dw_ref, ts)
        a_c, sb_c = a3[HALO:HALO + ts], sb3[HALO:HALO + ts]
        d_a = d_hh * sb_c
        d_b = d_hh * a_c * sb_c * (1.0 - sb_c)
        d_cv_t = d_cv[:ts]
        g_dw_rows = [None] * CONV_TAPS_PAD
        for r in range(8):
            for m in range(4):
                d = 8 * m + r
                if d < CONV_KERNEL:
                    g_dw_rows[CONV_KERNEL - 1 - d] = jnp.sum(
                        d_cv_t * hh_rolled[r][HALO - 8 * m: HALO - 8 * m + ts], axis=0, keepdims=True)
        g_dw_rows[CONV_KERNEL] = jnp.zeros((1, CONV_WIDTH), F32)

        u_ext = jnp.concatenate([prev[:, 0:256], cur[:, 0:256]], axis=0)
        diff = _pool_diff(u_ext, i * ts, ts)
        raw = _mm(diff, wp_ref[...])
        gp = jnp.concatenate([cur[:, 256:512], nxt[:, 256:512]], axis=0)
        sgp = _sig(gp)
        dyp = dyx[:, 0:256]
        d_yp = dyp * (gp * sgp)
        d_gp = dyp[:ts] * (raw * scale_v) * _dsilu(gp, sgp)[:ts]
        d_raw = d_yp * scale_v
        d_diff = _mm(d_raw, wp_ref[...], _NT)
        w = jnp.where(in_seq, d_diff / _pool_count(i * ts, n), 0.0)
        d_u = _pool_diff_bwd(w, ts) - d_diff[:ts]

        da_ref[...] = jnp.concatenate([d_u, d_gp, d_a, d_b, d_gc], axis=-1).astype(da_ref.dtype)

        @pl.when(i == 0)
        def _():
            gwp_ref[...] = jnp.zeros_like(gwp_ref)
            gpw_ref[...] = jnp.zeros_like(gpw_ref)
            gdw_ref[...] = jnp.zeros_like(gdw_ref)
            gvec_ref[...] = jnp.zeros_like(gvec_ref)

        gwp_ref[...] += _mm(diff, d_raw[:ts], _TN)
        gpw_ref[...] += _mm(sw[:ts], d_yc[:ts], _TN)
        gdw_ref[...] += jnp.concatenate(g_dw_rows, axis=0)
        zero_row = jnp.zeros((1, 256), F32)
        gvec_ref[...] += jnp.concatenate([
            jnp.sum(d_yp[:ts] * raw, axis=0, keepdims=True),
            jnp.sum(d_cv_t, axis=0, keepdims=True),
            jnp.sum((d_z * nrm)[:ts], axis=0, keepdims=True),
            jnp.sum(d_z[:ts], axis=0, keepdims=True),
            zero_row, zero_row, zero_row, zero_row], axis=0)

    vec = _full((1, 256))
    return pl.pallas_call(
        body, name="poolconv_bwd", grid=(nt,),
        in_specs=[pl.BlockSpec((ts, HALF_IN), lambda i: (i, 0)),
                  pl.BlockSpec((HALO, HALF_IN), lambda i: (jnp.maximum(i * hb - 1, 0), 0)),
                  pl.BlockSpec((HALO, HALF_IN), lambda i: (jnp.minimum((i + 1) * hb, last_halo), 0)),
                  pl.BlockSpec((ts, 512), lambda i: (i, 0)),
                  pl.BlockSpec((HALO, 512), lambda i: (jnp.minimum((i + 1) * hb, last_halo), 0)),
                  _full((256, 256)), vec, _full((CONV_TAPS_PAD, 256)), vec, vec, vec, _full((256, 256))],
        out_specs=[pl.BlockSpec((ts, HALF_IN), lambda i: (i, 0)), _full((256, 256)), _full((256, 256)),
                   _full((CONV_TAPS_PAD, 256)), _full((8, 256))],
        out_shape=[jax.ShapeDtypeStruct((s_len, HALF_IN), MXU_DTYPE), jax.ShapeDtypeStruct((256, 256), F32),
                   jax.ShapeDtypeStruct((256, 256), F32), jax.ShapeDtypeStruct((CONV_TAPS_PAD, 256), F32),
                   jax.ShapeDtypeStruct((8, 256), F32)],
        compiler_params=_params(1),
    )(proj, proj, proj, dy, dy, wp, scale, dw, cb, lng, lnb, pw)


DQ0, DKC0, DVC0, DKP0, DVP0, DGA0, DATTN_W = 0, 512, 640, 768, 896, 1024, 1536


def _attn_bwd(proj, dy, sinks):
    s_len = proj.shape[0]

    def body(p_ref, kvp_ref, dy_ref, sinks_ref, o_ref, gs_ref):
        i = pl.program_id(0)
        pc = p_ref[...]
        kvp = kvp_ref[...]
        dya = dy_ref[...]
        q, k, v, ga = pc[:, 0:512], pc[:, 512:640], pc[:, 640:768], pc[:, 768:1280]
        dq_parts, dga_parts, dk_parts, dv_parts, ds_rows = [], [], [], [], []
        for kh in range(N_KV_HEADS):
            lo, hi = HEAD_DIM * kh, HEAD_DIM * (kh + 1)
            k2 = jnp.concatenate([kvp[:, lo:hi], k[:, lo:hi]], axis=0).astype(MXU_DTYPE)
            v2 = jnp.concatenate([kvp[:, 128 + lo:128 + hi], v[:, lo:hi]], axis=0).astype(MXU_DTYPE)
            q4 = _stack_heads(q, kh).astype(MXU_DTYPE)
            p, p_sink = _attn_probs(q4, k2, kh, i == 0, sinks_ref)
            o4 = _mm(p, v2)
            ga4 = _stack_heads(ga, kh)
            sga = _sig(ga4)
            dyo4 = _stack_heads(dya, kh)
            do4 = dyo4 * (ga4 * sga)
            dga_parts.append(dyo4 * o4 * _dsilu(ga4, sga))
            dp = _mm(do4, v2, _NT)
            delta = jnp.sum(p * dp, axis=-1, keepdims=True)
            ds = p * (dp - delta)
            dsink = -p_sink * delta
            for g in range(Q_PER_KV):
                ds_rows.append(jnp.sum(dsink[BLOCK * g: BLOCK * (g + 1)], axis=0, keepdims=True))
            dq_parts.append(_mm(ds, k2) * SCALE)
            dk_parts.append(_mm(ds, q4, _TN) * SCALE)
            dv_parts.append(_mm(p, do4, _TN))
        dk = jnp.concatenate(dk_parts, axis=-1)
        dv = jnp.concatenate(dv_parts, axis=-1)
        o_ref[...] = jnp.concatenate([_unstack_heads(dq_parts), dk[BLOCK:], dv[BLOCK:], dk[:BLOCK], dv[:BLOCK],
                                      _unstack_heads(dga_parts)], axis=-1)

        @pl.when(i == 0)
        def _():
            gs_ref[...] = jnp.zeros_like(gs_ref)

        gs_ref[...] += jnp.broadcast_to(jnp.concatenate(ds_rows, axis=0), gs_ref.shape)

    return pl.pallas_call(
        body, name="attn_bwd", grid=(s_len // BLOCK,),
        in_specs=[pl.BlockSpec((BLOCK, HALF_IN), lambda i: (i, 1)),
                  pl.BlockSpec((BLOCK, 256), lambda i: (jnp.maximum(i - 1, 0), 7)),
                  pl.BlockSpec((BLOCK, 512), lambda i: (i, 1)),
                  pl.BlockSpec(memory_space=pltpu.SMEM)],
        out_specs=[pl.BlockSpec((BLOCK, DATTN_W), lambda i: (i, 0)), _full((N_Q_HEADS, 128))],
        out_shape=[jax.ShapeDtypeStruct((s_len, DATTN_W), F32), jax.ShapeDtypeStruct((N_Q_HEADS, 128), F32)],
        compiler_params=_params(1),
    )(proj, proj, dy, sinks)


def _in_bwd(da, dattn, x, dxo, g, w_t):
    s_len, d = x.shape
    ts = _row_tile(s_len, IN_BWD_TILE)
    bpt = ts // BLOCK
    nt = s_len // ts
    last_block = s_len // BLOCK - 1

    def body(da_ref, dat_ref, nxt_ref, x_ref, dxo_ref, g_ref, w_ref, dx_ref, dg_ref, gw_ref):
        i = pl.program_id(0)
        dat = dat_ref[...]
        nxt = jnp.where(i < nt - 1, nxt_ref[...], 0.0)
        shifted = jnp.concatenate([dat[BLOCK:, DKP0:DGA0], nxt], axis=0) if bpt > 1 else nxt
        dkv = dat[:, DKC0:DKP0] + shifted
        dproj = jnp.concatenate([da_ref[...], dat[:, DQ0:DKC0].astype(MXU_DTYPE), dkv.astype(MXU_DTYPE),
                                 dat[:, DGA0:DATTN_W].astype(MXU_DTYPE)], axis=-1)
        d_h = _mm(dproj, w_ref[...])
        xv = x_ref[...]
        gv = g_ref[...]
        r = lax.rsqrt(jnp.mean(xv * xv, axis=-1, keepdims=True) + EPS)
        xr = xv * r
        w = d_h * gv
        dx_ref[...] = dxo_ref[...] + r * (w - xr * jnp.mean(w * xr, axis=-1, keepdims=True))

        @pl.when(i == 0)
        def _():
            dg_ref[...] = jnp.zeros_like(dg_ref)
            gw_ref[...] = jnp.zeros_like(gw_ref)

        dg_ref[...] += jnp.sum(d_h * xr, axis=0, keepdims=True)
        gw_ref[...] += _mm(dproj, xr * gv, _TN)

    return pl.pallas_call(
        body, name="in_bwd", grid=(nt,),
        in_specs=[pl.BlockSpec((ts, HALF_IN), lambda i: (i, 0)),
                  pl.BlockSpec((ts, DATTN_W), lambda i: (i, 0)),
                  pl.BlockSpec((BLOCK, 256), lambda i: (jnp.minimum((i + 1) * bpt, last_block), 3)),
                  pl.BlockSpec((ts, d), lambda i: (i, 0)), pl.BlockSpec((ts, d), lambda i: (i, 0)),
                  _full((1, d)), _full((D_IN, d))],
        out_specs=[pl.BlockSpec((ts, d), lambda i: (i, 0)), _full((1, d)), _full((D_IN, d))],
        out_shape=[jax.ShapeDtypeStruct((s_len, d), F32), jax.ShapeDtypeStruct((1, d), F32),
                   jax.ShapeDtypeStruct((D_IN, d), F32)],
        compiler_params=_params(1),
    )(da, dattn, dattn, x, dxo, g, w_t)


def _mesh_pos():
    return lax.axis_index("x"), lax.axis_index("y"), lax.axis_index("c")


def _all_gather(shard, name):
    m_per, n = shard.shape

    def body(x_ref, out_ref, send_sems, recv_sems, local_sem):
        x, y, c = _mesh_pos()
        me, sibling = (x, y, c), (x, y, 1 - c)
        chips = [(1 - x, y), (x, 1 - y), (1 - x, 1 - y)]

        def rows(px, py, pc):
            return out_ref.at[pl.ds((4 * px + 2 * py + pc) * m_per, m_per), :]

        def copy(k, block, to, src=None):
            return pltpu.make_async_remote_copy(
                src_ref=rows(*block) if src is None else src, dst_ref=rows(*block),
                send_sem=send_sems.at[k], recv_sem=recv_sems.at[k],
                device_id=to, device_id_type=pl.DeviceIdType.MESH)

        mine = pltpu.make_async_copy(x_ref, rows(*me), local_sem)
        mine.start()
        first = [copy(0, me, sibling, src=x_ref)]
        first += [copy(1 + j, me, (*chip, c), src=x_ref) for j, chip in enumerate(chips)]
        for cp in first:
            cp.start()
        passed = [copy(4 + j, (*chip, c), sibling) for j, chip in enumerate(chips)]
        for j, chip in enumerate(chips):
            copy(1 + j, (*chip, c), me).wait_recv()
            passed[j].start()
        copy(0, sibling, me).wait_recv()
        for j, chip in enumerate(chips):
            copy(4 + j, (*chip, 1 - c), me).wait_recv()
        for cp in first + passed:
            cp.wait_send()
        mine.wait()

    return pl.pallas_call(
        body, name=name,
        out_shape=jax.ShapeDtypeStruct((N_DEV * m_per, n), shard.dtype),
        in_specs=[pl.BlockSpec(memory_space=pltpu.VMEM)],
        out_specs=pl.BlockSpec(memory_space=pltpu.VMEM),
        scratch_shapes=[pltpu.SemaphoreType.DMA((7,)), pltpu.SemaphoreType.DMA((7,)), pltpu.SemaphoreType.DMA],
        compiler_params=pltpu.CompilerParams(vmem_limit_bytes=VMEM_LIMIT),
    )(shard)


def _all_to_all(send):
    def body(send_ref, recv_ref, send_sems, recv_sems, local_sem):
        x, y, c = _mesh_pos()
        me = 4 * x + 2 * y + c
        mine = pltpu.make_async_copy(send_ref.at[me], recv_ref.at[me], local_sem)
        mine.start()
        copies = []
        for k in range(1, N_DEV):
            tx = x ^ ((k >> 2) & 1)
            ty = y ^ ((k >> 1) & 1)
            tc = c ^ (k & 1)
            cp = pltpu.make_async_remote_copy(
                src_ref=send_ref.at[4 * tx + 2 * ty + tc], dst_ref=recv_ref.at[me],
                send_sem=send_sems.at[k - 1], recv_sem=recv_sems.at[k - 1],
                device_id=(tx, ty, tc), device_id_type=pl.DeviceIdType.MESH)
            cp.start()
            copies.append(cp)
        for cp in copies:
            cp.wait()
        mine.wait()

    return pl.pallas_call(
        body, name="grad_all_to_all",
        out_shape=jax.ShapeDtypeStruct(send.shape, send.dtype),
        in_specs=[pl.BlockSpec(memory_space=pl.ANY)],
        out_specs=pl.BlockSpec(memory_space=pl.ANY),
        scratch_shapes=[pltpu.SemaphoreType.DMA((N_DEV - 1,)), pltpu.SemaphoreType.DMA((N_DEV - 1,)),
                        pltpu.SemaphoreType.DMA],
    )(send)


def _adamw(parts, w, m, v, name):
    rows, n = w.shape
    tr = _row_tile(rows, 256)

    def body(p_ref, w_ref, m_ref, v_ref, g_ref, d_ref, nm_ref, nv_ref):
        g = p_ref[0]
        for k in range(1, N_DEV):
            g = g + p_ref[k]
        nm = ADAM_B1 * m_ref[...] + (1.0 - ADAM_B1) * g
        nv = ADAM_B2 * v_ref[...] + (1.0 - ADAM_B2) * (g * g)
        m_hat = nm / (1.0 - ADAM_B1 ** ADAM_STEP)
        v_hat = nv / (1.0 - ADAM_B2 ** ADAM_STEP)
        g_ref[...] = g
        nm_ref[...] = nm
        nv_ref[...] = nv
        d_ref[...] = -ADAM_LR * (m_hat / (jnp.sqrt(v_hat) + ADAM_EPS) + ADAM_WD * w_ref[...])

    tile = pl.BlockSpec((tr, n), lambda i: (i, 0))
    shape = jax.ShapeDtypeStruct((rows, n), F32)
    return pl.pallas_call(
        body, name=name, grid=(rows // tr,),
        in_specs=[pl.BlockSpec((N_DEV, tr, n), lambda i: (0, i, 0)), tile, tile, tile],
        out_specs=[tile, tile, tile, tile],
        out_shape=[shape, shape, shape, shape],
        compiler_params=_params(1),
    )(parts, w, m, v)


def _pad_rows(a, mult):
    pad = (-a.shape[0]) % mult
    return a if pad == 0 else jnp.concatenate([a, jnp.zeros((pad, a.shape[1]), a.dtype)], axis=0)


def _pack_sharded(w_in, w_out, conv_pw, conv_dw, d):
    dw_t = jnp.pad(jnp.swapaxes(conv_dw, 1, 2), ((0, 0), (0, 0), (0, CONV_TAPS_PAD - CONV_KERNEL)))
    return jnp.concatenate([jnp.swapaxes(w_in, 1, 2).reshape(-1, d), w_out.reshape(-1, d),
                            conv_pw.reshape(-1, d), dw_t.reshape(-1, d)], axis=0)


def _sharded_rows(d):
    n_in, n_out = DEPTH * D_IN // N_DEV, DEPTH * D_MIX // N_DEV
    n_pw = DEPTH * (CONV_WIDTH // N_DEV) * CONV_WIDTH // d
    n_dw = DEPTH * (CONV_WIDTH // N_DEV) * CONV_TAPS_PAD // d
    return n_in, n_out, n_pw, n_dw


def _unpack_sharded(rows, d):
    n_in, n_out, n_pw, n_dw = _sharded_rows(d)
    o1, o2, o3 = n_in, n_in + n_out, n_in + n_out + n_pw
    w_in = jnp.swapaxes(rows[:o1].reshape(DEPTH, D_IN // N_DEV, d), 1, 2)
    w_out = rows[o1:o2].reshape(DEPTH, D_MIX // N_DEV, d)
    pw = rows[o2:o3].reshape(DEPTH, CONV_WIDTH // N_DEV, CONV_WIDTH)
    dw = jnp.swapaxes(rows[o3:o3 + n_dw].reshape(DEPTH, CONV_WIDTH // N_DEV, CONV_TAPS_PAD), 1, 2)[:, :CONV_KERNEL]
    return w_in, w_out, pw, dw


def _pack_replicated(ln_g, pool_w, pool_scale, conv_b, conv_ln_g, conv_ln_b, attn_sinks, final_g, d):
    sinks = jnp.pad(attn_sinks, ((0, 0), (0, 256 - N_Q_HEADS)))
    small = jnp.concatenate([pool_scale, conv_b, conv_ln_g, conv_ln_b, sinks], axis=0)
    small = _pad_rows(small, d // 256)
    return jnp.concatenate([ln_g.reshape(-1, d), final_g.reshape(-1, d), pool_w.reshape(-1, d),
                            small.reshape(-1, d)], axis=0)


def _unpack_replicated(rows, d):
    n_pool = DEPTH * 4 * POOL_GROUP * POOL_GROUP // d
    ln_g = rows[:DEPTH]
    final_g = rows[DEPTH]
    pool_w = rows[DEPTH + 1: DEPTH + 1 + n_pool].reshape(DEPTH, 4, POOL_GROUP, POOL_GROUP)
    small = rows[DEPTH + 1 + n_pool:].reshape(-1, 256)[: 5 * DEPTH]
    pool_scale, conv_b, conv_ln_g, conv_ln_b = (small[DEPTH * k: DEPTH * (k + 1)] for k in range(4))
    sinks = small[4 * DEPTH: 5 * DEPTH, :N_Q_HEADS]
    return ln_g, pool_w, pool_scale, conv_b, conv_ln_g, conv_ln_b, sinks, final_g


def _gather_weights(w_in, w_out, conv_pw, conv_dw, d):
    n_in, n_out, n_pw, n_dw = _sharded_rows(d)
    dw_t = jnp.pad(jnp.swapaxes(conv_dw, 1, 2), ((0, 0), (0, 0), (0, CONV_TAPS_PAD - CONV_KERNEL)))
    per_word = 4 // jnp.dtype(MXU_DTYPE).itemsize
    dw_bits = (lax.bitcast_convert_type(dw_t, MXU_DTYPE) if per_word > 1 else dw_t).reshape(-1, d)
    shard = jnp.concatenate([jnp.swapaxes(w_in, 1, 2).reshape(-1, d).astype(MXU_DTYPE),
                             w_out.reshape(-1, d).astype(MXU_DTYPE),
                             conv_pw.reshape(-1, d).astype(MXU_DTYPE), dw_bits], axis=0)
    used = shard.shape[0]
    shard = _pad_rows(shard, 16)
    full = _all_gather(shard, "weight_all_gather").reshape(N_DEV, shard.shape[0], d)
    o1, o2, o3 = n_in, n_in + n_out, n_in + n_out + n_pw
    per_in, per_out, per_pw = n_in // DEPTH, n_out // DEPTH, n_pw // DEPTH
    w_in_t = [full[:, per_in * l: per_in * (l + 1)].reshape(D_IN, d) for l in range(DEPTH)]
    w_out_f = [full[:, o1 + per_out * l: o1 + per_out * (l + 1)].reshape(D_MIX, d) for l in range(DEPTH)]
    pw_f = [full[:, o2 + per_pw * l: o2 + per_pw * (l + 1)].reshape(CONV_WIDTH, CONV_WIDTH) for l in range(DEPTH)]
    bits = full[:, o3:used].reshape((N_DEV, DEPTH, CONV_WIDTH // N_DEV, CONV_TAPS_PAD) + (per_word,) * (per_word > 1))
    dw_all = lax.bitcast_convert_type(bits, F32) if per_word > 1 else bits
    dw_f = [jnp.swapaxes(dw_all[:, l].reshape(CONV_WIDTH, CONV_TAPS_PAD), 0, 1) for l in range(DEPTH)]
    return w_in_t, w_out_f, pw_f, dw_f


def _block_diag(pool_w):
    out = jnp.zeros((POOL_WIDTH, POOL_WIDTH), pool_w.dtype)
    for gi in range(4):
        out = out.at[POOL_GROUP * gi: POOL_GROUP * (gi + 1), POOL_GROUP * gi: POOL_GROUP * (gi + 1)].set(pool_w[gi])
    return out


def _diag_blocks(mat):
    return jnp.stack([mat[POOL_GROUP * gi: POOL_GROUP * (gi + 1), POOL_GROUP * gi: POOL_GROUP * (gi + 1)]
                      for gi in range(4)], axis=0)


def _forward_backward(x0, target, ln_g, w_in_t, pool_w, pool_scale, dw_f, conv_b, conv_ln_g, conv_ln_b, pw_f,
                      attn_sinks, w_out_f, final_g):
    wp_bd = [_block_diag(pool_w[l]).astype(MXU_DTYPE) for l in range(DEPTH)]
    row = lambda a: a.reshape(1, -1)

    xs, projs, ys = [x0], [], []
    for l in range(DEPTH):
        proj = _in_proj(xs[l], row(ln_g[l]), w_in_t[l])
        y_pc = _poolconv_fwd(proj, wp_bd[l], row(pool_scale[l]), dw_f[l], row(conv_b[l]), row(conv_ln_g[l]),
                             row(conv_ln_b[l]), pw_f[l])
        y_at = _attn_fwd(proj, attn_sinks[l])
        xs.append(_out_proj(xs[l], y_pc, y_at, w_out_f[l]))
        projs.append(proj)
        ys.append((y_pc, y_at))

    sq, g_final, dx = _loss_bwd(xs[DEPTH], target, row(final_g))

    g_ln, g_win_t, g_wout, g_pw, g_dw, g_pool_w, g_vec, g_sinks = ([None] * DEPTH for _ in range(8))
    for l in reversed(range(DEPTH)):
        dy, g_wout[l] = _out_bwd(dx, ys[l][0], ys[l][1], w_out_f[l])
        da, g_wp_bd, g_pw[l], g_dw[l], g_vec[l] = _poolconv_bwd(
            projs[l], dy, wp_bd[l], row(pool_scale[l]), dw_f[l], row(conv_b[l]), row(conv_ln_g[l]),
            row(conv_ln_b[l]), pw_f[l])
        dattn, gs = _attn_bwd(projs[l], dy, attn_sinks[l])
        dx, g_ln[l], g_win_t[l] = _in_bwd(da, dattn, xs[l], dx, row(ln_g[l]), w_in_t[l])
        g_pool_w[l] = _diag_blocks(g_wp_bd)
        g_sinks[l] = gs[:, 0]
    return sq, dx, g_final, g_ln, g_win_t, g_wout, g_pw, g_dw, g_pool_w, g_vec, g_sinks


def kernel(x, ln_g, w_in, pool_w, pool_scale, conv_dw, conv_b, conv_ln_g, conv_ln_b, conv_pw, attn_sinks, w_out, final_g, loss_target, m_ln_g, m_w_in, m_pool_w, m_pool_scale, m_conv_dw, m_conv_b, m_conv_ln_g, m_conv_ln_b, m_conv_pw, m_attn_sinks, m_w_out, m_final_g, v_ln_g, v_w_in, v_pool_w, v_pool_scale, v_conv_dw, v_conv_b, v_conv_ln_g, v_conv_ln_b, v_conv_pw, v_attn_sinks, v_w_out, v_final_g):
    x0 = x[0]
    d = x0.shape[1]

    w_in_t, w_out_f, pw_f, dw_f = _gather_weights(w_in, w_out, conv_pw, conv_dw, d)
    (sq, dx, g_final, g_ln, g_win_t, g_wout, g_pw, g_dw, g_pool_w, g_vec, g_sinks) = _forward_backward(
        x0, loss_target[0], ln_g, w_in_t, pool_w, pool_scale, dw_f, conv_b, conv_ln_g, conv_ln_b, pw_f, attn_sinks,
        w_out_f, final_g)
    loss = lax.psum(0.5 / d * sq[0, 0], ("x", "y", "c"))
    grad_x = dx[None]

    n_in, n_out, n_pw, n_dw = _sharded_rows(d)
    per = lambda a, rows: a.reshape(N_DEV, rows, d)
    send = jnp.concatenate(
        [per(g_win_t[l], D_IN // N_DEV) for l in range(DEPTH)]
        + [per(g_wout[l], D_MIX // N_DEV) for l in range(DEPTH)]
        + [per(g_pw[l], n_pw // DEPTH) for l in range(DEPTH)]
        + [per(jnp.swapaxes(g_dw[l], 0, 1), n_dw // DEPTH) for l in range(DEPTH)], axis=1)
    used = send.shape[1]
    send = jnp.pad(send, ((0, 0), (0, (-used) % 8), (0, 0)))
    recv = _all_to_all(send)
    w_rows = _pad_rows(_pack_sharded(w_in, w_out, conv_pw, conv_dw, d), 8)
    m_rows = _pad_rows(_pack_sharded(m_w_in, m_w_out, m_conv_pw, m_conv_dw, d), 8)
    v_rows = _pad_rows(_pack_sharded(v_w_in, v_w_out, v_conv_pw, v_conv_dw, d), 8)
    sharded = [_unpack_sharded(o, d) for o in _adamw(recv, w_rows, m_rows, v_rows, "adamw_sharded")]

    gv = jnp.stack(g_vec, axis=0)
    part = _pack_replicated(jnp.concatenate(g_ln, axis=0), jnp.stack(g_pool_w, axis=0), gv[:, 0], gv[:, 1],
                            gv[:, 2], gv[:, 3], jnp.stack(g_sinks, axis=0), g_final, d)
    used_r = part.shape[0]
    part = _pad_rows(part, 8)
    parts = _all_gather(part, "replicated_grad_all_gather").reshape(N_DEV, part.shape[0], d)
    pack_r = lambda *a: _pad_rows(_pack_replicated(*a, d), 8)
    w_r = pack_r(ln_g, pool_w, pool_scale, conv_b, conv_ln_g, conv_ln_b, attn_sinks, final_g)
    m_r = pack_r(m_ln_g, m_pool_w, m_pool_scale, m_conv_b, m_conv_ln_g, m_conv_ln_b, m_attn_sinks, m_final_g)
    v_r = pack_r(v_ln_g, v_pool_w, v_pool_scale, v_conv_b, v_conv_ln_g, v_conv_ln_b, v_attn_sinks, v_final_g)
    replicated = [_unpack_replicated(o[:used_r], d) for o in _adamw(parts, w_r, m_r, v_r, "adamw_replicated")]

    outs = []
    for (s_win, s_wout, s_pw, s_dw), (r_ln, r_pool, r_scale, r_cb, r_lng, r_lnb, r_sinks, r_final) in zip(
            sharded, replicated):
        outs.append([r_ln, s_win, r_pool, r_scale, s_dw, r_cb, r_lng, r_lnb, s_pw, r_sinks, s_wout, r_final])
    return (loss, grad_x, *outs[0], *outs[1], *outs[2], *outs[3])
```

```python
import jax
import jax.numpy as jnp
from jax import lax
from jax.experimental import pallas as pl
from jax.experimental.pallas import tpu as pltpu

F32 = jnp.float32
MXU_DTYPE = jnp.bfloat16

N_DEV = 8
DEPTH = 2
POOL_WIDTH = 256
POOL_GROUP = 64
CONV_WIDTH = 256
CONV_KERNEL = 31
CONV_TAPS_PAD = 32
HEAD_DIM = 64
N_KV_HEADS = 2
Q_PER_KV = 4
N_Q_HEADS = 8
ATTN_WIDTH = 512
BLOCK = 128
D_MIX = 1024
D_IN = 2560
HALF_IN = 1280
EPS = 1e-6
SCALE = HEAD_DIM ** -0.5
NEG = -1e30

ADAM_LR = 0.001
ADAM_B1 = 0.9
ADAM_B2 = 0.999
ADAM_EPS = 1e-08
ADAM_WD = 0.01
ADAM_STEP = 10

HALO = 32
ROW_TILE = 512
IN_BWD_TILE = 256
VMEM_LIMIT = 56 * 1024 * 1024

_NN = (((1,), (0,)), ((), ()))
_NT = (((1,), (1,)), ((), ()))
_TN = (((0,), (0,)), ((), ()))
_ANY = pl.BlockSpec(memory_space=pl.ANY)


def _mm(a, b, dims=_NN):
    return lax.dot_general(a.astype(MXU_DTYPE), b.astype(MXU_DTYPE), dims, preferred_element_type=F32)


def _sig(x):
    return 1.0 / (1.0 + jnp.exp(-x))


def _dsilu(z, s):
    return s * (1.0 + z * (1.0 - s))


def _params(n_grid):
    return pltpu.CompilerParams(dimension_semantics=("arbitrary",) * n_grid, vmem_limit_bytes=VMEM_LIMIT)


def _row_tile(rows, cap):
    t = min(rows, cap)
    while rows % t or t % 8:
        t -= 8
    return t


def _full(shape):
    return pl.BlockSpec(shape, lambda i: (0,) * len(shape))


def _mesh_pos():
    return lax.axis_index("x"), lax.axis_index("y"), lax.axis_index("c")


class _Exchange:
    def __init__(self, sources):
        self.sources = list(sources)
        self.n = len(self.sources)
        self.gather = [s.ndim == 2 for s in self.sources]

    def out_shapes(self):
        return [jax.ShapeDtypeStruct((N_DEV,) + s.shape[-2:], s.dtype) for s in self.sources]

    def scratch(self):
        if not self.n:
            return []
        return [pltpu.SemaphoreType.DMA((7 * self.n,)), pltpu.SemaphoreType.DMA((7 * self.n,)),
                pltpu.SemaphoreType.DMA((self.n,))]

    def copies(self, src_refs, dst_refs, sems):
        send_sems, recv_sems, local_sems = sems
        x, y, c = _mesh_pos()
        me = 4 * x + 2 * y + c
        out = []
        for a, (src, dst) in enumerate(zip(src_refs, dst_refs)):
            out.append(pltpu.make_async_copy(src if self.gather[a] else src.at[me], dst.at[me], local_sems.at[a]))
            for k in range(1, N_DEV):
                tx, ty, tc = x ^ ((k >> 2) & 1), y ^ ((k >> 1) & 1), c ^ (k & 1)
                out.append(pltpu.make_async_remote_copy(
                    src_ref=src if self.gather[a] else src.at[4 * tx + 2 * ty + tc], dst_ref=dst.at[me],
                    send_sem=send_sems.at[7 * a + k - 1], recv_sem=recv_sems.at[7 * a + k - 1],
                    device_id=(tx, ty, tc), device_id_type=pl.DeviceIdType.MESH))
        return out

    def run(self, refs, first, last):
        if not self.n:
            return
        src_refs, dst_refs, sems = refs

        @pl.when(first)
        def _():
            for cp in self.copies(src_refs, dst_refs, sems):
                cp.start()

        @pl.when(last)
        def _():
            for cp in self.copies(src_refs, dst_refs, sems):
                cp.wait()


def _split_refs(refs, n_in, n_out, exch):
    ins = refs[:n_in]
    srcs = refs[n_in:n_in + exch.n]
    outs = refs[n_in + exch.n:n_in + exch.n + n_out]
    dsts = refs[n_in + exch.n + n_out:n_in + 2 * exch.n + n_out]
    rest = refs[n_in + 2 * exch.n + n_out:]
    sems = rest[len(rest) - 3:] if exch.n else ()
    scratch = rest[:len(rest) - 3] if exch.n else rest
    return ins, outs, scratch, (srcs, dsts, sems)


def _final_exchange(sources):
    exch = _Exchange(sources)

    def body(*refs):
        _, _, _, xrefs = _split_refs(refs, 0, 0, exch)
        for cp in exch.copies(*xrefs):
            cp.start()
        for cp in exch.copies(*xrefs):
            cp.wait()

    return pl.pallas_call(
        body, name="final_exchange", out_shape=exch.out_shapes(),
        in_specs=[_ANY] * exch.n, out_specs=[_ANY] * exch.n, scratch_shapes=exch.scratch(),
    )(*exch.sources)


def _all_gather(shard, name):
    m_per, n = shard.shape

    def body(x_ref, out_ref, send_sems, recv_sems, local_sem):
        x, y, c = _mesh_pos()
        me, sibling = (x, y, c), (x, y, 1 - c)
        chips = [(1 - x, y), (x, 1 - y), (1 - x, 1 - y)]

        def rows(px, py, pc):
            return out_ref.at[pl.ds((4 * px + 2 * py + pc) * m_per, m_per), :]

        def copy(k, block, to, src=None):
            return pltpu.make_async_remote_copy(
                src_ref=rows(*block) if src is None else src, dst_ref=rows(*block),
                send_sem=send_sems.at[k], recv_sem=recv_sems.at[k],
                device_id=to, device_id_type=pl.DeviceIdType.MESH)

        mine = pltpu.make_async_copy(x_ref, rows(*me), local_sem)
        mine.start()
        first = [copy(0, me, sibling, src=x_ref)]
        first += [copy(1 + j, me, (*chip, c), src=x_ref) for j, chip in enumerate(chips)]
        for cp in first:
            cp.start()
        passed = [copy(4 + j, (*chip, c), sibling) for j, chip in enumerate(chips)]
        for j, chip in enumerate(chips):
            copy(1 + j, (*chip, c), me).wait_recv()
            passed[j].start()
        copy(0, sibling, me).wait_recv()
        for j, chip in enumerate(chips):
            copy(4 + j, (*chip, 1 - c), me).wait_recv()
        for cp in first + passed:
            cp.wait_send()
        mine.wait()

    return pl.pallas_call(
        body, name=name,
        out_shape=jax.ShapeDtypeStruct((N_DEV * m_per, n), shard.dtype),
        in_specs=[pl.BlockSpec(memory_space=pltpu.VMEM)],
        out_specs=pl.BlockSpec(memory_space=pltpu.VMEM),
        scratch_shapes=[pltpu.SemaphoreType.DMA((7,)), pltpu.SemaphoreType.DMA((7,)), pltpu.SemaphoreType.DMA],
        compiler_params=pltpu.CompilerParams(vmem_limit_bytes=VMEM_LIMIT),
    )(shard)


def _by_group(lane, v2, v4, v8, v16):
    return jnp.where(lane < 64, v2, jnp.where(lane < 128, v4, jnp.where(lane < 192, v8, v16)))


def _pool_count(t0, n):
    lane = lax.broadcasted_iota(jnp.int32, (1, POOL_WIDTH), 1)
    t = (t0 + lax.broadcasted_iota(jnp.int32, (n, 1), 0)).astype(F32)
    wnd = _by_group(lane, 2.0, 4.0, 8.0, 16.0)
    return jnp.minimum(t + 1.0, wnd)


def _pool_diff(u_ext, t0, ts):
    lane = lax.broadcasted_iota(jnp.int32, (1, POOL_WIDTH), 1)
    s2 = u_ext + pltpu.roll(u_ext, 1, 0)
    s4 = s2 + pltpu.roll(s2, 2, 0)
    s8 = s4 + pltpu.roll(s4, 4, 0)
    s16 = s8 + pltpu.roll(s8, 8, 0)
    pooled = _by_group(lane, s2, s4, s8, s16)[HALO:]
    return pooled / _pool_count(t0, ts) - u_ext[HALO:]


def _pool_diff_bwd(w, ts):
    n = w.shape[0]
    lane = lax.broadcasted_iota(jnp.int32, (1, POOL_WIDTH), 1)
    f2 = w + pltpu.roll(w, n - 1, 0)
    f4 = f2 + pltpu.roll(f2, n - 2, 0)
    f8 = f4 + pltpu.roll(f4, n - 4, 0)
    f16 = f8 + pltpu.roll(f8, n - 8, 0)
    return _by_group(lane, f2, f4, f8, f16)[:ts]


def _shifted_down(x):
    return [x if r == 0 else pltpu.roll(x, r, 0) for r in range(8)]


def _causal_conv(hh_rolled, dw_ref, base, n_out):
    acc = None
    for r in range(8):
        for m in range(4):
            d = 8 * m + r
            if d >= CONV_KERNEL:
                continue
            term = dw_ref[pl.ds(CONV_KERNEL - 1 - d, 1), :] * hh_rolled[r][base - 8 * m: base - 8 * m + n_out]
            acc = term if acc is None else acc + term
    return acc


def _anticausal_conv(dcv, dw_ref, n_out):
    n = dcv.shape[0]
    acc = None
    for r in range(8):
        up = dcv if r == 0 else pltpu.roll(dcv, n - r, 0)
        for m in range(4):
            d = 8 * m + r
            if d >= CONV_KERNEL:
                continue
            term = dw_ref[pl.ds(CONV_KERNEL - 1 - d, 1), :] * up[8 * m: 8 * m + n_out]
            acc = term if acc is None else acc + term
    return acc


def _layer_norm(cv):
    mu = jnp.mean(cv, axis=-1, keepdims=True)
    xc = cv - mu
    var = jnp.mean(xc * xc, axis=-1, keepdims=True)
    rstd = lax.rsqrt(var + EPS)
    return xc * rstd, rstd


def _stack_heads(a, kh):
    return jnp.concatenate(
        [a[:, HEAD_DIM * (Q_PER_KV * kh + g): HEAD_DIM * (Q_PER_KV * kh + g + 1)] for g in range(Q_PER_KV)], axis=0)


def _unstack_heads(parts):
    return jnp.concatenate([p[BLOCK * g: BLOCK * (g + 1)] for p in parts for g in range(Q_PER_KV)], axis=-1)


def _per_head_rows(vals):
    g = lax.shift_right_logical(lax.broadcasted_iota(jnp.int32, (Q_PER_KV * BLOCK, 1), 0), 7)
    return jnp.where(g == 0, vals[0], jnp.where(g == 1, vals[1], jnp.where(g == 2, vals[2], vals[3])))


def _fill_attn_bias(bias_ref):
    rows = lax.broadcasted_iota(jnp.int32, (Q_PER_KV * BLOCK, 2 * BLOCK), 0)
    cols = lax.broadcasted_iota(jnp.int32, (Q_PER_KV * BLOCK, 2 * BLOCK), 1)
    dist = BLOCK + (rows & (BLOCK - 1)) - cols
    in_band = (dist >= 0) & (dist < BLOCK)
    distf = dist.astype(F32)
    for kh in range(N_KV_HEADS):
        slope = _per_head_rows([2.0 ** -(Q_PER_KV * kh + g + 1) for g in range(Q_PER_KV)])
        bias = -slope * distf
        bias_ref[0, kh] = jnp.where(in_band & (cols >= BLOCK), bias, NEG)
        bias_ref[1, kh] = jnp.where(in_band, bias, NEG)


def _attn_probs(q4s, k2, bias, kh, sinks_ref):
    sink = _per_head_rows([sinks_ref[Q_PER_KV * kh + g] for g in range(Q_PER_KV)])
    s = _mm(q4s, k2, _NT) + bias
    m = jnp.maximum(jnp.max(s, axis=-1, keepdims=True), sink)
    e = jnp.exp(s - m)
    es = jnp.exp(sink - m)
    inv = 1.0 / (jnp.sum(e, axis=-1, keepdims=True) + es)
    return e * inv, es * inv


def _attn_operands(pc, kvp, kh):
    lo, hi = HEAD_DIM * kh, HEAD_DIM * (kh + 1)
    k2 = jnp.concatenate([kvp[:, lo:hi], pc[:, 512 + lo:512 + hi]], axis=0).astype(MXU_DTYPE)
    v2 = jnp.concatenate([kvp[:, 128 + lo:128 + hi], pc[:, 640 + lo:640 + hi]], axis=0).astype(MXU_DTYPE)
    q4s = (_stack_heads(pc[:, 0:512], kh) * SCALE).astype(MXU_DTYPE)
    return q4s, k2, v2


def _in_proj(x, g, w_t):
    s_len, d = x.shape
    ts = _row_tile(s_len, ROW_TILE)

    def body(x_ref, g_ref, w_ref, o_ref):
        xv = x_ref[...]
        r = lax.rsqrt(jnp.mean(xv * xv, axis=-1, keepdims=True) + EPS)
        o_ref[...] = _mm(xv * r * g_ref[...], w_ref[...], _NT)

    return pl.pallas_call(
        body, name="in_proj", grid=(s_len // ts,),
        in_specs=[pl.BlockSpec((ts, d), lambda i: (i, 0)), _full((1, d)), _full((D_IN, d))],
        out_specs=pl.BlockSpec((ts, D_IN), lambda i: (i, 0)),
        out_shape=jax.ShapeDtypeStruct((s_len, D_IN), F32),
        compiler_params=_params(1),
    )(x, g, w_t)


def _poolconv_fwd(proj, wp, scale, dw, cb, lng, lnb, pw):
    s_len = proj.shape[0]
    ts = _row_tile(s_len, ROW_TILE)
    hb = ts // HALO

    def body(p_ref, ph_ref, wp_ref, sc_ref, dw_ref, cb_ref, lng_ref, lnb_ref, pw_ref, y_ref, cv_ref):
        i = pl.program_id(0)
        cur = p_ref[...]
        halo = jnp.where(i > 0, ph_ref[...], 0.0)
        ext = jnp.concatenate([halo, cur], axis=0)
        diff = _pool_diff(ext[:, 0:256], i * ts, ts)
        gp = cur[:, 256:512]
        y_pool = _mm(diff, wp_ref[...]) * sc_ref[...] * (gp * _sig(gp))
        hh = ext[:, 512:768] * _sig(ext[:, 768:1024])
        cv = _causal_conv(_shifted_down(hh), dw_ref, HALO, ts) + cb_ref[...]
        cv_ref[...] = cv
        n, _ = _layer_norm(cv)
        z = n * lng_ref[...] + lnb_ref[...]
        gc = cur[:, 1024:1280]
        y_conv = _mm(z * _sig(z), pw_ref[...]) * (gc * _sig(gc))
        y_ref[...] = jnp.concatenate([y_pool, y_conv], axis=-1).astype(y_ref.dtype)

    vec = _full((1, 256))
    return pl.pallas_call(
        body, name="poolconv_fwd", grid=(s_len // ts,),
        in_specs=[pl.BlockSpec((ts, HALF_IN), lambda i: (i, 0)),
                  pl.BlockSpec((HALO, HALF_IN), lambda i: (jnp.maximum(i * hb - 1, 0), 0)),
                  _full((256, 256)), vec, _full((CONV_TAPS_PAD, 256)), vec, vec, vec, _full((256, 256))],
        out_specs=[pl.BlockSpec((ts, 512), lambda i: (i, 0)), pl.BlockSpec((ts, 256), lambda i: (i, 0))],
        out_shape=[jax.ShapeDtypeStruct((s_len, 512), MXU_DTYPE), jax.ShapeDtypeStruct((s_len, 256), F32)],
        compiler_params=_params(1),
    )(proj, proj, wp, scale, dw, cb, lng, lnb, pw)


def _attn_fwd(proj, sinks, carried=()):
    s_len = proj.shape[0]
    nb = s_len // BLOCK
    exch = _Exchange(carried)

    def body(*refs):
        (p_ref, kvp_ref, sinks_ref), (y_ref,), (bias_ref,), xrefs = _split_refs(refs, 3, 1, exch)
        i = pl.program_id(0)
        exch.run(xrefs, i == 0, i == nb - 1)

        @pl.when(i == 0)
        def _():
            _fill_attn_bias(bias_ref)

        pc = p_ref[...]
        kvp = kvp_ref[...]
        ga = pc[:, 768:1280]
        variant = jnp.where(i == 0, 0, 1)
        outs = []
        for kh in range(N_KV_HEADS):
            q4s, k2, v2 = _attn_operands(pc, kvp, kh)
            p, _ = _attn_probs(q4s, k2, bias_ref[variant, kh], kh, sinks_ref)
            outs.append(_mm(p, v2))
        y_ref[...] = (_unstack_heads(outs) * (ga * _sig(ga))).astype(y_ref.dtype)

    res = pl.pallas_call(
        body, name="attn_fwd_carrier" if exch.n else "attn_fwd", grid=(nb,),
        in_specs=[pl.BlockSpec((BLOCK, HALF_IN), lambda i: (i, 1)),
                  pl.BlockSpec((BLOCK, 256), lambda i: (jnp.maximum(i - 1, 0), 7)),
                  pl.BlockSpec(memory_space=pltpu.SMEM)] + [_ANY] * exch.n,
        out_specs=[pl.BlockSpec((BLOCK, 512), lambda i: (i, 0))] + [_ANY] * exch.n,
        out_shape=[jax.ShapeDtypeStruct((s_len, 512), MXU_DTYPE)] + exch.out_shapes(),
        scratch_shapes=[pltpu.VMEM((2, N_KV_HEADS, Q_PER_KV * BLOCK, 2 * BLOCK), F32)] + exch.scratch(),
        compiler_params=_params(1),
    )(proj, proj, sinks, *exch.sources)
    return res[0], res[1:]


def _out_proj(x, y_pc, y_at, w_out):
    s_len, d = x.shape
    ts = _row_tile(s_len, ROW_TILE)

    def body(x_ref, a_ref, b_ref, w_ref, o_ref):
        y = jnp.concatenate([a_ref[...], b_ref[...]], axis=-1)
        o_ref[...] = x_ref[...] + _mm(y, w_ref[...])

    return pl.pallas_call(
        body, name="out_proj", grid=(s_len // ts,),
        in_specs=[pl.BlockSpec((ts, d), lambda i: (i, 0)), pl.BlockSpec((ts, 512), lambda i: (i, 0)),
                  pl.BlockSpec((ts, 512), lambda i: (i, 0)), _full((D_MIX, d))],
        out_specs=pl.BlockSpec((ts, d), lambda i: (i, 0)),
        out_shape=jax.ShapeDtypeStruct((s_len, d), F32),
        compiler_params=_params(1),
    )(x, y_pc, y_at, w_out)


def _loss_bwd(x, target, g):
    s_len, d = x.shape
    ts = _row_tile(s_len, ROW_TILE)

    def body(x_ref, t_ref, g_ref, sq_ref, dg_ref, dx_ref):
        i = pl.program_id(0)
        xv = x_ref[...]
        gv = g_ref[...]
        r = lax.rsqrt(jnp.mean(xv * xv, axis=-1, keepdims=True) + EPS)
        xr = xv * r
        err = xr * gv - t_ref[...]
        dout = err * (1.0 / d)
        w = dout * gv
        dx_ref[...] = r * (w - xr * jnp.mean(w * xr, axis=-1, keepdims=True))

        @pl.when(i == 0)
        def _():
            sq_ref[...] = jnp.zeros_like(sq_ref)
            dg_ref[...] = jnp.zeros_like(dg_ref)

        sq = jnp.sum(jnp.sum(err * err, axis=-1, keepdims=True), axis=0, keepdims=True)
        sq_ref[...] += jnp.broadcast_to(sq, sq_ref.shape)
        dg_ref[...] += jnp.sum(dout * xr, axis=0, keepdims=True)

    return pl.pallas_call(
        body, name="loss_bwd", grid=(s_len // ts,),
        in_specs=[pl.BlockSpec((ts, d), lambda i: (i, 0)), pl.BlockSpec((ts, d), lambda i: (i, 0)), _full((1, d))],
        out_specs=[_full((1, 128)), _full((1, d)), pl.BlockSpec((ts, d), lambda i: (i, 0))],
        out_shape=[jax.ShapeDtypeStruct((1, 128), F32), jax.ShapeDtypeStruct((1, d), F32),
                   jax.ShapeDtypeStruct((s_len, d), F32)],
        compiler_params=_params(1),
    )(x, target, g)


def _out_bwd(dxo, y_pc, y_at, w_out):
    s_len, d = dxo.shape
    ts = _row_tile(s_len, ROW_TILE)

    def body(dx_ref, a_ref, b_ref, w_ref, dy_ref, gw_ref):
        i = pl.program_id(0)
        dxv = dx_ref[...].astype(MXU_DTYPE)
        dy_ref[...] = _mm(dxv, w_ref[...], _NT)

        @pl.when(i == 0)
        def _():
            gw_ref[...] = jnp.zeros_like(gw_ref)

        y = jnp.concatenate([a_ref[...], b_ref[...]], axis=-1)
        gw_ref[...] += _mm(y, dxv, _TN)

    return pl.pallas_call(
        body, name="out_bwd", grid=(s_len // ts,),
        in_specs=[pl.BlockSpec((ts, d), lambda i: (i, 0)), pl.BlockSpec((ts, 512), lambda i: (i, 0)),
                  pl.BlockSpec((ts, 512), lambda i: (i, 0)), _full((D_MIX, d))],
        out_specs=[pl.BlockSpec((ts, D_MIX), lambda i: (i, 0)), _full((D_MIX, d))],
        out_shape=[jax.ShapeDtypeStruct((s_len, D_MIX), F32), jax.ShapeDtypeStruct((D_MIX, d), F32)],
        compiler_params=_params(1),
    )(dxo, y_pc, y_at, w_out)


def _poolconv_bwd(proj, cv, dy, wp, scale, dw, lng, lnb, pw, carried=()):
    s_len = proj.shape[0]
    ts = _row_tile(s_len, ROW_TILE)
    hb = ts // HALO
    nt = s_len // ts
    last_halo = s_len // HALO - 1
    n = ts + HALO
    exch = _Exchange(carried)

    def body(*refs):
        ins, outs, _, xrefs = _split_refs(refs, 13, 5, exch)
        (p_ref, ph_ref, pn_ref, cv_ref, cvn_ref, dy_ref, dyn_ref, wp_ref, sc_ref, dw_ref, lng_ref, lnb_ref,
         pw_ref) = ins
        da_ref, gwp_ref, gpw_ref, gdw_ref, gvec_ref = outs
        i = pl.program_id(0)
        exch.run(xrefs, i == 0, i == nt - 1)
        has_next = i < nt - 1
        cur = p_ref[...]
        prev = jnp.where(i > 0, ph_ref[...], 0.0)
        nxt = jnp.where(has_next, pn_ref[...], 0.0)
        dyx = jnp.concatenate([dy_ref[...], jnp.where(has_next, dyn_ref[...], 0.0)], axis=0)
        row = lax.broadcasted_iota(jnp.int32, (n, 1), 0)
        in_seq = (row < ts) | has_next
        scale_v = sc_ref[...]

        cvx = jnp.concatenate([cv_ref[...], jnp.where(has_next, cvn_ref[...], 0.0)], axis=0)
        nrm, rstd = _layer_norm(cvx)
        z = nrm * lng_ref[...] + lnb_ref[...]
        sz = _sig(z)
        sw = z * sz
        gc = jnp.concatenate([cur[:, 1024:1280], nxt[:, 1024:1280]], axis=0)
        sgc = _sig(gc)
        yc = _mm(sw, pw_ref[...])
        dyc = dyx[:, 256:512]
        d_yc = dyc * (gc * sgc)
        d_gc = (dyc * yc * _dsilu(gc, sgc))[:ts]
        d_z = _mm(d_yc, pw_ref[...], _NT) * _dsilu(z, sz)
        d_n = d_z * lng_ref[...]
        d_cv = rstd * (d_n - jnp.mean(d_n, axis=-1, keepdims=True)
                       - nrm * jnp.mean(d_n * nrm, axis=-1, keepdims=True))
        d_cv = jnp.where(in_seq, d_cv, 0.0)
        d_hh = _anticausal_conv(d_cv, dw_ref, ts)
        a2 = jnp.concatenate([prev[:, 512:768], cur[:, 512:768]], axis=0)
        sb2 = _sig(jnp.concatenate([prev[:, 768:1024], cur[:, 768:1024]], axis=0))
        hh_rolled = _shifted_down(a2 * sb2)
        a_c, sb_c = a2[HALO:], sb2[HALO:]
        d_a = d_hh * sb_c
        d_b = d_hh * a_c * sb_c * (1.0 - sb_c)
        d_cv_t = d_cv[:ts]
        g_dw_rows = [None] * CONV_TAPS_PAD
        for r in range(8):
            for m in range(4):
                d = 8 * m + r
                if d < CONV_KERNEL:
                    g_dw_rows[CONV_KERNEL - 1 - d] = jnp.sum(
                        d_cv_t * hh_rolled[r][HALO - 8 * m: HALO - 8 * m + ts], axis=0, keepdims=True)
        g_dw_rows[CONV_KERNEL] = jnp.zeros((1, CONV_WIDTH), F32)

        u_ext = jnp.concatenate([prev[:, 0:256], cur[:, 0:256]], axis=0)
        diff = _pool_diff(u_ext, i * ts, ts)
        raw = _mm(diff, wp_ref[...])
        gp = jnp.concatenate([cur[:, 256:512], nxt[:, 256:512]], axis=0)
        sgp = _sig(gp)
        dyp = dyx[:, 0:256]
        d_yp = dyp * (gp * sgp)
        d_gp = dyp[:ts] * (raw * scale_v) * _dsilu(gp, sgp)[:ts]
        d_raw = d_yp * scale_v
        d_diff = _mm(d_raw, wp_ref[...], _NT)
        w = jnp.where(in_seq, d_diff / _pool_count(i * ts, n), 0.0)
        d_u = _pool_diff_bwd(w, ts) - d_diff[:ts]

        da_ref[...] = jnp.concatenate([d_u, d_gp, d_a, d_b, d_gc], axis=-1).astype(da_ref.dtype)

        @pl.when(i == 0)
        def _():
            gwp_ref[...] = jnp.zeros_like(gwp_ref)
            gpw_ref[...] = jnp.zeros_like(gpw_ref)
            gdw_ref[...] = jnp.zeros_like(gdw_ref)
            gvec_ref[...] = jnp.zeros_like(gvec_ref)

        gwp_ref[...] += _mm(diff, d_raw[:ts], _TN)
        gpw_ref[...] += _mm(sw[:ts], d_yc[:ts], _TN)
        gdw_ref[...] += jnp.concatenate(g_dw_rows, axis=0)
        zero_row = jnp.zeros((1, 256), F32)
        gvec_ref[...] += jnp.concatenate([
            jnp.sum(d_yp[:ts] * raw, axis=0, keepdims=True),
            jnp.sum(d_cv_t, axis=0, keepdims=True),
            jnp.sum((d_z * nrm)[:ts], axis=0, keepdims=True),
            jnp.sum(d_z[:ts], axis=0, keepdims=True),
            zero_row, zero_row, zero_row, zero_row], axis=0)

    vec = _full((1, 256))
    nxt_halo = lambda i: (jnp.minimum((i + 1) * hb, last_halo), 0)
    res = pl.pallas_call(
        body, name="poolconv_bwd_carrier" if exch.n else "poolconv_bwd", grid=(nt,),
        in_specs=[pl.BlockSpec((ts, HALF_IN), lambda i: (i, 0)),
                  pl.BlockSpec((HALO, HALF_IN), lambda i: (jnp.maximum(i * hb - 1, 0), 0)),
                  pl.BlockSpec((HALO, HALF_IN), nxt_halo),
                  pl.BlockSpec((ts, 256), lambda i: (i, 0)), pl.BlockSpec((HALO, 256), nxt_halo),
                  pl.BlockSpec((ts, 512), lambda i: (i, 0)), pl.BlockSpec((HALO, 512), nxt_halo),
                  _full((256, 256)), vec, _full((CONV_TAPS_PAD, 256)), vec, vec, _full((256, 256))]
        + [_ANY] * exch.n,
        out_specs=[pl.BlockSpec((ts, HALF_IN), lambda i: (i, 0)), _full((256, 256)), _full((256, 256)),
                   _full((CONV_TAPS_PAD, 256)), _full((8, 256))] + [_ANY] * exch.n,
        out_shape=[jax.ShapeDtypeStruct((s_len, HALF_IN), MXU_DTYPE), jax.ShapeDtypeStruct((256, 256), F32),
                   jax.ShapeDtypeStruct((256, 256), F32), jax.ShapeDtypeStruct((CONV_TAPS_PAD, 256), F32),
                   jax.ShapeDtypeStruct((8, 256), F32)] + exch.out_shapes(),
        scratch_shapes=exch.scratch(),
        compiler_params=_params(1),
    )(proj, proj, proj, cv, cv, dy, dy, wp, scale, dw, lng, lnb, pw, *exch.sources)
    return res[:5], res[5:]


DQ0, DKC0, DVC0, DKP0, DVP0, DGA0, DATTN_W = 0, 512, 640, 768, 896, 1024, 1536


def _attn_bwd(proj, dy, sinks, carried=()):
    s_len = proj.shape[0]
    nb = s_len // BLOCK
    exch = _Exchange(carried)

    def body(*refs):
        (p_ref, kvp_ref, dy_ref, sinks_ref), (o_ref, gs_ref), (bias_ref,), xrefs = _split_refs(refs, 4, 2, exch)
        i = pl.program_id(0)
        exch.run(xrefs, i == 0, i == nb - 1)

        @pl.when(i == 0)
        def _():
            _fill_attn_bias(bias_ref)
            gs_ref[...] = jnp.zeros_like(gs_ref)

        pc = p_ref[...]
        kvp = kvp_ref[...]
        dya = dy_ref[...]
        ga = pc[:, 768:1280]
        variant = jnp.where(i == 0, 0, 1)
        dq_parts, dga_parts, dk_parts, dv_parts, ds_rows = [], [], [], [], []
        for kh in range(N_KV_HEADS):
            q4s, k2, v2 = _attn_operands(pc, kvp, kh)
            p, p_sink = _attn_probs(q4s, k2, bias_ref[variant, kh], kh, sinks_ref)
            o4 = _mm(p, v2)
            ga4 = _stack_heads(ga, kh)
            sga = _sig(ga4)
            dyo4 = _stack_heads(dya, kh)
            do4 = dyo4 * (ga4 * sga)
            dga_parts.append(dyo4 * o4 * _dsilu(ga4, sga))
            dp = _mm(do4, v2, _NT)
            delta = jnp.sum(p * dp, axis=-1, keepdims=True)
            ds = p * (dp - delta)
            dsink = -p_sink * delta
            for g in range(Q_PER_KV):
                ds_rows.append(jnp.sum(dsink[BLOCK * g: BLOCK * (g + 1)], axis=0, keepdims=True))
            dq_parts.append(_mm(ds, k2) * SCALE)
            dk_parts.append(_mm(ds, q4s, _TN))
            dv_parts.append(_mm(p, do4, _TN))
        dk = jnp.concatenate(dk_parts, axis=-1)
        dv = jnp.concatenate(dv_parts, axis=-1)
        o_ref[...] = jnp.concatenate([_unstack_heads(dq_parts), dk[BLOCK:], dv[BLOCK:], dk[:BLOCK], dv[:BLOCK],
                                      _unstack_heads(dga_parts)], axis=-1)
        gs_ref[...] += jnp.broadcast_to(jnp.concatenate(ds_rows, axis=0), gs_ref.shape)

    res = pl.pallas_call(
        body, name="attn_bwd_carrier" if exch.n else "attn_bwd", grid=(nb,),
        in_specs=[pl.BlockSpec((BLOCK, HALF_IN), lambda i: (i, 1)),
                  pl.BlockSpec((BLOCK, 256), lambda i: (jnp.maximum(i - 1, 0), 7)),
                  pl.BlockSpec((BLOCK, 512), lambda i: (i, 1)),
                  pl.BlockSpec(memory_space=pltpu.SMEM)] + [_ANY] * exch.n,
        out_specs=[pl.BlockSpec((BLOCK, DATTN_W), lambda i: (i, 0)), _full((N_Q_HEADS, 128))] + [_ANY] * exch.n,
        out_shape=[jax.ShapeDtypeStruct((s_len, DATTN_W), F32), jax.ShapeDtypeStruct((N_Q_HEADS, 128), F32)]
        + exch.out_shapes(),
        scratch_shapes=[pltpu.VMEM((2, N_KV_HEADS, Q_PER_KV * BLOCK, 2 * BLOCK), F32)] + exch.scratch(),
        compiler_params=_params(1),
    )(proj, proj, dy, sinks, *exch.sources)
    return res[:2], res[2:]


def _in_bwd(da, dattn, x, dxo, g, w_t, carried=()):
    s_len, d = x.shape
    ts = _row_tile(s_len, IN_BWD_TILE)
    bpt = ts // BLOCK
    nt = s_len // ts
    last_block = s_len // BLOCK - 1
    exch = _Exchange(carried)

    def body(*refs):
        ins, (dx_ref, dg_ref, gw_ref), _, xrefs = _split_refs(refs, 7, 3, exch)
        da_ref, dat_ref, nxt_ref, x_ref, dxo_ref, g_ref, w_ref = ins
        i = pl.program_id(0)
        exch.run(xrefs, i == 0, i == nt - 1)
        dat = dat_ref[...]
        nxt = jnp.where(i < nt - 1, nxt_ref[...], 0.0)
        shifted = jnp.concatenate([dat[BLOCK:, DKP0:DGA0], nxt], axis=0) if bpt > 1 else nxt
        dkv = dat[:, DKC0:DKP0] + shifted
        dproj = jnp.concatenate([da_ref[...], dat[:, DQ0:DKC0].astype(MXU_DTYPE), dkv.astype(MXU_DTYPE),
                                 dat[:, DGA0:DATTN_W].astype(MXU_DTYPE)], axis=-1)
        d_h = _mm(dproj, w_ref[...])
        xv = x_ref[...]
        gv = g_ref[...]
        r = lax.rsqrt(jnp.mean(xv * xv, axis=-1, keepdims=True) + EPS)
        xr = xv * r
        w = d_h * gv
        dx_ref[...] = dxo_ref[...] + r * (w - xr * jnp.mean(w * xr, axis=-1, keepdims=True))

        @pl.when(i == 0)
        def _():
            dg_ref[...] = jnp.zeros_like(dg_ref)
            gw_ref[...] = jnp.zeros_like(gw_ref)

        dg_ref[...] += jnp.sum(d_h * xr, axis=0, keepdims=True)
        gw_ref[...] += _mm(dproj, xr * gv, _TN)

    res = pl.pallas_call(
        body, name="in_bwd_carrier" if exch.n else "in_bwd", grid=(nt,),
        in_specs=[pl.BlockSpec((ts, HALF_IN), lambda i: (i, 0)),
                  pl.BlockSpec((ts, DATTN_W), lambda i: (i, 0)),
                  pl.BlockSpec((BLOCK, 256), lambda i: (jnp.minimum((i + 1) * bpt, last_block), 3)),
                  pl.BlockSpec((ts, d), lambda i: (i, 0)), pl.BlockSpec((ts, d), lambda i: (i, 0)),
                  _full((1, d)), _full((D_IN, d))] + [_ANY] * exch.n,
        out_specs=[pl.BlockSpec((ts, d), lambda i: (i, 0)), _full((1, d)), _full((D_IN, d))] + [_ANY] * exch.n,
        out_shape=[jax.ShapeDtypeStruct((s_len, d), F32), jax.ShapeDtypeStruct((1, d), F32),
                   jax.ShapeDtypeStruct((D_IN, d), F32)] + exch.out_shapes(),
        scratch_shapes=exch.scratch(),
        compiler_params=_params(1),
    )(da, dattn, dattn, x, dxo, g, w_t, *exch.sources)
    return res[:3], res[3:]


def _adamw(parts, w, m, v, name):
    rows, n = w.shape
    tr = _row_tile(rows, 256)

    def body(p_ref, w_ref, m_ref, v_ref, g_ref, d_ref, nm_ref, nv_ref):
        g = p_ref[0]
        for k in range(1, N_DEV):
            g = g + p_ref[k]
        nm = ADAM_B1 * m_ref[...] + (1.0 - ADAM_B1) * g
        nv = ADAM_B2 * v_ref[...] + (1.0 - ADAM_B2) * (g * g)
        m_hat = nm / (1.0 - ADAM_B1 ** ADAM_STEP)
        v_hat = nv / (1.0 - ADAM_B2 ** ADAM_STEP)
        g_ref[...] = g
        nm_ref[...] = nm
        nv_ref[...] = nv
        d_ref[...] = -ADAM_LR * (m_hat / (jnp.sqrt(v_hat) + ADAM_EPS) + ADAM_WD * w_ref[...])

    tile = pl.BlockSpec((tr, n), lambda i: (i, 0))
    shape = jax.ShapeDtypeStruct((rows, n), F32)
    return pl.pallas_call(
        body, name=name, grid=(rows // tr,),
        in_specs=[pl.BlockSpec((N_DEV, tr, n), lambda i: (0, i, 0)), tile, tile, tile],
        out_specs=[tile, tile, tile, tile],
        out_shape=[shape, shape, shape, shape],
        compiler_params=_params(1),
    )(parts, w, m, v)


def _pad_rows(a, mult):
    pad = (-a.shape[0]) % mult
    return a if pad == 0 else jnp.concatenate([a, jnp.zeros((pad, a.shape[1]), a.dtype)], axis=0)


def _dw_rows(conv_dw_l):
    return jnp.pad(jnp.swapaxes(conv_dw_l, 0, 1), ((0, 0), (0, CONV_TAPS_PAD - CONV_KERNEL)))


def _pack_small(pw_l, dw_l, d):
    rows = jnp.concatenate([pw_l.reshape(-1, d), _dw_rows(dw_l).reshape(-1, d)], axis=0)
    return _pad_rows(rows, 8)


def _small_slabs(g_pw, g_dw, d):
    a = g_pw.reshape(N_DEV, -1, d)
    b = jnp.swapaxes(g_dw, 0, 1).reshape(N_DEV, -1, d)
    used = a.shape[1] + b.shape[1]
    return jnp.concatenate([a, b, jnp.zeros((N_DEV, (-used) % 8, d), g_pw.dtype)], axis=1)


def _unpack_small(rows, d):
    c = CONV_WIDTH // N_DEV
    n_pw = c * CONV_WIDTH // d
    n_dw = c * CONV_TAPS_PAD // d
    pw = rows[:n_pw].reshape(c, CONV_WIDTH)
    dw = jnp.swapaxes(rows[n_pw:n_pw + n_dw].reshape(c, CONV_TAPS_PAD), 0, 1)[:CONV_KERNEL]
    return pw, dw


def _pack_replicated(ln_g, pool_w, pool_scale, conv_b, conv_ln_g, conv_ln_b, attn_sinks, final_g, d):
    sinks = jnp.pad(attn_sinks, ((0, 0), (0, 256 - N_Q_HEADS)))
    small = jnp.concatenate([pool_scale, conv_b, conv_ln_g, conv_ln_b, sinks], axis=0)
    small = _pad_rows(small, d // 256)
    return _pad_rows(jnp.concatenate([ln_g.reshape(-1, d), final_g.reshape(-1, d), pool_w.reshape(-1, d),
                                      small.reshape(-1, d)], axis=0), 8)


def _unpack_replicated(rows, d):
    n_pool = DEPTH * 4 * POOL_GROUP * POOL_GROUP // d
    ln_g = rows[:DEPTH]
    final_g = rows[DEPTH]
    pool_w = rows[DEPTH + 1: DEPTH + 1 + n_pool].reshape(DEPTH, 4, POOL_GROUP, POOL_GROUP)
    small = rows[DEPTH + 1 + n_pool:].reshape(-1, 256)[: 5 * DEPTH]
    pool_scale, conv_b, conv_ln_g, conv_ln_b = (small[DEPTH * k: DEPTH * (k + 1)] for k in range(4))
    sinks = small[4 * DEPTH: 5 * DEPTH, :N_Q_HEADS]
    return ln_g, pool_w, pool_scale, conv_b, conv_ln_g, conv_ln_b, sinks, final_g


def _weight_shard(w_in_l, w_out_l, pw_l, extra, d):
    rows = [jnp.swapaxes(w_in_l, 0, 1).astype(MXU_DTYPE), w_out_l.astype(MXU_DTYPE),
            pw_l.reshape(-1, d).astype(MXU_DTYPE)] + extra
    return _pad_rows(jnp.concatenate(rows, axis=0), 16)


def _unpack_weights(full, d):
    n_in, n_out = D_IN // N_DEV, D_MIX // N_DEV
    n_pw = (CONV_WIDTH // N_DEV) * CONV_WIDTH // d
    w_in_t = full[:, :n_in].reshape(D_IN, d)
    w_out = full[:, n_in:n_in + n_out].reshape(D_MIX, d)
    pw = full[:, n_in + n_out:n_in + n_out + n_pw].reshape(CONV_WIDTH, CONV_WIDTH)
    return w_in_t, w_out, pw, full[:, n_in + n_out + n_pw:]


def _block_diag(pool_w):
    out = jnp.zeros((POOL_WIDTH, POOL_WIDTH), pool_w.dtype)
    for gi in range(4):
        out = out.at[POOL_GROUP * gi: POOL_GROUP * (gi + 1), POOL_GROUP * gi: POOL_GROUP * (gi + 1)].set(pool_w[gi])
    return out


def _diag_blocks(mat):
    return jnp.stack([mat[POOL_GROUP * gi: POOL_GROUP * (gi + 1), POOL_GROUP * gi: POOL_GROUP * (gi + 1)]
                      for gi in range(4)], axis=0)


def kernel(x, ln_g, w_in, pool_w, pool_scale, conv_dw, conv_b, conv_ln_g, conv_ln_b, conv_pw, attn_sinks, w_out, final_g, loss_target, m_ln_g, m_w_in, m_pool_w, m_pool_scale, m_conv_dw, m_conv_b, m_conv_ln_g, m_conv_ln_b, m_conv_pw, m_attn_sinks, m_w_out, m_final_g, v_ln_g, v_w_in, v_pool_w, v_pool_scale, v_conv_dw, v_conv_b, v_conv_ln_g, v_conv_ln_b, v_conv_pw, v_attn_sinks, v_w_out, v_final_g):
    x0 = x[0]
    d = x0.shape[1]
    row = lambda a: a.reshape(1, -1)
    slabs = lambda a: a.reshape(N_DEV, a.shape[0] // N_DEV, d)
    c_shard = CONV_WIDTH // N_DEV

    per_word = 4 // jnp.dtype(MXU_DTYPE).itemsize
    dw_t = jnp.stack([_dw_rows(conv_dw[l]) for l in range(DEPTH)], axis=0)
    dw_bits = (lax.bitcast_convert_type(dw_t, MXU_DTYPE) if per_word > 1 else dw_t).reshape(-1, d)
    shard0 = _weight_shard(w_in[0], w_out[0], conv_pw[0], [dw_bits], d)
    shard1 = _weight_shard(w_in[1], w_out[1], conv_pw[1], [], d)
    full0 = _all_gather(shard0, "weight_all_gather").reshape(N_DEV, shard0.shape[0], d)
    w_in_t, w_out_f, pw_f = [None] * DEPTH, [None] * DEPTH, [None] * DEPTH
    w_in_t[0], w_out_f[0], pw_f[0], rest = _unpack_weights(full0, d)
    bits = rest[:, :dw_bits.shape[0]].reshape((N_DEV, DEPTH, c_shard, CONV_TAPS_PAD) + (per_word,) * (per_word > 1))
    dw_all = lax.bitcast_convert_type(bits, F32) if per_word > 1 else bits
    dw_f = [jnp.swapaxes(dw_all[:, l].reshape(CONV_WIDTH, CONV_TAPS_PAD), 0, 1) for l in range(DEPTH)]
    wp_bd = [_block_diag(pool_w[l]).astype(MXU_DTYPE) for l in range(DEPTH)]

    xs, projs, cvs, ys = [x0], [], [], []
    for l in range(DEPTH):
        proj = _in_proj(xs[l], row(ln_g[l]), w_in_t[l])
        y_pc, cv = _poolconv_fwd(proj, wp_bd[l], row(pool_scale[l]), dw_f[l], row(conv_b[l]), row(conv_ln_g[l]),
                                 row(conv_ln_b[l]), pw_f[l])
        y_at, got = _attn_fwd(proj, attn_sinks[l], carried=[shard1] if l == 0 else [])
        if l == 0:
            w_in_t[1], w_out_f[1], pw_f[1], _ = _unpack_weights(got[0], d)
        xs.append(_out_proj(xs[l], y_pc, y_at, w_out_f[l]))
        projs.append(proj)
        cvs.append(cv)
        ys.append((y_pc, y_at))

    sq, g_final, dx = _loss_bwd(xs[DEPTH], loss_target[0], row(final_g))
    loss = lax.psum(0.5 / d * sq[0, 0], ("x", "y", "c"))

    l = 1
    dy, g_wout1 = _out_bwd(dx, ys[l][0], ys[l][1], w_out_f[l])
    (da, g_wp1, g_pw1, g_dw1, g_vec1), _ = _poolconv_bwd(
        projs[l], cvs[l], dy, wp_bd[l], row(pool_scale[l]), dw_f[l], row(conv_ln_g[l]), row(conv_ln_b[l]), pw_f[l])
    (dattn, gs1), _ = _attn_bwd(projs[l], dy, attn_sinks[l])
    (dx, g_ln1, g_win_t1), _ = _in_bwd(da, dattn, xs[l], dx, row(ln_g[l]), w_in_t[l])
    small1 = _small_slabs(g_pw1, g_dw1, d)
    l = 0
    dy, g_wout0 = _out_bwd(dx, ys[l][0], ys[l][1], w_out_f[l])
    (da, g_wp0, g_pw0, g_dw0, g_vec0), (r_wout1, r_small1) = _poolconv_bwd(
        projs[l], cvs[l], dy, wp_bd[l], row(pool_scale[l]), dw_f[l], row(conv_ln_g[l]), row(conv_ln_b[l]), pw_f[l],
        carried=[slabs(g_wout1), small1])
    (dattn, gs0), (r_win1,) = _attn_bwd(projs[l], dy, attn_sinks[l], carried=[slabs(g_win_t1)])
    small0 = _small_slabs(g_pw0, g_dw0, d)
    (dx, g_ln0, g_win_t0), (r_wout0, r_small0) = _in_bwd(da, dattn, xs[l], dx, row(ln_g[l]), w_in_t[l],
                                                         carried=[slabs(g_wout0), small0])
    grad_x = dx[None]

    gv = jnp.stack([g_vec0, g_vec1], axis=0)
    rep_part = _pack_replicated(
        jnp.concatenate([g_ln0, g_ln1], axis=0), jnp.stack([_diag_blocks(g_wp0), _diag_blocks(g_wp1)], axis=0),
        gv[:, 0], gv[:, 1], gv[:, 2], gv[:, 3], jnp.stack([gs0[:, 0], gs1[:, 0]], axis=0), g_final, d)
    r_win0, r_rep = _final_exchange([slabs(g_win_t0), rep_part])

    t = lambda a: jnp.swapaxes(a, 0, 1)
    win = [[t(o) for o in _adamw(r, t(w_in[l]), t(m_w_in[l]), t(v_w_in[l]), f"adamw_w_in_{l}")]
           for l, r in enumerate([r_win0, r_win1])]
    wout = [_adamw(r, w_out[l], m_w_out[l], v_w_out[l], f"adamw_w_out_{l}") for l, r in enumerate([r_wout0, r_wout1])]
    small = [[_unpack_small(o, d) for o in _adamw(
        r, _pack_small(conv_pw[l], conv_dw[l], d), _pack_small(m_conv_pw[l], m_conv_dw[l], d),
        _pack_small(v_conv_pw[l], v_conv_dw[l], d), f"adamw_conv_{l}")] for l, r in enumerate([r_small0, r_small1])]
    pack_r = lambda *a: _pack_replicated(*a, d)
    rep = [_unpack_replicated(o, d) for o in _adamw(
        r_rep, pack_r(ln_g, pool_w, pool_scale, conv_b, conv_ln_g, conv_ln_b, attn_sinks, final_g),
        pack_r(m_ln_g, m_pool_w, m_pool_scale, m_conv_b, m_conv_ln_g, m_conv_ln_b, m_attn_sinks, m_final_g),
        pack_r(v_ln_g, v_pool_w, v_pool_scale, v_conv_b, v_conv_ln_g, v_conv_ln_b, v_attn_sinks, v_final_g),
        "adamw_replicated")]

    outs = []
    for k in range(4):
        r_ln, r_pool, r_scale, r_cb, r_lng, r_lnb, r_sinks, r_final = rep[k]
        s_win = jnp.stack([win[l][k] for l in range(DEPTH)], axis=0)
        s_wout = jnp.stack([wout[l][k] for l in range(DEPTH)], axis=0)
        s_pw = jnp.stack([small[l][k][0] for l in range(DEPTH)], axis=0)
        s_dw = jnp.stack([small[l][k][1] for l in range(DEPTH)], axis=0)
        outs += [r_ln, s_win, r_pool, r_scale, s_dw, r_cb, r_lng, r_lnb, s_pw, r_sinks, s_wout, r_final]
    return (loss, grad_x, *outs)
```

```python
import jax
import jax.numpy as jnp
from jax import lax
from jax.experimental import pallas as pl
from jax.experimental.pallas import tpu as pltpu

F32 = jnp.float32
MXU_DTYPE = jnp.bfloat16
EXCHANGE_DTYPE = jnp.bfloat16

N_DEV = 8
DEPTH = 2
POOL_WIDTH = 256
POOL_GROUP = 64
CONV_WIDTH = 256
CONV_KERNEL = 31
CONV_TAPS_PAD = 32
HEAD_DIM = 64
N_KV_HEADS = 2
Q_PER_KV = 4
N_Q_HEADS = 8
ATTN_WIDTH = 512
BLOCK = 128
D_MIX = 1024
D_IN = 2560
HALF_IN = 1280
EPS = 1e-6
SCALE = HEAD_DIM ** -0.5
NEG = -1e30

ADAM_LR = 0.001
ADAM_B1 = 0.9
ADAM_B2 = 0.999
ADAM_EPS = 1e-08
ADAM_WD = 0.01
ADAM_STEP = 10

HALO = 32
ROW_TILE = 512
IN_BWD_TILE = 256
VMEM_LIMIT = 56 * 1024 * 1024

_NN = (((1,), (0,)), ((), ()))
_NT = (((1,), (1,)), ((), ()))
_TN = (((0,), (0,)), ((), ()))
_ANY = pl.BlockSpec(memory_space=pl.ANY)


def _mm(a, b, dims=_NN):
    return lax.dot_general(a.astype(MXU_DTYPE), b.astype(MXU_DTYPE), dims, preferred_element_type=F32)


def _sig(x):
    return 1.0 / (1.0 + jnp.exp(-x))


def _dsilu(z, s):
    return s * (1.0 + z * (1.0 - s))


def _params(n_grid):
    return pltpu.CompilerParams(dimension_semantics=("arbitrary",) * n_grid, vmem_limit_bytes=VMEM_LIMIT)


def _row_tile(rows, cap):
    t = min(rows, cap)
    while rows % t or t % 8:
        t -= 8
    return t


def _full(shape):
    return pl.BlockSpec(shape, lambda i: (0,) * len(shape))


def _mesh_pos():
    return lax.axis_index("x"), lax.axis_index("y"), lax.axis_index("c")


class _Exchange:
    def __init__(self, sources):
        self.sources = list(sources)
        self.n = len(self.sources)
        self.gather = [s.ndim == 2 for s in self.sources]

    def out_shapes(self):
        return [jax.ShapeDtypeStruct((N_DEV,) + s.shape[-2:], s.dtype) for s in self.sources]

    def scratch(self):
        if not self.n:
            return []
        return [pltpu.SemaphoreType.DMA((7 * self.n,)), pltpu.SemaphoreType.DMA((7 * self.n,)),
                pltpu.SemaphoreType.DMA((self.n,))]

    def copies(self, src_refs, dst_refs, sems):
        send_sems, recv_sems, local_sems = sems
        x, y, c = _mesh_pos()
        me = 4 * x + 2 * y + c
        out = []
        for a, (src, dst) in enumerate(zip(src_refs, dst_refs)):
            out.append(pltpu.make_async_copy(src if self.gather[a] else src.at[me], dst.at[me], local_sems.at[a]))
            for k in range(1, N_DEV):
                tx, ty, tc = x ^ ((k >> 2) & 1), y ^ ((k >> 1) & 1), c ^ (k & 1)
                out.append(pltpu.make_async_remote_copy(
                    src_ref=src if self.gather[a] else src.at[4 * tx + 2 * ty + tc], dst_ref=dst.at[me],
                    send_sem=send_sems.at[7 * a + k - 1], recv_sem=recv_sems.at[7 * a + k - 1],
                    device_id=(tx, ty, tc), device_id_type=pl.DeviceIdType.MESH))
        return out

    def run(self, refs, first, last):
        if not self.n:
            return
        src_refs, dst_refs, sems = refs

        @pl.when(first)
        def _():
            for cp in self.copies(src_refs, dst_refs, sems):
                cp.start()

        @pl.when(last)
        def _():
            for cp in self.copies(src_refs, dst_refs, sems):
                cp.wait()


def _split_refs(refs, n_in, n_out, exch):
    ins = refs[:n_in]
    srcs = refs[n_in:n_in + exch.n]
    outs = refs[n_in + exch.n:n_in + exch.n + n_out]
    dsts = refs[n_in + exch.n + n_out:n_in + 2 * exch.n + n_out]
    rest = refs[n_in + 2 * exch.n + n_out:]
    sems = rest[len(rest) - 3:] if exch.n else ()
    scratch = rest[:len(rest) - 3] if exch.n else rest
    return ins, outs, scratch, (srcs, dsts, sems)


def _final_exchange(sources):
    exch = _Exchange(sources)

    def body(*refs):
        _, _, _, xrefs = _split_refs(refs, 0, 0, exch)
        for cp in exch.copies(*xrefs):
            cp.start()
        for cp in exch.copies(*xrefs):
            cp.wait()

    return pl.pallas_call(
        body, name="final_exchange", out_shape=exch.out_shapes(),
        in_specs=[_ANY] * exch.n, out_specs=[_ANY] * exch.n, scratch_shapes=exch.scratch(),
    )(*exch.sources)


def _all_gather(shard, name):
    m_per, n = shard.shape

    def body(x_ref, out_ref, send_sems, recv_sems, local_sem):
        x, y, c = _mesh_pos()
        me, sibling = (x, y, c), (x, y, 1 - c)
        chips = [(1 - x, y), (x, 1 - y), (1 - x, 1 - y)]

        def rows(px, py, pc):
            return out_ref.at[pl.ds((4 * px + 2 * py + pc) * m_per, m_per), :]

        def copy(k, block, to, src=None):
            return pltpu.make_async_remote_copy(
                src_ref=rows(*block) if src is None else src, dst_ref=rows(*block),
                send_sem=send_sems.at[k], recv_sem=recv_sems.at[k],
                device_id=to, device_id_type=pl.DeviceIdType.MESH)

        mine = pltpu.make_async_copy(x_ref, rows(*me), local_sem)
        mine.start()
        first = [copy(0, me, sibling, src=x_ref)]
        first += [copy(1 + j, me, (*chip, c), src=x_ref) for j, chip in enumerate(chips)]
        for cp in first:
            cp.start()
        passed = [copy(4 + j, (*chip, c), sibling) for j, chip in enumerate(chips)]
        for j, chip in enumerate(chips):
            copy(1 + j, (*chip, c), me).wait_recv()
            passed[j].start()
        copy(0, sibling, me).wait_recv()
        for j, chip in enumerate(chips):
            copy(4 + j, (*chip, 1 - c), me).wait_recv()
        for cp in first + passed:
            cp.wait_send()
        mine.wait()

    return pl.pallas_call(
        body, name=name,
        out_shape=jax.ShapeDtypeStruct((N_DEV * m_per, n), shard.dtype),
        in_specs=[pl.BlockSpec(memory_space=pltpu.VMEM)],
        out_specs=pl.BlockSpec(memory_space=pltpu.VMEM),
        scratch_shapes=[pltpu.SemaphoreType.DMA((7,)), pltpu.SemaphoreType.DMA((7,)), pltpu.SemaphoreType.DMA],
        compiler_params=pltpu.CompilerParams(vmem_limit_bytes=VMEM_LIMIT),
    )(shard)


def _by_group(lane, v2, v4, v8, v16):
    return jnp.where(lane < 64, v2, jnp.where(lane < 128, v4, jnp.where(lane < 192, v8, v16)))


def _pool_count(t0, n):
    lane = lax.broadcasted_iota(jnp.int32, (1, POOL_WIDTH), 1)
    t = (t0 + lax.broadcasted_iota(jnp.int32, (n, 1), 0)).astype(F32)
    wnd = _by_group(lane, 2.0, 4.0, 8.0, 16.0)
    return jnp.minimum(t + 1.0, wnd)


def _pool_diff(u_ext, t0, ts):
    lane = lax.broadcasted_iota(jnp.int32, (1, POOL_WIDTH), 1)
    s2 = u_ext + pltpu.roll(u_ext, 1, 0)
    s4 = s2 + pltpu.roll(s2, 2, 0)
    s8 = s4 + pltpu.roll(s4, 4, 0)
    s16 = s8 + pltpu.roll(s8, 8, 0)
    pooled = _by_group(lane, s2, s4, s8, s16)[HALO:]
    return pooled / _pool_count(t0, ts) - u_ext[HALO:]


def _pool_diff_bwd(w, ts):
    n = w.shape[0]
    lane = lax.broadcasted_iota(jnp.int32, (1, POOL_WIDTH), 1)
    f2 = w + pltpu.roll(w, n - 1, 0)
    f4 = f2 + pltpu.roll(f2, n - 2, 0)
    f8 = f4 + pltpu.roll(f4, n - 4, 0)
    f16 = f8 + pltpu.roll(f8, n - 8, 0)
    return _by_group(lane, f2, f4, f8, f16)[:ts]


def _shifted_down(x):
    return [x if r == 0 else pltpu.roll(x, r, 0) for r in range(8)]


def _causal_conv(hh_rolled, dw_ref, base, n_out):
    acc = None
    for r in range(8):
        for m in range(4):
            d = 8 * m + r
            if d >= CONV_KERNEL:
                continue
            term = dw_ref[pl.ds(CONV_KERNEL - 1 - d, 1), :] * hh_rolled[r][base - 8 * m: base - 8 * m + n_out]
            acc = term if acc is None else acc + term
    return acc


def _anticausal_conv(dcv, dw_ref, n_out):
    n = dcv.shape[0]
    acc = None
    for r in range(8):
        up = dcv if r == 0 else pltpu.roll(dcv, n - r, 0)
        for m in range(4):
            d = 8 * m + r
            if d >= CONV_KERNEL:
                continue
            term = dw_ref[pl.ds(CONV_KERNEL - 1 - d, 1), :] * up[8 * m: 8 * m + n_out]
            acc = term if acc is None else acc + term
    return acc


def _layer_norm(cv):
    mu = jnp.mean(cv, axis=-1, keepdims=True)
    xc = cv - mu
    var = jnp.mean(xc * xc, axis=-1, keepdims=True)
    rstd = lax.rsqrt(var + EPS)
    return xc * rstd, rstd


PAIR_ROWS = 2 * BLOCK
PAIR_COLS = 4 * BLOCK
ATTN_BLOCKS_PER_STEP = 4


def _attn_blocks_per_step(s_len):
    qb = ATTN_BLOCKS_PER_STEP
    while (s_len // BLOCK) % qb:
        qb //= 2
    return qb


def _pair_rows(v0, v1):
    r = lax.broadcasted_iota(jnp.int32, (PAIR_ROWS, 1), 0)
    return jnp.where(r < BLOCK, v0, v1)


def _fill_attn_bias(bias_ref):
    rows = lax.broadcasted_iota(jnp.int32, (PAIR_ROWS, PAIR_COLS), 0)
    cols = lax.broadcasted_iota(jnp.int32, (PAIR_ROWS, PAIR_COLS), 1)
    key = cols & (2 * BLOCK - 1)
    dist = BLOCK + (rows & (BLOCK - 1)) - key
    in_band = (dist >= 0) & (dist < BLOCK)
    distf = dist.astype(F32)
    second = cols >= 2 * BLOCK
    for kh in range(N_KV_HEADS):
        slope_of = lambda j, a: 2.0 ** -(Q_PER_KV * kh + 2 * j + a + 1)
        slope = jnp.where(rows < BLOCK, jnp.where(second, slope_of(0, 1), slope_of(0, 0)),
                          jnp.where(second, slope_of(1, 1), slope_of(1, 0)))
        bias = -slope * distf
        bias_ref[0, kh] = jnp.where(in_band & (key >= BLOCK), bias, NEG)
        bias_ref[1, kh] = jnp.where(in_band, bias, NEG)


def _pair_block_matrix(x, x_swapped, kh):
    lo = lax.broadcasted_iota(jnp.int32, (1, 2 * HEAD_DIM), 1) < HEAD_DIM
    in_lo, in_hi = (x, x_swapped) if kh == 0 else (x_swapped, x)
    return jnp.concatenate([jnp.where(lo, in_lo, 0.0), jnp.where(lo, 0.0, in_hi)], axis=0).astype(MXU_DTYPE)


def _pair_queries(q, kh):
    return (_pair_stack(q, kh) * SCALE).astype(MXU_DTYPE)


def _pair_stack(a, kh):
    return jnp.concatenate([a[:, 2 * BLOCK * kh: 2 * BLOCK * kh + BLOCK],
                            a[:, 2 * BLOCK * kh + BLOCK: 2 * BLOCK * (kh + 1)]], axis=0)


def _pair_unstack(parts):
    return jnp.concatenate([p[BLOCK * j: BLOCK * (j + 1)] for p in parts for j in range(2)], axis=-1)


def _pair_softmax(s, kh, sinks_ref):
    ps, p_sinks = [], []
    for a in range(2):
        sa = s[:, 2 * BLOCK * a: 2 * BLOCK * (a + 1)]
        sink = _pair_rows(sinks_ref[Q_PER_KV * kh + a], sinks_ref[Q_PER_KV * kh + 2 + a])
        m = jnp.maximum(jnp.max(sa, axis=-1, keepdims=True), sink)
        e = jnp.exp(sa - m)
        es = jnp.exp(sink - m)
        inv = 1.0 / (jnp.sum(e, axis=-1, keepdims=True) + es)
        ps.append(e * inv)
        p_sinks.append(es * inv)
    return jnp.concatenate(ps, axis=-1), p_sinks


def _fold_pair_halves(t):
    lo = lax.broadcasted_iota(jnp.int32, (1, 2 * HEAD_DIM), 1) < HEAD_DIM
    u = jnp.where(lo, t[:2 * BLOCK], t[2 * BLOCK:])
    return u + pltpu.roll(u, HEAD_DIM, 1)


def _in_proj(x, g, w_t, carried=()):
    s_len, d = x.shape
    ts = _row_tile(s_len, ROW_TILE)
    nt = s_len // ts
    exch = _Exchange(carried)

    def body(*refs):
        (x_ref, g_ref, w_ref), (o_ref,), _, xrefs = _split_refs(refs, 3, 1, exch)
        i = pl.program_id(0)
        exch.run(xrefs, i == 0, i == nt - 1)
        xv = x_ref[...]
        r = lax.rsqrt(jnp.mean(xv * xv, axis=-1, keepdims=True) + EPS)
        o_ref[...] = _mm(xv * r * g_ref[...], w_ref[...], _NT)

    res = pl.pallas_call(
        body, name="in_proj_carrier" if exch.n else "in_proj", grid=(nt,),
        in_specs=[pl.BlockSpec((ts, d), lambda i: (i, 0)), _full((1, d)), _full((D_IN, d))] + [_ANY] * exch.n,
        out_specs=[pl.BlockSpec((ts, D_IN), lambda i: (i, 0))] + [_ANY] * exch.n,
        out_shape=[jax.ShapeDtypeStruct((s_len, D_IN), F32)] + exch.out_shapes(),
        scratch_shapes=exch.scratch(),
        compiler_params=_params(1),
    )(x, g, w_t, *exch.sources)
    return res[0], res[1:]


def _poolconv_fwd(proj, wp, scale, dw, cb, lng, lnb, pw):
    s_len = proj.shape[0]
    ts = _row_tile(s_len, ROW_TILE)
    hb = ts // HALO

    def body(p_ref, ph_ref, wp_ref, sc_ref, dw_ref, cb_ref, lng_ref, lnb_ref, pw_ref, y_ref, cv_ref):
        i = pl.program_id(0)
        cur = p_ref[...]
        halo = jnp.where(i > 0, ph_ref[...], 0.0)
        ext = jnp.concatenate([halo, cur], axis=0)
        diff = _pool_diff(ext[:, 0:256], i * ts, ts)
        gp = cur[:, 256:512]
        y_pool = _mm(diff, wp_ref[...]) * sc_ref[...] * (gp * _sig(gp))
        hh = ext[:, 512:768] * _sig(ext[:, 768:1024])
        cv = _causal_conv(_shifted_down(hh), dw_ref, HALO, ts) + cb_ref[...]
        cv_ref[...] = cv
        n, _ = _layer_norm(cv)
        z = n * lng_ref[...] + lnb_ref[...]
        gc = cur[:, 1024:1280]
        y_conv = _mm(z * _sig(z), pw_ref[...]) * (gc * _sig(gc))
        y_ref[...] = jnp.concatenate([y_pool, y_conv], axis=-1).astype(y_ref.dtype)

    vec = _full((1, 256))
    return pl.pallas_call(
        body, name="poolconv_fwd", grid=(s_len // ts,),
        in_specs=[pl.BlockSpec((ts, HALF_IN), lambda i: (i, 0)),
                  pl.BlockSpec((HALO, HALF_IN), lambda i: (jnp.maximum(i * hb - 1, 0), 0)),
                  _full((256, 256)), vec, _full((CONV_TAPS_PAD, 256)), vec, vec, vec, _full((256, 256))],
        out_specs=[pl.BlockSpec((ts, 512), lambda i: (i, 0)), pl.BlockSpec((ts, 256), lambda i: (i, 0))],
        out_shape=[jax.ShapeDtypeStruct((s_len, 512), MXU_DTYPE), jax.ShapeDtypeStruct((s_len, 256), F32)],
        compiler_params=_params(1),
    )(proj, proj, wp, scale, dw, cb, lng, lnb, pw)


def _attn_fwd(proj, sinks, carried=()):
    s_len = proj.shape[0]
    qb = _attn_blocks_per_step(s_len)
    ts = qb * BLOCK
    nt = s_len // ts
    exch = _Exchange(carried)

    def body(*refs):
        (p_ref, kvp_ref, sinks_ref), (y_ref,), (bias_ref,), xrefs = _split_refs(refs, 3, 1, exch)
        i = pl.program_id(0)
        exch.run(xrefs, i == 0, i == nt - 1)

        @pl.when(i == 0)
        def _():
            _fill_attn_bias(bias_ref)

        for b in range(qb):
            r0 = BLOCK * b
            q = p_ref[r0:r0 + BLOCK, 0:512]
            ga = p_ref[r0:r0 + BLOCK, 768:1280]
            kv_prev = kvp_ref[...] if b == 0 else p_ref[r0 - BLOCK:r0, 512:768]
            kv2 = jnp.concatenate([kv_prev, p_ref[r0:r0 + BLOCK, 512:768]], axis=0)
            k2, v2 = kv2[:, :BLOCK], kv2[:, BLOCK:]
            k2_swapped, v2_swapped = pltpu.roll(k2, HEAD_DIM, 1), pltpu.roll(v2, HEAD_DIM, 1)
            variant = jnp.where(i == 0, 0, 1) if b == 0 else 1
            outs = []
            for kh in range(N_KV_HEADS):
                k_bd = _pair_block_matrix(k2, k2_swapped, kh)
                v_bd = _pair_block_matrix(v2, v2_swapped, kh)
                s = _mm(_pair_queries(q, kh), k_bd, _NT) + bias_ref[variant, kh]
                p, _ = _pair_softmax(s, kh, sinks_ref)
                outs.append(_mm(p, v_bd))
            y_ref[r0:r0 + BLOCK, :] = (_pair_unstack(outs) * (ga * _sig(ga))).astype(y_ref.dtype)

    res = pl.pallas_call(
        body, name="attn_fwd_carrier" if exch.n else "attn_fwd", grid=(nt,),
        in_specs=[pl.BlockSpec((ts, HALF_IN), lambda i: (i, 1)),
                  pl.BlockSpec((BLOCK, 256), lambda i: (jnp.maximum(i * qb - 1, 0), 7)),
                  pl.BlockSpec(memory_space=pltpu.SMEM)] + [_ANY] * exch.n,
        out_specs=[pl.BlockSpec((ts, 512), lambda i: (i, 0))] + [_ANY] * exch.n,
        out_shape=[jax.ShapeDtypeStruct((s_len, 512), MXU_DTYPE)] + exch.out_shapes(),
        scratch_shapes=[pltpu.VMEM((2, N_KV_HEADS, PAIR_ROWS, PAIR_COLS), F32)] + exch.scratch(),
        compiler_params=_params(1),
    )(proj, proj, sinks, *exch.sources)
    return res[0], res[1:]


def _out_proj(x, y_pc, y_at, w_out):
    s_len, d = x.shape
    ts = _row_tile(s_len, ROW_TILE)

    def body(x_ref, a_ref, b_ref, w_ref, o_ref):
        y = jnp.concatenate([a_ref[...], b_ref[...]], axis=-1)
        o_ref[...] = x_ref[...] + _mm(y, w_ref[...])

    return pl.pallas_call(
        body, name="out_proj", grid=(s_len // ts,),
        in_specs=[pl.BlockSpec((ts, d), lambda i: (i, 0)), pl.BlockSpec((ts, 512), lambda i: (i, 0)),
                  pl.BlockSpec((ts, 512), lambda i: (i, 0)), _full((D_MIX, d))],
        out_specs=pl.BlockSpec((ts, d), lambda i: (i, 0)),
        out_shape=jax.ShapeDtypeStruct((s_len, d), F32),
        compiler_params=_params(1),
    )(x, y_pc, y_at, w_out)


def _loss_bwd(x, target, g):
    s_len, d = x.shape
    ts = _row_tile(s_len, ROW_TILE)

    def body(x_ref, t_ref, g_ref, sq_ref, dg_ref, dx_ref):
        i = pl.program_id(0)
        xv = x_ref[...]
        gv = g_ref[...]
        r = lax.rsqrt(jnp.mean(xv * xv, axis=-1, keepdims=True) + EPS)
        xr = xv * r
        err = xr * gv - t_ref[...]
        dout = err * (1.0 / d)
        w = dout * gv
        dx_ref[...] = r * (w - xr * jnp.mean(w * xr, axis=-1, keepdims=True))

        @pl.when(i == 0)
        def _():
            sq_ref[...] = jnp.zeros_like(sq_ref)
            dg_ref[...] = jnp.zeros_like(dg_ref)

        sq = jnp.sum(jnp.sum(err * err, axis=-1, keepdims=True), axis=0, keepdims=True)
        sq_ref[...] += jnp.broadcast_to(sq, sq_ref.shape)
        dg_ref[...] += jnp.sum(dout * xr, axis=0, keepdims=True)

    return pl.pallas_call(
        body, name="loss_bwd", grid=(s_len // ts,),
        in_specs=[pl.BlockSpec((ts, d), lambda i: (i, 0)), pl.BlockSpec((ts, d), lambda i: (i, 0)), _full((1, d))],
        out_specs=[_full((1, 128)), _full((1, d)), pl.BlockSpec((ts, d), lambda i: (i, 0))],
        out_shape=[jax.ShapeDtypeStruct((1, 128), F32), jax.ShapeDtypeStruct((1, d), F32),
                   jax.ShapeDtypeStruct((s_len, d), F32)],
        compiler_params=_params(1),
    )(x, target, g)


def _out_bwd(dxo, y_pc, y_at, w_out, carried=()):
    s_len, d = dxo.shape
    ts = _row_tile(s_len, ROW_TILE)
    nt = s_len // ts
    exch = _Exchange(carried)

    def body(*refs):
        (dx_ref, a_ref, b_ref, w_ref), (dy_ref, gw_ref), (acc_ref,), xrefs = _split_refs(refs, 4, 2, exch)
        i = pl.program_id(0)
        exch.run(xrefs, i == 0, i == nt - 1)
        dxv = dx_ref[...].astype(MXU_DTYPE)
        dy_ref[...] = _mm(dxv, w_ref[...], _NT)

        @pl.when(i == 0)
        def _():
            acc_ref[...] = jnp.zeros_like(acc_ref)

        y = jnp.concatenate([a_ref[...], b_ref[...]], axis=-1)
        acc_ref[...] += _mm(y, dxv, _TN)

        @pl.when(i == nt - 1)
        def _():
            gw_ref[...] = acc_ref[...].astype(gw_ref.dtype)

    res = pl.pallas_call(
        body, name="out_bwd_carrier" if exch.n else "out_bwd", grid=(nt,),
        in_specs=[pl.BlockSpec((ts, d), lambda i: (i, 0)), pl.BlockSpec((ts, 512), lambda i: (i, 0)),
                  pl.BlockSpec((ts, 512), lambda i: (i, 0)), _full((D_MIX, d))] + [_ANY] * exch.n,
        out_specs=[pl.BlockSpec((ts, D_MIX), lambda i: (i, 0)), _full((D_MIX, d))] + [_ANY] * exch.n,
        out_shape=[jax.ShapeDtypeStruct((s_len, D_MIX), F32), jax.ShapeDtypeStruct((D_MIX, d), EXCHANGE_DTYPE)]
        + exch.out_shapes(),
        scratch_shapes=[pltpu.VMEM((D_MIX, d), F32)] + exch.scratch(),
        compiler_params=_params(1),
    )(dxo, y_pc, y_at, w_out, *exch.sources)
    return res[:2], res[2:]


def _poolconv_bwd(proj, cv, dy, wp, scale, dw, lng, lnb, pw, carried=()):
    s_len = proj.shape[0]
    ts = _row_tile(s_len, ROW_TILE)
    hb = ts // HALO
    nt = s_len // ts
    last_halo = s_len // HALO - 1
    n = ts + HALO
    exch = _Exchange(carried)

    def body(*refs):
        ins, outs, _, xrefs = _split_refs(refs, 13, 5, exch)
        (p_ref, ph_ref, pn_ref, cv_ref, cvn_ref, dy_ref, dyn_ref, wp_ref, sc_ref, dw_ref, lng_ref, lnb_ref,
         pw_ref) = ins
        da_ref, gwp_ref, gpw_ref, gdw_ref, gvec_ref = outs
        i = pl.program_id(0)
        exch.run(xrefs, i == 0, i == nt - 1)
        has_next = i < nt - 1
        cur = p_ref[...]
        prev = jnp.where(i > 0, ph_ref[...], 0.0)
        nxt = jnp.where(has_next, pn_ref[...], 0.0)
        dyx = jnp.concatenate([dy_ref[...], jnp.where(has_next, dyn_ref[...], 0.0)], axis=0)
        row = lax.broadcasted_iota(jnp.int32, (n, 1), 0)
        in_seq = (row < ts) | has_next
        scale_v = sc_ref[...]

        cvx = jnp.concatenate([cv_ref[...], jnp.where(has_next, cvn_ref[...], 0.0)], axis=0)
        nrm, rstd = _layer_norm(cvx)
        z = nrm * lng_ref[...] + lnb_ref[...]
        sz = _sig(z)
        sw = z * sz
        gc = jnp.concatenate([cur[:, 1024:1280], nxt[:, 1024:1280]], axis=0)
        sgc = _sig(gc)
        yc = _mm(sw, pw_ref[...])
        dyc = dyx[:, 256:512]
        d_yc = dyc * (gc * sgc)
        d_gc = (dyc * yc * _dsilu(gc, sgc))[:ts]
        d_z = _mm(d_yc, pw_ref[...], _NT) * _dsilu(z, sz)
        d_n = d_z * lng_ref[...]
        d_cv = rstd * (d_n - jnp.mean(d_n, axis=-1, keepdims=True)
                       - nrm * jnp.mean(d_n * nrm, axis=-1, keepdims=True))
        d_cv = jnp.where(in_seq, d_cv, 0.0)
        d_hh = _anticausal_conv(d_cv, dw_ref, ts)
        a2 = jnp.concatenate([prev[:, 512:768], cur[:, 512:768]], axis=0)
        sb2 = _sig(jnp.concatenate([prev[:, 768:1024], cur[:, 768:1024]], axis=0))
        hh_rolled = _shifted_down(a2 * sb2)
        a_c, sb_c = a2[HALO:], sb2[HALO:]
        d_a = d_hh * sb_c
        d_b = d_hh * a_c * sb_c * (1.0 - sb_c)
        d_cv_t = d_cv[:ts]
        g_dw_rows = [None] * CONV_TAPS_PAD
        for r in range(8):
            for m in range(4):
                d = 8 * m + r
                if d < CONV_KERNEL:
                    g_dw_rows[CONV_KERNEL - 1 - d] = jnp.sum(
                        d_cv_t * hh_rolled[r][HALO - 8 * m: HALO - 8 * m + ts], axis=0, keepdims=True)
        g_dw_rows[CONV_KERNEL] = jnp.zeros((1, CONV_WIDTH), F32)

        u_ext = jnp.concatenate([prev[:, 0:256], cur[:, 0:256]], axis=0)
        diff = _pool_diff(u_ext, i * ts, ts)
        raw = _mm(diff, wp_ref[...])
        gp = jnp.concatenate([cur[:, 256:512], nxt[:, 256:512]], axis=0)
        sgp = _sig(gp)
        dyp = dyx[:, 0:256]
        d_yp = dyp * (gp * sgp)
        d_gp = dyp[:ts] * (raw * scale_v) * _dsilu(gp, sgp)[:ts]
        d_raw = d_yp * scale_v
        d_diff = _mm(d_raw, wp_ref[...], _NT)
        w = jnp.where(in_seq, d_diff / _pool_count(i * ts, n), 0.0)
        d_u = _pool_diff_bwd(w, ts) - d_diff[:ts]

        da_ref[...] = jnp.concatenate([d_u, d_gp, d_a, d_b, d_gc], axis=-1).astype(da_ref.dtype)

        @pl.when(i == 0)
        def _():
            gwp_ref[...] = jnp.zeros_like(gwp_ref)
            gpw_ref[...] = jnp.zeros_like(gpw_ref)
            gdw_ref[...] = jnp.zeros_like(gdw_ref)
            gvec_ref[...] = jnp.zeros_like(gvec_ref)

        gwp_ref[...] += _mm(diff, d_raw[:ts], _TN)
        gpw_ref[...] += _mm(sw[:ts], d_yc[:ts], _TN)
        gdw_ref[...] += jnp.concatenate(g_dw_rows, axis=0)
        zero_row = jnp.zeros((1, 256), F32)
        gvec_ref[...] += jnp.concatenate([
            jnp.sum(d_yp[:ts] * raw, axis=0, keepdims=True),
            jnp.sum(d_cv_t, axis=0, keepdims=True),
            jnp.sum((d_z * nrm)[:ts], axis=0, keepdims=True),
            jnp.sum(d_z[:ts], axis=0, keepdims=True),
            zero_row, zero_row, zero_row, zero_row], axis=0)

    vec = _full((1, 256))
    nxt_halo = lambda i: (jnp.minimum((i + 1) * hb, last_halo), 0)
    res = pl.pallas_call(
        body, name="poolconv_bwd_carrier" if exch.n else "poolconv_bwd", grid=(nt,),
        in_specs=[pl.BlockSpec((ts, HALF_IN), lambda i: (i, 0)),
                  pl.BlockSpec((HALO, HALF_IN), lambda i: (jnp.maximum(i * hb - 1, 0), 0)),
                  pl.BlockSpec((HALO, HALF_IN), nxt_halo),
                  pl.BlockSpec((ts, 256), lambda i: (i, 0)), pl.BlockSpec((HALO, 256), nxt_halo),
                  pl.BlockSpec((ts, 512), lambda i: (i, 0)), pl.BlockSpec((HALO, 512), nxt_halo),
                  _full((256, 256)), vec, _full((CONV_TAPS_PAD, 256)), vec, vec, _full((256, 256))]
        + [_ANY] * exch.n,
        out_specs=[pl.BlockSpec((ts, HALF_IN), lambda i: (i, 0)), _full((256, 256)), _full((256, 256)),
                   _full((CONV_TAPS_PAD, 256)), _full((8, 256))] + [_ANY] * exch.n,
        out_shape=[jax.ShapeDtypeStruct((s_len, HALF_IN), MXU_DTYPE), jax.ShapeDtypeStruct((256, 256), F32),
                   jax.ShapeDtypeStruct((256, 256), F32), jax.ShapeDtypeStruct((CONV_TAPS_PAD, 256), F32),
                   jax.ShapeDtypeStruct((8, 256), F32)] + exch.out_shapes(),
        scratch_shapes=exch.scratch(),
        compiler_params=_params(1),
    )(proj, proj, proj, cv, cv, dy, dy, wp, scale, dw, lng, lnb, pw, *exch.sources)
    return res[:5], res[5:]


DQ0, DKC0, DVC0, DKP0, DVP0, DGA0, DATTN_W = 0, 512, 640, 768, 896, 1024, 1536


def _attn_bwd(proj, dy, sinks, carried=()):
    s_len = proj.shape[0]
    qb = _attn_blocks_per_step(s_len)
    ts = qb * BLOCK
    nt = s_len // ts
    exch = _Exchange(carried)

    def body(*refs):
        (p_ref, kvp_ref, dy_ref, sinks_ref), (o_ref, gs_ref), (bias_ref,), xrefs = _split_refs(refs, 4, 2, exch)
        i = pl.program_id(0)
        exch.run(xrefs, i == 0, i == nt - 1)

        @pl.when(i == 0)
        def _():
            _fill_attn_bias(bias_ref)
            gs_ref[...] = jnp.zeros_like(gs_ref)

        lo = lax.broadcasted_iota(jnp.int32, (1, 2 * HEAD_DIM), 1) < HEAD_DIM
        d_sinks = [None] * N_Q_HEADS
        for b in range(qb):
            r0 = BLOCK * b
            q = p_ref[r0:r0 + BLOCK, 0:512]
            ga = p_ref[r0:r0 + BLOCK, 768:1280]
            dya = dy_ref[r0:r0 + BLOCK, :]
            kv_prev = kvp_ref[...] if b == 0 else p_ref[r0 - BLOCK:r0, 512:768]
            kv2 = jnp.concatenate([kv_prev, p_ref[r0:r0 + BLOCK, 512:768]], axis=0)
            k2, v2 = kv2[:, :BLOCK], kv2[:, BLOCK:]
            k2_swapped, v2_swapped = pltpu.roll(k2, HEAD_DIM, 1), pltpu.roll(v2, HEAD_DIM, 1)
            variant = jnp.where(i == 0, 0, 1) if b == 0 else 1
            sga = _sig(ga)
            d_o = dya * (ga * sga)
            outs, dq_parts, dk_folded, dv_folded = [], [], [], []
            for kh in range(N_KV_HEADS):
                k_bd = _pair_block_matrix(k2, k2_swapped, kh)
                v_bd = _pair_block_matrix(v2, v2_swapped, kh)
                q2 = _pair_queries(q, kh)
                s = _mm(q2, k_bd, _NT) + bias_ref[variant, kh]
                p, p_sinks = _pair_softmax(s, kh, sinks_ref)
                outs.append(_mm(p, v_bd))
                do2 = _pair_stack(d_o, kh)
                dp = _mm(do2, v_bd, _NT)
                ds_halves = []
                for a in range(2):
                    cols = slice(2 * BLOCK * a, 2 * BLOCK * (a + 1))
                    delta = jnp.sum(p[:, cols] * dp[:, cols], axis=-1, keepdims=True)
                    ds_halves.append(p[:, cols] * (dp[:, cols] - delta))
                    dsink = -p_sinks[a] * delta
                    for j in range(2):
                        part = jnp.sum(dsink[BLOCK * j: BLOCK * (j + 1)], axis=0, keepdims=True)
                        h = Q_PER_KV * kh + 2 * j + a
                        d_sinks[h] = part if d_sinks[h] is None else d_sinks[h] + part
                ds = jnp.concatenate(ds_halves, axis=-1)
                dq_parts.append(_mm(ds, k_bd) * SCALE)
                dk_folded.append(_fold_pair_halves(_mm(ds, q2, _TN)))
                dv_folded.append(_fold_pair_halves(_mm(p, do2, _TN)))
            dk = jnp.where(lo, dk_folded[0], dk_folded[1])
            dv = jnp.where(lo, dv_folded[0], dv_folded[1])
            d_ga = dya * _pair_unstack(outs) * _dsilu(ga, sga)
            o_ref[r0:r0 + BLOCK, :] = jnp.concatenate(
                [_pair_unstack(dq_parts), dk[BLOCK:], dv[BLOCK:], dk[:BLOCK], dv[:BLOCK], d_ga], axis=-1)
        gs_ref[...] += jnp.broadcast_to(jnp.concatenate(d_sinks, axis=0), gs_ref.shape)

    res = pl.pallas_call(
        body, name="attn_bwd_carrier" if exch.n else "attn_bwd", grid=(nt,),
        in_specs=[pl.BlockSpec((ts, HALF_IN), lambda i: (i, 1)),
                  pl.BlockSpec((BLOCK, 256), lambda i: (jnp.maximum(i * qb - 1, 0), 7)),
                  pl.BlockSpec((ts, 512), lambda i: (i, 1)),
                  pl.BlockSpec(memory_space=pltpu.SMEM)] + [_ANY] * exch.n,
        out_specs=[pl.BlockSpec((ts, DATTN_W), lambda i: (i, 0)), _full((N_Q_HEADS, 128))] + [_ANY] * exch.n,
        out_shape=[jax.ShapeDtypeStruct((s_len, DATTN_W), F32), jax.ShapeDtypeStruct((N_Q_HEADS, 128), F32)]
        + exch.out_shapes(),
        scratch_shapes=[pltpu.VMEM((2, N_KV_HEADS, PAIR_ROWS, PAIR_COLS), F32)] + exch.scratch(),
        compiler_params=_params(1),
    )(proj, proj, dy, sinks, *exch.sources)
    return res[:2], res[2:]


def _in_bwd(da, dattn, x, dxo, g, w_t, carried=()):
    s_len, d = x.shape
    ts = _row_tile(s_len, IN_BWD_TILE)
    bpt = ts // BLOCK
    nt = s_len // ts
    last_block = s_len // BLOCK - 1
    exch = _Exchange(carried)

    def body(*refs):
        ins, (dx_ref, dg_ref, gw_ref), (acc_ref,), xrefs = _split_refs(refs, 7, 3, exch)
        da_ref, dat_ref, nxt_ref, x_ref, dxo_ref, g_ref, w_ref = ins
        i = pl.program_id(0)
        exch.run(xrefs, i == 0, i == nt - 1)
        dat = dat_ref[...]
        nxt = jnp.where(i < nt - 1, nxt_ref[...], 0.0)
        shifted = jnp.concatenate([dat[BLOCK:, DKP0:DGA0], nxt], axis=0) if bpt > 1 else nxt
        dkv = dat[:, DKC0:DKP0] + shifted
        dproj = jnp.concatenate([da_ref[...], dat[:, DQ0:DKC0].astype(MXU_DTYPE), dkv.astype(MXU_DTYPE),
                                 dat[:, DGA0:DATTN_W].astype(MXU_DTYPE)], axis=-1)
        d_h = _mm(dproj, w_ref[...])
        xv = x_ref[...]
        gv = g_ref[...]
        r = lax.rsqrt(jnp.mean(xv * xv, axis=-1, keepdims=True) + EPS)
        xr = xv * r
        w = d_h * gv
        dx_ref[...] = dxo_ref[...] + r * (w - xr * jnp.mean(w * xr, axis=-1, keepdims=True))

        @pl.when(i == 0)
        def _():
            dg_ref[...] = jnp.zeros_like(dg_ref)
            acc_ref[...] = jnp.zeros_like(acc_ref)

        dg_ref[...] += jnp.sum(d_h * xr, axis=0, keepdims=True)
        acc_ref[...] += _mm(dproj, xr * gv, _TN)

        @pl.when(i == nt - 1)
        def _():
            gw_ref[...] = acc_ref[...].astype(gw_ref.dtype)

    res = pl.pallas_call(
        body, name="in_bwd_carrier" if exch.n else "in_bwd", grid=(nt,),
        in_specs=[pl.BlockSpec((ts, HALF_IN), lambda i: (i, 0)),
                  pl.BlockSpec((ts, DATTN_W), lambda i: (i, 0)),
                  pl.BlockSpec((BLOCK, 256), lambda i: (jnp.minimum((i + 1) * bpt, last_block), 3)),
                  pl.BlockSpec((ts, d), lambda i: (i, 0)), pl.BlockSpec((ts, d), lambda i: (i, 0)),
                  _full((1, d)), _full((D_IN, d))] + [_ANY] * exch.n,
        out_specs=[pl.BlockSpec((ts, d), lambda i: (i, 0)), _full((1, d)), _full((D_IN, d))] + [_ANY] * exch.n,
        out_shape=[jax.ShapeDtypeStruct((s_len, d), F32), jax.ShapeDtypeStruct((1, d), F32),
                   jax.ShapeDtypeStruct((D_IN, d), EXCHANGE_DTYPE)] + exch.out_shapes(),
        scratch_shapes=[pltpu.VMEM((D_IN, d), F32)] + exch.scratch(),
        compiler_params=_params(1),
    )(da, dattn, dattn, x, dxo, g, w_t, *exch.sources)
    return res[:3], res[3:]


def _sum_partials(p_ref):
    g = p_ref[0].astype(F32)
    for k in range(1, N_DEV):
        g = g + p_ref[k].astype(F32)
    return g


def _adamw_step(g, w, m, v):
    nm = ADAM_B1 * m + (1.0 - ADAM_B1) * g
    nv = ADAM_B2 * v + (1.0 - ADAM_B2) * (g * g)
    m_hat = nm / (1.0 - ADAM_B1 ** ADAM_STEP)
    v_hat = nv / (1.0 - ADAM_B2 ** ADAM_STEP)
    return -ADAM_LR * (m_hat / (jnp.sqrt(v_hat) + ADAM_EPS) + ADAM_WD * w), nm, nv


def _adamw_layers(parts, w, m, v, transposed, name):
    def body(*refs):
        p_refs = refs[:DEPTH]
        w_ref, m_ref, v_ref, g_ref, d_ref, nm_ref, nv_ref = refs[DEPTH:]
        for l in range(DEPTH):
            g = _sum_partials(p_refs[l])
            if transposed:
                g = g.T
            g_ref[l] = g
            d_ref[l], nm_ref[l], nv_ref[l] = _adamw_step(g, w_ref[l], m_ref[l], v_ref[l])

    vmem = pl.BlockSpec(memory_space=pltpu.VMEM)
    shape = jax.ShapeDtypeStruct(w.shape, F32)
    return pl.pallas_call(
        body, name=name, in_specs=[vmem] * (DEPTH + 3), out_specs=[vmem] * 4, out_shape=[shape] * 4,
        compiler_params=pltpu.CompilerParams(vmem_limit_bytes=VMEM_LIMIT),
    )(*parts, w, m, v)


def _adamw(parts, w, m, v, name):
    rows, n = w.shape
    tr = _row_tile(rows, 256)

    def body(p_ref, w_ref, m_ref, v_ref, g_ref, d_ref, nm_ref, nv_ref):
        g = _sum_partials(p_ref)
        g_ref[...] = g
        d_ref[...], nm_ref[...], nv_ref[...] = _adamw_step(g, w_ref[...], m_ref[...], v_ref[...])

    tile = pl.BlockSpec((tr, n), lambda i: (i, 0))
    shape = jax.ShapeDtypeStruct((rows, n), F32)
    return pl.pallas_call(
        body, name=name, grid=(rows // tr,),
        in_specs=[pl.BlockSpec((N_DEV, tr, n), lambda i: (0, i, 0)), tile, tile, tile],
        out_specs=[tile, tile, tile, tile],
        out_shape=[shape, shape, shape, shape],
        compiler_params=_params(1),
    )(parts, w, m, v)


def _pad_rows(a, mult):
    pad = (-a.shape[0]) % mult
    return a if pad == 0 else jnp.concatenate([a, jnp.zeros((pad, a.shape[1]), a.dtype)], axis=0)


def _dw_rows(conv_dw_l):
    return jnp.pad(jnp.swapaxes(conv_dw_l, 0, 1), ((0, 0), (0, CONV_TAPS_PAD - CONV_KERNEL)))


def _pack_small(pw_l, dw_l, d):
    rows = jnp.concatenate([pw_l.reshape(-1, d), _dw_rows(dw_l).reshape(-1, d)], axis=0)
    return _pad_rows(rows, 8)


def _small_slabs(g_pw, g_dw, d):
    a = g_pw.reshape(N_DEV, -1, d)
    b = jnp.swapaxes(g_dw, 0, 1).reshape(N_DEV, -1, d)
    used = a.shape[1] + b.shape[1]
    return jnp.concatenate([a, b, jnp.zeros((N_DEV, (-used) % 8, d), g_pw.dtype)], axis=1)


def _unpack_small(rows, d):
    c = CONV_WIDTH // N_DEV
    n_pw = c * CONV_WIDTH // d
    n_dw = c * CONV_TAPS_PAD // d
    pw = rows[:n_pw].reshape(c, CONV_WIDTH)
    dw = jnp.swapaxes(rows[n_pw:n_pw + n_dw].reshape(c, CONV_TAPS_PAD), 0, 1)[:CONV_KERNEL]
    return pw, dw


def _pack_replicated(ln_g, pool_w, pool_scale, conv_b, conv_ln_g, conv_ln_b, attn_sinks, final_g, scalar, d):
    sinks = jnp.pad(attn_sinks, ((0, 0), (0, 256 - N_Q_HEADS)))
    small = jnp.concatenate([pool_scale, conv_b, conv_ln_g, conv_ln_b, sinks], axis=0)
    small = _pad_rows(small, d // 256)
    last = jnp.pad(scalar.reshape(1, 1), ((0, 0), (0, d - 1)))
    return _pad_rows(jnp.concatenate([ln_g.reshape(-1, d), final_g.reshape(-1, d), pool_w.reshape(-1, d),
                                      small.reshape(-1, d), last], axis=0), 8)


def _unpack_replicated(rows, d):
    n_pool = DEPTH * 4 * POOL_GROUP * POOL_GROUP // d
    n_small = -(-5 * DEPTH * 256 // d)
    ln_g = rows[:DEPTH]
    final_g = rows[DEPTH]
    pool_w = rows[DEPTH + 1: DEPTH + 1 + n_pool].reshape(DEPTH, 4, POOL_GROUP, POOL_GROUP)
    small = rows[DEPTH + 1 + n_pool: DEPTH + 1 + n_pool + n_small].reshape(-1, 256)[: 5 * DEPTH]
    pool_scale, conv_b, conv_ln_g, conv_ln_b = (small[DEPTH * k: DEPTH * (k + 1)] for k in range(4))
    sinks = small[4 * DEPTH: 5 * DEPTH, :N_Q_HEADS]
    scalar = rows[DEPTH + 1 + n_pool + n_small, 0]
    return ln_g, pool_w, pool_scale, conv_b, conv_ln_g, conv_ln_b, sinks, final_g, scalar


def _block_diag(pool_w):
    out = jnp.zeros((POOL_WIDTH, POOL_WIDTH), pool_w.dtype)
    for gi in range(4):
        out = out.at[POOL_GROUP * gi: POOL_GROUP * (gi + 1), POOL_GROUP * gi: POOL_GROUP * (gi + 1)].set(pool_w[gi])
    return out


def _diag_blocks(mat):
    return jnp.stack([mat[POOL_GROUP * gi: POOL_GROUP * (gi + 1), POOL_GROUP * gi: POOL_GROUP * (gi + 1)]
                      for gi in range(4)], axis=0)


def kernel(x, ln_g, w_in, pool_w, pool_scale, conv_dw, conv_b, conv_ln_g, conv_ln_b, conv_pw, attn_sinks, w_out, final_g, loss_target, m_ln_g, m_w_in, m_pool_w, m_pool_scale, m_conv_dw, m_conv_b, m_conv_ln_g, m_conv_ln_b, m_conv_pw, m_attn_sinks, m_w_out, m_final_g, v_ln_g, v_w_in, v_pool_w, v_pool_scale, v_conv_dw, v_conv_b, v_conv_ln_g, v_conv_ln_b, v_conv_pw, v_attn_sinks, v_w_out, v_final_g):
    x0 = x[0]
    d = x0.shape[1]
    row = lambda a: a.reshape(1, -1)
    slabs = lambda a: a.reshape(N_DEV, a.shape[0] // N_DEV, d)
    c_shard = CONV_WIDTH // N_DEV

    w_in_rows = [jnp.swapaxes(w_in[l], 0, 1).astype(MXU_DTYPE) for l in range(DEPTH)]
    w_out_rows = [w_out[l].astype(MXU_DTYPE) for l in range(DEPTH)]
    per_word = 4 // jnp.dtype(MXU_DTYPE).itemsize
    dw_t = jnp.stack([_dw_rows(conv_dw[l]) for l in range(DEPTH)], axis=0)
    dw_bits = (lax.bitcast_convert_type(dw_t, MXU_DTYPE) if per_word > 1 else dw_t).reshape(-1, d)
    n_pw = DEPTH * c_shard * CONV_WIDTH // d
    conv_rows = _pad_rows(jnp.concatenate([conv_pw.reshape(-1, d).astype(MXU_DTYPE), dw_bits], axis=0), 16)
    w_in_t, w_out_f = [None] * DEPTH, [None] * DEPTH
    w_in_t[0] = _all_gather(w_in_rows[0], "w_in_all_gather")
    wp_bd = [_block_diag(pool_w[l]).astype(MXU_DTYPE) for l in range(DEPTH)]

    xs, projs, cvs, ys = [x0], [], [], []
    for l in range(DEPTH):
        proj, got = _in_proj(xs[l], row(ln_g[l]), w_in_t[l], carried=[w_out_rows[0], conv_rows] if l == 0 else [])
        if l == 0:
            w_out_f[0] = got[0].reshape(D_MIX, d)
            pw_all = got[1][:, :n_pw].reshape(N_DEV, DEPTH, c_shard, CONV_WIDTH)
            pw_f = [pw_all[:, k].reshape(CONV_WIDTH, CONV_WIDTH) for k in range(DEPTH)]
            bits = got[1][:, n_pw:n_pw + dw_bits.shape[0]].reshape(
                (N_DEV, DEPTH, c_shard, CONV_TAPS_PAD) + (per_word,) * (per_word > 1))
            dw_all = lax.bitcast_convert_type(bits, F32) if per_word > 1 else bits
            dw_f = [jnp.swapaxes(dw_all[:, k].reshape(CONV_WIDTH, CONV_TAPS_PAD), 0, 1) for k in range(DEPTH)]
        y_pc, cv = _poolconv_fwd(proj, wp_bd[l], row(pool_scale[l]), dw_f[l], row(conv_b[l]), row(conv_ln_g[l]),
                                 row(conv_ln_b[l]), pw_f[l])
        y_at, got = _attn_fwd(proj, attn_sinks[l], carried=[w_in_rows[1], w_out_rows[1]] if l == 0 else [])
        if l == 0:
            w_in_t[1], w_out_f[1] = got[0].reshape(D_IN, d), got[1].reshape(D_MIX, d)
        xs.append(_out_proj(xs[l], y_pc, y_at, w_out_f[l]))
        projs.append(proj)
        cvs.append(cv)
        ys.append((y_pc, y_at))

    sq, g_final, dx = _loss_bwd(xs[DEPTH], loss_target[0], row(final_g))

    l = 1
    (dy, g_wout1), _ = _out_bwd(dx, ys[l][0], ys[l][1], w_out_f[l])
    (da, g_wp1, g_pw1, g_dw1, g_vec1), _ = _poolconv_bwd(
        projs[l], cvs[l], dy, wp_bd[l], row(pool_scale[l]), dw_f[l], row(conv_ln_g[l]), row(conv_ln_b[l]), pw_f[l])
    (dattn, gs1), _ = _attn_bwd(projs[l], dy, attn_sinks[l])
    (dx, g_ln1, g_win_t1), _ = _in_bwd(da, dattn, xs[l], dx, row(ln_g[l]), w_in_t[l])
    l = 0
    (dy, g_wout0), (r_wout1, r_small1) = _out_bwd(dx, ys[l][0], ys[l][1], w_out_f[l],
                                                  carried=[slabs(g_wout1), _small_slabs(g_pw1, g_dw1, d)])
    (da, g_wp0, g_pw0, g_dw0, g_vec0), (r_win1,) = _poolconv_bwd(
        projs[l], cvs[l], dy, wp_bd[l], row(pool_scale[l]), dw_f[l], row(conv_ln_g[l]), row(conv_ln_b[l]), pw_f[l],
        carried=[slabs(g_win_t1)])
    (dattn, gs0), (r_wout0, r_small0) = _attn_bwd(projs[l], dy, attn_sinks[l],
                                                  carried=[slabs(g_wout0), _small_slabs(g_pw0, g_dw0, d)])
    (dx, g_ln0, g_win_t0), _ = _in_bwd(da, dattn, xs[l], dx, row(ln_g[l]), w_in_t[l])
    grad_x = dx[None]

    gv = jnp.stack([g_vec0, g_vec1], axis=0)
    rep_part = _pack_replicated(
        jnp.concatenate([g_ln0, g_ln1], axis=0), jnp.stack([_diag_blocks(g_wp0), _diag_blocks(g_wp1)], axis=0),
        gv[:, 0], gv[:, 1], gv[:, 2], gv[:, 3], jnp.stack([gs0[:, 0], gs1[:, 0]], axis=0), g_final, sq[0, 0], d)
    r_win0, r_rep = _final_exchange([slabs(g_win_t0), rep_part])

    win = _adamw_layers([r_win0, r_win1], w_in, m_w_in, v_w_in, True, "adamw_w_in")
    wout = _adamw_layers([r_wout0, r_wout1], w_out, m_w_out, v_w_out, False, "adamw_w_out")
    small = [[_unpack_small(o, d) for o in _adamw(
        r, _pack_small(conv_pw[l], conv_dw[l], d), _pack_small(m_conv_pw[l], m_conv_dw[l], d),
        _pack_small(v_conv_pw[l], v_conv_dw[l], d), f"adamw_conv_{l}")] for l, r in enumerate([r_small0, r_small1])]
    zero = jnp.zeros((), F32)
    pack_r = lambda *a: _pack_replicated(*a, zero, d)
    rep = [_unpack_replicated(o, d) for o in _adamw(
        r_rep, pack_r(ln_g, pool_w, pool_scale, conv_b, conv_ln_g, conv_ln_b, attn_sinks, final_g),
        pack_r(m_ln_g, m_pool_w, m_pool_scale, m_conv_b, m_conv_ln_g, m_conv_ln_b, m_attn_sinks, m_final_g),
        pack_r(v_ln_g, v_pool_w, v_pool_scale, v_conv_b, v_conv_ln_g, v_conv_ln_b, v_attn_sinks, v_final_g),
        "adamw_replicated")]
    loss = 0.5 / d * rep[0][-1]

    outs = []
    for k in range(4):
        r_ln, r_pool, r_scale, r_cb, r_lng, r_lnb, r_sinks, r_final, _ = rep[k]
        s_pw = jnp.stack([small[l][k][0] for l in range(DEPTH)], axis=0)
        s_dw = jnp.stack([small[l][k][1] for l in range(DEPTH)], axis=0)
        outs += [r_ln, win[k], r_pool, r_scale, s_dw, r_cb, r_lng, r_lnb, s_pw, r_sinks, wout[k], r_final]
    return (loss, grad_x, *outs)
```

```python
import jax
import jax.numpy as jnp
from jax import lax
from jax.experimental import pallas as pl
from jax.experimental.pallas import tpu as pltpu

F32 = jnp.float32
MXU_DTYPE = jnp.bfloat16
EXCHANGE_DTYPE = jnp.bfloat16

N_DEV = 8
DEPTH = 2
POOL_WIDTH = 256
POOL_GROUP = 64
CONV_WIDTH = 256
CONV_KERNEL = 31
CONV_TAPS_PAD = 32
HEAD_DIM = 64
N_KV_HEADS = 2
Q_PER_KV = 4
N_Q_HEADS = 8
ATTN_WIDTH = 512
BLOCK = 128
D_MIX = 1024
D_IN = 2560
HALF_IN = 1280
EPS = 1e-6
SCALE = HEAD_DIM ** -0.5
NEG = -1e30

ADAM_LR = 0.001
ADAM_B1 = 0.9
ADAM_B2 = 0.999
ADAM_EPS = 1e-08
ADAM_WD = 0.01
ADAM_STEP = 10

HALO = 32
ROW_TILE = 512
IN_BWD_TILE = 256
VMEM_LIMIT = 56 * 1024 * 1024

_NN = (((1,), (0,)), ((), ()))
_NT = (((1,), (1,)), ((), ()))
_TN = (((0,), (0,)), ((), ()))
_ANY = pl.BlockSpec(memory_space=pl.ANY)


def _mm(a, b, dims=_NN):
    return lax.dot_general(a.astype(MXU_DTYPE), b.astype(MXU_DTYPE), dims, preferred_element_type=F32)


def _sig(x):
    return 1.0 / (1.0 + jnp.exp(-x))


def _dsilu(z, s):
    return s * (1.0 + z * (1.0 - s))


def _params(n_grid):
    return pltpu.CompilerParams(dimension_semantics=("arbitrary",) * n_grid, vmem_limit_bytes=VMEM_LIMIT)


def _row_tile(rows, cap):
    t = min(rows, cap)
    while rows % t or t % 8:
        t -= 8
    return t


def _full(shape):
    return pl.BlockSpec(shape, lambda i: (0,) * len(shape))


def _mesh_pos():
    return lax.axis_index("x"), lax.axis_index("y"), lax.axis_index("c")


class _Exchange:
    def __init__(self, sources):
        self.sources = list(sources)
        self.n = len(self.sources)
        self.gather = [s.ndim == 2 for s in self.sources]

    def out_shapes(self):
        return [jax.ShapeDtypeStruct((N_DEV,) + s.shape[-2:], s.dtype) for s in self.sources]

    def scratch(self):
        if not self.n:
            return []
        return [pltpu.SemaphoreType.DMA((7 * self.n,)), pltpu.SemaphoreType.DMA((7 * self.n,)),
                pltpu.SemaphoreType.DMA((self.n,))]

    def copies(self, src_refs, dst_refs, sems):
        send_sems, recv_sems, local_sems = sems
        x, y, c = _mesh_pos()
        me = 4 * x + 2 * y + c
        out = []
        for a, (src, dst) in enumerate(zip(src_refs, dst_refs)):
            out.append(pltpu.make_async_copy(src if self.gather[a] else src.at[me], dst.at[me], local_sems.at[a]))
            for k in range(1, N_DEV):
                tx, ty, tc = x ^ ((k >> 2) & 1), y ^ ((k >> 1) & 1), c ^ (k & 1)
                out.append(pltpu.make_async_remote_copy(
                    src_ref=src if self.gather[a] else src.at[4 * tx + 2 * ty + tc], dst_ref=dst.at[me],
                    send_sem=send_sems.at[7 * a + k - 1], recv_sem=recv_sems.at[7 * a + k - 1],
                    device_id=(tx, ty, tc), device_id_type=pl.DeviceIdType.MESH))
        return out

    def run(self, refs, first, last):
        if not self.n:
            return
        src_refs, dst_refs, sems = refs

        @pl.when(first)
        def _():
            for cp in self.copies(src_refs, dst_refs, sems):
                cp.start()

        @pl.when(last)
        def _():
            for cp in self.copies(src_refs, dst_refs, sems):
                cp.wait()


def _split_refs(refs, n_in, n_out, exch):
    ins = refs[:n_in]
    srcs = refs[n_in:n_in + exch.n]
    outs = refs[n_in + exch.n:n_in + exch.n + n_out]
    dsts = refs[n_in + exch.n + n_out:n_in + 2 * exch.n + n_out]
    rest = refs[n_in + 2 * exch.n + n_out:]
    sems = rest[len(rest) - 3:] if exch.n else ()
    scratch = rest[:len(rest) - 3] if exch.n else rest
    return ins, outs, scratch, (srcs, dsts, sems)


def _final_exchange(sources):
    exch = _Exchange(sources)

    def body(*refs):
        _, _, _, xrefs = _split_refs(refs, 0, 0, exch)
        for cp in exch.copies(*xrefs):
            cp.start()
        for cp in exch.copies(*xrefs):
            cp.wait()

    return pl.pallas_call(
        body, name="final_exchange", out_shape=exch.out_shapes(),
        in_specs=[_ANY] * exch.n, out_specs=[_ANY] * exch.n, scratch_shapes=exch.scratch(),
    )(*exch.sources)


def _all_gather(shard, name):
    m_per, n = shard.shape

    def body(x_ref, out_ref, send_sems, recv_sems, local_sem):
        x, y, c = _mesh_pos()
        me, sibling = (x, y, c), (x, y, 1 - c)
        chips = [(1 - x, y), (x, 1 - y), (1 - x, 1 - y)]

        def rows(px, py, pc):
            return out_ref.at[pl.ds((4 * px + 2 * py + pc) * m_per, m_per), :]

        def copy(k, block, to, src=None):
            return pltpu.make_async_remote_copy(
                src_ref=rows(*block) if src is None else src, dst_ref=rows(*block),
                send_sem=send_sems.at[k], recv_sem=recv_sems.at[k],
                device_id=to, device_id_type=pl.DeviceIdType.MESH)

        mine = pltpu.make_async_copy(x_ref, rows(*me), local_sem)
        mine.start()
        first = [copy(0, me, sibling, src=x_ref)]
        first += [copy(1 + j, me, (*chip, c), src=x_ref) for j, chip in enumerate(chips)]
        for cp in first:
            cp.start()
        passed = [copy(4 + j, (*chip, c), sibling) for j, chip in enumerate(chips)]
        for j, chip in enumerate(chips):
            copy(1 + j, (*chip, c), me).wait_recv()
            passed[j].start()
        copy(0, sibling, me).wait_recv()
        for j, chip in enumerate(chips):
            copy(4 + j, (*chip, 1 - c), me).wait_recv()
        for cp in first + passed:
            cp.wait_send()
        mine.wait()

    return pl.pallas_call(
        body, name=name,
        out_shape=jax.ShapeDtypeStruct((N_DEV * m_per, n), shard.dtype),
        in_specs=[pl.BlockSpec(memory_space=pltpu.VMEM)],
        out_specs=pl.BlockSpec(memory_space=pltpu.VMEM),
        scratch_shapes=[pltpu.SemaphoreType.DMA((7,)), pltpu.SemaphoreType.DMA((7,)), pltpu.SemaphoreType.DMA],
        compiler_params=pltpu.CompilerParams(vmem_limit_bytes=VMEM_LIMIT),
    )(shard)


def _by_group(lane, v2, v4, v8, v16):
    return jnp.where(lane < 64, v2, jnp.where(lane < 128, v4, jnp.where(lane < 192, v8, v16)))


def _pool_count(t0, n):
    lane = lax.broadcasted_iota(jnp.int32, (1, POOL_WIDTH), 1)
    t = (t0 + lax.broadcasted_iota(jnp.int32, (n, 1), 0)).astype(F32)
    wnd = _by_group(lane, 2.0, 4.0, 8.0, 16.0)
    return jnp.minimum(t + 1.0, wnd)


def _pool_diff(u_ext, t0, ts):
    lane = lax.broadcasted_iota(jnp.int32, (1, POOL_WIDTH), 1)
    s2 = u_ext + pltpu.roll(u_ext, 1, 0)
    s4 = s2 + pltpu.roll(s2, 2, 0)
    s8 = s4 + pltpu.roll(s4, 4, 0)
    s16 = s8 + pltpu.roll(s8, 8, 0)
    pooled = _by_group(lane, s2, s4, s8, s16)[HALO:]
    return pooled / _pool_count(t0, ts) - u_ext[HALO:]


def _pool_diff_bwd(w, ts):
    n = w.shape[0]
    lane = lax.broadcasted_iota(jnp.int32, (1, POOL_WIDTH), 1)
    f2 = w + pltpu.roll(w, n - 1, 0)
    f4 = f2 + pltpu.roll(f2, n - 2, 0)
    f8 = f4 + pltpu.roll(f4, n - 4, 0)
    f16 = f8 + pltpu.roll(f8, n - 8, 0)
    return _by_group(lane, f2, f4, f8, f16)[:ts]


CONV_CHUNK = 64


def _conv_taps():
    return [(8 * m + r, r, m) for r in range(8) for m in range(4) if 8 * m + r < CONV_KERNEL]


def _store_shifted(dst_ref, x, up):
    n = x.shape[0]
    for r in range(8):
        dst_ref[r] = x if r == 0 else pltpu.roll(x, n - r if up else r, 0)


def _depthwise_conv(src_ref, dw_ref, out_ref, base, n_out, up, bias_ref=None):
    def chunk(c, carry):
        t0 = pl.multiple_of(c * CONV_CHUNK, CONV_CHUNK)
        acc = None
        for d, r, m in _conv_taps():
            rows = pl.ds(base + t0 + (8 * m if up else -8 * m), CONV_CHUNK)
            term = dw_ref[pl.ds(CONV_KERNEL - 1 - d, 1), :] * src_ref[r, rows, :]
            acc = term if acc is None else acc + term
        out_ref[pl.ds(t0, CONV_CHUNK), :] = acc if bias_ref is None else acc + bias_ref[...]
        return carry

    lax.fori_loop(0, n_out // CONV_CHUNK, chunk, 0)


def _depthwise_conv_weight_grad(dout_ref, src_ref, acc_ref, base, n_rows):
    acc_ref[...] = jnp.zeros_like(acc_ref)

    def chunk(c, carry):
        t0 = pl.multiple_of(c * CONV_CHUNK, CONV_CHUNK)
        g = dout_ref[0, pl.ds(t0, CONV_CHUNK), :]
        for d, r, m in _conv_taps():
            prod = g * src_ref[r, pl.ds(base + t0 - 8 * m, CONV_CHUNK), :]
            acc_ref[CONV_KERNEL - 1 - d] += jnp.sum(prod.reshape(CONV_CHUNK // 8, 8, prod.shape[-1]), axis=0)
        return carry

    lax.fori_loop(0, n_rows // CONV_CHUNK, chunk, 0)


def _layer_norm(cv):
    mu = jnp.mean(cv, axis=-1, keepdims=True)
    xc = cv - mu
    var = jnp.mean(xc * xc, axis=-1, keepdims=True)
    rstd = lax.rsqrt(var + EPS)
    return xc * rstd, rstd


PAIR_ROWS = 2 * BLOCK
PAIR_COLS = 4 * BLOCK
ATTN_BLOCKS_PER_STEP = 4


def _attn_blocks_per_step(s_len):
    qb = ATTN_BLOCKS_PER_STEP
    while (s_len // BLOCK) % qb:
        qb //= 2
    return qb


def _pair_rows(v0, v1):
    r = lax.broadcasted_iota(jnp.int32, (PAIR_ROWS, 1), 0)
    return jnp.where(r < BLOCK, v0, v1)


def _fill_attn_bias(bias_ref):
    rows = lax.broadcasted_iota(jnp.int32, (PAIR_ROWS, PAIR_COLS), 0)
    cols = lax.broadcasted_iota(jnp.int32, (PAIR_ROWS, PAIR_COLS), 1)
    key = cols & (2 * BLOCK - 1)
    dist = BLOCK + (rows & (BLOCK - 1)) - key
    in_band = (dist >= 0) & (dist < BLOCK)
    distf = dist.astype(F32)
    second = cols >= 2 * BLOCK
    for kh in range(N_KV_HEADS):
        slope_of = lambda j, a: 2.0 ** -(Q_PER_KV * kh + 2 * j + a + 1)
        slope = jnp.where(rows < BLOCK, jnp.where(second, slope_of(0, 1), slope_of(0, 0)),
                          jnp.where(second, slope_of(1, 1), slope_of(1, 0)))
        bias = -slope * distf
        bias_ref[0, kh] = jnp.where(in_band & (key >= BLOCK), bias, NEG)
        bias_ref[1, kh] = jnp.where(in_band, bias, NEG)


def _pair_block_matrix(x, x_swapped, kh):
    lo = lax.broadcasted_iota(jnp.int32, (1, 2 * HEAD_DIM), 1) < HEAD_DIM
    in_lo, in_hi = (x, x_swapped) if kh == 0 else (x_swapped, x)
    return jnp.concatenate([jnp.where(lo, in_lo, 0.0), jnp.where(lo, 0.0, in_hi)], axis=0).astype(MXU_DTYPE)


def _pair_queries(q, kh):
    return (_pair_stack(q, kh) * SCALE).astype(MXU_DTYPE)


def _pair_stack(a, kh):
    return jnp.concatenate([a[:, 2 * BLOCK * kh: 2 * BLOCK * kh + BLOCK],
                            a[:, 2 * BLOCK * kh + BLOCK: 2 * BLOCK * (kh + 1)]], axis=0)


def _pair_unstack(parts):
    return jnp.concatenate([p[BLOCK * j: BLOCK * (j + 1)] for p in parts for j in range(2)], axis=-1)


def _pair_softmax(s, kh, sinks_ref):
    ps, p_sinks = [], []
    for a in range(2):
        sa = s[:, 2 * BLOCK * a: 2 * BLOCK * (a + 1)]
        sink = _pair_rows(sinks_ref[Q_PER_KV * kh + a], sinks_ref[Q_PER_KV * kh + 2 + a])
        m = jnp.maximum(jnp.max(sa, axis=-1, keepdims=True), sink)
        e = jnp.exp(sa - m)
        es = jnp.exp(sink - m)
        inv = 1.0 / (jnp.sum(e, axis=-1, keepdims=True) + es)
        ps.append(e * inv)
        p_sinks.append(es * inv)
    return jnp.concatenate(ps, axis=-1), p_sinks


def _fold_pair_halves(t):
    lo = lax.broadcasted_iota(jnp.int32, (1, 2 * HEAD_DIM), 1) < HEAD_DIM
    u = jnp.where(lo, t[:2 * BLOCK], t[2 * BLOCK:])
    return u + pltpu.roll(u, HEAD_DIM, 1)


def _in_proj(x, g, w_t, carried=()):
    s_len, d = x.shape
    ts = _row_tile(s_len, ROW_TILE)
    nt = s_len // ts
    exch = _Exchange(carried)

    def body(*refs):
        (x_ref, g_ref, w_ref), (o_ref,), _, xrefs = _split_refs(refs, 3, 1, exch)
        i = pl.program_id(0)
        exch.run(xrefs, i == 0, i == nt - 1)
        xv = x_ref[...]
        r = lax.rsqrt(jnp.mean(xv * xv, axis=-1, keepdims=True) + EPS)
        o_ref[...] = _mm(xv * r * g_ref[...], w_ref[...], _NT)

    res = pl.pallas_call(
        body, name="in_proj_carrier" if exch.n else "in_proj", grid=(nt,),
        in_specs=[pl.BlockSpec((ts, d), lambda i: (i, 0)), _full((1, d)), _full((D_IN, d))] + [_ANY] * exch.n,
        out_specs=[pl.BlockSpec((ts, D_IN), lambda i: (i, 0))] + [_ANY] * exch.n,
        out_shape=[jax.ShapeDtypeStruct((s_len, D_IN), F32)] + exch.out_shapes(),
        scratch_shapes=exch.scratch(),
        compiler_params=_params(1),
    )(x, g, w_t, *exch.sources)
    return res[0], res[1:]


def _poolconv_fwd(proj, wp, scale, dw, cb, lng, lnb, pw, carried=()):
    s_len = proj.shape[0]
    ts = _row_tile(s_len, ROW_TILE)
    hb = ts // HALO
    nt = s_len // ts
    exch = _Exchange(carried)

    def body(*refs):
        ins, (y_ref, cv_ref), (shifted_ref,), xrefs = _split_refs(refs, 9, 2, exch)
        p_ref, ph_ref, wp_ref, sc_ref, dw_ref, cb_ref, lng_ref, lnb_ref, pw_ref = ins
        i = pl.program_id(0)
        exch.run(xrefs, i == 0, i == nt - 1)
        cur = p_ref[...]
        halo = jnp.where(i > 0, ph_ref[...], 0.0)
        ext = jnp.concatenate([halo, cur], axis=0)
        diff = _pool_diff(ext[:, 0:256], i * ts, ts)
        gp = cur[:, 256:512]
        y_pool = _mm(diff, wp_ref[...]) * sc_ref[...] * (gp * _sig(gp))
        _store_shifted(shifted_ref, ext[:, 512:768] * _sig(ext[:, 768:1024]), up=False)
        _depthwise_conv(shifted_ref, dw_ref, cv_ref, HALO, ts, up=False, bias_ref=cb_ref)
        n, _ = _layer_norm(cv_ref[...])
        z = n * lng_ref[...] + lnb_ref[...]
        gc = cur[:, 1024:1280]
        y_conv = _mm(z * _sig(z), pw_ref[...]) * (gc * _sig(gc))
        y_ref[...] = jnp.concatenate([y_pool, y_conv], axis=-1).astype(y_ref.dtype)

    vec = _full((1, 256))
    res = pl.pallas_call(
        body, name="poolconv_fwd_carrier" if exch.n else "poolconv_fwd", grid=(nt,),
        in_specs=[pl.BlockSpec((ts, HALF_IN), lambda i: (i, 0)),
                  pl.BlockSpec((HALO, HALF_IN), lambda i: (jnp.maximum(i * hb - 1, 0), 0)),
                  _full((256, 256)), vec, _full((CONV_TAPS_PAD, 256)), vec, vec, vec, _full((256, 256))]
        + [_ANY] * exch.n,
        out_specs=[pl.BlockSpec((ts, 512), lambda i: (i, 0)), pl.BlockSpec((ts, 256), lambda i: (i, 0))]
        + [_ANY] * exch.n,
        out_shape=[jax.ShapeDtypeStruct((s_len, 512), MXU_DTYPE), jax.ShapeDtypeStruct((s_len, 256), F32)]
        + exch.out_shapes(),
        scratch_shapes=[pltpu.VMEM((8, HALO + ts, CONV_WIDTH), F32)] + exch.scratch(),
        compiler_params=_params(1),
    )(proj, proj, wp, scale, dw, cb, lng, lnb, pw, *exch.sources)
    return res[:2], res[2:]


def _attn_fwd(proj, sinks, carried=()):
    s_len = proj.shape[0]
    qb = _attn_blocks_per_step(s_len)
    ts = qb * BLOCK
    nt = s_len // ts
    exch = _Exchange(carried)

    def body(*refs):
        (p_ref, kvp_ref, sinks_ref), (y_ref,), (bias_ref,), xrefs = _split_refs(refs, 3, 1, exch)
        i = pl.program_id(0)
        exch.run(xrefs, i == 0, i == nt - 1)

        @pl.when(i == 0)
        def _():
            _fill_attn_bias(bias_ref)

        probs = [(b, kh) for b in range(qb) for kh in range(N_KV_HEADS)]
        scores, v_bds = {}, {}
        for b in range(qb):
            r0 = BLOCK * b
            kv_prev = kvp_ref[...] if b == 0 else p_ref[r0 - BLOCK:r0, 512:768]
            kv2 = jnp.concatenate([kv_prev, p_ref[r0:r0 + BLOCK, 512:768]], axis=0)
            k2, v2 = kv2[:, :BLOCK], kv2[:, BLOCK:]
            k2_swapped, v2_swapped = pltpu.roll(k2, HEAD_DIM, 1), pltpu.roll(v2, HEAD_DIM, 1)
            variant = jnp.where(i == 0, 0, 1) if b == 0 else 1
            q = p_ref[r0:r0 + BLOCK, 0:512]
            for kh in range(N_KV_HEADS):
                k_bd = _pair_block_matrix(k2, k2_swapped, kh)
                v_bds[b, kh] = _pair_block_matrix(v2, v2_swapped, kh)
                scores[b, kh] = _mm(_pair_queries(q, kh), k_bd, _NT) + bias_ref[variant, kh]
        ps = {pr: _pair_softmax(scores[pr], pr[1], sinks_ref)[0] for pr in probs}
        outs = {pr: _mm(ps[pr], v_bds[pr]) for pr in probs}
        for b in range(qb):
            r0 = BLOCK * b
            ga = p_ref[r0:r0 + BLOCK, 768:1280]
            ya = _pair_unstack([outs[b, kh] for kh in range(N_KV_HEADS)])
            y_ref[r0:r0 + BLOCK, :] = (ya * (ga * _sig(ga))).astype(y_ref.dtype)

    res = pl.pallas_call(
        body, name="attn_fwd_carrier" if exch.n else "attn_fwd", grid=(nt,),
        in_specs=[pl.BlockSpec((ts, HALF_IN), lambda i: (i, 1)),
                  pl.BlockSpec((BLOCK, 256), lambda i: (jnp.maximum(i * qb - 1, 0), 7)),
                  pl.BlockSpec(memory_space=pltpu.SMEM)] + [_ANY] * exch.n,
        out_specs=[pl.BlockSpec((ts, 512), lambda i: (i, 0))] + [_ANY] * exch.n,
        out_shape=[jax.ShapeDtypeStruct((s_len, 512), MXU_DTYPE)] + exch.out_shapes(),
        scratch_shapes=[pltpu.VMEM((2, N_KV_HEADS, PAIR_ROWS, PAIR_COLS), F32)] + exch.scratch(),
        compiler_params=_params(1),
    )(proj, proj, sinks, *exch.sources)
    return res[0], res[1:]


def _out_proj(x, y_pc, y_at, w_out):
    s_len, d = x.shape
    ts = _row_tile(s_len, ROW_TILE)

    def body(x_ref, a_ref, b_ref, w_ref, o_ref):
        y = jnp.concatenate([a_ref[...], b_ref[...]], axis=-1)
        o_ref[...] = x_ref[...] + _mm(y, w_ref[...])

    return pl.pallas_call(
        body, name="out_proj", grid=(s_len // ts,),
        in_specs=[pl.BlockSpec((ts, d), lambda i: (i, 0)), pl.BlockSpec((ts, 512), lambda i: (i, 0)),
                  pl.BlockSpec((ts, 512), lambda i: (i, 0)), _full((D_MIX, d))],
        out_specs=pl.BlockSpec((ts, d), lambda i: (i, 0)),
        out_shape=jax.ShapeDtypeStruct((s_len, d), F32),
        compiler_params=_params(1),
    )(x, y_pc, y_at, w_out)


def _loss_bwd(x, target, g):
    s_len, d = x.shape
    ts = _row_tile(s_len, ROW_TILE)

    def body(x_ref, t_ref, g_ref, sq_ref, dg_ref, dx_ref):
        i = pl.program_id(0)
        xv = x_ref[...]
        gv = g_ref[...]
        r = lax.rsqrt(jnp.mean(xv * xv, axis=-1, keepdims=True) + EPS)
        xr = xv * r
        err = xr * gv - t_ref[...]
        dout = err * (1.0 / d)
        w = dout * gv
        dx_ref[...] = r * (w - xr * jnp.mean(w * xr, axis=-1, keepdims=True))

        @pl.when(i == 0)
        def _():
            sq_ref[...] = jnp.zeros_like(sq_ref)
            dg_ref[...] = jnp.zeros_like(dg_ref)

        sq = jnp.sum(jnp.sum(err * err, axis=-1, keepdims=True), axis=0, keepdims=True)
        sq_ref[...] += jnp.broadcast_to(sq, sq_ref.shape)
        dg_ref[...] += jnp.sum(dout * xr, axis=0, keepdims=True)

    return pl.pallas_call(
        body, name="loss_bwd", grid=(s_len // ts,),
        in_specs=[pl.BlockSpec((ts, d), lambda i: (i, 0)), pl.BlockSpec((ts, d), lambda i: (i, 0)), _full((1, d))],
        out_specs=[_full((1, 128)), _full((1, d)), pl.BlockSpec((ts, d), lambda i: (i, 0))],
        out_shape=[jax.ShapeDtypeStruct((1, 128), F32), jax.ShapeDtypeStruct((1, d), F32),
                   jax.ShapeDtypeStruct((s_len, d), F32)],
        compiler_params=_params(1),
    )(x, target, g)


def _out_bwd(dxo, y_pc, y_at, w_out, carried=()):
    s_len, d = dxo.shape
    ts = _row_tile(s_len, ROW_TILE)
    nt = s_len // ts
    exch = _Exchange(carried)

    def body(*refs):
        (dx_ref, a_ref, b_ref, w_ref), (dy_ref, gw_ref), (acc_ref,), xrefs = _split_refs(refs, 4, 2, exch)
        i = pl.program_id(0)
        exch.run(xrefs, i == 0, i == nt - 1)
        dxv = dx_ref[...].astype(MXU_DTYPE)
        dy_ref[...] = _mm(dxv, w_ref[...], _NT)

        @pl.when(i == 0)
        def _():
            acc_ref[...] = jnp.zeros_like(acc_ref)

        y = jnp.concatenate([a_ref[...], b_ref[...]], axis=-1)
        acc_ref[...] += _mm(y, dxv, _TN)

        @pl.when(i == nt - 1)
        def _():
            gw_ref[...] = acc_ref[...].astype(gw_ref.dtype)

    res = pl.pallas_call(
        body, name="out_bwd_carrier" if exch.n else "out_bwd", grid=(nt,),
        in_specs=[pl.BlockSpec((ts, d), lambda i: (i, 0)), pl.BlockSpec((ts, 512), lambda i: (i, 0)),
                  pl.BlockSpec((ts, 512), lambda i: (i, 0)), _full((D_MIX, d))] + [_ANY] * exch.n,
        out_specs=[pl.BlockSpec((ts, D_MIX), lambda i: (i, 0)), _full((D_MIX, d))] + [_ANY] * exch.n,
        out_shape=[jax.ShapeDtypeStruct((s_len, D_MIX), F32), jax.ShapeDtypeStruct((D_MIX, d), EXCHANGE_DTYPE)]
        + exch.out_shapes(),
        scratch_shapes=[pltpu.VMEM((D_MIX, d), F32)] + exch.scratch(),
        compiler_params=_params(1),
    )(dxo, y_pc, y_at, w_out, *exch.sources)
    return res[:2], res[2:]


def _poolconv_bwd(proj, cv, dy, wp, scale, dw, lng, lnb, pw, carried=()):
    s_len = proj.shape[0]
    ts = _row_tile(s_len, ROW_TILE)
    hb = ts // HALO
    nt = s_len // ts
    last_halo = s_len // HALO - 1
    n = ts + HALO
    exch = _Exchange(carried)

    def body(*refs):
        ins, outs, (up_ref, down_ref, dhh_ref, gdw_acc_ref), xrefs = _split_refs(refs, 13, 5, exch)
        (p_ref, ph_ref, pn_ref, cv_ref, cvn_ref, dy_ref, dyn_ref, wp_ref, sc_ref, dw_ref, lng_ref, lnb_ref,
         pw_ref) = ins
        da_ref, gwp_ref, gpw_ref, gdw_ref, gvec_ref = outs
        i = pl.program_id(0)
        exch.run(xrefs, i == 0, i == nt - 1)
        has_next = i < nt - 1
        cur = p_ref[...]
        prev = jnp.where(i > 0, ph_ref[...], 0.0)
        nxt = jnp.where(has_next, pn_ref[...], 0.0)
        dyx = jnp.concatenate([dy_ref[...], jnp.where(has_next, dyn_ref[...], 0.0)], axis=0)
        row = lax.broadcasted_iota(jnp.int32, (n, 1), 0)
        in_seq = (row < ts) | has_next
        scale_v = sc_ref[...]

        cvx = jnp.concatenate([cv_ref[...], jnp.where(has_next, cvn_ref[...], 0.0)], axis=0)
        nrm, rstd = _layer_norm(cvx)
        z = nrm * lng_ref[...] + lnb_ref[...]
        sz = _sig(z)
        sw = z * sz
        gc = jnp.concatenate([cur[:, 1024:1280], nxt[:, 1024:1280]], axis=0)
        sgc = _sig(gc)
        yc = _mm(sw, pw_ref[...])
        dyc = dyx[:, 256:512]
        d_yc = dyc * (gc * sgc)
        d_gc = (dyc * yc * _dsilu(gc, sgc))[:ts]
        d_z = _mm(d_yc, pw_ref[...], _NT) * _dsilu(z, sz)
        d_n = d_z * lng_ref[...]
        d_cv = rstd * (d_n - jnp.mean(d_n, axis=-1, keepdims=True)
                       - nrm * jnp.mean(d_n * nrm, axis=-1, keepdims=True))
        d_cv = jnp.where(in_seq, d_cv, 0.0)
        _store_shifted(up_ref, d_cv, up=True)
        _depthwise_conv(up_ref, dw_ref, dhh_ref, 0, ts, up=True)
        d_hh = dhh_ref[...]
        a2 = jnp.concatenate([prev[:, 512:768], cur[:, 512:768]], axis=0)
        sb2 = _sig(jnp.concatenate([prev[:, 768:1024], cur[:, 768:1024]], axis=0))
        _store_shifted(down_ref, a2 * sb2, up=False)
        a_c, sb_c = a2[HALO:], sb2[HALO:]
        d_a = d_hh * sb_c
        d_b = d_hh * a_c * sb_c * (1.0 - sb_c)
        d_cv_t = d_cv[:ts]
        _depthwise_conv_weight_grad(up_ref, down_ref, gdw_acc_ref, HALO, ts)

        u_ext = jnp.concatenate([prev[:, 0:256], cur[:, 0:256]], axis=0)
        diff = _pool_diff(u_ext, i * ts, ts)
        raw = _mm(diff, wp_ref[...])
        gp = jnp.concatenate([cur[:, 256:512], nxt[:, 256:512]], axis=0)
        sgp = _sig(gp)
        dyp = dyx[:, 0:256]
        d_yp = dyp * (gp * sgp)
        d_gp = dyp[:ts] * (raw * scale_v) * _dsilu(gp, sgp)[:ts]
        d_raw = d_yp * scale_v
        d_diff = _mm(d_raw, wp_ref[...], _NT)
        w = jnp.where(in_seq, d_diff / _pool_count(i * ts, n), 0.0)
        d_u = _pool_diff_bwd(w, ts) - d_diff[:ts]

        da_ref[...] = jnp.concatenate([d_u, d_gp, d_a, d_b, d_gc], axis=-1).astype(da_ref.dtype)

        @pl.when(i == 0)
        def _():
            gwp_ref[...] = jnp.zeros_like(gwp_ref)
            gpw_ref[...] = jnp.zeros_like(gpw_ref)
            gdw_ref[...] = jnp.zeros_like(gdw_ref)
            gvec_ref[...] = jnp.zeros_like(gvec_ref)

        gwp_ref[...] += _mm(diff, d_raw[:ts], _TN)
        gpw_ref[...] += _mm(sw[:ts], d_yc[:ts], _TN)
        gdw_ref[...] += jnp.sum(gdw_acc_ref[...], axis=1)
        zero_row = jnp.zeros((1, 256), F32)
        gvec_ref[...] += jnp.concatenate([
            jnp.sum(d_yp[:ts] * raw, axis=0, keepdims=True),
            jnp.sum(d_cv_t, axis=0, keepdims=True),
            jnp.sum((d_z * nrm)[:ts], axis=0, keepdims=True),
            jnp.sum(d_z[:ts], axis=0, keepdims=True),
            zero_row, zero_row, zero_row, zero_row], axis=0)

    vec = _full((1, 256))
    nxt_halo = lambda i: (jnp.minimum((i + 1) * hb, last_halo), 0)
    res = pl.pallas_call(
        body, name="poolconv_bwd_carrier" if exch.n else "poolconv_bwd", grid=(nt,),
        in_specs=[pl.BlockSpec((ts, HALF_IN), lambda i: (i, 0)),
                  pl.BlockSpec((HALO, HALF_IN), lambda i: (jnp.maximum(i * hb - 1, 0), 0)),
                  pl.BlockSpec((HALO, HALF_IN), nxt_halo),
                  pl.BlockSpec((ts, 256), lambda i: (i, 0)), pl.BlockSpec((HALO, 256), nxt_halo),
                  pl.BlockSpec((ts, 512), lambda i: (i, 0)), pl.BlockSpec((HALO, 512), nxt_halo),
                  _full((256, 256)), vec, _full((CONV_TAPS_PAD, 256)), vec, vec, _full((256, 256))]
        + [_ANY] * exch.n,
        out_specs=[pl.BlockSpec((ts, HALF_IN), lambda i: (i, 0)), _full((256, 256)), _full((256, 256)),
                   _full((CONV_TAPS_PAD, 256)), _full((8, 256))] + [_ANY] * exch.n,
        out_shape=[jax.ShapeDtypeStruct((s_len, HALF_IN), MXU_DTYPE), jax.ShapeDtypeStruct((256, 256), F32),
                   jax.ShapeDtypeStruct((256, 256), F32), jax.ShapeDtypeStruct((CONV_TAPS_PAD, 256), F32),
                   jax.ShapeDtypeStruct((8, 256), F32)] + exch.out_shapes(),
        scratch_shapes=[pltpu.VMEM((8, n, CONV_WIDTH), F32), pltpu.VMEM((8, n, CONV_WIDTH), F32),
                        pltpu.VMEM((ts, CONV_WIDTH), F32), pltpu.VMEM((CONV_TAPS_PAD, 8, CONV_WIDTH), F32)]
        + exch.scratch(),
        compiler_params=_params(1),
    )(proj, proj, proj, cv, cv, dy, dy, wp, scale, dw, lng, lnb, pw, *exch.sources)
    return res[:5], res[5:]


DQ0, DKC0, DVC0, DKP0, DVP0, DGA0, DATTN_W = 0, 512, 640, 768, 896, 1024, 1536


def _attn_bwd(proj, dy, sinks, carried=()):
    s_len = proj.shape[0]
    qb = _attn_blocks_per_step(s_len)
    ts = qb * BLOCK
    nt = s_len // ts
    exch = _Exchange(carried)

    def body(*refs):
        (p_ref, kvp_ref, dy_ref, sinks_ref), (o_ref, gs_ref), (bias_ref,), xrefs = _split_refs(refs, 4, 2, exch)
        i = pl.program_id(0)
        exch.run(xrefs, i == 0, i == nt - 1)

        @pl.when(i == 0)
        def _():
            _fill_attn_bias(bias_ref)
            gs_ref[...] = jnp.zeros_like(gs_ref)

        lo = lax.broadcasted_iota(jnp.int32, (1, 2 * HEAD_DIM), 1) < HEAD_DIM
        probs = [(b, kh) for b in range(qb) for kh in range(N_KV_HEADS)]
        kv_heads = range(N_KV_HEADS)
        scores, k_bds, v_bds, q2s, do2s, dyas, gas, sgas = {}, {}, {}, {}, {}, {}, {}, {}
        for b in range(qb):
            r0 = BLOCK * b
            kv_prev = kvp_ref[...] if b == 0 else p_ref[r0 - BLOCK:r0, 512:768]
            kv2 = jnp.concatenate([kv_prev, p_ref[r0:r0 + BLOCK, 512:768]], axis=0)
            k2, v2 = kv2[:, :BLOCK], kv2[:, BLOCK:]
            k2_swapped, v2_swapped = pltpu.roll(k2, HEAD_DIM, 1), pltpu.roll(v2, HEAD_DIM, 1)
            variant = jnp.where(i == 0, 0, 1) if b == 0 else 1
            q = p_ref[r0:r0 + BLOCK, 0:512]
            gas[b] = p_ref[r0:r0 + BLOCK, 768:1280]
            dyas[b] = dy_ref[r0:r0 + BLOCK, :]
            sgas[b] = _sig(gas[b])
            d_o = dyas[b] * (gas[b] * sgas[b])
            for kh in kv_heads:
                k_bds[b, kh] = _pair_block_matrix(k2, k2_swapped, kh)
                v_bds[b, kh] = _pair_block_matrix(v2, v2_swapped, kh)
                q2s[b, kh] = _pair_queries(q, kh)
                do2s[b, kh] = _pair_stack(d_o, kh)
                scores[b, kh] = _mm(q2s[b, kh], k_bds[b, kh], _NT) + bias_ref[variant, kh]
        softmaxes = {pr: _pair_softmax(scores[pr], pr[1], sinks_ref) for pr in probs}
        ps = {pr: softmaxes[pr][0] for pr in probs}
        outs = {pr: _mm(ps[pr], v_bds[pr]) for pr in probs}
        dps = {pr: _mm(do2s[pr], v_bds[pr], _NT) for pr in probs}
        dss = {}
        d_sinks = [None] * N_Q_HEADS
        for pr in probs:
            p, dp, kh = ps[pr], dps[pr], pr[1]
            ds_halves = []
            for a in range(2):
                cols = slice(2 * BLOCK * a, 2 * BLOCK * (a + 1))
                delta = jnp.sum(p[:, cols] * dp[:, cols], axis=-1, keepdims=True)
                ds_halves.append(p[:, cols] * (dp[:, cols] - delta))
                dsink = -softmaxes[pr][1][a] * delta
                for j in range(2):
                    part = jnp.sum(dsink[BLOCK * j: BLOCK * (j + 1)], axis=0, keepdims=True)
                    h = Q_PER_KV * kh + 2 * j + a
                    d_sinks[h] = part if d_sinks[h] is None else d_sinks[h] + part
            dss[pr] = jnp.concatenate(ds_halves, axis=-1)
        dqs = {pr: _mm(dss[pr], k_bds[pr]) * SCALE for pr in probs}
        dks = {pr: _fold_pair_halves(_mm(dss[pr], q2s[pr], _TN)) for pr in probs}
        dvs = {pr: _fold_pair_halves(_mm(ps[pr], do2s[pr], _TN)) for pr in probs}
        for b in range(qb):
            r0 = BLOCK * b
            dk = jnp.where(lo, dks[b, 0], dks[b, 1])
            dv = jnp.where(lo, dvs[b, 0], dvs[b, 1])
            d_ga = dyas[b] * _pair_unstack([outs[b, kh] for kh in kv_heads]) * _dsilu(gas[b], sgas[b])
            o_ref[r0:r0 + BLOCK, :] = jnp.concatenate(
                [_pair_unstack([dqs[b, kh] for kh in kv_heads]), dk[BLOCK:], dv[BLOCK:], dk[:BLOCK], dv[:BLOCK],
                 d_ga], axis=-1)
        gs_ref[...] += jnp.broadcast_to(jnp.concatenate(d_sinks, axis=0), gs_ref.shape)

    res = pl.pallas_call(
        body, name="attn_bwd_carrier" if exch.n else "attn_bwd", grid=(nt,),
        in_specs=[pl.BlockSpec((ts, HALF_IN), lambda i: (i, 1)),
                  pl.BlockSpec((BLOCK, 256), lambda i: (jnp.maximum(i * qb - 1, 0), 7)),
                  pl.BlockSpec((ts, 512), lambda i: (i, 1)),
                  pl.BlockSpec(memory_space=pltpu.SMEM)] + [_ANY] * exch.n,
        out_specs=[pl.BlockSpec((ts, DATTN_W), lambda i: (i, 0)), _full((N_Q_HEADS, 128))] + [_ANY] * exch.n,
        out_shape=[jax.ShapeDtypeStruct((s_len, DATTN_W), F32), jax.ShapeDtypeStruct((N_Q_HEADS, 128), F32)]
        + exch.out_shapes(),
        scratch_shapes=[pltpu.VMEM((2, N_KV_HEADS, PAIR_ROWS, PAIR_COLS), F32)] + exch.scratch(),
        compiler_params=_params(1),
    )(proj, proj, dy, sinks, *exch.sources)
    return res[:2], res[2:]


def _in_bwd(da, dattn, x, dxo, g, w_t, carried=()):
    s_len, d = x.shape
    ts = _row_tile(s_len, IN_BWD_TILE)
    bpt = ts // BLOCK
    nt = s_len // ts
    last_block = s_len // BLOCK - 1
    exch = _Exchange(carried)

    def body(*refs):
        ins, (dx_ref, dg_ref, gw_ref), (acc_ref,), xrefs = _split_refs(refs, 7, 3, exch)
        da_ref, dat_ref, nxt_ref, x_ref, dxo_ref, g_ref, w_ref = ins
        i = pl.program_id(0)
        exch.run(xrefs, i == 0, i == nt - 1)
        dat = dat_ref[...]
        nxt = jnp.where(i < nt - 1, nxt_ref[...], 0.0)
        shifted = jnp.concatenate([dat[BLOCK:, DKP0:DGA0], nxt], axis=0) if bpt > 1 else nxt
        dkv = dat[:, DKC0:DKP0] + shifted
        dproj = jnp.concatenate([da_ref[...], dat[:, DQ0:DKC0].astype(MXU_DTYPE), dkv.astype(MXU_DTYPE),
                                 dat[:, DGA0:DATTN_W].astype(MXU_DTYPE)], axis=-1)
        d_h = _mm(dproj, w_ref[...])
        xv = x_ref[...]
        gv = g_ref[...]
        r = lax.rsqrt(jnp.mean(xv * xv, axis=-1, keepdims=True) + EPS)
        xr = xv * r
        w = d_h * gv
        dx_ref[...] = dxo_ref[...] + r * (w - xr * jnp.mean(w * xr, axis=-1, keepdims=True))

        @pl.when(i == 0)
        def _():
            dg_ref[...] = jnp.zeros_like(dg_ref)
            acc_ref[...] = jnp.zeros_like(acc_ref)

        dg_ref[...] += jnp.sum(d_h * xr, axis=0, keepdims=True)
        acc_ref[...] += _mm(dproj, xr * gv, _TN)

        @pl.when(i == nt - 1)
        def _():
            gw_ref[...] = acc_ref[...].astype(gw_ref.dtype)

    res = pl.pallas_call(
        body, name="in_bwd_carrier" if exch.n else "in_bwd", grid=(nt,),
        in_specs=[pl.BlockSpec((ts, HALF_IN), lambda i: (i, 0)),
                  pl.BlockSpec((ts, DATTN_W), lambda i: (i, 0)),
                  pl.BlockSpec((BLOCK, 256), lambda i: (jnp.minimum((i + 1) * bpt, last_block), 3)),
                  pl.BlockSpec((ts, d), lambda i: (i, 0)), pl.BlockSpec((ts, d), lambda i: (i, 0)),
                  _full((1, d)), _full((D_IN, d))] + [_ANY] * exch.n,
        out_specs=[pl.BlockSpec((ts, d), lambda i: (i, 0)), _full((1, d)), _full((D_IN, d))] + [_ANY] * exch.n,
        out_shape=[jax.ShapeDtypeStruct((s_len, d), F32), jax.ShapeDtypeStruct((1, d), F32),
                   jax.ShapeDtypeStruct((D_IN, d), EXCHANGE_DTYPE)] + exch.out_shapes(),
        scratch_shapes=[pltpu.VMEM((D_IN, d), F32)] + exch.scratch(),
        compiler_params=_params(1),
    )(da, dattn, dattn, x, dxo, g, w_t, *exch.sources)
    return res[:3], res[3:]


def _sum_partials(p_ref):
    g = p_ref[0].astype(F32)
    for k in range(1, N_DEV):
        g = g + p_ref[k].astype(F32)
    return g


def _adamw_step(g, w, m, v):
    nm = ADAM_B1 * m + (1.0 - ADAM_B1) * g
    nv = ADAM_B2 * v + (1.0 - ADAM_B2) * (g * g)
    m_hat = nm / (1.0 - ADAM_B1 ** ADAM_STEP)
    v_hat = nv / (1.0 - ADAM_B2 ** ADAM_STEP)
    return -ADAM_LR * (m_hat / (jnp.sqrt(v_hat) + ADAM_EPS) + ADAM_WD * w), nm, nv


def _adamw_layers(parts, w, m, v, transposed, name):
    def body(*refs):
        p_refs = refs[:DEPTH]
        w_ref, m_ref, v_ref, g_ref, d_ref, nm_ref, nv_ref = refs[DEPTH:]
        for l in range(DEPTH):
            g = _sum_partials(p_refs[l])
            if transposed:
                g = g.T
            g_ref[l] = g
            d_ref[l], nm_ref[l], nv_ref[l] = _adamw_step(g, w_ref[l], m_ref[l], v_ref[l])

    vmem = pl.BlockSpec(memory_space=pltpu.VMEM)
    shape = jax.ShapeDtypeStruct(w.shape, F32)
    return pl.pallas_call(
        body, name=name, in_specs=[vmem] * (DEPTH + 3), out_specs=[vmem] * 4, out_shape=[shape] * 4,
        compiler_params=pltpu.CompilerParams(vmem_limit_bytes=VMEM_LIMIT),
    )(*parts, w, m, v)


def _adamw(parts, w, m, v, name):
    rows, n = w.shape
    tr = _row_tile(rows, 256)

    def body(p_ref, w_ref, m_ref, v_ref, g_ref, d_ref, nm_ref, nv_ref):
        g = _sum_partials(p_ref)
        g_ref[...] = g
        d_ref[...], nm_ref[...], nv_ref[...] = _adamw_step(g, w_ref[...], m_ref[...], v_ref[...])

    tile = pl.BlockSpec((tr, n), lambda i: (i, 0))
    shape = jax.ShapeDtypeStruct((rows, n), F32)
    return pl.pallas_call(
        body, name=name, grid=(rows // tr,),
        in_specs=[pl.BlockSpec((N_DEV, tr, n), lambda i: (0, i, 0)), tile, tile, tile],
        out_specs=[tile, tile, tile, tile],
        out_shape=[shape, shape, shape, shape],
        compiler_params=_params(1),
    )(parts, w, m, v)


def _pad_rows(a, mult):
    pad = (-a.shape[0]) % mult
    return a if pad == 0 else jnp.concatenate([a, jnp.zeros((pad, a.shape[1]), a.dtype)], axis=0)


def _dw_rows(conv_dw_l):
    return jnp.pad(jnp.swapaxes(conv_dw_l, 0, 1), ((0, 0), (0, CONV_TAPS_PAD - CONV_KERNEL)))


def _pack_small(pw_l, dw_l, d):
    rows = jnp.concatenate([pw_l.reshape(-1, d), _dw_rows(dw_l).reshape(-1, d)], axis=0)
    return _pad_rows(rows, 8)


def _small_slabs(g_pw, g_dw, d):
    a = g_pw.reshape(N_DEV, -1, d)
    b = jnp.swapaxes(g_dw, 0, 1).reshape(N_DEV, -1, d)
    used = a.shape[1] + b.shape[1]
    return jnp.concatenate([a, b, jnp.zeros((N_DEV, (-used) % 8, d), g_pw.dtype)], axis=1)


def _unpack_small(rows, d):
    c = CONV_WIDTH // N_DEV
    n_pw = c * CONV_WIDTH // d
    n_dw = c * CONV_TAPS_PAD // d
    pw = rows[:n_pw].reshape(c, CONV_WIDTH)
    dw = jnp.swapaxes(rows[n_pw:n_pw + n_dw].reshape(c, CONV_TAPS_PAD), 0, 1)[:CONV_KERNEL]
    return pw, dw


def _pack_replicated(ln_g, pool_w, pool_scale, conv_b, conv_ln_g, conv_ln_b, attn_sinks, final_g, scalar, d):
    sinks = jnp.pad(attn_sinks, ((0, 0), (0, 256 - N_Q_HEADS)))
    small = jnp.concatenate([pool_scale, conv_b, conv_ln_g, conv_ln_b, sinks], axis=0)
    small = _pad_rows(small, d // 256)
    last = jnp.pad(scalar.reshape(1, 1), ((0, 0), (0, d - 1)))
    return _pad_rows(jnp.concatenate([ln_g.reshape(-1, d), final_g.reshape(-1, d), pool_w.reshape(-1, d),
                                      small.reshape(-1, d), last], axis=0), 8)


def _unpack_replicated(rows, d):
    n_pool = DEPTH * 4 * POOL_GROUP * POOL_GROUP // d
    n_small = -(-5 * DEPTH * 256 // d)
    ln_g = rows[:DEPTH]
    final_g = rows[DEPTH]
    pool_w = rows[DEPTH + 1: DEPTH + 1 + n_pool].reshape(DEPTH, 4, POOL_GROUP, POOL_GROUP)
    small = rows[DEPTH + 1 + n_pool: DEPTH + 1 + n_pool + n_small].reshape(-1, 256)[: 5 * DEPTH]
    pool_scale, conv_b, conv_ln_g, conv_ln_b = (small[DEPTH * k: DEPTH * (k + 1)] for k in range(4))
    sinks = small[4 * DEPTH: 5 * DEPTH, :N_Q_HEADS]
    scalar = rows[DEPTH + 1 + n_pool + n_small, 0]
    return ln_g, pool_w, pool_scale, conv_b, conv_ln_g, conv_ln_b, sinks, final_g, scalar


def _block_diag(pool_w):
    out = jnp.zeros((POOL_WIDTH, POOL_WIDTH), pool_w.dtype)
    for gi in range(4):
        out = out.at[POOL_GROUP * gi: POOL_GROUP * (gi + 1), POOL_GROUP * gi: POOL_GROUP * (gi + 1)].set(pool_w[gi])
    return out


def _diag_blocks(mat):
    return jnp.stack([mat[POOL_GROUP * gi: POOL_GROUP * (gi + 1), POOL_GROUP * gi: POOL_GROUP * (gi + 1)]
                      for gi in range(4)], axis=0)


def kernel(x, ln_g, w_in, pool_w, pool_scale, conv_dw, conv_b, conv_ln_g, conv_ln_b, conv_pw, attn_sinks, w_out, final_g, loss_target, m_ln_g, m_w_in, m_pool_w, m_pool_scale, m_conv_dw, m_conv_b, m_conv_ln_g, m_conv_ln_b, m_conv_pw, m_attn_sinks, m_w_out, m_final_g, v_ln_g, v_w_in, v_pool_w, v_pool_scale, v_conv_dw, v_conv_b, v_conv_ln_g, v_conv_ln_b, v_conv_pw, v_attn_sinks, v_w_out, v_final_g):
    x0 = x[0]
    d = x0.shape[1]
    row = lambda a: a.reshape(1, -1)
    slabs = lambda a: a.reshape(N_DEV, a.shape[0] // N_DEV, d)
    c_shard = CONV_WIDTH // N_DEV

    w_in_rows = [jnp.swapaxes(w_in[l], 0, 1).astype(MXU_DTYPE) for l in range(DEPTH)]
    w_out_rows = [w_out[l].astype(MXU_DTYPE) for l in range(DEPTH)]
    per_word = 4 // jnp.dtype(MXU_DTYPE).itemsize
    dw_t = jnp.stack([_dw_rows(conv_dw[l]) for l in range(DEPTH)], axis=0)
    dw_bits = (lax.bitcast_convert_type(dw_t, MXU_DTYPE) if per_word > 1 else dw_t).reshape(-1, d)
    n_pw = DEPTH * c_shard * CONV_WIDTH // d
    conv_rows = _pad_rows(jnp.concatenate([conv_pw.reshape(-1, d).astype(MXU_DTYPE), dw_bits], axis=0), 16)
    w_in_t, w_out_f = [None] * DEPTH, [None] * DEPTH
    w_in_t[0] = _all_gather(w_in_rows[0], "w_in_all_gather")
    wp_bd = [_block_diag(pool_w[l]).astype(MXU_DTYPE) for l in range(DEPTH)]

    xs, projs, cvs, ys = [x0], [], [], []
    half = D_IN // N_DEV // 2
    for l in range(DEPTH):
        proj, got = _in_proj(xs[l], row(ln_g[l]), w_in_t[l],
                             carried=[w_out_rows[0], conv_rows, w_in_rows[1][:half]] if l == 0 else [])
        if l == 0:
            w_in_1a = got[2]
            w_out_f[0] = got[0].reshape(D_MIX, d)
            pw_all = got[1][:, :n_pw].reshape(N_DEV, DEPTH, c_shard, CONV_WIDTH)
            pw_f = [pw_all[:, k].reshape(CONV_WIDTH, CONV_WIDTH) for k in range(DEPTH)]
            bits = got[1][:, n_pw:n_pw + dw_bits.shape[0]].reshape(
                (N_DEV, DEPTH, c_shard, CONV_TAPS_PAD) + (per_word,) * (per_word > 1))
            dw_all = lax.bitcast_convert_type(bits, F32) if per_word > 1 else bits
            dw_f = [jnp.swapaxes(dw_all[:, k].reshape(CONV_WIDTH, CONV_TAPS_PAD), 0, 1) for k in range(DEPTH)]
        (y_pc, cv), got = _poolconv_fwd(proj, wp_bd[l], row(pool_scale[l]), dw_f[l], row(conv_b[l]),
                                        row(conv_ln_g[l]), row(conv_ln_b[l]), pw_f[l],
                                        carried=[w_in_rows[1][half:]] if l == 0 else [])
        if l == 0:
            w_in_t[1] = jnp.concatenate([w_in_1a, got[0]], axis=1).reshape(D_IN, d)
        y_at, got = _attn_fwd(proj, attn_sinks[l], carried=[w_out_rows[1]] if l == 0 else [])
        if l == 0:
            w_out_f[1] = got[0].reshape(D_MIX, d)
        xs.append(_out_proj(xs[l], y_pc, y_at, w_out_f[l]))
        projs.append(proj)
        cvs.append(cv)
        ys.append((y_pc, y_at))

    sq, g_final, dx = _loss_bwd(xs[DEPTH], loss_target[0], row(final_g))

    l = 1
    (dy, g_wout1), _ = _out_bwd(dx, ys[l][0], ys[l][1], w_out_f[l])
    (da, g_wp1, g_pw1, g_dw1, g_vec1), _ = _poolconv_bwd(
        projs[l], cvs[l], dy, wp_bd[l], row(pool_scale[l]), dw_f[l], row(conv_ln_g[l]), row(conv_ln_b[l]), pw_f[l])
    (dattn, gs1), _ = _attn_bwd(projs[l], dy, attn_sinks[l])
    (dx, g_ln1, g_win_t1), _ = _in_bwd(da, dattn, xs[l], dx, row(ln_g[l]), w_in_t[l])
    l = 0
    (dy, g_wout0), (r_wout1, r_small1) = _out_bwd(dx, ys[l][0], ys[l][1], w_out_f[l],
                                                  carried=[slabs(g_wout1), _small_slabs(g_pw1, g_dw1, d)])
    (da, g_wp0, g_pw0, g_dw0, g_vec0), (r_win1,) = _poolconv_bwd(
        projs[l], cvs[l], dy, wp_bd[l], row(pool_scale[l]), dw_f[l], row(conv_ln_g[l]), row(conv_ln_b[l]), pw_f[l],
        carried=[slabs(g_win_t1)])
    (dattn, gs0), (r_wout0, r_small0) = _attn_bwd(projs[l], dy, attn_sinks[l],
                                                  carried=[slabs(g_wout0), _small_slabs(g_pw0, g_dw0, d)])
    (dx, g_ln0, g_win_t0), _ = _in_bwd(da, dattn, xs[l], dx, row(ln_g[l]), w_in_t[l])
    grad_x = dx[None]

    gv = jnp.stack([g_vec0, g_vec1], axis=0)
    rep_part = _pack_replicated(
        jnp.concatenate([g_ln0, g_ln1], axis=0), jnp.stack([_diag_blocks(g_wp0), _diag_blocks(g_wp1)], axis=0),
        gv[:, 0], gv[:, 1], gv[:, 2], gv[:, 3], jnp.stack([gs0[:, 0], gs1[:, 0]], axis=0), g_final, sq[0, 0], d)
    r_win0, r_rep = _final_exchange([slabs(g_win_t0), rep_part])

    win = _adamw_layers([r_win0, r_win1], w_in, m_w_in, v_w_in, True, "adamw_w_in")
    wout = _adamw_layers([r_wout0, r_wout1], w_out, m_w_out, v_w_out, False, "adamw_w_out")
    small = [[_unpack_small(o, d) for o in _adamw(
        r, _pack_small(conv_pw[l], conv_dw[l], d), _pack_small(m_conv_pw[l], m_conv_dw[l], d),
        _pack_small(v_conv_pw[l], v_conv_dw[l], d), f"adamw_conv_{l}")] for l, r in enumerate([r_small0, r_small1])]
    zero = jnp.zeros((), F32)
    pack_r = lambda *a: _pack_replicated(*a, zero, d)
    rep = [_unpack_replicated(o, d) for o in _adamw(
        r_rep, pack_r(ln_g, pool_w, pool_scale, conv_b, conv_ln_g, conv_ln_b, attn_sinks, final_g),
        pack_r(m_ln_g, m_pool_w, m_pool_scale, m_conv_b, m_conv_ln_g, m_conv_ln_b, m_attn_sinks, m_final_g),
        pack_r(v_ln_g, v_pool_w, v_pool_scale, v_conv_b, v_conv_ln_g, v_conv_ln_b, v_attn_sinks, v_final_g),
        "adamw_replicated")]
    loss = 0.5 / d * rep[0][-1]

    outs = []
    for k in range(4):
        r_ln, r_pool, r_scale, r_cb, r_lng, r_lnb, r_sinks, r_final, _ = rep[k]
        s_pw = jnp.stack([small[l][k][0] for l in range(DEPTH)], axis=0)
        s_dw = jnp.stack([small[l][k][1] for l in range(DEPTH)], axis=0)
        outs += [r_ln, win[k], r_pool, r_scale, s_dw, r_cb, r_lng, r_lnb, s_pw, r_sinks, wout[k], r_final]
    return (loss, grad_x, *outs)
```

```python
import jax
import jax.numpy as jnp
from jax import lax
from jax.experimental import pallas as pl
from jax.experimental.pallas import tpu as pltpu

F32 = jnp.float32
MXU_DTYPE = jnp.bfloat16
EXCHANGE_DTYPE = jnp.bfloat16

N_DEV = 8
DEPTH = 2
POOL_WIDTH = 256
POOL_GROUP = 64
CONV_WIDTH = 256
CONV_KERNEL = 31
CONV_TAPS_PAD = 32
HEAD_DIM = 64
N_KV_HEADS = 2
Q_PER_KV = 4
N_Q_HEADS = 8
ATTN_WIDTH = 512
BLOCK = 128
D_MIX = 1024
D_IN = 2560
HALF_IN = 1280
EPS = 1e-6
SCALE = HEAD_DIM ** -0.5
NEG = -1e30

ADAM_LR = 0.001
ADAM_B1 = 0.9
ADAM_B2 = 0.999
ADAM_EPS = 1e-08
ADAM_WD = 0.01
ADAM_STEP = 10

HALO = 32
ROW_TILE = 512
IN_BWD_TILE = 256
VMEM_LIMIT = 56 * 1024 * 1024

_NN = (((1,), (0,)), ((), ()))
_NT = (((1,), (1,)), ((), ()))
_TN = (((0,), (0,)), ((), ()))
_ANY = pl.BlockSpec(memory_space=pl.ANY)


def _mm(a, b, dims=_NN):
    return lax.dot_general(a.astype(MXU_DTYPE), b.astype(MXU_DTYPE), dims, preferred_element_type=F32)


def _sig(x):
    return 1.0 / (1.0 + jnp.exp(-x))


def _dsilu(z, s):
    return s * (1.0 + z * (1.0 - s))


def _params(n_grid):
    return pltpu.CompilerParams(dimension_semantics=("arbitrary",) * n_grid, vmem_limit_bytes=VMEM_LIMIT)


def _row_tile(rows, cap):
    t = min(rows, cap)
    while rows % t or t % 8:
        t -= 8
    return t


def _full(shape):
    return pl.BlockSpec(shape, lambda i: (0,) * len(shape))


def _mesh_pos():
    return lax.axis_index("x"), lax.axis_index("y"), lax.axis_index("c")


class _Exchange:
    def __init__(self, sources):
        self.sources = list(sources)
        self.n = len(self.sources)
        self.gather = [s.ndim == 2 for s in self.sources]

    def out_shapes(self):
        return [jax.ShapeDtypeStruct((N_DEV,) + s.shape[-2:], s.dtype) for s in self.sources]

    def scratch(self):
        if not self.n:
            return []
        return [pltpu.SemaphoreType.DMA((7 * self.n,)), pltpu.SemaphoreType.DMA((7 * self.n,)),
                pltpu.SemaphoreType.DMA((self.n,))]

    def copies(self, src_refs, dst_refs, sems):
        send_sems, recv_sems, local_sems = sems
        x, y, c = _mesh_pos()
        me = 4 * x + 2 * y + c
        out = []
        for a, (src, dst) in enumerate(zip(src_refs, dst_refs)):
            out.append(pltpu.make_async_copy(src if self.gather[a] else src.at[me], dst.at[me], local_sems.at[a]))
            for k in range(1, N_DEV):
                tx, ty, tc = x ^ ((k >> 2) & 1), y ^ ((k >> 1) & 1), c ^ (k & 1)
                out.append(pltpu.make_async_remote_copy(
                    src_ref=src if self.gather[a] else src.at[4 * tx + 2 * ty + tc], dst_ref=dst.at[me],
                    send_sem=send_sems.at[7 * a + k - 1], recv_sem=recv_sems.at[7 * a + k - 1],
                    device_id=(tx, ty, tc), device_id_type=pl.DeviceIdType.MESH))
        return out

    def run(self, refs, first, last):
        if not self.n:
            return
        src_refs, dst_refs, sems = refs

        @pl.when(first)
        def _():
            for cp in self.copies(src_refs, dst_refs, sems):
                cp.start()

        @pl.when(last)
        def _():
            for cp in self.copies(src_refs, dst_refs, sems):
                cp.wait()


def _split_refs(refs, n_in, n_out, exch):
    ins = refs[:n_in]
    srcs = refs[n_in:n_in + exch.n]
    outs = refs[n_in + exch.n:n_in + exch.n + n_out]
    dsts = refs[n_in + exch.n + n_out:n_in + 2 * exch.n + n_out]
    rest = refs[n_in + 2 * exch.n + n_out:]
    sems = rest[len(rest) - 3:] if exch.n else ()
    scratch = rest[:len(rest) - 3] if exch.n else rest
    return ins, outs, scratch, (srcs, dsts, sems)


def _final_exchange(sources):
    exch = _Exchange(sources)

    def body(*refs):
        _, _, _, xrefs = _split_refs(refs, 0, 0, exch)
        for cp in exch.copies(*xrefs):
            cp.start()
        for cp in exch.copies(*xrefs):
            cp.wait()

    return pl.pallas_call(
        body, name="final_exchange", out_shape=exch.out_shapes(),
        in_specs=[_ANY] * exch.n, out_specs=[_ANY] * exch.n, scratch_shapes=exch.scratch(),
    )(*exch.sources)


def _all_gather(shard, name):
    m_per, n = shard.shape

    def body(x_ref, out_ref, send_sems, recv_sems, local_sem):
        x, y, c = _mesh_pos()
        me, sibling = (x, y, c), (x, y, 1 - c)
        chips = [(1 - x, y), (x, 1 - y), (1 - x, 1 - y)]

        def rows(px, py, pc):
            return out_ref.at[pl.ds((4 * px + 2 * py + pc) * m_per, m_per), :]

        def copy(k, block, to, src=None):
            return pltpu.make_async_remote_copy(
                src_ref=rows(*block) if src is None else src, dst_ref=rows(*block),
                send_sem=send_sems.at[k], recv_sem=recv_sems.at[k],
                device_id=to, device_id_type=pl.DeviceIdType.MESH)

        mine = pltpu.make_async_copy(x_ref, rows(*me), local_sem)
        mine.start()
        first = [copy(0, me, sibling, src=x_ref)]
        first += [copy(1 + j, me, (*chip, c), src=x_ref) for j, chip in enumerate(chips)]
        for cp in first:
            cp.start()
        passed = [copy(4 + j, (*chip, c), sibling) for j, chip in enumerate(chips)]
        for j, chip in enumerate(chips):
            copy(1 + j, (*chip, c), me).wait_recv()
            passed[j].start()
        copy(0, sibling, me).wait_recv()
        for j, chip in enumerate(chips):
            copy(4 + j, (*chip, 1 - c), me).wait_recv()
        for cp in first + passed:
            cp.wait_send()
        mine.wait()

    return pl.pallas_call(
        body, name=name,
        out_shape=jax.ShapeDtypeStruct((N_DEV * m_per, n), shard.dtype),
        in_specs=[pl.BlockSpec(memory_space=pltpu.VMEM)],
        out_specs=pl.BlockSpec(memory_space=pltpu.VMEM),
        scratch_shapes=[pltpu.SemaphoreType.DMA((7,)), pltpu.SemaphoreType.DMA((7,)), pltpu.SemaphoreType.DMA],
        compiler_params=pltpu.CompilerParams(vmem_limit_bytes=VMEM_LIMIT),
    )(shard)


def _by_group(lane, v2, v4, v8, v16):
    return jnp.where(lane < 64, v2, jnp.where(lane < 128, v4, jnp.where(lane < 192, v8, v16)))


def _pool_count(t0, n):
    lane = lax.broadcasted_iota(jnp.int32, (1, POOL_WIDTH), 1)
    t = (t0 + lax.broadcasted_iota(jnp.int32, (n, 1), 0)).astype(F32)
    wnd = _by_group(lane, 2.0, 4.0, 8.0, 16.0)
    return jnp.minimum(t + 1.0, wnd)


def _pool_diff(u_ext, t0, ts):
    lane = lax.broadcasted_iota(jnp.int32, (1, POOL_WIDTH), 1)
    s2 = u_ext + pltpu.roll(u_ext, 1, 0)
    s4 = s2 + pltpu.roll(s2, 2, 0)
    s8 = s4 + pltpu.roll(s4, 4, 0)
    s16 = s8 + pltpu.roll(s8, 8, 0)
    pooled = _by_group(lane, s2, s4, s8, s16)[HALO:]
    return pooled / _pool_count(t0, ts) - u_ext[HALO:]


def _pool_diff_bwd(w, ts):
    n = w.shape[0]
    lane = lax.broadcasted_iota(jnp.int32, (1, POOL_WIDTH), 1)
    f2 = w + pltpu.roll(w, n - 1, 0)
    f4 = f2 + pltpu.roll(f2, n - 2, 0)
    f8 = f4 + pltpu.roll(f4, n - 4, 0)
    f16 = f8 + pltpu.roll(f8, n - 8, 0)
    return _by_group(lane, f2, f4, f8, f16)[:ts]


CONV_CHUNK = 64


def _conv_taps():
    return [(8 * m + r, r, m) for r in range(8) for m in range(4) if 8 * m + r < CONV_KERNEL]


def _store_shifted(dst_ref, x, up):
    n = x.shape[0]
    for r in range(8):
        dst_ref[r] = x if r == 0 else pltpu.roll(x, n - r if up else r, 0)


def _anticausal_conv(src_ref, dw_ref, out_ref, n_out):
    def chunk(c, carry):
        t0 = pl.multiple_of(c * CONV_CHUNK, CONV_CHUNK)
        acc = None
        for d, r, m in _conv_taps():
            term = dw_ref[pl.ds(CONV_KERNEL - 1 - d, 1), :] * src_ref[r, pl.ds(t0 + 8 * m, CONV_CHUNK), :]
            acc = term if acc is None else acc + term
        out_ref[pl.ds(t0, CONV_CHUNK), :] = acc
        return carry

    lax.fori_loop(0, n_out // CONV_CHUNK, chunk, 0)


def _depthwise_conv_weight_grad(dout_ref, src_ref, acc_ref, base, n_rows):
    acc_ref[...] = jnp.zeros_like(acc_ref)

    def chunk(c, carry):
        t0 = pl.multiple_of(c * CONV_CHUNK, CONV_CHUNK)
        g = dout_ref[0, pl.ds(t0, CONV_CHUNK), :]
        for d, r, m in _conv_taps():
            prod = g * src_ref[r, pl.ds(base + t0 - 8 * m, CONV_CHUNK), :]
            acc_ref[CONV_KERNEL - 1 - d] += jnp.sum(prod.reshape(CONV_CHUNK // 8, 8, prod.shape[-1]), axis=0)
        return carry

    lax.fori_loop(0, n_rows // CONV_CHUNK, chunk, 0)


def _layer_norm(cv):
    mu = jnp.mean(cv, axis=-1, keepdims=True)
    xc = cv - mu
    var = jnp.mean(xc * xc, axis=-1, keepdims=True)
    rstd = lax.rsqrt(var + EPS)
    return xc * rstd, rstd


PAIR_ROWS = 2 * BLOCK
PAIR_COLS = 4 * BLOCK
ATTN_BLOCKS_PER_STEP = 4


def _attn_blocks_per_step(s_len):
    qb = ATTN_BLOCKS_PER_STEP
    while (s_len // BLOCK) % qb:
        qb //= 2
    return qb


def _pair_rows(v0, v1):
    r = lax.broadcasted_iota(jnp.int32, (PAIR_ROWS, 1), 0)
    return jnp.where(r < BLOCK, v0, v1)


def _fill_attn_bias(bias_ref):
    rows = lax.broadcasted_iota(jnp.int32, (PAIR_ROWS, PAIR_COLS), 0)
    cols = lax.broadcasted_iota(jnp.int32, (PAIR_ROWS, PAIR_COLS), 1)
    key = cols & (2 * BLOCK - 1)
    dist = BLOCK + (rows & (BLOCK - 1)) - key
    in_band = (dist >= 0) & (dist < BLOCK)
    distf = dist.astype(F32)
    second = cols >= 2 * BLOCK
    for kh in range(N_KV_HEADS):
        slope_of = lambda j, a: 2.0 ** -(Q_PER_KV * kh + 2 * j + a + 1)
        slope = jnp.where(rows < BLOCK, jnp.where(second, slope_of(0, 1), slope_of(0, 0)),
                          jnp.where(second, slope_of(1, 1), slope_of(1, 0)))
        bias = -slope * distf
        bias_ref[0, kh] = jnp.where(in_band & (key >= BLOCK), bias, NEG)
        bias_ref[1, kh] = jnp.where(in_band, bias, NEG)


def _pair_block_matrix(x, x_swapped, kh):
    lo = lax.broadcasted_iota(jnp.int32, (1, 2 * HEAD_DIM), 1) < HEAD_DIM
    in_lo, in_hi = (x, x_swapped) if kh == 0 else (x_swapped, x)
    return jnp.concatenate([jnp.where(lo, in_lo, 0.0), jnp.where(lo, 0.0, in_hi)], axis=0).astype(MXU_DTYPE)


def _pair_queries(q, kh):
    return (_pair_stack(q, kh) * SCALE).astype(MXU_DTYPE)


def _pair_stack(a, kh):
    return jnp.concatenate([a[:, 2 * BLOCK * kh: 2 * BLOCK * kh + BLOCK],
                            a[:, 2 * BLOCK * kh + BLOCK: 2 * BLOCK * (kh + 1)]], axis=0)


def _pair_unstack(parts):
    return jnp.concatenate([p[BLOCK * j: BLOCK * (j + 1)] for p in parts for j in range(2)], axis=-1)


def _pair_softmax(s, kh, sinks_ref):
    ps, p_sinks = [], []
    for a in range(2):
        sa = s[:, 2 * BLOCK * a: 2 * BLOCK * (a + 1)]
        sink = _pair_rows(sinks_ref[Q_PER_KV * kh + a], sinks_ref[Q_PER_KV * kh + 2 + a])
        m = jnp.maximum(jnp.max(sa, axis=-1, keepdims=True), sink)
        e = jnp.exp(sa - m)
        es = jnp.exp(sink - m)
        inv = 1.0 / (jnp.sum(e, axis=-1, keepdims=True) + es)
        ps.append(e * inv)
        p_sinks.append(es * inv)
    return jnp.concatenate(ps, axis=-1), p_sinks


def _fold_pair_halves(t):
    lo = lax.broadcasted_iota(jnp.int32, (1, 2 * HEAD_DIM), 1) < HEAD_DIM
    u = jnp.where(lo, t[:2 * BLOCK], t[2 * BLOCK:])
    return u + pltpu.roll(u, HEAD_DIM, 1)


def _in_proj(x, g, w_t, carried=()):
    s_len, d = x.shape
    ts = _row_tile(s_len, ROW_TILE)
    nt = s_len // ts
    exch = _Exchange(carried)

    def body(*refs):
        (x_ref, g_ref, w_ref), (o_ref,), _, xrefs = _split_refs(refs, 3, 1, exch)
        i = pl.program_id(0)
        exch.run(xrefs, i == 0, i == nt - 1)
        xv = x_ref[...]
        r = lax.rsqrt(jnp.mean(xv * xv, axis=-1, keepdims=True) + EPS)
        o_ref[...] = _mm(xv * r * g_ref[...], w_ref[...], _NT)

    res = pl.pallas_call(
        body, name="in_proj_carrier" if exch.n else "in_proj", grid=(nt,),
        in_specs=[pl.BlockSpec((ts, d), lambda i: (i, 0)), _full((1, d)), _full((D_IN, d))] + [_ANY] * exch.n,
        out_specs=[pl.BlockSpec((ts, D_IN), lambda i: (i, 0))] + [_ANY] * exch.n,
        out_shape=[jax.ShapeDtypeStruct((s_len, D_IN), F32)] + exch.out_shapes(),
        scratch_shapes=exch.scratch(),
        compiler_params=_params(1),
    )(x, g, w_t, *exch.sources)
    return res[0], res[1:]


def _poolconv_fwd(proj, wp, scale, dw, cb, lng, lnb, pw, carried=()):
    s_len = proj.shape[0]
    ts = _row_tile(s_len, ROW_TILE)
    hb = ts // HALO
    nt = s_len // ts
    exch = _Exchange(carried)

    def body(*refs):
        ins, (y_ref, cv_ref), _, xrefs = _split_refs(refs, 9, 2, exch)
        p_ref, ph_ref, wp_ref, sc_ref, dw_ref, cb_ref, lng_ref, lnb_ref, pw_ref = ins
        i = pl.program_id(0)
        exch.run(xrefs, i == 0, i == nt - 1)
        cur = p_ref[...]
        halo = jnp.where(i > 0, ph_ref[...], 0.0)
        ext = jnp.concatenate([halo, cur], axis=0)
        diff = _pool_diff(ext[:, 0:256], i * ts, ts)
        gp = cur[:, 256:512]
        y_pool = _mm(diff, wp_ref[...]) * sc_ref[...] * (gp * _sig(gp))
        hh = ext[:, 512:768] * _sig(ext[:, 768:1024])
        shifted = [hh if r == 0 else pltpu.roll(hh, r, 0) for r in range(8)]
        cv = cb_ref[...]
        for dist, r, m in _conv_taps():
            cv = cv + dw_ref[pl.ds(CONV_KERNEL - 1 - dist, 1), :] * shifted[r][HALO - 8 * m: HALO - 8 * m + ts]
        cv_ref[...] = cv
        n, _ = _layer_norm(cv)
        z = n * lng_ref[...] + lnb_ref[...]
        gc = cur[:, 1024:1280]
        y_conv = _mm(z * _sig(z), pw_ref[...]) * (gc * _sig(gc))
        y_ref[...] = jnp.concatenate([y_pool, y_conv], axis=-1).astype(y_ref.dtype)

    vec = _full((1, 256))
    res = pl.pallas_call(
        body, name="poolconv_fwd_carrier" if exch.n else "poolconv_fwd", grid=(nt,),
        in_specs=[pl.BlockSpec((ts, HALF_IN), lambda i: (i, 0)),
                  pl.BlockSpec((HALO, HALF_IN), lambda i: (jnp.maximum(i * hb - 1, 0), 0)),
                  _full((256, 256)), vec, _full((CONV_TAPS_PAD, 256)), vec, vec, vec, _full((256, 256))]
        + [_ANY] * exch.n,
        out_specs=[pl.BlockSpec((ts, 512), lambda i: (i, 0)), pl.BlockSpec((ts, 256), lambda i: (i, 0))]
        + [_ANY] * exch.n,
        out_shape=[jax.ShapeDtypeStruct((s_len, 512), MXU_DTYPE), jax.ShapeDtypeStruct((s_len, 256), F32)]
        + exch.out_shapes(),
        scratch_shapes=exch.scratch(),
        compiler_params=_params(1),
    )(proj, proj, wp, scale, dw, cb, lng, lnb, pw, *exch.sources)
    return res[:2], res[2:]


def _attn_fwd(proj, sinks, carried=()):
    s_len = proj.shape[0]
    qb = _attn_blocks_per_step(s_len)
    ts = qb * BLOCK
    nt = s_len // ts
    exch = _Exchange(carried)

    def body(*refs):
        (p_ref, kvp_ref, sinks_ref), (y_ref,), (bias_ref,), xrefs = _split_refs(refs, 3, 1, exch)
        i = pl.program_id(0)
        exch.run(xrefs, i == 0, i == nt - 1)

        @pl.when(i == 0)
        def _():
            _fill_attn_bias(bias_ref)

        probs = [(b, kh) for b in range(qb) for kh in range(N_KV_HEADS)]
        scores, v_bds = {}, {}
        for b in range(qb):
            r0 = BLOCK * b
            kv_prev = kvp_ref[...] if b == 0 else p_ref[r0 - BLOCK:r0, 512:768]
            kv2 = jnp.concatenate([kv_prev, p_ref[r0:r0 + BLOCK, 512:768]], axis=0)
            k2, v2 = kv2[:, :BLOCK], kv2[:, BLOCK:]
            k2_swapped, v2_swapped = pltpu.roll(k2, HEAD_DIM, 1), pltpu.roll(v2, HEAD_DIM, 1)
            variant = jnp.where(i == 0, 0, 1) if b == 0 else 1
            q = p_ref[r0:r0 + BLOCK, 0:512]
            for kh in range(N_KV_HEADS):
                k_bd = _pair_block_matrix(k2, k2_swapped, kh)
                v_bds[b, kh] = _pair_block_matrix(v2, v2_swapped, kh)
                scores[b, kh] = _mm(_pair_queries(q, kh), k_bd, _NT) + bias_ref[variant, kh]
        ps = {pr: _pair_softmax(scores[pr], pr[1], sinks_ref)[0] for pr in probs}
        outs = {pr: _mm(ps[pr], v_bds[pr]) for pr in probs}
        for b in range(qb):
            r0 = BLOCK * b
            ga = p_ref[r0:r0 + BLOCK, 768:1280]
            ya = _pair_unstack([outs[b, kh] for kh in range(N_KV_HEADS)])
            y_ref[r0:r0 + BLOCK, :] = (ya * (ga * _sig(ga))).astype(y_ref.dtype)

    res = pl.pallas_call(
        body, name="attn_fwd_carrier" if exch.n else "attn_fwd", grid=(nt,),
        in_specs=[pl.BlockSpec((ts, HALF_IN), lambda i: (i, 1)),
                  pl.BlockSpec((BLOCK, 256), lambda i: (jnp.maximum(i * qb - 1, 0), 7)),
                  pl.BlockSpec(memory_space=pltpu.SMEM)] + [_ANY] * exch.n,
        out_specs=[pl.BlockSpec((ts, 512), lambda i: (i, 0))] + [_ANY] * exch.n,
        out_shape=[jax.ShapeDtypeStruct((s_len, 512), MXU_DTYPE)] + exch.out_shapes(),
        scratch_shapes=[pltpu.VMEM((2, N_KV_HEADS, PAIR_ROWS, PAIR_COLS), F32)] + exch.scratch(),
        compiler_params=_params(1),
    )(proj, proj, sinks, *exch.sources)
    return res[0], res[1:]


def _out_proj(x, y_pc, y_at, w_out):
    s_len, d = x.shape
    ts = _row_tile(s_len, ROW_TILE)

    def body(x_ref, a_ref, b_ref, w_ref, o_ref):
        y = jnp.concatenate([a_ref[...], b_ref[...]], axis=-1)
        o_ref[...] = x_ref[...] + _mm(y, w_ref[...])

    return pl.pallas_call(
        body, name="out_proj", grid=(s_len // ts,),
        in_specs=[pl.BlockSpec((ts, d), lambda i: (i, 0)), pl.BlockSpec((ts, 512), lambda i: (i, 0)),
                  pl.BlockSpec((ts, 512), lambda i: (i, 0)), _full((D_MIX, d))],
        out_specs=pl.BlockSpec((ts, d), lambda i: (i, 0)),
        out_shape=jax.ShapeDtypeStruct((s_len, d), F32),
        compiler_params=_params(1),
    )(x, y_pc, y_at, w_out)


def _out_proj_loss(x, y_pc, y_at, w_out, target, g):
    s_len, d = x.shape
    ts = _row_tile(s_len, ROW_TILE)

    def body(x_ref, a_ref, b_ref, w_ref, t_ref, g_ref, sq_ref, dg_ref, dx_ref):
        i = pl.program_id(0)
        y = jnp.concatenate([a_ref[...], b_ref[...]], axis=-1)
        xv = x_ref[...] + _mm(y, w_ref[...])
        gv = g_ref[...]
        r = lax.rsqrt(jnp.mean(xv * xv, axis=-1, keepdims=True) + EPS)
        xr = xv * r
        err = xr * gv - t_ref[...]
        dout = err * (1.0 / d)
        w = dout * gv
        dx_ref[...] = r * (w - xr * jnp.mean(w * xr, axis=-1, keepdims=True))

        @pl.when(i == 0)
        def _():
            sq_ref[...] = jnp.zeros_like(sq_ref)
            dg_ref[...] = jnp.zeros_like(dg_ref)

        sq = jnp.sum(jnp.sum(err * err, axis=-1, keepdims=True), axis=0, keepdims=True)
        sq_ref[...] += jnp.broadcast_to(sq, sq_ref.shape)
        dg_ref[...] += jnp.sum(dout * xr, axis=0, keepdims=True)

    tile = pl.BlockSpec((ts, d), lambda i: (i, 0))
    half = pl.BlockSpec((ts, 512), lambda i: (i, 0))
    return pl.pallas_call(
        body, name="out_proj_loss", grid=(s_len // ts,),
        in_specs=[tile, half, half, _full((D_MIX, d)), tile, _full((1, d))],
        out_specs=[_full((1, 128)), _full((1, d)), tile],
        out_shape=[jax.ShapeDtypeStruct((1, 128), F32), jax.ShapeDtypeStruct((1, d), F32),
                   jax.ShapeDtypeStruct((s_len, d), F32)],
        compiler_params=_params(1),
    )(x, y_pc, y_at, w_out, target, g)


def _out_bwd(dxo, y_pc, y_at, w_out, carried=()):
    s_len, d = dxo.shape
    ts = _row_tile(s_len, ROW_TILE)
    nt = s_len // ts
    exch = _Exchange(carried)

    def body(*refs):
        (dx_ref, a_ref, b_ref, w_ref), (dy_ref, gw_ref), (acc_ref,), xrefs = _split_refs(refs, 4, 2, exch)
        i = pl.program_id(0)
        exch.run(xrefs, i == 0, i == nt - 1)
        dxv = dx_ref[...].astype(MXU_DTYPE)
        dy_ref[...] = _mm(dxv, w_ref[...], _NT)

        @pl.when(i == 0)
        def _():
            acc_ref[...] = jnp.zeros_like(acc_ref)

        y = jnp.concatenate([a_ref[...], b_ref[...]], axis=-1)
        acc_ref[...] += _mm(y, dxv, _TN)

        @pl.when(i == nt - 1)
        def _():
            gw_ref[...] = acc_ref[...].astype(gw_ref.dtype)

    res = pl.pallas_call(
        body, name="out_bwd_carrier" if exch.n else "out_bwd", grid=(nt,),
        in_specs=[pl.BlockSpec((ts, d), lambda i: (i, 0)), pl.BlockSpec((ts, 512), lambda i: (i, 0)),
                  pl.BlockSpec((ts, 512), lambda i: (i, 0)), _full((D_MIX, d))] + [_ANY] * exch.n,
        out_specs=[pl.BlockSpec((ts, D_MIX), lambda i: (i, 0)), _full((D_MIX, d))] + [_ANY] * exch.n,
        out_shape=[jax.ShapeDtypeStruct((s_len, D_MIX), F32), jax.ShapeDtypeStruct((D_MIX, d), EXCHANGE_DTYPE)]
        + exch.out_shapes(),
        scratch_shapes=[pltpu.VMEM((D_MIX, d), F32)] + exch.scratch(),
        compiler_params=_params(1),
    )(dxo, y_pc, y_at, w_out, *exch.sources)
    return res[:2], res[2:]


def _poolconv_bwd(proj, cv, dy, wp, scale, dw, lng, lnb, pw, carried=()):
    s_len = proj.shape[0]
    ts = _row_tile(s_len, ROW_TILE)
    hb = ts // HALO
    nt = s_len // ts
    last_halo = s_len // HALO - 1
    n = ts + HALO
    exch = _Exchange(carried)

    def body(*refs):
        ins, outs, (up_ref, down_ref, dhh_ref, gdw_acc_ref), xrefs = _split_refs(refs, 13, 5, exch)
        (p_ref, ph_ref, pn_ref, cv_ref, cvn_ref, dy_ref, dyn_ref, wp_ref, sc_ref, dw_ref, lng_ref, lnb_ref,
         pw_ref) = ins
        da_ref, gwp_ref, gpw_ref, gdw_ref, gvec_ref = outs
        i = pl.program_id(0)
        exch.run(xrefs, i == 0, i == nt - 1)
        has_next = i < nt - 1
        cur = p_ref[...]
        prev = jnp.where(i > 0, ph_ref[...], 0.0)
        nxt = jnp.where(has_next, pn_ref[...], 0.0)
        dyx = jnp.concatenate([dy_ref[...], jnp.where(has_next, dyn_ref[...], 0.0)], axis=0)
        row = lax.broadcasted_iota(jnp.int32, (n, 1), 0)
        in_seq = (row < ts) | has_next
        scale_v = sc_ref[...]

        cvx = jnp.concatenate([cv_ref[...], jnp.where(has_next, cvn_ref[...], 0.0)], axis=0)
        nrm, rstd = _layer_norm(cvx)
        z = nrm * lng_ref[...] + lnb_ref[...]
        sz = _sig(z)
        sw = z * sz
        gc = jnp.concatenate([cur[:, 1024:1280], nxt[:, 1024:1280]], axis=0)
        sgc = _sig(gc)
        yc = _mm(sw, pw_ref[...])
        dyc = dyx[:, 256:512]
        d_yc = dyc * (gc * sgc)
        d_gc = (dyc * yc * _dsilu(gc, sgc))[:ts]
        d_z = _mm(d_yc, pw_ref[...], _NT) * _dsilu(z, sz)
        d_n = d_z * lng_ref[...]
        d_cv = rstd * (d_n - jnp.mean(d_n, axis=-1, keepdims=True)
                       - nrm * jnp.mean(d_n * nrm, axis=-1, keepdims=True))
        d_cv = jnp.where(in_seq, d_cv, 0.0)
        _store_shifted(up_ref, d_cv, up=True)
        _anticausal_conv(up_ref, dw_ref, dhh_ref, ts)
        d_hh = dhh_ref[...]
        a2 = jnp.concatenate([prev[:, 512:768], cur[:, 512:768]], axis=0)
        sb2 = _sig(jnp.concatenate([prev[:, 768:1024], cur[:, 768:1024]], axis=0))
        _store_shifted(down_ref, a2 * sb2, up=False)
        a_c, sb_c = a2[HALO:], sb2[HALO:]
        d_a = d_hh * sb_c
        d_b = d_hh * a_c * sb_c * (1.0 - sb_c)
        d_cv_t = d_cv[:ts]
        _depthwise_conv_weight_grad(up_ref, down_ref, gdw_acc_ref, HALO, ts)

        u_ext = jnp.concatenate([prev[:, 0:256], cur[:, 0:256]], axis=0)
        diff = _pool_diff(u_ext, i * ts, ts)
        raw = _mm(diff, wp_ref[...])
        gp = jnp.concatenate([cur[:, 256:512], nxt[:, 256:512]], axis=0)
        sgp = _sig(gp)
        dyp = dyx[:, 0:256]
        d_yp = dyp * (gp * sgp)
        d_gp = dyp[:ts] * (raw * scale_v) * _dsilu(gp, sgp)[:ts]
        d_raw = d_yp * scale_v
        d_diff = _mm(d_raw, wp_ref[...], _NT)
        w = jnp.where(in_seq, d_diff / _pool_count(i * ts, n), 0.0)
        d_u = _pool_diff_bwd(w, ts) - d_diff[:ts]

        da_ref[...] = jnp.concatenate([d_u, d_gp, d_a, d_b, d_gc], axis=-1).astype(da_ref.dtype)

        @pl.when(i == 0)
        def _():
            gwp_ref[...] = jnp.zeros_like(gwp_ref)
            gpw_ref[...] = jnp.zeros_like(gpw_ref)
            gdw_ref[...] = jnp.zeros_like(gdw_ref)
            gvec_ref[...] = jnp.zeros_like(gvec_ref)

        gwp_ref[...] += _mm(diff, d_raw[:ts], _TN)
        gpw_ref[...] += _mm(sw[:ts], d_yc[:ts], _TN)
        gdw_ref[...] += jnp.sum(gdw_acc_ref[...], axis=1)
        zero_row = jnp.zeros((1, 256), F32)
        gvec_ref[...] += jnp.concatenate([
            jnp.sum(d_yp[:ts] * raw, axis=0, keepdims=True),
            jnp.sum(d_cv_t, axis=0, keepdims=True),
            jnp.sum((d_z * nrm)[:ts], axis=0, keepdims=True),
            jnp.sum(d_z[:ts], axis=0, keepdims=True),
            zero_row, zero_row, zero_row, zero_row], axis=0)

    vec = _full((1, 256))
    nxt_halo = lambda i: (jnp.minimum((i + 1) * hb, last_halo), 0)
    res = pl.pallas_call(
        body, name="poolconv_bwd_carrier" if exch.n else "poolconv_bwd", grid=(nt,),
        in_specs=[pl.BlockSpec((ts, HALF_IN), lambda i: (i, 0)),
                  pl.BlockSpec((HALO, HALF_IN), lambda i: (jnp.maximum(i * hb - 1, 0), 0)),
                  pl.BlockSpec((HALO, HALF_IN), nxt_halo),
                  pl.BlockSpec((ts, 256), lambda i: (i, 0)), pl.BlockSpec((HALO, 256), nxt_halo),
                  pl.BlockSpec((ts, 512), lambda i: (i, 0)), pl.BlockSpec((HALO, 512), nxt_halo),
                  _full((256, 256)), vec, _full((CONV_TAPS_PAD, 256)), vec, vec, _full((256, 256))]
        + [_ANY] * exch.n,
        out_specs=[pl.BlockSpec((ts, HALF_IN), lambda i: (i, 0)), _full((256, 256)), _full((256, 256)),
                   _full((CONV_TAPS_PAD, 256)), _full((8, 256))] + [_ANY] * exch.n,
        out_shape=[jax.ShapeDtypeStruct((s_len, HALF_IN), MXU_DTYPE), jax.ShapeDtypeStruct((256, 256), F32),
                   jax.ShapeDtypeStruct((256, 256), F32), jax.ShapeDtypeStruct((CONV_TAPS_PAD, 256), F32),
                   jax.ShapeDtypeStruct((8, 256), F32)] + exch.out_shapes(),
        scratch_shapes=[pltpu.VMEM((8, n, CONV_WIDTH), F32), pltpu.VMEM((8, n, CONV_WIDTH), F32),
                        pltpu.VMEM((ts, CONV_WIDTH), F32), pltpu.VMEM((CONV_TAPS_PAD, 8, CONV_WIDTH), F32)]
        + exch.scratch(),
        compiler_params=_params(1),
    )(proj, proj, proj, cv, cv, dy, dy, wp, scale, dw, lng, lnb, pw, *exch.sources)
    return res[:5], res[5:]


DQ0, DKC0, DVC0, DKP0, DVP0, DGA0, DATTN_W = 0, 512, 640, 768, 896, 1024, 1536


def _attn_bwd(proj, dy, sinks, carried=()):
    s_len = proj.shape[0]
    qb = _attn_blocks_per_step(s_len)
    ts = qb * BLOCK
    nt = s_len // ts
    exch = _Exchange(carried)

    def body(*refs):
        (p_ref, kvp_ref, dy_ref, sinks_ref), (o_ref, gs_ref), (bias_ref,), xrefs = _split_refs(refs, 4, 2, exch)
        i = pl.program_id(0)
        exch.run(xrefs, i == 0, i == nt - 1)

        @pl.when(i == 0)
        def _():
            _fill_attn_bias(bias_ref)
            gs_ref[...] = jnp.zeros_like(gs_ref)

        lo = lax.broadcasted_iota(jnp.int32, (1, 2 * HEAD_DIM), 1) < HEAD_DIM
        probs = [(b, kh) for b in range(qb) for kh in range(N_KV_HEADS)]
        kv_heads = range(N_KV_HEADS)
        scores, k_bds, v_bds, q2s, do2s, dyas, gas, sgas = {}, {}, {}, {}, {}, {}, {}, {}
        for b in range(qb):
            r0 = BLOCK * b
            kv_prev = kvp_ref[...] if b == 0 else p_ref[r0 - BLOCK:r0, 512:768]
            kv2 = jnp.concatenate([kv_prev, p_ref[r0:r0 + BLOCK, 512:768]], axis=0)
            k2, v2 = kv2[:, :BLOCK], kv2[:, BLOCK:]
            k2_swapped, v2_swapped = pltpu.roll(k2, HEAD_DIM, 1), pltpu.roll(v2, HEAD_DIM, 1)
            variant = jnp.where(i == 0, 0, 1) if b == 0 else 1
            q = p_ref[r0:r0 + BLOCK, 0:512]
            gas[b] = p_ref[r0:r0 + BLOCK, 768:1280]
            dyas[b] = dy_ref[r0:r0 + BLOCK, :]
            sgas[b] = _sig(gas[b])
            d_o = dyas[b] * (gas[b] * sgas[b])
            for kh in kv_heads:
                k_bds[b, kh] = _pair_block_matrix(k2, k2_swapped, kh)
                v_bds[b, kh] = _pair_block_matrix(v2, v2_swapped, kh)
                q2s[b, kh] = _pair_queries(q, kh)
                do2s[b, kh] = _pair_stack(d_o, kh)
                scores[b, kh] = _mm(q2s[b, kh], k_bds[b, kh], _NT) + bias_ref[variant, kh]
        softmaxes = {pr: _pair_softmax(scores[pr], pr[1], sinks_ref) for pr in probs}
        ps = {pr: softmaxes[pr][0] for pr in probs}
        outs = {pr: _mm(ps[pr], v_bds[pr]) for pr in probs}
        dps = {pr: _mm(do2s[pr], v_bds[pr], _NT) for pr in probs}
        dss = {}
        d_sinks = [None] * N_Q_HEADS
        for pr in probs:
            p, dp, kh = ps[pr], dps[pr], pr[1]
            ds_halves = []
            for a in range(2):
                cols = slice(2 * BLOCK * a, 2 * BLOCK * (a + 1))
                delta = jnp.sum(p[:, cols] * dp[:, cols], axis=-1, keepdims=True)
                ds_halves.append(p[:, cols] * (dp[:, cols] - delta))
                dsink = -softmaxes[pr][1][a] * delta
                for j in range(2):
                    part = jnp.sum(dsink[BLOCK * j: BLOCK * (j + 1)], axis=0, keepdims=True)
                    h = Q_PER_KV * kh + 2 * j + a
                    d_sinks[h] = part if d_sinks[h] is None else d_sinks[h] + part
            dss[pr] = jnp.concatenate(ds_halves, axis=-1)
        dqs = {pr: _mm(dss[pr], k_bds[pr]) * SCALE for pr in probs}
        dks = {pr: _fold_pair_halves(_mm(dss[pr], q2s[pr], _TN)) for pr in probs}
        dvs = {pr: _fold_pair_halves(_mm(ps[pr], do2s[pr], _TN)) for pr in probs}
        for b in range(qb):
            r0 = BLOCK * b
            dk = jnp.where(lo, dks[b, 0], dks[b, 1])
            dv = jnp.where(lo, dvs[b, 0], dvs[b, 1])
            d_ga = dyas[b] * _pair_unstack([outs[b, kh] for kh in kv_heads]) * _dsilu(gas[b], sgas[b])
            o_ref[r0:r0 + BLOCK, :] = jnp.concatenate(
                [_pair_unstack([dqs[b, kh] for kh in kv_heads]), dk[BLOCK:], dv[BLOCK:], dk[:BLOCK], dv[:BLOCK],
                 d_ga], axis=-1)
        gs_ref[...] += jnp.broadcast_to(jnp.concatenate(d_sinks, axis=0), gs_ref.shape)

    res = pl.pallas_call(
        body, name="attn_bwd_carrier" if exch.n else "attn_bwd", grid=(nt,),
        in_specs=[pl.BlockSpec((ts, HALF_IN), lambda i: (i, 1)),
                  pl.BlockSpec((BLOCK, 256), lambda i: (jnp.maximum(i * qb - 1, 0), 7)),
                  pl.BlockSpec((ts, 512), lambda i: (i, 1)),
                  pl.BlockSpec(memory_space=pltpu.SMEM)] + [_ANY] * exch.n,
        out_specs=[pl.BlockSpec((ts, DATTN_W), lambda i: (i, 0)), _full((N_Q_HEADS, 128))] + [_ANY] * exch.n,
        out_shape=[jax.ShapeDtypeStruct((s_len, DATTN_W), F32), jax.ShapeDtypeStruct((N_Q_HEADS, 128), F32)]
        + exch.out_shapes(),
        scratch_shapes=[pltpu.VMEM((2, N_KV_HEADS, PAIR_ROWS, PAIR_COLS), F32)] + exch.scratch(),
        compiler_params=_params(1),
    )(proj, proj, dy, sinks, *exch.sources)
    return res[:2], res[2:]


def _in_bwd(da, dattn, x, dxo, g, w_t, carried=()):
    s_len, d = x.shape
    ts = _row_tile(s_len, IN_BWD_TILE)
    bpt = ts // BLOCK
    nt = s_len // ts
    last_block = s_len // BLOCK - 1
    exch = _Exchange(carried)

    def body(*refs):
        ins, (dx_ref, dg_ref, gw_ref), (acc_ref,), xrefs = _split_refs(refs, 7, 3, exch)
        da_ref, dat_ref, nxt_ref, x_ref, dxo_ref, g_ref, w_ref = ins
        i = pl.program_id(0)
        exch.run(xrefs, i == 0, i == nt - 1)
        dat = dat_ref[...]
        nxt = jnp.where(i < nt - 1, nxt_ref[...], 0.0)
        shifted = jnp.concatenate([dat[BLOCK:, DKP0:DGA0], nxt], axis=0) if bpt > 1 else nxt
        dkv = dat[:, DKC0:DKP0] + shifted
        dproj = jnp.concatenate([da_ref[...], dat[:, DQ0:DKC0].astype(MXU_DTYPE), dkv.astype(MXU_DTYPE),
                                 dat[:, DGA0:DATTN_W].astype(MXU_DTYPE)], axis=-1)
        d_h = _mm(dproj, w_ref[...])
        xv = x_ref[...]
        gv = g_ref[...]
        r = lax.rsqrt(jnp.mean(xv * xv, axis=-1, keepdims=True) + EPS)
        xr = xv * r
        w = d_h * gv
        dx_ref[...] = dxo_ref[...] + r * (w - xr * jnp.mean(w * xr, axis=-1, keepdims=True))

        @pl.when(i == 0)
        def _():
            dg_ref[...] = jnp.zeros_like(dg_ref)
            acc_ref[...] = jnp.zeros_like(acc_ref)

        dg_ref[...] += jnp.sum(d_h * xr, axis=0, keepdims=True)
        acc_ref[...] += _mm(dproj, xr * gv, _TN)

        @pl.when(i == nt - 1)
        def _():
            gw_ref[...] = acc_ref[...].astype(gw_ref.dtype)

    res = pl.pallas_call(
        body, name="in_bwd_carrier" if exch.n else "in_bwd", grid=(nt,),
        in_specs=[pl.BlockSpec((ts, HALF_IN), lambda i: (i, 0)),
                  pl.BlockSpec((ts, DATTN_W), lambda i: (i, 0)),
                  pl.BlockSpec((BLOCK, 256), lambda i: (jnp.minimum((i + 1) * bpt, last_block), 3)),
                  pl.BlockSpec((ts, d), lambda i: (i, 0)), pl.BlockSpec((ts, d), lambda i: (i, 0)),
                  _full((1, d)), _full((D_IN, d))] + [_ANY] * exch.n,
        out_specs=[pl.BlockSpec((ts, d), lambda i: (i, 0)), _full((1, d)), _full((D_IN, d))] + [_ANY] * exch.n,
        out_shape=[jax.ShapeDtypeStruct((s_len, d), F32), jax.ShapeDtypeStruct((1, d), F32),
                   jax.ShapeDtypeStruct((D_IN, d), EXCHANGE_DTYPE)] + exch.out_shapes(),
        scratch_shapes=[pltpu.VMEM((D_IN, d), F32)] + exch.scratch(),
        compiler_params=_params(1),
    )(da, dattn, dattn, x, dxo, g, w_t, *exch.sources)
    return res[:3], res[3:]


def _sum_partials(p_ref):
    g = p_ref[0].astype(F32)
    for k in range(1, N_DEV):
        g = g + p_ref[k].astype(F32)
    return g


def _adamw_step(g, w, m, v):
    nm = ADAM_B1 * m + (1.0 - ADAM_B1) * g
    nv = ADAM_B2 * v + (1.0 - ADAM_B2) * (g * g)
    m_hat = nm / (1.0 - ADAM_B1 ** ADAM_STEP)
    v_hat = nv / (1.0 - ADAM_B2 ** ADAM_STEP)
    return -ADAM_LR * (m_hat / (jnp.sqrt(v_hat) + ADAM_EPS) + ADAM_WD * w), nm, nv


def _adamw_layers(parts, w, m, v, name):
    def body(*refs):
        p_refs = refs[:DEPTH]
        w_ref, m_ref, v_ref, g_ref, d_ref, nm_ref, nv_ref = refs[DEPTH:]
        for l in range(DEPTH):
            g = _sum_partials(p_refs[l])
            g_ref[l] = g
            d_ref[l], nm_ref[l], nv_ref[l] = _adamw_step(g, w_ref[l], m_ref[l], v_ref[l])

    vmem = pl.BlockSpec(memory_space=pltpu.VMEM)
    shape = jax.ShapeDtypeStruct(w.shape, F32)
    return pl.pallas_call(
        body, name=name, in_specs=[vmem] * (DEPTH + 3), out_specs=[vmem] * 4, out_shape=[shape] * 4,
        compiler_params=pltpu.CompilerParams(vmem_limit_bytes=VMEM_LIMIT),
    )(*parts, w, m, v)


def _adamw(parts, w, m, v, name):
    rows, n = w.shape
    tr = _row_tile(rows, 256)

    def body(p_ref, w_ref, m_ref, v_ref, g_ref, d_ref, nm_ref, nv_ref):
        g = _sum_partials(p_ref)
        g_ref[...] = g
        d_ref[...], nm_ref[...], nv_ref[...] = _adamw_step(g, w_ref[...], m_ref[...], v_ref[...])

    tile = pl.BlockSpec((tr, n), lambda i: (i, 0))
    shape = jax.ShapeDtypeStruct((rows, n), F32)
    return pl.pallas_call(
        body, name=name, grid=(rows // tr,),
        in_specs=[pl.BlockSpec((N_DEV, tr, n), lambda i: (0, i, 0)), tile, tile, tile],
        out_specs=[tile, tile, tile, tile],
        out_shape=[shape, shape, shape, shape],
        compiler_params=_params(1),
    )(parts, w, m, v)


def _pad_rows(a, mult):
    pad = (-a.shape[0]) % mult
    return a if pad == 0 else jnp.concatenate([a, jnp.zeros((pad, a.shape[1]), a.dtype)], axis=0)


def _dw_rows(conv_dw_l):
    return jnp.pad(jnp.swapaxes(conv_dw_l, 0, 1), ((0, 0), (0, CONV_TAPS_PAD - CONV_KERNEL)))


def _pack_small(pw_l, dw_l, d):
    rows = jnp.concatenate([pw_l.reshape(-1, d), _dw_rows(dw_l).reshape(-1, d)], axis=0)
    return _pad_rows(rows, 8)


def _small_slabs(g_pw, g_dw, d):
    a = g_pw.reshape(N_DEV, -1, d)
    b = jnp.swapaxes(g_dw, 0, 1).reshape(N_DEV, -1, d)
    used = a.shape[1] + b.shape[1]
    return jnp.concatenate([a, b, jnp.zeros((N_DEV, (-used) % 8, d), g_pw.dtype)], axis=1)


def _unpack_small(rows, d):
    c = CONV_WIDTH // N_DEV
    n_pw = c * CONV_WIDTH // d
    n_dw = c * CONV_TAPS_PAD // d
    pw = rows[:n_pw].reshape(c, CONV_WIDTH)
    dw = jnp.swapaxes(rows[n_pw:n_pw + n_dw].reshape(c, CONV_TAPS_PAD), 0, 1)[:CONV_KERNEL]
    return pw, dw


def _pack_replicated(ln_g, pool_w, pool_scale, conv_b, conv_ln_g, conv_ln_b, attn_sinks, final_g, scalar, d):
    sinks = jnp.pad(attn_sinks, ((0, 0), (0, 256 - N_Q_HEADS)))
    small = jnp.concatenate([pool_scale, conv_b, conv_ln_g, conv_ln_b, sinks], axis=0)
    small = _pad_rows(small, d // 256)
    last = jnp.pad(scalar.reshape(1, 1), ((0, 0), (0, d - 1)))
    return _pad_rows(jnp.concatenate([ln_g.reshape(-1, d), final_g.reshape(-1, d), pool_w.reshape(-1, d),
                                      small.reshape(-1, d), last], axis=0), 8)


def _unpack_replicated(rows, d):
    n_pool = DEPTH * 4 * POOL_GROUP * POOL_GROUP // d
    n_small = -(-5 * DEPTH * 256 // d)
    ln_g = rows[:DEPTH]
    final_g = rows[DEPTH]
    pool_w = rows[DEPTH + 1: DEPTH + 1 + n_pool].reshape(DEPTH, 4, POOL_GROUP, POOL_GROUP)
    small = rows[DEPTH + 1 + n_pool: DEPTH + 1 + n_pool + n_small].reshape(-1, 256)[: 5 * DEPTH]
    pool_scale, conv_b, conv_ln_g, conv_ln_b = (small[DEPTH * k: DEPTH * (k + 1)] for k in range(4))
    sinks = small[4 * DEPTH: 5 * DEPTH, :N_Q_HEADS]
    scalar = rows[DEPTH + 1 + n_pool + n_small, 0]
    return ln_g, pool_w, pool_scale, conv_b, conv_ln_g, conv_ln_b, sinks, final_g, scalar


def _block_diag(pool_w):
    out = jnp.zeros((POOL_WIDTH, POOL_WIDTH), pool_w.dtype)
    for gi in range(4):
        out = out.at[POOL_GROUP * gi: POOL_GROUP * (gi + 1), POOL_GROUP * gi: POOL_GROUP * (gi + 1)].set(pool_w[gi])
    return out


def _diag_blocks(mat):
    return jnp.stack([mat[POOL_GROUP * gi: POOL_GROUP * (gi + 1), POOL_GROUP * gi: POOL_GROUP * (gi + 1)]
                      for gi in range(4)], axis=0)


def kernel(x, ln_g, w_in, pool_w, pool_scale, conv_dw, conv_b, conv_ln_g, conv_ln_b, conv_pw, attn_sinks, w_out, final_g, loss_target, m_ln_g, m_w_in, m_pool_w, m_pool_scale, m_conv_dw, m_conv_b, m_conv_ln_g, m_conv_ln_b, m_conv_pw, m_attn_sinks, m_w_out, m_final_g, v_ln_g, v_w_in, v_pool_w, v_pool_scale, v_conv_dw, v_conv_b, v_conv_ln_g, v_conv_ln_b, v_conv_pw, v_attn_sinks, v_w_out, v_final_g):
    x0 = x[0]
    d = x0.shape[1]
    row = lambda a: a.reshape(1, -1)
    slabs = lambda a: a.reshape(N_DEV, a.shape[0] // N_DEV, d)
    c_shard = CONV_WIDTH // N_DEV

    w_in_rows = [jnp.swapaxes(w_in[l], 0, 1).astype(MXU_DTYPE) for l in range(DEPTH)]
    w_out_rows = [w_out[l].astype(MXU_DTYPE) for l in range(DEPTH)]
    per_word = 4 // jnp.dtype(MXU_DTYPE).itemsize
    dw_t = jnp.stack([_dw_rows(conv_dw[l]) for l in range(DEPTH)], axis=0)
    dw_bits = (lax.bitcast_convert_type(dw_t, MXU_DTYPE) if per_word > 1 else dw_t).reshape(-1, d)
    n_pw = DEPTH * c_shard * CONV_WIDTH // d
    conv_rows = _pad_rows(jnp.concatenate([conv_pw.reshape(-1, d).astype(MXU_DTYPE), dw_bits], axis=0), 16)
    w_in_t, w_out_f = [None] * DEPTH, [None] * DEPTH
    w_in_t[0] = _all_gather(w_in_rows[0], "w_in_all_gather")
    wp_bd = [_block_diag(pool_w[l]).astype(MXU_DTYPE) for l in range(DEPTH)]

    xs, projs, cvs, ys = [x0], [], [], []
    q4 = D_IN // N_DEV // 4
    w_in_1 = [w_in_rows[1][q4 * k: q4 * (k + 1)] for k in range(4)]
    for l in range(DEPTH):
        proj, got = _in_proj(xs[l], row(ln_g[l]), w_in_t[l],
                             carried=[w_out_rows[0], conv_rows, w_in_1[0]] if l == 0 else [])
        if l == 0:
            w_in_1_got = [got[2]]
            w_out_f[0] = got[0].reshape(D_MIX, d)
            pw_all = got[1][:, :n_pw].reshape(N_DEV, DEPTH, c_shard, CONV_WIDTH)
            pw_f = [pw_all[:, k].reshape(CONV_WIDTH, CONV_WIDTH) for k in range(DEPTH)]
            bits = got[1][:, n_pw:n_pw + dw_bits.shape[0]].reshape(
                (N_DEV, DEPTH, c_shard, CONV_TAPS_PAD) + (per_word,) * (per_word > 1))
            dw_all = lax.bitcast_convert_type(bits, F32) if per_word > 1 else bits
            dw_f = [jnp.swapaxes(dw_all[:, k].reshape(CONV_WIDTH, CONV_TAPS_PAD), 0, 1) for k in range(DEPTH)]
        (y_pc, cv), got = _poolconv_fwd(proj, wp_bd[l], row(pool_scale[l]), dw_f[l], row(conv_b[l]),
                                        row(conv_ln_g[l]), row(conv_ln_b[l]), pw_f[l],
                                        carried=[w_in_1[1], w_in_1[2]] if l == 0 else [])
        if l == 0:
            w_in_1_got += list(got)
        y_at, got = _attn_fwd(proj, attn_sinks[l], carried=[w_out_rows[1], w_in_1[3]] if l == 0 else [])
        if l == 0:
            w_out_f[1] = got[0].reshape(D_MIX, d)
            w_in_t[1] = jnp.concatenate(w_in_1_got + [got[1]], axis=1).reshape(D_IN, d)
        if l < DEPTH - 1:
            xs.append(_out_proj(xs[l], y_pc, y_at, w_out_f[l]))
        projs.append(proj)
        cvs.append(cv)
        ys.append((y_pc, y_at))

    l = DEPTH - 1
    sq, g_final, dx = _out_proj_loss(xs[l], ys[l][0], ys[l][1], w_out_f[l], loss_target[0], row(final_g))

    l = 1
    (dy, g_wout1), _ = _out_bwd(dx, ys[l][0], ys[l][1], w_out_f[l])
    (da, g_wp1, g_pw1, g_dw1, g_vec1), _ = _poolconv_bwd(
        projs[l], cvs[l], dy, wp_bd[l], row(pool_scale[l]), dw_f[l], row(conv_ln_g[l]), row(conv_ln_b[l]), pw_f[l])
    (dattn, gs1), _ = _attn_bwd(projs[l], dy, attn_sinks[l])
    (dx, g_ln1, g_win_t1), _ = _in_bwd(da, dattn, xs[l], dx, row(ln_g[l]), w_in_t[l])
    l = 0
    (dy, g_wout0), (r_wout1, r_small1) = _out_bwd(dx, ys[l][0], ys[l][1], w_out_f[l],
                                                  carried=[slabs(g_wout1), _small_slabs(g_pw1, g_dw1, d)])
    (da, g_wp0, g_pw0, g_dw0, g_vec0), (r_win1,) = _poolconv_bwd(
        projs[l], cvs[l], dy, wp_bd[l], row(pool_scale[l]), dw_f[l], row(conv_ln_g[l]), row(conv_ln_b[l]), pw_f[l],
        carried=[slabs(g_win_t1)])
    (dattn, gs0), (r_wout0, r_small0) = _attn_bwd(projs[l], dy, attn_sinks[l],
                                                  carried=[slabs(g_wout0), _small_slabs(g_pw0, g_dw0, d)])
    (dx, g_ln0, g_win_t0), _ = _in_bwd(da, dattn, xs[l], dx, row(ln_g[l]), w_in_t[l])
    grad_x = dx[None]

    gv = jnp.stack([g_vec0, g_vec1], axis=0)
    rep_part = _pack_replicated(
        jnp.concatenate([g_ln0, g_ln1], axis=0), jnp.stack([_diag_blocks(g_wp0), _diag_blocks(g_wp1)], axis=0),
        gv[:, 0], gv[:, 1], gv[:, 2], gv[:, 3], jnp.stack([gs0[:, 0], gs1[:, 0]], axis=0), g_final, sq[0, 0], d)
    r_win0, r_rep = _final_exchange([slabs(g_win_t0), rep_part])

    t = lambda a: jnp.swapaxes(a, 1, 2)
    win = [t(o) for o in _adamw_layers([r_win0, r_win1], t(w_in), t(m_w_in), t(v_w_in), "adamw_w_in")]
    wout = _adamw_layers([r_wout0, r_wout1], w_out, m_w_out, v_w_out, "adamw_w_out")
    small = [[_unpack_small(o, d) for o in _adamw(
        r, _pack_small(conv_pw[l], conv_dw[l], d), _pack_small(m_conv_pw[l], m_conv_dw[l], d),
        _pack_small(v_conv_pw[l], v_conv_dw[l], d), f"adamw_conv_{l}")] for l, r in enumerate([r_small0, r_small1])]
    zero = jnp.zeros((), F32)
    pack_r = lambda *a: _pack_replicated(*a, zero, d)
    rep = [_unpack_replicated(o, d) for o in _adamw(
        r_rep, pack_r(ln_g, pool_w, pool_scale, conv_b, conv_ln_g, conv_ln_b, attn_sinks, final_g),
        pack_r(m_ln_g, m_pool_w, m_pool_scale, m_conv_b, m_conv_ln_g, m_conv_ln_b, m_attn_sinks, m_final_g),
        pack_r(v_ln_g, v_pool_w, v_pool_scale, v_conv_b, v_conv_ln_g, v_conv_ln_b, v_attn_sinks, v_final_g),
        "adamw_replicated")]
    loss = 0.5 / d * rep[0][-1]

    outs = []
    for k in range(4):
        r_ln, r_pool, r_scale, r_cb, r_lng, r_lnb, r_sinks, r_final, _ = rep[k]
        s_pw = jnp.stack([small[l][k][0] for l in range(DEPTH)], axis=0)
        s_dw = jnp.stack([small[l][k][1] for l in range(DEPTH)], axis=0)
        outs += [r_ln, win[k], r_pool, r_scale, s_dw, r_cb, r_lng, r_lnb, s_pw, r_sinks, wout[k], r_final]
    return (loss, grad_x, *outs)
```

```python
import jax
import jax.numpy as jnp
from jax import lax
from jax.experimental import pallas as pl
from jax.experimental.pallas import tpu as pltpu

F32 = jnp.float32
MXU_DTYPE = jnp.bfloat16
EXCHANGE_DTYPE = jnp.bfloat16

N_DEV = 8
DEPTH = 2
POOL_WIDTH = 256
POOL_GROUP = 64
CONV_WIDTH = 256
CONV_KERNEL = 31
CONV_TAPS_PAD = 32
HEAD_DIM = 64
N_KV_HEADS = 2
Q_PER_KV = 4
N_Q_HEADS = 8
ATTN_WIDTH = 512
BLOCK = 128
D_MIX = 1024
D_IN = 2560
HALF_IN = 1280
EPS = 1e-6
SCALE = HEAD_DIM ** -0.5
NEG = -1e30

ADAM_LR = 0.001
ADAM_B1 = 0.9
ADAM_B2 = 0.999
ADAM_EPS = 1e-08
ADAM_WD = 0.01
ADAM_STEP = 10

HALO = 32
ROW_TILE = 512
IN_BWD_TILE = 512
VMEM_LIMIT = 56 * 1024 * 1024

_NN = (((1,), (0,)), ((), ()))
_NT = (((1,), (1,)), ((), ()))
_TN = (((0,), (0,)), ((), ()))
_ANY = pl.BlockSpec(memory_space=pl.ANY)


def _mm(a, b, dims=_NN):
    return lax.dot_general(a.astype(MXU_DTYPE), b.astype(MXU_DTYPE), dims, preferred_element_type=F32)


def _sig(x):
    return 1.0 / (1.0 + jnp.exp(-x))


def _dsilu(z, s):
    return s * (1.0 + z * (1.0 - s))


def _params(n_grid):
    return pltpu.CompilerParams(dimension_semantics=("arbitrary",) * n_grid, vmem_limit_bytes=VMEM_LIMIT)


def _row_tile(rows, cap):
    t = min(rows, cap)
    while rows % t or t % 8:
        t -= 8
    return t


def _full(shape):
    return pl.BlockSpec(shape, lambda i: (0,) * len(shape))


def _mesh_pos():
    return lax.axis_index("x"), lax.axis_index("y"), lax.axis_index("c")


class _Exchange:
    def __init__(self, sources):
        self.sources = list(sources)
        self.n = len(self.sources)
        self.gather = [s.ndim == 2 for s in self.sources]

    def out_shapes(self):
        return [jax.ShapeDtypeStruct((N_DEV,) + s.shape[-2:], s.dtype) for s in self.sources]

    def scratch(self):
        if not self.n:
            return []
        return [pltpu.SemaphoreType.DMA((7 * self.n,)), pltpu.SemaphoreType.DMA((7 * self.n,)),
                pltpu.SemaphoreType.DMA((self.n,))]

    def copies(self, src_refs, dst_refs, sems):
        send_sems, recv_sems, local_sems = sems
        x, y, c = _mesh_pos()
        me = 4 * x + 2 * y + c
        out = []
        for a, (src, dst) in enumerate(zip(src_refs, dst_refs)):
            out.append(pltpu.make_async_copy(src if self.gather[a] else src.at[me], dst.at[me], local_sems.at[a]))
            for k in range(1, N_DEV):
                tx, ty, tc = x ^ ((k >> 2) & 1), y ^ ((k >> 1) & 1), c ^ (k & 1)
                out.append(pltpu.make_async_remote_copy(
                    src_ref=src if self.gather[a] else src.at[4 * tx + 2 * ty + tc], dst_ref=dst.at[me],
                    send_sem=send_sems.at[7 * a + k - 1], recv_sem=recv_sems.at[7 * a + k - 1],
                    device_id=(tx, ty, tc), device_id_type=pl.DeviceIdType.MESH))
        return out

    def run(self, refs, first, last):
        if not self.n:
            return
        src_refs, dst_refs, sems = refs

        @pl.when(first)
        def _():
            for cp in self.copies(src_refs, dst_refs, sems):
                cp.start()

        @pl.when(last)
        def _():
            for cp in self.copies(src_refs, dst_refs, sems):
                cp.wait()


def _split_refs(refs, n_in, n_out, exch):
    ins = refs[:n_in]
    srcs = refs[n_in:n_in + exch.n]
    outs = refs[n_in + exch.n:n_in + exch.n + n_out]
    dsts = refs[n_in + exch.n + n_out:n_in + 2 * exch.n + n_out]
    rest = refs[n_in + 2 * exch.n + n_out:]
    sems = rest[len(rest) - 3:] if exch.n else ()
    scratch = rest[:len(rest) - 3] if exch.n else rest
    return ins, outs, scratch, (srcs, dsts, sems)


def _final_exchange(sources):
    exch = _Exchange(sources)

    def body(*refs):
        _, _, _, xrefs = _split_refs(refs, 0, 0, exch)
        for cp in exch.copies(*xrefs):
            cp.start()
        for cp in exch.copies(*xrefs):
            cp.wait()

    return pl.pallas_call(
        body, name="final_exchange", out_shape=exch.out_shapes(),
        in_specs=[_ANY] * exch.n, out_specs=[_ANY] * exch.n, scratch_shapes=exch.scratch(),
    )(*exch.sources)


def _all_gather(shard, name):
    m_per, n = shard.shape

    def body(x_ref, out_ref, send_sems, recv_sems, local_sem):
        x, y, c = _mesh_pos()
        me, sibling = (x, y, c), (x, y, 1 - c)
        chips = [(1 - x, y), (x, 1 - y), (1 - x, 1 - y)]

        def rows(px, py, pc):
            return out_ref.at[pl.ds((4 * px + 2 * py + pc) * m_per, m_per), :]

        def copy(k, block, to, src=None):
            return pltpu.make_async_remote_copy(
                src_ref=rows(*block) if src is None else src, dst_ref=rows(*block),
                send_sem=send_sems.at[k], recv_sem=recv_sems.at[k],
                device_id=to, device_id_type=pl.DeviceIdType.MESH)

        mine = pltpu.make_async_copy(x_ref, rows(*me), local_sem)
        mine.start()
        first = [copy(0, me, sibling, src=x_ref)]
        first += [copy(1 + j, me, (*chip, c), src=x_ref) for j, chip in enumerate(chips)]
        for cp in first:
            cp.start()
        passed = [copy(4 + j, (*chip, c), sibling) for j, chip in enumerate(chips)]
        for j, chip in enumerate(chips):
            copy(1 + j, (*chip, c), me).wait_recv()
            passed[j].start()
        copy(0, sibling, me).wait_recv()
        for j, chip in enumerate(chips):
            copy(4 + j, (*chip, 1 - c), me).wait_recv()
        for cp in first + passed:
            cp.wait_send()
        mine.wait()

    return pl.pallas_call(
        body, name=name,
        out_shape=jax.ShapeDtypeStruct((N_DEV * m_per, n), shard.dtype),
        in_specs=[pl.BlockSpec(memory_space=pltpu.VMEM)],
        out_specs=pl.BlockSpec(memory_space=pltpu.VMEM),
        scratch_shapes=[pltpu.SemaphoreType.DMA((7,)), pltpu.SemaphoreType.DMA((7,)), pltpu.SemaphoreType.DMA],
        compiler_params=pltpu.CompilerParams(vmem_limit_bytes=VMEM_LIMIT),
    )(shard)


def _by_group(lane, v2, v4, v8, v16):
    return jnp.where(lane < 64, v2, jnp.where(lane < 128, v4, jnp.where(lane < 192, v8, v16)))


def _pool_count(t0, n):
    lane = lax.broadcasted_iota(jnp.int32, (1, POOL_WIDTH), 1)
    t = (t0 + lax.broadcasted_iota(jnp.int32, (n, 1), 0)).astype(F32)
    wnd = _by_group(lane, 2.0, 4.0, 8.0, 16.0)
    return jnp.minimum(t + 1.0, wnd)


def _pool_diff(u_ext, t0, ts):
    lane = lax.broadcasted_iota(jnp.int32, (1, POOL_WIDTH), 1)
    s2 = u_ext + pltpu.roll(u_ext, 1, 0)
    s4 = s2 + pltpu.roll(s2, 2, 0)
    s8 = s4 + pltpu.roll(s4, 4, 0)
    s16 = s8 + pltpu.roll(s8, 8, 0)
    pooled = _by_group(lane, s2, s4, s8, s16)[HALO:]
    return pooled / _pool_count(t0, ts) - u_ext[HALO:]


def _pool_diff_bwd(w, ts):
    n = w.shape[0]
    lane = lax.broadcasted_iota(jnp.int32, (1, POOL_WIDTH), 1)
    f2 = w + pltpu.roll(w, n - 1, 0)
    f4 = f2 + pltpu.roll(f2, n - 2, 0)
    f8 = f4 + pltpu.roll(f4, n - 4, 0)
    f16 = f8 + pltpu.roll(f8, n - 8, 0)
    return _by_group(lane, f2, f4, f8, f16)[:ts]


CONV_CHUNK = 64


def _conv_taps():
    return [(8 * m + r, r, m) for r in range(8) for m in range(4) if 8 * m + r < CONV_KERNEL]


def _store_shifted(dst_ref, x, up):
    n = x.shape[0]
    for r in range(8):
        dst_ref[r] = x if r == 0 else pltpu.roll(x, n - r if up else r, 0)


def _anticausal_conv(src_ref, dw_ref, out_ref, n_out):
    def chunk(c, carry):
        t0 = pl.multiple_of(c * CONV_CHUNK, CONV_CHUNK)
        acc = None
        for d, r, m in _conv_taps():
            term = dw_ref[pl.ds(CONV_KERNEL - 1 - d, 1), :] * src_ref[r, pl.ds(t0 + 8 * m, CONV_CHUNK), :]
            acc = term if acc is None else acc + term
        out_ref[pl.ds(t0, CONV_CHUNK), :] = acc
        return carry

    lax.fori_loop(0, n_out // CONV_CHUNK, chunk, 0)


def _depthwise_conv_weight_grad(x_ref, dout_up_ref, acc_ref, n_rows):
    acc_ref[...] = jnp.zeros_like(acc_ref)

    def chunk(c, carry):
        t0 = pl.multiple_of(c * CONV_CHUNK, CONV_CHUNK)
        xv = x_ref[pl.ds(t0, CONV_CHUNK), :]
        for d, r, m in _conv_taps():
            prod = xv * dout_up_ref[r, pl.ds(t0 + 8 * m, CONV_CHUNK), :]
            acc_ref[CONV_KERNEL - 1 - d] += jnp.sum(prod.reshape(CONV_CHUNK // 8, 8, prod.shape[-1]), axis=0)
        return carry

    lax.fori_loop(0, n_rows // CONV_CHUNK, chunk, 0)


def _layer_norm(cv):
    mu = jnp.mean(cv, axis=-1, keepdims=True)
    xc = cv - mu
    var = jnp.mean(xc * xc, axis=-1, keepdims=True)
    rstd = lax.rsqrt(var + EPS)
    return xc * rstd, rstd


PAIR_ROWS = 2 * BLOCK
PAIR_COLS = 4 * BLOCK
ATTN_BLOCKS_PER_STEP = 4


def _attn_blocks_per_step(s_len):
    qb = ATTN_BLOCKS_PER_STEP
    while (s_len // BLOCK) % qb:
        qb //= 2
    return qb


def _pair_rows(v0, v1):
    r = lax.broadcasted_iota(jnp.int32, (PAIR_ROWS, 1), 0)
    return jnp.where(r < BLOCK, v0, v1)


def _fill_attn_bias(bias_ref):
    rows = lax.broadcasted_iota(jnp.int32, (PAIR_ROWS, PAIR_COLS), 0)
    cols = lax.broadcasted_iota(jnp.int32, (PAIR_ROWS, PAIR_COLS), 1)
    key = cols & (2 * BLOCK - 1)
    dist = BLOCK + (rows & (BLOCK - 1)) - key
    in_band = (dist >= 0) & (dist < BLOCK)
    distf = dist.astype(F32)
    second = cols >= 2 * BLOCK
    for kh in range(N_KV_HEADS):
        slope_of = lambda j, a: 2.0 ** -(Q_PER_KV * kh + 2 * j + a + 1)
        slope = jnp.where(rows < BLOCK, jnp.where(second, slope_of(0, 1), slope_of(0, 0)),
                          jnp.where(second, slope_of(1, 1), slope_of(1, 0)))
        bias = -slope * distf
        bias_ref[0, kh] = jnp.where(in_band & (key >= BLOCK), bias, NEG)
        bias_ref[1, kh] = jnp.where(in_band, bias, NEG)


def _pair_block_matrix(x, x_swapped, kh):
    lo = lax.broadcasted_iota(jnp.int32, (1, 2 * HEAD_DIM), 1) < HEAD_DIM
    in_lo, in_hi = (x, x_swapped) if kh == 0 else (x_swapped, x)
    return jnp.concatenate([jnp.where(lo, in_lo, 0.0), jnp.where(lo, 0.0, in_hi)], axis=0).astype(MXU_DTYPE)


def _pair_queries(q, kh):
    return (_pair_stack(q, kh) * SCALE).astype(MXU_DTYPE)


def _pair_stack(a, kh):
    return jnp.concatenate([a[:, 2 * BLOCK * kh: 2 * BLOCK * kh + BLOCK],
                            a[:, 2 * BLOCK * kh + BLOCK: 2 * BLOCK * (kh + 1)]], axis=0)


def _pair_unstack(parts):
    return jnp.concatenate([p[BLOCK * j: BLOCK * (j + 1)] for p in parts for j in range(2)], axis=-1)


def _pair_softmax(s, kh, sinks_ref):
    ps, p_sinks = [], []
    for a in range(2):
        sa = s[:, 2 * BLOCK * a: 2 * BLOCK * (a + 1)]
        sink = _pair_rows(sinks_ref[Q_PER_KV * kh + a], sinks_ref[Q_PER_KV * kh + 2 + a])
        m = jnp.maximum(jnp.max(sa, axis=-1, keepdims=True), sink)
        e = jnp.exp(sa - m)
        es = jnp.exp(sink - m)
        inv = 1.0 / (jnp.sum(e, axis=-1, keepdims=True) + es)
        ps.append(e * inv)
        p_sinks.append(es * inv)
    return jnp.concatenate(ps, axis=-1), p_sinks


def _fold_pair_halves(t):
    lo = lax.broadcasted_iota(jnp.int32, (1, 2 * HEAD_DIM), 1) < HEAD_DIM
    u = jnp.where(lo, t[:2 * BLOCK], t[2 * BLOCK:])
    return u + pltpu.roll(u, HEAD_DIM, 1)


def _in_proj(x, g, w_t, carried=()):
    s_len, d = x.shape
    ts = _row_tile(s_len, ROW_TILE)
    nt = s_len // ts
    exch = _Exchange(carried)

    def body(*refs):
        (x_ref, g_ref, w_ref), (o_ref,), _, xrefs = _split_refs(refs, 3, 1, exch)
        i = pl.program_id(0)
        exch.run(xrefs, i == 0, i == nt - 1)
        xv = x_ref[...]
        r = lax.rsqrt(jnp.mean(xv * xv, axis=-1, keepdims=True) + EPS)
        o_ref[...] = _mm(xv * r * g_ref[...], w_ref[...], _NT)

    res = pl.pallas_call(
        body, name="in_proj_carrier" if exch.n else "in_proj", grid=(nt,),
        in_specs=[pl.BlockSpec((ts, d), lambda i: (i, 0)), _full((1, d)), _full((D_IN, d))] + [_ANY] * exch.n,
        out_specs=[pl.BlockSpec((ts, D_IN), lambda i: (i, 0))] + [_ANY] * exch.n,
        out_shape=[jax.ShapeDtypeStruct((s_len, D_IN), F32)] + exch.out_shapes(),
        scratch_shapes=exch.scratch(),
        compiler_params=_params(1),
    )(x, g, w_t, *exch.sources)
    return res[0], res[1:]


def _poolconv_fwd(proj, wp, scale, dw, cb, lng, lnb, pw, carried=()):
    s_len = proj.shape[0]
    ts = _row_tile(s_len, ROW_TILE)
    hb = ts // HALO
    nt = s_len // ts
    exch = _Exchange(carried)

    def body(*refs):
        ins, (y_ref, cv_ref, diff_ref), _, xrefs = _split_refs(refs, 9, 3, exch)
        p_ref, ph_ref, wp_ref, sc_ref, dw_ref, cb_ref, lng_ref, lnb_ref, pw_ref = ins
        i = pl.program_id(0)
        exch.run(xrefs, i == 0, i == nt - 1)
        cur = p_ref[...]
        halo = jnp.where(i > 0, ph_ref[...], 0.0)
        ext = jnp.concatenate([halo, cur], axis=0)
        diff = _pool_diff(ext[:, 0:256], i * ts, ts).astype(MXU_DTYPE)
        diff_ref[...] = diff
        gp = cur[:, 256:512]
        y_pool = _mm(diff, wp_ref[...]) * sc_ref[...] * (gp * _sig(gp))
        hh = ext[:, 512:768] * _sig(ext[:, 768:1024])
        shifted = [hh if r == 0 else pltpu.roll(hh, r, 0) for r in range(8)]
        cv = cb_ref[...]
        for dist, r, m in _conv_taps():
            cv = cv + dw_ref[pl.ds(CONV_KERNEL - 1 - dist, 1), :] * shifted[r][HALO - 8 * m: HALO - 8 * m + ts]
        cv_ref[...] = cv
        n, _ = _layer_norm(cv)
        z = n * lng_ref[...] + lnb_ref[...]
        gc = cur[:, 1024:1280]
        y_conv = _mm(z * _sig(z), pw_ref[...]) * (gc * _sig(gc))
        y_ref[...] = jnp.concatenate([y_pool, y_conv], axis=-1).astype(y_ref.dtype)

    vec = _full((1, 256))
    res = pl.pallas_call(
        body, name="poolconv_fwd_carrier" if exch.n else "poolconv_fwd", grid=(nt,),
        in_specs=[pl.BlockSpec((ts, HALF_IN), lambda i: (i, 0)),
                  pl.BlockSpec((HALO, HALF_IN), lambda i: (jnp.maximum(i * hb - 1, 0), 0)),
                  _full((256, 256)), vec, _full((CONV_TAPS_PAD, 256)), vec, vec, vec, _full((256, 256))]
        + [_ANY] * exch.n,
        out_specs=[pl.BlockSpec((ts, 512), lambda i: (i, 0)), pl.BlockSpec((ts, 256), lambda i: (i, 0)),
                   pl.BlockSpec((ts, 256), lambda i: (i, 0))] + [_ANY] * exch.n,
        out_shape=[jax.ShapeDtypeStruct((s_len, 512), MXU_DTYPE), jax.ShapeDtypeStruct((s_len, 256), F32),
                   jax.ShapeDtypeStruct((s_len, 256), MXU_DTYPE)] + exch.out_shapes(),
        scratch_shapes=exch.scratch(),
        compiler_params=_params(1),
    )(proj, proj, wp, scale, dw, cb, lng, lnb, pw, *exch.sources)
    return res[:3], res[3:]


def _attn_fwd(proj, sinks, carried=()):
    s_len = proj.shape[0]
    qb = _attn_blocks_per_step(s_len)
    ts = qb * BLOCK
    nt = s_len // ts
    exch = _Exchange(carried)

    def body(*refs):
        (p_ref, kvp_ref, sinks_ref), (y_ref,), (bias_ref,), xrefs = _split_refs(refs, 3, 1, exch)
        i = pl.program_id(0)
        exch.run(xrefs, i == 0, i == nt - 1)

        @pl.when(i == 0)
        def _():
            _fill_attn_bias(bias_ref)

        probs = [(b, kh) for b in range(qb) for kh in range(N_KV_HEADS)]
        scores, v_bds = {}, {}
        for b in range(qb):
            r0 = BLOCK * b
            kv_prev = kvp_ref[...] if b == 0 else p_ref[r0 - BLOCK:r0, 512:768]
            kv2 = jnp.concatenate([kv_prev, p_ref[r0:r0 + BLOCK, 512:768]], axis=0)
            k2, v2 = kv2[:, :BLOCK], kv2[:, BLOCK:]
            k2_swapped, v2_swapped = pltpu.roll(k2, HEAD_DIM, 1), pltpu.roll(v2, HEAD_DIM, 1)
            variant = jnp.where(i == 0, 0, 1) if b == 0 else 1
            q = p_ref[r0:r0 + BLOCK, 0:512]
            for kh in range(N_KV_HEADS):
                k_bd = _pair_block_matrix(k2, k2_swapped, kh)
                v_bds[b, kh] = _pair_block_matrix(v2, v2_swapped, kh)
                scores[b, kh] = _mm(_pair_queries(q, kh), k_bd, _NT) + bias_ref[variant, kh]
        ps = {pr: _pair_softmax(scores[pr], pr[1], sinks_ref)[0] for pr in probs}
        outs = {pr: _mm(ps[pr], v_bds[pr]) for pr in probs}
        for b in range(qb):
            r0 = BLOCK * b
            ga = p_ref[r0:r0 + BLOCK, 768:1280]
            ya = _pair_unstack([outs[b, kh] for kh in range(N_KV_HEADS)])
            y_ref[r0:r0 + BLOCK, :] = (ya * (ga * _sig(ga))).astype(y_ref.dtype)

    res = pl.pallas_call(
        body, name="attn_fwd_carrier" if exch.n else "attn_fwd", grid=(nt,),
        in_specs=[pl.BlockSpec((ts, HALF_IN), lambda i: (i, 1)),
                  pl.BlockSpec((BLOCK, 256), lambda i: (jnp.maximum(i * qb - 1, 0), 7)),
                  pl.BlockSpec(memory_space=pltpu.SMEM)] + [_ANY] * exch.n,
        out_specs=[pl.BlockSpec((ts, 512), lambda i: (i, 0))] + [_ANY] * exch.n,
        out_shape=[jax.ShapeDtypeStruct((s_len, 512), MXU_DTYPE)] + exch.out_shapes(),
        scratch_shapes=[pltpu.VMEM((2, N_KV_HEADS, PAIR_ROWS, PAIR_COLS), F32)] + exch.scratch(),
        compiler_params=_params(1),
    )(proj, proj, sinks, *exch.sources)
    return res[0], res[1:]


def _out_proj(x, y_pc, y_at, w_out):
    s_len, d = x.shape
    ts = _row_tile(s_len, ROW_TILE)

    def body(x_ref, a_ref, b_ref, w_ref, o_ref):
        y = jnp.concatenate([a_ref[...], b_ref[...]], axis=-1)
        o_ref[...] = x_ref[...] + _mm(y, w_ref[...])

    return pl.pallas_call(
        body, name="out_proj", grid=(s_len // ts,),
        in_specs=[pl.BlockSpec((ts, d), lambda i: (i, 0)), pl.BlockSpec((ts, 512), lambda i: (i, 0)),
                  pl.BlockSpec((ts, 512), lambda i: (i, 0)), _full((D_MIX, d))],
        out_specs=pl.BlockSpec((ts, d), lambda i: (i, 0)),
        out_shape=jax.ShapeDtypeStruct((s_len, d), F32),
        compiler_params=_params(1),
    )(x, y_pc, y_at, w_out)


def _out_proj_loss(x, y_pc, y_at, w_out, target, g):
    s_len, d = x.shape
    ts = _row_tile(s_len, ROW_TILE)

    def body(x_ref, a_ref, b_ref, w_ref, t_ref, g_ref, sq_ref, dg_ref, dx_ref):
        i = pl.program_id(0)
        y = jnp.concatenate([a_ref[...], b_ref[...]], axis=-1)
        xv = x_ref[...] + _mm(y, w_ref[...])
        gv = g_ref[...]
        r = lax.rsqrt(jnp.mean(xv * xv, axis=-1, keepdims=True) + EPS)
        xr = xv * r
        err = xr * gv - t_ref[...]
        dout = err * (1.0 / d)
        w = dout * gv
        dx_ref[...] = r * (w - xr * jnp.mean(w * xr, axis=-1, keepdims=True))

        @pl.when(i == 0)
        def _():
            sq_ref[...] = jnp.zeros_like(sq_ref)
            dg_ref[...] = jnp.zeros_like(dg_ref)

        sq = jnp.sum(jnp.sum(err * err, axis=-1, keepdims=True), axis=0, keepdims=True)
        sq_ref[...] += jnp.broadcast_to(sq, sq_ref.shape)
        dg_ref[...] += jnp.sum(dout * xr, axis=0, keepdims=True)

    tile = pl.BlockSpec((ts, d), lambda i: (i, 0))
    half = pl.BlockSpec((ts, 512), lambda i: (i, 0))
    return pl.pallas_call(
        body, name="out_proj_loss", grid=(s_len // ts,),
        in_specs=[tile, half, half, _full((D_MIX, d)), tile, _full((1, d))],
        out_specs=[_full((1, 128)), _full((1, d)), tile],
        out_shape=[jax.ShapeDtypeStruct((1, 128), F32), jax.ShapeDtypeStruct((1, d), F32),
                   jax.ShapeDtypeStruct((s_len, d), F32)],
        compiler_params=_params(1),
    )(x, y_pc, y_at, w_out, target, g)


def _out_bwd(dxo, y_pc, y_at, w_out, carried=()):
    s_len, d = dxo.shape
    ts = _row_tile(s_len, ROW_TILE)
    nt = s_len // ts
    exch = _Exchange(carried)

    def body(*refs):
        (dx_ref, a_ref, b_ref, w_ref), (dy_ref, gw_ref), (acc_ref,), xrefs = _split_refs(refs, 4, 2, exch)
        i = pl.program_id(0)
        exch.run(xrefs, i == 0, i == nt - 1)
        dxv = dx_ref[...].astype(MXU_DTYPE)
        dy_ref[...] = _mm(dxv, w_ref[...], _NT)

        @pl.when(i == 0)
        def _():
            acc_ref[...] = jnp.zeros_like(acc_ref)

        y = jnp.concatenate([a_ref[...], b_ref[...]], axis=-1)
        acc_ref[...] += _mm(y, dxv, _TN)

        @pl.when(i == nt - 1)
        def _():
            gw_ref[...] = acc_ref[...].astype(gw_ref.dtype)

    res = pl.pallas_call(
        body, name="out_bwd_carrier" if exch.n else "out_bwd", grid=(nt,),
        in_specs=[pl.BlockSpec((ts, d), lambda i: (i, 0)), pl.BlockSpec((ts, 512), lambda i: (i, 0)),
                  pl.BlockSpec((ts, 512), lambda i: (i, 0)), _full((D_MIX, d))] + [_ANY] * exch.n,
        out_specs=[pl.BlockSpec((ts, D_MIX), lambda i: (i, 0)), _full((D_MIX, d))] + [_ANY] * exch.n,
        out_shape=[jax.ShapeDtypeStruct((s_len, D_MIX), F32), jax.ShapeDtypeStruct((D_MIX, d), EXCHANGE_DTYPE)]
        + exch.out_shapes(),
        scratch_shapes=[pltpu.VMEM((D_MIX, d), F32)] + exch.scratch(),
        compiler_params=_params(1),
    )(dxo, y_pc, y_at, w_out, *exch.sources)
    return res[:2], res[2:]


def _poolconv_bwd(proj, cv, diff, dy, wp, scale, dw, lng, lnb, pw, carried=()):
    s_len = proj.shape[0]
    ts = _row_tile(s_len, ROW_TILE)
    hb = ts // HALO
    nt = s_len // ts
    last_halo = s_len // HALO - 1
    n = ts + HALO
    exch = _Exchange(carried)

    def body(*refs):
        ins, outs, (up_ref, hh_ref, dhh_ref, gdw_acc_ref), xrefs = _split_refs(refs, 13, 5, exch)
        (p_ref, pn_ref, cv_ref, cvn_ref, diff_ref, dy_ref, dyn_ref, wp_ref, sc_ref, dw_ref, lng_ref, lnb_ref,
         pw_ref) = ins
        da_ref, gwp_ref, gpw_ref, gdw_ref, gvec_ref = outs
        i = pl.program_id(0)
        exch.run(xrefs, i == 0, i == nt - 1)
        has_next = i < nt - 1
        cur = p_ref[...]
        nxt = jnp.where(has_next, pn_ref[...], 0.0)
        dyx = jnp.concatenate([dy_ref[...], jnp.where(has_next, dyn_ref[...], 0.0)], axis=0)
        row = lax.broadcasted_iota(jnp.int32, (n, 1), 0)
        in_seq = (row < ts) | has_next
        scale_v = sc_ref[...]

        cvx = jnp.concatenate([cv_ref[...], jnp.where(has_next, cvn_ref[...], 0.0)], axis=0)
        nrm, rstd = _layer_norm(cvx)
        z = nrm * lng_ref[...] + lnb_ref[...]
        sz = _sig(z)
        sw = z * sz
        gc = jnp.concatenate([cur[:, 1024:1280], nxt[:, 1024:1280]], axis=0)
        sgc = _sig(gc)
        yc = _mm(sw, pw_ref[...])
        dyc = dyx[:, 256:512]
        d_yc = dyc * (gc * sgc)
        d_gc = (dyc * yc * _dsilu(gc, sgc))[:ts]
        d_z = _mm(d_yc, pw_ref[...], _NT) * _dsilu(z, sz)
        d_n = d_z * lng_ref[...]
        d_cv = rstd * (d_n - jnp.mean(d_n, axis=-1, keepdims=True)
                       - nrm * jnp.mean(d_n * nrm, axis=-1, keepdims=True))
        d_cv = jnp.where(in_seq, d_cv, 0.0)
        _store_shifted(up_ref, d_cv, up=True)
        _anticausal_conv(up_ref, dw_ref, dhh_ref, ts)
        d_hh = dhh_ref[...]
        a_c, sb_c = cur[:, 512:768], _sig(cur[:, 768:1024])
        hh_ref[...] = a_c * sb_c
        d_a = d_hh * sb_c
        d_b = d_hh * a_c * sb_c * (1.0 - sb_c)
        d_cv_t = d_cv[:ts]
        _depthwise_conv_weight_grad(hh_ref, up_ref, gdw_acc_ref, ts)

        diff = diff_ref[...]
        raw = _mm(diff, wp_ref[...])
        gp = jnp.concatenate([cur[:, 256:512], nxt[:, 256:512]], axis=0)
        sgp = _sig(gp)
        dyp = dyx[:, 0:256]
        d_yp = dyp * (gp * sgp)
        d_gp = dyp[:ts] * (raw * scale_v) * _dsilu(gp, sgp)[:ts]
        d_raw = d_yp * scale_v
        d_diff = _mm(d_raw, wp_ref[...], _NT)
        w = jnp.where(in_seq, d_diff / _pool_count(i * ts, n), 0.0)
        d_u = _pool_diff_bwd(w, ts) - d_diff[:ts]

        da_ref[...] = jnp.concatenate([d_u, d_gp, d_a, d_b, d_gc], axis=-1).astype(da_ref.dtype)

        @pl.when(i == 0)
        def _():
            gwp_ref[...] = jnp.zeros_like(gwp_ref)
            gpw_ref[...] = jnp.zeros_like(gpw_ref)
            gdw_ref[...] = jnp.zeros_like(gdw_ref)
            gvec_ref[...] = jnp.zeros_like(gvec_ref)

        gwp_ref[...] += _mm(diff, d_raw[:ts], _TN)
        gpw_ref[...] += _mm(sw[:ts], d_yc[:ts], _TN)
        gdw_ref[...] += jnp.sum(gdw_acc_ref[...], axis=1)
        zero_row = jnp.zeros((1, 256), F32)
        gvec_ref[...] += jnp.concatenate([
            jnp.sum(d_yp[:ts] * raw, axis=0, keepdims=True),
            jnp.sum(d_cv_t, axis=0, keepdims=True),
            jnp.sum((d_z * nrm)[:ts], axis=0, keepdims=True),
            jnp.sum(d_z[:ts], axis=0, keepdims=True),
            zero_row, zero_row, zero_row, zero_row], axis=0)

    vec = _full((1, 256))
    nxt_halo = lambda i: (jnp.minimum((i + 1) * hb, last_halo), 0)
    res = pl.pallas_call(
        body, name="poolconv_bwd_carrier" if exch.n else "poolconv_bwd", grid=(nt,),
        in_specs=[pl.BlockSpec((ts, HALF_IN), lambda i: (i, 0)), pl.BlockSpec((HALO, HALF_IN), nxt_halo),
                  pl.BlockSpec((ts, 256), lambda i: (i, 0)), pl.BlockSpec((HALO, 256), nxt_halo),
                  pl.BlockSpec((ts, 256), lambda i: (i, 0)),
                  pl.BlockSpec((ts, 512), lambda i: (i, 0)), pl.BlockSpec((HALO, 512), nxt_halo),
                  _full((256, 256)), vec, _full((CONV_TAPS_PAD, 256)), vec, vec, _full((256, 256))]
        + [_ANY] * exch.n,
        out_specs=[pl.BlockSpec((ts, HALF_IN), lambda i: (i, 0)), _full((256, 256)), _full((256, 256)),
                   _full((CONV_TAPS_PAD, 256)), _full((8, 256))] + [_ANY] * exch.n,
        out_shape=[jax.ShapeDtypeStruct((s_len, HALF_IN), MXU_DTYPE), jax.ShapeDtypeStruct((256, 256), F32),
                   jax.ShapeDtypeStruct((256, 256), F32), jax.ShapeDtypeStruct((CONV_TAPS_PAD, 256), F32),
                   jax.ShapeDtypeStruct((8, 256), F32)] + exch.out_shapes(),
        scratch_shapes=[pltpu.VMEM((8, n, CONV_WIDTH), F32), pltpu.VMEM((ts, CONV_WIDTH), F32),
                        pltpu.VMEM((ts, CONV_WIDTH), F32), pltpu.VMEM((CONV_TAPS_PAD, 8, CONV_WIDTH), F32)]
        + exch.scratch(),
        compiler_params=_params(1),
    )(proj, proj, cv, cv, diff, dy, dy, wp, scale, dw, lng, lnb, pw, *exch.sources)
    return res[:5], res[5:]


DQ0, DKC0, DVC0, DKP0, DVP0, DGA0, DATTN_W = 0, 512, 640, 768, 896, 1024, 1536


def _attn_bwd(proj, dy, sinks, carried=()):
    s_len = proj.shape[0]
    qb = _attn_blocks_per_step(s_len)
    ts = qb * BLOCK
    nt = s_len // ts
    exch = _Exchange(carried)

    def body(*refs):
        (p_ref, kvp_ref, dy_ref, sinks_ref), (o_ref, gs_ref), (bias_ref,), xrefs = _split_refs(refs, 4, 2, exch)
        i = pl.program_id(0)
        exch.run(xrefs, i == 0, i == nt - 1)

        @pl.when(i == 0)
        def _():
            _fill_attn_bias(bias_ref)
            gs_ref[...] = jnp.zeros_like(gs_ref)

        lo = lax.broadcasted_iota(jnp.int32, (1, 2 * HEAD_DIM), 1) < HEAD_DIM
        probs = [(b, kh) for b in range(qb) for kh in range(N_KV_HEADS)]
        kv_heads = range(N_KV_HEADS)
        scores, k_bds, v_bds, q2s, do2s, dyas, gas, sgas = {}, {}, {}, {}, {}, {}, {}, {}
        for b in range(qb):
            r0 = BLOCK * b
            kv_prev = kvp_ref[...] if b == 0 else p_ref[r0 - BLOCK:r0, 512:768]
            kv2 = jnp.concatenate([kv_prev, p_ref[r0:r0 + BLOCK, 512:768]], axis=0)
            k2, v2 = kv2[:, :BLOCK], kv2[:, BLOCK:]
            k2_swapped, v2_swapped = pltpu.roll(k2, HEAD_DIM, 1), pltpu.roll(v2, HEAD_DIM, 1)
            variant = jnp.where(i == 0, 0, 1) if b == 0 else 1
            q = p_ref[r0:r0 + BLOCK, 0:512]
            gas[b] = p_ref[r0:r0 + BLOCK, 768:1280]
            dyas[b] = dy_ref[r0:r0 + BLOCK, :]
            sgas[b] = _sig(gas[b])
            d_o = dyas[b] * (gas[b] * sgas[b])
            for kh in kv_heads:
                k_bds[b, kh] = _pair_block_matrix(k2, k2_swapped, kh)
                v_bds[b, kh] = _pair_block_matrix(v2, v2_swapped, kh)
                q2s[b, kh] = _pair_queries(q, kh)
                do2s[b, kh] = _pair_stack(d_o, kh)
                scores[b, kh] = _mm(q2s[b, kh], k_bds[b, kh], _NT) + bias_ref[variant, kh]
        softmaxes = {pr: _pair_softmax(scores[pr], pr[1], sinks_ref) for pr in probs}
        ps = {pr: softmaxes[pr][0] for pr in probs}
        outs = {pr: _mm(ps[pr], v_bds[pr]) for pr in probs}
        dps = {pr: _mm(do2s[pr], v_bds[pr], _NT) for pr in probs}
        dss = {}
        d_sinks = [None] * N_Q_HEADS
        for pr in probs:
            p, dp, kh = ps[pr], dps[pr], pr[1]
            ds_halves = []
            for a in range(2):
                cols = slice(2 * BLOCK * a, 2 * BLOCK * (a + 1))
                delta = jnp.sum(p[:, cols] * dp[:, cols], axis=-1, keepdims=True)
                ds_halves.append(p[:, cols] * (dp[:, cols] - delta))
                dsink = -softmaxes[pr][1][a] * delta
                for j in range(2):
                    part = jnp.sum(dsink[BLOCK * j: BLOCK * (j + 1)], axis=0, keepdims=True)
                    h = Q_PER_KV * kh + 2 * j + a
                    d_sinks[h] = part if d_sinks[h] is None else d_sinks[h] + part
            dss[pr] = jnp.concatenate(ds_halves, axis=-1)
        dqs = {pr: _mm(dss[pr], k_bds[pr]) * SCALE for pr in probs}
        dks = {pr: _fold_pair_halves(_mm(dss[pr], q2s[pr], _TN)) for pr in probs}
        dvs = {pr: _fold_pair_halves(_mm(ps[pr], do2s[pr], _TN)) for pr in probs}
        for b in range(qb):
            r0 = BLOCK * b
            dk = jnp.where(lo, dks[b, 0], dks[b, 1])
            dv = jnp.where(lo, dvs[b, 0], dvs[b, 1])
            d_ga = dyas[b] * _pair_unstack([outs[b, kh] for kh in kv_heads]) * _dsilu(gas[b], sgas[b])
            o_ref[r0:r0 + BLOCK, :] = jnp.concatenate(
                [_pair_unstack([dqs[b, kh] for kh in kv_heads]), dk[BLOCK:], dv[BLOCK:], dk[:BLOCK], dv[:BLOCK],
                 d_ga], axis=-1)
        gs_ref[...] += jnp.broadcast_to(jnp.concatenate(d_sinks, axis=0), gs_ref.shape)

    res = pl.pallas_call(
        body, name="attn_bwd_carrier" if exch.n else "attn_bwd", grid=(nt,),
        in_specs=[pl.BlockSpec((ts, HALF_IN), lambda i: (i, 1)),
                  pl.BlockSpec((BLOCK, 256), lambda i: (jnp.maximum(i * qb - 1, 0), 7)),
                  pl.BlockSpec((ts, 512), lambda i: (i, 1)),
                  pl.BlockSpec(memory_space=pltpu.SMEM)] + [_ANY] * exch.n,
        out_specs=[pl.BlockSpec((ts, DATTN_W), lambda i: (i, 0)), _full((N_Q_HEADS, 128))] + [_ANY] * exch.n,
        out_shape=[jax.ShapeDtypeStruct((s_len, DATTN_W), F32), jax.ShapeDtypeStruct((N_Q_HEADS, 128), F32)]
        + exch.out_shapes(),
        scratch_shapes=[pltpu.VMEM((2, N_KV_HEADS, PAIR_ROWS, PAIR_COLS), F32)] + exch.scratch(),
        compiler_params=_params(1),
    )(proj, proj, dy, sinks, *exch.sources)
    return res[:2], res[2:]


def _in_bwd(da, dattn, x, dxo, g, w_t, carried=()):
    s_len, d = x.shape
    ts = _row_tile(s_len, IN_BWD_TILE)
    bpt = ts // BLOCK
    nt = s_len // ts
    last_block = s_len // BLOCK - 1
    exch = _Exchange(carried)

    def body(*refs):
        ins, (dx_ref, dg_ref, gw_ref), (acc_ref, stage_ref, stage_sem), xrefs = _split_refs(refs, 7, 3, exch)
        da_ref, dat_ref, nxt_ref, x_ref, dxo_ref, g_ref, w_ref = ins
        i = pl.program_id(0)
        exch.run(xrefs, i == 0, i == nt - 1)
        dat = dat_ref[...]
        nxt = jnp.where(i < nt - 1, nxt_ref[...], 0.0)
        shifted = jnp.concatenate([dat[BLOCK:, DKP0:DGA0], nxt], axis=0) if bpt > 1 else nxt
        dkv = dat[:, DKC0:DKP0] + shifted
        dproj = jnp.concatenate([da_ref[...], dat[:, DQ0:DKC0].astype(MXU_DTYPE), dkv.astype(MXU_DTYPE),
                                 dat[:, DGA0:DATTN_W].astype(MXU_DTYPE)], axis=-1)
        d_h = _mm(dproj, w_ref[...])
        xv = x_ref[...]
        gv = g_ref[...]
        r = lax.rsqrt(jnp.mean(xv * xv, axis=-1, keepdims=True) + EPS)
        xr = xv * r
        w = d_h * gv
        dx_ref[...] = dxo_ref[...] + r * (w - xr * jnp.mean(w * xr, axis=-1, keepdims=True))

        @pl.when(i == 0)
        def _():
            dg_ref[...] = jnp.zeros_like(dg_ref)
            acc_ref[...] = jnp.zeros_like(acc_ref)

        dg_ref[...] += jnp.sum(d_h * xr, axis=0, keepdims=True)
        acc_ref[...] += _mm(dproj, xr * gv, _TN)

        @pl.when(i == nt - 1)
        def _():
            stage_ref[...] = acc_ref[...].astype(stage_ref.dtype)
            out = pltpu.make_async_copy(stage_ref, gw_ref, stage_sem)
            out.start()
            out.wait()

    res = pl.pallas_call(
        body, name="in_bwd_carrier" if exch.n else "in_bwd", grid=(nt,),
        in_specs=[pl.BlockSpec((ts, HALF_IN), lambda i: (i, 0)),
                  pl.BlockSpec((ts, DATTN_W), lambda i: (i, 0)),
                  pl.BlockSpec((BLOCK, 256), lambda i: (jnp.minimum((i + 1) * bpt, last_block), 3)),
                  pl.BlockSpec((ts, d), lambda i: (i, 0)), pl.BlockSpec((ts, d), lambda i: (i, 0)),
                  _full((1, d)), pl.BlockSpec((D_IN, d), lambda i: (0, 0), pipeline_mode=pl.Buffered(1))]
        + [_ANY] * exch.n,
        out_specs=[pl.BlockSpec((ts, d), lambda i: (i, 0)), _full((1, d)), _ANY] + [_ANY] * exch.n,
        out_shape=[jax.ShapeDtypeStruct((s_len, d), F32), jax.ShapeDtypeStruct((1, d), F32),
                   jax.ShapeDtypeStruct((D_IN, d), EXCHANGE_DTYPE)] + exch.out_shapes(),
        scratch_shapes=[pltpu.VMEM((D_IN, d), F32), pltpu.VMEM((D_IN, d), EXCHANGE_DTYPE), pltpu.SemaphoreType.DMA]
        + exch.scratch(),
        compiler_params=_params(1),
    )(da, dattn, dattn, x, dxo, g, w_t, *exch.sources)
    return res[:3], res[3:]


def _sum_partials(p_ref):
    g = p_ref[0].astype(F32)
    for k in range(1, N_DEV):
        g = g + p_ref[k].astype(F32)
    return g


def _adamw_step(g, w, m, v):
    nm = ADAM_B1 * m + (1.0 - ADAM_B1) * g
    nv = ADAM_B2 * v + (1.0 - ADAM_B2) * (g * g)
    m_hat = nm / (1.0 - ADAM_B1 ** ADAM_STEP)
    v_hat = nv / (1.0 - ADAM_B2 ** ADAM_STEP)
    return -ADAM_LR * (m_hat / (jnp.sqrt(v_hat) + ADAM_EPS) + ADAM_WD * w), nm, nv


def _adamw_layers(parts, w, m, v, name):
    def body(*refs):
        p_refs = refs[:DEPTH]
        w_ref, m_ref, v_ref, g_ref, d_ref, nm_ref, nv_ref = refs[DEPTH:]
        for l in range(DEPTH):
            g = _sum_partials(p_refs[l])
            g_ref[l] = g
            d_ref[l], nm_ref[l], nv_ref[l] = _adamw_step(g, w_ref[l], m_ref[l], v_ref[l])

    vmem = pl.BlockSpec(memory_space=pltpu.VMEM)
    shape = jax.ShapeDtypeStruct(w.shape, F32)
    return pl.pallas_call(
        body, name=name, in_specs=[vmem] * (DEPTH + 3), out_specs=[vmem] * 4, out_shape=[shape] * 4,
        compiler_params=pltpu.CompilerParams(vmem_limit_bytes=VMEM_LIMIT),
    )(*parts, w, m, v)


def _adamw(parts, w, m, v, name):
    rows, n = w.shape
    tr = _row_tile(rows, 256)

    def body(p_ref, w_ref, m_ref, v_ref, g_ref, d_ref, nm_ref, nv_ref):
        g = _sum_partials(p_ref)
        g_ref[...] = g
        d_ref[...], nm_ref[...], nv_ref[...] = _adamw_step(g, w_ref[...], m_ref[...], v_ref[...])

    tile = pl.BlockSpec((tr, n), lambda i: (i, 0))
    shape = jax.ShapeDtypeStruct((rows, n), F32)
    return pl.pallas_call(
        body, name=name, grid=(rows // tr,),
        in_specs=[pl.BlockSpec((N_DEV, tr, n), lambda i: (0, i, 0)), tile, tile, tile],
        out_specs=[tile, tile, tile, tile],
        out_shape=[shape, shape, shape, shape],
        compiler_params=_params(1),
    )(parts, w, m, v)


def _pad_rows(a, mult):
    pad = (-a.shape[0]) % mult
    return a if pad == 0 else jnp.concatenate([a, jnp.zeros((pad, a.shape[1]), a.dtype)], axis=0)


def _dw_rows(conv_dw_l):
    return jnp.pad(jnp.swapaxes(conv_dw_l, 0, 1), ((0, 0), (0, CONV_TAPS_PAD - CONV_KERNEL)))


def _pack_small(pw_l, dw_l, d):
    rows = jnp.concatenate([pw_l.reshape(-1, d), _dw_rows(dw_l).reshape(-1, d)], axis=0)
    return _pad_rows(rows, 8)


def _small_slabs(g_pw, g_dw, d):
    a = g_pw.reshape(N_DEV, -1, d)
    b = jnp.swapaxes(g_dw, 0, 1).reshape(N_DEV, -1, d)
    used = a.shape[1] + b.shape[1]
    return jnp.concatenate([a, b, jnp.zeros((N_DEV, (-used) % 8, d), g_pw.dtype)], axis=1)


def _unpack_small(rows, d):
    c = CONV_WIDTH // N_DEV
    n_pw = c * CONV_WIDTH // d
    n_dw = c * CONV_TAPS_PAD // d
    pw = rows[:n_pw].reshape(c, CONV_WIDTH)
    dw = jnp.swapaxes(rows[n_pw:n_pw + n_dw].reshape(c, CONV_TAPS_PAD), 0, 1)[:CONV_KERNEL]
    return pw, dw


def _pack_replicated(ln_g, pool_w, pool_scale, conv_b, conv_ln_g, conv_ln_b, attn_sinks, final_g, scalar, d):
    sinks = jnp.pad(attn_sinks, ((0, 0), (0, 256 - N_Q_HEADS)))
    small = jnp.concatenate([pool_scale, conv_b, conv_ln_g, conv_ln_b, sinks], axis=0)
    small = _pad_rows(small, d // 256)
    last = jnp.pad(scalar.reshape(1, 1), ((0, 0), (0, d - 1)))
    return _pad_rows(jnp.concatenate([ln_g.reshape(-1, d), final_g.reshape(-1, d), pool_w.reshape(-1, d),
                                      small.reshape(-1, d), last], axis=0), 8)


def _unpack_replicated(rows, d):
    n_pool = DEPTH * 4 * POOL_GROUP * POOL_GROUP // d
    n_small = -(-5 * DEPTH * 256 // d)
    ln_g = rows[:DEPTH]
    final_g = rows[DEPTH]
    pool_w = rows[DEPTH + 1: DEPTH + 1 + n_pool].reshape(DEPTH, 4, POOL_GROUP, POOL_GROUP)
    small = rows[DEPTH + 1 + n_pool: DEPTH + 1 + n_pool + n_small].reshape(-1, 256)[: 5 * DEPTH]
    pool_scale, conv_b, conv_ln_g, conv_ln_b = (small[DEPTH * k: DEPTH * (k + 1)] for k in range(4))
    sinks = small[4 * DEPTH: 5 * DEPTH, :N_Q_HEADS]
    scalar = rows[DEPTH + 1 + n_pool + n_small, 0]
    return ln_g, pool_w, pool_scale, conv_b, conv_ln_g, conv_ln_b, sinks, final_g, scalar


def _block_diag(pool_w):
    out = jnp.zeros((POOL_WIDTH, POOL_WIDTH), pool_w.dtype)
    for gi in range(4):
        out = out.at[POOL_GROUP * gi: POOL_GROUP * (gi + 1), POOL_GROUP * gi: POOL_GROUP * (gi + 1)].set(pool_w[gi])
    return out


def _diag_blocks(mat):
    return jnp.stack([mat[POOL_GROUP * gi: POOL_GROUP * (gi + 1), POOL_GROUP * gi: POOL_GROUP * (gi + 1)]
                      for gi in range(4)], axis=0)


def kernel(x, ln_g, w_in, pool_w, pool_scale, conv_dw, conv_b, conv_ln_g, conv_ln_b, conv_pw, attn_sinks, w_out, final_g, loss_target, m_ln_g, m_w_in, m_pool_w, m_pool_scale, m_conv_dw, m_conv_b, m_conv_ln_g, m_conv_ln_b, m_conv_pw, m_attn_sinks, m_w_out, m_final_g, v_ln_g, v_w_in, v_pool_w, v_pool_scale, v_conv_dw, v_conv_b, v_conv_ln_g, v_conv_ln_b, v_conv_pw, v_attn_sinks, v_w_out, v_final_g):
    x0 = x[0]
    d = x0.shape[1]
    row = lambda a: a.reshape(1, -1)
    slabs = lambda a: a.reshape(N_DEV, a.shape[0] // N_DEV, d)
    c_shard = CONV_WIDTH // N_DEV

    w_in_rows = [jnp.swapaxes(w_in[l], 0, 1).astype(MXU_DTYPE) for l in range(DEPTH)]
    w_out_rows = [w_out[l].astype(MXU_DTYPE) for l in range(DEPTH)]
    per_word = 4 // jnp.dtype(MXU_DTYPE).itemsize
    dw_t = jnp.stack([_dw_rows(conv_dw[l]) for l in range(DEPTH)], axis=0)
    dw_bits = (lax.bitcast_convert_type(dw_t, MXU_DTYPE) if per_word > 1 else dw_t).reshape(-1, d)
    n_pw = DEPTH * c_shard * CONV_WIDTH // d
    conv_rows = _pad_rows(jnp.concatenate([conv_pw.reshape(-1, d).astype(MXU_DTYPE), dw_bits], axis=0), 16)
    w_in_t, w_out_f = [None] * DEPTH, [None] * DEPTH
    w_in_t[0] = _all_gather(w_in_rows[0], "w_in_all_gather")
    wp_bd = [_block_diag(pool_w[l]).astype(MXU_DTYPE) for l in range(DEPTH)]

    xs, projs, cvs, ys = [x0], [], [], []
    q4 = D_IN // N_DEV // 4
    w_in_1 = [w_in_rows[1][q4 * k: q4 * (k + 1)] for k in range(4)]
    for l in range(DEPTH):
        proj, got = _in_proj(xs[l], row(ln_g[l]), w_in_t[l],
                             carried=[w_out_rows[0], conv_rows, w_in_1[0]] if l == 0 else [])
        if l == 0:
            w_in_1_got = [got[2]]
            w_out_f[0] = got[0].reshape(D_MIX, d)
            pw_all = got[1][:, :n_pw].reshape(N_DEV, DEPTH, c_shard, CONV_WIDTH)
            pw_f = [pw_all[:, k].reshape(CONV_WIDTH, CONV_WIDTH) for k in range(DEPTH)]
            bits = got[1][:, n_pw:n_pw + dw_bits.shape[0]].reshape(
                (N_DEV, DEPTH, c_shard, CONV_TAPS_PAD) + (per_word,) * (per_word > 1))
            dw_all = lax.bitcast_convert_type(bits, F32) if per_word > 1 else bits
            dw_f = [jnp.swapaxes(dw_all[:, k].reshape(CONV_WIDTH, CONV_TAPS_PAD), 0, 1) for k in range(DEPTH)]
        (y_pc, cv, diff), got = _poolconv_fwd(proj, wp_bd[l], row(pool_scale[l]), dw_f[l], row(conv_b[l]),
                                        row(conv_ln_g[l]), row(conv_ln_b[l]), pw_f[l],
                                        carried=[w_in_1[1], w_in_1[2]] if l == 0 else [])
        if l == 0:
            w_in_1_got += list(got)
        y_at, got = _attn_fwd(proj, attn_sinks[l], carried=[w_out_rows[1], w_in_1[3]] if l == 0 else [])
        if l == 0:
            w_out_f[1] = got[0].reshape(D_MIX, d)
            w_in_t[1] = jnp.concatenate(w_in_1_got + [got[1]], axis=1).reshape(D_IN, d)
        if l < DEPTH - 1:
            xs.append(_out_proj(xs[l], y_pc, y_at, w_out_f[l]))
        projs.append(proj)
        cvs.append((cv, diff))
        ys.append((y_pc, y_at))

    l = DEPTH - 1
    sq, g_final, dx = _out_proj_loss(xs[l], ys[l][0], ys[l][1], w_out_f[l], loss_target[0], row(final_g))

    l = 1
    (dy, g_wout1), _ = _out_bwd(dx, ys[l][0], ys[l][1], w_out_f[l])
    (da, g_wp1, g_pw1, g_dw1, g_vec1), _ = _poolconv_bwd(
        projs[l], *cvs[l], dy, wp_bd[l], row(pool_scale[l]), dw_f[l], row(conv_ln_g[l]), row(conv_ln_b[l]), pw_f[l])
    (dattn, gs1), _ = _attn_bwd(projs[l], dy, attn_sinks[l])
    (dx, g_ln1, g_win_t1), _ = _in_bwd(da, dattn, xs[l], dx, row(ln_g[l]), w_in_t[l])
    l = 0
    (dy, g_wout0), (r_wout1, r_small1) = _out_bwd(dx, ys[l][0], ys[l][1], w_out_f[l],
                                                  carried=[slabs(g_wout1), _small_slabs(g_pw1, g_dw1, d)])
    (da, g_wp0, g_pw0, g_dw0, g_vec0), (r_win1,) = _poolconv_bwd(
        projs[l], *cvs[l], dy, wp_bd[l], row(pool_scale[l]), dw_f[l], row(conv_ln_g[l]), row(conv_ln_b[l]), pw_f[l],
        carried=[slabs(g_win_t1)])
    (dattn, gs0), (r_wout0, r_small0) = _attn_bwd(projs[l], dy, attn_sinks[l],
                                                  carried=[slabs(g_wout0), _small_slabs(g_pw0, g_dw0, d)])
    (dx, g_ln0, g_win_t0), _ = _in_bwd(da, dattn, xs[l], dx, row(ln_g[l]), w_in_t[l])
    grad_x = dx[None]

    gv = jnp.stack([g_vec0, g_vec1], axis=0)
    rep_part = _pack_replicated(
        jnp.concatenate([g_ln0, g_ln1], axis=0), jnp.stack([_diag_blocks(g_wp0), _diag_blocks(g_wp1)], axis=0),
        gv[:, 0], gv[:, 1], gv[:, 2], gv[:, 3], jnp.stack([gs0[:, 0], gs1[:, 0]], axis=0), g_final, sq[0, 0], d)
    r_win0, r_rep = _final_exchange([slabs(g_win_t0), rep_part])

    t = lambda a: jnp.swapaxes(a, 1, 2)
    win = [t(o) for o in _adamw_layers([r_win0, r_win1], t(w_in), t(m_w_in), t(v_w_in), "adamw_w_in")]
    wout = _adamw_layers([r_wout0, r_wout1], w_out, m_w_out, v_w_out, "adamw_w_out")
    small = [[_unpack_small(o, d) for o in _adamw(
        r, _pack_small(conv_pw[l], conv_dw[l], d), _pack_small(m_conv_pw[l], m_conv_dw[l], d),
        _pack_small(v_conv_pw[l], v_conv_dw[l], d), f"adamw_conv_{l}")] for l, r in enumerate([r_small0, r_small1])]
    zero = jnp.zeros((), F32)
    pack_r = lambda *a: _pack_replicated(*a, zero, d)
    rep = [_unpack_replicated(o, d) for o in _adamw(
        r_rep, pack_r(ln_g, pool_w, pool_scale, conv_b, conv_ln_g, conv_ln_b, attn_sinks, final_g),
        pack_r(m_ln_g, m_pool_w, m_pool_scale, m_conv_b, m_conv_ln_g, m_conv_ln_b, m_attn_sinks, m_final_g),
        pack_r(v_ln_g, v_pool_w, v_pool_scale, v_conv_b, v_conv_ln_g, v_conv_ln_b, v_attn_sinks, v_final_g),
        "adamw_replicated")]
    loss = 0.5 / d * rep[0][-1]

    outs = []
    for k in range(4):
        r_ln, r_pool, r_scale, r_cb, r_lng, r_lnb, r_sinks, r_final, _ = rep[k]
        s_pw = jnp.stack([small[l][k][0] for l in range(DEPTH)], axis=0)
        s_dw = jnp.stack([small[l][k][1] for l in range(DEPTH)], axis=0)
        outs += [r_ln, win[k], r_pool, r_scale, s_dw, r_cb, r_lng, r_lnb, s_pw, r_sinks, wout[k], r_final]
    return (loss, grad_x, *outs)
```

```python
import jax
import jax.numpy as jnp
from jax import lax
from jax.experimental import pallas as pl
from jax.experimental.pallas import tpu as pltpu

F32 = jnp.float32
MXU_DTYPE = jnp.bfloat16
EXCHANGE_DTYPE = jnp.bfloat16

N_DEV = 8
DEPTH = 2
POOL_WIDTH = 256
POOL_GROUP = 64
CONV_WIDTH = 256
CONV_KERNEL = 31
CONV_TAPS_PAD = 32
HEAD_DIM = 64
N_KV_HEADS = 2
Q_PER_KV = 4
N_Q_HEADS = 8
ATTN_WIDTH = 512
BLOCK = 128
D_MIX = 1024
D_IN = 2560
HALF_IN = 1280
EPS = 1e-6
SCALE = HEAD_DIM ** -0.5
NEG = -1e30

ADAM_LR = 0.001
ADAM_B1 = 0.9
ADAM_B2 = 0.999
ADAM_EPS = 1e-08
ADAM_WD = 0.01
ADAM_STEP = 10

HALO = 32
ROW_TILE = 512
IN_BWD_TILE = 512
VMEM_LIMIT = 56 * 1024 * 1024

_NN = (((1,), (0,)), ((), ()))
_NT = (((1,), (1,)), ((), ()))
_TN = (((0,), (0,)), ((), ()))
_ANY = pl.BlockSpec(memory_space=pl.ANY)


def _mm(a, b, dims=_NN):
    return lax.dot_general(a.astype(MXU_DTYPE), b.astype(MXU_DTYPE), dims, preferred_element_type=F32)


def _sig(x):
    return 1.0 / (1.0 + jnp.exp(-x))


def _dsilu(z, s):
    return s * (1.0 + z * (1.0 - s))


def _params(n_grid):
    return pltpu.CompilerParams(dimension_semantics=("arbitrary",) * n_grid, vmem_limit_bytes=VMEM_LIMIT)


def _row_tile(rows, cap):
    t = min(rows, cap)
    while rows % t or t % 8:
        t -= 8
    return t


def _full(shape):
    return pl.BlockSpec(shape, lambda i: (0,) * len(shape))


def _mesh_pos():
    return lax.axis_index("x"), lax.axis_index("y"), lax.axis_index("c")


class _Exchange:
    def __init__(self, sources):
        self.sources = list(sources)
        self.n = len(self.sources)
        self.gather = [s.ndim == 2 for s in self.sources]

    def out_shapes(self):
        return [jax.ShapeDtypeStruct((N_DEV,) + s.shape[-2:], s.dtype) for s in self.sources]

    def scratch(self):
        if not self.n:
            return []
        return [pltpu.SemaphoreType.DMA((7 * self.n,)), pltpu.SemaphoreType.DMA((7 * self.n,)),
                pltpu.SemaphoreType.DMA((self.n,))]

    def copies(self, src_refs, dst_refs, sems):
        send_sems, recv_sems, local_sems = sems
        x, y, c = _mesh_pos()
        me = 4 * x + 2 * y + c
        out = []
        for a, (src, dst) in enumerate(zip(src_refs, dst_refs)):
            out.append(pltpu.make_async_copy(src if self.gather[a] else src.at[me], dst.at[me], local_sems.at[a]))
            for k in range(1, N_DEV):
                tx, ty, tc = x ^ ((k >> 2) & 1), y ^ ((k >> 1) & 1), c ^ (k & 1)
                out.append(pltpu.make_async_remote_copy(
                    src_ref=src if self.gather[a] else src.at[4 * tx + 2 * ty + tc], dst_ref=dst.at[me],
                    send_sem=send_sems.at[7 * a + k - 1], recv_sem=recv_sems.at[7 * a + k - 1],
                    device_id=(tx, ty, tc), device_id_type=pl.DeviceIdType.MESH))
        return out

    def run(self, refs, first, last):
        if not self.n:
            return
        src_refs, dst_refs, sems = refs

        @pl.when(first)
        def _():
            for cp in self.copies(src_refs, dst_refs, sems):
                cp.start()

        @pl.when(last)
        def _():
            for cp in self.copies(src_refs, dst_refs, sems):
                cp.wait()


def _split_refs(refs, n_in, n_out, exch):
    ins = refs[:n_in]
    srcs = refs[n_in:n_in + exch.n]
    outs = refs[n_in + exch.n:n_in + exch.n + n_out]
    dsts = refs[n_in + exch.n + n_out:n_in + 2 * exch.n + n_out]
    rest = refs[n_in + 2 * exch.n + n_out:]
    sems = rest[len(rest) - 3:] if exch.n else ()
    scratch = rest[:len(rest) - 3] if exch.n else rest
    return ins, outs, scratch, (srcs, dsts, sems)


def _final_exchange(sources):
    exch = _Exchange(sources)

    def body(*refs):
        _, _, _, xrefs = _split_refs(refs, 0, 0, exch)
        for cp in exch.copies(*xrefs):
            cp.start()
        for cp in exch.copies(*xrefs):
            cp.wait()

    return pl.pallas_call(
        body, name="final_exchange", out_shape=exch.out_shapes(),
        in_specs=[_ANY] * exch.n, out_specs=[_ANY] * exch.n, scratch_shapes=exch.scratch(),
    )(*exch.sources)


def _all_gather(shard, name):
    m_per, n = shard.shape

    def body(x_ref, out_ref, send_sems, recv_sems, local_sem):
        x, y, c = _mesh_pos()
        me, sibling = (x, y, c), (x, y, 1 - c)
        chips = [(1 - x, y), (x, 1 - y), (1 - x, 1 - y)]

        def rows(px, py, pc):
            return out_ref.at[pl.ds((4 * px + 2 * py + pc) * m_per, m_per), :]

        def copy(k, block, to, src=None):
            return pltpu.make_async_remote_copy(
                src_ref=rows(*block) if src is None else src, dst_ref=rows(*block),
                send_sem=send_sems.at[k], recv_sem=recv_sems.at[k],
                device_id=to, device_id_type=pl.DeviceIdType.MESH)

        mine = pltpu.make_async_copy(x_ref, rows(*me), local_sem)
        mine.start()
        first = [copy(0, me, sibling, src=x_ref)]
        first += [copy(1 + j, me, (*chip, c), src=x_ref) for j, chip in enumerate(chips)]
        for cp in first:
            cp.start()
        passed = [copy(4 + j, (*chip, c), sibling) for j, chip in enumerate(chips)]
        for j, chip in enumerate(chips):
            copy(1 + j, (*chip, c), me).wait_recv()
            passed[j].start()
        copy(0, sibling, me).wait_recv()
        for j, chip in enumerate(chips):
            copy(4 + j, (*chip, 1 - c), me).wait_recv()
        for cp in first + passed:
            cp.wait_send()
        mine.wait()

    return pl.pallas_call(
        body, name=name,
        out_shape=jax.ShapeDtypeStruct((N_DEV * m_per, n), shard.dtype),
        in_specs=[pl.BlockSpec(memory_space=pltpu.VMEM)],
        out_specs=pl.BlockSpec(memory_space=pltpu.VMEM),
        scratch_shapes=[pltpu.SemaphoreType.DMA((7,)), pltpu.SemaphoreType.DMA((7,)), pltpu.SemaphoreType.DMA],
        compiler_params=pltpu.CompilerParams(vmem_limit_bytes=VMEM_LIMIT),
    )(shard)


def _by_group(lane, v2, v4, v8, v16):
    return jnp.where(lane < 64, v2, jnp.where(lane < 128, v4, jnp.where(lane < 192, v8, v16)))


def _pool_count(t0, n):
    lane = lax.broadcasted_iota(jnp.int32, (1, POOL_WIDTH), 1)
    t = (t0 + lax.broadcasted_iota(jnp.int32, (n, 1), 0)).astype(F32)
    wnd = _by_group(lane, 2.0, 4.0, 8.0, 16.0)
    return jnp.minimum(t + 1.0, wnd)


def _pool_diff(u_ext, t0, ts):
    lane = lax.broadcasted_iota(jnp.int32, (1, POOL_WIDTH), 1)
    s2 = u_ext + pltpu.roll(u_ext, 1, 0)
    s4 = s2 + pltpu.roll(s2, 2, 0)
    s8 = s4 + pltpu.roll(s4, 4, 0)
    s16 = s8 + pltpu.roll(s8, 8, 0)
    pooled = _by_group(lane, s2, s4, s8, s16)[HALO:]
    return pooled / _pool_count(t0, ts) - u_ext[HALO:]


def _pool_diff_bwd(w, ts):
    n = w.shape[0]
    lane = lax.broadcasted_iota(jnp.int32, (1, POOL_WIDTH), 1)
    f2 = w + pltpu.roll(w, n - 1, 0)
    f4 = f2 + pltpu.roll(f2, n - 2, 0)
    f8 = f4 + pltpu.roll(f4, n - 4, 0)
    f16 = f8 + pltpu.roll(f8, n - 8, 0)
    return _by_group(lane, f2, f4, f8, f16)[:ts]


CONV_CHUNK = 64


def _conv_taps():
    return [(8 * m + r, r, m) for r in range(8) for m in range(4) if 8 * m + r < CONV_KERNEL]


def _store_shifted(dst_ref, x, up):
    n = x.shape[0]
    for r in range(8):
        dst_ref[r] = x if r == 0 else pltpu.roll(x, n - r if up else r, 0)


def _anticausal_conv(src_ref, dw_ref, out_ref, n_out):
    def chunk(c, carry):
        t0 = pl.multiple_of(c * CONV_CHUNK, CONV_CHUNK)
        acc = None
        for d, r, m in _conv_taps():
            term = dw_ref[pl.ds(CONV_KERNEL - 1 - d, 1), :] * src_ref[r, pl.ds(t0 + 8 * m, CONV_CHUNK), :]
            acc = term if acc is None else acc + term
        out_ref[pl.ds(t0, CONV_CHUNK), :] = acc
        return carry

    lax.fori_loop(0, n_out // CONV_CHUNK, chunk, 0)


def _depthwise_conv_weight_grad(x_ref, dout_up_ref, acc_ref, n_rows):
    acc_ref[...] = jnp.zeros_like(acc_ref)

    def chunk(c, carry):
        t0 = pl.multiple_of(c * CONV_CHUNK, CONV_CHUNK)
        xv = x_ref[pl.ds(t0, CONV_CHUNK), :]
        for d, r, m in _conv_taps():
            prod = xv * dout_up_ref[r, pl.ds(t0 + 8 * m, CONV_CHUNK), :]
            acc_ref[CONV_KERNEL - 1 - d] += jnp.sum(prod.reshape(CONV_CHUNK // 8, 8, prod.shape[-1]), axis=0)
        return carry

    lax.fori_loop(0, n_rows // CONV_CHUNK, chunk, 0)


def _layer_norm(cv):
    mu = jnp.mean(cv, axis=-1, keepdims=True)
    xc = cv - mu
    var = jnp.mean(xc * xc, axis=-1, keepdims=True)
    rstd = lax.rsqrt(var + EPS)
    return xc * rstd, rstd


PAIR_ROWS = 2 * BLOCK
PAIR_COLS = 4 * BLOCK
ATTN_BLOCKS_PER_STEP = 4


def _attn_blocks_per_step(s_len):
    qb = ATTN_BLOCKS_PER_STEP
    while (s_len // BLOCK) % qb:
        qb //= 2
    return qb


def _pair_rows(v0, v1):
    r = lax.broadcasted_iota(jnp.int32, (PAIR_ROWS, 1), 0)
    return jnp.where(r < BLOCK, v0, v1)


def _fill_attn_bias(bias_ref):
    rows = lax.broadcasted_iota(jnp.int32, (PAIR_ROWS, PAIR_COLS), 0)
    cols = lax.broadcasted_iota(jnp.int32, (PAIR_ROWS, PAIR_COLS), 1)
    key = cols & (2 * BLOCK - 1)
    dist = BLOCK + (rows & (BLOCK - 1)) - key
    in_band = (dist >= 0) & (dist < BLOCK)
    distf = dist.astype(F32)
    second = cols >= 2 * BLOCK
    for kh in range(N_KV_HEADS):
        slope_of = lambda j, a: 2.0 ** -(Q_PER_KV * kh + 2 * j + a + 1)
        slope = jnp.where(rows < BLOCK, jnp.where(second, slope_of(0, 1), slope_of(0, 0)),
                          jnp.where(second, slope_of(1, 1), slope_of(1, 0)))
        bias = -slope * distf
        bias_ref[0, kh] = jnp.where(in_band & (key >= BLOCK), bias, NEG)
        bias_ref[1, kh] = jnp.where(in_band, bias, NEG)


def _pair_block_matrix(x, x_swapped, kh):
    lo = lax.broadcasted_iota(jnp.int32, (1, 2 * HEAD_DIM), 1) < HEAD_DIM
    in_lo, in_hi = (x, x_swapped) if kh == 0 else (x_swapped, x)
    return jnp.concatenate([jnp.where(lo, in_lo, 0.0), jnp.where(lo, 0.0, in_hi)], axis=0).astype(MXU_DTYPE)


def _pair_queries(q, kh):
    return (_pair_stack(q, kh) * SCALE).astype(MXU_DTYPE)


def _pair_stack(a, kh):
    return jnp.concatenate([a[:, 2 * BLOCK * kh: 2 * BLOCK * kh + BLOCK],
                            a[:, 2 * BLOCK * kh + BLOCK: 2 * BLOCK * (kh + 1)]], axis=0)


def _pair_unstack(parts):
    return jnp.concatenate([p[BLOCK * j: BLOCK * (j + 1)] for p in parts for j in range(2)], axis=-1)


def _pair_softmax(s, kh, sinks_ref):
    ps, p_sinks = [], []
    for a in range(2):
        sa = s[:, 2 * BLOCK * a: 2 * BLOCK * (a + 1)]
        sink = _pair_rows(sinks_ref[Q_PER_KV * kh + a], sinks_ref[Q_PER_KV * kh + 2 + a])
        m = jnp.maximum(jnp.max(sa, axis=-1, keepdims=True), sink)
        e = jnp.exp(sa - m)
        es = jnp.exp(sink - m)
        inv = 1.0 / (jnp.sum(e, axis=-1, keepdims=True) + es)
        ps.append(e * inv)
        p_sinks.append(es * inv)
    return jnp.concatenate(ps, axis=-1), p_sinks


def _fold_pair_halves(t):
    lo = lax.broadcasted_iota(jnp.int32, (1, 2 * HEAD_DIM), 1) < HEAD_DIM
    u = jnp.where(lo, t[:2 * BLOCK], t[2 * BLOCK:])
    return u + pltpu.roll(u, HEAD_DIM, 1)


def _in_proj(x, g, w_t, carried=()):
    s_len, d = x.shape
    ts = _row_tile(s_len, ROW_TILE)
    nt = s_len // ts
    exch = _Exchange(carried)

    def body(*refs):
        (x_ref, g_ref, w_ref), (o_ref,), _, xrefs = _split_refs(refs, 3, 1, exch)
        i = pl.program_id(0)
        exch.run(xrefs, i == 0, i == nt - 1)
        xv = x_ref[...]
        r = lax.rsqrt(jnp.mean(xv * xv, axis=-1, keepdims=True) + EPS)
        o_ref[...] = _mm(xv * r * g_ref[...], w_ref[...], _NT)

    res = pl.pallas_call(
        body, name="in_proj_carrier" if exch.n else "in_proj", grid=(nt,),
        in_specs=[pl.BlockSpec((ts, d), lambda i: (i, 0)), _full((1, d)), _full((D_IN, d))] + [_ANY] * exch.n,
        out_specs=[pl.BlockSpec((ts, D_IN), lambda i: (i, 0))] + [_ANY] * exch.n,
        out_shape=[jax.ShapeDtypeStruct((s_len, D_IN), F32)] + exch.out_shapes(),
        scratch_shapes=exch.scratch(),
        compiler_params=_params(1),
    )(x, g, w_t, *exch.sources)
    return res[0], res[1:]


def _poolconv_fwd(proj, wp, scale, dw, cb, lng, lnb, pw, carried=()):
    s_len = proj.shape[0]
    ts = _row_tile(s_len, ROW_TILE)
    hb = ts // HALO
    nt = s_len // ts
    exch = _Exchange(carried)

    def body(*refs):
        ins, (y_ref, cv_ref, diff_ref), _, xrefs = _split_refs(refs, 9, 3, exch)
        p_ref, ph_ref, wp_ref, sc_ref, dw_ref, cb_ref, lng_ref, lnb_ref, pw_ref = ins
        i = pl.program_id(0)
        exch.run(xrefs, i == 0, i == nt - 1)
        cur = p_ref[...]
        halo = jnp.where(i > 0, ph_ref[...], 0.0)
        ext = jnp.concatenate([halo, cur], axis=0)
        diff = _pool_diff(ext[:, 0:256], i * ts, ts).astype(MXU_DTYPE)
        diff_ref[...] = diff
        gp = cur[:, 256:512]
        y_pool = _mm(diff, wp_ref[...]) * sc_ref[...] * (gp * _sig(gp))
        hh = ext[:, 512:768] * _sig(ext[:, 768:1024])
        shifted = [hh if r == 0 else pltpu.roll(hh, r, 0) for r in range(8)]
        cv = cb_ref[...]
        for dist, r, m in _conv_taps():
            cv = cv + dw_ref[pl.ds(CONV_KERNEL - 1 - dist, 1), :] * shifted[r][HALO - 8 * m: HALO - 8 * m + ts]
        cv_ref[...] = cv
        n, _ = _layer_norm(cv)
        z = n * lng_ref[...] + lnb_ref[...]
        gc = cur[:, 1024:1280]
        y_conv = _mm(z * _sig(z), pw_ref[...]) * (gc * _sig(gc))
        y_ref[...] = jnp.concatenate([y_pool, y_conv], axis=-1).astype(y_ref.dtype)

    vec = _full((1, 256))
    res = pl.pallas_call(
        body, name="poolconv_fwd_carrier" if exch.n else "poolconv_fwd", grid=(nt,),
        in_specs=[pl.BlockSpec((ts, HALF_IN), lambda i: (i, 0)),
                  pl.BlockSpec((HALO, HALF_IN), lambda i: (jnp.maximum(i * hb - 1, 0), 0)),
                  _full((256, 256)), vec, _full((CONV_TAPS_PAD, 256)), vec, vec, vec, _full((256, 256))]
        + [_ANY] * exch.n,
        out_specs=[pl.BlockSpec((ts, 512), lambda i: (i, 0)), pl.BlockSpec((ts, 256), lambda i: (i, 0)),
                   pl.BlockSpec((ts, 256), lambda i: (i, 0))] + [_ANY] * exch.n,
        out_shape=[jax.ShapeDtypeStruct((s_len, 512), MXU_DTYPE), jax.ShapeDtypeStruct((s_len, 256), F32),
                   jax.ShapeDtypeStruct((s_len, 256), MXU_DTYPE)] + exch.out_shapes(),
        scratch_shapes=exch.scratch(),
        compiler_params=_params(1),
    )(proj, proj, wp, scale, dw, cb, lng, lnb, pw, *exch.sources)
    return res[:3], res[3:]


def _attn_fwd(proj, sinks, carried=()):
    s_len = proj.shape[0]
    qb = _attn_blocks_per_step(s_len)
    ts = qb * BLOCK
    nt = s_len // ts
    exch = _Exchange(carried)

    def body(*refs):
        (p_ref, kvp_ref, sinks_ref), (y_ref,), (bias_ref,), xrefs = _split_refs(refs, 3, 1, exch)
        i = pl.program_id(0)
        exch.run(xrefs, i == 0, i == nt - 1)

        @pl.when(i == 0)
        def _():
            _fill_attn_bias(bias_ref)

        probs = [(b, kh) for b in range(qb) for kh in range(N_KV_HEADS)]
        scores, v_bds = {}, {}
        for b in range(qb):
            r0 = BLOCK * b
            kv_prev = kvp_ref[...] if b == 0 else p_ref[r0 - BLOCK:r0, 512:768]
            kv2 = jnp.concatenate([kv_prev, p_ref[r0:r0 + BLOCK, 512:768]], axis=0)
            k2, v2 = kv2[:, :BLOCK], kv2[:, BLOCK:]
            k2_swapped, v2_swapped = pltpu.roll(k2, HEAD_DIM, 1), pltpu.roll(v2, HEAD_DIM, 1)
            variant = jnp.where(i == 0, 0, 1) if b == 0 else 1
            q = p_ref[r0:r0 + BLOCK, 0:512]
            for kh in range(N_KV_HEADS):
                k_bd = _pair_block_matrix(k2, k2_swapped, kh)
                v_bds[b, kh] = _pair_block_matrix(v2, v2_swapped, kh)
                scores[b, kh] = _mm(_pair_queries(q, kh), k_bd, _NT) + bias_ref[variant, kh]
        ps = {pr: _pair_softmax(scores[pr], pr[1], sinks_ref)[0] for pr in probs}
        outs = {pr: _mm(ps[pr], v_bds[pr]) for pr in probs}
        for b in range(qb):
            r0 = BLOCK * b
            ga = p_ref[r0:r0 + BLOCK, 768:1280]
            ya = _pair_unstack([outs[b, kh] for kh in range(N_KV_HEADS)])
            y_ref[r0:r0 + BLOCK, :] = (ya * (ga * _sig(ga))).astype(y_ref.dtype)

    res = pl.pallas_call(
        body, name="attn_fwd_carrier" if exch.n else "attn_fwd", grid=(nt,),
        in_specs=[pl.BlockSpec((ts, HALF_IN), lambda i: (i, 1)),
                  pl.BlockSpec((BLOCK, 256), lambda i: (jnp.maximum(i * qb - 1, 0), 7)),
                  pl.BlockSpec(memory_space=pltpu.SMEM)] + [_ANY] * exch.n,
        out_specs=[pl.BlockSpec((ts, 512), lambda i: (i, 0))] + [_ANY] * exch.n,
        out_shape=[jax.ShapeDtypeStruct((s_len, 512), MXU_DTYPE)] + exch.out_shapes(),
        scratch_shapes=[pltpu.VMEM((2, N_KV_HEADS, PAIR_ROWS, PAIR_COLS), F32)] + exch.scratch(),
        compiler_params=_params(1),
    )(proj, proj, sinks, *exch.sources)
    return res[0], res[1:]


def _out_in_proj(x, y_pc, y_at, w_out, g_next, w_t_next):
    s_len, d = x.shape
    ts = _row_tile(s_len, ROW_TILE)

    def body(x_ref, a_ref, b_ref, w_ref, g_ref, wn_ref, o_ref, p_ref):
        y = jnp.concatenate([a_ref[...], b_ref[...]], axis=-1)
        xv = x_ref[...] + _mm(y, w_ref[...])
        o_ref[...] = xv
        r = lax.rsqrt(jnp.mean(xv * xv, axis=-1, keepdims=True) + EPS)
        p_ref[...] = _mm(xv * r * g_ref[...], wn_ref[...], _NT)

    tile = pl.BlockSpec((ts, d), lambda i: (i, 0))
    half = pl.BlockSpec((ts, 512), lambda i: (i, 0))
    return pl.pallas_call(
        body, name="out_in_proj", grid=(s_len // ts,),
        in_specs=[tile, half, half, _full((D_MIX, d)), _full((1, d)), _full((D_IN, d))],
        out_specs=[tile, pl.BlockSpec((ts, D_IN), lambda i: (i, 0))],
        out_shape=[jax.ShapeDtypeStruct((s_len, d), F32), jax.ShapeDtypeStruct((s_len, D_IN), F32)],
        compiler_params=_params(1),
    )(x, y_pc, y_at, w_out, g_next, w_t_next)


def _out_proj_loss(x, y_pc, y_at, w_out, target, g):
    s_len, d = x.shape
    ts = _row_tile(s_len, ROW_TILE)

    def body(x_ref, a_ref, b_ref, w_ref, t_ref, g_ref, sq_ref, dg_ref, dx_ref):
        i = pl.program_id(0)
        y = jnp.concatenate([a_ref[...], b_ref[...]], axis=-1)
        xv = x_ref[...] + _mm(y, w_ref[...])
        gv = g_ref[...]
        r = lax.rsqrt(jnp.mean(xv * xv, axis=-1, keepdims=True) + EPS)
        xr = xv * r
        err = xr * gv - t_ref[...]
        dout = err * (1.0 / d)
        w = dout * gv
        dx_ref[...] = r * (w - xr * jnp.mean(w * xr, axis=-1, keepdims=True))

        @pl.when(i == 0)
        def _():
            sq_ref[...] = jnp.zeros_like(sq_ref)
            dg_ref[...] = jnp.zeros_like(dg_ref)

        sq = jnp.sum(jnp.sum(err * err, axis=-1, keepdims=True), axis=0, keepdims=True)
        sq_ref[...] += jnp.broadcast_to(sq, sq_ref.shape)
        dg_ref[...] += jnp.sum(dout * xr, axis=0, keepdims=True)

    tile = pl.BlockSpec((ts, d), lambda i: (i, 0))
    half = pl.BlockSpec((ts, 512), lambda i: (i, 0))
    return pl.pallas_call(
        body, name="out_proj_loss", grid=(s_len // ts,),
        in_specs=[tile, half, half, _full((D_MIX, d)), tile, _full((1, d))],
        out_specs=[_full((1, 128)), _full((1, d)), tile],
        out_shape=[jax.ShapeDtypeStruct((1, 128), F32), jax.ShapeDtypeStruct((1, d), F32),
                   jax.ShapeDtypeStruct((s_len, d), F32)],
        compiler_params=_params(1),
    )(x, y_pc, y_at, w_out, target, g)


def _out_bwd(dxo, y_pc, y_at, w_out, carried=()):
    s_len, d = dxo.shape
    ts = _row_tile(s_len, ROW_TILE)
    nt = s_len // ts
    exch = _Exchange(carried)

    def body(*refs):
        (dx_ref, a_ref, b_ref, w_ref), (dy_ref, gw_ref), (acc_ref,), xrefs = _split_refs(refs, 4, 2, exch)
        i = pl.program_id(0)
        exch.run(xrefs, i == 0, i == nt - 1)
        dxv = dx_ref[...].astype(MXU_DTYPE)
        dy_ref[...] = _mm(dxv, w_ref[...], _NT)

        @pl.when(i == 0)
        def _():
            acc_ref[...] = jnp.zeros_like(acc_ref)

        y = jnp.concatenate([a_ref[...], b_ref[...]], axis=-1)
        acc_ref[...] += _mm(y, dxv, _TN)

        @pl.when(i == nt - 1)
        def _():
            gw_ref[...] = acc_ref[...].astype(gw_ref.dtype)

    res = pl.pallas_call(
        body, name="out_bwd_carrier" if exch.n else "out_bwd", grid=(nt,),
        in_specs=[pl.BlockSpec((ts, d), lambda i: (i, 0)), pl.BlockSpec((ts, 512), lambda i: (i, 0)),
                  pl.BlockSpec((ts, 512), lambda i: (i, 0)), _full((D_MIX, d))] + [_ANY] * exch.n,
        out_specs=[pl.BlockSpec((ts, D_MIX), lambda i: (i, 0)), _full((D_MIX, d))] + [_ANY] * exch.n,
        out_shape=[jax.ShapeDtypeStruct((s_len, D_MIX), F32), jax.ShapeDtypeStruct((D_MIX, d), EXCHANGE_DTYPE)]
        + exch.out_shapes(),
        scratch_shapes=[pltpu.VMEM((D_MIX, d), F32)] + exch.scratch(),
        compiler_params=_params(1),
    )(dxo, y_pc, y_at, w_out, *exch.sources)
    return res[:2], res[2:]


def _poolconv_bwd(proj, cv, diff, dy, wp, scale, dw, lng, lnb, pw, carried=()):
    s_len = proj.shape[0]
    ts = _row_tile(s_len, ROW_TILE)
    hb = ts // HALO
    nt = s_len // ts
    last_halo = s_len // HALO - 1
    n = ts + HALO
    exch = _Exchange(carried)

    def body(*refs):
        ins, outs, (up_ref, hh_ref, dhh_ref, gdw_acc_ref), xrefs = _split_refs(refs, 13, 5, exch)
        (p_ref, pn_ref, cv_ref, cvn_ref, diff_ref, dy_ref, dyn_ref, wp_ref, sc_ref, dw_ref, lng_ref, lnb_ref,
         pw_ref) = ins
        da_ref, gwp_ref, gpw_ref, gdw_ref, gvec_ref = outs
        i = pl.program_id(0)
        exch.run(xrefs, i == 0, i == nt - 1)
        has_next = i < nt - 1
        cur = p_ref[...]
        nxt = jnp.where(has_next, pn_ref[...], 0.0)
        dyx = jnp.concatenate([dy_ref[...], jnp.where(has_next, dyn_ref[...], 0.0)], axis=0)
        row = lax.broadcasted_iota(jnp.int32, (n, 1), 0)
        in_seq = (row < ts) | has_next
        scale_v = sc_ref[...]

        cvx = jnp.concatenate([cv_ref[...], jnp.where(has_next, cvn_ref[...], 0.0)], axis=0)
        nrm, rstd = _layer_norm(cvx)
        z = nrm * lng_ref[...] + lnb_ref[...]
        sz = _sig(z)
        sw = z * sz
        gc = jnp.concatenate([cur[:, 1024:1280], nxt[:, 1024:1280]], axis=0)
        sgc = _sig(gc)
        yc = _mm(sw, pw_ref[...])
        dyc = dyx[:, 256:512]
        d_yc = dyc * (gc * sgc)
        d_gc = (dyc * yc * _dsilu(gc, sgc))[:ts]
        d_z = _mm(d_yc, pw_ref[...], _NT) * _dsilu(z, sz)
        d_n = d_z * lng_ref[...]
        d_cv = rstd * (d_n - jnp.mean(d_n, axis=-1, keepdims=True)
                       - nrm * jnp.mean(d_n * nrm, axis=-1, keepdims=True))
        d_cv = jnp.where(in_seq, d_cv, 0.0)
        _store_shifted(up_ref, d_cv, up=True)
        _anticausal_conv(up_ref, dw_ref, dhh_ref, ts)
        d_hh = dhh_ref[...]
        a_c, sb_c = cur[:, 512:768], _sig(cur[:, 768:1024])
        hh_ref[...] = a_c * sb_c
        d_a = d_hh * sb_c
        d_b = d_hh * a_c * sb_c * (1.0 - sb_c)
        d_cv_t = d_cv[:ts]
        _depthwise_conv_weight_grad(hh_ref, up_ref, gdw_acc_ref, ts)

        diff = diff_ref[...]
        raw = _mm(diff, wp_ref[...])
        gp = jnp.concatenate([cur[:, 256:512], nxt[:, 256:512]], axis=0)
        sgp = _sig(gp)
        dyp = dyx[:, 0:256]
        d_yp = dyp * (gp * sgp)
        d_gp = dyp[:ts] * (raw * scale_v) * _dsilu(gp, sgp)[:ts]
        d_raw = d_yp * scale_v
        d_diff = _mm(d_raw, wp_ref[...], _NT)
        w = jnp.where(in_seq, d_diff / _pool_count(i * ts, n), 0.0)
        d_u = _pool_diff_bwd(w, ts) - d_diff[:ts]

        da_ref[...] = jnp.concatenate([d_u, d_gp, d_a, d_b, d_gc], axis=-1).astype(da_ref.dtype)

        @pl.when(i == 0)
        def _():
            gwp_ref[...] = jnp.zeros_like(gwp_ref)
            gpw_ref[...] = jnp.zeros_like(gpw_ref)
            gdw_ref[...] = jnp.zeros_like(gdw_ref)
            gvec_ref[...] = jnp.zeros_like(gvec_ref)

        gwp_ref[...] += _mm(diff, d_raw[:ts], _TN)
        gpw_ref[...] += _mm(sw[:ts], d_yc[:ts], _TN)
        gdw_ref[...] += jnp.sum(gdw_acc_ref[...], axis=1)
        zero_row = jnp.zeros((1, 256), F32)
        gvec_ref[...] += jnp.concatenate([
            jnp.sum(d_yp[:ts] * raw, axis=0, keepdims=True),
            jnp.sum(d_cv_t, axis=0, keepdims=True),
            jnp.sum((d_z * nrm)[:ts], axis=0, keepdims=True),
            jnp.sum(d_z[:ts], axis=0, keepdims=True),
            zero_row, zero_row, zero_row, zero_row], axis=0)

    vec = _full((1, 256))
    nxt_halo = lambda i: (jnp.minimum((i + 1) * hb, last_halo), 0)
    res = pl.pallas_call(
        body, name="poolconv_bwd_carrier" if exch.n else "poolconv_bwd", grid=(nt,),
        in_specs=[pl.BlockSpec((ts, HALF_IN), lambda i: (i, 0)), pl.BlockSpec((HALO, HALF_IN), nxt_halo),
                  pl.BlockSpec((ts, 256), lambda i: (i, 0)), pl.BlockSpec((HALO, 256), nxt_halo),
                  pl.BlockSpec((ts, 256), lambda i: (i, 0)),
                  pl.BlockSpec((ts, 512), lambda i: (i, 0)), pl.BlockSpec((HALO, 512), nxt_halo),
                  _full((256, 256)), vec, _full((CONV_TAPS_PAD, 256)), vec, vec, _full((256, 256))]
        + [_ANY] * exch.n,
        out_specs=[pl.BlockSpec((ts, HALF_IN), lambda i: (i, 0)), _full((256, 256)), _full((256, 256)),
                   _full((CONV_TAPS_PAD, 256)), _full((8, 256))] + [_ANY] * exch.n,
        out_shape=[jax.ShapeDtypeStruct((s_len, HALF_IN), MXU_DTYPE), jax.ShapeDtypeStruct((256, 256), F32),
                   jax.ShapeDtypeStruct((256, 256), F32), jax.ShapeDtypeStruct((CONV_TAPS_PAD, 256), F32),
                   jax.ShapeDtypeStruct((8, 256), F32)] + exch.out_shapes(),
        scratch_shapes=[pltpu.VMEM((8, n, CONV_WIDTH), F32), pltpu.VMEM((ts, CONV_WIDTH), F32),
                        pltpu.VMEM((ts, CONV_WIDTH), F32), pltpu.VMEM((CONV_TAPS_PAD, 8, CONV_WIDTH), F32)]
        + exch.scratch(),
        compiler_params=_params(1),
    )(proj, proj, cv, cv, diff, dy, dy, wp, scale, dw, lng, lnb, pw, *exch.sources)
    return res[:5], res[5:]


DQ0, DKC0, DVC0, DKP0, DVP0, DGA0, DATTN_W = 0, 512, 640, 768, 896, 1024, 1536


def _attn_bwd(proj, dy, sinks, carried=()):
    s_len = proj.shape[0]
    qb = _attn_blocks_per_step(s_len)
    ts = qb * BLOCK
    nt = s_len // ts
    exch = _Exchange(carried)

    def body(*refs):
        (p_ref, kvp_ref, dy_ref, sinks_ref), (o_ref, gs_ref), (bias_ref,), xrefs = _split_refs(refs, 4, 2, exch)
        i = pl.program_id(0)
        exch.run(xrefs, i == 0, i == nt - 1)

        @pl.when(i == 0)
        def _():
            _fill_attn_bias(bias_ref)
            gs_ref[...] = jnp.zeros_like(gs_ref)

        lo = lax.broadcasted_iota(jnp.int32, (1, 2 * HEAD_DIM), 1) < HEAD_DIM
        probs = [(b, kh) for b in range(qb) for kh in range(N_KV_HEADS)]
        kv_heads = range(N_KV_HEADS)
        scores, k_bds, v_bds, q2s, do2s, dyas, gas, sgas = {}, {}, {}, {}, {}, {}, {}, {}
        for b in range(qb):
            r0 = BLOCK * b
            kv_prev = kvp_ref[...] if b == 0 else p_ref[r0 - BLOCK:r0, 512:768]
            kv2 = jnp.concatenate([kv_prev, p_ref[r0:r0 + BLOCK, 512:768]], axis=0)
            k2, v2 = kv2[:, :BLOCK], kv2[:, BLOCK:]
            k2_swapped, v2_swapped = pltpu.roll(k2, HEAD_DIM, 1), pltpu.roll(v2, HEAD_DIM, 1)
            variant = jnp.where(i == 0, 0, 1) if b == 0 else 1
            q = p_ref[r0:r0 + BLOCK, 0:512]
            gas[b] = p_ref[r0:r0 + BLOCK, 768:1280]
            dyas[b] = dy_ref[r0:r0 + BLOCK, :]
            sgas[b] = _sig(gas[b])
            d_o = dyas[b] * (gas[b] * sgas[b])
            for kh in kv_heads:
                k_bds[b, kh] = _pair_block_matrix(k2, k2_swapped, kh)
                v_bds[b, kh] = _pair_block_matrix(v2, v2_swapped, kh)
                q2s[b, kh] = _pair_queries(q, kh)
                do2s[b, kh] = _pair_stack(d_o, kh)
                scores[b, kh] = _mm(q2s[b, kh], k_bds[b, kh], _NT) + bias_ref[variant, kh]
        softmaxes = {pr: _pair_softmax(scores[pr], pr[1], sinks_ref) for pr in probs}
        ps = {pr: softmaxes[pr][0] for pr in probs}
        outs = {pr: _mm(ps[pr], v_bds[pr]) for pr in probs}
        dps = {pr: _mm(do2s[pr], v_bds[pr], _NT) for pr in probs}
        dss = {}
        d_sinks = [None] * N_Q_HEADS
        for pr in probs:
            p, dp, kh = ps[pr], dps[pr], pr[1]
            ds_halves = []
            for a in range(2):
                cols = slice(2 * BLOCK * a, 2 * BLOCK * (a + 1))
                delta = jnp.sum(p[:, cols] * dp[:, cols], axis=-1, keepdims=True)
                ds_halves.append(p[:, cols] * (dp[:, cols] - delta))
                dsink = -softmaxes[pr][1][a] * delta
                for j in range(2):
                    part = jnp.sum(dsink[BLOCK * j: BLOCK * (j + 1)], axis=0, keepdims=True)
                    h = Q_PER_KV * kh + 2 * j + a
                    d_sinks[h] = part if d_sinks[h] is None else d_sinks[h] + part
            dss[pr] = jnp.concatenate(ds_halves, axis=-1)
        dqs = {pr: _mm(dss[pr], k_bds[pr]) * SCALE for pr in probs}
        dks = {pr: _fold_pair_halves(_mm(dss[pr], q2s[pr], _TN)) for pr in probs}
        dvs = {pr: _fold_pair_halves(_mm(ps[pr], do2s[pr], _TN)) for pr in probs}
        for b in range(qb):
            r0 = BLOCK * b
            dk = jnp.where(lo, dks[b, 0], dks[b, 1])
            dv = jnp.where(lo, dvs[b, 0], dvs[b, 1])
            d_ga = dyas[b] * _pair_unstack([outs[b, kh] for kh in kv_heads]) * _dsilu(gas[b], sgas[b])
            o_ref[r0:r0 + BLOCK, :] = jnp.concatenate(
                [_pair_unstack([dqs[b, kh] for kh in kv_heads]), dk[BLOCK:], dv[BLOCK:], dk[:BLOCK], dv[:BLOCK],
                 d_ga], axis=-1)
        gs_ref[...] += jnp.broadcast_to(jnp.concatenate(d_sinks, axis=0), gs_ref.shape)

    res = pl.pallas_call(
        body, name="attn_bwd_carrier" if exch.n else "attn_bwd", grid=(nt,),
        in_specs=[pl.BlockSpec((ts, HALF_IN), lambda i: (i, 1)),
                  pl.BlockSpec((BLOCK, 256), lambda i: (jnp.maximum(i * qb - 1, 0), 7)),
                  pl.BlockSpec((ts, 512), lambda i: (i, 1)),
                  pl.BlockSpec(memory_space=pltpu.SMEM)] + [_ANY] * exch.n,
        out_specs=[pl.BlockSpec((ts, DATTN_W), lambda i: (i, 0)), _full((N_Q_HEADS, 128))] + [_ANY] * exch.n,
        out_shape=[jax.ShapeDtypeStruct((s_len, DATTN_W), F32), jax.ShapeDtypeStruct((N_Q_HEADS, 128), F32)]
        + exch.out_shapes(),
        scratch_shapes=[pltpu.VMEM((2, N_KV_HEADS, PAIR_ROWS, PAIR_COLS), F32)] + exch.scratch(),
        compiler_params=_params(1),
    )(proj, proj, dy, sinks, *exch.sources)
    return res[:2], res[2:]


def _in_bwd(da, dattn, x, dxo, g, w_t, carried=()):
    s_len, d = x.shape
    ts = _row_tile(s_len, IN_BWD_TILE)
    bpt = ts // BLOCK
    nt = s_len // ts
    last_block = s_len // BLOCK - 1
    exch = _Exchange(carried)

    def body(*refs):
        ins, (dx_ref, dg_ref, gw_ref), (acc_ref, stage_ref, stage_sem), xrefs = _split_refs(refs, 7, 3, exch)
        da_ref, dat_ref, nxt_ref, x_ref, dxo_ref, g_ref, w_ref = ins
        i = pl.program_id(0)
        exch.run(xrefs, i == 0, i == nt - 1)
        dat = dat_ref[...]
        nxt = jnp.where(i < nt - 1, nxt_ref[...], 0.0)
        shifted = jnp.concatenate([dat[BLOCK:, DKP0:DGA0], nxt], axis=0) if bpt > 1 else nxt
        dkv = dat[:, DKC0:DKP0] + shifted
        dproj = jnp.concatenate([da_ref[...], dat[:, DQ0:DKC0].astype(MXU_DTYPE), dkv.astype(MXU_DTYPE),
                                 dat[:, DGA0:DATTN_W].astype(MXU_DTYPE)], axis=-1)
        d_h = _mm(dproj, w_ref[...])
        xv = x_ref[...]
        gv = g_ref[...]
        r = lax.rsqrt(jnp.mean(xv * xv, axis=-1, keepdims=True) + EPS)
        xr = xv * r
        w = d_h * gv
        dx_ref[...] = dxo_ref[...] + r * (w - xr * jnp.mean(w * xr, axis=-1, keepdims=True))

        @pl.when(i == 0)
        def _():
            dg_ref[...] = jnp.zeros_like(dg_ref)
            acc_ref[...] = jnp.zeros_like(acc_ref)

        dg_ref[...] += jnp.sum(d_h * xr, axis=0, keepdims=True)
        acc_ref[...] += _mm(dproj, xr * gv, _TN)

        @pl.when(i == nt - 1)
        def _():
            stage_ref[...] = acc_ref[...].astype(stage_ref.dtype)
            out = pltpu.make_async_copy(stage_ref, gw_ref, stage_sem)
            out.start()
            out.wait()

    res = pl.pallas_call(
        body, name="in_bwd_carrier" if exch.n else "in_bwd", grid=(nt,),
        in_specs=[pl.BlockSpec((ts, HALF_IN), lambda i: (i, 0)),
                  pl.BlockSpec((ts, DATTN_W), lambda i: (i, 0)),
                  pl.BlockSpec((BLOCK, 256), lambda i: (jnp.minimum((i + 1) * bpt, last_block), 3)),
                  pl.BlockSpec((ts, d), lambda i: (i, 0)), pl.BlockSpec((ts, d), lambda i: (i, 0)),
                  _full((1, d)), pl.BlockSpec((D_IN, d), lambda i: (0, 0), pipeline_mode=pl.Buffered(1))]
        + [_ANY] * exch.n,
        out_specs=[pl.BlockSpec((ts, d), lambda i: (i, 0)), _full((1, d)), _ANY] + [_ANY] * exch.n,
        out_shape=[jax.ShapeDtypeStruct((s_len, d), F32), jax.ShapeDtypeStruct((1, d), F32),
                   jax.ShapeDtypeStruct((D_IN, d), EXCHANGE_DTYPE)] + exch.out_shapes(),
        scratch_shapes=[pltpu.VMEM((D_IN, d), F32), pltpu.VMEM((D_IN, d), EXCHANGE_DTYPE), pltpu.SemaphoreType.DMA]
        + exch.scratch(),
        compiler_params=_params(1),
    )(da, dattn, dattn, x, dxo, g, w_t, *exch.sources)
    return res[:3], res[3:]


def _in_bwd_dw(da, dattn, x, g):
    s_len, d = x.shape
    ts = _row_tile(s_len, IN_BWD_TILE)
    bpt = ts // BLOCK
    nt = s_len // ts
    last_block = s_len // BLOCK - 1

    def body(da_ref, dat_ref, nxt_ref, x_ref, g_ref, dp_ref, gw_ref, acc_ref, stage_ref, stage_sem):
        i = pl.program_id(0)
        dat = dat_ref[...]
        nxt = jnp.where(i < nt - 1, nxt_ref[...], 0.0)
        shifted = jnp.concatenate([dat[BLOCK:, DKP0:DGA0], nxt], axis=0) if bpt > 1 else nxt
        dkv = dat[:, DKC0:DKP0] + shifted
        dproj = jnp.concatenate([da_ref[...], dat[:, DQ0:DKC0].astype(MXU_DTYPE), dkv.astype(MXU_DTYPE),
                                 dat[:, DGA0:DATTN_W].astype(MXU_DTYPE)], axis=-1)
        dp_ref[...] = dproj
        xv = x_ref[...]
        r = lax.rsqrt(jnp.mean(xv * xv, axis=-1, keepdims=True) + EPS)

        @pl.when(i == 0)
        def _():
            acc_ref[...] = jnp.zeros_like(acc_ref)

        acc_ref[...] += _mm(dproj, xv * r * g_ref[...], _TN)

        @pl.when(i == nt - 1)
        def _():
            stage_ref[...] = acc_ref[...].astype(stage_ref.dtype)
            out = pltpu.make_async_copy(stage_ref, gw_ref, stage_sem)
            out.start()
            out.wait()

    return pl.pallas_call(
        body, name="in_bwd_dw", grid=(nt,),
        in_specs=[pl.BlockSpec((ts, HALF_IN), lambda i: (i, 0)),
                  pl.BlockSpec((ts, DATTN_W), lambda i: (i, 0)),
                  pl.BlockSpec((BLOCK, 256), lambda i: (jnp.minimum((i + 1) * bpt, last_block), 3)),
                  pl.BlockSpec((ts, d), lambda i: (i, 0)), _full((1, d))],
        out_specs=[pl.BlockSpec((ts, D_IN), lambda i: (i, 0)), _ANY],
        out_shape=[jax.ShapeDtypeStruct((s_len, D_IN), MXU_DTYPE), jax.ShapeDtypeStruct((D_IN, d), EXCHANGE_DTYPE)],
        scratch_shapes=[pltpu.VMEM((D_IN, d), F32), pltpu.VMEM((D_IN, d), EXCHANGE_DTYPE), pltpu.SemaphoreType.DMA],
        compiler_params=_params(1),
    )(da, dattn, dattn, x, g)


def _in_bwd_dx(dproj, x, dxo, g, w_t, carried=()):
    s_len, d = x.shape
    ts = _row_tile(s_len, ROW_TILE)
    nt = s_len // ts
    exch = _Exchange(carried)

    def body(*refs):
        (dp_ref, x_ref, dxo_ref, g_ref, w_ref), (dx_ref, dg_ref), _, xrefs = _split_refs(refs, 5, 2, exch)
        i = pl.program_id(0)
        exch.run(xrefs, i == 0, i == nt - 1)
        d_h = _mm(dp_ref[...], w_ref[...])
        xv = x_ref[...]
        r = lax.rsqrt(jnp.mean(xv * xv, axis=-1, keepdims=True) + EPS)
        xr = xv * r
        w = d_h * g_ref[...]
        dx_ref[...] = dxo_ref[...] + r * (w - xr * jnp.mean(w * xr, axis=-1, keepdims=True))

        @pl.when(i == 0)
        def _():
            dg_ref[...] = jnp.zeros_like(dg_ref)

        dg_ref[...] += jnp.sum(d_h * xr, axis=0, keepdims=True)

    tile = pl.BlockSpec((ts, d), lambda i: (i, 0))
    res = pl.pallas_call(
        body, name="in_bwd_dx_carrier" if exch.n else "in_bwd_dx", grid=(nt,),
        in_specs=[pl.BlockSpec((ts, D_IN), lambda i: (i, 0)), tile, tile, _full((1, d)), _full((D_IN, d))]
        + [_ANY] * exch.n,
        out_specs=[tile, _full((1, d))] + [_ANY] * exch.n,
        out_shape=[jax.ShapeDtypeStruct((s_len, d), F32), jax.ShapeDtypeStruct((1, d), F32)] + exch.out_shapes(),
        scratch_shapes=exch.scratch(),
        compiler_params=_params(1),
    )(dproj, x, dxo, g, w_t, *exch.sources)
    return res[:2], res[2:]


def _sum_partials(p_ref):
    g = p_ref[0].astype(F32)
    for k in range(1, N_DEV):
        g = g + p_ref[k].astype(F32)
    return g


def _adamw_step(g, w, m, v):
    nm = ADAM_B1 * m + (1.0 - ADAM_B1) * g
    nv = ADAM_B2 * v + (1.0 - ADAM_B2) * (g * g)
    m_hat = nm / (1.0 - ADAM_B1 ** ADAM_STEP)
    v_hat = nv / (1.0 - ADAM_B2 ** ADAM_STEP)
    return -ADAM_LR * (m_hat / (jnp.sqrt(v_hat) + ADAM_EPS) + ADAM_WD * w), nm, nv


def _adamw_layers(parts, w, m, v, name):
    def body(*refs):
        p_refs = refs[:DEPTH]
        w_ref, m_ref, v_ref, g_ref, d_ref, nm_ref, nv_ref = refs[DEPTH:]
        for l in range(DEPTH):
            g = _sum_partials(p_refs[l])
            g_ref[l] = g
            d_ref[l], nm_ref[l], nv_ref[l] = _adamw_step(g, w_ref[l], m_ref[l], v_ref[l])

    vmem = pl.BlockSpec(memory_space=pltpu.VMEM)
    shape = jax.ShapeDtypeStruct(w.shape, F32)
    return pl.pallas_call(
        body, name=name, in_specs=[vmem] * (DEPTH + 3), out_specs=[vmem] * 4, out_shape=[shape] * 4,
        compiler_params=pltpu.CompilerParams(vmem_limit_bytes=VMEM_LIMIT),
    )(*parts, w, m, v)


def _adamw(parts, w, m, v, name):
    rows, n = w.shape
    tr = _row_tile(rows, 256)

    def body(p_ref, w_ref, m_ref, v_ref, g_ref, d_ref, nm_ref, nv_ref):
        g = _sum_partials(p_ref)
        g_ref[...] = g
        d_ref[...], nm_ref[...], nv_ref[...] = _adamw_step(g, w_ref[...], m_ref[...], v_ref[...])

    tile = pl.BlockSpec((tr, n), lambda i: (i, 0))
    shape = jax.ShapeDtypeStruct((rows, n), F32)
    return pl.pallas_call(
        body, name=name, grid=(rows // tr,),
        in_specs=[pl.BlockSpec((N_DEV, tr, n), lambda i: (0, i, 0)), tile, tile, tile],
        out_specs=[tile, tile, tile, tile],
        out_shape=[shape, shape, shape, shape],
        compiler_params=_params(1),
    )(parts, w, m, v)


def _pad_rows(a, mult):
    pad = (-a.shape[0]) % mult
    return a if pad == 0 else jnp.concatenate([a, jnp.zeros((pad, a.shape[1]), a.dtype)], axis=0)


def _dw_rows(conv_dw_l):
    return jnp.pad(jnp.swapaxes(conv_dw_l, 0, 1), ((0, 0), (0, CONV_TAPS_PAD - CONV_KERNEL)))


def _pack_small(pw_l, dw_l, d):
    rows = jnp.concatenate([pw_l.reshape(-1, d), _dw_rows(dw_l).reshape(-1, d)], axis=0)
    return _pad_rows(rows, 8)


def _small_slabs(g_pw, g_dw, d):
    a = g_pw.reshape(N_DEV, -1, d)
    b = jnp.swapaxes(g_dw, 0, 1).reshape(N_DEV, -1, d)
    used = a.shape[1] + b.shape[1]
    return jnp.concatenate([a, b, jnp.zeros((N_DEV, (-used) % 8, d), g_pw.dtype)], axis=1)


def _unpack_small(rows, d):
    c = CONV_WIDTH // N_DEV
    n_pw = c * CONV_WIDTH // d
    n_dw = c * CONV_TAPS_PAD // d
    pw = rows[:n_pw].reshape(c, CONV_WIDTH)
    dw = jnp.swapaxes(rows[n_pw:n_pw + n_dw].reshape(c, CONV_TAPS_PAD), 0, 1)[:CONV_KERNEL]
    return pw, dw


def _pack_replicated(ln_g, pool_w, pool_scale, conv_b, conv_ln_g, conv_ln_b, attn_sinks, final_g, scalar, d):
    sinks = jnp.pad(attn_sinks, ((0, 0), (0, 256 - N_Q_HEADS)))
    small = jnp.concatenate([pool_scale, conv_b, conv_ln_g, conv_ln_b, sinks], axis=0)
    small = _pad_rows(small, d // 256)
    last = jnp.pad(scalar.reshape(1, 1), ((0, 0), (0, d - 1)))
    return _pad_rows(jnp.concatenate([ln_g.reshape(-1, d), final_g.reshape(-1, d), pool_w.reshape(-1, d),
                                      small.reshape(-1, d), last], axis=0), 8)


def _unpack_replicated(rows, d):
    n_pool = DEPTH * 4 * POOL_GROUP * POOL_GROUP // d
    n_small = -(-5 * DEPTH * 256 // d)
    ln_g = rows[:DEPTH]
    final_g = rows[DEPTH]
    pool_w = rows[DEPTH + 1: DEPTH + 1 + n_pool].reshape(DEPTH, 4, POOL_GROUP, POOL_GROUP)
    small = rows[DEPTH + 1 + n_pool: DEPTH + 1 + n_pool + n_small].reshape(-1, 256)[: 5 * DEPTH]
    pool_scale, conv_b, conv_ln_g, conv_ln_b = (small[DEPTH * k: DEPTH * (k + 1)] for k in range(4))
    sinks = small[4 * DEPTH: 5 * DEPTH, :N_Q_HEADS]
    scalar = rows[DEPTH + 1 + n_pool + n_small, 0]
    return ln_g, pool_w, pool_scale, conv_b, conv_ln_g, conv_ln_b, sinks, final_g, scalar


def _block_diag(pool_w):
    out = jnp.zeros((POOL_WIDTH, POOL_WIDTH), pool_w.dtype)
    for gi in range(4):
        out = out.at[POOL_GROUP * gi: POOL_GROUP * (gi + 1), POOL_GROUP * gi: POOL_GROUP * (gi + 1)].set(pool_w[gi])
    return out


def _diag_blocks(mat):
    return jnp.stack([mat[POOL_GROUP * gi: POOL_GROUP * (gi + 1), POOL_GROUP * gi: POOL_GROUP * (gi + 1)]
                      for gi in range(4)], axis=0)


def kernel(x, ln_g, w_in, pool_w, pool_scale, conv_dw, conv_b, conv_ln_g, conv_ln_b, conv_pw, attn_sinks, w_out, final_g, loss_target, m_ln_g, m_w_in, m_pool_w, m_pool_scale, m_conv_dw, m_conv_b, m_conv_ln_g, m_conv_ln_b, m_conv_pw, m_attn_sinks, m_w_out, m_final_g, v_ln_g, v_w_in, v_pool_w, v_pool_scale, v_conv_dw, v_conv_b, v_conv_ln_g, v_conv_ln_b, v_conv_pw, v_attn_sinks, v_w_out, v_final_g):
    x0 = x[0]
    d = x0.shape[1]
    row = lambda a: a.reshape(1, -1)
    slabs = lambda a: a.reshape(N_DEV, a.shape[0] // N_DEV, d)
    c_shard = CONV_WIDTH // N_DEV

    w_in_rows = [jnp.swapaxes(w_in[l], 0, 1).astype(MXU_DTYPE) for l in range(DEPTH)]
    w_out_rows = [w_out[l].astype(MXU_DTYPE) for l in range(DEPTH)]
    per_word = 4 // jnp.dtype(MXU_DTYPE).itemsize
    dw_t = jnp.stack([_dw_rows(conv_dw[l]) for l in range(DEPTH)], axis=0)
    dw_bits = (lax.bitcast_convert_type(dw_t, MXU_DTYPE) if per_word > 1 else dw_t).reshape(-1, d)
    n_pw = DEPTH * c_shard * CONV_WIDTH // d
    conv_rows = _pad_rows(jnp.concatenate([conv_pw.reshape(-1, d).astype(MXU_DTYPE), dw_bits], axis=0), 16)
    w_in_t, w_out_f = [None] * DEPTH, [None] * DEPTH
    w_in_t[0] = _all_gather(w_in_rows[0], "w_in_all_gather")
    wp_bd = [_block_diag(pool_w[l]).astype(MXU_DTYPE) for l in range(DEPTH)]

    xs, projs, cvs, ys = [x0], [], [], []
    q4 = D_IN // N_DEV // 4
    w_in_1 = [w_in_rows[1][q4 * k: q4 * (k + 1)] for k in range(4)]
    for l in range(DEPTH):
        if l == 0:
            proj, got = _in_proj(xs[0], row(ln_g[0]), w_in_t[0], carried=[w_out_rows[0], conv_rows, w_in_1[0]])
            w_in_1_got = [got[2]]
            w_out_f[0] = got[0].reshape(D_MIX, d)
            pw_all = got[1][:, :n_pw].reshape(N_DEV, DEPTH, c_shard, CONV_WIDTH)
            pw_f = [pw_all[:, k].reshape(CONV_WIDTH, CONV_WIDTH) for k in range(DEPTH)]
            bits = got[1][:, n_pw:n_pw + dw_bits.shape[0]].reshape(
                (N_DEV, DEPTH, c_shard, CONV_TAPS_PAD) + (per_word,) * (per_word > 1))
            dw_all = lax.bitcast_convert_type(bits, F32) if per_word > 1 else bits
            dw_f = [jnp.swapaxes(dw_all[:, k].reshape(CONV_WIDTH, CONV_TAPS_PAD), 0, 1) for k in range(DEPTH)]
        (y_pc, cv, diff), got = _poolconv_fwd(proj, wp_bd[l], row(pool_scale[l]), dw_f[l], row(conv_b[l]),
                                        row(conv_ln_g[l]), row(conv_ln_b[l]), pw_f[l],
                                        carried=[w_in_1[1], w_in_1[2]] if l == 0 else [])
        if l == 0:
            w_in_1_got += list(got)
        y_at, got = _attn_fwd(proj, attn_sinks[l], carried=[w_out_rows[1], w_in_1[3]] if l == 0 else [])
        if l == 0:
            w_out_f[1] = got[0].reshape(D_MIX, d)
            w_in_t[1] = jnp.concatenate(w_in_1_got + [got[1]], axis=1).reshape(D_IN, d)
        projs.append(proj)
        cvs.append((cv, diff))
        ys.append((y_pc, y_at))
        if l < DEPTH - 1:
            x_next, proj = _out_in_proj(xs[l], y_pc, y_at, w_out_f[l], row(ln_g[l + 1]), w_in_t[l + 1])
            xs.append(x_next)

    l = DEPTH - 1
    sq, g_final, dx = _out_proj_loss(xs[l], ys[l][0], ys[l][1], w_out_f[l], loss_target[0], row(final_g))

    l = 1
    (dy, g_wout1), _ = _out_bwd(dx, ys[l][0], ys[l][1], w_out_f[l])
    (da, g_wp1, g_pw1, g_dw1, g_vec1), _ = _poolconv_bwd(
        projs[l], *cvs[l], dy, wp_bd[l], row(pool_scale[l]), dw_f[l], row(conv_ln_g[l]), row(conv_ln_b[l]), pw_f[l])
    (dattn, gs1), _ = _attn_bwd(projs[l], dy, attn_sinks[l])
    (dx, g_ln1, g_win_t1), _ = _in_bwd(da, dattn, xs[l], dx, row(ln_g[l]), w_in_t[l])
    l = 0
    (dy, g_wout0), _ = _out_bwd(dx, ys[l][0], ys[l][1], w_out_f[l])
    (da, g_wp0, g_pw0, g_dw0, g_vec0), (r_win1, r_wout1, r_small1) = _poolconv_bwd(
        projs[l], *cvs[l], dy, wp_bd[l], row(pool_scale[l]), dw_f[l], row(conv_ln_g[l]), row(conv_ln_b[l]), pw_f[l],
        carried=[slabs(g_win_t1), slabs(g_wout1), _small_slabs(g_pw1, g_dw1, d)])
    (dattn, gs0), (r_wout0, r_small0) = _attn_bwd(projs[l], dy, attn_sinks[l],
                                                  carried=[slabs(g_wout0), _small_slabs(g_pw0, g_dw0, d)])
    dproj, g_win_t0 = _in_bwd_dw(da, dattn, xs[l], row(ln_g[l]))
    (dx, g_ln0), (r_win0,) = _in_bwd_dx(dproj, xs[l], dx, row(ln_g[l]), w_in_t[l], carried=[slabs(g_win_t0)])
    grad_x = dx[None]

    gv = jnp.stack([g_vec0, g_vec1], axis=0)
    rep_part = _pack_replicated(
        jnp.concatenate([g_ln0, g_ln1], axis=0), jnp.stack([_diag_blocks(g_wp0), _diag_blocks(g_wp1)], axis=0),
        gv[:, 0], gv[:, 1], gv[:, 2], gv[:, 3], jnp.stack([gs0[:, 0], gs1[:, 0]], axis=0), g_final, sq[0, 0], d)
    (r_rep,) = _final_exchange([rep_part])

    t = lambda a: jnp.swapaxes(a, 1, 2)
    win = [t(o) for o in _adamw_layers([r_win0, r_win1], t(w_in), t(m_w_in), t(v_w_in), "adamw_w_in")]
    wout = _adamw_layers([r_wout0, r_wout1], w_out, m_w_out, v_w_out, "adamw_w_out")
    small = [[_unpack_small(o, d) for o in _adamw(
        r, _pack_small(conv_pw[l], conv_dw[l], d), _pack_small(m_conv_pw[l], m_conv_dw[l], d),
        _pack_small(v_conv_pw[l], v_conv_dw[l], d), f"adamw_conv_{l}")] for l, r in enumerate([r_small0, r_small1])]
    zero = jnp.zeros((), F32)
    pack_r = lambda *a: _pack_replicated(*a, zero, d)
    rep = [_unpack_replicated(o, d) for o in _adamw(
        r_rep, pack_r(ln_g, pool_w, pool_scale, conv_b, conv_ln_g, conv_ln_b, attn_sinks, final_g),
        pack_r(m_ln_g, m_pool_w, m_pool_scale, m_conv_b, m_conv_ln_g, m_conv_ln_b, m_attn_sinks, m_final_g),
        pack_r(v_ln_g, v_pool_w, v_pool_scale, v_conv_b, v_conv_ln_g, v_conv_ln_b, v_attn_sinks, v_final_g),
        "adamw_replicated")]
    loss = 0.5 / d * rep[0][-1]

    outs = []
    for k in range(4):
        r_ln, r_pool, r_scale, r_cb, r_lng, r_lnb, r_sinks, r_final, _ = rep[k]
        s_pw = jnp.stack([small[l][k][0] for l in range(DEPTH)], axis=0)
        s_dw = jnp.stack([small[l][k][1] for l in range(DEPTH)], axis=0)
        outs += [r_ln, win[k], r_pool, r_scale, s_dw, r_cb, r_lng, r_lnb, s_pw, r_sinks, wout[k], r_final]
    return (loss, grad_x, *outs)
```

```python
import jax
import jax.numpy as jnp
from jax import lax
from jax.experimental import pallas as pl
from jax.experimental.pallas import tpu as pltpu

F32 = jnp.float32
MXU_DTYPE = jnp.bfloat16
EXCHANGE_DTYPE = jnp.bfloat16

N_DEV = 8
DEPTH = 2
POOL_WIDTH = 256
POOL_GROUP = 64
CONV_WIDTH = 256
CONV_KERNEL = 31
CONV_TAPS_PAD = 32
HEAD_DIM = 64
N_KV_HEADS = 2
Q_PER_KV = 4
N_Q_HEADS = 8
ATTN_WIDTH = 512
BLOCK = 128
D_MIX = 1024
D_IN = 2560
HALF_IN = 1280
EPS = 1e-6
SCALE = HEAD_DIM ** -0.5
NEG = -1e30

ADAM_LR = 0.001
ADAM_B1 = 0.9
ADAM_B2 = 0.999
ADAM_EPS = 1e-08
ADAM_WD = 0.01
ADAM_STEP = 10

HALO = 32
ROW_TILE = 512
IN_BWD_TILE = 512
VMEM_LIMIT = 56 * 1024 * 1024

_NN = (((1,), (0,)), ((), ()))
_NT = (((1,), (1,)), ((), ()))
_TN = (((0,), (0,)), ((), ()))
_ANY = pl.BlockSpec(memory_space=pl.ANY)


def _mm(a, b, dims=_NN):
    return lax.dot_general(a.astype(MXU_DTYPE), b.astype(MXU_DTYPE), dims, preferred_element_type=F32)


def _sig(x):
    return 1.0 / (1.0 + jnp.exp(-x))


def _dsilu(z, s):
    return s * (1.0 + z * (1.0 - s))


def _params(n_grid):
    return pltpu.CompilerParams(dimension_semantics=("arbitrary",) * n_grid, vmem_limit_bytes=VMEM_LIMIT)


def _row_tile(rows, cap):
    t = min(rows, cap)
    while rows % t or t % 8:
        t -= 8
    return t


def _full(shape):
    return pl.BlockSpec(shape, lambda i: (0,) * len(shape))


def _mesh_pos():
    return lax.axis_index("x"), lax.axis_index("y"), lax.axis_index("c")


class _Exchange:
    def __init__(self, sources):
        self.sources = list(sources)
        self.n = len(self.sources)
        self.gather = [s.ndim == 2 for s in self.sources]

    def out_shapes(self):
        return [jax.ShapeDtypeStruct((N_DEV,) + s.shape[-2:], s.dtype) for s in self.sources]

    def scratch(self):
        if not self.n:
            return []
        return [pltpu.SemaphoreType.DMA((7 * self.n,)), pltpu.SemaphoreType.DMA((7 * self.n,)),
                pltpu.SemaphoreType.DMA((self.n,))]

    def copies(self, src_refs, dst_refs, sems):
        send_sems, recv_sems, local_sems = sems
        x, y, c = _mesh_pos()
        me = 4 * x + 2 * y + c
        out = []
        for a, (src, dst) in enumerate(zip(src_refs, dst_refs)):
            out.append(pltpu.make_async_copy(src if self.gather[a] else src.at[me], dst.at[me], local_sems.at[a]))
            for k in range(1, N_DEV):
                tx, ty, tc = x ^ ((k >> 2) & 1), y ^ ((k >> 1) & 1), c ^ (k & 1)
                out.append(pltpu.make_async_remote_copy(
                    src_ref=src if self.gather[a] else src.at[4 * tx + 2 * ty + tc], dst_ref=dst.at[me],
                    send_sem=send_sems.at[7 * a + k - 1], recv_sem=recv_sems.at[7 * a + k - 1],
                    device_id=(tx, ty, tc), device_id_type=pl.DeviceIdType.MESH))
        return out

    def run(self, refs, first, last):
        if not self.n:
            return
        src_refs, dst_refs, sems = refs

        @pl.when(first)
        def _():
            for cp in self.copies(src_refs, dst_refs, sems):
                cp.start()

        @pl.when(last)
        def _():
            for cp in self.copies(src_refs, dst_refs, sems):
                cp.wait()


def _split_refs(refs, n_in, n_out, exch):
    ins = refs[:n_in]
    srcs = refs[n_in:n_in + exch.n]
    outs = refs[n_in + exch.n:n_in + exch.n + n_out]
    dsts = refs[n_in + exch.n + n_out:n_in + 2 * exch.n + n_out]
    rest = refs[n_in + 2 * exch.n + n_out:]
    sems = rest[len(rest) - 3:] if exch.n else ()
    scratch = rest[:len(rest) - 3] if exch.n else rest
    return ins, outs, scratch, (srcs, dsts, sems)


def _final_exchange(sources):
    exch = _Exchange(sources)

    def body(*refs):
        _, _, _, xrefs = _split_refs(refs, 0, 0, exch)
        for cp in exch.copies(*xrefs):
            cp.start()
        for cp in exch.copies(*xrefs):
            cp.wait()

    return pl.pallas_call(
        body, name="final_exchange", out_shape=exch.out_shapes(),
        in_specs=[_ANY] * exch.n, out_specs=[_ANY] * exch.n, scratch_shapes=exch.scratch(),
    )(*exch.sources)


def _all_gather(shard, name):
    m_per, n = shard.shape

    def body(x_ref, out_ref, send_sems, recv_sems, local_sem):
        x, y, c = _mesh_pos()
        me, sibling = (x, y, c), (x, y, 1 - c)
        chips = [(1 - x, y), (x, 1 - y), (1 - x, 1 - y)]

        def rows(px, py, pc):
            return out_ref.at[pl.ds((4 * px + 2 * py + pc) * m_per, m_per), :]

        def copy(k, block, to, src=None):
            return pltpu.make_async_remote_copy(
                src_ref=rows(*block) if src is None else src, dst_ref=rows(*block),
                send_sem=send_sems.at[k], recv_sem=recv_sems.at[k],
                device_id=to, device_id_type=pl.DeviceIdType.MESH)

        mine = pltpu.make_async_copy(x_ref, rows(*me), local_sem)
        mine.start()
        first = [copy(0, me, sibling, src=x_ref)]
        first += [copy(1 + j, me, (*chip, c), src=x_ref) for j, chip in enumerate(chips)]
        for cp in first:
            cp.start()
        passed = [copy(4 + j, (*chip, c), sibling) for j, chip in enumerate(chips)]
        for j, chip in enumerate(chips):
            copy(1 + j, (*chip, c), me).wait_recv()
            passed[j].start()
        copy(0, sibling, me).wait_recv()
        for j, chip in enumerate(chips):
            copy(4 + j, (*chip, 1 - c), me).wait_recv()
        for cp in first + passed:
            cp.wait_send()
        mine.wait()

    return pl.pallas_call(
        body, name=name,
        out_shape=jax.ShapeDtypeStruct((N_DEV * m_per, n), shard.dtype),
        in_specs=[pl.BlockSpec(memory_space=pltpu.VMEM)],
        out_specs=pl.BlockSpec(memory_space=pltpu.VMEM),
        scratch_shapes=[pltpu.SemaphoreType.DMA((7,)), pltpu.SemaphoreType.DMA((7,)), pltpu.SemaphoreType.DMA],
        compiler_params=pltpu.CompilerParams(vmem_limit_bytes=VMEM_LIMIT),
    )(shard)


def _by_group(lane, v2, v4, v8, v16):
    return jnp.where(lane < 64, v2, jnp.where(lane < 128, v4, jnp.where(lane < 192, v8, v16)))


def _pool_count(t0, n):
    lane = lax.broadcasted_iota(jnp.int32, (1, POOL_WIDTH), 1)
    t = (t0 + lax.broadcasted_iota(jnp.int32, (n, 1), 0)).astype(F32)
    wnd = _by_group(lane, 2.0, 4.0, 8.0, 16.0)
    return jnp.minimum(t + 1.0, wnd)


def _pool_diff(u_ext, t0, ts):
    lane = lax.broadcasted_iota(jnp.int32, (1, POOL_WIDTH), 1)
    s2 = u_ext + pltpu.roll(u_ext, 1, 0)
    s4 = s2 + pltpu.roll(s2, 2, 0)
    s8 = s4 + pltpu.roll(s4, 4, 0)
    s16 = s8 + pltpu.roll(s8, 8, 0)
    pooled = _by_group(lane, s2, s4, s8, s16)[HALO:]
    return pooled / _pool_count(t0, ts) - u_ext[HALO:]


def _pool_diff_bwd(w, ts):
    n = w.shape[0]
    lane = lax.broadcasted_iota(jnp.int32, (1, POOL_WIDTH), 1)
    f2 = w + pltpu.roll(w, n - 1, 0)
    f4 = f2 + pltpu.roll(f2, n - 2, 0)
    f8 = f4 + pltpu.roll(f4, n - 4, 0)
    f16 = f8 + pltpu.roll(f8, n - 8, 0)
    return _by_group(lane, f2, f4, f8, f16)[:ts]


CONV_CHUNK = 64


def _conv_taps():
    return [(8 * m + r, r, m) for r in range(8) for m in range(4) if 8 * m + r < CONV_KERNEL]


def _store_shifted(dst_ref, x, up):
    n = x.shape[0]
    for r in range(8):
        dst_ref[r] = x if r == 0 else pltpu.roll(x, n - r if up else r, 0)


def _anticausal_conv(src_ref, dw_ref, out_ref, n_out):
    def chunk(c, carry):
        t0 = pl.multiple_of(c * CONV_CHUNK, CONV_CHUNK)
        acc = None
        for d, r, m in _conv_taps():
            term = dw_ref[pl.ds(CONV_KERNEL - 1 - d, 1), :] * src_ref[r, pl.ds(t0 + 8 * m, CONV_CHUNK), :]
            acc = term if acc is None else acc + term
        out_ref[pl.ds(t0, CONV_CHUNK), :] = acc
        return carry

    lax.fori_loop(0, n_out // CONV_CHUNK, chunk, 0)


def _depthwise_conv_weight_grad(x_ref, dout_up_ref, acc_ref, n_rows):
    acc_ref[...] = jnp.zeros_like(acc_ref)

    def chunk(c, carry):
        t0 = pl.multiple_of(c * CONV_CHUNK, CONV_CHUNK)
        xv = x_ref[pl.ds(t0, CONV_CHUNK), :]
        for d, r, m in _conv_taps():
            prod = xv * dout_up_ref[r, pl.ds(t0 + 8 * m, CONV_CHUNK), :]
            acc_ref[CONV_KERNEL - 1 - d] += jnp.sum(prod.reshape(CONV_CHUNK // 8, 8, prod.shape[-1]), axis=0)
        return carry

    lax.fori_loop(0, n_rows // CONV_CHUNK, chunk, 0)


def _layer_norm(cv):
    mu = jnp.mean(cv, axis=-1, keepdims=True)
    xc = cv - mu
    var = jnp.mean(xc * xc, axis=-1, keepdims=True)
    rstd = lax.rsqrt(var + EPS)
    return xc * rstd, rstd


PAIR_ROWS = 2 * BLOCK
PAIR_COLS = 4 * BLOCK
ATTN_BLOCKS_PER_STEP = 4


def _attn_blocks_per_step(s_len):
    qb = ATTN_BLOCKS_PER_STEP
    while (s_len // BLOCK) % qb:
        qb //= 2
    return qb


def _pair_rows(v0, v1):
    r = lax.broadcasted_iota(jnp.int32, (PAIR_ROWS, 1), 0)
    return jnp.where(r < BLOCK, v0, v1)


def _fill_attn_bias(bias_ref):
    rows = lax.broadcasted_iota(jnp.int32, (PAIR_ROWS, PAIR_COLS), 0)
    cols = lax.broadcasted_iota(jnp.int32, (PAIR_ROWS, PAIR_COLS), 1)
    key = cols & (2 * BLOCK - 1)
    dist = BLOCK + (rows & (BLOCK - 1)) - key
    in_band = (dist >= 0) & (dist < BLOCK)
    distf = dist.astype(F32)
    second = cols >= 2 * BLOCK
    for kh in range(N_KV_HEADS):
        slope_of = lambda j, a: 2.0 ** -(Q_PER_KV * kh + 2 * j + a + 1)
        slope = jnp.where(rows < BLOCK, jnp.where(second, slope_of(0, 1), slope_of(0, 0)),
                          jnp.where(second, slope_of(1, 1), slope_of(1, 0)))
        bias = -slope * distf
        bias_ref[0, kh] = jnp.where(in_band & (key >= BLOCK), bias, NEG)
        bias_ref[1, kh] = jnp.where(in_band, bias, NEG)


def _pair_block_matrix(x, x_swapped, kh):
    lo = lax.broadcasted_iota(jnp.int32, (1, 2 * HEAD_DIM), 1) < HEAD_DIM
    in_lo, in_hi = (x, x_swapped) if kh == 0 else (x_swapped, x)
    return jnp.concatenate([jnp.where(lo, in_lo, 0.0), jnp.where(lo, 0.0, in_hi)], axis=0).astype(MXU_DTYPE)


def _pair_queries(q, kh):
    return (_pair_stack(q, kh) * SCALE).astype(MXU_DTYPE)


def _pair_stack(a, kh):
    return jnp.concatenate([a[:, 2 * BLOCK * kh: 2 * BLOCK * kh + BLOCK],
                            a[:, 2 * BLOCK * kh + BLOCK: 2 * BLOCK * (kh + 1)]], axis=0)


def _pair_unstack(parts):
    return jnp.concatenate([p[BLOCK * j: BLOCK * (j + 1)] for p in parts for j in range(2)], axis=-1)


def _pair_softmax(s, kh, sinks_ref):
    ps, p_sinks = [], []
    for a in range(2):
        sa = s[:, 2 * BLOCK * a: 2 * BLOCK * (a + 1)]
        sink = _pair_rows(sinks_ref[Q_PER_KV * kh + a], sinks_ref[Q_PER_KV * kh + 2 + a])
        m = jnp.maximum(jnp.max(sa, axis=-1, keepdims=True), sink)
        e = jnp.exp(sa - m)
        es = jnp.exp(sink - m)
        inv = 1.0 / (jnp.sum(e, axis=-1, keepdims=True) + es)
        ps.append(e * inv)
        p_sinks.append(es * inv)
    return jnp.concatenate(ps, axis=-1), p_sinks


def _fold_pair_halves(t):
    lo = lax.broadcasted_iota(jnp.int32, (1, 2 * HEAD_DIM), 1) < HEAD_DIM
    u = jnp.where(lo, t[:2 * BLOCK], t[2 * BLOCK:])
    return u + pltpu.roll(u, HEAD_DIM, 1)


def _in_proj(x, g, w_t, carried=()):
    s_len, d = x.shape
    ts = _row_tile(s_len, ROW_TILE)
    nt = s_len // ts
    exch = _Exchange(carried)

    def body(*refs):
        (x_ref, g_ref, w_ref), (o_ref,), _, xrefs = _split_refs(refs, 3, 1, exch)
        i = pl.program_id(0)
        exch.run(xrefs, i == 0, i == nt - 1)
        xv = x_ref[...]
        r = lax.rsqrt(jnp.mean(xv * xv, axis=-1, keepdims=True) + EPS)
        o_ref[...] = _mm(xv * r * g_ref[...], w_ref[...], _NT)

    res = pl.pallas_call(
        body, name="in_proj_carrier" if exch.n else "in_proj", grid=(nt,),
        in_specs=[pl.BlockSpec((ts, d), lambda i: (i, 0)), _full((1, d)), _full((D_IN, d))] + [_ANY] * exch.n,
        out_specs=[pl.BlockSpec((ts, D_IN), lambda i: (i, 0))] + [_ANY] * exch.n,
        out_shape=[jax.ShapeDtypeStruct((s_len, D_IN), F32)] + exch.out_shapes(),
        scratch_shapes=exch.scratch(),
        compiler_params=_params(1),
    )(x, g, w_t, *exch.sources)
    return res[0], res[1:]


def _poolconv_fwd(proj, wp, scale, dw, cb, lng, lnb, pw, carried=()):
    s_len = proj.shape[0]
    ts = _row_tile(s_len, ROW_TILE)
    hb = ts // HALO
    nt = s_len // ts
    exch = _Exchange(carried)

    def body(*refs):
        ins, (y_ref, cv_ref, diff_ref), _, xrefs = _split_refs(refs, 9, 3, exch)
        p_ref, ph_ref, wp_ref, sc_ref, dw_ref, cb_ref, lng_ref, lnb_ref, pw_ref = ins
        i = pl.program_id(0)
        exch.run(xrefs, i == 0, i == nt - 1)
        cur = p_ref[...]
        halo = jnp.where(i > 0, ph_ref[...], 0.0)
        ext = jnp.concatenate([halo, cur], axis=0)
        diff = _pool_diff(ext[:, 0:256], i * ts, ts).astype(MXU_DTYPE)
        diff_ref[...] = diff
        gp = cur[:, 256:512]
        y_pool = _mm(diff, wp_ref[...]) * sc_ref[...] * (gp * _sig(gp))
        hh = ext[:, 512:768] * _sig(ext[:, 768:1024])
        shifted = [hh if r == 0 else pltpu.roll(hh, r, 0) for r in range(8)]
        cv = cb_ref[...]
        for dist, r, m in _conv_taps():
            cv = cv + dw_ref[pl.ds(CONV_KERNEL - 1 - dist, 1), :] * shifted[r][HALO - 8 * m: HALO - 8 * m + ts]
        cv_ref[...] = cv
        n, _ = _layer_norm(cv)
        z = n * lng_ref[...] + lnb_ref[...]
        gc = cur[:, 1024:1280]
        y_conv = _mm(z * _sig(z), pw_ref[...]) * (gc * _sig(gc))
        y_ref[...] = jnp.concatenate([y_pool, y_conv], axis=-1).astype(y_ref.dtype)

    vec = _full((1, 256))
    res = pl.pallas_call(
        body, name="poolconv_fwd_carrier" if exch.n else "poolconv_fwd", grid=(nt,),
        in_specs=[pl.BlockSpec((ts, HALF_IN), lambda i: (i, 0)),
                  pl.BlockSpec((HALO, HALF_IN), lambda i: (jnp.maximum(i * hb - 1, 0), 0)),
                  _full((256, 256)), vec, _full((CONV_TAPS_PAD, 256)), vec, vec, vec, _full((256, 256))]
        + [_ANY] * exch.n,
        out_specs=[pl.BlockSpec((ts, 512), lambda i: (i, 0)), pl.BlockSpec((ts, 256), lambda i: (i, 0)),
                   pl.BlockSpec((ts, 256), lambda i: (i, 0))] + [_ANY] * exch.n,
        out_shape=[jax.ShapeDtypeStruct((s_len, 512), MXU_DTYPE), jax.ShapeDtypeStruct((s_len, 256), F32),
                   jax.ShapeDtypeStruct((s_len, 256), MXU_DTYPE)] + exch.out_shapes(),
        scratch_shapes=exch.scratch(),
        compiler_params=_params(1),
    )(proj, proj, wp, scale, dw, cb, lng, lnb, pw, *exch.sources)
    return res[:3], res[3:]


def _attn_fwd(proj, sinks, carried=()):
    s_len = proj.shape[0]
    qb = _attn_blocks_per_step(s_len)
    ts = qb * BLOCK
    nt = s_len // ts
    exch = _Exchange(carried)

    def body(*refs):
        (p_ref, kvp_ref, sinks_ref), (y_ref,), (bias_ref,), xrefs = _split_refs(refs, 3, 1, exch)
        i = pl.program_id(0)
        exch.run(xrefs, i == 0, i == nt - 1)

        @pl.when(i == 0)
        def _():
            _fill_attn_bias(bias_ref)

        probs = [(b, kh) for b in range(qb) for kh in range(N_KV_HEADS)]
        scores, v_bds = {}, {}
        for b in range(qb):
            r0 = BLOCK * b
            kv_prev = kvp_ref[...] if b == 0 else p_ref[r0 - BLOCK:r0, 512:768]
            kv2 = jnp.concatenate([kv_prev, p_ref[r0:r0 + BLOCK, 512:768]], axis=0)
            k2, v2 = kv2[:, :BLOCK], kv2[:, BLOCK:]
            k2_swapped, v2_swapped = pltpu.roll(k2, HEAD_DIM, 1), pltpu.roll(v2, HEAD_DIM, 1)
            variant = jnp.where(i == 0, 0, 1) if b == 0 else 1
            q = p_ref[r0:r0 + BLOCK, 0:512]
            for kh in range(N_KV_HEADS):
                k_bd = _pair_block_matrix(k2, k2_swapped, kh)
                v_bds[b, kh] = _pair_block_matrix(v2, v2_swapped, kh)
                scores[b, kh] = _mm(_pair_queries(q, kh), k_bd, _NT) + bias_ref[variant, kh]
        ps = {pr: _pair_softmax(scores[pr], pr[1], sinks_ref)[0] for pr in probs}
        outs = {pr: _mm(ps[pr], v_bds[pr]) for pr in probs}
        for b in range(qb):
            r0 = BLOCK * b
            ga = p_ref[r0:r0 + BLOCK, 768:1280]
            ya = _pair_unstack([outs[b, kh] for kh in range(N_KV_HEADS)])
            y_ref[r0:r0 + BLOCK, :] = (ya * (ga * _sig(ga))).astype(y_ref.dtype)

    res = pl.pallas_call(
        body, name="attn_fwd_carrier" if exch.n else "attn_fwd", grid=(nt,),
        in_specs=[pl.BlockSpec((ts, HALF_IN), lambda i: (i, 1)),
                  pl.BlockSpec((BLOCK, 256), lambda i: (jnp.maximum(i * qb - 1, 0), 7)),
                  pl.BlockSpec(memory_space=pltpu.SMEM)] + [_ANY] * exch.n,
        out_specs=[pl.BlockSpec((ts, 512), lambda i: (i, 0))] + [_ANY] * exch.n,
        out_shape=[jax.ShapeDtypeStruct((s_len, 512), MXU_DTYPE)] + exch.out_shapes(),
        scratch_shapes=[pltpu.VMEM((2, N_KV_HEADS, PAIR_ROWS, PAIR_COLS), F32)] + exch.scratch(),
        compiler_params=_params(1),
    )(proj, proj, sinks, *exch.sources)
    return res[0], res[1:]


def _out_in_proj(x, y_pc, y_at, w_out, g_next, w_t_next):
    s_len, d = x.shape
    ts = _row_tile(s_len, ROW_TILE)

    def body(x_ref, a_ref, b_ref, w_ref, g_ref, wn_ref, o_ref, p_ref):
        y = jnp.concatenate([a_ref[...], b_ref[...]], axis=-1)
        xv = x_ref[...] + _mm(y, w_ref[...])
        o_ref[...] = xv
        r = lax.rsqrt(jnp.mean(xv * xv, axis=-1, keepdims=True) + EPS)
        p_ref[...] = _mm(xv * r * g_ref[...], wn_ref[...], _NT)

    tile = pl.BlockSpec((ts, d), lambda i: (i, 0))
    half = pl.BlockSpec((ts, 512), lambda i: (i, 0))
    return pl.pallas_call(
        body, name="out_in_proj", grid=(s_len // ts,),
        in_specs=[tile, half, half, _full((D_MIX, d)), _full((1, d)), _full((D_IN, d))],
        out_specs=[tile, pl.BlockSpec((ts, D_IN), lambda i: (i, 0))],
        out_shape=[jax.ShapeDtypeStruct((s_len, d), F32), jax.ShapeDtypeStruct((s_len, D_IN), F32)],
        compiler_params=_params(1),
    )(x, y_pc, y_at, w_out, g_next, w_t_next)


def _out_proj_loss(x, y_pc, y_at, w_out, target, g):
    s_len, d = x.shape
    ts = _row_tile(s_len, ROW_TILE)

    def body(x_ref, a_ref, b_ref, w_ref, t_ref, g_ref, sq_ref, dg_ref, dx_ref):
        i = pl.program_id(0)
        y = jnp.concatenate([a_ref[...], b_ref[...]], axis=-1)
        xv = x_ref[...] + _mm(y, w_ref[...])
        gv = g_ref[...]
        r = lax.rsqrt(jnp.mean(xv * xv, axis=-1, keepdims=True) + EPS)
        xr = xv * r
        err = xr * gv - t_ref[...]
        dout = err * (1.0 / d)
        w = dout * gv
        dx_ref[...] = r * (w - xr * jnp.mean(w * xr, axis=-1, keepdims=True))

        @pl.when(i == 0)
        def _():
            sq_ref[...] = jnp.zeros_like(sq_ref)
            dg_ref[...] = jnp.zeros_like(dg_ref)

        sq = jnp.sum(jnp.sum(err * err, axis=-1, keepdims=True), axis=0, keepdims=True)
        sq_ref[...] += jnp.broadcast_to(sq, sq_ref.shape)
        dg_ref[...] += jnp.sum(dout * xr, axis=0, keepdims=True)

    tile = pl.BlockSpec((ts, d), lambda i: (i, 0))
    half = pl.BlockSpec((ts, 512), lambda i: (i, 0))
    return pl.pallas_call(
        body, name="out_proj_loss", grid=(s_len // ts,),
        in_specs=[tile, half, half, _full((D_MIX, d)), tile, _full((1, d))],
        out_specs=[_full((1, 128)), _full((1, d)), tile],
        out_shape=[jax.ShapeDtypeStruct((1, 128), F32), jax.ShapeDtypeStruct((1, d), F32),
                   jax.ShapeDtypeStruct((s_len, d), F32)],
        compiler_params=_params(1),
    )(x, y_pc, y_at, w_out, target, g)


def _out_bwd(dxo, y_pc, y_at, w_out):
    s_len, d = dxo.shape
    ts = _row_tile(s_len, ROW_TILE)
    nt = s_len // ts

    def body(dx_ref, a_ref, b_ref, w_ref, dy_ref, gw_ref, acc_ref):
        i = pl.program_id(0)
        dxv = dx_ref[...].astype(MXU_DTYPE)
        dy_ref[...] = _mm(dxv, w_ref[...], _NT)

        @pl.when(i == 0)
        def _():
            acc_ref[...] = jnp.zeros_like(acc_ref)

        y = jnp.concatenate([a_ref[...], b_ref[...]], axis=-1)
        acc_ref[...] += _mm(y, dxv, _TN)

        @pl.when(i == nt - 1)
        def _():
            gw_ref[...] = acc_ref[...].astype(gw_ref.dtype)

    return pl.pallas_call(
        body, name="out_bwd", grid=(nt,),
        in_specs=[pl.BlockSpec((ts, d), lambda i: (i, 0)), pl.BlockSpec((ts, 512), lambda i: (i, 0)),
                  pl.BlockSpec((ts, 512), lambda i: (i, 0)), _full((D_MIX, d))],
        out_specs=[pl.BlockSpec((ts, D_MIX), lambda i: (i, 0)), _full((D_MIX, d))],
        out_shape=[jax.ShapeDtypeStruct((s_len, D_MIX), F32), jax.ShapeDtypeStruct((D_MIX, d), EXCHANGE_DTYPE)],
        scratch_shapes=[pltpu.VMEM((D_MIX, d), F32)],
        compiler_params=_params(1),
    )(dxo, y_pc, y_at, w_out)


def _poolconv_bwd(proj, cv, diff, dy, wp, scale, dw, lng, lnb, pw, carried=()):
    s_len = proj.shape[0]
    ts = _row_tile(s_len, ROW_TILE)
    hb = ts // HALO
    nt = s_len // ts
    last_halo = s_len // HALO - 1
    n = ts + HALO
    exch = _Exchange(carried)

    def body(*refs):
        ins, outs, (up_ref, hh_ref, dhh_ref, gdw_acc_ref), xrefs = _split_refs(refs, 13, 5, exch)
        (p_ref, pn_ref, cv_ref, cvn_ref, diff_ref, dy_ref, dyn_ref, wp_ref, sc_ref, dw_ref, lng_ref, lnb_ref,
         pw_ref) = ins
        da_ref, gwp_ref, gpw_ref, gdw_ref, gvec_ref = outs
        i = pl.program_id(0)
        exch.run(xrefs, i == 0, i == nt - 1)
        has_next = i < nt - 1
        cur = p_ref[...]
        nxt = jnp.where(has_next, pn_ref[...], 0.0)
        dyx = jnp.concatenate([dy_ref[...], jnp.where(has_next, dyn_ref[...], 0.0)], axis=0)
        row = lax.broadcasted_iota(jnp.int32, (n, 1), 0)
        in_seq = (row < ts) | has_next
        scale_v = sc_ref[...]

        cvx = jnp.concatenate([cv_ref[...], jnp.where(has_next, cvn_ref[...], 0.0)], axis=0)
        nrm, rstd = _layer_norm(cvx)
        z = nrm * lng_ref[...] + lnb_ref[...]
        sz = _sig(z)
        sw = z * sz
        gc = jnp.concatenate([cur[:, 1024:1280], nxt[:, 1024:1280]], axis=0)
        sgc = _sig(gc)
        yc = _mm(sw, pw_ref[...])
        dyc = dyx[:, 256:512]
        d_yc = dyc * (gc * sgc)
        d_gc = (dyc * yc * _dsilu(gc, sgc))[:ts]
        d_z = _mm(d_yc, pw_ref[...], _NT) * _dsilu(z, sz)
        d_n = d_z * lng_ref[...]
        d_cv = rstd * (d_n - jnp.mean(d_n, axis=-1, keepdims=True)
                       - nrm * jnp.mean(d_n * nrm, axis=-1, keepdims=True))
        d_cv = jnp.where(in_seq, d_cv, 0.0)
        _store_shifted(up_ref, d_cv, up=True)
        _anticausal_conv(up_ref, dw_ref, dhh_ref, ts)
        d_hh = dhh_ref[...]
        a_c, sb_c = cur[:, 512:768], _sig(cur[:, 768:1024])
        hh_ref[...] = a_c * sb_c
        d_a = d_hh * sb_c
        d_b = d_hh * a_c * sb_c * (1.0 - sb_c)
        d_cv_t = d_cv[:ts]
        _depthwise_conv_weight_grad(hh_ref, up_ref, gdw_acc_ref, ts)

        diff = diff_ref[...]
        raw = _mm(diff, wp_ref[...])
        gp = jnp.concatenate([cur[:, 256:512], nxt[:, 256:512]], axis=0)
        sgp = _sig(gp)
        dyp = dyx[:, 0:256]
        d_yp = dyp * (gp * sgp)
        d_gp = dyp[:ts] * (raw * scale_v) * _dsilu(gp, sgp)[:ts]
        d_raw = d_yp * scale_v
        d_diff = _mm(d_raw, wp_ref[...], _NT)
        w = jnp.where(in_seq, d_diff / _pool_count(i * ts, n), 0.0)
        d_u = _pool_diff_bwd(w, ts) - d_diff[:ts]

        da_ref[...] = jnp.concatenate([d_u, d_gp, d_a, d_b, d_gc], axis=-1).astype(da_ref.dtype)

        @pl.when(i == 0)
        def _():
            gwp_ref[...] = jnp.zeros_like(gwp_ref)
            gpw_ref[...] = jnp.zeros_like(gpw_ref)
            gdw_ref[...] = jnp.zeros_like(gdw_ref)
            gvec_ref[...] = jnp.zeros_like(gvec_ref)

        gwp_ref[...] += _mm(diff, d_raw[:ts], _TN)
        gpw_ref[...] += _mm(sw[:ts], d_yc[:ts], _TN)
        gdw_ref[...] += jnp.sum(gdw_acc_ref[...], axis=1)
        zero_row = jnp.zeros((1, 256), F32)
        gvec_ref[...] += jnp.concatenate([
            jnp.sum(d_yp[:ts] * raw, axis=0, keepdims=True),
            jnp.sum(d_cv_t, axis=0, keepdims=True),
            jnp.sum((d_z * nrm)[:ts], axis=0, keepdims=True),
            jnp.sum(d_z[:ts], axis=0, keepdims=True),
            zero_row, zero_row, zero_row, zero_row], axis=0)

    vec = _full((1, 256))
    nxt_halo = lambda i: (jnp.minimum((i + 1) * hb, last_halo), 0)
    res = pl.pallas_call(
        body, name="poolconv_bwd_carrier" if exch.n else "poolconv_bwd", grid=(nt,),
        in_specs=[pl.BlockSpec((ts, HALF_IN), lambda i: (i, 0)), pl.BlockSpec((HALO, HALF_IN), nxt_halo),
                  pl.BlockSpec((ts, 256), lambda i: (i, 0)), pl.BlockSpec((HALO, 256), nxt_halo),
                  pl.BlockSpec((ts, 256), lambda i: (i, 0)),
                  pl.BlockSpec((ts, 512), lambda i: (i, 0)), pl.BlockSpec((HALO, 512), nxt_halo),
                  _full((256, 256)), vec, _full((CONV_TAPS_PAD, 256)), vec, vec, _full((256, 256))]
        + [_ANY] * exch.n,
        out_specs=[pl.BlockSpec((ts, HALF_IN), lambda i: (i, 0)), _full((256, 256)), _full((256, 256)),
                   _full((CONV_TAPS_PAD, 256)), _full((8, 256))] + [_ANY] * exch.n,
        out_shape=[jax.ShapeDtypeStruct((s_len, HALF_IN), MXU_DTYPE), jax.ShapeDtypeStruct((256, 256), F32),
                   jax.ShapeDtypeStruct((256, 256), F32), jax.ShapeDtypeStruct((CONV_TAPS_PAD, 256), F32),
                   jax.ShapeDtypeStruct((8, 256), F32)] + exch.out_shapes(),
        scratch_shapes=[pltpu.VMEM((8, n, CONV_WIDTH), F32), pltpu.VMEM((ts, CONV_WIDTH), F32),
                        pltpu.VMEM((ts, CONV_WIDTH), F32), pltpu.VMEM((CONV_TAPS_PAD, 8, CONV_WIDTH), F32)]
        + exch.scratch(),
        compiler_params=_params(1),
    )(proj, proj, cv, cv, diff, dy, dy, wp, scale, dw, lng, lnb, pw, *exch.sources)
    return res[:5], res[5:]


DQ0, DKC0, DVC0, DKP0, DVP0, DGA0, DATTN_W = 0, 512, 640, 768, 896, 1024, 1536


def _attn_bwd(proj, dy, sinks, carried=()):
    s_len = proj.shape[0]
    qb = _attn_blocks_per_step(s_len)
    ts = qb * BLOCK
    nt = s_len // ts
    exch = _Exchange(carried)

    def body(*refs):
        (p_ref, kvp_ref, dy_ref, sinks_ref), (o_ref, gs_ref), (bias_ref,), xrefs = _split_refs(refs, 4, 2, exch)
        i = pl.program_id(0)
        exch.run(xrefs, i == 0, i == nt - 1)

        @pl.when(i == 0)
        def _():
            _fill_attn_bias(bias_ref)
            gs_ref[...] = jnp.zeros_like(gs_ref)

        lo = lax.broadcasted_iota(jnp.int32, (1, 2 * HEAD_DIM), 1) < HEAD_DIM
        probs = [(b, kh) for b in range(qb) for kh in range(N_KV_HEADS)]
        kv_heads = range(N_KV_HEADS)
        scores, k_bds, v_bds, q2s, do2s, dyas, gas, sgas = {}, {}, {}, {}, {}, {}, {}, {}
        for b in range(qb):
            r0 = BLOCK * b
            kv_prev = kvp_ref[...] if b == 0 else p_ref[r0 - BLOCK:r0, 512:768]
            kv2 = jnp.concatenate([kv_prev, p_ref[r0:r0 + BLOCK, 512:768]], axis=0)
            k2, v2 = kv2[:, :BLOCK], kv2[:, BLOCK:]
            k2_swapped, v2_swapped = pltpu.roll(k2, HEAD_DIM, 1), pltpu.roll(v2, HEAD_DIM, 1)
            variant = jnp.where(i == 0, 0, 1) if b == 0 else 1
            q = p_ref[r0:r0 + BLOCK, 0:512]
            gas[b] = p_ref[r0:r0 + BLOCK, 768:1280]
            dyas[b] = dy_ref[r0:r0 + BLOCK, :]
            sgas[b] = _sig(gas[b])
            d_o = dyas[b] * (gas[b] * sgas[b])
            for kh in kv_heads:
                k_bds[b, kh] = _pair_block_matrix(k2, k2_swapped, kh)
                v_bds[b, kh] = _pair_block_matrix(v2, v2_swapped, kh)
                q2s[b, kh] = _pair_queries(q, kh)
                do2s[b, kh] = _pair_stack(d_o, kh)
                scores[b, kh] = _mm(q2s[b, kh], k_bds[b, kh], _NT) + bias_ref[variant, kh]
        ps, p_sinks, outs, dps, dss, dqs, dks, dvs = {}, {}, {}, {}, {}, {}, {}, {}
        d_sinks = [None] * N_Q_HEADS

        def softmax_stage(pr):
            ps[pr], p_sinks[pr] = _pair_softmax(scores[pr], pr[1], sinks_ref)

        def value_stage(pr):
            outs[pr] = _mm(ps[pr], v_bds[pr])
            dps[pr] = _mm(do2s[pr], v_bds[pr], _NT)

        def score_grad_stage(pr):
            p, dp, kh = ps[pr], dps[pr], pr[1]
            ds_halves = []
            for a in range(2):
                cols = slice(2 * BLOCK * a, 2 * BLOCK * (a + 1))
                delta = jnp.sum(p[:, cols] * dp[:, cols], axis=-1, keepdims=True)
                ds_halves.append(p[:, cols] * (dp[:, cols] - delta))
                dsink = -p_sinks[pr][a] * delta
                for j in range(2):
                    part = jnp.sum(dsink[BLOCK * j: BLOCK * (j + 1)], axis=0, keepdims=True)
                    h = Q_PER_KV * kh + 2 * j + a
                    d_sinks[h] = part if d_sinks[h] is None else d_sinks[h] + part
            dss[pr] = jnp.concatenate(ds_halves, axis=-1)

        def operand_grad_stage(pr):
            dqs[pr] = _mm(dss[pr], k_bds[pr]) * SCALE
            dks[pr] = _fold_pair_halves(_mm(dss[pr], q2s[pr], _TN))
            dvs[pr] = _fold_pair_halves(_mm(ps[pr], do2s[pr], _TN))

        for stage in (softmax_stage, value_stage, score_grad_stage, operand_grad_stage):
            for pr in probs:
                stage(pr)
        for b in range(qb):
            r0 = BLOCK * b
            dk = jnp.where(lo, dks[b, 0], dks[b, 1])
            dv = jnp.where(lo, dvs[b, 0], dvs[b, 1])
            d_ga = dyas[b] * _pair_unstack([outs[b, kh] for kh in kv_heads]) * _dsilu(gas[b], sgas[b])
            o_ref[r0:r0 + BLOCK, :] = jnp.concatenate(
                [_pair_unstack([dqs[b, kh] for kh in kv_heads]), dk[BLOCK:], dv[BLOCK:], dk[:BLOCK], dv[:BLOCK],
                 d_ga], axis=-1)
        gs_ref[...] += jnp.broadcast_to(jnp.concatenate(d_sinks, axis=0), gs_ref.shape)

    res = pl.pallas_call(
        body, name="attn_bwd_carrier" if exch.n else "attn_bwd", grid=(nt,),
        in_specs=[pl.BlockSpec((ts, HALF_IN), lambda i: (i, 1)),
                  pl.BlockSpec((BLOCK, 256), lambda i: (jnp.maximum(i * qb - 1, 0), 7)),
                  pl.BlockSpec((ts, 512), lambda i: (i, 1)),
                  pl.BlockSpec(memory_space=pltpu.SMEM)] + [_ANY] * exch.n,
        out_specs=[pl.BlockSpec((ts, DATTN_W), lambda i: (i, 0)), _full((N_Q_HEADS, 128))] + [_ANY] * exch.n,
        out_shape=[jax.ShapeDtypeStruct((s_len, DATTN_W), F32), jax.ShapeDtypeStruct((N_Q_HEADS, 128), F32)]
        + exch.out_shapes(),
        scratch_shapes=[pltpu.VMEM((2, N_KV_HEADS, PAIR_ROWS, PAIR_COLS), F32)] + exch.scratch(),
        compiler_params=_params(1),
    )(proj, proj, dy, sinks, *exch.sources)
    return res[:2], res[2:]


def _in_bwd(da, dattn, x, dxo, g, w_t):
    s_len, d = x.shape
    ts = _row_tile(s_len, IN_BWD_TILE)
    bpt = ts // BLOCK
    nt = s_len // ts
    last_block = s_len // BLOCK - 1

    def body(da_ref, dat_ref, nxt_ref, x_ref, dxo_ref, g_ref, w_ref, dx_ref, dg_ref, gw_ref, acc_ref, stage_ref,
             stage_sem):
        i = pl.program_id(0)
        dat = dat_ref[...]
        nxt = jnp.where(i < nt - 1, nxt_ref[...], 0.0)
        shifted = jnp.concatenate([dat[BLOCK:, DKP0:DGA0], nxt], axis=0) if bpt > 1 else nxt
        dkv = dat[:, DKC0:DKP0] + shifted
        dproj = jnp.concatenate([da_ref[...], dat[:, DQ0:DKC0].astype(MXU_DTYPE), dkv.astype(MXU_DTYPE),
                                 dat[:, DGA0:DATTN_W].astype(MXU_DTYPE)], axis=-1)
        d_h = _mm(dproj, w_ref[...])
        xv = x_ref[...]
        gv = g_ref[...]
        r = lax.rsqrt(jnp.mean(xv * xv, axis=-1, keepdims=True) + EPS)
        xr = xv * r
        w = d_h * gv
        dx_ref[...] = dxo_ref[...] + r * (w - xr * jnp.mean(w * xr, axis=-1, keepdims=True))

        @pl.when(i == 0)
        def _():
            dg_ref[...] = jnp.zeros_like(dg_ref)
            acc_ref[...] = jnp.zeros_like(acc_ref)

        dg_ref[...] += jnp.sum(d_h * xr, axis=0, keepdims=True)
        acc_ref[...] += _mm(dproj, xr * gv, _TN)

        @pl.when(i == nt - 1)
        def _():
            stage_ref[...] = acc_ref[...].astype(stage_ref.dtype)
            out = pltpu.make_async_copy(stage_ref, gw_ref, stage_sem)
            out.start()
            out.wait()

    return pl.pallas_call(
        body, name="in_bwd", grid=(nt,),
        in_specs=[pl.BlockSpec((ts, HALF_IN), lambda i: (i, 0)),
                  pl.BlockSpec((ts, DATTN_W), lambda i: (i, 0)),
                  pl.BlockSpec((BLOCK, 256), lambda i: (jnp.minimum((i + 1) * bpt, last_block), 3)),
                  pl.BlockSpec((ts, d), lambda i: (i, 0)), pl.BlockSpec((ts, d), lambda i: (i, 0)),
                  _full((1, d)), pl.BlockSpec((D_IN, d), lambda i: (0, 0), pipeline_mode=pl.Buffered(1))],
        out_specs=[pl.BlockSpec((ts, d), lambda i: (i, 0)), _full((1, d)), _ANY],
        out_shape=[jax.ShapeDtypeStruct((s_len, d), F32), jax.ShapeDtypeStruct((1, d), F32),
                   jax.ShapeDtypeStruct((D_IN, d), EXCHANGE_DTYPE)],
        scratch_shapes=[pltpu.VMEM((D_IN, d), F32), pltpu.VMEM((D_IN, d), EXCHANGE_DTYPE), pltpu.SemaphoreType.DMA],
        compiler_params=_params(1),
    )(da, dattn, dattn, x, dxo, g, w_t)


def _in_bwd_dw(da, dattn, x, g):
    s_len, d = x.shape
    ts = _row_tile(s_len, IN_BWD_TILE)
    bpt = ts // BLOCK
    nt = s_len // ts
    last_block = s_len // BLOCK - 1

    def body(da_ref, dat_ref, nxt_ref, x_ref, g_ref, dp_ref, gw_ref, acc_ref, stage_ref, stage_sem):
        i = pl.program_id(0)
        dat = dat_ref[...]
        nxt = jnp.where(i < nt - 1, nxt_ref[...], 0.0)
        shifted = jnp.concatenate([dat[BLOCK:, DKP0:DGA0], nxt], axis=0) if bpt > 1 else nxt
        dkv = dat[:, DKC0:DKP0] + shifted
        dproj = jnp.concatenate([da_ref[...], dat[:, DQ0:DKC0].astype(MXU_DTYPE), dkv.astype(MXU_DTYPE),
                                 dat[:, DGA0:DATTN_W].astype(MXU_DTYPE)], axis=-1)
        dp_ref[...] = dproj
        xv = x_ref[...]
        r = lax.rsqrt(jnp.mean(xv * xv, axis=-1, keepdims=True) + EPS)

        @pl.when(i == 0)
        def _():
            acc_ref[...] = jnp.zeros_like(acc_ref)

        acc_ref[...] += _mm(dproj, xv * r * g_ref[...], _TN)

        @pl.when(i == nt - 1)
        def _():
            stage_ref[...] = acc_ref[...].astype(stage_ref.dtype)
            out = pltpu.make_async_copy(stage_ref, gw_ref, stage_sem)
            out.start()
            out.wait()

    return pl.pallas_call(
        body, name="in_bwd_dw", grid=(nt,),
        in_specs=[pl.BlockSpec((ts, HALF_IN), lambda i: (i, 0)),
                  pl.BlockSpec((ts, DATTN_W), lambda i: (i, 0)),
                  pl.BlockSpec((BLOCK, 256), lambda i: (jnp.minimum((i + 1) * bpt, last_block), 3)),
                  pl.BlockSpec((ts, d), lambda i: (i, 0)), _full((1, d))],
        out_specs=[pl.BlockSpec((ts, D_IN), lambda i: (i, 0)), _ANY],
        out_shape=[jax.ShapeDtypeStruct((s_len, D_IN), MXU_DTYPE), jax.ShapeDtypeStruct((D_IN, d), EXCHANGE_DTYPE)],
        scratch_shapes=[pltpu.VMEM((D_IN, d), F32), pltpu.VMEM((D_IN, d), EXCHANGE_DTYPE), pltpu.SemaphoreType.DMA],
        compiler_params=_params(1),
    )(da, dattn, dattn, x, g)


def _in_bwd_dx(dproj, x, dxo, g, w_t, carried=()):
    s_len, d = x.shape
    ts = _row_tile(s_len, ROW_TILE)
    nt = s_len // ts
    exch = _Exchange(carried)

    def body(*refs):
        (dp_ref, x_ref, dxo_ref, g_ref, w_ref), (dx_ref, dg_ref), _, xrefs = _split_refs(refs, 5, 2, exch)
        i = pl.program_id(0)
        exch.run(xrefs, i == 0, i == nt - 1)
        d_h = _mm(dp_ref[...], w_ref[...])
        xv = x_ref[...]
        r = lax.rsqrt(jnp.mean(xv * xv, axis=-1, keepdims=True) + EPS)
        xr = xv * r
        w = d_h * g_ref[...]
        dx_ref[...] = dxo_ref[...] + r * (w - xr * jnp.mean(w * xr, axis=-1, keepdims=True))

        @pl.when(i == 0)
        def _():
            dg_ref[...] = jnp.zeros_like(dg_ref)

        dg_ref[...] += jnp.sum(d_h * xr, axis=0, keepdims=True)

    tile = pl.BlockSpec((ts, d), lambda i: (i, 0))
    res = pl.pallas_call(
        body, name="in_bwd_dx_carrier" if exch.n else "in_bwd_dx", grid=(nt,),
        in_specs=[pl.BlockSpec((ts, D_IN), lambda i: (i, 0)), tile, tile, _full((1, d)), _full((D_IN, d))]
        + [_ANY] * exch.n,
        out_specs=[tile, _full((1, d))] + [_ANY] * exch.n,
        out_shape=[jax.ShapeDtypeStruct((s_len, d), F32), jax.ShapeDtypeStruct((1, d), F32)] + exch.out_shapes(),
        scratch_shapes=exch.scratch(),
        compiler_params=_params(1),
    )(dproj, x, dxo, g, w_t, *exch.sources)
    return res[:2], res[2:]


def _sum_partials(p_ref):
    g = p_ref[0].astype(F32)
    for k in range(1, N_DEV):
        g = g + p_ref[k].astype(F32)
    return g


def _adamw_step(g, w, m, v):
    nm = ADAM_B1 * m + (1.0 - ADAM_B1) * g
    nv = ADAM_B2 * v + (1.0 - ADAM_B2) * (g * g)
    m_hat = nm / (1.0 - ADAM_B1 ** ADAM_STEP)
    v_hat = nv / (1.0 - ADAM_B2 ** ADAM_STEP)
    return -ADAM_LR * (m_hat / (jnp.sqrt(v_hat) + ADAM_EPS) + ADAM_WD * w), nm, nv


def _adamw_layers(parts, w, m, v, name):
    _, rows, n = w.shape
    tr = _row_tile(rows, 64)

    def body(*refs):
        p_refs = refs[:DEPTH]
        w_ref, m_ref, v_ref, g_ref, d_ref, nm_ref, nv_ref = refs[DEPTH:]
        for l in range(DEPTH):
            g = _sum_partials(p_refs[l])
            g_ref[l] = g
            d_ref[l], nm_ref[l], nv_ref[l] = _adamw_step(g, w_ref[l], m_ref[l], v_ref[l])

    tile = pl.BlockSpec((DEPTH, tr, n), lambda i: (0, i, 0))
    shape = jax.ShapeDtypeStruct(w.shape, F32)
    return pl.pallas_call(
        body, name=name, grid=(rows // tr,),
        in_specs=[pl.BlockSpec((N_DEV, tr, n), lambda i: (0, i, 0))] * DEPTH + [tile] * 3,
        out_specs=[tile] * 4, out_shape=[shape] * 4,
        compiler_params=_params(1),
    )(*parts, w, m, v)


def _adamw(parts, w, m, v, name):
    rows, n = w.shape
    tr = _row_tile(rows, 256)

    def body(p_ref, w_ref, m_ref, v_ref, g_ref, d_ref, nm_ref, nv_ref):
        g = _sum_partials(p_ref)
        g_ref[...] = g
        d_ref[...], nm_ref[...], nv_ref[...] = _adamw_step(g, w_ref[...], m_ref[...], v_ref[...])

    tile = pl.BlockSpec((tr, n), lambda i: (i, 0))
    shape = jax.ShapeDtypeStruct((rows, n), F32)
    return pl.pallas_call(
        body, name=name, grid=(rows // tr,),
        in_specs=[pl.BlockSpec((N_DEV, tr, n), lambda i: (0, i, 0)), tile, tile, tile],
        out_specs=[tile, tile, tile, tile],
        out_shape=[shape, shape, shape, shape],
        compiler_params=_params(1),
    )(parts, w, m, v)


def _pad_rows(a, mult):
    pad = (-a.shape[0]) % mult
    return a if pad == 0 else jnp.concatenate([a, jnp.zeros((pad, a.shape[1]), a.dtype)], axis=0)


def _dw_rows(conv_dw_l):
    return jnp.pad(jnp.swapaxes(conv_dw_l, 0, 1), ((0, 0), (0, CONV_TAPS_PAD - CONV_KERNEL)))


def _pack_small(pw_l, dw_l, d):
    rows = jnp.concatenate([pw_l.reshape(-1, d), _dw_rows(dw_l).reshape(-1, d)], axis=0)
    return _pad_rows(rows, 8)


def _small_slabs(g_pw, g_dw, d):
    a = g_pw.reshape(N_DEV, -1, d)
    b = jnp.swapaxes(g_dw, 0, 1).reshape(N_DEV, -1, d)
    used = a.shape[1] + b.shape[1]
    return jnp.concatenate([a, b, jnp.zeros((N_DEV, (-used) % 8, d), g_pw.dtype)], axis=1)


def _unpack_small(rows, d):
    c = CONV_WIDTH // N_DEV
    n_pw = c * CONV_WIDTH // d
    n_dw = c * CONV_TAPS_PAD // d
    pw = rows[:n_pw].reshape(c, CONV_WIDTH)
    dw = jnp.swapaxes(rows[n_pw:n_pw + n_dw].reshape(c, CONV_TAPS_PAD), 0, 1)[:CONV_KERNEL]
    return pw, dw


def _pack_replicated(ln_g, pool_w, pool_scale, conv_b, conv_ln_g, conv_ln_b, attn_sinks, final_g, scalar, d):
    sinks = jnp.pad(attn_sinks, ((0, 0), (0, 256 - N_Q_HEADS)))
    small = jnp.concatenate([pool_scale, conv_b, conv_ln_g, conv_ln_b, sinks], axis=0)
    small = _pad_rows(small, d // 256)
    last = jnp.pad(scalar.reshape(1, 1), ((0, 0), (0, d - 1)))
    return _pad_rows(jnp.concatenate([ln_g.reshape(-1, d), final_g.reshape(-1, d), pool_w.reshape(-1, d),
                                      small.reshape(-1, d), last], axis=0), 8)


def _unpack_replicated(rows, d):
    n_pool = DEPTH * 4 * POOL_GROUP * POOL_GROUP // d
    n_small = -(-5 * DEPTH * 256 // d)
    ln_g = rows[:DEPTH]
    final_g = rows[DEPTH]
    pool_w = rows[DEPTH + 1: DEPTH + 1 + n_pool].reshape(DEPTH, 4, POOL_GROUP, POOL_GROUP)
    small = rows[DEPTH + 1 + n_pool: DEPTH + 1 + n_pool + n_small].reshape(-1, 256)[: 5 * DEPTH]
    pool_scale, conv_b, conv_ln_g, conv_ln_b = (small[DEPTH * k: DEPTH * (k + 1)] for k in range(4))
    sinks = small[4 * DEPTH: 5 * DEPTH, :N_Q_HEADS]
    scalar = rows[DEPTH + 1 + n_pool + n_small, 0]
    return ln_g, pool_w, pool_scale, conv_b, conv_ln_g, conv_ln_b, sinks, final_g, scalar


def _block_diag(pool_w):
    out = jnp.zeros((POOL_WIDTH, POOL_WIDTH), pool_w.dtype)
    for gi in range(4):
        out = out.at[POOL_GROUP * gi: POOL_GROUP * (gi + 1), POOL_GROUP * gi: POOL_GROUP * (gi + 1)].set(pool_w[gi])
    return out


def _diag_blocks(mat):
    return jnp.stack([mat[POOL_GROUP * gi: POOL_GROUP * (gi + 1), POOL_GROUP * gi: POOL_GROUP * (gi + 1)]
                      for gi in range(4)], axis=0)


def kernel(x, ln_g, w_in, pool_w, pool_scale, conv_dw, conv_b, conv_ln_g, conv_ln_b, conv_pw, attn_sinks, w_out, final_g, loss_target, m_ln_g, m_w_in, m_pool_w, m_pool_scale, m_conv_dw, m_conv_b, m_conv_ln_g, m_conv_ln_b, m_conv_pw, m_attn_sinks, m_w_out, m_final_g, v_ln_g, v_w_in, v_pool_w, v_pool_scale, v_conv_dw, v_conv_b, v_conv_ln_g, v_conv_ln_b, v_conv_pw, v_attn_sinks, v_w_out, v_final_g):
    x0 = x[0]
    d = x0.shape[1]
    row = lambda a: a.reshape(1, -1)
    slabs = lambda a: a.reshape(N_DEV, a.shape[0] // N_DEV, d)
    c_shard = CONV_WIDTH // N_DEV

    w_in_rows = [jnp.swapaxes(w_in[l], 0, 1).astype(MXU_DTYPE) for l in range(DEPTH)]
    w_out_rows = [w_out[l].astype(MXU_DTYPE) for l in range(DEPTH)]
    per_word = 4 // jnp.dtype(MXU_DTYPE).itemsize
    dw_t = jnp.stack([_dw_rows(conv_dw[l]) for l in range(DEPTH)], axis=0)
    dw_bits = (lax.bitcast_convert_type(dw_t, MXU_DTYPE) if per_word > 1 else dw_t).reshape(-1, d)
    n_pw = DEPTH * c_shard * CONV_WIDTH // d
    conv_rows = _pad_rows(jnp.concatenate([conv_pw.reshape(-1, d).astype(MXU_DTYPE), dw_bits], axis=0), 16)
    w_in_t, w_out_f = [None] * DEPTH, [None] * DEPTH
    w_in_t[0] = _all_gather(w_in_rows[0], "w_in_all_gather")
    wp_bd = [_block_diag(pool_w[l]).astype(MXU_DTYPE) for l in range(DEPTH)]

    xs, projs, cvs, ys = [x0], [], [], []
    q4 = D_IN // N_DEV // 4
    w_in_1 = [w_in_rows[1][q4 * k: q4 * (k + 1)] for k in range(4)]
    for l in range(DEPTH):
        if l == 0:
            proj, got = _in_proj(xs[0], row(ln_g[0]), w_in_t[0], carried=[w_out_rows[0], conv_rows, w_in_1[0]])
            w_in_1_got = [got[2]]
            w_out_f[0] = got[0].reshape(D_MIX, d)
            pw_all = got[1][:, :n_pw].reshape(N_DEV, DEPTH, c_shard, CONV_WIDTH)
            pw_f = [pw_all[:, k].reshape(CONV_WIDTH, CONV_WIDTH) for k in range(DEPTH)]
            bits = got[1][:, n_pw:n_pw + dw_bits.shape[0]].reshape(
                (N_DEV, DEPTH, c_shard, CONV_TAPS_PAD) + (per_word,) * (per_word > 1))
            dw_all = lax.bitcast_convert_type(bits, F32) if per_word > 1 else bits
            dw_f = [jnp.swapaxes(dw_all[:, k].reshape(CONV_WIDTH, CONV_TAPS_PAD), 0, 1) for k in range(DEPTH)]
        (y_pc, cv, diff), got = _poolconv_fwd(proj, wp_bd[l], row(pool_scale[l]), dw_f[l], row(conv_b[l]),
                                        row(conv_ln_g[l]), row(conv_ln_b[l]), pw_f[l],
                                        carried=[w_in_1[1], w_in_1[2]] if l == 0 else [])
        if l == 0:
            w_in_1_got += list(got)
        y_at, got = _attn_fwd(proj, attn_sinks[l], carried=[w_out_rows[1], w_in_1[3]] if l == 0 else [])
        if l == 0:
            w_out_f[1] = got[0].reshape(D_MIX, d)
            w_in_t[1] = jnp.concatenate(w_in_1_got + [got[1]], axis=1).reshape(D_IN, d)
        projs.append(proj)
        cvs.append((cv, diff))
        ys.append((y_pc, y_at))
        if l < DEPTH - 1:
            x_next, proj = _out_in_proj(xs[l], y_pc, y_at, w_out_f[l], row(ln_g[l + 1]), w_in_t[l + 1])
            xs.append(x_next)

    l = DEPTH - 1
    sq, g_final, dx = _out_proj_loss(xs[l], ys[l][0], ys[l][1], w_out_f[l], loss_target[0], row(final_g))

    l = 1
    dy, g_wout1 = _out_bwd(dx, ys[l][0], ys[l][1], w_out_f[l])
    (da, g_wp1, g_pw1, g_dw1, g_vec1), _ = _poolconv_bwd(
        projs[l], *cvs[l], dy, wp_bd[l], row(pool_scale[l]), dw_f[l], row(conv_ln_g[l]), row(conv_ln_b[l]), pw_f[l])
    (dattn, gs1), _ = _attn_bwd(projs[l], dy, attn_sinks[l])
    dx, g_ln1, g_win_t1 = _in_bwd(da, dattn, xs[l], dx, row(ln_g[l]), w_in_t[l])
    l = 0
    dy, g_wout0 = _out_bwd(dx, ys[l][0], ys[l][1], w_out_f[l])
    (da, g_wp0, g_pw0, g_dw0, g_vec0), (r_win1, r_wout1, r_small1) = _poolconv_bwd(
        projs[l], *cvs[l], dy, wp_bd[l], row(pool_scale[l]), dw_f[l], row(conv_ln_g[l]), row(conv_ln_b[l]), pw_f[l],
        carried=[slabs(g_win_t1), slabs(g_wout1), _small_slabs(g_pw1, g_dw1, d)])
    (dattn, gs0), (r_wout0, r_small0) = _attn_bwd(projs[l], dy, attn_sinks[l],
                                                  carried=[slabs(g_wout0), _small_slabs(g_pw0, g_dw0, d)])
    dproj, g_win_t0 = _in_bwd_dw(da, dattn, xs[l], row(ln_g[l]))
    (dx, g_ln0), (r_win0,) = _in_bwd_dx(dproj, xs[l], dx, row(ln_g[l]), w_in_t[l], carried=[slabs(g_win_t0)])
    grad_x = dx[None]

    gv = jnp.stack([g_vec0, g_vec1], axis=0)
    rep_part = _pack_replicated(
        jnp.concatenate([g_ln0, g_ln1], axis=0), jnp.stack([_diag_blocks(g_wp0), _diag_blocks(g_wp1)], axis=0),
        gv[:, 0], gv[:, 1], gv[:, 2], gv[:, 3], jnp.stack([gs0[:, 0], gs1[:, 0]], axis=0), g_final, sq[0, 0], d)
    (r_rep,) = _final_exchange([rep_part])

    t = lambda a: jnp.swapaxes(a, 1, 2)
    win = [t(o) for o in _adamw_layers([r_win0, r_win1], t(w_in), t(m_w_in), t(v_w_in), "adamw_w_in")]
    wout = _adamw_layers([r_wout0, r_wout1], w_out, m_w_out, v_w_out, "adamw_w_out")
    pack_s = lambda pw_, dw_: jnp.stack([_pack_small(pw_[l], dw_[l], d) for l in range(DEPTH)], axis=0)
    small = _adamw_layers([r_small0, r_small1], pack_s(conv_pw, conv_dw), pack_s(m_conv_pw, m_conv_dw),
                          pack_s(v_conv_pw, v_conv_dw), "adamw_conv")
    small = [[_unpack_small(o[l], d) for l in range(DEPTH)] for o in small]
    zero = jnp.zeros((), F32)
    pack_r = lambda *a: _pack_replicated(*a, zero, d)
    rep = [_unpack_replicated(o, d) for o in _adamw(
        r_rep, pack_r(ln_g, pool_w, pool_scale, conv_b, conv_ln_g, conv_ln_b, attn_sinks, final_g),
        pack_r(m_ln_g, m_pool_w, m_pool_scale, m_conv_b, m_conv_ln_g, m_conv_ln_b, m_attn_sinks, m_final_g),
        pack_r(v_ln_g, v_pool_w, v_pool_scale, v_conv_b, v_conv_ln_g, v_conv_ln_b, v_attn_sinks, v_final_g),
        "adamw_replicated")]
    loss = 0.5 / d * rep[0][-1]

    outs = []
    for k in range(4):
        r_ln, r_pool, r_scale, r_cb, r_lng, r_lnb, r_sinks, r_final, _ = rep[k]
        s_pw = jnp.stack([small[k][l][0] for l in range(DEPTH)], axis=0)
        s_dw = jnp.stack([small[k][l][1] for l in range(DEPTH)], axis=0)
        outs += [r_ln, win[k], r_pool, r_scale, s_dw, r_cb, r_lng, r_lnb, s_pw, r_sinks, wout[k], r_final]
    return (loss, grad_x, *outs)
```

```python
import jax
import jax.numpy as jnp
from jax import lax
from jax.experimental import pallas as pl
from jax.experimental.pallas import tpu as pltpu

F32 = jnp.float32
MXU_DTYPE = jnp.bfloat16
EXCHANGE_DTYPE = jnp.bfloat16

N_DEV = 8
DEPTH = 2
POOL_WIDTH = 256
POOL_GROUP = 64
CONV_WIDTH = 256
CONV_KERNEL = 31
CONV_TAPS_PAD = 32
HEAD_DIM = 64
N_KV_HEADS = 2
Q_PER_KV = 4
N_Q_HEADS = 8
ATTN_WIDTH = 512
BLOCK = 128
D_MIX = 1024
D_IN = 2560
HALF_IN = 1280
EPS = 1e-6
SCALE = HEAD_DIM ** -0.5
NEG = -1e30

ADAM_LR = 0.001
ADAM_B1 = 0.9
ADAM_B2 = 0.999
ADAM_EPS = 1e-08
ADAM_WD = 0.01
ADAM_STEP = 10

HALO = 32
ROW_TILE = 512
IN_BWD_TILE = 512
VMEM_LIMIT = 56 * 1024 * 1024

_NN = (((1,), (0,)), ((), ()))
_NT = (((1,), (1,)), ((), ()))
_TN = (((0,), (0,)), ((), ()))
_ANY = pl.BlockSpec(memory_space=pl.ANY)


def _mm(a, b, dims=_NN):
    return lax.dot_general(a.astype(MXU_DTYPE), b.astype(MXU_DTYPE), dims, preferred_element_type=F32)


def _sig(x):
    return 1.0 / (1.0 + jnp.exp(-x))


def _dsilu(z, s):
    return s * (1.0 + z * (1.0 - s))


def _params(n_grid):
    return pltpu.CompilerParams(dimension_semantics=("arbitrary",) * n_grid, vmem_limit_bytes=VMEM_LIMIT)


def _row_tile(rows, cap):
    t = min(rows, cap)
    while rows % t or t % 8:
        t -= 8
    return t


def _full(shape):
    return pl.BlockSpec(shape, lambda i: (0,) * len(shape))


def _layer_vec(stacked, layer):
    arr = stacked.reshape(stacked.shape[0], 1, stacked.shape[-1])
    return arr, pl.BlockSpec((None, 1, arr.shape[-1]), lambda i: (layer, 0, 0))


def _mesh_pos():
    return lax.axis_index("x"), lax.axis_index("y"), lax.axis_index("c")


class _Exchange:
    def __init__(self, sources):
        self.sources = list(sources)
        self.n = len(self.sources)
        self.gather = [s.ndim == 2 for s in self.sources]

    def out_shapes(self):
        return [jax.ShapeDtypeStruct((N_DEV,) + s.shape[-2:], s.dtype) for s in self.sources]

    def scratch(self):
        if not self.n:
            return []
        return [pltpu.SemaphoreType.DMA((7 * self.n,)), pltpu.SemaphoreType.DMA((7 * self.n,)),
                pltpu.SemaphoreType.DMA((self.n,))]

    def copies(self, src_refs, dst_refs, sems):
        send_sems, recv_sems, local_sems = sems
        x, y, c = _mesh_pos()
        me = 4 * x + 2 * y + c
        out = []
        for a, (src, dst) in enumerate(zip(src_refs, dst_refs)):
            out.append(pltpu.make_async_copy(src if self.gather[a] else src.at[me], dst.at[me], local_sems.at[a]))
            for k in range(1, N_DEV):
                tx, ty, tc = x ^ ((k >> 2) & 1), y ^ ((k >> 1) & 1), c ^ (k & 1)
                out.append(pltpu.make_async_remote_copy(
                    src_ref=src if self.gather[a] else src.at[4 * tx + 2 * ty + tc], dst_ref=dst.at[me],
                    send_sem=send_sems.at[7 * a + k - 1], recv_sem=recv_sems.at[7 * a + k - 1],
                    device_id=(tx, ty, tc), device_id_type=pl.DeviceIdType.MESH))
        return out

    def run(self, refs, first, last):
        if not self.n:
            return
        src_refs, dst_refs, sems = refs

        @pl.when(first)
        def _():
            for cp in self.copies(src_refs, dst_refs, sems):
                cp.start()

        @pl.when(last)
        def _():
            for cp in self.copies(src_refs, dst_refs, sems):
                cp.wait()


def _split_refs(refs, n_in, n_out, exch):
    ins = refs[:n_in]
    srcs = refs[n_in:n_in + exch.n]
    outs = refs[n_in + exch.n:n_in + exch.n + n_out]
    dsts = refs[n_in + exch.n + n_out:n_in + 2 * exch.n + n_out]
    rest = refs[n_in + 2 * exch.n + n_out:]
    sems = rest[len(rest) - 3:] if exch.n else ()
    scratch = rest[:len(rest) - 3] if exch.n else rest
    return ins, outs, scratch, (srcs, dsts, sems)


def _final_exchange(sources):
    exch = _Exchange(sources)

    def body(*refs):
        _, _, _, xrefs = _split_refs(refs, 0, 0, exch)
        for cp in exch.copies(*xrefs):
            cp.start()
        for cp in exch.copies(*xrefs):
            cp.wait()

    return pl.pallas_call(
        body, name="final_exchange", out_shape=exch.out_shapes(),
        in_specs=[_ANY] * exch.n, out_specs=[_ANY] * exch.n, scratch_shapes=exch.scratch(),
    )(*exch.sources)


def _all_gather(shard, name):
    m_per, n = shard.shape

    def body(x_ref, out_ref, send_sems, recv_sems, local_sem):
        x, y, c = _mesh_pos()
        me, sibling = (x, y, c), (x, y, 1 - c)
        chips = [(1 - x, y), (x, 1 - y), (1 - x, 1 - y)]

        def rows(px, py, pc):
            return out_ref.at[pl.ds((4 * px + 2 * py + pc) * m_per, m_per), :]

        def copy(k, block, to, src=None):
            return pltpu.make_async_remote_copy(
                src_ref=rows(*block) if src is None else src, dst_ref=rows(*block),
                send_sem=send_sems.at[k], recv_sem=recv_sems.at[k],
                device_id=to, device_id_type=pl.DeviceIdType.MESH)

        mine = pltpu.make_async_copy(x_ref, rows(*me), local_sem)
        mine.start()
        first = [copy(0, me, sibling, src=x_ref)]
        first += [copy(1 + j, me, (*chip, c), src=x_ref) for j, chip in enumerate(chips)]
        for cp in first:
            cp.start()
        passed = [copy(4 + j, (*chip, c), sibling) for j, chip in enumerate(chips)]
        for j, chip in enumerate(chips):
            copy(1 + j, (*chip, c), me).wait_recv()
            passed[j].start()
        copy(0, sibling, me).wait_recv()
        for j, chip in enumerate(chips):
            copy(4 + j, (*chip, 1 - c), me).wait_recv()
        for cp in first + passed:
            cp.wait_send()
        mine.wait()

    return pl.pallas_call(
        body, name=name,
        out_shape=jax.ShapeDtypeStruct((N_DEV * m_per, n), shard.dtype),
        in_specs=[pl.BlockSpec(memory_space=pltpu.VMEM)],
        out_specs=pl.BlockSpec(memory_space=pltpu.VMEM),
        scratch_shapes=[pltpu.SemaphoreType.DMA((7,)), pltpu.SemaphoreType.DMA((7,)), pltpu.SemaphoreType.DMA],
        compiler_params=pltpu.CompilerParams(vmem_limit_bytes=VMEM_LIMIT),
    )(shard)


def _by_group(lane, v2, v4, v8, v16):
    return jnp.where(lane < 64, v2, jnp.where(lane < 128, v4, jnp.where(lane < 192, v8, v16)))


def _pool_count(t0, n):
    lane = lax.broadcasted_iota(jnp.int32, (1, POOL_WIDTH), 1)
    t = (t0 + lax.broadcasted_iota(jnp.int32, (n, 1), 0)).astype(F32)
    wnd = _by_group(lane, 2.0, 4.0, 8.0, 16.0)
    return jnp.minimum(t + 1.0, wnd)


def _pool_diff(u_ext, t0, ts):
    lane = lax.broadcasted_iota(jnp.int32, (1, POOL_WIDTH), 1)
    s2 = u_ext + pltpu.roll(u_ext, 1, 0)
    s4 = s2 + pltpu.roll(s2, 2, 0)
    s8 = s4 + pltpu.roll(s4, 4, 0)
    s16 = s8 + pltpu.roll(s8, 8, 0)
    pooled = _by_group(lane, s2, s4, s8, s16)[HALO:]
    return pooled / _pool_count(t0, ts) - u_ext[HALO:]


def _pool_diff_bwd(w, ts):
    n = w.shape[0]
    lane = lax.broadcasted_iota(jnp.int32, (1, POOL_WIDTH), 1)
    f2 = w + pltpu.roll(w, n - 1, 0)
    f4 = f2 + pltpu.roll(f2, n - 2, 0)
    f8 = f4 + pltpu.roll(f4, n - 4, 0)
    f16 = f8 + pltpu.roll(f8, n - 8, 0)
    return _by_group(lane, f2, f4, f8, f16)[:ts]


CONV_CHUNK = 64


def _conv_taps():
    return [(8 * m + r, r, m) for r in range(8) for m in range(4) if 8 * m + r < CONV_KERNEL]


def _store_shifted(dst_ref, x, up):
    n = x.shape[0]
    for r in range(8):
        dst_ref[r] = x if r == 0 else pltpu.roll(x, n - r if up else r, 0)


def _anticausal_conv(src_ref, dw_ref, out_ref, n_out):
    def chunk(c, carry):
        t0 = pl.multiple_of(c * CONV_CHUNK, CONV_CHUNK)
        acc = None
        for d, r, m in _conv_taps():
            term = dw_ref[pl.ds(CONV_KERNEL - 1 - d, 1), :] * src_ref[r, pl.ds(t0 + 8 * m, CONV_CHUNK), :]
            acc = term if acc is None else acc + term
        out_ref[pl.ds(t0, CONV_CHUNK), :] = acc
        return carry

    lax.fori_loop(0, n_out // CONV_CHUNK, chunk, 0)


def _depthwise_conv_weight_grad(x_ref, dout_up_ref, acc_ref, n_rows):
    acc_ref[...] = jnp.zeros_like(acc_ref)

    def chunk(c, carry):
        t0 = pl.multiple_of(c * CONV_CHUNK, CONV_CHUNK)
        xv = x_ref[pl.ds(t0, CONV_CHUNK), :]
        for d, r, m in _conv_taps():
            prod = xv * dout_up_ref[r, pl.ds(t0 + 8 * m, CONV_CHUNK), :]
            acc_ref[CONV_KERNEL - 1 - d] += jnp.sum(prod.reshape(CONV_CHUNK // 8, 8, prod.shape[-1]), axis=0)
        return carry

    lax.fori_loop(0, n_rows // CONV_CHUNK, chunk, 0)


def _layer_norm(cv):
    mu = jnp.mean(cv, axis=-1, keepdims=True)
    xc = cv - mu
    var = jnp.mean(xc * xc, axis=-1, keepdims=True)
    rstd = lax.rsqrt(var + EPS)
    return xc * rstd, rstd


PAIR_ROWS = 2 * BLOCK
PAIR_COLS = 4 * BLOCK
ATTN_BLOCKS_PER_STEP = 4


def _attn_blocks_per_step(s_len):
    qb = ATTN_BLOCKS_PER_STEP
    while (s_len // BLOCK) % qb:
        qb //= 2
    return qb


def _pair_rows(v0, v1):
    r = lax.broadcasted_iota(jnp.int32, (PAIR_ROWS, 1), 0)
    return jnp.where(r < BLOCK, v0, v1)


def _fill_attn_bias(bias_ref):
    rows = lax.broadcasted_iota(jnp.int32, (PAIR_ROWS, PAIR_COLS), 0)
    cols = lax.broadcasted_iota(jnp.int32, (PAIR_ROWS, PAIR_COLS), 1)
    key = cols & (2 * BLOCK - 1)
    dist = BLOCK + (rows & (BLOCK - 1)) - key
    in_band = (dist >= 0) & (dist < BLOCK)
    distf = dist.astype(F32)
    second = cols >= 2 * BLOCK
    for kh in range(N_KV_HEADS):
        slope_of = lambda j, a: 2.0 ** -(Q_PER_KV * kh + 2 * j + a + 1)
        slope = jnp.where(rows < BLOCK, jnp.where(second, slope_of(0, 1), slope_of(0, 0)),
                          jnp.where(second, slope_of(1, 1), slope_of(1, 0)))
        bias = -slope * distf
        bias_ref[0, kh] = jnp.where(in_band & (key >= BLOCK), bias, NEG)
        bias_ref[1, kh] = jnp.where(in_band, bias, NEG)


def _pair_block_matrix(x, x_swapped, kh):
    lo = lax.broadcasted_iota(jnp.int32, (1, 2 * HEAD_DIM), 1) < HEAD_DIM
    in_lo, in_hi = (x, x_swapped) if kh == 0 else (x_swapped, x)
    return jnp.concatenate([jnp.where(lo, in_lo, 0.0), jnp.where(lo, 0.0, in_hi)], axis=0).astype(MXU_DTYPE)


def _pair_queries(q, kh):
    return (_pair_stack(q, kh) * SCALE).astype(MXU_DTYPE)


def _pair_stack(a, kh):
    return jnp.concatenate([a[:, 2 * BLOCK * kh: 2 * BLOCK * kh + BLOCK],
                            a[:, 2 * BLOCK * kh + BLOCK: 2 * BLOCK * (kh + 1)]], axis=0)


def _pair_unstack(parts):
    return jnp.concatenate([p[BLOCK * j: BLOCK * (j + 1)] for p in parts for j in range(2)], axis=-1)


def _pair_softmax(s, kh, sinks_ref):
    ps, p_sinks = [], []
    for a in range(2):
        sa = s[:, 2 * BLOCK * a: 2 * BLOCK * (a + 1)]
        sink = _pair_rows(sinks_ref[Q_PER_KV * kh + a], sinks_ref[Q_PER_KV * kh + 2 + a])
        m = jnp.maximum(jnp.max(sa, axis=-1, keepdims=True), sink)
        e = jnp.exp(sa - m)
        es = jnp.exp(sink - m)
        inv = 1.0 / (jnp.sum(e, axis=-1, keepdims=True) + es)
        ps.append(e * inv)
        p_sinks.append(es * inv)
    return jnp.concatenate(ps, axis=-1), p_sinks


def _fold_pair_halves(t):
    lo = lax.broadcasted_iota(jnp.int32, (1, 2 * HEAD_DIM), 1) < HEAD_DIM
    u = jnp.where(lo, t[:2 * BLOCK], t[2 * BLOCK:])
    return u + pltpu.roll(u, HEAD_DIM, 1)


def _in_proj(x, g, w_t, carried=()):
    s_len, d = x.shape
    ts = _row_tile(s_len, ROW_TILE)
    nt = s_len // ts
    exch = _Exchange(carried)

    def body(*refs):
        (x_ref, g_ref, w_ref), (o_ref,), _, xrefs = _split_refs(refs, 3, 1, exch)
        i = pl.program_id(0)
        exch.run(xrefs, i == 0, i == nt - 1)
        xv = x_ref[...]
        r = lax.rsqrt(jnp.mean(xv * xv, axis=-1, keepdims=True) + EPS)
        o_ref[...] = _mm(xv * r * g_ref[...], w_ref[...], _NT)

    res = pl.pallas_call(
        body, name="in_proj_carrier" if exch.n else "in_proj", grid=(nt,),
        in_specs=[pl.BlockSpec((ts, d), lambda i: (i, 0)), g[1], _full((D_IN, d))] + [_ANY] * exch.n,
        out_specs=[pl.BlockSpec((ts, D_IN), lambda i: (i, 0))] + [_ANY] * exch.n,
        out_shape=[jax.ShapeDtypeStruct((s_len, D_IN), F32)] + exch.out_shapes(),
        scratch_shapes=exch.scratch(),
        compiler_params=_params(1),
    )(x, g[0], w_t, *exch.sources)
    return res[0], res[1:]


def _poolconv_fwd(proj, wp, scale, dw, cb, lng, lnb, pw, carried=()):
    s_len = proj.shape[0]
    ts = _row_tile(s_len, ROW_TILE)
    hb = ts // HALO
    nt = s_len // ts
    exch = _Exchange(carried)

    def body(*refs):
        ins, (y_ref, cv_ref, diff_ref), _, xrefs = _split_refs(refs, 9, 3, exch)
        p_ref, ph_ref, wp_ref, sc_ref, dw_ref, cb_ref, lng_ref, lnb_ref, pw_ref = ins
        i = pl.program_id(0)
        exch.run(xrefs, i == 0, i == nt - 1)
        halo = jnp.where(i > 0, ph_ref[...], 0.0)
        y_mix, cv, diff = _pool_conv_mixers(p_ref[...], halo, i * ts, ts, wp_ref, sc_ref, dw_ref, cb_ref, lng_ref,
                                            lnb_ref, pw_ref)
        y_ref[...] = y_mix.astype(y_ref.dtype)
        cv_ref[...] = cv
        diff_ref[...] = diff

    res = pl.pallas_call(
        body, name="poolconv_fwd_carrier" if exch.n else "poolconv_fwd", grid=(nt,),
        in_specs=[pl.BlockSpec((ts, HALF_IN), lambda i: (i, 0)),
                  pl.BlockSpec((HALO, HALF_IN), lambda i: (jnp.maximum(i * hb - 1, 0), 0)),
                  _full((256, 256)), scale[1], _full((CONV_TAPS_PAD, 256)), cb[1], lng[1], lnb[1],
                  _full((256, 256))]
        + [_ANY] * exch.n,
        out_specs=[pl.BlockSpec((ts, 512), lambda i: (i, 0)), pl.BlockSpec((ts, 256), lambda i: (i, 0)),
                   pl.BlockSpec((ts, 256), lambda i: (i, 0))] + [_ANY] * exch.n,
        out_shape=[jax.ShapeDtypeStruct((s_len, 512), MXU_DTYPE), jax.ShapeDtypeStruct((s_len, 256), F32),
                   jax.ShapeDtypeStruct((s_len, 256), MXU_DTYPE)] + exch.out_shapes(),
        scratch_shapes=exch.scratch(),
        compiler_params=_params(1),
    )(proj, proj, wp, scale[0], dw, cb[0], lng[0], lnb[0], pw, *exch.sources)
    return res[:3], res[3:]


def _pool_conv_mixers(cur, halo, t0, ts, wp_ref, sc_ref, dw_ref, cb_ref, lng_ref, lnb_ref, pw_ref):
    ext = jnp.concatenate([halo, cur], axis=0)
    diff = _pool_diff(ext[:, 0:256], t0, ts).astype(MXU_DTYPE)
    gp = cur[:, 256:512]
    y_pool = _mm(diff, wp_ref[...]) * sc_ref[...] * (gp * _sig(gp))
    hh = ext[:, 512:768] * _sig(ext[:, 768:1024])
    shifted = [hh if r == 0 else pltpu.roll(hh, r, 0) for r in range(8)]
    cv = cb_ref[...]
    for dist, r, m in _conv_taps():
        cv = cv + dw_ref[pl.ds(CONV_KERNEL - 1 - dist, 1), :] * shifted[r][HALO - 8 * m: HALO - 8 * m + ts]
    n, _ = _layer_norm(cv)
    z = n * lng_ref[...] + lnb_ref[...]
    gc = cur[:, 1024:1280]
    y_conv = _mm(z * _sig(z), pw_ref[...]) * (gc * _sig(gc))
    return jnp.concatenate([y_pool, y_conv], axis=-1), cv, diff


def _attn_fwd(proj, sinks, carried=()):
    s_len = proj.shape[0]
    qb = _attn_blocks_per_step(s_len)
    ts = qb * BLOCK
    nt = s_len // ts
    exch = _Exchange(carried)

    def body(*refs):
        (p_ref, kvp_ref, sinks_ref), (y_ref,), (bias_ref,), xrefs = _split_refs(refs, 3, 1, exch)
        i = pl.program_id(0)
        exch.run(xrefs, i == 0, i == nt - 1)

        @pl.when(i == 0)
        def _():
            _fill_attn_bias(bias_ref)

        probs = [(b, kh) for b in range(qb) for kh in range(N_KV_HEADS)]
        scores, v_bds = {}, {}
        for b in range(qb):
            r0 = BLOCK * b
            kv_prev = kvp_ref[...] if b == 0 else p_ref[r0 - BLOCK:r0, 512:768]
            kv2 = jnp.concatenate([kv_prev, p_ref[r0:r0 + BLOCK, 512:768]], axis=0)
            k2, v2 = kv2[:, :BLOCK], kv2[:, BLOCK:]
            k2_swapped, v2_swapped = pltpu.roll(k2, HEAD_DIM, 1), pltpu.roll(v2, HEAD_DIM, 1)
            variant = jnp.where(i == 0, 0, 1) if b == 0 else 1
            q = p_ref[r0:r0 + BLOCK, 0:512]
            for kh in range(N_KV_HEADS):
                k_bd = _pair_block_matrix(k2, k2_swapped, kh)
                v_bds[b, kh] = _pair_block_matrix(v2, v2_swapped, kh)
                scores[b, kh] = _mm(_pair_queries(q, kh), k_bd, _NT) + bias_ref[variant, kh]
        ps = {pr: _pair_softmax(scores[pr], pr[1], sinks_ref)[0] for pr in probs}
        outs = {pr: _mm(ps[pr], v_bds[pr]) for pr in probs}
        for b in range(qb):
            r0 = BLOCK * b
            ga = p_ref[r0:r0 + BLOCK, 768:1280]
            ya = _pair_unstack([outs[b, kh] for kh in range(N_KV_HEADS)])
            y_ref[r0:r0 + BLOCK, :] = (ya * (ga * _sig(ga))).astype(y_ref.dtype)

    res = pl.pallas_call(
        body, name="attn_fwd_carrier" if exch.n else "attn_fwd", grid=(nt,),
        in_specs=[pl.BlockSpec((ts, HALF_IN), lambda i: (i, 1)),
                  pl.BlockSpec((BLOCK, 256), lambda i: (jnp.maximum(i * qb - 1, 0), 7)),
                  pl.BlockSpec(memory_space=pltpu.SMEM)] + [_ANY] * exch.n,
        out_specs=[pl.BlockSpec((ts, 512), lambda i: (i, 0))] + [_ANY] * exch.n,
        out_shape=[jax.ShapeDtypeStruct((s_len, 512), MXU_DTYPE)] + exch.out_shapes(),
        scratch_shapes=[pltpu.VMEM((2, N_KV_HEADS, PAIR_ROWS, PAIR_COLS), F32)] + exch.scratch(),
        compiler_params=_params(1),
    )(proj, proj, sinks, *exch.sources)
    return res[0], res[1:]


def _out_in_proj(x, y_pc, y_at, w_out, g_next, w_t_next):
    s_len, d = x.shape
    ts = _row_tile(s_len, ROW_TILE)

    def body(x_ref, a_ref, b_ref, w_ref, g_ref, wn_ref, o_ref, p_ref):
        y = jnp.concatenate([a_ref[...], b_ref[...]], axis=-1)
        xv = x_ref[...] + _mm(y, w_ref[...])
        o_ref[...] = xv
        r = lax.rsqrt(jnp.mean(xv * xv, axis=-1, keepdims=True) + EPS)
        p_ref[...] = _mm(xv * r * g_ref[...], wn_ref[...], _NT)

    tile = pl.BlockSpec((ts, d), lambda i: (i, 0))
    half = pl.BlockSpec((ts, 512), lambda i: (i, 0))
    return pl.pallas_call(
        body, name="out_in_proj", grid=(s_len // ts,),
        in_specs=[tile, half, half, _full((D_MIX, d)), g_next[1], _full((D_IN, d))],
        out_specs=[tile, pl.BlockSpec((ts, D_IN), lambda i: (i, 0))],
        out_shape=[jax.ShapeDtypeStruct((s_len, d), F32), jax.ShapeDtypeStruct((s_len, D_IN), F32)],
        compiler_params=_params(1),
    )(x, y_pc, y_at, w_out, g_next[0], w_t_next)


def _out_proj_loss(x, y_pc, y_at, w_out, target, g):
    s_len, d = x.shape
    ts = _row_tile(s_len, ROW_TILE)

    def body(x_ref, a_ref, b_ref, w_ref, t_ref, g_ref, sq_ref, dg_ref, dx_ref):
        i = pl.program_id(0)
        y = jnp.concatenate([a_ref[...], b_ref[...]], axis=-1)
        xv = x_ref[...] + _mm(y, w_ref[...])
        gv = g_ref[...]
        r = lax.rsqrt(jnp.mean(xv * xv, axis=-1, keepdims=True) + EPS)
        xr = xv * r
        err = xr * gv - t_ref[...]
        dout = err * (1.0 / d)
        w = dout * gv
        dx_ref[...] = r * (w - xr * jnp.mean(w * xr, axis=-1, keepdims=True))

        @pl.when(i == 0)
        def _():
            sq_ref[...] = jnp.zeros_like(sq_ref)
            dg_ref[...] = jnp.zeros_like(dg_ref)

        sq = jnp.sum(jnp.sum(err * err, axis=-1, keepdims=True), axis=0, keepdims=True)
        sq_ref[...] += jnp.broadcast_to(sq, sq_ref.shape)
        dg_ref[...] += jnp.sum(dout * xr, axis=0, keepdims=True)

    tile = pl.BlockSpec((ts, d), lambda i: (i, 0))
    half = pl.BlockSpec((ts, 512), lambda i: (i, 0))
    return pl.pallas_call(
        body, name="out_proj_loss", grid=(s_len // ts,),
        in_specs=[tile, half, half, _full((D_MIX, d)), tile, _full((1, d))],
        out_specs=[_full((1, 128)), _full((1, d)), tile],
        out_shape=[jax.ShapeDtypeStruct((1, 128), F32), jax.ShapeDtypeStruct((1, d), F32),
                   jax.ShapeDtypeStruct((s_len, d), F32)],
        compiler_params=_params(1),
    )(x, y_pc, y_at, w_out, target, g)


def _out_bwd(dxo, y_pc, y_at, w_out):
    s_len, d = dxo.shape
    ts = _row_tile(s_len, ROW_TILE)
    nt = s_len // ts

    def body(dx_ref, a_ref, b_ref, w_ref, dy_ref, gw_ref, acc_ref):
        i = pl.program_id(0)
        dxv = dx_ref[...].astype(MXU_DTYPE)
        dy_ref[...] = _mm(dxv, w_ref[...], _NT)

        @pl.when(i == 0)
        def _():
            acc_ref[...] = jnp.zeros_like(acc_ref)

        y = jnp.concatenate([a_ref[...], b_ref[...]], axis=-1)
        acc_ref[...] += _mm(y, dxv, _TN)

        @pl.when(i == nt - 1)
        def _():
            gw_ref[...] = acc_ref[...].astype(gw_ref.dtype)

    return pl.pallas_call(
        body, name="out_bwd", grid=(nt,),
        in_specs=[pl.BlockSpec((ts, d), lambda i: (i, 0)), pl.BlockSpec((ts, 512), lambda i: (i, 0)),
                  pl.BlockSpec((ts, 512), lambda i: (i, 0)), _full((D_MIX, d))],
        out_specs=[pl.BlockSpec((ts, D_MIX), lambda i: (i, 0)), _full((D_MIX, d))],
        out_shape=[jax.ShapeDtypeStruct((s_len, D_MIX), F32), jax.ShapeDtypeStruct((D_MIX, d), EXCHANGE_DTYPE)],
        scratch_shapes=[pltpu.VMEM((D_MIX, d), F32)],
        compiler_params=_params(1),
    )(dxo, y_pc, y_at, w_out)


def _poolconv_bwd(proj, cv, diff, dy, wp, scale, dw, lng, lnb, pw, carried=()):
    s_len = proj.shape[0]
    ts = _row_tile(s_len, ROW_TILE)
    hb = ts // HALO
    nt = s_len // ts
    last_halo = s_len // HALO - 1
    n = ts + HALO
    exch = _Exchange(carried)

    def body(*refs):
        ins, outs, (up_ref, hh_ref, dhh_ref, gdw_acc_ref), xrefs = _split_refs(refs, 13, 5, exch)
        (p_ref, pn_ref, cv_ref, cvn_ref, diff_ref, dy_ref, dyn_ref, wp_ref, sc_ref, dw_ref, lng_ref, lnb_ref,
         pw_ref) = ins
        da_ref, gwp_ref, gpw_ref, gdw_ref, gvec_ref = outs
        i = pl.program_id(0)
        exch.run(xrefs, i == 0, i == nt - 1)
        has_next = i < nt - 1
        cur = p_ref[...]
        nxt = jnp.where(has_next, pn_ref[...], 0.0)
        dyx = jnp.concatenate([dy_ref[...], jnp.where(has_next, dyn_ref[...], 0.0)], axis=0)
        row = lax.broadcasted_iota(jnp.int32, (n, 1), 0)
        in_seq = (row < ts) | has_next
        scale_v = sc_ref[...]

        cvx = jnp.concatenate([cv_ref[...], jnp.where(has_next, cvn_ref[...], 0.0)], axis=0)
        nrm, rstd = _layer_norm(cvx)
        z = nrm * lng_ref[...] + lnb_ref[...]
        sz = _sig(z)
        sw = z * sz
        gc = jnp.concatenate([cur[:, 1024:1280], nxt[:, 1024:1280]], axis=0)
        sgc = _sig(gc)
        yc = _mm(sw, pw_ref[...])
        dyc = dyx[:, 256:512]
        d_yc = dyc * (gc * sgc)
        d_gc = (dyc * yc * _dsilu(gc, sgc))[:ts]
        d_z = _mm(d_yc, pw_ref[...], _NT) * _dsilu(z, sz)
        d_n = d_z * lng_ref[...]
        d_cv = rstd * (d_n - jnp.mean(d_n, axis=-1, keepdims=True)
                       - nrm * jnp.mean(d_n * nrm, axis=-1, keepdims=True))
        d_cv = jnp.where(in_seq, d_cv, 0.0)
        _store_shifted(up_ref, d_cv, up=True)
        _anticausal_conv(up_ref, dw_ref, dhh_ref, ts)
        d_hh = dhh_ref[...]
        a_c, sb_c = cur[:, 512:768], _sig(cur[:, 768:1024])
        hh_ref[...] = a_c * sb_c
        d_a = d_hh * sb_c
        d_b = d_hh * a_c * sb_c * (1.0 - sb_c)
        d_cv_t = d_cv[:ts]
        _depthwise_conv_weight_grad(hh_ref, up_ref, gdw_acc_ref, ts)

        diff = diff_ref[...]
        raw = _mm(diff, wp_ref[...])
        gp = jnp.concatenate([cur[:, 256:512], nxt[:, 256:512]], axis=0)
        sgp = _sig(gp)
        dyp = dyx[:, 0:256]
        d_yp = dyp * (gp * sgp)
        d_gp = dyp[:ts] * (raw * scale_v) * _dsilu(gp, sgp)[:ts]
        d_raw = d_yp * scale_v
        d_diff = _mm(d_raw, wp_ref[...], _NT)
        w = jnp.where(in_seq, d_diff / _pool_count(i * ts, n), 0.0)
        d_u = _pool_diff_bwd(w, ts) - d_diff[:ts]

        da_ref[...] = jnp.concatenate([d_u, d_gp, d_a, d_b, d_gc], axis=-1).astype(da_ref.dtype)

        @pl.when(i == 0)
        def _():
            gwp_ref[...] = jnp.zeros_like(gwp_ref)
            gpw_ref[...] = jnp.zeros_like(gpw_ref)
            gdw_ref[...] = jnp.zeros_like(gdw_ref)
            gvec_ref[...] = jnp.zeros_like(gvec_ref)

        gwp_ref[...] += _mm(diff, d_raw[:ts], _TN)
        gpw_ref[...] += _mm(sw[:ts], d_yc[:ts], _TN)
        gdw_ref[...] += jnp.sum(gdw_acc_ref[...], axis=1)
        zero_row = jnp.zeros((1, 256), F32)
        gvec_ref[...] += jnp.concatenate([
            jnp.sum(d_yp[:ts] * raw, axis=0, keepdims=True),
            jnp.sum(d_cv_t, axis=0, keepdims=True),
            jnp.sum((d_z * nrm)[:ts], axis=0, keepdims=True),
            jnp.sum(d_z[:ts], axis=0, keepdims=True),
            zero_row, zero_row, zero_row, zero_row], axis=0)

    nxt_halo = lambda i: (jnp.minimum((i + 1) * hb, last_halo), 0)
    res = pl.pallas_call(
        body, name="poolconv_bwd_carrier" if exch.n else "poolconv_bwd", grid=(nt,),
        in_specs=[pl.BlockSpec((ts, HALF_IN), lambda i: (i, 0)), pl.BlockSpec((HALO, HALF_IN), nxt_halo),
                  pl.BlockSpec((ts, 256), lambda i: (i, 0)), pl.BlockSpec((HALO, 256), nxt_halo),
                  pl.BlockSpec((ts, 256), lambda i: (i, 0)),
                  pl.BlockSpec((ts, 512), lambda i: (i, 0)), pl.BlockSpec((HALO, 512), nxt_halo),
                  _full((256, 256)), scale[1], _full((CONV_TAPS_PAD, 256)), lng[1], lnb[1], _full((256, 256))]
        + [_ANY] * exch.n,
        out_specs=[pl.BlockSpec((ts, HALF_IN), lambda i: (i, 0)), _full((256, 256)), _full((256, 256)),
                   _full((CONV_TAPS_PAD, 256)), _full((8, 256))] + [_ANY] * exch.n,
        out_shape=[jax.ShapeDtypeStruct((s_len, HALF_IN), MXU_DTYPE), jax.ShapeDtypeStruct((256, 256), F32),
                   jax.ShapeDtypeStruct((256, 256), F32), jax.ShapeDtypeStruct((CONV_TAPS_PAD, 256), F32),
                   jax.ShapeDtypeStruct((8, 256), F32)] + exch.out_shapes(),
        scratch_shapes=[pltpu.VMEM((8, n, CONV_WIDTH), F32), pltpu.VMEM((ts, CONV_WIDTH), F32),
                        pltpu.VMEM((ts, CONV_WIDTH), F32), pltpu.VMEM((CONV_TAPS_PAD, 8, CONV_WIDTH), F32)]
        + exch.scratch(),
        compiler_params=_params(1),
    )(proj, proj, cv, cv, diff, dy, dy, wp, scale[0], dw, lng[0], lnb[0], pw, *exch.sources)
    return res[:5], res[5:]


DQ0, DKC0, DVC0, DKP0, DVP0, DGA0, DATTN_W = 0, 512, 640, 768, 896, 1024, 1536


def _attn_bwd(proj, dy, sinks, carried=()):
    s_len = proj.shape[0]
    qb = _attn_blocks_per_step(s_len)
    ts = qb * BLOCK
    nt = s_len // ts
    exch = _Exchange(carried)

    def body(*refs):
        (p_ref, kvp_ref, dy_ref, sinks_ref), (o_ref, gs_ref), (bias_ref,), xrefs = _split_refs(refs, 4, 2, exch)
        i = pl.program_id(0)
        exch.run(xrefs, i == 0, i == nt - 1)

        @pl.when(i == 0)
        def _():
            _fill_attn_bias(bias_ref)
            gs_ref[...] = jnp.zeros_like(gs_ref)

        lo = lax.broadcasted_iota(jnp.int32, (1, 2 * HEAD_DIM), 1) < HEAD_DIM
        probs = [(b, kh) for b in range(qb) for kh in range(N_KV_HEADS)]
        kv_heads = range(N_KV_HEADS)
        scores, k_bds, v_bds, q2s, do2s, dyas, gas, sgas = {}, {}, {}, {}, {}, {}, {}, {}
        for b in range(qb):
            r0 = BLOCK * b
            kv_prev = kvp_ref[...] if b == 0 else p_ref[r0 - BLOCK:r0, 512:768]
            kv2 = jnp.concatenate([kv_prev, p_ref[r0:r0 + BLOCK, 512:768]], axis=0)
            k2, v2 = kv2[:, :BLOCK], kv2[:, BLOCK:]
            k2_swapped, v2_swapped = pltpu.roll(k2, HEAD_DIM, 1), pltpu.roll(v2, HEAD_DIM, 1)
            variant = jnp.where(i == 0, 0, 1) if b == 0 else 1
            q = p_ref[r0:r0 + BLOCK, 0:512]
            gas[b] = p_ref[r0:r0 + BLOCK, 768:1280]
            dyas[b] = dy_ref[r0:r0 + BLOCK, :]
            sgas[b] = _sig(gas[b])
            d_o = dyas[b] * (gas[b] * sgas[b])
            for kh in kv_heads:
                k_bds[b, kh] = _pair_block_matrix(k2, k2_swapped, kh)
                v_bds[b, kh] = _pair_block_matrix(v2, v2_swapped, kh)
                q2s[b, kh] = _pair_queries(q, kh)
                do2s[b, kh] = _pair_stack(d_o, kh)
                scores[b, kh] = _mm(q2s[b, kh], k_bds[b, kh], _NT) + bias_ref[variant, kh]
        ps, p_sinks, outs, dps, dss, dqs, dks, dvs = {}, {}, {}, {}, {}, {}, {}, {}
        d_sinks = [None] * N_Q_HEADS

        def softmax_stage(pr):
            ps[pr], p_sinks[pr] = _pair_softmax(scores[pr], pr[1], sinks_ref)

        def value_stage(pr):
            outs[pr] = _mm(ps[pr], v_bds[pr])
            dps[pr] = _mm(do2s[pr], v_bds[pr], _NT)

        def score_grad_stage(pr):
            p, dp, kh = ps[pr], dps[pr], pr[1]
            ds_halves = []
            for a in range(2):
                cols = slice(2 * BLOCK * a, 2 * BLOCK * (a + 1))
                delta = jnp.sum(p[:, cols] * dp[:, cols], axis=-1, keepdims=True)
                ds_halves.append(p[:, cols] * (dp[:, cols] - delta))
                dsink = -p_sinks[pr][a] * delta
                for j in range(2):
                    part = jnp.sum(dsink[BLOCK * j: BLOCK * (j + 1)], axis=0, keepdims=True)
                    h = Q_PER_KV * kh + 2 * j + a
                    d_sinks[h] = part if d_sinks[h] is None else d_sinks[h] + part
            dss[pr] = jnp.concatenate(ds_halves, axis=-1)

        def operand_grad_stage(pr):
            dqs[pr] = _mm(dss[pr], k_bds[pr]) * SCALE
            dks[pr] = _fold_pair_halves(_mm(dss[pr], q2s[pr], _TN))
            dvs[pr] = _fold_pair_halves(_mm(ps[pr], do2s[pr], _TN))

        for stage in (softmax_stage, value_stage, score_grad_stage, operand_grad_stage):
            for pr in probs:
                stage(pr)
        for b in range(qb):
            r0 = BLOCK * b
            dk = jnp.where(lo, dks[b, 0], dks[b, 1])
            dv = jnp.where(lo, dvs[b, 0], dvs[b, 1])
            d_ga = dyas[b] * _pair_unstack([outs[b, kh] for kh in kv_heads]) * _dsilu(gas[b], sgas[b])
            o_ref[r0:r0 + BLOCK, :] = jnp.concatenate(
                [_pair_unstack([dqs[b, kh] for kh in kv_heads]), dk[BLOCK:], dv[BLOCK:], dk[:BLOCK], dv[:BLOCK],
                 d_ga], axis=-1)
        gs_ref[...] += jnp.broadcast_to(jnp.concatenate(d_sinks, axis=0), gs_ref.shape)

    res = pl.pallas_call(
        body, name="attn_bwd_carrier" if exch.n else "attn_bwd", grid=(nt,),
        in_specs=[pl.BlockSpec((ts, HALF_IN), lambda i: (i, 1)),
                  pl.BlockSpec((BLOCK, 256), lambda i: (jnp.maximum(i * qb - 1, 0), 7)),
                  pl.BlockSpec((ts, 512), lambda i: (i, 1)),
                  pl.BlockSpec(memory_space=pltpu.SMEM)] + [_ANY] * exch.n,
        out_specs=[pl.BlockSpec((ts, DATTN_W), lambda i: (i, 0)), _full((N_Q_HEADS, 128))] + [_ANY] * exch.n,
        out_shape=[jax.ShapeDtypeStruct((s_len, DATTN_W), F32), jax.ShapeDtypeStruct((N_Q_HEADS, 128), F32)]
        + exch.out_shapes(),
        scratch_shapes=[pltpu.VMEM((2, N_KV_HEADS, PAIR_ROWS, PAIR_COLS), F32)] + exch.scratch(),
        compiler_params=_params(1),
    )(proj, proj, dy, sinks, *exch.sources)
    return res[:2], res[2:]


def _in_bwd(da, dattn, x, dxo, g, w_t):
    s_len, d = x.shape
    ts = _row_tile(s_len, IN_BWD_TILE)
    bpt = ts // BLOCK
    nt = s_len // ts
    last_block = s_len // BLOCK - 1

    def body(da_ref, dat_ref, nxt_ref, x_ref, dxo_ref, g_ref, w_ref, dx_ref, dg_ref, gw_ref, acc_ref, stage_ref,
             stage_sem):
        i = pl.program_id(0)
        dat = dat_ref[...]
        nxt = jnp.where(i < nt - 1, nxt_ref[...], 0.0)
        shifted = jnp.concatenate([dat[BLOCK:, DKP0:DGA0], nxt], axis=0) if bpt > 1 else nxt
        dkv = dat[:, DKC0:DKP0] + shifted
        dproj = jnp.concatenate([da_ref[...], dat[:, DQ0:DKC0].astype(MXU_DTYPE), dkv.astype(MXU_DTYPE),
                                 dat[:, DGA0:DATTN_W].astype(MXU_DTYPE)], axis=-1)
        d_h = _mm(dproj, w_ref[...])
        xv = x_ref[...]
        gv = g_ref[...]
        r = lax.rsqrt(jnp.mean(xv * xv, axis=-1, keepdims=True) + EPS)
        xr = xv * r
        w = d_h * gv
        dx_ref[...] = dxo_ref[...] + r * (w - xr * jnp.mean(w * xr, axis=-1, keepdims=True))

        @pl.when(i == 0)
        def _():
            dg_ref[...] = jnp.zeros_like(dg_ref)
            acc_ref[...] = jnp.zeros_like(acc_ref)

        dg_ref[...] += jnp.sum(d_h * xr, axis=0, keepdims=True)
        acc_ref[...] += _mm(dproj, xr * gv, _TN)

        @pl.when(i == nt - 1)
        def _():
            stage_ref[...] = acc_ref[...].astype(stage_ref.dtype)
            out = pltpu.make_async_copy(stage_ref, gw_ref, stage_sem)
            out.start()
            out.wait()

    return pl.pallas_call(
        body, name="in_bwd", grid=(nt,),
        in_specs=[pl.BlockSpec((ts, HALF_IN), lambda i: (i, 0)),
                  pl.BlockSpec((ts, DATTN_W), lambda i: (i, 0)),
                  pl.BlockSpec((BLOCK, 256), lambda i: (jnp.minimum((i + 1) * bpt, last_block), 3)),
                  pl.BlockSpec((ts, d), lambda i: (i, 0)), pl.BlockSpec((ts, d), lambda i: (i, 0)),
                  g[1], pl.BlockSpec((D_IN, d), lambda i: (0, 0), pipeline_mode=pl.Buffered(1))],
        out_specs=[pl.BlockSpec((ts, d), lambda i: (i, 0)), _full((1, d)), _ANY],
        out_shape=[jax.ShapeDtypeStruct((s_len, d), F32), jax.ShapeDtypeStruct((1, d), F32),
                   jax.ShapeDtypeStruct((D_IN, d), EXCHANGE_DTYPE)],
        scratch_shapes=[pltpu.VMEM((D_IN, d), F32), pltpu.VMEM((D_IN, d), EXCHANGE_DTYPE), pltpu.SemaphoreType.DMA],
        compiler_params=_params(1),
    )(da, dattn, dattn, x, dxo, g[0], w_t)


def _in_bwd_dw(da, dattn, x, g):
    s_len, d = x.shape
    ts = _row_tile(s_len, IN_BWD_TILE)
    bpt = ts // BLOCK
    nt = s_len // ts
    last_block = s_len // BLOCK - 1

    def body(da_ref, dat_ref, nxt_ref, x_ref, g_ref, dp_ref, gw_ref, acc_ref, stage_ref, stage_sem):
        i = pl.program_id(0)
        dat = dat_ref[...]
        nxt = jnp.where(i < nt - 1, nxt_ref[...], 0.0)
        shifted = jnp.concatenate([dat[BLOCK:, DKP0:DGA0], nxt], axis=0) if bpt > 1 else nxt
        dkv = dat[:, DKC0:DKP0] + shifted
        dproj = jnp.concatenate([da_ref[...], dat[:, DQ0:DKC0].astype(MXU_DTYPE), dkv.astype(MXU_DTYPE),
                                 dat[:, DGA0:DATTN_W].astype(MXU_DTYPE)], axis=-1)
        dp_ref[...] = dproj
        xv = x_ref[...]
        r = lax.rsqrt(jnp.mean(xv * xv, axis=-1, keepdims=True) + EPS)

        @pl.when(i == 0)
        def _():
            acc_ref[...] = jnp.zeros_like(acc_ref)

        acc_ref[...] += _mm(dproj, xv * r * g_ref[...], _TN)

        @pl.when(i == nt - 1)
        def _():
            stage_ref[...] = acc_ref[...].astype(stage_ref.dtype)
            out = pltpu.make_async_copy(stage_ref, gw_ref, stage_sem)
            out.start()
            out.wait()

    return pl.pallas_call(
        body, name="in_bwd_dw", grid=(nt,),
        in_specs=[pl.BlockSpec((ts, HALF_IN), lambda i: (i, 0)),
                  pl.BlockSpec((ts, DATTN_W), lambda i: (i, 0)),
                  pl.BlockSpec((BLOCK, 256), lambda i: (jnp.minimum((i + 1) * bpt, last_block), 3)),
                  pl.BlockSpec((ts, d), lambda i: (i, 0)), g[1]],
        out_specs=[pl.BlockSpec((ts, D_IN), lambda i: (i, 0)), _ANY],
        out_shape=[jax.ShapeDtypeStruct((s_len, D_IN), MXU_DTYPE), jax.ShapeDtypeStruct((D_IN, d), EXCHANGE_DTYPE)],
        scratch_shapes=[pltpu.VMEM((D_IN, d), F32), pltpu.VMEM((D_IN, d), EXCHANGE_DTYPE), pltpu.SemaphoreType.DMA],
        compiler_params=_params(1),
    )(da, dattn, dattn, x, g[0])


def _in_bwd_dx(dproj, x, dxo, g, w_t, carried=()):
    s_len, d = x.shape
    ts = _row_tile(s_len, ROW_TILE)
    nt = s_len // ts
    exch = _Exchange(carried)

    def body(*refs):
        (dp_ref, x_ref, dxo_ref, g_ref, w_ref), (dx_ref, dg_ref), _, xrefs = _split_refs(refs, 5, 2, exch)
        i = pl.program_id(0)
        exch.run(xrefs, i == 0, i == nt - 1)
        d_h = _mm(dp_ref[...], w_ref[...])
        xv = x_ref[...]
        r = lax.rsqrt(jnp.mean(xv * xv, axis=-1, keepdims=True) + EPS)
        xr = xv * r
        w = d_h * g_ref[...]
        dx_ref[...] = dxo_ref[...] + r * (w - xr * jnp.mean(w * xr, axis=-1, keepdims=True))

        @pl.when(i == 0)
        def _():
            dg_ref[...] = jnp.zeros_like(dg_ref)

        dg_ref[...] += jnp.sum(d_h * xr, axis=0, keepdims=True)

    tile = pl.BlockSpec((ts, d), lambda i: (i, 0))
    res = pl.pallas_call(
        body, name="in_bwd_dx_carrier" if exch.n else "in_bwd_dx", grid=(nt,),
        in_specs=[pl.BlockSpec((ts, D_IN), lambda i: (i, 0)), tile, tile, g[1], _full((D_IN, d))]
        + [_ANY] * exch.n,
        out_specs=[tile, _full((1, d))] + [_ANY] * exch.n,
        out_shape=[jax.ShapeDtypeStruct((s_len, d), F32), jax.ShapeDtypeStruct((1, d), F32)] + exch.out_shapes(),
        scratch_shapes=exch.scratch(),
        compiler_params=_params(1),
    )(dproj, x, dxo, g[0], w_t, *exch.sources)
    return res[:2], res[2:]


def _sum_partials(p_ref):
    g = p_ref[0].astype(F32)
    for k in range(1, N_DEV):
        g = g + p_ref[k].astype(F32)
    return g


def _adamw_step(g, w, m, v):
    nm = ADAM_B1 * m + (1.0 - ADAM_B1) * g
    nv = ADAM_B2 * v + (1.0 - ADAM_B2) * (g * g)
    m_hat = nm / (1.0 - ADAM_B1 ** ADAM_STEP)
    v_hat = nv / (1.0 - ADAM_B2 ** ADAM_STEP)
    return -ADAM_LR * (m_hat / (jnp.sqrt(v_hat) + ADAM_EPS) + ADAM_WD * w), nm, nv


def _adamw_layers(parts, w, m, v, name):
    _, rows, n = w.shape
    tr = _row_tile(rows, 64)

    def body(*refs):
        p_refs = refs[:DEPTH]
        w_ref, m_ref, v_ref, g_ref, d_ref, nm_ref, nv_ref = refs[DEPTH:]
        for l in range(DEPTH):
            g = _sum_partials(p_refs[l])
            g_ref[l] = g
            d_ref[l], nm_ref[l], nv_ref[l] = _adamw_step(g, w_ref[l], m_ref[l], v_ref[l])

    tile = pl.BlockSpec((DEPTH, tr, n), lambda i: (0, i, 0))
    shape = jax.ShapeDtypeStruct(w.shape, F32)
    return pl.pallas_call(
        body, name=name, grid=(rows // tr,),
        in_specs=[pl.BlockSpec((N_DEV, tr, n), lambda i: (0, i, 0))] * DEPTH + [tile] * 3,
        out_specs=[tile] * 4, out_shape=[shape] * 4,
        compiler_params=_params(1),
    )(*parts, w, m, v)


def _sum_rows(parts):
    def body(p_ref, o_ref):
        o_ref[...] = _sum_partials(p_ref)

    vmem = pl.BlockSpec(memory_space=pltpu.VMEM)
    return pl.pallas_call(body, name="sum_replicated_partials", in_specs=[vmem], out_specs=vmem,
                          out_shape=jax.ShapeDtypeStruct(parts.shape[1:], F32))(parts)


def _adamw_tensors(gs, ws, ms, vs):
    n = len(ws)
    shapes = [w.shape for w in ws]
    two_d = lambda a: a.reshape(1, -1) if a.ndim == 1 else a
    ops = [two_d(a) for a in list(gs) + list(ws) + list(ms) + list(vs)]

    def body(*refs):
        ins, outs = refs[:4 * n], refs[4 * n:]
        for k in range(n):
            g, w, m, v = (ins[j * n + k][...] for j in range(4))
            outs[k][...], outs[n + k][...], outs[2 * n + k][...] = _adamw_step(g, w, m, v)

    vmem = pl.BlockSpec(memory_space=pltpu.VMEM)
    res = pl.pallas_call(
        body, name="adamw_replicated", in_specs=[vmem] * (4 * n), out_specs=[vmem] * (3 * n),
        out_shape=[jax.ShapeDtypeStruct(o.shape, F32) for o in ops[n:2 * n]] * 3,
    )(*ops)
    return [[res[j * n + k].reshape(shapes[k]) for k in range(n)] for j in range(3)]


def _pad_rows(a, mult):
    pad = (-a.shape[0]) % mult
    return a if pad == 0 else jnp.concatenate([a, jnp.zeros((pad, a.shape[1]), a.dtype)], axis=0)


def _dw_rows(conv_dw_l):
    return jnp.pad(jnp.swapaxes(conv_dw_l, 0, 1), ((0, 0), (0, CONV_TAPS_PAD - CONV_KERNEL)))


def _pack_small(pw_l, dw_l, d):
    rows = jnp.concatenate([pw_l.reshape(-1, d), _dw_rows(dw_l).reshape(-1, d)], axis=0)
    return _pad_rows(rows, 8)


def _small_slabs(g_pw, g_dw, d):
    a = g_pw.reshape(N_DEV, -1, d)
    b = jnp.swapaxes(g_dw, 0, 1).reshape(N_DEV, -1, d)
    used = a.shape[1] + b.shape[1]
    return jnp.concatenate([a, b, jnp.zeros((N_DEV, (-used) % 8, d), g_pw.dtype)], axis=1)


def _unpack_small(rows, d):
    c = CONV_WIDTH // N_DEV
    n_pw = c * CONV_WIDTH // d
    n_dw = c * CONV_TAPS_PAD // d
    pw = rows[:n_pw].reshape(c, CONV_WIDTH)
    dw = jnp.swapaxes(rows[n_pw:n_pw + n_dw].reshape(c, CONV_TAPS_PAD), 0, 1)[:CONV_KERNEL]
    return pw, dw


def _pack_replicated(ln_g, pool_w, pool_scale, conv_b, conv_ln_g, conv_ln_b, attn_sinks, final_g, scalar, d):
    sinks = jnp.pad(attn_sinks, ((0, 0), (0, 256 - N_Q_HEADS)))
    small = jnp.concatenate([pool_scale, conv_b, conv_ln_g, conv_ln_b, sinks], axis=0)
    small = _pad_rows(small, d // 256)
    last = jnp.pad(scalar.reshape(1, 1), ((0, 0), (0, d - 1)))
    return _pad_rows(jnp.concatenate([ln_g.reshape(-1, d), final_g.reshape(-1, d), pool_w.reshape(-1, d),
                                      small.reshape(-1, d), last], axis=0), 8)


def _unpack_replicated(rows, d):
    n_pool = DEPTH * 4 * POOL_GROUP * POOL_GROUP // d
    n_small = -(-5 * DEPTH * 256 // d)
    ln_g = rows[:DEPTH]
    final_g = rows[DEPTH]
    pool_w = rows[DEPTH + 1: DEPTH + 1 + n_pool].reshape(DEPTH, 4, POOL_GROUP, POOL_GROUP)
    small = rows[DEPTH + 1 + n_pool: DEPTH + 1 + n_pool + n_small].reshape(-1, 256)[: 5 * DEPTH]
    pool_scale, conv_b, conv_ln_g, conv_ln_b = (small[DEPTH * k: DEPTH * (k + 1)] for k in range(4))
    sinks = small[4 * DEPTH: 5 * DEPTH, :N_Q_HEADS]
    scalar = rows[DEPTH + 1 + n_pool + n_small, 0]
    return ln_g, pool_w, pool_scale, conv_b, conv_ln_g, conv_ln_b, sinks, final_g, scalar


def _block_diag(pool_w):
    wide = jnp.tile(pool_w.reshape(POOL_WIDTH, POOL_GROUP), (1, POOL_WIDTH // POOL_GROUP))
    rows = lax.broadcasted_iota(jnp.int32, wide.shape, 0) // POOL_GROUP
    cols = lax.broadcasted_iota(jnp.int32, wide.shape, 1) // POOL_GROUP
    return jnp.where(rows == cols, wide, jnp.zeros_like(wide))


def _diag_blocks(mat):
    return jnp.stack([mat[POOL_GROUP * gi: POOL_GROUP * (gi + 1), POOL_GROUP * gi: POOL_GROUP * (gi + 1)]
                      for gi in range(4)], axis=0)


def kernel(x, ln_g, w_in, pool_w, pool_scale, conv_dw, conv_b, conv_ln_g, conv_ln_b, conv_pw, attn_sinks, w_out, final_g, loss_target, m_ln_g, m_w_in, m_pool_w, m_pool_scale, m_conv_dw, m_conv_b, m_conv_ln_g, m_conv_ln_b, m_conv_pw, m_attn_sinks, m_w_out, m_final_g, v_ln_g, v_w_in, v_pool_w, v_pool_scale, v_conv_dw, v_conv_b, v_conv_ln_g, v_conv_ln_b, v_conv_pw, v_attn_sinks, v_w_out, v_final_g):
    x0 = x[0]
    d = x0.shape[1]
    row = lambda a: a.reshape(1, -1)
    vec = _layer_vec
    slabs = lambda a: a.reshape(N_DEV, a.shape[0] // N_DEV, d)
    c_shard = CONV_WIDTH // N_DEV

    w_in_rows = [jnp.swapaxes(w_in[l], 0, 1).astype(MXU_DTYPE) for l in range(DEPTH)]
    w_out_rows = [w_out[l].astype(MXU_DTYPE) for l in range(DEPTH)]
    per_word = 4 // jnp.dtype(MXU_DTYPE).itemsize
    dw_t = jnp.stack([_dw_rows(conv_dw[l]) for l in range(DEPTH)], axis=0)
    dw_bits = (lax.bitcast_convert_type(dw_t, MXU_DTYPE) if per_word > 1 else dw_t).reshape(-1, d)
    n_pw = DEPTH * c_shard * CONV_WIDTH // d
    conv_rows = _pad_rows(jnp.concatenate([conv_pw.reshape(-1, d).astype(MXU_DTYPE), dw_bits], axis=0), 16)
    w_in_t, w_out_f = [None] * DEPTH, [None] * DEPTH
    w_in_t[0] = _all_gather(w_in_rows[0], "w_in_all_gather")
    wp_bd = [_block_diag(pool_w[l]).astype(MXU_DTYPE) for l in range(DEPTH)]

    xs, projs, cvs, ys = [x0], [], [], []
    q4 = D_IN // N_DEV // 4
    w_in_1 = [w_in_rows[1][q4 * k: q4 * (k + 1)] for k in range(4)]
    for l in range(DEPTH):
        if l == 0:
            proj, got = _in_proj(xs[0], vec(ln_g, 0), w_in_t[0], carried=[w_out_rows[0], conv_rows, w_in_1[0]])
            w_in_1_got = [got[2]]
            w_out_f[0] = got[0].reshape(D_MIX, d)
            pw_all = got[1][:, :n_pw].reshape(N_DEV, DEPTH, c_shard, CONV_WIDTH)
            pw_f = [pw_all[:, k].reshape(CONV_WIDTH, CONV_WIDTH) for k in range(DEPTH)]
            bits = got[1][:, n_pw:n_pw + dw_bits.shape[0]].reshape(
                (N_DEV, DEPTH, c_shard, CONV_TAPS_PAD) + (per_word,) * (per_word > 1))
            dw_all = lax.bitcast_convert_type(bits, F32) if per_word > 1 else bits
            dw_f = [jnp.swapaxes(dw_all[:, k].reshape(CONV_WIDTH, CONV_TAPS_PAD), 0, 1) for k in range(DEPTH)]
        (y_pc, cv, diff), got = _poolconv_fwd(proj, wp_bd[l], vec(pool_scale, l), dw_f[l], vec(conv_b, l),
                                              vec(conv_ln_g, l), vec(conv_ln_b, l), pw_f[l],
                                              carried=[w_in_1[1], w_in_1[2]] if l == 0 else [])
        if l == 0:
            w_in_1_got += list(got)
        y_at, got = _attn_fwd(proj, attn_sinks[l], carried=[w_out_rows[1], w_in_1[3]] if l == 0 else [])
        if l == 0:
            w_out_f[1] = got[0].reshape(D_MIX, d)
            w_in_t[1] = jnp.concatenate(w_in_1_got + [got[1]], axis=1).reshape(D_IN, d)
        projs.append(proj)
        cvs.append((cv, diff))
        ys.append((y_pc, y_at))
        if l < DEPTH - 1:
            x_next, proj = _out_in_proj(xs[l], y_pc, y_at, w_out_f[l], vec(ln_g, l + 1), w_in_t[l + 1])
            xs.append(x_next)

    l = DEPTH - 1
    sq, g_final, dx = _out_proj_loss(xs[l], ys[l][0], ys[l][1], w_out_f[l], loss_target[0], row(final_g))

    l = 1
    dy, g_wout1 = _out_bwd(dx, ys[l][0], ys[l][1], w_out_f[l])
    (da, g_wp1, g_pw1, g_dw1, g_vec1), _ = _poolconv_bwd(
        projs[l], *cvs[l], dy, wp_bd[l], vec(pool_scale, l), dw_f[l], vec(conv_ln_g, l), vec(conv_ln_b, l), pw_f[l])
    (dattn, gs1), _ = _attn_bwd(projs[l], dy, attn_sinks[l])
    dx, g_ln1, g_win_t1 = _in_bwd(da, dattn, xs[l], dx, vec(ln_g, l), w_in_t[l])
    l = 0
    dy, g_wout0 = _out_bwd(dx, ys[l][0], ys[l][1], w_out_f[l])
    (da, g_wp0, g_pw0, g_dw0, g_vec0), (r_win1, r_wout1, r_small1) = _poolconv_bwd(
        projs[l], *cvs[l], dy, wp_bd[l], vec(pool_scale, l), dw_f[l], vec(conv_ln_g, l), vec(conv_ln_b, l), pw_f[l],
        carried=[slabs(g_win_t1), slabs(g_wout1), _small_slabs(g_pw1, g_dw1, d)])
    (dattn, gs0), (r_wout0, r_small0) = _attn_bwd(projs[l], dy, attn_sinks[l],
                                                  carried=[slabs(g_wout0), _small_slabs(g_pw0, g_dw0, d)])
    dproj, g_win_t0 = _in_bwd_dw(da, dattn, xs[l], vec(ln_g, l))
    (dx, g_ln0), (r_win0,) = _in_bwd_dx(dproj, xs[l], dx, vec(ln_g, l), w_in_t[l], carried=[slabs(g_win_t0)])
    grad_x = dx[None]

    gv = jnp.stack([g_vec0, g_vec1], axis=0)
    rep_part = _pack_replicated(
        jnp.concatenate([g_ln0, g_ln1], axis=0), jnp.stack([_diag_blocks(g_wp0), _diag_blocks(g_wp1)], axis=0),
        gv[:, 0], gv[:, 1], gv[:, 2], gv[:, 3], jnp.stack([gs0[:, 0], gs1[:, 0]], axis=0), g_final, sq[0, 0], d)
    (r_rep,) = _final_exchange([rep_part])

    t = lambda a: jnp.swapaxes(a, 1, 2)
    win = [t(o) for o in _adamw_layers([r_win0, r_win1], t(w_in), t(m_w_in), t(v_w_in), "adamw_w_in")]
    wout = _adamw_layers([r_wout0, r_wout1], w_out, m_w_out, v_w_out, "adamw_w_out")
    pack_s = lambda pw_, dw_: jnp.stack([_pack_small(pw_[l], dw_[l], d) for l in range(DEPTH)], axis=0)
    small = _adamw_layers([r_small0, r_small1], pack_s(conv_pw, conv_dw), pack_s(m_conv_pw, m_conv_dw),
                          pack_s(v_conv_pw, v_conv_dw), "adamw_conv")
    small = [[_unpack_small(o[l], d) for l in range(DEPTH)] for o in small]
    *rep_grads, sq_sum = _unpack_replicated(_sum_rows(r_rep), d)
    loss = 0.5 / d * sq_sum
    rep_w = [ln_g, pool_w, pool_scale, conv_b, conv_ln_g, conv_ln_b, attn_sinks, final_g]
    rep_m = [m_ln_g, m_pool_w, m_pool_scale, m_conv_b, m_conv_ln_g, m_conv_ln_b, m_attn_sinks, m_final_g]
    rep_v = [v_ln_g, v_pool_w, v_pool_scale, v_conv_b, v_conv_ln_g, v_conv_ln_b, v_attn_sinks, v_final_g]
    rep = [rep_grads] + _adamw_tensors(rep_grads, rep_w, rep_m, rep_v)

    outs = []
    for k in range(4):
        r_ln, r_pool, r_scale, r_cb, r_lng, r_lnb, r_sinks, r_final = rep[k]
        s_pw = jnp.stack([small[k][l][0] for l in range(DEPTH)], axis=0)
        s_dw = jnp.stack([small[k][l][1] for l in range(DEPTH)], axis=0)
        outs += [r_ln, win[k], r_pool, r_scale, s_dw, r_cb, r_lng, r_lnb, s_pw, r_sinks, wout[k], r_final]
    return (loss, grad_x, *outs)
```

```python
import jax
import jax.numpy as jnp
from jax import lax
from jax.experimental import pallas as pl
from jax.experimental.pallas import tpu as pltpu

F32 = jnp.float32
MXU_DTYPE = jnp.bfloat16
EXCHANGE_DTYPE = jnp.bfloat16

N_DEV = 8
DEPTH = 2
POOL_WIDTH = 256
POOL_GROUP = 64
CONV_WIDTH = 256
CONV_KERNEL = 31
CONV_TAPS_PAD = 32
HEAD_DIM = 64
N_KV_HEADS = 2
Q_PER_KV = 4
N_Q_HEADS = 8
ATTN_WIDTH = 512
BLOCK = 128
D_MIX = 1024
D_IN = 2560
HALF_IN = 1280
EPS = 1e-6
SCALE = HEAD_DIM ** -0.5
NEG = -1e30

ADAM_LR = 0.001
ADAM_B1 = 0.9
ADAM_B2 = 0.999
ADAM_EPS = 1e-08
ADAM_WD = 0.01
ADAM_STEP = 10

HALO = 32
ROW_TILE = 512
IN_BWD_TILE = 512
VMEM_LIMIT = 56 * 1024 * 1024

_NN = (((1,), (0,)), ((), ()))
_NT = (((1,), (1,)), ((), ()))
_TN = (((0,), (0,)), ((), ()))
_ANY = pl.BlockSpec(memory_space=pl.ANY)


def _mm(a, b, dims=_NN):
    return lax.dot_general(a.astype(MXU_DTYPE), b.astype(MXU_DTYPE), dims, preferred_element_type=F32)


def _sig(x):
    return 1.0 / (1.0 + jnp.exp(-x))


def _dsilu(z, s):
    return s * (1.0 + z * (1.0 - s))


def _params(n_grid):
    return pltpu.CompilerParams(dimension_semantics=("arbitrary",) * n_grid, vmem_limit_bytes=VMEM_LIMIT)


def _row_tile(rows, cap):
    t = min(rows, cap)
    while rows % t or t % 8:
        t -= 8
    return t


def _full(shape):
    return pl.BlockSpec(shape, lambda i: (0,) * len(shape))


def _layer_vec(stacked, layer):
    arr = stacked.reshape(stacked.shape[0], 1, stacked.shape[-1])
    return arr, pl.BlockSpec((None, 1, arr.shape[-1]), lambda i: (layer, 0, 0))


def _mesh_pos():
    return lax.axis_index("x"), lax.axis_index("y"), lax.axis_index("c")


class _Exchange:
    def __init__(self, sources):
        self.sources = list(sources)
        self.n = len(self.sources)
        self.gather = [s.ndim == 2 for s in self.sources]

    def out_shapes(self):
        return [jax.ShapeDtypeStruct((N_DEV,) + s.shape[-2:], s.dtype) for s in self.sources]

    def scratch(self):
        if not self.n:
            return []
        return [pltpu.SemaphoreType.DMA((7 * self.n,)), pltpu.SemaphoreType.DMA((7 * self.n,)),
                pltpu.SemaphoreType.DMA((self.n,))]

    def copies(self, src_refs, dst_refs, sems):
        send_sems, recv_sems, local_sems = sems
        x, y, c = _mesh_pos()
        me = 4 * x + 2 * y + c
        out = []
        for a, (src, dst) in enumerate(zip(src_refs, dst_refs)):
            out.append(pltpu.make_async_copy(src if self.gather[a] else src.at[me], dst.at[me], local_sems.at[a]))
            for k in range(1, N_DEV):
                tx, ty, tc = x ^ ((k >> 2) & 1), y ^ ((k >> 1) & 1), c ^ (k & 1)
                out.append(pltpu.make_async_remote_copy(
                    src_ref=src if self.gather[a] else src.at[4 * tx + 2 * ty + tc], dst_ref=dst.at[me],
                    send_sem=send_sems.at[7 * a + k - 1], recv_sem=recv_sems.at[7 * a + k - 1],
                    device_id=(tx, ty, tc), device_id_type=pl.DeviceIdType.MESH))
        return out

    def run(self, refs, first, last):
        if not self.n:
            return
        src_refs, dst_refs, sems = refs

        @pl.when(first)
        def _():
            for cp in self.copies(src_refs, dst_refs, sems):
                cp.start()

        @pl.when(last)
        def _():
            for cp in self.copies(src_refs, dst_refs, sems):
                cp.wait()


def _split_refs(refs, n_in, n_out, exch):
    ins = refs[:n_in]
    srcs = refs[n_in:n_in + exch.n]
    outs = refs[n_in + exch.n:n_in + exch.n + n_out]
    dsts = refs[n_in + exch.n + n_out:n_in + 2 * exch.n + n_out]
    rest = refs[n_in + 2 * exch.n + n_out:]
    sems = rest[len(rest) - 3:] if exch.n else ()
    scratch = rest[:len(rest) - 3] if exch.n else rest
    return ins, outs, scratch, (srcs, dsts, sems)


def _final_exchange(sources):
    exch = _Exchange(sources)

    def body(*refs):
        _, _, _, xrefs = _split_refs(refs, 0, 0, exch)
        for cp in exch.copies(*xrefs):
            cp.start()
        for cp in exch.copies(*xrefs):
            cp.wait()

    return pl.pallas_call(
        body, name="final_exchange", out_shape=exch.out_shapes(),
        in_specs=[_ANY] * exch.n, out_specs=[_ANY] * exch.n, scratch_shapes=exch.scratch(),
    )(*exch.sources)


def _all_gather(shard, name):
    m_per, n = shard.shape

    def body(x_ref, out_ref, send_sems, recv_sems, local_sem):
        x, y, c = _mesh_pos()
        me, sibling = (x, y, c), (x, y, 1 - c)
        chips = [(1 - x, y), (x, 1 - y), (1 - x, 1 - y)]

        def rows(px, py, pc):
            return out_ref.at[pl.ds((4 * px + 2 * py + pc) * m_per, m_per), :]

        def copy(k, block, to, src=None):
            return pltpu.make_async_remote_copy(
                src_ref=rows(*block) if src is None else src, dst_ref=rows(*block),
                send_sem=send_sems.at[k], recv_sem=recv_sems.at[k],
                device_id=to, device_id_type=pl.DeviceIdType.MESH)

        mine = pltpu.make_async_copy(x_ref, rows(*me), local_sem)
        mine.start()
        first = [copy(0, me, sibling, src=x_ref)]
        first += [copy(1 + j, me, (*chip, c), src=x_ref) for j, chip in enumerate(chips)]
        for cp in first:
            cp.start()
        passed = [copy(4 + j, (*chip, c), sibling) for j, chip in enumerate(chips)]
        for j, chip in enumerate(chips):
            copy(1 + j, (*chip, c), me).wait_recv()
            passed[j].start()
        copy(0, sibling, me).wait_recv()
        for j, chip in enumerate(chips):
            copy(4 + j, (*chip, 1 - c), me).wait_recv()
        for cp in first + passed:
            cp.wait_send()
        mine.wait()

    return pl.pallas_call(
        body, name=name,
        out_shape=jax.ShapeDtypeStruct((N_DEV * m_per, n), shard.dtype),
        in_specs=[pl.BlockSpec(memory_space=pltpu.VMEM)],
        out_specs=pl.BlockSpec(memory_space=pltpu.VMEM),
        scratch_shapes=[pltpu.SemaphoreType.DMA((7,)), pltpu.SemaphoreType.DMA((7,)), pltpu.SemaphoreType.DMA],
        compiler_params=pltpu.CompilerParams(vmem_limit_bytes=VMEM_LIMIT),
    )(shard)


def _by_group(lane, v2, v4, v8, v16):
    return jnp.where(lane < 64, v2, jnp.where(lane < 128, v4, jnp.where(lane < 192, v8, v16)))


def _pool_count(t0, n):
    lane = lax.broadcasted_iota(jnp.int32, (1, POOL_WIDTH), 1)
    t = (t0 + lax.broadcasted_iota(jnp.int32, (n, 1), 0)).astype(F32)
    wnd = _by_group(lane, 2.0, 4.0, 8.0, 16.0)
    return jnp.minimum(t + 1.0, wnd)


def _pool_diff(u_ext, t0, ts):
    lane = lax.broadcasted_iota(jnp.int32, (1, POOL_WIDTH), 1)
    s2 = u_ext + pltpu.roll(u_ext, 1, 0)
    s4 = s2 + pltpu.roll(s2, 2, 0)
    s8 = s4 + pltpu.roll(s4, 4, 0)
    s16 = s8 + pltpu.roll(s8, 8, 0)
    pooled = _by_group(lane, s2, s4, s8, s16)[HALO:]
    return pooled / _pool_count(t0, ts) - u_ext[HALO:]


def _pool_diff_bwd(w, ts):
    n = w.shape[0]
    lane = lax.broadcasted_iota(jnp.int32, (1, POOL_WIDTH), 1)
    f2 = w + pltpu.roll(w, n - 1, 0)
    f4 = f2 + pltpu.roll(f2, n - 2, 0)
    f8 = f4 + pltpu.roll(f4, n - 4, 0)
    f16 = f8 + pltpu.roll(f8, n - 8, 0)
    return _by_group(lane, f2, f4, f8, f16)[:ts]


CONV_CHUNK = 64


def _conv_taps():
    return [(8 * m + r, r, m) for r in range(8) for m in range(4) if 8 * m + r < CONV_KERNEL]


def _store_shifted(dst_ref, x, up):
    n = x.shape[0]
    for r in range(8):
        dst_ref[r] = x if r == 0 else pltpu.roll(x, n - r if up else r, 0)


def _anticausal_conv(src_ref, dw_ref, out_ref, n_out):
    def chunk(c, carry):
        t0 = pl.multiple_of(c * CONV_CHUNK, CONV_CHUNK)
        acc = None
        for d, r, m in _conv_taps():
            term = dw_ref[pl.ds(CONV_KERNEL - 1 - d, 1), :] * src_ref[r, pl.ds(t0 + 8 * m, CONV_CHUNK), :]
            acc = term if acc is None else acc + term
        out_ref[pl.ds(t0, CONV_CHUNK), :] = acc
        return carry

    lax.fori_loop(0, n_out // CONV_CHUNK, chunk, 0)


def _depthwise_conv_weight_grad(x_ref, dout_up_ref, acc_ref, n_rows):
    acc_ref[...] = jnp.zeros_like(acc_ref)

    def chunk(c, carry):
        t0 = pl.multiple_of(c * CONV_CHUNK, CONV_CHUNK)
        xv = x_ref[pl.ds(t0, CONV_CHUNK), :]
        for d, r, m in _conv_taps():
            prod = xv * dout_up_ref[r, pl.ds(t0 + 8 * m, CONV_CHUNK), :]
            acc_ref[CONV_KERNEL - 1 - d] += jnp.sum(prod.reshape(CONV_CHUNK // 8, 8, prod.shape[-1]), axis=0)
        return carry

    lax.fori_loop(0, n_rows // CONV_CHUNK, chunk, 0)


def _layer_norm(cv):
    mu = jnp.mean(cv, axis=-1, keepdims=True)
    xc = cv - mu
    var = jnp.mean(xc * xc, axis=-1, keepdims=True)
    rstd = lax.rsqrt(var + EPS)
    return xc * rstd, rstd


PAIR_ROWS = 2 * BLOCK
PAIR_COLS = 4 * BLOCK
ATTN_BLOCKS_PER_STEP = 4


def _attn_blocks_per_step(s_len):
    qb = ATTN_BLOCKS_PER_STEP
    while (s_len // BLOCK) % qb:
        qb //= 2
    return qb


def _pair_rows(v0, v1):
    r = lax.broadcasted_iota(jnp.int32, (PAIR_ROWS, 1), 0)
    return jnp.where(r < BLOCK, v0, v1)


def _fill_attn_bias(bias_ref):
    rows = lax.broadcasted_iota(jnp.int32, (PAIR_ROWS, PAIR_COLS), 0)
    cols = lax.broadcasted_iota(jnp.int32, (PAIR_ROWS, PAIR_COLS), 1)
    key = cols & (2 * BLOCK - 1)
    dist = BLOCK + (rows & (BLOCK - 1)) - key
    in_band = (dist >= 0) & (dist < BLOCK)
    distf = dist.astype(F32)
    second = cols >= 2 * BLOCK
    for kh in range(N_KV_HEADS):
        slope_of = lambda j, a: 2.0 ** -(Q_PER_KV * kh + 2 * j + a + 1)
        slope = jnp.where(rows < BLOCK, jnp.where(second, slope_of(0, 1), slope_of(0, 0)),
                          jnp.where(second, slope_of(1, 1), slope_of(1, 0)))
        bias = -slope * distf
        bias_ref[0, kh] = jnp.where(in_band & (key >= BLOCK), bias, NEG)
        bias_ref[1, kh] = jnp.where(in_band, bias, NEG)


def _pair_block_matrix(x, x_swapped, kh):
    lo = lax.broadcasted_iota(jnp.int32, (1, 2 * HEAD_DIM), 1) < HEAD_DIM
    in_lo, in_hi = (x, x_swapped) if kh == 0 else (x_swapped, x)
    return jnp.concatenate([jnp.where(lo, in_lo, 0.0), jnp.where(lo, 0.0, in_hi)], axis=0).astype(MXU_DTYPE)


def _pair_queries(q, kh):
    return (_pair_stack(q, kh) * SCALE).astype(MXU_DTYPE)


def _pair_stack(a, kh):
    return jnp.concatenate([a[:, 2 * BLOCK * kh: 2 * BLOCK * kh + BLOCK],
                            a[:, 2 * BLOCK * kh + BLOCK: 2 * BLOCK * (kh + 1)]], axis=0)


def _pair_unstack(parts):
    return jnp.concatenate([p[BLOCK * j: BLOCK * (j + 1)] for p in parts for j in range(2)], axis=-1)


def _pair_softmax(s, kh, sinks_ref):
    ps, p_sinks = [], []
    for a in range(2):
        sa = s[:, 2 * BLOCK * a: 2 * BLOCK * (a + 1)]
        sink = _pair_rows(sinks_ref[Q_PER_KV * kh + a], sinks_ref[Q_PER_KV * kh + 2 + a])
        m = jnp.maximum(jnp.max(sa, axis=-1, keepdims=True), sink)
        e = jnp.exp(sa - m)
        es = jnp.exp(sink - m)
        inv = 1.0 / (jnp.sum(e, axis=-1, keepdims=True) + es)
        ps.append(e * inv)
        p_sinks.append(es * inv)
    return jnp.concatenate(ps, axis=-1), p_sinks


def _fold_pair_halves(t):
    lo = lax.broadcasted_iota(jnp.int32, (1, 2 * HEAD_DIM), 1) < HEAD_DIM
    u = jnp.where(lo, t[:2 * BLOCK], t[2 * BLOCK:])
    return u + pltpu.roll(u, HEAD_DIM, 1)


def _in_proj(x, g, w_t, carried=()):
    s_len, d = x.shape
    ts = _row_tile(s_len, ROW_TILE)
    nt = s_len // ts
    exch = _Exchange(carried)

    def body(*refs):
        (x_ref, g_ref, w_ref), (o_ref,), _, xrefs = _split_refs(refs, 3, 1, exch)
        i = pl.program_id(0)
        exch.run(xrefs, i == 0, i == nt - 1)
        xv = x_ref[...]
        r = lax.rsqrt(jnp.mean(xv * xv, axis=-1, keepdims=True) + EPS)
        o_ref[...] = _mm(xv * r * g_ref[...], w_ref[...], _NT)

    res = pl.pallas_call(
        body, name="in_proj_carrier" if exch.n else "in_proj", grid=(nt,),
        in_specs=[pl.BlockSpec((ts, d), lambda i: (i, 0)), g[1], _full((D_IN, d))] + [_ANY] * exch.n,
        out_specs=[pl.BlockSpec((ts, D_IN), lambda i: (i, 0))] + [_ANY] * exch.n,
        out_shape=[jax.ShapeDtypeStruct((s_len, D_IN), F32)] + exch.out_shapes(),
        scratch_shapes=exch.scratch(),
        compiler_params=_params(1),
    )(x, g[0], w_t, *exch.sources)
    return res[0], res[1:]


def _poolconv_fwd(proj, wp, scale, dw, cb, lng, lnb, pw, carried=()):
    s_len = proj.shape[0]
    ts = _row_tile(s_len, ROW_TILE)
    hb = ts // HALO
    nt = s_len // ts
    exch = _Exchange(carried)

    def body(*refs):
        ins, (y_ref, cv_ref, diff_ref), _, xrefs = _split_refs(refs, 9, 3, exch)
        p_ref, ph_ref, wp_ref, sc_ref, dw_ref, cb_ref, lng_ref, lnb_ref, pw_ref = ins
        i = pl.program_id(0)
        exch.run(xrefs, i == 0, i == nt - 1)
        halo = jnp.where(i > 0, ph_ref[...], 0.0)
        y_mix, cv, diff = _pool_conv_mixers(p_ref[...], halo, i * ts, ts, wp_ref, sc_ref, dw_ref, cb_ref, lng_ref,
                                            lnb_ref, pw_ref)
        y_ref[...] = y_mix.astype(y_ref.dtype)
        cv_ref[...] = cv
        diff_ref[...] = diff

    res = pl.pallas_call(
        body, name="poolconv_fwd_carrier" if exch.n else "poolconv_fwd", grid=(nt,),
        in_specs=[pl.BlockSpec((ts, HALF_IN), lambda i: (i, 0)),
                  pl.BlockSpec((HALO, HALF_IN), lambda i: (jnp.maximum(i * hb - 1, 0), 0)),
                  _full((256, 256)), scale[1], _full((CONV_TAPS_PAD, 256)), cb[1], lng[1], lnb[1],
                  _full((256, 256))]
        + [_ANY] * exch.n,
        out_specs=[pl.BlockSpec((ts, 512), lambda i: (i, 0)), pl.BlockSpec((ts, 256), lambda i: (i, 0)),
                   pl.BlockSpec((ts, 256), lambda i: (i, 0))] + [_ANY] * exch.n,
        out_shape=[jax.ShapeDtypeStruct((s_len, 512), MXU_DTYPE), jax.ShapeDtypeStruct((s_len, 256), F32),
                   jax.ShapeDtypeStruct((s_len, 256), MXU_DTYPE)] + exch.out_shapes(),
        scratch_shapes=exch.scratch(),
        compiler_params=_params(1),
    )(proj, proj, wp, scale[0], dw, cb[0], lng[0], lnb[0], pw, *exch.sources)
    return res[:3], res[3:]


def _pool_conv_mixers(cur, halo, t0, ts, wp_ref, sc_ref, dw_ref, cb_ref, lng_ref, lnb_ref, pw_ref):
    ext = jnp.concatenate([halo, cur], axis=0)
    diff = _pool_diff(ext[:, 0:256], t0, ts).astype(MXU_DTYPE)
    gp = cur[:, 256:512]
    y_pool = _mm(diff, wp_ref[...]) * sc_ref[...] * (gp * _sig(gp))
    hh = ext[:, 512:768] * _sig(ext[:, 768:1024])
    shifted = [hh if r == 0 else pltpu.roll(hh, r, 0) for r in range(8)]
    cv = cb_ref[...]
    for dist, r, m in _conv_taps():
        cv = cv + dw_ref[pl.ds(CONV_KERNEL - 1 - dist, 1), :] * shifted[r][HALO - 8 * m: HALO - 8 * m + ts]
    n, _ = _layer_norm(cv)
    z = n * lng_ref[...] + lnb_ref[...]
    gc = cur[:, 1024:1280]
    y_conv = _mm(z * _sig(z), pw_ref[...]) * (gc * _sig(gc))
    return jnp.concatenate([y_pool, y_conv], axis=-1), cv, diff


def _attn_fwd(proj, sinks, carried=()):
    s_len = proj.shape[0]
    qb = _attn_blocks_per_step(s_len)
    ts = qb * BLOCK
    nt = s_len // ts
    exch = _Exchange(carried)

    def body(*refs):
        (p_ref, kvp_ref, sinks_ref), (y_ref, ya_ref), (bias_ref,), xrefs = _split_refs(refs, 3, 2, exch)
        i = pl.program_id(0)
        exch.run(xrefs, i == 0, i == nt - 1)

        @pl.when(i == 0)
        def _():
            _fill_attn_bias(bias_ref)

        probs = [(b, kh) for b in range(qb) for kh in range(N_KV_HEADS)]
        scores, v_bds = {}, {}
        for b in range(qb):
            r0 = BLOCK * b
            kv_prev = kvp_ref[...] if b == 0 else p_ref[r0 - BLOCK:r0, 512:768]
            kv2 = jnp.concatenate([kv_prev, p_ref[r0:r0 + BLOCK, 512:768]], axis=0)
            k2, v2 = kv2[:, :BLOCK], kv2[:, BLOCK:]
            k2_swapped, v2_swapped = pltpu.roll(k2, HEAD_DIM, 1), pltpu.roll(v2, HEAD_DIM, 1)
            variant = jnp.where(i == 0, 0, 1) if b == 0 else 1
            q = p_ref[r0:r0 + BLOCK, 0:512]
            for kh in range(N_KV_HEADS):
                k_bd = _pair_block_matrix(k2, k2_swapped, kh)
                v_bds[b, kh] = _pair_block_matrix(v2, v2_swapped, kh)
                scores[b, kh] = _mm(_pair_queries(q, kh), k_bd, _NT) + bias_ref[variant, kh]
        ps = {pr: _pair_softmax(scores[pr], pr[1], sinks_ref)[0] for pr in probs}
        outs = {pr: _mm(ps[pr], v_bds[pr]) for pr in probs}
        for b in range(qb):
            r0 = BLOCK * b
            ga = p_ref[r0:r0 + BLOCK, 768:1280]
            ya = _pair_unstack([outs[b, kh] for kh in range(N_KV_HEADS)])
            ya_ref[r0:r0 + BLOCK, :] = ya
            y_ref[r0:r0 + BLOCK, :] = (ya * (ga * _sig(ga))).astype(y_ref.dtype)

    res = pl.pallas_call(
        body, name="attn_fwd_carrier" if exch.n else "attn_fwd", grid=(nt,),
        in_specs=[pl.BlockSpec((ts, HALF_IN), lambda i: (i, 1)),
                  pl.BlockSpec((BLOCK, 256), lambda i: (jnp.maximum(i * qb - 1, 0), 7)),
                  pl.BlockSpec(memory_space=pltpu.SMEM)] + [_ANY] * exch.n,
        out_specs=[pl.BlockSpec((ts, 512), lambda i: (i, 0)), pl.BlockSpec((ts, 512), lambda i: (i, 0))]
        + [_ANY] * exch.n,
        out_shape=[jax.ShapeDtypeStruct((s_len, 512), MXU_DTYPE), jax.ShapeDtypeStruct((s_len, 512), F32)]
        + exch.out_shapes(),
        scratch_shapes=[pltpu.VMEM((2, N_KV_HEADS, PAIR_ROWS, PAIR_COLS), F32)] + exch.scratch(),
        compiler_params=_params(1),
    )(proj, proj, sinks, *exch.sources)
    return res[:2], res[2:]


def _out_in_proj(x, y_pc, y_at, w_out, g_next, w_t_next):
    s_len, d = x.shape
    ts = _row_tile(s_len, ROW_TILE)

    def body(x_ref, a_ref, b_ref, w_ref, g_ref, wn_ref, o_ref, p_ref):
        y = jnp.concatenate([a_ref[...], b_ref[...]], axis=-1)
        xv = x_ref[...] + _mm(y, w_ref[...])
        o_ref[...] = xv
        r = lax.rsqrt(jnp.mean(xv * xv, axis=-1, keepdims=True) + EPS)
        p_ref[...] = _mm(xv * r * g_ref[...], wn_ref[...], _NT)

    tile = pl.BlockSpec((ts, d), lambda i: (i, 0))
    half = pl.BlockSpec((ts, 512), lambda i: (i, 0))
    return pl.pallas_call(
        body, name="out_in_proj", grid=(s_len // ts,),
        in_specs=[tile, half, half, _full((D_MIX, d)), g_next[1], _full((D_IN, d))],
        out_specs=[tile, pl.BlockSpec((ts, D_IN), lambda i: (i, 0))],
        out_shape=[jax.ShapeDtypeStruct((s_len, d), F32), jax.ShapeDtypeStruct((s_len, D_IN), F32)],
        compiler_params=_params(1),
    )(x, y_pc, y_at, w_out, g_next[0], w_t_next)


def _out_proj_loss(x, y_pc, y_at, w_out, target, g):
    s_len, d = x.shape
    ts = _row_tile(s_len, ROW_TILE)

    def body(x_ref, a_ref, b_ref, w_ref, t_ref, g_ref, sq_ref, dg_ref, dx_ref):
        i = pl.program_id(0)
        y = jnp.concatenate([a_ref[...], b_ref[...]], axis=-1)
        xv = x_ref[...] + _mm(y, w_ref[...])
        gv = g_ref[...]
        r = lax.rsqrt(jnp.mean(xv * xv, axis=-1, keepdims=True) + EPS)
        xr = xv * r
        err = xr * gv - t_ref[...]
        dout = err * (1.0 / d)
        w = dout * gv
        dx_ref[...] = r * (w - xr * jnp.mean(w * xr, axis=-1, keepdims=True))

        @pl.when(i == 0)
        def _():
            sq_ref[...] = jnp.zeros_like(sq_ref)
            dg_ref[...] = jnp.zeros_like(dg_ref)

        sq = jnp.sum(jnp.sum(err * err, axis=-1, keepdims=True), axis=0, keepdims=True)
        sq_ref[...] += jnp.broadcast_to(sq, sq_ref.shape)
        dg_ref[...] += jnp.sum(dout * xr, axis=0, keepdims=True)

    tile = pl.BlockSpec((ts, d), lambda i: (i, 0))
    half = pl.BlockSpec((ts, 512), lambda i: (i, 0))
    return pl.pallas_call(
        body, name="out_proj_loss", grid=(s_len // ts,),
        in_specs=[tile, half, half, _full((D_MIX, d)), tile, _full((1, d))],
        out_specs=[_full((1, 128)), _full((1, d)), tile],
        out_shape=[jax.ShapeDtypeStruct((1, 128), F32), jax.ShapeDtypeStruct((1, d), F32),
                   jax.ShapeDtypeStruct((s_len, d), F32)],
        compiler_params=_params(1),
    )(x, y_pc, y_at, w_out, target, g)


def _out_bwd(dxo, y_pc, y_at, w_out):
    s_len, d = dxo.shape
    ts = _row_tile(s_len, ROW_TILE)
    nt = s_len // ts

    def body(dx_ref, a_ref, b_ref, w_ref, dy_ref, gw_ref, acc_ref):
        i = pl.program_id(0)
        dxv = dx_ref[...].astype(MXU_DTYPE)
        dy_ref[...] = _mm(dxv, w_ref[...], _NT)

        @pl.when(i == 0)
        def _():
            acc_ref[...] = jnp.zeros_like(acc_ref)

        y = jnp.concatenate([a_ref[...], b_ref[...]], axis=-1)
        acc_ref[...] += _mm(y, dxv, _TN)

        @pl.when(i == nt - 1)
        def _():
            gw_ref[...] = acc_ref[...].astype(gw_ref.dtype)

    return pl.pallas_call(
        body, name="out_bwd", grid=(nt,),
        in_specs=[pl.BlockSpec((ts, d), lambda i: (i, 0)), pl.BlockSpec((ts, 512), lambda i: (i, 0)),
                  pl.BlockSpec((ts, 512), lambda i: (i, 0)), _full((D_MIX, d))],
        out_specs=[pl.BlockSpec((ts, D_MIX), lambda i: (i, 0)), _full((D_MIX, d))],
        out_shape=[jax.ShapeDtypeStruct((s_len, D_MIX), F32), jax.ShapeDtypeStruct((D_MIX, d), EXCHANGE_DTYPE)],
        scratch_shapes=[pltpu.VMEM((D_MIX, d), F32)],
        compiler_params=_params(1),
    )(dxo, y_pc, y_at, w_out)


def _poolconv_bwd(proj, cv, diff, dy, wp, scale, dw, lng, lnb, pw, carried=()):
    s_len = proj.shape[0]
    ts = _row_tile(s_len, ROW_TILE)
    hb = ts // HALO
    nt = s_len // ts
    last_halo = s_len // HALO - 1
    n = ts + HALO
    exch = _Exchange(carried)

    def body(*refs):
        ins, outs, (up_ref, hh_ref, dhh_ref, gdw_acc_ref), xrefs = _split_refs(refs, 13, 5, exch)
        (p_ref, pn_ref, cv_ref, cvn_ref, diff_ref, dy_ref, dyn_ref, wp_ref, sc_ref, dw_ref, lng_ref, lnb_ref,
         pw_ref) = ins
        da_ref, gwp_ref, gpw_ref, gdw_ref, gvec_ref = outs
        i = pl.program_id(0)
        exch.run(xrefs, i == 0, i == nt - 1)
        has_next = i < nt - 1
        cur = p_ref[...]
        nxt = jnp.where(has_next, pn_ref[...], 0.0)
        dyx = jnp.concatenate([dy_ref[...], jnp.where(has_next, dyn_ref[...], 0.0)], axis=0)
        row = lax.broadcasted_iota(jnp.int32, (n, 1), 0)
        in_seq = (row < ts) | has_next
        scale_v = sc_ref[...]

        cvx = jnp.concatenate([cv_ref[...], jnp.where(has_next, cvn_ref[...], 0.0)], axis=0)
        nrm, rstd = _layer_norm(cvx)
        z = nrm * lng_ref[...] + lnb_ref[...]
        sz = _sig(z)
        sw = z * sz
        gc = jnp.concatenate([cur[:, 1024:1280], nxt[:, 1024:1280]], axis=0)
        sgc = _sig(gc)
        yc = _mm(sw, pw_ref[...])
        dyc = dyx[:, 256:512]
        d_yc = dyc * (gc * sgc)
        d_gc = (dyc * yc * _dsilu(gc, sgc))[:ts]
        d_z = _mm(d_yc, pw_ref[...], _NT) * _dsilu(z, sz)
        d_n = d_z * lng_ref[...]
        d_cv = rstd * (d_n - jnp.mean(d_n, axis=-1, keepdims=True)
                       - nrm * jnp.mean(d_n * nrm, axis=-1, keepdims=True))
        d_cv = jnp.where(in_seq, d_cv, 0.0)
        _store_shifted(up_ref, d_cv, up=True)
        _anticausal_conv(up_ref, dw_ref, dhh_ref, ts)
        d_hh = dhh_ref[...]
        a_c, sb_c = cur[:, 512:768], _sig(cur[:, 768:1024])
        hh_ref[...] = a_c * sb_c
        d_a = d_hh * sb_c
        d_b = d_hh * a_c * sb_c * (1.0 - sb_c)
        d_cv_t = d_cv[:ts]
        _depthwise_conv_weight_grad(hh_ref, up_ref, gdw_acc_ref, ts)

        diff = diff_ref[...]
        raw = _mm(diff, wp_ref[...])
        gp = jnp.concatenate([cur[:, 256:512], nxt[:, 256:512]], axis=0)
        sgp = _sig(gp)
        dyp = dyx[:, 0:256]
        d_yp = dyp * (gp * sgp)
        d_gp = dyp[:ts] * (raw * scale_v) * _dsilu(gp, sgp)[:ts]
        d_raw = d_yp * scale_v
        d_diff = _mm(d_raw, wp_ref[...], _NT)
        w = jnp.where(in_seq, d_diff / _pool_count(i * ts, n), 0.0)
        d_u = _pool_diff_bwd(w, ts) - d_diff[:ts]

        da_ref[...] = jnp.concatenate([d_u, d_gp, d_a, d_b, d_gc], axis=-1).astype(da_ref.dtype)

        @pl.when(i == 0)
        def _():
            gwp_ref[...] = jnp.zeros_like(gwp_ref)
            gpw_ref[...] = jnp.zeros_like(gpw_ref)
            gdw_ref[...] = jnp.zeros_like(gdw_ref)
            gvec_ref[...] = jnp.zeros_like(gvec_ref)

        gwp_ref[...] += _mm(diff, d_raw[:ts], _TN)
        gpw_ref[...] += _mm(sw[:ts], d_yc[:ts], _TN)
        gdw_ref[...] += jnp.sum(gdw_acc_ref[...], axis=1)
        zero_row = jnp.zeros((1, 256), F32)
        gvec_ref[...] += jnp.concatenate([
            jnp.sum(d_yp[:ts] * raw, axis=0, keepdims=True),
            jnp.sum(d_cv_t, axis=0, keepdims=True),
            jnp.sum((d_z * nrm)[:ts], axis=0, keepdims=True),
            jnp.sum(d_z[:ts], axis=0, keepdims=True),
            zero_row, zero_row, zero_row, zero_row], axis=0)

    nxt_halo = lambda i: (jnp.minimum((i + 1) * hb, last_halo), 0)
    res = pl.pallas_call(
        body, name="poolconv_bwd_carrier" if exch.n else "poolconv_bwd", grid=(nt,),
        in_specs=[pl.BlockSpec((ts, HALF_IN), lambda i: (i, 0)), pl.BlockSpec((HALO, HALF_IN), nxt_halo),
                  pl.BlockSpec((ts, 256), lambda i: (i, 0)), pl.BlockSpec((HALO, 256), nxt_halo),
                  pl.BlockSpec((ts, 256), lambda i: (i, 0)),
                  pl.BlockSpec((ts, 512), lambda i: (i, 0)), pl.BlockSpec((HALO, 512), nxt_halo),
                  _full((256, 256)), scale[1], _full((CONV_TAPS_PAD, 256)), lng[1], lnb[1], _full((256, 256))]
        + [_ANY] * exch.n,
        out_specs=[pl.BlockSpec((ts, HALF_IN), lambda i: (i, 0)), _full((256, 256)), _full((256, 256)),
                   _full((CONV_TAPS_PAD, 256)), _full((8, 256))] + [_ANY] * exch.n,
        out_shape=[jax.ShapeDtypeStruct((s_len, HALF_IN), MXU_DTYPE), jax.ShapeDtypeStruct((256, 256), F32),
                   jax.ShapeDtypeStruct((256, 256), F32), jax.ShapeDtypeStruct((CONV_TAPS_PAD, 256), F32),
                   jax.ShapeDtypeStruct((8, 256), F32)] + exch.out_shapes(),
        scratch_shapes=[pltpu.VMEM((8, n, CONV_WIDTH), F32), pltpu.VMEM((ts, CONV_WIDTH), F32),
                        pltpu.VMEM((ts, CONV_WIDTH), F32), pltpu.VMEM((CONV_TAPS_PAD, 8, CONV_WIDTH), F32)]
        + exch.scratch(),
        compiler_params=_params(1),
    )(proj, proj, cv, cv, diff, dy, dy, wp, scale[0], dw, lng[0], lnb[0], pw, *exch.sources)
    return res[:5], res[5:]


DQ0, DKC0, DVC0, DKP0, DVP0, DGA0, DATTN_W = 0, 512, 640, 768, 896, 1024, 1536


def _attn_bwd(proj, ya, dy, sinks, carried=()):
    s_len = proj.shape[0]
    qb = _attn_blocks_per_step(s_len)
    ts = qb * BLOCK
    nt = s_len // ts
    exch = _Exchange(carried)

    def body(*refs):
        ins, (o_ref, gs_ref), (bias_ref,), xrefs = _split_refs(refs, 5, 2, exch)
        p_ref, kvp_ref, ya_ref, dy_ref, sinks_ref = ins
        i = pl.program_id(0)
        exch.run(xrefs, i == 0, i == nt - 1)

        @pl.when(i == 0)
        def _():
            _fill_attn_bias(bias_ref)
            gs_ref[...] = jnp.zeros_like(gs_ref)

        lo = lax.broadcasted_iota(jnp.int32, (1, 2 * HEAD_DIM), 1) < HEAD_DIM
        probs = [(b, kh) for b in range(qb) for kh in range(N_KV_HEADS)]
        kv_heads = range(N_KV_HEADS)
        scores, k_bds, v_bds, q2s, do2s, dyas, gas, sgas = {}, {}, {}, {}, {}, {}, {}, {}
        for b in range(qb):
            r0 = BLOCK * b
            kv_prev = kvp_ref[...] if b == 0 else p_ref[r0 - BLOCK:r0, 512:768]
            kv2 = jnp.concatenate([kv_prev, p_ref[r0:r0 + BLOCK, 512:768]], axis=0)
            k2, v2 = kv2[:, :BLOCK], kv2[:, BLOCK:]
            k2_swapped, v2_swapped = pltpu.roll(k2, HEAD_DIM, 1), pltpu.roll(v2, HEAD_DIM, 1)
            variant = jnp.where(i == 0, 0, 1) if b == 0 else 1
            q = p_ref[r0:r0 + BLOCK, 0:512]
            gas[b] = p_ref[r0:r0 + BLOCK, 768:1280]
            dyas[b] = dy_ref[r0:r0 + BLOCK, :]
            sgas[b] = _sig(gas[b])
            d_o = dyas[b] * (gas[b] * sgas[b])
            for kh in kv_heads:
                k_bds[b, kh] = _pair_block_matrix(k2, k2_swapped, kh)
                v_bds[b, kh] = _pair_block_matrix(v2, v2_swapped, kh)
                q2s[b, kh] = _pair_queries(q, kh)
                do2s[b, kh] = _pair_stack(d_o, kh)
                scores[b, kh] = _mm(q2s[b, kh], k_bds[b, kh], _NT) + bias_ref[variant, kh]
        ps, p_sinks, dps, dss, dqs, dks, dvs = {}, {}, {}, {}, {}, {}, {}
        d_sinks = [None] * N_Q_HEADS

        def softmax_stage(pr):
            ps[pr], p_sinks[pr] = _pair_softmax(scores[pr], pr[1], sinks_ref)

        def value_stage(pr):
            dps[pr] = _mm(do2s[pr], v_bds[pr], _NT)

        def score_grad_stage(pr):
            p, dp, kh = ps[pr], dps[pr], pr[1]
            ds_halves = []
            for a in range(2):
                cols = slice(2 * BLOCK * a, 2 * BLOCK * (a + 1))
                delta = jnp.sum(p[:, cols] * dp[:, cols], axis=-1, keepdims=True)
                ds_halves.append(p[:, cols] * (dp[:, cols] - delta))
                dsink = -p_sinks[pr][a] * delta
                for j in range(2):
                    part = jnp.sum(dsink[BLOCK * j: BLOCK * (j + 1)], axis=0, keepdims=True)
                    h = Q_PER_KV * kh + 2 * j + a
                    d_sinks[h] = part if d_sinks[h] is None else d_sinks[h] + part
            dss[pr] = jnp.concatenate(ds_halves, axis=-1)

        def operand_grad_stage(pr):
            dqs[pr] = _mm(dss[pr], k_bds[pr]) * SCALE
            dks[pr] = _fold_pair_halves(_mm(dss[pr], q2s[pr], _TN))
            dvs[pr] = _fold_pair_halves(_mm(ps[pr], do2s[pr], _TN))

        for stage in (softmax_stage, value_stage, score_grad_stage, operand_grad_stage):
            for pr in probs:
                stage(pr)
        for b in range(qb):
            r0 = BLOCK * b
            dk = jnp.where(lo, dks[b, 0], dks[b, 1])
            dv = jnp.where(lo, dvs[b, 0], dvs[b, 1])
            d_ga = dyas[b] * ya_ref[r0:r0 + BLOCK, :] * _dsilu(gas[b], sgas[b])
            o_ref[r0:r0 + BLOCK, :] = jnp.concatenate(
                [_pair_unstack([dqs[b, kh] for kh in kv_heads]), dk[BLOCK:], dv[BLOCK:], dk[:BLOCK], dv[:BLOCK],
                 d_ga], axis=-1)
        gs_ref[...] += jnp.broadcast_to(jnp.concatenate(d_sinks, axis=0), gs_ref.shape)

    res = pl.pallas_call(
        body, name="attn_bwd_carrier" if exch.n else "attn_bwd", grid=(nt,),
        in_specs=[pl.BlockSpec((ts, HALF_IN), lambda i: (i, 1)),
                  pl.BlockSpec((BLOCK, 256), lambda i: (jnp.maximum(i * qb - 1, 0), 7)),
                  pl.BlockSpec((ts, 512), lambda i: (i, 0)), pl.BlockSpec((ts, 512), lambda i: (i, 1)),
                  pl.BlockSpec(memory_space=pltpu.SMEM)] + [_ANY] * exch.n,
        out_specs=[pl.BlockSpec((ts, DATTN_W), lambda i: (i, 0)), _full((N_Q_HEADS, 128))] + [_ANY] * exch.n,
        out_shape=[jax.ShapeDtypeStruct((s_len, DATTN_W), F32), jax.ShapeDtypeStruct((N_Q_HEADS, 128), F32)]
        + exch.out_shapes(),
        scratch_shapes=[pltpu.VMEM((2, N_KV_HEADS, PAIR_ROWS, PAIR_COLS), F32)] + exch.scratch(),
        compiler_params=_params(1),
    )(proj, proj, ya, dy, sinks, *exch.sources)
    return res[:2], res[2:]


def _in_bwd(da, dattn, x, dxo, g, w_t):
    s_len, d = x.shape
    ts = _row_tile(s_len, IN_BWD_TILE)
    bpt = ts // BLOCK
    nt = s_len // ts
    last_block = s_len // BLOCK - 1

    def body(da_ref, dat_ref, nxt_ref, x_ref, dxo_ref, g_ref, w_ref, dx_ref, dg_ref, gw_ref, acc_ref, stage_ref,
             stage_sem):
        i = pl.program_id(0)
        dat = dat_ref[...]
        nxt = jnp.where(i < nt - 1, nxt_ref[...], 0.0)
        shifted = jnp.concatenate([dat[BLOCK:, DKP0:DGA0], nxt], axis=0) if bpt > 1 else nxt
        dkv = dat[:, DKC0:DKP0] + shifted
        dproj = jnp.concatenate([da_ref[...], dat[:, DQ0:DKC0].astype(MXU_DTYPE), dkv.astype(MXU_DTYPE),
                                 dat[:, DGA0:DATTN_W].astype(MXU_DTYPE)], axis=-1)
        d_h = _mm(dproj, w_ref[...])
        xv = x_ref[...]
        gv = g_ref[...]
        r = lax.rsqrt(jnp.mean(xv * xv, axis=-1, keepdims=True) + EPS)
        xr = xv * r
        w = d_h * gv
        dx_ref[...] = dxo_ref[...] + r * (w - xr * jnp.mean(w * xr, axis=-1, keepdims=True))

        @pl.when(i == 0)
        def _():
            dg_ref[...] = jnp.zeros_like(dg_ref)
            acc_ref[...] = jnp.zeros_like(acc_ref)

        dg_ref[...] += jnp.sum(d_h * xr, axis=0, keepdims=True)
        acc_ref[...] += _mm(dproj, xr * gv, _TN)

        @pl.when(i == nt - 1)
        def _():
            stage_ref[...] = acc_ref[...].astype(stage_ref.dtype)
            out = pltpu.make_async_copy(stage_ref, gw_ref, stage_sem)
            out.start()
            out.wait()

    return pl.pallas_call(
        body, name="in_bwd", grid=(nt,),
        in_specs=[pl.BlockSpec((ts, HALF_IN), lambda i: (i, 0)),
                  pl.BlockSpec((ts, DATTN_W), lambda i: (i, 0)),
                  pl.BlockSpec((BLOCK, 256), lambda i: (jnp.minimum((i + 1) * bpt, last_block), 3)),
                  pl.BlockSpec((ts, d), lambda i: (i, 0)), pl.BlockSpec((ts, d), lambda i: (i, 0)),
                  g[1], pl.BlockSpec((D_IN, d), lambda i: (0, 0), pipeline_mode=pl.Buffered(1))],
        out_specs=[pl.BlockSpec((ts, d), lambda i: (i, 0)), _full((1, d)), _ANY],
        out_shape=[jax.ShapeDtypeStruct((s_len, d), F32), jax.ShapeDtypeStruct((1, d), F32),
                   jax.ShapeDtypeStruct((D_IN, d), EXCHANGE_DTYPE)],
        scratch_shapes=[pltpu.VMEM((D_IN, d), F32), pltpu.VMEM((D_IN, d), EXCHANGE_DTYPE), pltpu.SemaphoreType.DMA],
        compiler_params=_params(1),
    )(da, dattn, dattn, x, dxo, g[0], w_t)


def _in_bwd_dw(da, dattn, x, g):
    s_len, d = x.shape
    ts = _row_tile(s_len, IN_BWD_TILE)
    bpt = ts // BLOCK
    nt = s_len // ts
    last_block = s_len // BLOCK - 1

    def body(da_ref, dat_ref, nxt_ref, x_ref, g_ref, dp_ref, gw_ref, acc_ref, stage_ref, stage_sem):
        i = pl.program_id(0)
        dat = dat_ref[...]
        nxt = jnp.where(i < nt - 1, nxt_ref[...], 0.0)
        shifted = jnp.concatenate([dat[BLOCK:, DKP0:DGA0], nxt], axis=0) if bpt > 1 else nxt
        dkv = dat[:, DKC0:DKP0] + shifted
        dproj = jnp.concatenate([da_ref[...], dat[:, DQ0:DKC0].astype(MXU_DTYPE), dkv.astype(MXU_DTYPE),
                                 dat[:, DGA0:DATTN_W].astype(MXU_DTYPE)], axis=-1)
        dp_ref[...] = dproj
        xv = x_ref[...]
        r = lax.rsqrt(jnp.mean(xv * xv, axis=-1, keepdims=True) + EPS)

        @pl.when(i == 0)
        def _():
            acc_ref[...] = jnp.zeros_like(acc_ref)

        acc_ref[...] += _mm(dproj, xv * r * g_ref[...], _TN)

        @pl.when(i == nt - 1)
        def _():
            stage_ref[...] = acc_ref[...].astype(stage_ref.dtype)
            out = pltpu.make_async_copy(stage_ref, gw_ref, stage_sem)
            out.start()
            out.wait()

    return pl.pallas_call(
        body, name="in_bwd_dw", grid=(nt,),
        in_specs=[pl.BlockSpec((ts, HALF_IN), lambda i: (i, 0)),
                  pl.BlockSpec((ts, DATTN_W), lambda i: (i, 0)),
                  pl.BlockSpec((BLOCK, 256), lambda i: (jnp.minimum((i + 1) * bpt, last_block), 3)),
                  pl.BlockSpec((ts, d), lambda i: (i, 0)), g[1]],
        out_specs=[pl.BlockSpec((ts, D_IN), lambda i: (i, 0)), _ANY],
        out_shape=[jax.ShapeDtypeStruct((s_len, D_IN), MXU_DTYPE), jax.ShapeDtypeStruct((D_IN, d), EXCHANGE_DTYPE)],
        scratch_shapes=[pltpu.VMEM((D_IN, d), F32), pltpu.VMEM((D_IN, d), EXCHANGE_DTYPE), pltpu.SemaphoreType.DMA],
        compiler_params=_params(1),
    )(da, dattn, dattn, x, g[0])


def _in_bwd_dx(dproj, x, dxo, g, w_t, carried=()):
    s_len, d = x.shape
    ts = _row_tile(s_len, ROW_TILE)
    nt = s_len // ts
    exch = _Exchange(carried)

    def body(*refs):
        (dp_ref, x_ref, dxo_ref, g_ref, w_ref), (dx_ref, dg_ref), _, xrefs = _split_refs(refs, 5, 2, exch)
        i = pl.program_id(0)
        exch.run(xrefs, i == 0, i == nt - 1)
        d_h = _mm(dp_ref[...], w_ref[...])
        xv = x_ref[...]
        r = lax.rsqrt(jnp.mean(xv * xv, axis=-1, keepdims=True) + EPS)
        xr = xv * r
        w = d_h * g_ref[...]
        dx_ref[...] = dxo_ref[...] + r * (w - xr * jnp.mean(w * xr, axis=-1, keepdims=True))

        @pl.when(i == 0)
        def _():
            dg_ref[...] = jnp.zeros_like(dg_ref)

        dg_ref[...] += jnp.sum(d_h * xr, axis=0, keepdims=True)

    tile = pl.BlockSpec((ts, d), lambda i: (i, 0))
    res = pl.pallas_call(
        body, name="in_bwd_dx_carrier" if exch.n else "in_bwd_dx", grid=(nt,),
        in_specs=[pl.BlockSpec((ts, D_IN), lambda i: (i, 0)), tile, tile, g[1], _full((D_IN, d))]
        + [_ANY] * exch.n,
        out_specs=[tile, _full((1, d))] + [_ANY] * exch.n,
        out_shape=[jax.ShapeDtypeStruct((s_len, d), F32), jax.ShapeDtypeStruct((1, d), F32)] + exch.out_shapes(),
        scratch_shapes=exch.scratch(),
        compiler_params=_params(1),
    )(dproj, x, dxo, g[0], w_t, *exch.sources)
    return res[:2], res[2:]


def _sum_partials(p_ref):
    g = p_ref[0].astype(F32)
    for k in range(1, N_DEV):
        g = g + p_ref[k].astype(F32)
    return g


def _adamw_step(g, w, m, v):
    nm = ADAM_B1 * m + (1.0 - ADAM_B1) * g
    nv = ADAM_B2 * v + (1.0 - ADAM_B2) * (g * g)
    m_hat = nm / (1.0 - ADAM_B1 ** ADAM_STEP)
    v_hat = nv / (1.0 - ADAM_B2 ** ADAM_STEP)
    return -ADAM_LR * (m_hat / (jnp.sqrt(v_hat) + ADAM_EPS) + ADAM_WD * w), nm, nv


def _adamw_layers(parts, w, m, v, name):
    _, rows, n = w.shape
    tr = _row_tile(rows, 64)

    def body(*refs):
        p_refs = refs[:DEPTH]
        w_ref, m_ref, v_ref, g_ref, d_ref, nm_ref, nv_ref = refs[DEPTH:]
        for l in range(DEPTH):
            g = _sum_partials(p_refs[l])
            g_ref[l] = g
            d_ref[l], nm_ref[l], nv_ref[l] = _adamw_step(g, w_ref[l], m_ref[l], v_ref[l])

    tile = pl.BlockSpec((DEPTH, tr, n), lambda i: (0, i, 0))
    shape = jax.ShapeDtypeStruct(w.shape, F32)
    return pl.pallas_call(
        body, name=name, grid=(rows // tr,),
        in_specs=[pl.BlockSpec((N_DEV, tr, n), lambda i: (0, i, 0))] * DEPTH + [tile] * 3,
        out_specs=[tile] * 4, out_shape=[shape] * 4,
        compiler_params=_params(1),
    )(*parts, w, m, v)


def _sum_rows(parts):
    def body(p_ref, o_ref):
        o_ref[...] = _sum_partials(p_ref)

    vmem = pl.BlockSpec(memory_space=pltpu.VMEM)
    return pl.pallas_call(body, name="sum_replicated_partials", in_specs=[vmem], out_specs=vmem,
                          out_shape=jax.ShapeDtypeStruct(parts.shape[1:], F32))(parts)


def _adamw_tensors(gs, ws, ms, vs):
    n = len(ws)
    shapes = [w.shape for w in ws]
    two_d = lambda a: a.reshape(1, -1) if a.ndim == 1 else a
    ops = [two_d(a) for a in list(gs) + list(ws) + list(ms) + list(vs)]

    def body(*refs):
        ins, outs = refs[:4 * n], refs[4 * n:]
        for k in range(n):
            g, w, m, v = (ins[j * n + k][...] for j in range(4))
            outs[k][...], outs[n + k][...], outs[2 * n + k][...] = _adamw_step(g, w, m, v)

    vmem = pl.BlockSpec(memory_space=pltpu.VMEM)
    res = pl.pallas_call(
        body, name="adamw_replicated", in_specs=[vmem] * (4 * n), out_specs=[vmem] * (3 * n),
        out_shape=[jax.ShapeDtypeStruct(o.shape, F32) for o in ops[n:2 * n]] * 3,
    )(*ops)
    return [[res[j * n + k].reshape(shapes[k]) for k in range(n)] for j in range(3)]


def _pad_rows(a, mult):
    pad = (-a.shape[0]) % mult
    return a if pad == 0 else jnp.concatenate([a, jnp.zeros((pad, a.shape[1]), a.dtype)], axis=0)


def _dw_rows(conv_dw_l):
    return jnp.pad(jnp.swapaxes(conv_dw_l, 0, 1), ((0, 0), (0, CONV_TAPS_PAD - CONV_KERNEL)))


def _pack_small(pw_l, dw_l, d):
    rows = jnp.concatenate([pw_l.reshape(-1, d), _dw_rows(dw_l).reshape(-1, d)], axis=0)
    return _pad_rows(rows, 8)


def _small_slabs(g_pw, g_dw, d):
    a = g_pw.reshape(N_DEV, -1, d)
    b = jnp.swapaxes(g_dw, 0, 1).reshape(N_DEV, -1, d)
    used = a.shape[1] + b.shape[1]
    return jnp.concatenate([a, b, jnp.zeros((N_DEV, (-used) % 8, d), g_pw.dtype)], axis=1)


def _unpack_small(rows, d):
    c = CONV_WIDTH // N_DEV
    n_pw = c * CONV_WIDTH // d
    n_dw = c * CONV_TAPS_PAD // d
    pw = rows[:n_pw].reshape(c, CONV_WIDTH)
    dw = jnp.swapaxes(rows[n_pw:n_pw + n_dw].reshape(c, CONV_TAPS_PAD), 0, 1)[:CONV_KERNEL]
    return pw, dw


def _pack_replicated(ln_g, pool_w, pool_scale, conv_b, conv_ln_g, conv_ln_b, attn_sinks, final_g, scalar, d):
    sinks = jnp.pad(attn_sinks, ((0, 0), (0, 256 - N_Q_HEADS)))
    small = jnp.concatenate([pool_scale, conv_b, conv_ln_g, conv_ln_b, sinks], axis=0)
    small = _pad_rows(small, d // 256)
    last = jnp.pad(scalar.reshape(1, 1), ((0, 0), (0, d - 1)))
    return _pad_rows(jnp.concatenate([ln_g.reshape(-1, d), final_g.reshape(-1, d), pool_w.reshape(-1, d),
                                      small.reshape(-1, d), last], axis=0), 8)


def _unpack_replicated(rows, d):
    n_pool = DEPTH * 4 * POOL_GROUP * POOL_GROUP // d
    n_small = -(-5 * DEPTH * 256 // d)
    ln_g = rows[:DEPTH]
    final_g = rows[DEPTH]
    pool_w = rows[DEPTH + 1: DEPTH + 1 + n_pool].reshape(DEPTH, 4, POOL_GROUP, POOL_GROUP)
    small = rows[DEPTH + 1 + n_pool: DEPTH + 1 + n_pool + n_small].reshape(-1, 256)[: 5 * DEPTH]
    pool_scale, conv_b, conv_ln_g, conv_ln_b = (small[DEPTH * k: DEPTH * (k + 1)] for k in range(4))
    sinks = small[4 * DEPTH: 5 * DEPTH, :N_Q_HEADS]
    scalar = rows[DEPTH + 1 + n_pool + n_small, 0]
    return ln_g, pool_w, pool_scale, conv_b, conv_ln_g, conv_ln_b, sinks, final_g, scalar


def _block_diag(pool_w):
    wide = jnp.tile(pool_w.reshape(POOL_WIDTH, POOL_GROUP), (1, POOL_WIDTH // POOL_GROUP))
    rows = lax.broadcasted_iota(jnp.int32, wide.shape, 0) // POOL_GROUP
    cols = lax.broadcasted_iota(jnp.int32, wide.shape, 1) // POOL_GROUP
    return jnp.where(rows == cols, wide, jnp.zeros_like(wide))


def _diag_blocks(mat):
    return jnp.stack([mat[POOL_GROUP * gi: POOL_GROUP * (gi + 1), POOL_GROUP * gi: POOL_GROUP * (gi + 1)]
                      for gi in range(4)], axis=0)


def kernel(x, ln_g, w_in, pool_w, pool_scale, conv_dw, conv_b, conv_ln_g, conv_ln_b, conv_pw, attn_sinks, w_out, final_g, loss_target, m_ln_g, m_w_in, m_pool_w, m_pool_scale, m_conv_dw, m_conv_b, m_conv_ln_g, m_conv_ln_b, m_conv_pw, m_attn_sinks, m_w_out, m_final_g, v_ln_g, v_w_in, v_pool_w, v_pool_scale, v_conv_dw, v_conv_b, v_conv_ln_g, v_conv_ln_b, v_conv_pw, v_attn_sinks, v_w_out, v_final_g):
    x0 = x[0]
    d = x0.shape[1]
    row = lambda a: a.reshape(1, -1)
    vec = _layer_vec
    slabs = lambda a: a.reshape(N_DEV, a.shape[0] // N_DEV, d)
    c_shard = CONV_WIDTH // N_DEV

    w_in_rows = [jnp.swapaxes(w_in[l], 0, 1).astype(MXU_DTYPE) for l in range(DEPTH)]
    w_out_rows = [w_out[l].astype(MXU_DTYPE) for l in range(DEPTH)]
    per_word = 4 // jnp.dtype(MXU_DTYPE).itemsize
    dw_t = jnp.stack([_dw_rows(conv_dw[l]) for l in range(DEPTH)], axis=0)
    dw_bits = (lax.bitcast_convert_type(dw_t, MXU_DTYPE) if per_word > 1 else dw_t).reshape(-1, d)
    n_pw = DEPTH * c_shard * CONV_WIDTH // d
    conv_rows = _pad_rows(jnp.concatenate([conv_pw.reshape(-1, d).astype(MXU_DTYPE), dw_bits], axis=0), 16)
    w_in_t, w_out_f = [None] * DEPTH, [None] * DEPTH
    w_in_t[0] = _all_gather(w_in_rows[0], "w_in_all_gather")
    wp_bd = [_block_diag(pool_w[l]).astype(MXU_DTYPE) for l in range(DEPTH)]

    xs, projs, cvs, ys = [x0], [], [], []
    q4 = D_IN // N_DEV // 4
    w_in_1 = [w_in_rows[1][q4 * k: q4 * (k + 1)] for k in range(4)]
    for l in range(DEPTH):
        if l == 0:
            proj, got = _in_proj(xs[0], vec(ln_g, 0), w_in_t[0], carried=[w_out_rows[0], conv_rows, w_in_1[0]])
            w_in_1_got = [got[2]]
            w_out_f[0] = got[0].reshape(D_MIX, d)
            pw_all = got[1][:, :n_pw].reshape(N_DEV, DEPTH, c_shard, CONV_WIDTH)
            pw_f = [pw_all[:, k].reshape(CONV_WIDTH, CONV_WIDTH) for k in range(DEPTH)]
            bits = got[1][:, n_pw:n_pw + dw_bits.shape[0]].reshape(
                (N_DEV, DEPTH, c_shard, CONV_TAPS_PAD) + (per_word,) * (per_word > 1))
            dw_all = lax.bitcast_convert_type(bits, F32) if per_word > 1 else bits
            dw_f = [jnp.swapaxes(dw_all[:, k].reshape(CONV_WIDTH, CONV_TAPS_PAD), 0, 1) for k in range(DEPTH)]
        (y_pc, cv, diff), got = _poolconv_fwd(proj, wp_bd[l], vec(pool_scale, l), dw_f[l], vec(conv_b, l),
                                              vec(conv_ln_g, l), vec(conv_ln_b, l), pw_f[l],
                                              carried=[w_in_1[1], w_in_1[2]] if l == 0 else [])
        if l == 0:
            w_in_1_got += list(got)
        (y_at, ya), got = _attn_fwd(proj, attn_sinks[l], carried=[w_out_rows[1], w_in_1[3]] if l == 0 else [])
        if l == 0:
            w_out_f[1] = got[0].reshape(D_MIX, d)
            w_in_t[1] = jnp.concatenate(w_in_1_got + [got[1]], axis=1).reshape(D_IN, d)
        projs.append(proj)
        cvs.append((cv, diff))
        ys.append((y_pc, y_at, ya))
        if l < DEPTH - 1:
            x_next, proj = _out_in_proj(xs[l], y_pc, y_at, w_out_f[l], vec(ln_g, l + 1), w_in_t[l + 1])
            xs.append(x_next)

    l = DEPTH - 1
    sq, g_final, dx = _out_proj_loss(xs[l], ys[l][0], ys[l][1], w_out_f[l], loss_target[0], row(final_g))

    l = 1
    dy, g_wout1 = _out_bwd(dx, ys[l][0], ys[l][1], w_out_f[l])
    (da, g_wp1, g_pw1, g_dw1, g_vec1), _ = _poolconv_bwd(
        projs[l], *cvs[l], dy, wp_bd[l], vec(pool_scale, l), dw_f[l], vec(conv_ln_g, l), vec(conv_ln_b, l), pw_f[l])
    (dattn, gs1), _ = _attn_bwd(projs[l], ys[l][2], dy, attn_sinks[l])
    dx, g_ln1, g_win_t1 = _in_bwd(da, dattn, xs[l], dx, vec(ln_g, l), w_in_t[l])
    l = 0
    dy, g_wout0 = _out_bwd(dx, ys[l][0], ys[l][1], w_out_f[l])
    (da, g_wp0, g_pw0, g_dw0, g_vec0), (r_win1, r_wout1, r_small1) = _poolconv_bwd(
        projs[l], *cvs[l], dy, wp_bd[l], vec(pool_scale, l), dw_f[l], vec(conv_ln_g, l), vec(conv_ln_b, l), pw_f[l],
        carried=[slabs(g_win_t1), slabs(g_wout1), _small_slabs(g_pw1, g_dw1, d)])
    (dattn, gs0), (r_wout0, r_small0) = _attn_bwd(projs[l], ys[l][2], dy, attn_sinks[l],
                                                  carried=[slabs(g_wout0), _small_slabs(g_pw0, g_dw0, d)])
    dproj, g_win_t0 = _in_bwd_dw(da, dattn, xs[l], vec(ln_g, l))
    (dx, g_ln0), (r_win0,) = _in_bwd_dx(dproj, xs[l], dx, vec(ln_g, l), w_in_t[l], carried=[slabs(g_win_t0)])
    grad_x = dx[None]

    gv = jnp.stack([g_vec0, g_vec1], axis=0)
    rep_part = _pack_replicated(
        jnp.concatenate([g_ln0, g_ln1], axis=0), jnp.stack([_diag_blocks(g_wp0), _diag_blocks(g_wp1)], axis=0),
        gv[:, 0], gv[:, 1], gv[:, 2], gv[:, 3], jnp.stack([gs0[:, 0], gs1[:, 0]], axis=0), g_final, sq[0, 0], d)
    (r_rep,) = _final_exchange([rep_part])

    t = lambda a: jnp.swapaxes(a, 1, 2)
    win = [t(o) for o in _adamw_layers([r_win0, r_win1], t(w_in), t(m_w_in), t(v_w_in), "adamw_w_in")]
    wout = _adamw_layers([r_wout0, r_wout1], w_out, m_w_out, v_w_out, "adamw_w_out")
    pack_s = lambda pw_, dw_: jnp.stack([_pack_small(pw_[l], dw_[l], d) for l in range(DEPTH)], axis=0)
    small = _adamw_layers([r_small0, r_small1], pack_s(conv_pw, conv_dw), pack_s(m_conv_pw, m_conv_dw),
                          pack_s(v_conv_pw, v_conv_dw), "adamw_conv")
    small = [[_unpack_small(o[l], d) for l in range(DEPTH)] for o in small]
    *rep_grads, sq_sum = _unpack_replicated(_sum_rows(r_rep), d)
    loss = 0.5 / d * sq_sum
    rep_w = [ln_g, pool_w, pool_scale, conv_b, conv_ln_g, conv_ln_b, attn_sinks, final_g]
    rep_m = [m_ln_g, m_pool_w, m_pool_scale, m_conv_b, m_conv_ln_g, m_conv_ln_b, m_attn_sinks, m_final_g]
    rep_v = [v_ln_g, v_pool_w, v_pool_scale, v_conv_b, v_conv_ln_g, v_conv_ln_b, v_attn_sinks, v_final_g]
    rep = [rep_grads] + _adamw_tensors(rep_grads, rep_w, rep_m, rep_v)

    outs = []
    for k in range(4):
        r_ln, r_pool, r_scale, r_cb, r_lng, r_lnb, r_sinks, r_final = rep[k]
        s_pw = jnp.stack([small[k][l][0] for l in range(DEPTH)], axis=0)
        s_dw = jnp.stack([small[k][l][1] for l in range(DEPTH)], axis=0)
        outs += [r_ln, win[k], r_pool, r_scale, s_dw, r_cb, r_lng, r_lnb, s_pw, r_sinks, wout[k], r_final]
    return (loss, grad_x, *outs)
```

```python
import jax
import jax.numpy as jnp
from jax import lax
from jax.experimental import pallas as pl
from jax.experimental.pallas import tpu as pltpu

F32 = jnp.float32
MXU_DTYPE = jnp.bfloat16
EXCHANGE_DTYPE = jnp.bfloat16

N_DEV = 8
DEPTH = 2
POOL_WIDTH = 256
POOL_GROUP = 64
CONV_WIDTH = 256
CONV_KERNEL = 31
CONV_TAPS_PAD = 32
HEAD_DIM = 64
N_KV_HEADS = 2
Q_PER_KV = 4
N_Q_HEADS = 8
ATTN_WIDTH = 512
BLOCK = 128
D_MIX = 1024
D_IN = 2560
HALF_IN = 1280
EPS = 1e-6
SCALE = HEAD_DIM ** -0.5
NEG = -1e30

ADAM_LR = 0.001
ADAM_B1 = 0.9
ADAM_B2 = 0.999
ADAM_EPS = 1e-08
ADAM_WD = 0.01
ADAM_STEP = 10

HALO = 32
ROW_TILE = 512
IN_BWD_TILE = 512
VMEM_LIMIT = 56 * 1024 * 1024

_NN = (((1,), (0,)), ((), ()))
_NT = (((1,), (1,)), ((), ()))
_TN = (((0,), (0,)), ((), ()))
_ANY = pl.BlockSpec(memory_space=pl.ANY)


def _mm(a, b, dims=_NN):
    return lax.dot_general(a.astype(MXU_DTYPE), b.astype(MXU_DTYPE), dims, preferred_element_type=F32)


def _sig(x):
    return 1.0 / (1.0 + jnp.exp(-x))


def _dsilu(z, s):
    return s * (1.0 + z * (1.0 - s))


def _params(n_grid):
    return pltpu.CompilerParams(dimension_semantics=("arbitrary",) * n_grid, vmem_limit_bytes=VMEM_LIMIT)


def _row_tile(rows, cap):
    t = min(rows, cap)
    while rows % t or t % 8:
        t -= 8
    return t


def _full(shape):
    return pl.BlockSpec(shape, lambda i: (0,) * len(shape))


def _layer_vec(stacked, layer):
    arr = stacked.reshape(stacked.shape[0], 1, stacked.shape[-1])
    return arr, pl.BlockSpec((None, 1, arr.shape[-1]), lambda i: (layer, 0, 0))


def _mesh_pos():
    return lax.axis_index("x"), lax.axis_index("y"), lax.axis_index("c")


class _Exchange:
    def __init__(self, sources):
        self.sources = list(sources)
        self.n = len(self.sources)
        self.gather = [s.ndim == 2 for s in self.sources]

    def out_shapes(self):
        return [jax.ShapeDtypeStruct((N_DEV,) + s.shape[-2:], s.dtype) for s in self.sources]

    def scratch(self):
        if not self.n:
            return []
        return [pltpu.SemaphoreType.DMA((7 * self.n,)), pltpu.SemaphoreType.DMA((7 * self.n,)),
                pltpu.SemaphoreType.DMA((self.n,))]

    def copies(self, src_refs, dst_refs, sems):
        send_sems, recv_sems, local_sems = sems
        x, y, c = _mesh_pos()
        me = 4 * x + 2 * y + c
        out = []
        for a, (src, dst) in enumerate(zip(src_refs, dst_refs)):
            out.append(pltpu.make_async_copy(src if self.gather[a] else src.at[me], dst.at[me], local_sems.at[a]))
            for k in range(1, N_DEV):
                tx, ty, tc = x ^ ((k >> 2) & 1), y ^ ((k >> 1) & 1), c ^ (k & 1)
                out.append(pltpu.make_async_remote_copy(
                    src_ref=src if self.gather[a] else src.at[4 * tx + 2 * ty + tc], dst_ref=dst.at[me],
                    send_sem=send_sems.at[7 * a + k - 1], recv_sem=recv_sems.at[7 * a + k - 1],
                    device_id=(tx, ty, tc), device_id_type=pl.DeviceIdType.MESH))
        return out

    def run(self, refs, first, last):
        if not self.n:
            return
        src_refs, dst_refs, sems = refs

        @pl.when(first)
        def _():
            for cp in self.copies(src_refs, dst_refs, sems):
                cp.start()

        @pl.when(last)
        def _():
            for cp in self.copies(src_refs, dst_refs, sems):
                cp.wait()


def _split_refs(refs, n_in, n_out, exch):
    ins = refs[:n_in]
    srcs = refs[n_in:n_in + exch.n]
    outs = refs[n_in + exch.n:n_in + exch.n + n_out]
    dsts = refs[n_in + exch.n + n_out:n_in + 2 * exch.n + n_out]
    rest = refs[n_in + 2 * exch.n + n_out:]
    sems = rest[len(rest) - 3:] if exch.n else ()
    scratch = rest[:len(rest) - 3] if exch.n else rest
    return ins, outs, scratch, (srcs, dsts, sems)


def _final_exchange(sources):
    exch = _Exchange(sources)

    def body(*refs):
        _, _, _, xrefs = _split_refs(refs, 0, 0, exch)
        for cp in exch.copies(*xrefs):
            cp.start()
        for cp in exch.copies(*xrefs):
            cp.wait()

    return pl.pallas_call(
        body, name="final_exchange", out_shape=exch.out_shapes(),
        in_specs=[_ANY] * exch.n, out_specs=[_ANY] * exch.n, scratch_shapes=exch.scratch(),
    )(*exch.sources)


def _all_gather(shard, name):
    m_per, n = shard.shape

    def body(x_ref, out_ref, send_sems, recv_sems, local_sem):
        x, y, c = _mesh_pos()
        me, sibling = (x, y, c), (x, y, 1 - c)
        chips = [(1 - x, y), (x, 1 - y), (1 - x, 1 - y)]

        def rows(px, py, pc):
            return out_ref.at[pl.ds((4 * px + 2 * py + pc) * m_per, m_per), :]

        def copy(k, block, to, src=None):
            return pltpu.make_async_remote_copy(
                src_ref=rows(*block) if src is None else src, dst_ref=rows(*block),
                send_sem=send_sems.at[k], recv_sem=recv_sems.at[k],
                device_id=to, device_id_type=pl.DeviceIdType.MESH)

        mine = pltpu.make_async_copy(x_ref, rows(*me), local_sem)
        mine.start()
        first = [copy(0, me, sibling, src=x_ref)]
        first += [copy(1 + j, me, (*chip, c), src=x_ref) for j, chip in enumerate(chips)]
        for cp in first:
            cp.start()
        passed = [copy(4 + j, (*chip, c), sibling) for j, chip in enumerate(chips)]
        for j, chip in enumerate(chips):
            copy(1 + j, (*chip, c), me).wait_recv()
            passed[j].start()
        copy(0, sibling, me).wait_recv()
        for j, chip in enumerate(chips):
            copy(4 + j, (*chip, 1 - c), me).wait_recv()
        for cp in first + passed:
            cp.wait_send()
        mine.wait()

    return pl.pallas_call(
        body, name=name,
        out_shape=jax.ShapeDtypeStruct((N_DEV * m_per, n), shard.dtype),
        in_specs=[pl.BlockSpec(memory_space=pltpu.VMEM)],
        out_specs=pl.BlockSpec(memory_space=pltpu.VMEM),
        scratch_shapes=[pltpu.SemaphoreType.DMA((7,)), pltpu.SemaphoreType.DMA((7,)), pltpu.SemaphoreType.DMA],
        compiler_params=pltpu.CompilerParams(vmem_limit_bytes=VMEM_LIMIT),
    )(shard)


def _by_group(lane, v2, v4, v8, v16):
    return jnp.where(lane < 64, v2, jnp.where(lane < 128, v4, jnp.where(lane < 192, v8, v16)))


def _pool_count(t0, n):
    lane = lax.broadcasted_iota(jnp.int32, (1, POOL_WIDTH), 1)
    t = (t0 + lax.broadcasted_iota(jnp.int32, (n, 1), 0)).astype(F32)
    wnd = _by_group(lane, 2.0, 4.0, 8.0, 16.0)
    return jnp.minimum(t + 1.0, wnd)


def _pool_diff(u_ext, t0, ts):
    lane = lax.broadcasted_iota(jnp.int32, (1, POOL_WIDTH), 1)
    s2 = u_ext + pltpu.roll(u_ext, 1, 0)
    s4 = s2 + pltpu.roll(s2, 2, 0)
    s8 = s4 + pltpu.roll(s4, 4, 0)
    s16 = s8 + pltpu.roll(s8, 8, 0)
    pooled = _by_group(lane, s2, s4, s8, s16)[HALO:]
    return pooled / _pool_count(t0, ts) - u_ext[HALO:]


def _pool_diff_bwd(w, ts):
    n = w.shape[0]
    lane = lax.broadcasted_iota(jnp.int32, (1, POOL_WIDTH), 1)
    f2 = w + pltpu.roll(w, n - 1, 0)
    f4 = f2 + pltpu.roll(f2, n - 2, 0)
    f8 = f4 + pltpu.roll(f4, n - 4, 0)
    f16 = f8 + pltpu.roll(f8, n - 8, 0)
    return _by_group(lane, f2, f4, f8, f16)[:ts]


CONV_CHUNK = 64


def _conv_taps():
    return [(8 * m + r, r, m) for r in range(8) for m in range(4) if 8 * m + r < CONV_KERNEL]


def _store_shifted(dst_ref, x, up):
    n = x.shape[0]
    for r in range(8):
        dst_ref[r] = x if r == 0 else pltpu.roll(x, n - r if up else r, 0)


def _anticausal_conv(src_ref, dw_ref, out_ref, n_out):
    def chunk(c, carry):
        t0 = pl.multiple_of(c * CONV_CHUNK, CONV_CHUNK)
        acc = None
        for d, r, m in _conv_taps():
            term = dw_ref[pl.ds(CONV_KERNEL - 1 - d, 1), :] * src_ref[r, pl.ds(t0 + 8 * m, CONV_CHUNK), :]
            acc = term if acc is None else acc + term
        out_ref[pl.ds(t0, CONV_CHUNK), :] = acc
        return carry

    lax.fori_loop(0, n_out // CONV_CHUNK, chunk, 0)


def _depthwise_conv_weight_grad(x_ref, dout_up_ref, acc_ref, n_rows):
    acc_ref[...] = jnp.zeros_like(acc_ref)

    def chunk(c, carry):
        t0 = pl.multiple_of(c * CONV_CHUNK, CONV_CHUNK)
        xv = x_ref[pl.ds(t0, CONV_CHUNK), :]
        for d, r, m in _conv_taps():
            prod = xv * dout_up_ref[r, pl.ds(t0 + 8 * m, CONV_CHUNK), :]
            acc_ref[CONV_KERNEL - 1 - d] += jnp.sum(prod.reshape(CONV_CHUNK // 8, 8, prod.shape[-1]), axis=0)
        return carry

    lax.fori_loop(0, n_rows // CONV_CHUNK, chunk, 0)


def _layer_norm(cv):
    mu = jnp.mean(cv, axis=-1, keepdims=True)
    xc = cv - mu
    var = jnp.mean(xc * xc, axis=-1, keepdims=True)
    rstd = lax.rsqrt(var + EPS)
    return xc * rstd, rstd


PAIR_ROWS = 2 * BLOCK
PAIR_COLS = 4 * BLOCK
ATTN_BLOCKS_PER_STEP = 4


def _attn_blocks_per_step(s_len):
    qb = ATTN_BLOCKS_PER_STEP
    while (s_len // BLOCK) % qb:
        qb //= 2
    return qb


def _pair_rows(v0, v1):
    r = lax.broadcasted_iota(jnp.int32, (PAIR_ROWS, 1), 0)
    return jnp.where(r < BLOCK, v0, v1)


def _fill_attn_bias(bias_ref):
    rows = lax.broadcasted_iota(jnp.int32, (PAIR_ROWS, PAIR_COLS), 0)
    cols = lax.broadcasted_iota(jnp.int32, (PAIR_ROWS, PAIR_COLS), 1)
    key = cols & (2 * BLOCK - 1)
    dist = BLOCK + (rows & (BLOCK - 1)) - key
    in_band = (dist >= 0) & (dist < BLOCK)
    distf = dist.astype(F32)
    second = cols >= 2 * BLOCK
    for kh in range(N_KV_HEADS):
        slope_of = lambda j, a: 2.0 ** -(Q_PER_KV * kh + 2 * j + a + 1)
        slope = jnp.where(rows < BLOCK, jnp.where(second, slope_of(0, 1), slope_of(0, 0)),
                          jnp.where(second, slope_of(1, 1), slope_of(1, 0)))
        bias = -slope * distf
        bias_ref[0, kh] = jnp.where(in_band & (key >= BLOCK), bias, NEG)
        bias_ref[1, kh] = jnp.where(in_band, bias, NEG)


def _pair_block_matrix(x, x_swapped, kh):
    lo = lax.broadcasted_iota(jnp.int32, (1, 2 * HEAD_DIM), 1) < HEAD_DIM
    in_lo, in_hi = (x, x_swapped) if kh == 0 else (x_swapped, x)
    return jnp.concatenate([jnp.where(lo, in_lo, 0.0), jnp.where(lo, 0.0, in_hi)], axis=0).astype(MXU_DTYPE)


def _pair_queries(q, kh):
    return (_pair_stack(q, kh) * SCALE).astype(MXU_DTYPE)


def _pair_stack(a, kh):
    return jnp.concatenate([a[:, 2 * BLOCK * kh: 2 * BLOCK * kh + BLOCK],
                            a[:, 2 * BLOCK * kh + BLOCK: 2 * BLOCK * (kh + 1)]], axis=0)


def _pair_unstack(parts):
    return jnp.concatenate([p[BLOCK * j: BLOCK * (j + 1)] for p in parts for j in range(2)], axis=-1)


def _pair_softmax(s, kh, sinks_ref):
    ps, p_sinks = [], []
    for a in range(2):
        sa = s[:, 2 * BLOCK * a: 2 * BLOCK * (a + 1)]
        sink = _pair_rows(sinks_ref[Q_PER_KV * kh + a], sinks_ref[Q_PER_KV * kh + 2 + a])
        m = jnp.maximum(jnp.max(sa, axis=-1, keepdims=True), sink)
        e = jnp.exp(sa - m)
        es = jnp.exp(sink - m)
        inv = 1.0 / (jnp.sum(e, axis=-1, keepdims=True) + es)
        ps.append(e * inv)
        p_sinks.append(es * inv)
    return jnp.concatenate(ps, axis=-1), p_sinks


def _fold_pair_rows(t):
    return t[:HEAD_DIM, :2 * BLOCK] + t[HEAD_DIM:, 2 * BLOCK:]


def _in_proj(x, g, w_t, carried=()):
    s_len, d = x.shape
    ts = _row_tile(s_len, ROW_TILE)
    nt = s_len // ts
    exch = _Exchange(carried)

    def body(*refs):
        (x_ref, g_ref, w_ref), (o_ref,), _, xrefs = _split_refs(refs, 3, 1, exch)
        i = pl.program_id(0)
        exch.run(xrefs, i == 0, i == nt - 1)
        xv = x_ref[...]
        r = lax.rsqrt(jnp.mean(xv * xv, axis=-1, keepdims=True) + EPS)
        o_ref[...] = _mm(xv * r * g_ref[...], w_ref[...], _NT)

    res = pl.pallas_call(
        body, name="in_proj_carrier" if exch.n else "in_proj", grid=(nt,),
        in_specs=[pl.BlockSpec((ts, d), lambda i: (i, 0)), g[1], _full((D_IN, d))] + [_ANY] * exch.n,
        out_specs=[pl.BlockSpec((ts, D_IN), lambda i: (i, 0))] + [_ANY] * exch.n,
        out_shape=[jax.ShapeDtypeStruct((s_len, D_IN), F32)] + exch.out_shapes(),
        scratch_shapes=exch.scratch(),
        compiler_params=_params(1),
    )(x, g[0], w_t, *exch.sources)
    return res[0], res[1:]


def _poolconv_fwd(proj, wp, scale, dw, cb, lng, lnb, pw, carried=()):
    s_len = proj.shape[0]
    ts = _row_tile(s_len, ROW_TILE)
    hb = ts // HALO
    nt = s_len // ts
    exch = _Exchange(carried)

    def body(*refs):
        ins, (y_ref, cv_ref, diff_ref), _, xrefs = _split_refs(refs, 9, 3, exch)
        p_ref, ph_ref, wp_ref, sc_ref, dw_ref, cb_ref, lng_ref, lnb_ref, pw_ref = ins
        i = pl.program_id(0)
        exch.run(xrefs, i == 0, i == nt - 1)
        halo = jnp.where(i > 0, ph_ref[...], 0.0)
        y_mix, cv, diff = _pool_conv_mixers(p_ref[...], halo, i * ts, ts, wp_ref, sc_ref, dw_ref, cb_ref, lng_ref,
                                            lnb_ref, pw_ref)
        y_ref[...] = y_mix.astype(y_ref.dtype)
        cv_ref[...] = cv
        diff_ref[...] = diff

    res = pl.pallas_call(
        body, name="poolconv_fwd_carrier" if exch.n else "poolconv_fwd", grid=(nt,),
        in_specs=[pl.BlockSpec((ts, HALF_IN), lambda i: (i, 0)),
                  pl.BlockSpec((HALO, HALF_IN), lambda i: (jnp.maximum(i * hb - 1, 0), 0)),
                  _full((256, 256)), scale[1], _full((CONV_TAPS_PAD, 256)), cb[1], lng[1], lnb[1],
                  _full((256, 256))]
        + [_ANY] * exch.n,
        out_specs=[pl.BlockSpec((ts, 512), lambda i: (i, 0)), pl.BlockSpec((ts, 256), lambda i: (i, 0)),
                   pl.BlockSpec((ts, 256), lambda i: (i, 0))] + [_ANY] * exch.n,
        out_shape=[jax.ShapeDtypeStruct((s_len, 512), MXU_DTYPE), jax.ShapeDtypeStruct((s_len, 256), F32),
                   jax.ShapeDtypeStruct((s_len, 256), MXU_DTYPE)] + exch.out_shapes(),
        scratch_shapes=exch.scratch(),
        compiler_params=_params(1),
    )(proj, proj, wp, scale[0], dw, cb[0], lng[0], lnb[0], pw, *exch.sources)
    return res[:3], res[3:]


def _pool_conv_mixers(cur, halo, t0, ts, wp_ref, sc_ref, dw_ref, cb_ref, lng_ref, lnb_ref, pw_ref):
    ext = jnp.concatenate([halo, cur], axis=0)
    diff = _pool_diff(ext[:, 0:256], t0, ts).astype(MXU_DTYPE)
    gp = cur[:, 256:512]
    y_pool = _mm(diff, wp_ref[...]) * sc_ref[...] * (gp * _sig(gp))
    hh = ext[:, 512:768] * _sig(ext[:, 768:1024])
    shifted = [hh if r == 0 else pltpu.roll(hh, r, 0) for r in range(8)]
    cv = cb_ref[...]
    for dist, r, m in _conv_taps():
        cv = cv + dw_ref[pl.ds(CONV_KERNEL - 1 - dist, 1), :] * shifted[r][HALO - 8 * m: HALO - 8 * m + ts]
    n, _ = _layer_norm(cv)
    z = n * lng_ref[...] + lnb_ref[...]
    gc = cur[:, 1024:1280]
    y_conv = _mm(z * _sig(z), pw_ref[...]) * (gc * _sig(gc))
    return jnp.concatenate([y_pool, y_conv], axis=-1), cv, diff


def _attn_fwd(proj, sinks, carried=()):
    s_len = proj.shape[0]
    qb = _attn_blocks_per_step(s_len)
    ts = qb * BLOCK
    nt = s_len // ts
    exch = _Exchange(carried)

    def body(*refs):
        (p_ref, kvp_ref, sinks_ref), (y_ref, ya_ref), (bias_ref,), xrefs = _split_refs(refs, 3, 2, exch)
        i = pl.program_id(0)
        exch.run(xrefs, i == 0, i == nt - 1)

        @pl.when(i == 0)
        def _():
            _fill_attn_bias(bias_ref)

        probs = [(b, kh) for b in range(qb) for kh in range(N_KV_HEADS)]
        scores, v_bds = {}, {}
        for b in range(qb):
            r0 = BLOCK * b
            kv_prev = kvp_ref[...] if b == 0 else p_ref[r0 - BLOCK:r0, 512:768]
            kv2 = jnp.concatenate([kv_prev, p_ref[r0:r0 + BLOCK, 512:768]], axis=0)
            k2, v2 = kv2[:, :BLOCK], kv2[:, BLOCK:]
            k2_swapped, v2_swapped = pltpu.roll(k2, HEAD_DIM, 1), pltpu.roll(v2, HEAD_DIM, 1)
            variant = jnp.where(i == 0, 0, 1) if b == 0 else 1
            q = p_ref[r0:r0 + BLOCK, 0:512]
            for kh in range(N_KV_HEADS):
                k_bd = _pair_block_matrix(k2, k2_swapped, kh)
                v_bds[b, kh] = _pair_block_matrix(v2, v2_swapped, kh)
                scores[b, kh] = _mm(_pair_queries(q, kh), k_bd, _NT) + bias_ref[variant, kh]
        ps = {pr: _pair_softmax(scores[pr], pr[1], sinks_ref)[0] for pr in probs}
        outs = {pr: _mm(ps[pr], v_bds[pr]) for pr in probs}
        for b in range(qb):
            r0 = BLOCK * b
            ga = p_ref[r0:r0 + BLOCK, 768:1280]
            ya = _pair_unstack([outs[b, kh] for kh in range(N_KV_HEADS)])
            ya_ref[r0:r0 + BLOCK, :] = ya
            y_ref[r0:r0 + BLOCK, :] = (ya * (ga * _sig(ga))).astype(y_ref.dtype)

    res = pl.pallas_call(
        body, name="attn_fwd_carrier" if exch.n else "attn_fwd", grid=(nt,),
        in_specs=[pl.BlockSpec((ts, HALF_IN), lambda i: (i, 1)),
                  pl.BlockSpec((BLOCK, 256), lambda i: (jnp.maximum(i * qb - 1, 0), 7)),
                  pl.BlockSpec(memory_space=pltpu.SMEM)] + [_ANY] * exch.n,
        out_specs=[pl.BlockSpec((ts, 512), lambda i: (i, 0)), pl.BlockSpec((ts, 512), lambda i: (i, 0))]
        + [_ANY] * exch.n,
        out_shape=[jax.ShapeDtypeStruct((s_len, 512), MXU_DTYPE), jax.ShapeDtypeStruct((s_len, 512), F32)]
        + exch.out_shapes(),
        scratch_shapes=[pltpu.VMEM((2, N_KV_HEADS, PAIR_ROWS, PAIR_COLS), F32)] + exch.scratch(),
        compiler_params=_params(1),
    )(proj, proj, sinks, *exch.sources)
    return res[:2], res[2:]


def _out_in_proj(x, y_pc, y_at, w_out, g_next, w_t_next):
    s_len, d = x.shape
    ts = _row_tile(s_len, ROW_TILE)

    def body(x_ref, a_ref, b_ref, w_ref, g_ref, wn_ref, o_ref, p_ref):
        y = jnp.concatenate([a_ref[...], b_ref[...]], axis=-1)
        xv = x_ref[...] + _mm(y, w_ref[...])
        o_ref[...] = xv
        r = lax.rsqrt(jnp.mean(xv * xv, axis=-1, keepdims=True) + EPS)
        p_ref[...] = _mm(xv * r * g_ref[...], wn_ref[...], _NT)

    tile = pl.BlockSpec((ts, d), lambda i: (i, 0))
    half = pl.BlockSpec((ts, 512), lambda i: (i, 0))
    return pl.pallas_call(
        body, name="out_in_proj", grid=(s_len // ts,),
        in_specs=[tile, half, half, _full((D_MIX, d)), g_next[1], _full((D_IN, d))],
        out_specs=[tile, pl.BlockSpec((ts, D_IN), lambda i: (i, 0))],
        out_shape=[jax.ShapeDtypeStruct((s_len, d), F32), jax.ShapeDtypeStruct((s_len, D_IN), F32)],
        compiler_params=_params(1),
    )(x, y_pc, y_at, w_out, g_next[0], w_t_next)


def _out_proj_loss(x, y_pc, y_at, w_out, target, g):
    s_len, d = x.shape
    ts = _row_tile(s_len, ROW_TILE)

    def body(x_ref, a_ref, b_ref, w_ref, t_ref, g_ref, sq_ref, dg_ref, dx_ref):
        i = pl.program_id(0)
        y = jnp.concatenate([a_ref[...], b_ref[...]], axis=-1)
        xv = x_ref[...] + _mm(y, w_ref[...])
        gv = g_ref[...]
        r = lax.rsqrt(jnp.mean(xv * xv, axis=-1, keepdims=True) + EPS)
        xr = xv * r
        err = xr * gv - t_ref[...]
        dout = err * (1.0 / d)
        w = dout * gv
        dx_ref[...] = r * (w - xr * jnp.mean(w * xr, axis=-1, keepdims=True))

        @pl.when(i == 0)
        def _():
            sq_ref[...] = jnp.zeros_like(sq_ref)
            dg_ref[...] = jnp.zeros_like(dg_ref)

        sq = jnp.sum(jnp.sum(err * err, axis=-1, keepdims=True), axis=0, keepdims=True)
        sq_ref[...] += jnp.broadcast_to(sq, sq_ref.shape)
        dg_ref[...] += jnp.sum(dout * xr, axis=0, keepdims=True)

    tile = pl.BlockSpec((ts, d), lambda i: (i, 0))
    half = pl.BlockSpec((ts, 512), lambda i: (i, 0))
    return pl.pallas_call(
        body, name="out_proj_loss", grid=(s_len // ts,),
        in_specs=[tile, half, half, _full((D_MIX, d)), tile, _full((1, d))],
        out_specs=[_full((1, 128)), _full((1, d)), tile],
        out_shape=[jax.ShapeDtypeStruct((1, 128), F32), jax.ShapeDtypeStruct((1, d), F32),
                   jax.ShapeDtypeStruct((s_len, d), F32)],
        compiler_params=_params(1),
    )(x, y_pc, y_at, w_out, target, g)


def _out_bwd(dxo, y_pc, y_at, w_out):
    s_len, d = dxo.shape
    ts = _row_tile(s_len, ROW_TILE)
    nt = s_len // ts

    def body(dx_ref, a_ref, b_ref, w_ref, dy_ref, gw_ref, acc_ref):
        i = pl.program_id(0)
        dxv = dx_ref[...].astype(MXU_DTYPE)
        dy_ref[...] = _mm(dxv, w_ref[...], _NT)

        @pl.when(i == 0)
        def _():
            acc_ref[...] = jnp.zeros_like(acc_ref)

        y = jnp.concatenate([a_ref[...], b_ref[...]], axis=-1)
        acc_ref[...] += _mm(y, dxv, _TN)

        @pl.when(i == nt - 1)
        def _():
            gw_ref[...] = acc_ref[...].astype(gw_ref.dtype)

    return pl.pallas_call(
        body, name="out_bwd", grid=(nt,),
        in_specs=[pl.BlockSpec((ts, d), lambda i: (i, 0)), pl.BlockSpec((ts, 512), lambda i: (i, 0)),
                  pl.BlockSpec((ts, 512), lambda i: (i, 0)), _full((D_MIX, d))],
        out_specs=[pl.BlockSpec((ts, D_MIX), lambda i: (i, 0)), _full((D_MIX, d))],
        out_shape=[jax.ShapeDtypeStruct((s_len, D_MIX), F32), jax.ShapeDtypeStruct((D_MIX, d), EXCHANGE_DTYPE)],
        scratch_shapes=[pltpu.VMEM((D_MIX, d), F32)],
        compiler_params=_params(1),
    )(dxo, y_pc, y_at, w_out)


def _poolconv_bwd(proj, cv, diff, dy, wp, scale, dw, lng, lnb, pw, carried=()):
    s_len = proj.shape[0]
    ts = _row_tile(s_len, ROW_TILE)
    hb = ts // HALO
    nt = s_len // ts
    last_halo = s_len // HALO - 1
    n = ts + HALO
    exch = _Exchange(carried)

    def body(*refs):
        ins, outs, (up_ref, hh_ref, dhh_ref, gdw_acc_ref), xrefs = _split_refs(refs, 13, 5, exch)
        (p_ref, pn_ref, cv_ref, cvn_ref, diff_ref, dy_ref, dyn_ref, wp_ref, sc_ref, dw_ref, lng_ref, lnb_ref,
         pw_ref) = ins
        da_ref, gwp_ref, gpw_ref, gdw_ref, gvec_ref = outs
        i = pl.program_id(0)
        exch.run(xrefs, i == 0, i == nt - 1)
        has_next = i < nt - 1
        cur = p_ref[...]
        nxt = jnp.where(has_next, pn_ref[...], 0.0)
        dyx = jnp.concatenate([dy_ref[...], jnp.where(has_next, dyn_ref[...], 0.0)], axis=0)
        row = lax.broadcasted_iota(jnp.int32, (n, 1), 0)
        in_seq = (row < ts) | has_next
        scale_v = sc_ref[...]

        cvx = jnp.concatenate([cv_ref[...], jnp.where(has_next, cvn_ref[...], 0.0)], axis=0)
        nrm, rstd = _layer_norm(cvx)
        z = nrm * lng_ref[...] + lnb_ref[...]
        sz = _sig(z)
        sw = z * sz
        gc = jnp.concatenate([cur[:, 1024:1280], nxt[:, 1024:1280]], axis=0)
        sgc = _sig(gc)
        yc = _mm(sw, pw_ref[...])
        dyc = dyx[:, 256:512]
        d_yc = dyc * (gc * sgc)
        d_gc = (dyc * yc * _dsilu(gc, sgc))[:ts]
        d_z = _mm(d_yc, pw_ref[...], _NT) * _dsilu(z, sz)
        d_n = d_z * lng_ref[...]
        d_cv = rstd * (d_n - jnp.mean(d_n, axis=-1, keepdims=True)
                       - nrm * jnp.mean(d_n * nrm, axis=-1, keepdims=True))
        d_cv = jnp.where(in_seq, d_cv, 0.0)
        _store_shifted(up_ref, d_cv, up=True)
        _anticausal_conv(up_ref, dw_ref, dhh_ref, ts)
        d_hh = dhh_ref[...]
        a_c, sb_c = cur[:, 512:768], _sig(cur[:, 768:1024])
        hh_ref[...] = a_c * sb_c
        d_a = d_hh * sb_c
        d_b = d_hh * a_c * sb_c * (1.0 - sb_c)
        d_cv_t = d_cv[:ts]
        _depthwise_conv_weight_grad(hh_ref, up_ref, gdw_acc_ref, ts)

        diff = diff_ref[...]
        raw = _mm(diff, wp_ref[...])
        gp = jnp.concatenate([cur[:, 256:512], nxt[:, 256:512]], axis=0)
        sgp = _sig(gp)
        dyp = dyx[:, 0:256]
        d_yp = dyp * (gp * sgp)
        d_gp = dyp[:ts] * (raw * scale_v) * _dsilu(gp, sgp)[:ts]
        d_raw = d_yp * scale_v
        d_diff = _mm(d_raw, wp_ref[...], _NT)
        w = jnp.where(in_seq, d_diff / _pool_count(i * ts, n), 0.0)
        d_u = _pool_diff_bwd(w, ts) - d_diff[:ts]

        da_ref[...] = jnp.concatenate([d_u, d_gp, d_a, d_b, d_gc], axis=-1).astype(da_ref.dtype)

        @pl.when(i == 0)
        def _():
            gwp_ref[...] = jnp.zeros_like(gwp_ref)
            gpw_ref[...] = jnp.zeros_like(gpw_ref)
            gdw_ref[...] = jnp.zeros_like(gdw_ref)
            gvec_ref[...] = jnp.zeros_like(gvec_ref)

        gwp_ref[...] += _mm(diff, d_raw[:ts], _TN)
        gpw_ref[...] += _mm(sw[:ts], d_yc[:ts], _TN)
        gdw_ref[...] += jnp.sum(gdw_acc_ref[...], axis=1)
        zero_row = jnp.zeros((1, 256), F32)
        gvec_ref[...] += jnp.concatenate([
            jnp.sum(d_yp[:ts] * raw, axis=0, keepdims=True),
            jnp.sum(d_cv_t, axis=0, keepdims=True),
            jnp.sum((d_z * nrm)[:ts], axis=0, keepdims=True),
            jnp.sum(d_z[:ts], axis=0, keepdims=True),
            zero_row, zero_row, zero_row, zero_row], axis=0)

    nxt_halo = lambda i: (jnp.minimum((i + 1) * hb, last_halo), 0)
    res = pl.pallas_call(
        body, name="poolconv_bwd_carrier" if exch.n else "poolconv_bwd", grid=(nt,),
        in_specs=[pl.BlockSpec((ts, HALF_IN), lambda i: (i, 0)), pl.BlockSpec((HALO, HALF_IN), nxt_halo),
                  pl.BlockSpec((ts, 256), lambda i: (i, 0)), pl.BlockSpec((HALO, 256), nxt_halo),
                  pl.BlockSpec((ts, 256), lambda i: (i, 0)),
                  pl.BlockSpec((ts, 512), lambda i: (i, 0)), pl.BlockSpec((HALO, 512), nxt_halo),
                  _full((256, 256)), scale[1], _full((CONV_TAPS_PAD, 256)), lng[1], lnb[1], _full((256, 256))]
        + [_ANY] * exch.n,
        out_specs=[pl.BlockSpec((ts, HALF_IN), lambda i: (i, 0)), _full((256, 256)), _full((256, 256)),
                   _full((CONV_TAPS_PAD, 256)), _full((8, 256))] + [_ANY] * exch.n,
        out_shape=[jax.ShapeDtypeStruct((s_len, HALF_IN), MXU_DTYPE), jax.ShapeDtypeStruct((256, 256), F32),
                   jax.ShapeDtypeStruct((256, 256), F32), jax.ShapeDtypeStruct((CONV_TAPS_PAD, 256), F32),
                   jax.ShapeDtypeStruct((8, 256), F32)] + exch.out_shapes(),
        scratch_shapes=[pltpu.VMEM((8, n, CONV_WIDTH), F32), pltpu.VMEM((ts, CONV_WIDTH), F32),
                        pltpu.VMEM((ts, CONV_WIDTH), F32), pltpu.VMEM((CONV_TAPS_PAD, 8, CONV_WIDTH), F32)]
        + exch.scratch(),
        compiler_params=_params(1),
    )(proj, proj, cv, cv, diff, dy, dy, wp, scale[0], dw, lng[0], lnb[0], pw, *exch.sources)
    return res[:5], res[5:]


DQ0, DKC0, DVC0, DKP0, DVP0, DGA0, DATTN_W = 0, 512, 640, 768, 896, 1024, 1536


def _attn_bwd(proj, ya, dy, sinks, carried=()):
    s_len = proj.shape[0]
    qb = _attn_blocks_per_step(s_len)
    ts = qb * BLOCK
    nt = s_len // ts
    exch = _Exchange(carried)

    def body(*refs):
        ins, (o_ref, gs_ref), (bias_ref,), xrefs = _split_refs(refs, 5, 2, exch)
        p_ref, kvp_ref, ya_ref, dy_ref, sinks_ref = ins
        i = pl.program_id(0)
        exch.run(xrefs, i == 0, i == nt - 1)

        @pl.when(i == 0)
        def _():
            _fill_attn_bias(bias_ref)
            gs_ref[...] = jnp.zeros_like(gs_ref)

        probs = [(b, kh) for b in range(qb) for kh in range(N_KV_HEADS)]
        kv_heads = range(N_KV_HEADS)
        scores, k_bds, v_bds, q2s, do2s, dyas, gas, sgas = {}, {}, {}, {}, {}, {}, {}, {}
        for b in range(qb):
            r0 = BLOCK * b
            kv_prev = kvp_ref[...] if b == 0 else p_ref[r0 - BLOCK:r0, 512:768]
            kv2 = jnp.concatenate([kv_prev, p_ref[r0:r0 + BLOCK, 512:768]], axis=0)
            k2, v2 = kv2[:, :BLOCK], kv2[:, BLOCK:]
            k2_swapped, v2_swapped = pltpu.roll(k2, HEAD_DIM, 1), pltpu.roll(v2, HEAD_DIM, 1)
            variant = jnp.where(i == 0, 0, 1) if b == 0 else 1
            q = p_ref[r0:r0 + BLOCK, 0:512]
            gas[b] = p_ref[r0:r0 + BLOCK, 768:1280]
            dyas[b] = dy_ref[r0:r0 + BLOCK, :]
            sgas[b] = _sig(gas[b])
            d_o = dyas[b] * (gas[b] * sgas[b])
            for kh in kv_heads:
                k_bds[b, kh] = _pair_block_matrix(k2, k2_swapped, kh)
                v_bds[b, kh] = _pair_block_matrix(v2, v2_swapped, kh)
                q2s[b, kh] = _pair_queries(q, kh)
                do2s[b, kh] = _pair_stack(d_o, kh)
                scores[b, kh] = _mm(q2s[b, kh], k_bds[b, kh], _NT) + bias_ref[variant, kh]
        ps, p_sinks, dps, dss, dqs, dks, dvs = {}, {}, {}, {}, {}, {}, {}
        d_sinks = [None] * N_Q_HEADS

        def softmax_stage(pr):
            ps[pr], p_sinks[pr] = _pair_softmax(scores[pr], pr[1], sinks_ref)

        def value_stage(pr):
            dps[pr] = _mm(do2s[pr], v_bds[pr], _NT)

        def score_grad_stage(pr):
            p, dp, kh = ps[pr], dps[pr], pr[1]
            ds_halves = []
            for a in range(2):
                cols = slice(2 * BLOCK * a, 2 * BLOCK * (a + 1))
                delta = jnp.sum(p[:, cols] * dp[:, cols], axis=-1, keepdims=True)
                ds_halves.append(p[:, cols] * (dp[:, cols] - delta))
                dsink = -p_sinks[pr][a] * delta
                for j in range(2):
                    part = jnp.sum(dsink[BLOCK * j: BLOCK * (j + 1)], axis=0, keepdims=True)
                    h = Q_PER_KV * kh + 2 * j + a
                    d_sinks[h] = part if d_sinks[h] is None else d_sinks[h] + part
            dss[pr] = jnp.concatenate(ds_halves, axis=-1)

        def operand_grad_stage(pr):
            dqs[pr] = _mm(dss[pr], k_bds[pr]) * SCALE
            dks[pr] = _fold_pair_rows(_mm(q2s[pr], dss[pr], _TN))
            dvs[pr] = _fold_pair_rows(_mm(do2s[pr], ps[pr], _TN))

        for stage in (softmax_stage, value_stage, score_grad_stage, operand_grad_stage):
            for pr in probs:
                stage(pr)
        for b in range(qb):
            r0 = BLOCK * b
            dk = jnp.concatenate([dks[b, kh] for kh in kv_heads], axis=0).T
            dv = jnp.concatenate([dvs[b, kh] for kh in kv_heads], axis=0).T
            d_ga = dyas[b] * ya_ref[r0:r0 + BLOCK, :] * _dsilu(gas[b], sgas[b])
            o_ref[r0:r0 + BLOCK, :] = jnp.concatenate(
                [_pair_unstack([dqs[b, kh] for kh in kv_heads]), dk[BLOCK:], dv[BLOCK:], dk[:BLOCK], dv[:BLOCK],
                 d_ga], axis=-1)
        gs_ref[...] += jnp.broadcast_to(jnp.concatenate(d_sinks, axis=0), gs_ref.shape)

    res = pl.pallas_call(
        body, name="attn_bwd_carrier" if exch.n else "attn_bwd", grid=(nt,),
        in_specs=[pl.BlockSpec((ts, HALF_IN), lambda i: (i, 1)),
                  pl.BlockSpec((BLOCK, 256), lambda i: (jnp.maximum(i * qb - 1, 0), 7)),
                  pl.BlockSpec((ts, 512), lambda i: (i, 0)), pl.BlockSpec((ts, 512), lambda i: (i, 1)),
                  pl.BlockSpec(memory_space=pltpu.SMEM)] + [_ANY] * exch.n,
        out_specs=[pl.BlockSpec((ts, DATTN_W), lambda i: (i, 0)), _full((N_Q_HEADS, 128))] + [_ANY] * exch.n,
        out_shape=[jax.ShapeDtypeStruct((s_len, DATTN_W), F32), jax.ShapeDtypeStruct((N_Q_HEADS, 128), F32)]
        + exch.out_shapes(),
        scratch_shapes=[pltpu.VMEM((2, N_KV_HEADS, PAIR_ROWS, PAIR_COLS), F32)] + exch.scratch(),
        compiler_params=_params(1),
    )(proj, proj, ya, dy, sinks, *exch.sources)
    return res[:2], res[2:]


def _in_bwd(da, dattn, x, dxo, g, w_t):
    s_len, d = x.shape
    ts = _row_tile(s_len, IN_BWD_TILE)
    bpt = ts // BLOCK
    nt = s_len // ts
    last_block = s_len // BLOCK - 1

    def body(da_ref, dat_ref, nxt_ref, x_ref, dxo_ref, g_ref, w_ref, dx_ref, dg_ref, gw_ref, acc_ref, stage_ref,
             stage_sem):
        i = pl.program_id(0)
        dat = dat_ref[...]
        nxt = jnp.where(i < nt - 1, nxt_ref[...], 0.0)
        shifted = jnp.concatenate([dat[BLOCK:, DKP0:DGA0], nxt], axis=0) if bpt > 1 else nxt
        dkv = dat[:, DKC0:DKP0] + shifted
        dproj = jnp.concatenate([da_ref[...], dat[:, DQ0:DKC0].astype(MXU_DTYPE), dkv.astype(MXU_DTYPE),
                                 dat[:, DGA0:DATTN_W].astype(MXU_DTYPE)], axis=-1)
        d_h = _mm(dproj, w_ref[...])
        xv = x_ref[...]
        gv = g_ref[...]
        r = lax.rsqrt(jnp.mean(xv * xv, axis=-1, keepdims=True) + EPS)
        xr = xv * r
        w = d_h * gv
        dx_ref[...] = dxo_ref[...] + r * (w - xr * jnp.mean(w * xr, axis=-1, keepdims=True))

        @pl.when(i == 0)
        def _():
            dg_ref[...] = jnp.zeros_like(dg_ref)
            acc_ref[...] = jnp.zeros_like(acc_ref)

        dg_ref[...] += jnp.sum(d_h * xr, axis=0, keepdims=True)
        acc_ref[...] += _mm(dproj, xr * gv, _TN)

        @pl.when(i == nt - 1)
        def _():
            stage_ref[...] = acc_ref[...].astype(stage_ref.dtype)
            out = pltpu.make_async_copy(stage_ref, gw_ref, stage_sem)
            out.start()
            out.wait()

    return pl.pallas_call(
        body, name="in_bwd", grid=(nt,),
        in_specs=[pl.BlockSpec((ts, HALF_IN), lambda i: (i, 0)),
                  pl.BlockSpec((ts, DATTN_W), lambda i: (i, 0)),
                  pl.BlockSpec((BLOCK, 256), lambda i: (jnp.minimum((i + 1) * bpt, last_block), 3)),
                  pl.BlockSpec((ts, d), lambda i: (i, 0)), pl.BlockSpec((ts, d), lambda i: (i, 0)),
                  g[1], pl.BlockSpec((D_IN, d), lambda i: (0, 0), pipeline_mode=pl.Buffered(1))],
        out_specs=[pl.BlockSpec((ts, d), lambda i: (i, 0)), _full((1, d)), _ANY],
        out_shape=[jax.ShapeDtypeStruct((s_len, d), F32), jax.ShapeDtypeStruct((1, d), F32),
                   jax.ShapeDtypeStruct((D_IN, d), EXCHANGE_DTYPE)],
        scratch_shapes=[pltpu.VMEM((D_IN, d), F32), pltpu.VMEM((D_IN, d), EXCHANGE_DTYPE), pltpu.SemaphoreType.DMA],
        compiler_params=_params(1),
    )(da, dattn, dattn, x, dxo, g[0], w_t)


def _in_bwd_dw(da, dattn, x, g):
    s_len, d = x.shape
    ts = _row_tile(s_len, IN_BWD_TILE)
    bpt = ts // BLOCK
    nt = s_len // ts
    last_block = s_len // BLOCK - 1

    def body(da_ref, dat_ref, nxt_ref, x_ref, g_ref, dp_ref, gw_ref, acc_ref, stage_ref, stage_sem):
        i = pl.program_id(0)
        dat = dat_ref[...]
        nxt = jnp.where(i < nt - 1, nxt_ref[...], 0.0)
        shifted = jnp.concatenate([dat[BLOCK:, DKP0:DGA0], nxt], axis=0) if bpt > 1 else nxt
        dkv = dat[:, DKC0:DKP0] + shifted
        dproj = jnp.concatenate([da_ref[...], dat[:, DQ0:DKC0].astype(MXU_DTYPE), dkv.astype(MXU_DTYPE),
                                 dat[:, DGA0:DATTN_W].astype(MXU_DTYPE)], axis=-1)
        dp_ref[...] = dproj
        xv = x_ref[...]
        r = lax.rsqrt(jnp.mean(xv * xv, axis=-1, keepdims=True) + EPS)

        @pl.when(i == 0)
        def _():
            acc_ref[...] = jnp.zeros_like(acc_ref)

        acc_ref[...] += _mm(dproj, xv * r * g_ref[...], _TN)

        @pl.when(i == nt - 1)
        def _():
            stage_ref[...] = acc_ref[...].astype(stage_ref.dtype)
            out = pltpu.make_async_copy(stage_ref, gw_ref, stage_sem)
            out.start()
            out.wait()

    return pl.pallas_call(
        body, name="in_bwd_dw", grid=(nt,),
        in_specs=[pl.BlockSpec((ts, HALF_IN), lambda i: (i, 0)),
                  pl.BlockSpec((ts, DATTN_W), lambda i: (i, 0)),
                  pl.BlockSpec((BLOCK, 256), lambda i: (jnp.minimum((i + 1) * bpt, last_block), 3)),
                  pl.BlockSpec((ts, d), lambda i: (i, 0)), g[1]],
        out_specs=[pl.BlockSpec((ts, D_IN), lambda i: (i, 0)), _ANY],
        out_shape=[jax.ShapeDtypeStruct((s_len, D_IN), MXU_DTYPE), jax.ShapeDtypeStruct((D_IN, d), EXCHANGE_DTYPE)],
        scratch_shapes=[pltpu.VMEM((D_IN, d), F32), pltpu.VMEM((D_IN, d), EXCHANGE_DTYPE), pltpu.SemaphoreType.DMA],
        compiler_params=_params(1),
    )(da, dattn, dattn, x, g[0])


def _in_bwd_dx(dproj, x, dxo, g, w_t, carried=()):
    s_len, d = x.shape
    ts = _row_tile(s_len, ROW_TILE)
    nt = s_len // ts
    exch = _Exchange(carried)

    def body(*refs):
        (dp_ref, x_ref, dxo_ref, g_ref, w_ref), (dx_ref, dg_ref), _, xrefs = _split_refs(refs, 5, 2, exch)
        i = pl.program_id(0)
        exch.run(xrefs, i == 0, i == nt - 1)
        d_h = _mm(dp_ref[...], w_ref[...])
        xv = x_ref[...]
        r = lax.rsqrt(jnp.mean(xv * xv, axis=-1, keepdims=True) + EPS)
        xr = xv * r
        w = d_h * g_ref[...]
        dx_ref[...] = dxo_ref[...] + r * (w - xr * jnp.mean(w * xr, axis=-1, keepdims=True))

        @pl.when(i == 0)
        def _():
            dg_ref[...] = jnp.zeros_like(dg_ref)

        dg_ref[...] += jnp.sum(d_h * xr, axis=0, keepdims=True)

    tile = pl.BlockSpec((ts, d), lambda i: (i, 0))
    res = pl.pallas_call(
        body, name="in_bwd_dx_carrier" if exch.n else "in_bwd_dx", grid=(nt,),
        in_specs=[pl.BlockSpec((ts, D_IN), lambda i: (i, 0)), tile, tile, g[1], _full((D_IN, d))]
        + [_ANY] * exch.n,
        out_specs=[tile, _full((1, d))] + [_ANY] * exch.n,
        out_shape=[jax.ShapeDtypeStruct((s_len, d), F32), jax.ShapeDtypeStruct((1, d), F32)] + exch.out_shapes(),
        scratch_shapes=exch.scratch(),
        compiler_params=_params(1),
    )(dproj, x, dxo, g[0], w_t, *exch.sources)
    return res[:2], res[2:]


def _sum_partials(p_ref):
    g = p_ref[0].astype(F32)
    for k in range(1, N_DEV):
        g = g + p_ref[k].astype(F32)
    return g


def _adamw_step(g, w, m, v):
    nm = ADAM_B1 * m + (1.0 - ADAM_B1) * g
    nv = ADAM_B2 * v + (1.0 - ADAM_B2) * (g * g)
    m_hat = nm / (1.0 - ADAM_B1 ** ADAM_STEP)
    v_hat = nv / (1.0 - ADAM_B2 ** ADAM_STEP)
    return -ADAM_LR * (m_hat / (jnp.sqrt(v_hat) + ADAM_EPS) + ADAM_WD * w), nm, nv


def _adamw_layers(parts, w, m, v, name):
    _, rows, n = w.shape
    tr = _row_tile(rows, 64)

    def body(*refs):
        p_refs = refs[:DEPTH]
        w_ref, m_ref, v_ref, g_ref, d_ref, nm_ref, nv_ref = refs[DEPTH:]
        for l in range(DEPTH):
            g = _sum_partials(p_refs[l])
            g_ref[l] = g
            d_ref[l], nm_ref[l], nv_ref[l] = _adamw_step(g, w_ref[l], m_ref[l], v_ref[l])

    tile = pl.BlockSpec((DEPTH, tr, n), lambda i: (0, i, 0))
    shape = jax.ShapeDtypeStruct(w.shape, F32)
    return pl.pallas_call(
        body, name=name, grid=(rows // tr,),
        in_specs=[pl.BlockSpec((N_DEV, tr, n), lambda i: (0, i, 0))] * DEPTH + [tile] * 3,
        out_specs=[tile] * 4, out_shape=[shape] * 4,
        compiler_params=_params(1),
    )(*parts, w, m, v)


def _sum_rows(parts):
    def body(p_ref, o_ref):
        o_ref[...] = _sum_partials(p_ref)

    vmem = pl.BlockSpec(memory_space=pltpu.VMEM)
    return pl.pallas_call(body, name="sum_replicated_partials", in_specs=[vmem], out_specs=vmem,
                          out_shape=jax.ShapeDtypeStruct(parts.shape[1:], F32))(parts)


def _adamw_tensors(gs, ws, ms, vs):
    n = len(ws)
    shapes = [w.shape for w in ws]
    two_d = lambda a: a.reshape(1, -1) if a.ndim == 1 else a
    ops = [two_d(a) for a in list(gs) + list(ws) + list(ms) + list(vs)]

    def body(*refs):
        ins, outs = refs[:4 * n], refs[4 * n:]
        for k in range(n):
            g, w, m, v = (ins[j * n + k][...] for j in range(4))
            outs[k][...], outs[n + k][...], outs[2 * n + k][...] = _adamw_step(g, w, m, v)

    vmem = pl.BlockSpec(memory_space=pltpu.VMEM)
    res = pl.pallas_call(
        body, name="adamw_replicated", in_specs=[vmem] * (4 * n), out_specs=[vmem] * (3 * n),
        out_shape=[jax.ShapeDtypeStruct(o.shape, F32) for o in ops[n:2 * n]] * 3,
    )(*ops)
    return [[res[j * n + k].reshape(shapes[k]) for k in range(n)] for j in range(3)]


def _pad_rows(a, mult):
    pad = (-a.shape[0]) % mult
    return a if pad == 0 else jnp.concatenate([a, jnp.zeros((pad, a.shape[1]), a.dtype)], axis=0)


def _dw_rows(conv_dw_l):
    return jnp.pad(jnp.swapaxes(conv_dw_l, 0, 1), ((0, 0), (0, CONV_TAPS_PAD - CONV_KERNEL)))


def _pack_small(pw_l, dw_l, d):
    rows = jnp.concatenate([pw_l.reshape(-1, d), _dw_rows(dw_l).reshape(-1, d)], axis=0)
    return _pad_rows(rows, 8)


def _small_slabs(g_pw, g_dw, d):
    a = g_pw.reshape(N_DEV, -1, d)
    b = jnp.swapaxes(g_dw, 0, 1).reshape(N_DEV, -1, d)
    used = a.shape[1] + b.shape[1]
    return jnp.concatenate([a, b, jnp.zeros((N_DEV, (-used) % 8, d), g_pw.dtype)], axis=1)


def _unpack_small(rows, d):
    c = CONV_WIDTH // N_DEV
    n_pw = c * CONV_WIDTH // d
    n_dw = c * CONV_TAPS_PAD // d
    pw = rows[:n_pw].reshape(c, CONV_WIDTH)
    dw = jnp.swapaxes(rows[n_pw:n_pw + n_dw].reshape(c, CONV_TAPS_PAD), 0, 1)[:CONV_KERNEL]
    return pw, dw


def _pack_replicated(ln_g, pool_w, pool_scale, conv_b, conv_ln_g, conv_ln_b, attn_sinks, final_g, scalar, d):
    sinks = jnp.pad(attn_sinks, ((0, 0), (0, 256 - N_Q_HEADS)))
    small = jnp.concatenate([pool_scale, conv_b, conv_ln_g, conv_ln_b, sinks], axis=0)
    small = _pad_rows(small, d // 256)
    last = jnp.pad(scalar.reshape(1, 1), ((0, 0), (0, d - 1)))
    return _pad_rows(jnp.concatenate([ln_g.reshape(-1, d), final_g.reshape(-1, d), pool_w.reshape(-1, d),
                                      small.reshape(-1, d), last], axis=0), 8)


def _unpack_replicated(rows, d):
    n_pool = DEPTH * 4 * POOL_GROUP * POOL_GROUP // d
    n_small = -(-5 * DEPTH * 256 // d)
    ln_g = rows[:DEPTH]
    final_g = rows[DEPTH]
    pool_w = rows[DEPTH + 1: DEPTH + 1 + n_pool].reshape(DEPTH, 4, POOL_GROUP, POOL_GROUP)
    small = rows[DEPTH + 1 + n_pool: DEPTH + 1 + n_pool + n_small].reshape(-1, 256)[: 5 * DEPTH]
    pool_scale, conv_b, conv_ln_g, conv_ln_b = (small[DEPTH * k: DEPTH * (k + 1)] for k in range(4))
    sinks = small[4 * DEPTH: 5 * DEPTH, :N_Q_HEADS]
    scalar = rows[DEPTH + 1 + n_pool + n_small, 0]
    return ln_g, pool_w, pool_scale, conv_b, conv_ln_g, conv_ln_b, sinks, final_g, scalar


def _block_diag(pool_w):
    wide = jnp.tile(pool_w.reshape(POOL_WIDTH, POOL_GROUP), (1, POOL_WIDTH // POOL_GROUP))
    rows = lax.broadcasted_iota(jnp.int32, wide.shape, 0) // POOL_GROUP
    cols = lax.broadcasted_iota(jnp.int32, wide.shape, 1) // POOL_GROUP
    return jnp.where(rows == cols, wide, jnp.zeros_like(wide))


def _diag_blocks(mat):
    return jnp.stack([mat[POOL_GROUP * gi: POOL_GROUP * (gi + 1), POOL_GROUP * gi: POOL_GROUP * (gi + 1)]
                      for gi in range(4)], axis=0)


def kernel(x, ln_g, w_in, pool_w, pool_scale, conv_dw, conv_b, conv_ln_g, conv_ln_b, conv_pw, attn_sinks, w_out, final_g, loss_target, m_ln_g, m_w_in, m_pool_w, m_pool_scale, m_conv_dw, m_conv_b, m_conv_ln_g, m_conv_ln_b, m_conv_pw, m_attn_sinks, m_w_out, m_final_g, v_ln_g, v_w_in, v_pool_w, v_pool_scale, v_conv_dw, v_conv_b, v_conv_ln_g, v_conv_ln_b, v_conv_pw, v_attn_sinks, v_w_out, v_final_g):
    x0 = x[0]
    d = x0.shape[1]
    row = lambda a: a.reshape(1, -1)
    vec = _layer_vec
    slabs = lambda a: a.reshape(N_DEV, a.shape[0] // N_DEV, d)
    c_shard = CONV_WIDTH // N_DEV

    w_in_rows = [jnp.swapaxes(w_in[l], 0, 1).astype(MXU_DTYPE) for l in range(DEPTH)]
    w_out_rows = [w_out[l].astype(MXU_DTYPE) for l in range(DEPTH)]
    per_word = 4 // jnp.dtype(MXU_DTYPE).itemsize
    dw_t = jnp.stack([_dw_rows(conv_dw[l]) for l in range(DEPTH)], axis=0)
    dw_bits = (lax.bitcast_convert_type(dw_t, MXU_DTYPE) if per_word > 1 else dw_t).reshape(-1, d)
    n_pw = DEPTH * c_shard * CONV_WIDTH // d
    conv_rows = _pad_rows(jnp.concatenate([conv_pw.reshape(-1, d).astype(MXU_DTYPE), dw_bits], axis=0), 16)
    w_in_t, w_out_f = [None] * DEPTH, [None] * DEPTH
    w_in_t[0] = _all_gather(w_in_rows[0], "w_in_all_gather")
    wp_bd = [_block_diag(pool_w[l]).astype(MXU_DTYPE) for l in range(DEPTH)]

    xs, projs, cvs, ys = [x0], [], [], []
    q4 = D_IN // N_DEV // 4
    w_in_1 = [w_in_rows[1][q4 * k: q4 * (k + 1)] for k in range(4)]
    for l in range(DEPTH):
        if l == 0:
            proj, got = _in_proj(xs[0], vec(ln_g, 0), w_in_t[0], carried=[w_out_rows[0], conv_rows, w_in_1[0]])
            w_in_1_got = [got[2]]
            w_out_f[0] = got[0].reshape(D_MIX, d)
            pw_all = got[1][:, :n_pw].reshape(N_DEV, DEPTH, c_shard, CONV_WIDTH)
            pw_f = [pw_all[:, k].reshape(CONV_WIDTH, CONV_WIDTH) for k in range(DEPTH)]
            bits = got[1][:, n_pw:n_pw + dw_bits.shape[0]].reshape(
                (N_DEV, DEPTH, c_shard, CONV_TAPS_PAD) + (per_word,) * (per_word > 1))
            dw_all = lax.bitcast_convert_type(bits, F32) if per_word > 1 else bits
            dw_f = [jnp.swapaxes(dw_all[:, k].reshape(CONV_WIDTH, CONV_TAPS_PAD), 0, 1) for k in range(DEPTH)]
        (y_pc, cv, diff), got = _poolconv_fwd(proj, wp_bd[l], vec(pool_scale, l), dw_f[l], vec(conv_b, l),
                                              vec(conv_ln_g, l), vec(conv_ln_b, l), pw_f[l],
                                              carried=[w_in_1[1], w_in_1[2]] if l == 0 else [])
        if l == 0:
            w_in_1_got += list(got)
        (y_at, ya), got = _attn_fwd(proj, attn_sinks[l], carried=[w_out_rows[1], w_in_1[3]] if l == 0 else [])
        if l == 0:
            w_out_f[1] = got[0].reshape(D_MIX, d)
            w_in_t[1] = jnp.concatenate(w_in_1_got + [got[1]], axis=1).reshape(D_IN, d)
        projs.append(proj)
        cvs.append((cv, diff))
        ys.append((y_pc, y_at, ya))
        if l < DEPTH - 1:
            x_next, proj = _out_in_proj(xs[l], y_pc, y_at, w_out_f[l], vec(ln_g, l + 1), w_in_t[l + 1])
            xs.append(x_next)

    l = DEPTH - 1
    sq, g_final, dx = _out_proj_loss(xs[l], ys[l][0], ys[l][1], w_out_f[l], loss_target[0], row(final_g))

    l = 1
    dy, g_wout1 = _out_bwd(dx, ys[l][0], ys[l][1], w_out_f[l])
    (da, g_wp1, g_pw1, g_dw1, g_vec1), _ = _poolconv_bwd(
        projs[l], *cvs[l], dy, wp_bd[l], vec(pool_scale, l), dw_f[l], vec(conv_ln_g, l), vec(conv_ln_b, l), pw_f[l])
    (dattn, gs1), _ = _attn_bwd(projs[l], ys[l][2], dy, attn_sinks[l])
    dx, g_ln1, g_win_t1 = _in_bwd(da, dattn, xs[l], dx, vec(ln_g, l), w_in_t[l])
    l = 0
    dy, g_wout0 = _out_bwd(dx, ys[l][0], ys[l][1], w_out_f[l])
    (da, g_wp0, g_pw0, g_dw0, g_vec0), (r_win1, r_wout1, r_small1) = _poolconv_bwd(
        projs[l], *cvs[l], dy, wp_bd[l], vec(pool_scale, l), dw_f[l], vec(conv_ln_g, l), vec(conv_ln_b, l), pw_f[l],
        carried=[slabs(g_win_t1), slabs(g_wout1), _small_slabs(g_pw1, g_dw1, d)])
    (dattn, gs0), (r_wout0, r_small0) = _attn_bwd(projs[l], ys[l][2], dy, attn_sinks[l],
                                                  carried=[slabs(g_wout0), _small_slabs(g_pw0, g_dw0, d)])
    dproj, g_win_t0 = _in_bwd_dw(da, dattn, xs[l], vec(ln_g, l))
    (dx, g_ln0), (r_win0,) = _in_bwd_dx(dproj, xs[l], dx, vec(ln_g, l), w_in_t[l], carried=[slabs(g_win_t0)])
    grad_x = dx[None]

    gv = jnp.stack([g_vec0, g_vec1], axis=0)
    rep_part = _pack_replicated(
        jnp.concatenate([g_ln0, g_ln1], axis=0), jnp.stack([_diag_blocks(g_wp0), _diag_blocks(g_wp1)], axis=0),
        gv[:, 0], gv[:, 1], gv[:, 2], gv[:, 3], jnp.stack([gs0[:, 0], gs1[:, 0]], axis=0), g_final, sq[0, 0], d)
    (r_rep,) = _final_exchange([rep_part])

    t = lambda a: jnp.swapaxes(a, 1, 2)
    win = [t(o) for o in _adamw_layers([r_win0, r_win1], t(w_in), t(m_w_in), t(v_w_in), "adamw_w_in")]
    wout = _adamw_layers([r_wout0, r_wout1], w_out, m_w_out, v_w_out, "adamw_w_out")
    pack_s = lambda pw_, dw_: jnp.stack([_pack_small(pw_[l], dw_[l], d) for l in range(DEPTH)], axis=0)
    small = _adamw_layers([r_small0, r_small1], pack_s(conv_pw, conv_dw), pack_s(m_conv_pw, m_conv_dw),
                          pack_s(v_conv_pw, v_conv_dw), "adamw_conv")
    small = [[_unpack_small(o[l], d) for l in range(DEPTH)] for o in small]
    *rep_grads, sq_sum = _unpack_replicated(_sum_rows(r_rep), d)
    loss = 0.5 / d * sq_sum
    rep_w = [ln_g, pool_w, pool_scale, conv_b, conv_ln_g, conv_ln_b, attn_sinks, final_g]
    rep_m = [m_ln_g, m_pool_w, m_pool_scale, m_conv_b, m_conv_ln_g, m_conv_ln_b, m_attn_sinks, m_final_g]
    rep_v = [v_ln_g, v_pool_w, v_pool_scale, v_conv_b, v_conv_ln_g, v_conv_ln_b, v_attn_sinks, v_final_g]
    rep = [rep_grads] + _adamw_tensors(rep_grads, rep_w, rep_m, rep_v)

    outs = []
    for k in range(4):
        r_ln, r_pool, r_scale, r_cb, r_lng, r_lnb, r_sinks, r_final = rep[k]
        s_pw = jnp.stack([small[k][l][0] for l in range(DEPTH)], axis=0)
        s_dw = jnp.stack([small[k][l][1] for l in range(DEPTH)], axis=0)
        outs += [r_ln, win[k], r_pool, r_scale, s_dw, r_cb, r_lng, r_lnb, s_pw, r_sinks, wout[k], r_final]
    return (loss, grad_x, *outs)
```

```python
import jax
import jax.numpy as jnp
from jax import lax
from jax.experimental import pallas as pl
from jax.experimental.pallas import tpu as pltpu

F32 = jnp.float32
MXU_DTYPE = jnp.bfloat16
EXCHANGE_DTYPE = jnp.bfloat16

N_DEV = 8
DEPTH = 2
POOL_WIDTH = 256
POOL_GROUP = 64
CONV_WIDTH = 256
CONV_KERNEL = 31
CONV_TAPS_PAD = 32
HEAD_DIM = 64
N_KV_HEADS = 2
Q_PER_KV = 4
N_Q_HEADS = 8
BLOCK = 128
D_MIX = 1024
D_IN = 2560
HALF_IN = 1280
EPS = 1e-6
SCALE = HEAD_DIM ** -0.5
NEG = -1e30

ADAM_LR = 0.001
ADAM_B1 = 0.9
ADAM_B2 = 0.999
ADAM_EPS = 1e-08
ADAM_WD = 0.01
ADAM_STEP = 10

HALO = 32
ROW_TILE = 512
IN_BWD_TILE = 512
POOLCONV_BWD_TILE = 1024
VMEM_LIMIT = 56 * 1024 * 1024

_NN = (((1,), (0,)), ((), ()))
_NT = (((1,), (1,)), ((), ()))
_TN = (((0,), (0,)), ((), ()))
_ANY = pl.BlockSpec(memory_space=pl.ANY)


def _mm(a, b, dims=_NN):
    return lax.dot_general(a.astype(MXU_DTYPE), b.astype(MXU_DTYPE), dims, preferred_element_type=F32)


def _sig(x):
    return 1.0 / (1.0 + jnp.exp(-x))


def _dsilu(z, s):
    return s * (1.0 + z * (1.0 - s))


def _params(n_grid):
    return pltpu.CompilerParams(dimension_semantics=("arbitrary",) * n_grid, vmem_limit_bytes=VMEM_LIMIT)


def _row_tile(rows, cap):
    t = min(rows, cap)
    while rows % t or t % 8:
        t -= 8
    return t


def _full(shape):
    return pl.BlockSpec(shape, lambda i: (0,) * len(shape))


def _layer_vec(stacked, layer):
    arr = stacked.reshape(stacked.shape[0], 1, stacked.shape[-1])
    return arr, pl.BlockSpec((None, 1, arr.shape[-1]), lambda i: (layer, 0, 0))


def _mesh_pos():
    return lax.axis_index("x"), lax.axis_index("y"), lax.axis_index("c")


class _Exchange:
    def __init__(self, sources):
        self.sources = list(sources)
        self.n = len(self.sources)
        self.gather = [s.ndim == 2 for s in self.sources]

    def out_shapes(self):
        return [jax.ShapeDtypeStruct((N_DEV,) + s.shape[-2:], s.dtype) for s in self.sources]

    def scratch(self):
        if not self.n:
            return []
        return [pltpu.SemaphoreType.DMA((7 * self.n,)), pltpu.SemaphoreType.DMA((7 * self.n,)),
                pltpu.SemaphoreType.DMA((self.n,))]

    def copies(self, src_refs, dst_refs, sems):
        send_sems, recv_sems, local_sems = sems
        x, y, c = _mesh_pos()
        me = 4 * x + 2 * y + c
        out = []
        for a, (src, dst) in enumerate(zip(src_refs, dst_refs)):
            out.append(pltpu.make_async_copy(src if self.gather[a] else src.at[me], dst.at[me], local_sems.at[a]))
            for k in range(1, N_DEV):
                tx, ty, tc = x ^ ((k >> 2) & 1), y ^ ((k >> 1) & 1), c ^ (k & 1)
                out.append(pltpu.make_async_remote_copy(
                    src_ref=src if self.gather[a] else src.at[4 * tx + 2 * ty + tc], dst_ref=dst.at[me],
                    send_sem=send_sems.at[7 * a + k - 1], recv_sem=recv_sems.at[7 * a + k - 1],
                    device_id=(tx, ty, tc), device_id_type=pl.DeviceIdType.MESH))
        return out

    def run(self, refs, first, last):
        if not self.n:
            return
        src_refs, dst_refs, sems = refs

        @pl.when(first)
        def _():
            for cp in self.copies(src_refs, dst_refs, sems):
                cp.start()

        @pl.when(last)
        def _():
            for cp in self.copies(src_refs, dst_refs, sems):
                cp.wait()


def _split_refs(refs, n_in, n_out, exch):
    ins = refs[:n_in]
    srcs = refs[n_in:n_in + exch.n]
    outs = refs[n_in + exch.n:n_in + exch.n + n_out]
    dsts = refs[n_in + exch.n + n_out:n_in + 2 * exch.n + n_out]
    rest = refs[n_in + 2 * exch.n + n_out:]
    sems = rest[len(rest) - 3:] if exch.n else ()
    scratch = rest[:len(rest) - 3] if exch.n else rest
    return ins, outs, scratch, (srcs, dsts, sems)


def _final_exchange(sources):
    exch = _Exchange(sources)

    def body(*refs):
        _, _, _, xrefs = _split_refs(refs, 0, 0, exch)
        for cp in exch.copies(*xrefs):
            cp.start()
        for cp in exch.copies(*xrefs):
            cp.wait()

    return pl.pallas_call(
        body, name="final_exchange", out_shape=exch.out_shapes(),
        in_specs=[_ANY] * exch.n, out_specs=[_ANY] * exch.n, scratch_shapes=exch.scratch(),
    )(*exch.sources)


def _all_gather(shard, name):
    m_per, n = shard.shape

    def body(x_ref, out_ref, send_sems, recv_sems, local_sem):
        x, y, c = _mesh_pos()
        me, sibling = (x, y, c), (x, y, 1 - c)
        chips = [(1 - x, y), (x, 1 - y), (1 - x, 1 - y)]

        def rows(px, py, pc):
            return out_ref.at[pl.ds((4 * px + 2 * py + pc) * m_per, m_per), :]

        def copy(k, block, to, src=None):
            return pltpu.make_async_remote_copy(
                src_ref=rows(*block) if src is None else src, dst_ref=rows(*block),
                send_sem=send_sems.at[k], recv_sem=recv_sems.at[k],
                device_id=to, device_id_type=pl.DeviceIdType.MESH)

        mine = pltpu.make_async_copy(x_ref, rows(*me), local_sem)
        mine.start()
        first = [copy(0, me, sibling, src=x_ref)]
        first += [copy(1 + j, me, (*chip, c), src=x_ref) for j, chip in enumerate(chips)]
        for cp in first:
            cp.start()
        passed = [copy(4 + j, (*chip, c), sibling) for j, chip in enumerate(chips)]
        for j, chip in enumerate(chips):
            copy(1 + j, (*chip, c), me).wait_recv()
            passed[j].start()
        copy(0, sibling, me).wait_recv()
        for j, chip in enumerate(chips):
            copy(4 + j, (*chip, 1 - c), me).wait_recv()
        for cp in first + passed:
            cp.wait_send()
        mine.wait()

    return pl.pallas_call(
        body, name=name,
        out_shape=jax.ShapeDtypeStruct((N_DEV * m_per, n), shard.dtype),
        in_specs=[pl.BlockSpec(memory_space=pltpu.VMEM)],
        out_specs=pl.BlockSpec(memory_space=pltpu.VMEM),
        scratch_shapes=[pltpu.SemaphoreType.DMA((7,)), pltpu.SemaphoreType.DMA((7,)), pltpu.SemaphoreType.DMA],
        compiler_params=pltpu.CompilerParams(vmem_limit_bytes=VMEM_LIMIT),
    )(shard)


def _by_group(lane, v2, v4, v8, v16):
    return jnp.where(lane < 64, v2, jnp.where(lane < 128, v4, jnp.where(lane < 192, v8, v16)))


def _pool_count(t0, n):
    lane = lax.broadcasted_iota(jnp.int32, (1, POOL_WIDTH), 1)
    t = (t0 + lax.broadcasted_iota(jnp.int32, (n, 1), 0)).astype(F32)
    wnd = _by_group(lane, 2.0, 4.0, 8.0, 16.0)
    return jnp.minimum(t + 1.0, wnd)


def _pool_diff(u_ext, t0, ts):
    lane = lax.broadcasted_iota(jnp.int32, (1, POOL_WIDTH), 1)
    s2 = u_ext + pltpu.roll(u_ext, 1, 0)
    s4 = s2 + pltpu.roll(s2, 2, 0)
    s8 = s4 + pltpu.roll(s4, 4, 0)
    s16 = s8 + pltpu.roll(s8, 8, 0)
    pooled = _by_group(lane, s2, s4, s8, s16)[HALO:]
    return pooled / _pool_count(t0, ts) - u_ext[HALO:]


def _pool_diff_bwd(w, ts):
    n = w.shape[0]
    lane = lax.broadcasted_iota(jnp.int32, (1, POOL_WIDTH), 1)
    f2 = w + pltpu.roll(w, n - 1, 0)
    f4 = f2 + pltpu.roll(f2, n - 2, 0)
    f8 = f4 + pltpu.roll(f4, n - 4, 0)
    f16 = f8 + pltpu.roll(f8, n - 8, 0)
    return _by_group(lane, f2, f4, f8, f16)[:ts]


CONV_CHUNK = 64


def _conv_taps():
    return [(8 * m + r, r, m) for r in range(8) for m in range(4) if 8 * m + r < CONV_KERNEL]


def _store_shifted(dst_ref, x, up):
    n = x.shape[0]
    for r in range(8):
        dst_ref[r] = x if r == 0 else pltpu.roll(x, n - r if up else r, 0)


def _anticausal_conv(src_ref, dw_ref, out_ref, n_out):
    def chunk(c, carry):
        t0 = pl.multiple_of(c * CONV_CHUNK, CONV_CHUNK)
        acc = None
        for d, r, m in _conv_taps():
            term = dw_ref[pl.ds(CONV_KERNEL - 1 - d, 1), :] * src_ref[r, pl.ds(t0 + 8 * m, CONV_CHUNK), :]
            acc = term if acc is None else acc + term
        out_ref[pl.ds(t0, CONV_CHUNK), :] = acc
        return carry

    lax.fori_loop(0, n_out // CONV_CHUNK, chunk, 0)


def _depthwise_conv_weight_grad(x_ref, dout_up_ref, acc_ref, n_rows):
    acc_ref[...] = jnp.zeros_like(acc_ref)

    def chunk(c, carry):
        t0 = pl.multiple_of(c * CONV_CHUNK, CONV_CHUNK)
        xv = x_ref[pl.ds(t0, CONV_CHUNK), :]
        for d, r, m in _conv_taps():
            prod = xv * dout_up_ref[r, pl.ds(t0 + 8 * m, CONV_CHUNK), :]
            acc_ref[CONV_KERNEL - 1 - d] += jnp.sum(prod.reshape(CONV_CHUNK // 8, 8, prod.shape[-1]), axis=0)
        return carry

    lax.fori_loop(0, n_rows // CONV_CHUNK, chunk, 0)


def _layer_norm(cv):
    mu = jnp.mean(cv, axis=-1, keepdims=True)
    xc = cv - mu
    var = jnp.mean(xc * xc, axis=-1, keepdims=True)
    rstd = lax.rsqrt(var + EPS)
    return xc * rstd, rstd


PAIR_ROWS = 2 * BLOCK
PAIR_COLS = 4 * BLOCK
ATTN_BLOCKS_PER_STEP = 4


def _attn_blocks_per_step(s_len):
    qb = ATTN_BLOCKS_PER_STEP
    while (s_len // BLOCK) % qb:
        qb //= 2
    return qb


def _pair_rows(v0, v1):
    r = lax.broadcasted_iota(jnp.int32, (PAIR_ROWS, 1), 0)
    return jnp.where(r < BLOCK, v0, v1)


def _fill_attn_bias(bias_ref):
    rows = lax.broadcasted_iota(jnp.int32, (PAIR_ROWS, PAIR_COLS), 0)
    cols = lax.broadcasted_iota(jnp.int32, (PAIR_ROWS, PAIR_COLS), 1)
    key = cols & (2 * BLOCK - 1)
    dist = BLOCK + (rows & (BLOCK - 1)) - key
    in_band = (dist >= 0) & (dist < BLOCK)
    distf = dist.astype(F32)
    second = cols >= 2 * BLOCK
    for kh in range(N_KV_HEADS):
        slope_of = lambda j, a: 2.0 ** -(Q_PER_KV * kh + 2 * j + a + 1)
        slope = jnp.where(rows < BLOCK, jnp.where(second, slope_of(0, 1), slope_of(0, 0)),
                          jnp.where(second, slope_of(1, 1), slope_of(1, 0)))
        bias = -slope * distf
        bias_ref[0, kh] = jnp.where(in_band & (key >= BLOCK), bias, NEG)
        bias_ref[1, kh] = jnp.where(in_band, bias, NEG)


def _pair_block_matrix(x, x_swapped, kh):
    lo = lax.broadcasted_iota(jnp.int32, (1, 2 * HEAD_DIM), 1) < HEAD_DIM
    in_lo, in_hi = (x, x_swapped) if kh == 0 else (x_swapped, x)
    return jnp.concatenate([jnp.where(lo, in_lo, 0.0), jnp.where(lo, 0.0, in_hi)], axis=0).astype(MXU_DTYPE)


def _pair_queries(q, kh):
    return (_pair_stack(q, kh) * SCALE).astype(MXU_DTYPE)


def _pair_stack(a, kh):
    return jnp.concatenate([a[:, 2 * BLOCK * kh: 2 * BLOCK * kh + BLOCK],
                            a[:, 2 * BLOCK * kh + BLOCK: 2 * BLOCK * (kh + 1)]], axis=0)


def _pair_unstack(parts):
    return jnp.concatenate([p[BLOCK * j: BLOCK * (j + 1)] for p in parts for j in range(2)], axis=-1)


def _pair_softmax(s, kh, sinks_ref):
    ps, p_sinks = [], []
    for a in range(2):
        sa = s[:, 2 * BLOCK * a: 2 * BLOCK * (a + 1)]
        sink = _pair_rows(sinks_ref[Q_PER_KV * kh + a], sinks_ref[Q_PER_KV * kh + 2 + a])
        m = jnp.maximum(jnp.max(sa, axis=-1, keepdims=True), sink)
        e = jnp.exp(sa - m)
        es = jnp.exp(sink - m)
        inv = 1.0 / (jnp.sum(e, axis=-1, keepdims=True) + es)
        ps.append(e * inv)
        p_sinks.append(es * inv)
    return jnp.concatenate(ps, axis=-1), p_sinks


def _fold_pair_rows(t):
    return t[:HEAD_DIM, :2 * BLOCK] + t[HEAD_DIM:, 2 * BLOCK:]


def _in_proj(x, g, w_t, carried=()):
    s_len, d = x.shape
    ts = _row_tile(s_len, ROW_TILE)
    nt = s_len // ts
    exch = _Exchange(carried)

    def body(*refs):
        (x_ref, g_ref, w_ref), (o_ref,), _, xrefs = _split_refs(refs, 3, 1, exch)
        i = pl.program_id(0)
        exch.run(xrefs, i == 0, i == nt - 1)
        xv = x_ref[...]
        r = lax.rsqrt(jnp.mean(xv * xv, axis=-1, keepdims=True) + EPS)
        o_ref[...] = _mm(xv * r * g_ref[...], w_ref[...], _NT)

    res = pl.pallas_call(
        body, name="in_proj_carrier" if exch.n else "in_proj", grid=(nt,),
        in_specs=[pl.BlockSpec((ts, d), lambda i: (i, 0)), g[1], _full((D_IN, d))] + [_ANY] * exch.n,
        out_specs=[pl.BlockSpec((ts, D_IN), lambda i: (i, 0))] + [_ANY] * exch.n,
        out_shape=[jax.ShapeDtypeStruct((s_len, D_IN), F32)] + exch.out_shapes(),
        scratch_shapes=exch.scratch(),
        compiler_params=_params(1),
    )(x, g[0], w_t, *exch.sources)
    return res[0], res[1:]


def _poolconv_fwd(proj, wp, scale, dw, cb, lng, lnb, pw, carried=()):
    s_len = proj.shape[0]
    ts = _row_tile(s_len, ROW_TILE)
    hb = ts // HALO
    nt = s_len // ts
    exch = _Exchange(carried)

    def body(*refs):
        ins, (y_ref, cv_ref, diff_ref), _, xrefs = _split_refs(refs, 9, 3, exch)
        p_ref, ph_ref, wp_ref, sc_ref, dw_ref, cb_ref, lng_ref, lnb_ref, pw_ref = ins
        i = pl.program_id(0)
        exch.run(xrefs, i == 0, i == nt - 1)
        halo = jnp.where(i > 0, ph_ref[...], 0.0)
        y_mix, cv, diff = _pool_conv_mixers(p_ref[...], halo, i * ts, ts, wp_ref, sc_ref, dw_ref, cb_ref, lng_ref,
                                            lnb_ref, pw_ref)
        y_ref[...] = y_mix.astype(y_ref.dtype)
        cv_ref[...] = cv
        diff_ref[...] = diff

    res = pl.pallas_call(
        body, name="poolconv_fwd_carrier" if exch.n else "poolconv_fwd", grid=(nt,),
        in_specs=[pl.BlockSpec((ts, HALF_IN), lambda i: (i, 0)),
                  pl.BlockSpec((HALO, HALF_IN), lambda i: (jnp.maximum(i * hb - 1, 0), 0)),
                  _full((256, 256)), scale[1], _full((CONV_TAPS_PAD, 256)), cb[1], lng[1], lnb[1],
                  _full((256, 256))]
        + [_ANY] * exch.n,
        out_specs=[pl.BlockSpec((ts, 512), lambda i: (i, 0)), pl.BlockSpec((ts, 256), lambda i: (i, 0)),
                   pl.BlockSpec((ts, 256), lambda i: (i, 0))] + [_ANY] * exch.n,
        out_shape=[jax.ShapeDtypeStruct((s_len, 512), MXU_DTYPE), jax.ShapeDtypeStruct((s_len, 256), F32),
                   jax.ShapeDtypeStruct((s_len, 256), MXU_DTYPE)] + exch.out_shapes(),
        scratch_shapes=exch.scratch(),
        compiler_params=_params(1),
    )(proj, proj, wp, scale[0], dw, cb[0], lng[0], lnb[0], pw, *exch.sources)
    return res[:3], res[3:]


def _pool_conv_mixers(cur, halo, t0, ts, wp_ref, sc_ref, dw_ref, cb_ref, lng_ref, lnb_ref, pw_ref):
    ext = jnp.concatenate([halo, cur], axis=0)
    diff = _pool_diff(ext[:, 0:256], t0, ts).astype(MXU_DTYPE)
    gp = cur[:, 256:512]
    y_pool = _mm(diff, wp_ref[...]) * sc_ref[...] * (gp * _sig(gp))
    hh = ext[:, 512:768] * _sig(ext[:, 768:1024])
    shifted = [hh if r == 0 else pltpu.roll(hh, r, 0) for r in range(8)]
    cv = cb_ref[...]
    for dist, r, m in _conv_taps():
        cv = cv + dw_ref[pl.ds(CONV_KERNEL - 1 - dist, 1), :] * shifted[r][HALO - 8 * m: HALO - 8 * m + ts]
    n, _ = _layer_norm(cv)
    z = n * lng_ref[...] + lnb_ref[...]
    gc = cur[:, 1024:1280]
    y_conv = _mm(z * _sig(z), pw_ref[...]) * (gc * _sig(gc))
    return jnp.concatenate([y_pool, y_conv], axis=-1), cv, diff


def _attn_fwd(proj, sinks, carried=()):
    s_len = proj.shape[0]
    qb = _attn_blocks_per_step(s_len)
    ts = qb * BLOCK
    nt = s_len // ts
    exch = _Exchange(carried)

    def body(*refs):
        (p_ref, kvp_ref, sinks_ref), (y_ref, ya_ref), (bias_ref,), xrefs = _split_refs(refs, 3, 2, exch)
        i = pl.program_id(0)
        exch.run(xrefs, i == 0, i == nt - 1)

        @pl.when(i == 0)
        def _():
            _fill_attn_bias(bias_ref)

        probs = [(b, kh) for b in range(qb) for kh in range(N_KV_HEADS)]
        scores, v_bds = {}, {}
        for b in range(qb):
            r0 = BLOCK * b
            kv_prev = kvp_ref[...] if b == 0 else p_ref[r0 - BLOCK:r0, 512:768]
            kv2 = jnp.concatenate([kv_prev, p_ref[r0:r0 + BLOCK, 512:768]], axis=0)
            k2, v2 = kv2[:, :BLOCK], kv2[:, BLOCK:]
            k2_swapped, v2_swapped = pltpu.roll(k2, HEAD_DIM, 1), pltpu.roll(v2, HEAD_DIM, 1)
            variant = jnp.where(i == 0, 0, 1) if b == 0 else 1
            q = p_ref[r0:r0 + BLOCK, 0:512]
            for kh in range(N_KV_HEADS):
                k_bd = _pair_block_matrix(k2, k2_swapped, kh)
                v_bds[b, kh] = _pair_block_matrix(v2, v2_swapped, kh)
                scores[b, kh] = _mm(_pair_queries(q, kh), k_bd, _NT) + bias_ref[variant, kh]
        ps = {pr: _pair_softmax(scores[pr], pr[1], sinks_ref)[0] for pr in probs}
        outs = {pr: _mm(ps[pr], v_bds[pr]) for pr in probs}
        for b in range(qb):
            r0 = BLOCK * b
            ga = p_ref[r0:r0 + BLOCK, 768:1280]
            ya = _pair_unstack([outs[b, kh] for kh in range(N_KV_HEADS)])
            ya_ref[r0:r0 + BLOCK, :] = ya
            y_ref[r0:r0 + BLOCK, :] = (ya * (ga * _sig(ga))).astype(y_ref.dtype)

    res = pl.pallas_call(
        body, name="attn_fwd_carrier" if exch.n else "attn_fwd", grid=(nt,),
        in_specs=[pl.BlockSpec((ts, HALF_IN), lambda i: (i, 1)),
                  pl.BlockSpec((BLOCK, 256), lambda i: (jnp.maximum(i * qb - 1, 0), 7)),
                  pl.BlockSpec(memory_space=pltpu.SMEM)] + [_ANY] * exch.n,
        out_specs=[pl.BlockSpec((ts, 512), lambda i: (i, 0)), pl.BlockSpec((ts, 512), lambda i: (i, 0))]
        + [_ANY] * exch.n,
        out_shape=[jax.ShapeDtypeStruct((s_len, 512), MXU_DTYPE), jax.ShapeDtypeStruct((s_len, 512), F32)]
        + exch.out_shapes(),
        scratch_shapes=[pltpu.VMEM((2, N_KV_HEADS, PAIR_ROWS, PAIR_COLS), F32)] + exch.scratch(),
        compiler_params=_params(1),
    )(proj, proj, sinks, *exch.sources)
    return res[:2], res[2:]


def _out_in_proj(x, y_pc, y_at, w_out, g_next, w_t_next):
    s_len, d = x.shape
    ts = _row_tile(s_len, ROW_TILE)

    def body(x_ref, a_ref, b_ref, w_ref, g_ref, wn_ref, o_ref, p_ref):
        y = jnp.concatenate([a_ref[...], b_ref[...]], axis=-1)
        xv = x_ref[...] + _mm(y, w_ref[...])
        o_ref[...] = xv
        r = lax.rsqrt(jnp.mean(xv * xv, axis=-1, keepdims=True) + EPS)
        p_ref[...] = _mm(xv * r * g_ref[...], wn_ref[...], _NT)

    tile = pl.BlockSpec((ts, d), lambda i: (i, 0))
    half = pl.BlockSpec((ts, 512), lambda i: (i, 0))
    return pl.pallas_call(
        body, name="out_in_proj", grid=(s_len // ts,),
        in_specs=[tile, half, half, _full((D_MIX, d)), g_next[1], _full((D_IN, d))],
        out_specs=[tile, pl.BlockSpec((ts, D_IN), lambda i: (i, 0))],
        out_shape=[jax.ShapeDtypeStruct((s_len, d), F32), jax.ShapeDtypeStruct((s_len, D_IN), F32)],
        compiler_params=_params(1),
    )(x, y_pc, y_at, w_out, g_next[0], w_t_next)


def _out_proj_loss(x, y_pc, y_at, w_out, target, g):
    s_len, d = x.shape
    ts = _row_tile(s_len, ROW_TILE)

    def body(x_ref, a_ref, b_ref, w_ref, t_ref, g_ref, sq_ref, dg_ref, dx_ref):
        i = pl.program_id(0)
        y = jnp.concatenate([a_ref[...], b_ref[...]], axis=-1)
        xv = x_ref[...] + _mm(y, w_ref[...])
        gv = g_ref[...]
        r = lax.rsqrt(jnp.mean(xv * xv, axis=-1, keepdims=True) + EPS)
        xr = xv * r
        err = xr * gv - t_ref[...]
        dout = err * (1.0 / d)
        w = dout * gv
        dx_ref[...] = r * (w - xr * jnp.mean(w * xr, axis=-1, keepdims=True))

        @pl.when(i == 0)
        def _():
            sq_ref[...] = jnp.zeros_like(sq_ref)
            dg_ref[...] = jnp.zeros_like(dg_ref)

        sq = jnp.sum(jnp.sum(err * err, axis=-1, keepdims=True), axis=0, keepdims=True)
        sq_ref[...] += jnp.broadcast_to(sq, sq_ref.shape)
        dg_ref[...] += jnp.sum(dout * xr, axis=0, keepdims=True)

    tile = pl.BlockSpec((ts, d), lambda i: (i, 0))
    half = pl.BlockSpec((ts, 512), lambda i: (i, 0))
    return pl.pallas_call(
        body, name="out_proj_loss", grid=(s_len // ts,),
        in_specs=[tile, half, half, _full((D_MIX, d)), tile, _full((1, d))],
        out_specs=[_full((1, 128)), _full((1, d)), tile],
        out_shape=[jax.ShapeDtypeStruct((1, 128), F32), jax.ShapeDtypeStruct((1, d), F32),
                   jax.ShapeDtypeStruct((s_len, d), F32)],
        compiler_params=_params(1),
    )(x, y_pc, y_at, w_out, target, g)


def _out_bwd(dxo, y_pc, y_at, w_out):
    s_len, d = dxo.shape
    ts = _row_tile(s_len, ROW_TILE)
    nt = s_len // ts

    def body(dx_ref, a_ref, b_ref, w_ref, dy_ref, gw_ref, acc_ref):
        i = pl.program_id(0)
        dxv = dx_ref[...].astype(MXU_DTYPE)
        dy_ref[...] = _mm(dxv, w_ref[...], _NT)

        @pl.when(i == 0)
        def _():
            acc_ref[...] = jnp.zeros_like(acc_ref)

        y = jnp.concatenate([a_ref[...], b_ref[...]], axis=-1)
        acc_ref[...] += _mm(y, dxv, _TN)

        @pl.when(i == nt - 1)
        def _():
            gw_ref[...] = acc_ref[...].astype(gw_ref.dtype)

    return pl.pallas_call(
        body, name="out_bwd", grid=(nt,),
        in_specs=[pl.BlockSpec((ts, d), lambda i: (i, 0)), pl.BlockSpec((ts, 512), lambda i: (i, 0)),
                  pl.BlockSpec((ts, 512), lambda i: (i, 0)), _full((D_MIX, d))],
        out_specs=[pl.BlockSpec((ts, D_MIX), lambda i: (i, 0)), _full((D_MIX, d))],
        out_shape=[jax.ShapeDtypeStruct((s_len, D_MIX), F32), jax.ShapeDtypeStruct((D_MIX, d), EXCHANGE_DTYPE)],
        scratch_shapes=[pltpu.VMEM((D_MIX, d), F32)],
        compiler_params=_params(1),
    )(dxo, y_pc, y_at, w_out)


def _poolconv_bwd(proj, cv, diff, dy, wp, scale, dw, lng, lnb, pw, carried=()):
    s_len = proj.shape[0]
    ts = _row_tile(s_len, POOLCONV_BWD_TILE)
    hb = ts // HALO
    nt = s_len // ts
    last_halo = s_len // HALO - 1
    n = ts + HALO
    exch = _Exchange(carried)

    def body(*refs):
        ins, outs, (up_ref, hh_ref, dhh_ref, gdw_acc_ref), xrefs = _split_refs(refs, 13, 5, exch)
        (p_ref, pn_ref, cv_ref, cvn_ref, diff_ref, dy_ref, dyn_ref, wp_ref, sc_ref, dw_ref, lng_ref, lnb_ref,
         pw_ref) = ins
        da_ref, gwp_ref, gpw_ref, gdw_ref, gvec_ref = outs
        i = pl.program_id(0)
        exch.run(xrefs, i == 0, i == nt - 1)
        has_next = i < nt - 1
        cur = p_ref[...]
        nxt = jnp.where(has_next, pn_ref[...], 0.0)
        dyx = jnp.concatenate([dy_ref[...], jnp.where(has_next, dyn_ref[...], 0.0)], axis=0)
        row = lax.broadcasted_iota(jnp.int32, (n, 1), 0)
        in_seq = (row < ts) | has_next
        scale_v = sc_ref[...]

        cvx = jnp.concatenate([cv_ref[...], jnp.where(has_next, cvn_ref[...], 0.0)], axis=0)
        nrm, rstd = _layer_norm(cvx)
        z = nrm * lng_ref[...] + lnb_ref[...]
        sz = _sig(z)
        sw = z * sz
        gc = jnp.concatenate([cur[:, 1024:1280], nxt[:, 1024:1280]], axis=0)
        sgc = _sig(gc)
        yc = _mm(sw, pw_ref[...])
        dyc = dyx[:, 256:512]
        d_yc = dyc * (gc * sgc)
        d_gc = (dyc * yc * _dsilu(gc, sgc))[:ts]
        d_z = _mm(d_yc, pw_ref[...], _NT) * _dsilu(z, sz)
        d_n = d_z * lng_ref[...]
        d_cv = rstd * (d_n - jnp.mean(d_n, axis=-1, keepdims=True)
                       - nrm * jnp.mean(d_n * nrm, axis=-1, keepdims=True))
        d_cv = jnp.where(in_seq, d_cv, 0.0)
        _store_shifted(up_ref, d_cv, up=True)
        _anticausal_conv(up_ref, dw_ref, dhh_ref, ts)
        d_hh = dhh_ref[...]
        a_c, sb_c = cur[:, 512:768], _sig(cur[:, 768:1024])
        hh_ref[...] = a_c * sb_c
        d_a = d_hh * sb_c
        d_b = d_hh * a_c * sb_c * (1.0 - sb_c)
        d_cv_t = d_cv[:ts]
        _depthwise_conv_weight_grad(hh_ref, up_ref, gdw_acc_ref, ts)

        diff = diff_ref[...]
        raw = _mm(diff, wp_ref[...])
        gp = jnp.concatenate([cur[:, 256:512], nxt[:, 256:512]], axis=0)
        sgp = _sig(gp)
        dyp = dyx[:, 0:256]
        d_yp = dyp * (gp * sgp)
        d_gp = dyp[:ts] * (raw * scale_v) * _dsilu(gp, sgp)[:ts]
        d_raw = d_yp * scale_v
        d_diff = _mm(d_raw, wp_ref[...], _NT)
        w = jnp.where(in_seq, d_diff / _pool_count(i * ts, n), 0.0)
        d_u = _pool_diff_bwd(w, ts) - d_diff[:ts]

        da_ref[...] = jnp.concatenate([d_u, d_gp, d_a, d_b, d_gc], axis=-1).astype(da_ref.dtype)

        @pl.when(i == 0)
        def _():
            gwp_ref[...] = jnp.zeros_like(gwp_ref)
            gpw_ref[...] = jnp.zeros_like(gpw_ref)
            gdw_ref[...] = jnp.zeros_like(gdw_ref)
            gvec_ref[...] = jnp.zeros_like(gvec_ref)

        gwp_ref[...] += _mm(diff, d_raw[:ts], _TN)
        gpw_ref[...] += _mm(sw[:ts], d_yc[:ts], _TN)
        gdw_ref[...] += jnp.sum(gdw_acc_ref[...], axis=1)
        zero_row = jnp.zeros((1, 256), F32)
        gvec_ref[...] += jnp.concatenate([
            jnp.sum(d_yp[:ts] * raw, axis=0, keepdims=True),
            jnp.sum(d_cv_t, axis=0, keepdims=True),
            jnp.sum((d_z * nrm)[:ts], axis=0, keepdims=True),
            jnp.sum(d_z[:ts], axis=0, keepdims=True),
            zero_row, zero_row, zero_row, zero_row], axis=0)

    nxt_halo = lambda i: (jnp.minimum((i + 1) * hb, last_halo), 0)
    res = pl.pallas_call(
        body, name="poolconv_bwd_carrier" if exch.n else "poolconv_bwd", grid=(nt,),
        in_specs=[pl.BlockSpec((ts, HALF_IN), lambda i: (i, 0)), pl.BlockSpec((HALO, HALF_IN), nxt_halo),
                  pl.BlockSpec((ts, 256), lambda i: (i, 0)), pl.BlockSpec((HALO, 256), nxt_halo),
                  pl.BlockSpec((ts, 256), lambda i: (i, 0)),
                  pl.BlockSpec((ts, 512), lambda i: (i, 0)), pl.BlockSpec((HALO, 512), nxt_halo),
                  _full((256, 256)), scale[1], _full((CONV_TAPS_PAD, 256)), lng[1], lnb[1], _full((256, 256))]
        + [_ANY] * exch.n,
        out_specs=[pl.BlockSpec((ts, HALF_IN), lambda i: (i, 0)), _full((256, 256)), _full((256, 256)),
                   _full((CONV_TAPS_PAD, 256)), _full((8, 256))] + [_ANY] * exch.n,
        out_shape=[jax.ShapeDtypeStruct((s_len, HALF_IN), MXU_DTYPE), jax.ShapeDtypeStruct((256, 256), F32),
                   jax.ShapeDtypeStruct((256, 256), F32), jax.ShapeDtypeStruct((CONV_TAPS_PAD, 256), F32),
                   jax.ShapeDtypeStruct((8, 256), F32)] + exch.out_shapes(),
        scratch_shapes=[pltpu.VMEM((8, n, CONV_WIDTH), F32), pltpu.VMEM((ts, CONV_WIDTH), F32),
                        pltpu.VMEM((ts, CONV_WIDTH), F32), pltpu.VMEM((CONV_TAPS_PAD, 8, CONV_WIDTH), F32)]
        + exch.scratch(),
        compiler_params=_params(1),
    )(proj, proj, cv, cv, diff, dy, dy, wp, scale[0], dw, lng[0], lnb[0], pw, *exch.sources)
    return res[:5], res[5:]


DQ0, DKC0, DVC0, DKP0, DVP0, DGA0, DATTN_W = 0, 512, 640, 768, 896, 1024, 1536


def _attn_bwd(proj, ya, dy, sinks, carried=()):
    s_len = proj.shape[0]
    qb = _attn_blocks_per_step(s_len)
    ts = qb * BLOCK
    nt = s_len // ts
    exch = _Exchange(carried)

    def body(*refs):
        ins, (o_ref, gs_ref), (bias_ref,), xrefs = _split_refs(refs, 5, 2, exch)
        p_ref, kvp_ref, ya_ref, dy_ref, sinks_ref = ins
        i = pl.program_id(0)
        exch.run(xrefs, i == 0, i == nt - 1)

        @pl.when(i == 0)
        def _():
            _fill_attn_bias(bias_ref)
            gs_ref[...] = jnp.zeros_like(gs_ref)

        probs = [(b, kh) for b in range(qb) for kh in range(N_KV_HEADS)]
        kv_heads = range(N_KV_HEADS)
        scores, k_bds, v_bds, q2s, do2s, dyas, gas, sgas = {}, {}, {}, {}, {}, {}, {}, {}
        for b in range(qb):
            r0 = BLOCK * b
            kv_prev = kvp_ref[...] if b == 0 else p_ref[r0 - BLOCK:r0, 512:768]
            kv2 = jnp.concatenate([kv_prev, p_ref[r0:r0 + BLOCK, 512:768]], axis=0)
            k2, v2 = kv2[:, :BLOCK], kv2[:, BLOCK:]
            k2_swapped, v2_swapped = pltpu.roll(k2, HEAD_DIM, 1), pltpu.roll(v2, HEAD_DIM, 1)
            variant = jnp.where(i == 0, 0, 1) if b == 0 else 1
            q = p_ref[r0:r0 + BLOCK, 0:512]
            gas[b] = p_ref[r0:r0 + BLOCK, 768:1280]
            dyas[b] = dy_ref[r0:r0 + BLOCK, :]
            sgas[b] = _sig(gas[b])
            d_o = dyas[b] * (gas[b] * sgas[b])
            for kh in kv_heads:
                k_bds[b, kh] = _pair_block_matrix(k2, k2_swapped, kh)
                v_bds[b, kh] = _pair_block_matrix(v2, v2_swapped, kh)
                q2s[b, kh] = _pair_queries(q, kh)
                do2s[b, kh] = _pair_stack(d_o, kh)
                scores[b, kh] = _mm(q2s[b, kh], k_bds[b, kh], _NT) + bias_ref[variant, kh]
        ps, p_sinks, dps, dss, dqs, dks, dvs = {}, {}, {}, {}, {}, {}, {}
        d_sinks = [None] * N_Q_HEADS

        def softmax_stage(pr):
            ps[pr], p_sinks[pr] = _pair_softmax(scores[pr], pr[1], sinks_ref)

        def value_stage(pr):
            dps[pr] = _mm(do2s[pr], v_bds[pr], _NT)

        def score_grad_stage(pr):
            p, dp, kh = ps[pr], dps[pr], pr[1]
            ds_halves = []
            for a in range(2):
                cols = slice(2 * BLOCK * a, 2 * BLOCK * (a + 1))
                delta = jnp.sum(p[:, cols] * dp[:, cols], axis=-1, keepdims=True)
                ds_halves.append(p[:, cols] * (dp[:, cols] - delta))
                dsink = -p_sinks[pr][a] * delta
                for j in range(2):
                    part = jnp.sum(dsink[BLOCK * j: BLOCK * (j + 1)], axis=0, keepdims=True)
                    h = Q_PER_KV * kh + 2 * j + a
                    d_sinks[h] = part if d_sinks[h] is None else d_sinks[h] + part
            dss[pr] = jnp.concatenate(ds_halves, axis=-1)

        def operand_grad_stage(pr):
            dqs[pr] = _mm(dss[pr], k_bds[pr]) * SCALE
            dks[pr] = _fold_pair_rows(_mm(q2s[pr], dss[pr], _TN))
            dvs[pr] = _fold_pair_rows(_mm(do2s[pr], ps[pr], _TN))

        for stage in (softmax_stage, value_stage, score_grad_stage, operand_grad_stage):
            for pr in probs:
                stage(pr)
        for b in range(qb):
            r0 = BLOCK * b
            dk = jnp.concatenate([dks[b, kh] for kh in kv_heads], axis=0).T
            dv = jnp.concatenate([dvs[b, kh] for kh in kv_heads], axis=0).T
            d_ga = dyas[b] * ya_ref[r0:r0 + BLOCK, :] * _dsilu(gas[b], sgas[b])
            o_ref[r0:r0 + BLOCK, :] = jnp.concatenate(
                [_pair_unstack([dqs[b, kh] for kh in kv_heads]), dk[BLOCK:], dv[BLOCK:], dk[:BLOCK], dv[:BLOCK],
                 d_ga], axis=-1)
        gs_ref[...] += jnp.broadcast_to(jnp.concatenate(d_sinks, axis=0), gs_ref.shape)

    res = pl.pallas_call(
        body, name="attn_bwd_carrier" if exch.n else "attn_bwd", grid=(nt,),
        in_specs=[pl.BlockSpec((ts, HALF_IN), lambda i: (i, 1)),
                  pl.BlockSpec((BLOCK, 256), lambda i: (jnp.maximum(i * qb - 1, 0), 7)),
                  pl.BlockSpec((ts, 512), lambda i: (i, 0)), pl.BlockSpec((ts, 512), lambda i: (i, 1)),
                  pl.BlockSpec(memory_space=pltpu.SMEM)] + [_ANY] * exch.n,
        out_specs=[pl.BlockSpec((ts, DATTN_W), lambda i: (i, 0)), _full((N_Q_HEADS, 128))] + [_ANY] * exch.n,
        out_shape=[jax.ShapeDtypeStruct((s_len, DATTN_W), F32), jax.ShapeDtypeStruct((N_Q_HEADS, 128), F32)]
        + exch.out_shapes(),
        scratch_shapes=[pltpu.VMEM((2, N_KV_HEADS, PAIR_ROWS, PAIR_COLS), F32)] + exch.scratch(),
        compiler_params=_params(1),
    )(proj, proj, ya, dy, sinks, *exch.sources)
    return res[:2], res[2:]


def _in_bwd(da, dattn, x, dxo, g, w_t):
    s_len, d = x.shape
    ts = _row_tile(s_len, IN_BWD_TILE)
    bpt = ts // BLOCK
    nt = s_len // ts
    last_block = s_len // BLOCK - 1

    def body(da_ref, dat_ref, nxt_ref, x_ref, dxo_ref, g_ref, w_ref, dx_ref, dg_ref, gw_ref, acc_ref, stage_ref,
             stage_sem):
        i = pl.program_id(0)
        dat = dat_ref[...]
        nxt = jnp.where(i < nt - 1, nxt_ref[...], 0.0)
        shifted = jnp.concatenate([dat[BLOCK:, DKP0:DGA0], nxt], axis=0) if bpt > 1 else nxt
        dkv = dat[:, DKC0:DKP0] + shifted
        dproj = jnp.concatenate([da_ref[...], dat[:, DQ0:DKC0].astype(MXU_DTYPE), dkv.astype(MXU_DTYPE),
                                 dat[:, DGA0:DATTN_W].astype(MXU_DTYPE)], axis=-1)
        d_h = _mm(dproj, w_ref[...])
        xv = x_ref[...]
        gv = g_ref[...]
        r = lax.rsqrt(jnp.mean(xv * xv, axis=-1, keepdims=True) + EPS)
        xr = xv * r
        w = d_h * gv
        dx_ref[...] = dxo_ref[...] + r * (w - xr * jnp.mean(w * xr, axis=-1, keepdims=True))

        @pl.when(i == 0)
        def _():
            dg_ref[...] = jnp.zeros_like(dg_ref)
            acc_ref[...] = jnp.zeros_like(acc_ref)

        dg_ref[...] += jnp.sum(d_h * xr, axis=0, keepdims=True)
        acc_ref[...] += _mm(dproj, xr * gv, _TN)

        @pl.when(i == nt - 1)
        def _():
            stage_ref[...] = acc_ref[...].astype(stage_ref.dtype)
            out = pltpu.make_async_copy(stage_ref, gw_ref, stage_sem)
            out.start()
            out.wait()

    return pl.pallas_call(
        body, name="in_bwd", grid=(nt,),
        in_specs=[pl.BlockSpec((ts, HALF_IN), lambda i: (i, 0)),
                  pl.BlockSpec((ts, DATTN_W), lambda i: (i, 0)),
                  pl.BlockSpec((BLOCK, 256), lambda i: (jnp.minimum((i + 1) * bpt, last_block), 3)),
                  pl.BlockSpec((ts, d), lambda i: (i, 0)), pl.BlockSpec((ts, d), lambda i: (i, 0)),
                  g[1], pl.BlockSpec((D_IN, d), lambda i: (0, 0), pipeline_mode=pl.Buffered(1))],
        out_specs=[pl.BlockSpec((ts, d), lambda i: (i, 0)), _full((1, d)), _ANY],
        out_shape=[jax.ShapeDtypeStruct((s_len, d), F32), jax.ShapeDtypeStruct((1, d), F32),
                   jax.ShapeDtypeStruct((D_IN, d), EXCHANGE_DTYPE)],
        scratch_shapes=[pltpu.VMEM((D_IN, d), F32), pltpu.VMEM((D_IN, d), EXCHANGE_DTYPE), pltpu.SemaphoreType.DMA],
        compiler_params=_params(1),
    )(da, dattn, dattn, x, dxo, g[0], w_t)


def _in_bwd_dw(da, dattn, x, g):
    s_len, d = x.shape
    ts = _row_tile(s_len, IN_BWD_TILE)
    bpt = ts // BLOCK
    nt = s_len // ts
    last_block = s_len // BLOCK - 1

    def body(da_ref, dat_ref, nxt_ref, x_ref, g_ref, dp_ref, gw_ref, acc_ref, stage_ref, stage_sem):
        i = pl.program_id(0)
        dat = dat_ref[...]
        nxt = jnp.where(i < nt - 1, nxt_ref[...], 0.0)
        shifted = jnp.concatenate([dat[BLOCK:, DKP0:DGA0], nxt], axis=0) if bpt > 1 else nxt
        dkv = dat[:, DKC0:DKP0] + shifted
        dproj = jnp.concatenate([da_ref[...], dat[:, DQ0:DKC0].astype(MXU_DTYPE), dkv.astype(MXU_DTYPE),
                                 dat[:, DGA0:DATTN_W].astype(MXU_DTYPE)], axis=-1)
        dp_ref[...] = dproj
        xv = x_ref[...]
        r = lax.rsqrt(jnp.mean(xv * xv, axis=-1, keepdims=True) + EPS)

        @pl.when(i == 0)
        def _():
            acc_ref[...] = jnp.zeros_like(acc_ref)

        acc_ref[...] += _mm(dproj, xv * r * g_ref[...], _TN)

        @pl.when(i == nt - 1)
        def _():
            stage_ref[...] = acc_ref[...].astype(stage_ref.dtype)
            out = pltpu.make_async_copy(stage_ref, gw_ref, stage_sem)
            out.start()
            out.wait()

    return pl.pallas_call(
        body, name="in_bwd_dw", grid=(nt,),
        in_specs=[pl.BlockSpec((ts, HALF_IN), lambda i: (i, 0)),
                  pl.BlockSpec((ts, DATTN_W), lambda i: (i, 0)),
                  pl.BlockSpec((BLOCK, 256), lambda i: (jnp.minimum((i + 1) * bpt, last_block), 3)),
                  pl.BlockSpec((ts, d), lambda i: (i, 0)), g[1]],
        out_specs=[pl.BlockSpec((ts, D_IN), lambda i: (i, 0)), _ANY],
        out_shape=[jax.ShapeDtypeStruct((s_len, D_IN), MXU_DTYPE), jax.ShapeDtypeStruct((D_IN, d), EXCHANGE_DTYPE)],
        scratch_shapes=[pltpu.VMEM((D_IN, d), F32), pltpu.VMEM((D_IN, d), EXCHANGE_DTYPE), pltpu.SemaphoreType.DMA],
        compiler_params=_params(1),
    )(da, dattn, dattn, x, g[0])


def _in_bwd_dx(dproj, x, dxo, g, w_t, carried=()):
    s_len, d = x.shape
    ts = _row_tile(s_len, ROW_TILE)
    nt = s_len // ts
    exch = _Exchange(carried)

    def body(*refs):
        (dp_ref, x_ref, dxo_ref, g_ref, w_ref), (dx_ref, dg_ref), _, xrefs = _split_refs(refs, 5, 2, exch)
        i = pl.program_id(0)
        exch.run(xrefs, i == 0, i == nt - 1)
        d_h = _mm(dp_ref[...], w_ref[...])
        xv = x_ref[...]
        r = lax.rsqrt(jnp.mean(xv * xv, axis=-1, keepdims=True) + EPS)
        xr = xv * r
        w = d_h * g_ref[...]
        dx_ref[...] = dxo_ref[...] + r * (w - xr * jnp.mean(w * xr, axis=-1, keepdims=True))

        @pl.when(i == 0)
        def _():
            dg_ref[...] = jnp.zeros_like(dg_ref)

        dg_ref[...] += jnp.sum(d_h * xr, axis=0, keepdims=True)

    tile = pl.BlockSpec((ts, d), lambda i: (i, 0))
    res = pl.pallas_call(
        body, name="in_bwd_dx_carrier" if exch.n else "in_bwd_dx", grid=(nt,),
        in_specs=[pl.BlockSpec((ts, D_IN), lambda i: (i, 0)), tile, tile, g[1], _full((D_IN, d))]
        + [_ANY] * exch.n,
        out_specs=[tile, _full((1, d))] + [_ANY] * exch.n,
        out_shape=[jax.ShapeDtypeStruct((s_len, d), F32), jax.ShapeDtypeStruct((1, d), F32)] + exch.out_shapes(),
        scratch_shapes=exch.scratch(),
        compiler_params=_params(1),
    )(dproj, x, dxo, g[0], w_t, *exch.sources)
    return res[:2], res[2:]


def _sum_partials(p_ref):
    g = p_ref[0].astype(F32)
    for k in range(1, N_DEV):
        g = g + p_ref[k].astype(F32)
    return g


def _adamw_step(g, w, m, v):
    nm = ADAM_B1 * m + (1.0 - ADAM_B1) * g
    nv = ADAM_B2 * v + (1.0 - ADAM_B2) * (g * g)
    m_hat = nm / (1.0 - ADAM_B1 ** ADAM_STEP)
    v_hat = nv / (1.0 - ADAM_B2 ** ADAM_STEP)
    return -ADAM_LR * (m_hat / (jnp.sqrt(v_hat) + ADAM_EPS) + ADAM_WD * w), nm, nv


def _adamw_layers(parts, w, m, v, name):
    _, rows, n = w.shape
    tr = _row_tile(rows, 64)

    def body(*refs):
        p_refs = refs[:DEPTH]
        w_ref, m_ref, v_ref, g_ref, d_ref, nm_ref, nv_ref = refs[DEPTH:]
        for l in range(DEPTH):
            g = _sum_partials(p_refs[l])
            g_ref[l] = g
            d_ref[l], nm_ref[l], nv_ref[l] = _adamw_step(g, w_ref[l], m_ref[l], v_ref[l])

    tile = pl.BlockSpec((DEPTH, tr, n), lambda i: (0, i, 0))
    shape = jax.ShapeDtypeStruct(w.shape, F32)
    return pl.pallas_call(
        body, name=name, grid=(rows // tr,),
        in_specs=[pl.BlockSpec((N_DEV, tr, n), lambda i: (0, i, 0))] * DEPTH + [tile] * 3,
        out_specs=[tile] * 4, out_shape=[shape] * 4,
        compiler_params=_params(1),
    )(*parts, w, m, v)


def _sum_rows(parts):
    def body(p_ref, o_ref):
        o_ref[...] = _sum_partials(p_ref)

    vmem = pl.BlockSpec(memory_space=pltpu.VMEM)
    return pl.pallas_call(body, name="sum_replicated_partials", in_specs=[vmem], out_specs=vmem,
                          out_shape=jax.ShapeDtypeStruct(parts.shape[1:], F32))(parts)


def _adamw_tensors(gs, ws, ms, vs):
    n = len(ws)
    shapes = [w.shape for w in ws]
    two_d = lambda a: a.reshape(1, -1) if a.ndim == 1 else a
    ops = [two_d(a) for a in list(gs) + list(ws) + list(ms) + list(vs)]

    def body(*refs):
        ins, outs = refs[:4 * n], refs[4 * n:]
        for k in range(n):
            g, w, m, v = (ins[j * n + k][...] for j in range(4))
            outs[k][...], outs[n + k][...], outs[2 * n + k][...] = _adamw_step(g, w, m, v)

    vmem = pl.BlockSpec(memory_space=pltpu.VMEM)
    res = pl.pallas_call(
        body, name="adamw_replicated", in_specs=[vmem] * (4 * n), out_specs=[vmem] * (3 * n),
        out_shape=[jax.ShapeDtypeStruct(o.shape, F32) for o in ops[n:2 * n]] * 3,
    )(*ops)
    return [[res[j * n + k].reshape(shapes[k]) for k in range(n)] for j in range(3)]


def _pad_rows(a, mult):
    pad = (-a.shape[0]) % mult
    return a if pad == 0 else jnp.concatenate([a, jnp.zeros((pad, a.shape[1]), a.dtype)], axis=0)


def _dw_rows(conv_dw_l):
    return jnp.pad(jnp.swapaxes(conv_dw_l, 0, 1), ((0, 0), (0, CONV_TAPS_PAD - CONV_KERNEL)))


def _pack_small(pw_l, dw_l, d):
    rows = jnp.concatenate([pw_l.reshape(-1, d), _dw_rows(dw_l).reshape(-1, d)], axis=0)
    return _pad_rows(rows, 8)


def _small_slabs(g_pw, g_dw, d):
    a = g_pw.reshape(N_DEV, -1, d)
    b = jnp.swapaxes(g_dw, 0, 1).reshape(N_DEV, -1, d)
    used = a.shape[1] + b.shape[1]
    return jnp.concatenate([a, b, jnp.zeros((N_DEV, (-used) % 8, d), g_pw.dtype)], axis=1)


def _unpack_small(rows, d):
    c = CONV_WIDTH // N_DEV
    n_pw = c * CONV_WIDTH // d
    n_dw = c * CONV_TAPS_PAD // d
    pw = rows[:n_pw].reshape(c, CONV_WIDTH)
    dw = jnp.swapaxes(rows[n_pw:n_pw + n_dw].reshape(c, CONV_TAPS_PAD), 0, 1)[:CONV_KERNEL]
    return pw, dw


def _pack_replicated(ln_g, pool_w, pool_scale, conv_b, conv_ln_g, conv_ln_b, attn_sinks, final_g, scalar, d):
    sinks = jnp.pad(attn_sinks, ((0, 0), (0, 256 - N_Q_HEADS)))
    small = jnp.concatenate([pool_scale, conv_b, conv_ln_g, conv_ln_b, sinks], axis=0)
    small = _pad_rows(small, d // 256)
    last = jnp.pad(scalar.reshape(1, 1), ((0, 0), (0, d - 1)))
    return _pad_rows(jnp.concatenate([ln_g.reshape(-1, d), final_g.reshape(-1, d), pool_w.reshape(-1, d),
                                      small.reshape(-1, d), last], axis=0), 8)


def _unpack_replicated(rows, d):
    n_pool = DEPTH * 4 * POOL_GROUP * POOL_GROUP // d
    n_small = -(-5 * DEPTH * 256 // d)
    ln_g = rows[:DEPTH]
    final_g = rows[DEPTH]
    pool_w = rows[DEPTH + 1: DEPTH + 1 + n_pool].reshape(DEPTH, 4, POOL_GROUP, POOL_GROUP)
    small = rows[DEPTH + 1 + n_pool: DEPTH + 1 + n_pool + n_small].reshape(-1, 256)[: 5 * DEPTH]
    pool_scale, conv_b, conv_ln_g, conv_ln_b = (small[DEPTH * k: DEPTH * (k + 1)] for k in range(4))
    sinks = small[4 * DEPTH: 5 * DEPTH, :N_Q_HEADS]
    scalar = rows[DEPTH + 1 + n_pool + n_small, 0]
    return ln_g, pool_w, pool_scale, conv_b, conv_ln_g, conv_ln_b, sinks, final_g, scalar


def _block_diag(pool_w):
    wide = jnp.tile(pool_w.reshape(POOL_WIDTH, POOL_GROUP), (1, POOL_WIDTH // POOL_GROUP))
    rows = lax.broadcasted_iota(jnp.int32, wide.shape, 0) // POOL_GROUP
    cols = lax.broadcasted_iota(jnp.int32, wide.shape, 1) // POOL_GROUP
    return jnp.where(rows == cols, wide, jnp.zeros_like(wide))


def _diag_blocks(mat):
    return jnp.stack([mat[POOL_GROUP * gi: POOL_GROUP * (gi + 1), POOL_GROUP * gi: POOL_GROUP * (gi + 1)]
                      for gi in range(4)], axis=0)


def kernel(x, ln_g, w_in, pool_w, pool_scale, conv_dw, conv_b, conv_ln_g, conv_ln_b, conv_pw, attn_sinks, w_out, final_g, loss_target, m_ln_g, m_w_in, m_pool_w, m_pool_scale, m_conv_dw, m_conv_b, m_conv_ln_g, m_conv_ln_b, m_conv_pw, m_attn_sinks, m_w_out, m_final_g, v_ln_g, v_w_in, v_pool_w, v_pool_scale, v_conv_dw, v_conv_b, v_conv_ln_g, v_conv_ln_b, v_conv_pw, v_attn_sinks, v_w_out, v_final_g):
    x0 = x[0]
    d = x0.shape[1]
    row = lambda a: a.reshape(1, -1)
    vec = _layer_vec
    slabs = lambda a: a.reshape(N_DEV, a.shape[0] // N_DEV, d)
    c_shard = CONV_WIDTH // N_DEV

    w_in_rows = [jnp.swapaxes(w_in[l], 0, 1).astype(MXU_DTYPE) for l in range(DEPTH)]
    w_out_rows = [w_out[l].astype(MXU_DTYPE) for l in range(DEPTH)]
    per_word = 4 // jnp.dtype(MXU_DTYPE).itemsize
    dw_t = jnp.stack([_dw_rows(conv_dw[l]) for l in range(DEPTH)], axis=0)
    dw_bits = (lax.bitcast_convert_type(dw_t, MXU_DTYPE) if per_word > 1 else dw_t).reshape(-1, d)
    n_pw = DEPTH * c_shard * CONV_WIDTH // d
    conv_rows = _pad_rows(jnp.concatenate([conv_pw.reshape(-1, d).astype(MXU_DTYPE), dw_bits], axis=0), 16)
    w_in_t, w_out_f = [None] * DEPTH, [None] * DEPTH
    w_in_t[0] = _all_gather(w_in_rows[0], "w_in_all_gather")
    wp_bd = [_block_diag(pool_w[l]).astype(MXU_DTYPE) for l in range(DEPTH)]

    xs, projs, cvs, ys = [x0], [], [], []
    q4 = D_IN // N_DEV // 4
    w_in_1 = [w_in_rows[1][q4 * k: q4 * (k + 1)] for k in range(4)]
    for l in range(DEPTH):
        if l == 0:
            proj, got = _in_proj(xs[0], vec(ln_g, 0), w_in_t[0], carried=[w_out_rows[0], conv_rows, w_in_1[0]])
            w_in_1_got = [got[2]]
            w_out_f[0] = got[0].reshape(D_MIX, d)
            pw_all = got[1][:, :n_pw].reshape(N_DEV, DEPTH, c_shard, CONV_WIDTH)
            pw_f = [pw_all[:, k].reshape(CONV_WIDTH, CONV_WIDTH) for k in range(DEPTH)]
            bits = got[1][:, n_pw:n_pw + dw_bits.shape[0]].reshape(
                (N_DEV, DEPTH, c_shard, CONV_TAPS_PAD) + (per_word,) * (per_word > 1))
            dw_all = lax.bitcast_convert_type(bits, F32) if per_word > 1 else bits
            dw_f = [jnp.swapaxes(dw_all[:, k].reshape(CONV_WIDTH, CONV_TAPS_PAD), 0, 1) for k in range(DEPTH)]
        (y_pc, cv, diff), got = _poolconv_fwd(proj, wp_bd[l], vec(pool_scale, l), dw_f[l], vec(conv_b, l),
                                              vec(conv_ln_g, l), vec(conv_ln_b, l), pw_f[l],
                                              carried=[w_in_1[1], w_in_1[2]] if l == 0 else [])
        if l == 0:
            w_in_1_got += list(got)
        (y_at, ya), got = _attn_fwd(proj, attn_sinks[l], carried=[w_out_rows[1], w_in_1[3]] if l == 0 else [])
        if l == 0:
            w_out_f[1] = got[0].reshape(D_MIX, d)
            w_in_t[1] = jnp.concatenate(w_in_1_got + [got[1]], axis=1).reshape(D_IN, d)
        projs.append(proj)
        cvs.append((cv, diff))
        ys.append((y_pc, y_at, ya))
        if l < DEPTH - 1:
            x_next, proj = _out_in_proj(xs[l], y_pc, y_at, w_out_f[l], vec(ln_g, l + 1), w_in_t[l + 1])
            xs.append(x_next)

    l = DEPTH - 1
    sq, g_final, dx = _out_proj_loss(xs[l], ys[l][0], ys[l][1], w_out_f[l], loss_target[0], row(final_g))

    l = 1
    dy, g_wout1 = _out_bwd(dx, ys[l][0], ys[l][1], w_out_f[l])
    (da, g_wp1, g_pw1, g_dw1, g_vec1), _ = _poolconv_bwd(
        projs[l], *cvs[l], dy, wp_bd[l], vec(pool_scale, l), dw_f[l], vec(conv_ln_g, l), vec(conv_ln_b, l), pw_f[l])
    (dattn, gs1), _ = _attn_bwd(projs[l], ys[l][2], dy, attn_sinks[l])
    dx, g_ln1, g_win_t1 = _in_bwd(da, dattn, xs[l], dx, vec(ln_g, l), w_in_t[l])
    l = 0
    dy, g_wout0 = _out_bwd(dx, ys[l][0], ys[l][1], w_out_f[l])
    (da, g_wp0, g_pw0, g_dw0, g_vec0), (r_win1, r_wout1, r_small1) = _poolconv_bwd(
        projs[l], *cvs[l], dy, wp_bd[l], vec(pool_scale, l), dw_f[l], vec(conv_ln_g, l), vec(conv_ln_b, l), pw_f[l],
        carried=[slabs(g_win_t1), slabs(g_wout1), _small_slabs(g_pw1, g_dw1, d)])
    (dattn, gs0), (r_wout0, r_small0) = _attn_bwd(projs[l], ys[l][2], dy, attn_sinks[l],
                                                  carried=[slabs(g_wout0), _small_slabs(g_pw0, g_dw0, d)])
    dproj, g_win_t0 = _in_bwd_dw(da, dattn, xs[l], vec(ln_g, l))
    (dx, g_ln0), (r_win0,) = _in_bwd_dx(dproj, xs[l], dx, vec(ln_g, l), w_in_t[l], carried=[slabs(g_win_t0)])
    grad_x = dx[None]

    gv = jnp.stack([g_vec0, g_vec1], axis=0)
    rep_part = _pack_replicated(
        jnp.concatenate([g_ln0, g_ln1], axis=0), jnp.stack([_diag_blocks(g_wp0), _diag_blocks(g_wp1)], axis=0),
        gv[:, 0], gv[:, 1], gv[:, 2], gv[:, 3], jnp.stack([gs0[:, 0], gs1[:, 0]], axis=0), g_final, sq[0, 0], d)
    (r_rep,) = _final_exchange([rep_part])

    t = lambda a: jnp.swapaxes(a, 1, 2)
    win = [t(o) for o in _adamw_layers([r_win0, r_win1], t(w_in), t(m_w_in), t(v_w_in), "adamw_w_in")]
    wout = _adamw_layers([r_wout0, r_wout1], w_out, m_w_out, v_w_out, "adamw_w_out")
    pack_s = lambda pw_, dw_: jnp.stack([_pack_small(pw_[l], dw_[l], d) for l in range(DEPTH)], axis=0)
    small = _adamw_layers([r_small0, r_small1], pack_s(conv_pw, conv_dw), pack_s(m_conv_pw, m_conv_dw),
                          pack_s(v_conv_pw, v_conv_dw), "adamw_conv")
    small = [[_unpack_small(o[l], d) for l in range(DEPTH)] for o in small]
    *rep_grads, sq_sum = _unpack_replicated(_sum_rows(r_rep), d)
    loss = 0.5 / d * sq_sum
    rep_w = [ln_g, pool_w, pool_scale, conv_b, conv_ln_g, conv_ln_b, attn_sinks, final_g]
    rep_m = [m_ln_g, m_pool_w, m_pool_scale, m_conv_b, m_conv_ln_g, m_conv_ln_b, m_attn_sinks, m_final_g]
    rep_v = [v_ln_g, v_pool_w, v_pool_scale, v_conv_b, v_conv_ln_g, v_conv_ln_b, v_attn_sinks, v_final_g]
    rep = [rep_grads] + _adamw_tensors(rep_grads, rep_w, rep_m, rep_v)

    outs = []
    for k in range(4):
        r_ln, r_pool, r_scale, r_cb, r_lng, r_lnb, r_sinks, r_final = rep[k]
        s_pw = jnp.stack([small[k][l][0] for l in range(DEPTH)], axis=0)
        s_dw = jnp.stack([small[k][l][1] for l in range(DEPTH)], axis=0)
        outs += [r_ln, win[k], r_pool, r_scale, s_dw, r_cb, r_lng, r_lnb, s_pw, r_sinks, wout[k], r_final]
    return (loss, grad_x, *outs)
```

```python
import jax
import jax.numpy as jnp
from jax import lax
from jax.experimental import pallas as pl
from jax.experimental.pallas import tpu as pltpu

F32 = jnp.float32
MXU_DTYPE = jnp.bfloat16
EXCHANGE_DTYPE = jnp.bfloat16

N_DEV = 8
DEPTH = 2
POOL_WIDTH = 256
POOL_GROUP = 64
CONV_WIDTH = 256
CONV_KERNEL = 31
CONV_TAPS_PAD = 32
HEAD_DIM = 64
N_KV_HEADS = 2
Q_PER_KV = 4
N_Q_HEADS = 8
BLOCK = 128
D_MIX = 1024
D_IN = 2560
HALF_IN = 1280
EPS = 1e-6
SCALE = HEAD_DIM ** -0.5
NEG = -1e30

ADAM_LR = 0.001
ADAM_B1 = 0.9
ADAM_B2 = 0.999
ADAM_EPS = 1e-08
ADAM_WD = 0.01
ADAM_STEP = 10

HALO = 32
ROW_TILE = 512
IN_BWD_TILE = 512
POOLCONV_BWD_TILE = 1024
VMEM_LIMIT = 56 * 1024 * 1024

_NN = (((1,), (0,)), ((), ()))
_NT = (((1,), (1,)), ((), ()))
_TN = (((0,), (0,)), ((), ()))
_ANY = pl.BlockSpec(memory_space=pl.ANY)


def _mm(a, b, dims=_NN):
    return lax.dot_general(a.astype(MXU_DTYPE), b.astype(MXU_DTYPE), dims, preferred_element_type=F32)


def _sig(x):
    return 1.0 / (1.0 + jnp.exp(-x))


def _dsilu(z, s):
    return s * (1.0 + z * (1.0 - s))


def _params(n_grid):
    return pltpu.CompilerParams(dimension_semantics=("arbitrary",) * n_grid, vmem_limit_bytes=VMEM_LIMIT)


def _row_tile(rows, cap):
    t = min(rows, cap)
    while rows % t or t % 8:
        t -= 8
    return t


def _full(shape):
    return pl.BlockSpec(shape, lambda i: (0,) * len(shape))


def _layer_vec(stacked, layer):
    arr = stacked.reshape(stacked.shape[0], 1, stacked.shape[-1])
    return arr, pl.BlockSpec((None, 1, arr.shape[-1]), lambda i: (layer, 0, 0))


def _mesh_pos():
    return lax.axis_index("x"), lax.axis_index("y"), lax.axis_index("c")


class _Exchange:
    def __init__(self, sources):
        self.sources = list(sources)
        self.n = len(self.sources)
        self.gather = [s.ndim == 2 for s in self.sources]

    def out_shapes(self):
        return [jax.ShapeDtypeStruct((N_DEV,) + s.shape[-2:], s.dtype) for s in self.sources]

    def scratch(self):
        if not self.n:
            return []
        return [pltpu.SemaphoreType.DMA((7 * self.n,)), pltpu.SemaphoreType.DMA((7 * self.n,)),
                pltpu.SemaphoreType.DMA((self.n,))]

    def copies(self, src_refs, dst_refs, sems):
        send_sems, recv_sems, local_sems = sems
        x, y, c = _mesh_pos()
        me = 4 * x + 2 * y + c
        out = []
        for a, (src, dst) in enumerate(zip(src_refs, dst_refs)):
            out.append(pltpu.make_async_copy(src if self.gather[a] else src.at[me], dst.at[me], local_sems.at[a]))
            for k in range(1, N_DEV):
                tx, ty, tc = x ^ ((k >> 2) & 1), y ^ ((k >> 1) & 1), c ^ (k & 1)
                out.append(pltpu.make_async_remote_copy(
                    src_ref=src if self.gather[a] else src.at[4 * tx + 2 * ty + tc], dst_ref=dst.at[me],
                    send_sem=send_sems.at[7 * a + k - 1], recv_sem=recv_sems.at[7 * a + k - 1],
                    device_id=(tx, ty, tc), device_id_type=pl.DeviceIdType.MESH))
        return out

    def run(self, refs, first, last):
        if not self.n:
            return
        src_refs, dst_refs, sems = refs

        @pl.when(first)
        def _():
            for cp in self.copies(src_refs, dst_refs, sems):
                cp.start()

        @pl.when(last)
        def _():
            for cp in self.copies(src_refs, dst_refs, sems):
                cp.wait()


def _split_refs(refs, n_in, n_out, exch):
    ins = refs[:n_in]
    srcs = refs[n_in:n_in + exch.n]
    outs = refs[n_in + exch.n:n_in + exch.n + n_out]
    dsts = refs[n_in + exch.n + n_out:n_in + 2 * exch.n + n_out]
    rest = refs[n_in + 2 * exch.n + n_out:]
    sems = rest[len(rest) - 3:] if exch.n else ()
    scratch = rest[:len(rest) - 3] if exch.n else rest
    return ins, outs, scratch, (srcs, dsts, sems)


def _final_exchange(sources):
    exch = _Exchange(sources)

    def body(*refs):
        _, _, _, xrefs = _split_refs(refs, 0, 0, exch)
        for cp in exch.copies(*xrefs):
            cp.start()
        for cp in exch.copies(*xrefs):
            cp.wait()

    return pl.pallas_call(
        body, name="final_exchange", out_shape=exch.out_shapes(),
        in_specs=[_ANY] * exch.n, out_specs=[_ANY] * exch.n, scratch_shapes=exch.scratch(),
    )(*exch.sources)


def _all_gather(shard, name):
    m_per, n = shard.shape

    def body(x_ref, out_ref, send_sems, recv_sems, local_sem):
        x, y, c = _mesh_pos()
        me, sibling = (x, y, c), (x, y, 1 - c)
        chips = [(1 - x, y), (x, 1 - y), (1 - x, 1 - y)]

        def rows(px, py, pc):
            return out_ref.at[pl.ds((4 * px + 2 * py + pc) * m_per, m_per), :]

        def copy(k, block, to, src=None):
            return pltpu.make_async_remote_copy(
                src_ref=rows(*block) if src is None else src, dst_ref=rows(*block),
                send_sem=send_sems.at[k], recv_sem=recv_sems.at[k],
                device_id=to, device_id_type=pl.DeviceIdType.MESH)

        mine = pltpu.make_async_copy(x_ref, rows(*me), local_sem)
        mine.start()
        first = [copy(0, me, sibling, src=x_ref)]
        first += [copy(1 + j, me, (*chip, c), src=x_ref) for j, chip in enumerate(chips)]
        for cp in first:
            cp.start()
        passed = [copy(4 + j, (*chip, c), sibling) for j, chip in enumerate(chips)]
        for j, chip in enumerate(chips):
            copy(1 + j, (*chip, c), me).wait_recv()
            passed[j].start()
        copy(0, sibling, me).wait_recv()
        for j, chip in enumerate(chips):
            copy(4 + j, (*chip, 1 - c), me).wait_recv()
        for cp in first + passed:
            cp.wait_send()
        mine.wait()

    return pl.pallas_call(
        body, name=name,
        out_shape=jax.ShapeDtypeStruct((N_DEV * m_per, n), shard.dtype),
        in_specs=[pl.BlockSpec(memory_space=pltpu.VMEM)],
        out_specs=pl.BlockSpec(memory_space=pltpu.VMEM),
        scratch_shapes=[pltpu.SemaphoreType.DMA((7,)), pltpu.SemaphoreType.DMA((7,)), pltpu.SemaphoreType.DMA],
        compiler_params=pltpu.CompilerParams(vmem_limit_bytes=VMEM_LIMIT),
    )(shard)


def _by_group(lane, v2, v4, v8, v16):
    return jnp.where(lane < 64, v2, jnp.where(lane < 128, v4, jnp.where(lane < 192, v8, v16)))


def _pool_count(t0, n):
    lane = lax.broadcasted_iota(jnp.int32, (1, POOL_WIDTH), 1)
    t = (t0 + lax.broadcasted_iota(jnp.int32, (n, 1), 0)).astype(F32)
    wnd = _by_group(lane, 2.0, 4.0, 8.0, 16.0)
    return jnp.minimum(t + 1.0, wnd)


def _pool_diff(u_ext, t0, ts):
    lane = lax.broadcasted_iota(jnp.int32, (1, POOL_WIDTH), 1)
    s2 = u_ext + pltpu.roll(u_ext, 1, 0)
    s4 = s2 + pltpu.roll(s2, 2, 0)
    s8 = s4 + pltpu.roll(s4, 4, 0)
    s16 = s8 + pltpu.roll(s8, 8, 0)
    pooled = _by_group(lane, s2, s4, s8, s16)[HALO:]
    return pooled / _pool_count(t0, ts) - u_ext[HALO:]


def _pool_diff_bwd(w, ts):
    n = w.shape[0]
    lane = lax.broadcasted_iota(jnp.int32, (1, POOL_WIDTH), 1)
    f2 = w + pltpu.roll(w, n - 1, 0)
    f4 = f2 + pltpu.roll(f2, n - 2, 0)
    f8 = f4 + pltpu.roll(f4, n - 4, 0)
    f16 = f8 + pltpu.roll(f8, n - 8, 0)
    return _by_group(lane, f2, f4, f8, f16)[:ts]


CONV_CHUNK = 64


def _conv_taps():
    return [(8 * m + r, r, m) for r in range(8) for m in range(4) if 8 * m + r < CONV_KERNEL]


def _store_shifted(dst_ref, x, up):
    n = x.shape[0]
    for r in range(8):
        dst_ref[r] = x if r == 0 else pltpu.roll(x, n - r if up else r, 0)


def _anticausal_conv(src_ref, dw_ref, out_ref, n_out):
    def chunk(c, carry):
        t0 = pl.multiple_of(c * CONV_CHUNK, CONV_CHUNK)
        acc = None
        for d, r, m in _conv_taps():
            term = dw_ref[pl.ds(CONV_KERNEL - 1 - d, 1), :] * src_ref[r, pl.ds(t0 + 8 * m, CONV_CHUNK), :]
            acc = term if acc is None else acc + term
        out_ref[pl.ds(t0, CONV_CHUNK), :] = acc
        return carry

    lax.fori_loop(0, n_out // CONV_CHUNK, chunk, 0)


def _depthwise_conv_weight_grad(x_ref, dout_up_ref, acc_ref, n_rows):
    acc_ref[...] = jnp.zeros_like(acc_ref)

    def chunk(c, carry):
        t0 = pl.multiple_of(c * CONV_CHUNK, CONV_CHUNK)
        xv = x_ref[pl.ds(t0, CONV_CHUNK), :]
        for d, r, m in _conv_taps():
            prod = xv * dout_up_ref[r, pl.ds(t0 + 8 * m, CONV_CHUNK), :]
            acc_ref[CONV_KERNEL - 1 - d] += jnp.sum(prod.reshape(CONV_CHUNK // 8, 8, prod.shape[-1]), axis=0)
        return carry

    lax.fori_loop(0, n_rows // CONV_CHUNK, chunk, 0)


def _layer_norm(cv):
    mu = jnp.mean(cv, axis=-1, keepdims=True)
    xc = cv - mu
    var = jnp.mean(xc * xc, axis=-1, keepdims=True)
    rstd = lax.rsqrt(var + EPS)
    return xc * rstd, rstd


PAIR_ROWS = 2 * BLOCK
PAIR_COLS = 4 * BLOCK
ATTN_BLOCKS_PER_STEP = 4


def _attn_blocks_per_step(s_len):
    qb = ATTN_BLOCKS_PER_STEP
    while (s_len // BLOCK) % qb:
        qb //= 2
    return qb


def _pair_rows(v0, v1):
    r = lax.broadcasted_iota(jnp.int32, (PAIR_ROWS, 1), 0)
    return jnp.where(r < BLOCK, v0, v1)


def _fill_attn_bias(bias_ref):
    rows = lax.broadcasted_iota(jnp.int32, (PAIR_ROWS, PAIR_COLS), 0)
    cols = lax.broadcasted_iota(jnp.int32, (PAIR_ROWS, PAIR_COLS), 1)
    key = cols & (2 * BLOCK - 1)
    dist = BLOCK + (rows & (BLOCK - 1)) - key
    in_band = (dist >= 0) & (dist < BLOCK)
    distf = dist.astype(F32)
    second = cols >= 2 * BLOCK
    for kh in range(N_KV_HEADS):
        slope_of = lambda j, a: 2.0 ** -(Q_PER_KV * kh + 2 * j + a + 1)
        slope = jnp.where(rows < BLOCK, jnp.where(second, slope_of(0, 1), slope_of(0, 0)),
                          jnp.where(second, slope_of(1, 1), slope_of(1, 0)))
        bias = -slope * distf
        bias_ref[0, kh] = jnp.where(in_band & (key >= BLOCK), bias, NEG)
        bias_ref[1, kh] = jnp.where(in_band, bias, NEG)


def _pair_block_matrix(x, x_swapped, kh):
    lo = lax.broadcasted_iota(jnp.int32, (1, 2 * HEAD_DIM), 1) < HEAD_DIM
    in_lo, in_hi = (x, x_swapped) if kh == 0 else (x_swapped, x)
    return jnp.concatenate([jnp.where(lo, in_lo, 0.0), jnp.where(lo, 0.0, in_hi)], axis=0).astype(MXU_DTYPE)


def _pair_queries(q, kh):
    return (_pair_stack(q, kh) * SCALE).astype(MXU_DTYPE)


def _pair_stack(a, kh):
    return jnp.concatenate([a[:, 2 * BLOCK * kh: 2 * BLOCK * kh + BLOCK],
                            a[:, 2 * BLOCK * kh + BLOCK: 2 * BLOCK * (kh + 1)]], axis=0)


def _pair_unstack(parts):
    return jnp.concatenate([p[BLOCK * j: BLOCK * (j + 1)] for p in parts for j in range(2)], axis=-1)


def _pair_softmax(s, kh, sinks_ref):
    ps, p_sinks = [], []
    for a in range(2):
        sa = s[:, 2 * BLOCK * a: 2 * BLOCK * (a + 1)]
        sink = _pair_rows(sinks_ref[Q_PER_KV * kh + a], sinks_ref[Q_PER_KV * kh + 2 + a])
        m = jnp.maximum(jnp.max(sa, axis=-1, keepdims=True), sink)
        e = jnp.exp(sa - m)
        es = jnp.exp(sink - m)
        inv = 1.0 / (jnp.sum(e, axis=-1, keepdims=True) + es)
        ps.append(e * inv)
        p_sinks.append(es * inv)
    return jnp.concatenate(ps, axis=-1), p_sinks


def _fold_pair_rows(t):
    return t[:HEAD_DIM, :2 * BLOCK] + t[HEAD_DIM:, 2 * BLOCK:]


def _in_proj(x, g, w_t, carried=()):
    s_len, d = x.shape
    ts = _row_tile(s_len, ROW_TILE)
    nt = s_len // ts
    exch = _Exchange(carried)

    def body(*refs):
        (x_ref, g_ref, w_ref), (o_ref,), _, xrefs = _split_refs(refs, 3, 1, exch)
        i = pl.program_id(0)
        exch.run(xrefs, i == 0, i == nt - 1)
        xv = x_ref[...]
        r = lax.rsqrt(jnp.mean(xv * xv, axis=-1, keepdims=True) + EPS)
        o_ref[...] = _mm(xv * r * g_ref[...], w_ref[...], _NT)

    res = pl.pallas_call(
        body, name="in_proj_carrier" if exch.n else "in_proj", grid=(nt,),
        in_specs=[pl.BlockSpec((ts, d), lambda i: (i, 0)), g[1], _full((D_IN, d))] + [_ANY] * exch.n,
        out_specs=[pl.BlockSpec((ts, D_IN), lambda i: (i, 0))] + [_ANY] * exch.n,
        out_shape=[jax.ShapeDtypeStruct((s_len, D_IN), F32)] + exch.out_shapes(),
        scratch_shapes=exch.scratch(),
        compiler_params=_params(1),
    )(x, g[0], w_t, *exch.sources)
    return res[0], res[1:]


def _poolconv_fwd(proj, wp, scale, dw, cb, lng, lnb, pw, carried=()):
    s_len = proj.shape[0]
    ts = _row_tile(s_len, ROW_TILE)
    hb = ts // HALO
    nt = s_len // ts
    exch = _Exchange(carried)

    def body(*refs):
        ins, (y_ref, cv_ref, diff_ref), _, xrefs = _split_refs(refs, 9, 3, exch)
        p_ref, ph_ref, wp_ref, sc_ref, dw_ref, cb_ref, lng_ref, lnb_ref, pw_ref = ins
        i = pl.program_id(0)
        exch.run(xrefs, i == 0, i == nt - 1)
        halo = jnp.where(i > 0, ph_ref[...], 0.0)
        y_mix, cv, diff = _pool_conv_mixers(p_ref[...], halo, i * ts, ts, wp_ref, sc_ref, dw_ref, cb_ref, lng_ref,
                                            lnb_ref, pw_ref)
        y_ref[...] = y_mix.astype(y_ref.dtype)
        cv_ref[...] = cv
        diff_ref[...] = diff

    res = pl.pallas_call(
        body, name="poolconv_fwd_carrier" if exch.n else "poolconv_fwd", grid=(nt,),
        in_specs=[pl.BlockSpec((ts, HALF_IN), lambda i: (i, 0)),
                  pl.BlockSpec((HALO, HALF_IN), lambda i: (jnp.maximum(i * hb - 1, 0), 0)),
                  _full((256, 256)), scale[1], _full((CONV_TAPS_PAD, 256)), cb[1], lng[1], lnb[1],
                  _full((256, 256))]
        + [_ANY] * exch.n,
        out_specs=[pl.BlockSpec((ts, 512), lambda i: (i, 0)), pl.BlockSpec((ts, 256), lambda i: (i, 0)),
                   pl.BlockSpec((ts, 256), lambda i: (i, 0))] + [_ANY] * exch.n,
        out_shape=[jax.ShapeDtypeStruct((s_len, 512), MXU_DTYPE), jax.ShapeDtypeStruct((s_len, 256), F32),
                   jax.ShapeDtypeStruct((s_len, 256), MXU_DTYPE)] + exch.out_shapes(),
        scratch_shapes=exch.scratch(),
        compiler_params=_params(1),
    )(proj, proj, wp, scale[0], dw, cb[0], lng[0], lnb[0], pw, *exch.sources)
    return res[:3], res[3:]


def _pool_conv_mixers(cur, halo, t0, ts, wp_ref, sc_ref, dw_ref, cb_ref, lng_ref, lnb_ref, pw_ref):
    ext = jnp.concatenate([halo, cur], axis=0)
    diff = _pool_diff(ext[:, 0:256], t0, ts).astype(MXU_DTYPE)
    gp = cur[:, 256:512]
    y_pool = _mm(diff, wp_ref[...]) * sc_ref[...] * (gp * _sig(gp))
    hh = ext[:, 512:768] * _sig(ext[:, 768:1024])
    shifted = [hh if r == 0 else pltpu.roll(hh, r, 0) for r in range(8)]
    cv = cb_ref[...]
    for dist, r, m in _conv_taps():
        cv = cv + dw_ref[pl.ds(CONV_KERNEL - 1 - dist, 1), :] * shifted[r][HALO - 8 * m: HALO - 8 * m + ts]
    n, _ = _layer_norm(cv)
    z = n * lng_ref[...] + lnb_ref[...]
    gc = cur[:, 1024:1280]
    y_conv = _mm(z * _sig(z), pw_ref[...]) * (gc * _sig(gc))
    return jnp.concatenate([y_pool, y_conv], axis=-1), cv, diff


def _attn_fwd(proj, sinks, carried=()):
    s_len = proj.shape[0]
    qb = _attn_blocks_per_step(s_len)
    ts = qb * BLOCK
    nt = s_len // ts
    exch = _Exchange(carried)

    def body(*refs):
        (p_ref, kvp_ref, sinks_ref), (y_ref, ya_ref), (bias_ref,), xrefs = _split_refs(refs, 3, 2, exch)
        i = pl.program_id(0)
        exch.run(xrefs, i == 0, i == nt - 1)

        @pl.when(i == 0)
        def _():
            _fill_attn_bias(bias_ref)

        probs = [(b, kh) for b in range(qb) for kh in range(N_KV_HEADS)]
        scores, v_bds = {}, {}
        for b in range(qb):
            r0 = BLOCK * b
            kv_prev = kvp_ref[...] if b == 0 else p_ref[r0 - BLOCK:r0, 512:768]
            kv2 = jnp.concatenate([kv_prev, p_ref[r0:r0 + BLOCK, 512:768]], axis=0)
            k2, v2 = kv2[:, :BLOCK], kv2[:, BLOCK:]
            k2_swapped, v2_swapped = pltpu.roll(k2, HEAD_DIM, 1), pltpu.roll(v2, HEAD_DIM, 1)
            variant = jnp.where(i == 0, 0, 1) if b == 0 else 1
            q = p_ref[r0:r0 + BLOCK, 0:512]
            for kh in range(N_KV_HEADS):
                k_bd = _pair_block_matrix(k2, k2_swapped, kh)
                v_bds[b, kh] = _pair_block_matrix(v2, v2_swapped, kh)
                scores[b, kh] = _mm(_pair_queries(q, kh), k_bd, _NT) + bias_ref[variant, kh]
        ps = {pr: _pair_softmax(scores[pr], pr[1], sinks_ref)[0] for pr in probs}
        outs = {pr: _mm(ps[pr], v_bds[pr]) for pr in probs}
        for b in range(qb):
            r0 = BLOCK * b
            ga = p_ref[r0:r0 + BLOCK, 768:1280]
            ya = _pair_unstack([outs[b, kh] for kh in range(N_KV_HEADS)])
            ya_ref[r0:r0 + BLOCK, :] = ya
            y_ref[r0:r0 + BLOCK, :] = (ya * (ga * _sig(ga))).astype(y_ref.dtype)

    res = pl.pallas_call(
        body, name="attn_fwd_carrier" if exch.n else "attn_fwd", grid=(nt,),
        in_specs=[pl.BlockSpec((ts, HALF_IN), lambda i: (i, 1)),
                  pl.BlockSpec((BLOCK, 256), lambda i: (jnp.maximum(i * qb - 1, 0), 7)),
                  pl.BlockSpec(memory_space=pltpu.SMEM)] + [_ANY] * exch.n,
        out_specs=[pl.BlockSpec((ts, 512), lambda i: (i, 0)), pl.BlockSpec((ts, 512), lambda i: (i, 0))]
        + [_ANY] * exch.n,
        out_shape=[jax.ShapeDtypeStruct((s_len, 512), MXU_DTYPE), jax.ShapeDtypeStruct((s_len, 512), F32)]
        + exch.out_shapes(),
        scratch_shapes=[pltpu.VMEM((2, N_KV_HEADS, PAIR_ROWS, PAIR_COLS), F32)] + exch.scratch(),
        compiler_params=_params(1),
    )(proj, proj, sinks, *exch.sources)
    return res[:2], res[2:]


def _out_in_proj(x, y_pc, y_at, w_out, g_next, w_t_next):
    s_len, d = x.shape
    ts = _row_tile(s_len, ROW_TILE)

    def body(x_ref, a_ref, b_ref, w_ref, g_ref, wn_ref, o_ref, p_ref):
        y = jnp.concatenate([a_ref[...], b_ref[...]], axis=-1)
        xv = x_ref[...] + _mm(y, w_ref[...])
        o_ref[...] = xv
        r = lax.rsqrt(jnp.mean(xv * xv, axis=-1, keepdims=True) + EPS)
        p_ref[...] = _mm(xv * r * g_ref[...], wn_ref[...], _NT)

    tile = pl.BlockSpec((ts, d), lambda i: (i, 0))
    half = pl.BlockSpec((ts, 512), lambda i: (i, 0))
    return pl.pallas_call(
        body, name="out_in_proj", grid=(s_len // ts,),
        in_specs=[tile, half, half, _full((D_MIX, d)), g_next[1], _full((D_IN, d))],
        out_specs=[tile, pl.BlockSpec((ts, D_IN), lambda i: (i, 0))],
        out_shape=[jax.ShapeDtypeStruct((s_len, d), F32), jax.ShapeDtypeStruct((s_len, D_IN), F32)],
        compiler_params=_params(1),
    )(x, y_pc, y_at, w_out, g_next[0], w_t_next)


def _out_proj_loss(x, y_pc, y_at, w_out, target, g):
    s_len, d = x.shape
    ts = _row_tile(s_len, ROW_TILE)

    def body(x_ref, a_ref, b_ref, w_ref, t_ref, g_ref, sq_ref, dg_ref, dx_ref):
        i = pl.program_id(0)
        y = jnp.concatenate([a_ref[...], b_ref[...]], axis=-1)
        xv = x_ref[...] + _mm(y, w_ref[...])
        gv = g_ref[...]
        r = lax.rsqrt(jnp.mean(xv * xv, axis=-1, keepdims=True) + EPS)
        xr = xv * r
        err = xr * gv - t_ref[...]
        dout = err * (1.0 / d)
        w = dout * gv
        dx_ref[...] = r * (w - xr * jnp.mean(w * xr, axis=-1, keepdims=True))

        @pl.when(i == 0)
        def _():
            sq_ref[...] = jnp.zeros_like(sq_ref)
            dg_ref[...] = jnp.zeros_like(dg_ref)

        sq = jnp.sum(jnp.sum(err * err, axis=-1, keepdims=True), axis=0, keepdims=True)
        sq_ref[...] += jnp.broadcast_to(sq, sq_ref.shape)
        dg_ref[...] += jnp.sum(dout * xr, axis=0, keepdims=True)

    tile = pl.BlockSpec((ts, d), lambda i: (i, 0))
    half = pl.BlockSpec((ts, 512), lambda i: (i, 0))
    return pl.pallas_call(
        body, name="out_proj_loss", grid=(s_len // ts,),
        in_specs=[tile, half, half, _full((D_MIX, d)), tile, _full((1, d))],
        out_specs=[_full((1, 128)), _full((1, d)), tile],
        out_shape=[jax.ShapeDtypeStruct((1, 128), F32), jax.ShapeDtypeStruct((1, d), F32),
                   jax.ShapeDtypeStruct((s_len, d), F32)],
        compiler_params=_params(1),
    )(x, y_pc, y_at, w_out, target, g)


def _out_bwd(dxo, y_pc, y_at, w_out):
    s_len, d = dxo.shape
    ts = _row_tile(s_len, ROW_TILE)
    nt = s_len // ts

    def body(dx_ref, a_ref, b_ref, w_ref, dy_ref, gw_ref, acc_ref):
        i = pl.program_id(0)
        dxv = dx_ref[...].astype(MXU_DTYPE)
        dy_ref[...] = _mm(dxv, w_ref[...], _NT)

        @pl.when(i == 0)
        def _():
            acc_ref[...] = jnp.zeros_like(acc_ref)

        y = jnp.concatenate([a_ref[...], b_ref[...]], axis=-1)
        acc_ref[...] += _mm(y, dxv, _TN)

        @pl.when(i == nt - 1)
        def _():
            gw_ref[...] = acc_ref[...].astype(gw_ref.dtype)

    return pl.pallas_call(
        body, name="out_bwd", grid=(nt,),
        in_specs=[pl.BlockSpec((ts, d), lambda i: (i, 0)), pl.BlockSpec((ts, 512), lambda i: (i, 0)),
                  pl.BlockSpec((ts, 512), lambda i: (i, 0)), _full((D_MIX, d))],
        out_specs=[pl.BlockSpec((ts, D_MIX), lambda i: (i, 0)), _full((D_MIX, d))],
        out_shape=[jax.ShapeDtypeStruct((s_len, D_MIX), F32), jax.ShapeDtypeStruct((D_MIX, d), EXCHANGE_DTYPE)],
        scratch_shapes=[pltpu.VMEM((D_MIX, d), F32)],
        compiler_params=_params(1),
    )(dxo, y_pc, y_at, w_out)


def _poolconv_bwd(proj, cv, diff, dy, wp, scale, dw, lng, lnb, pw, carried=()):
    s_len = proj.shape[0]
    ts = _row_tile(s_len, POOLCONV_BWD_TILE)
    hb = ts // HALO
    nt = s_len // ts
    last_halo = s_len // HALO - 1
    n = ts + HALO
    exch = _Exchange(carried)

    def body(*refs):
        ins, outs, (up_ref, hh_ref, dhh_ref, gdw_acc_ref), xrefs = _split_refs(refs, 13, 5, exch)
        (p_ref, pn_ref, cv_ref, cvn_ref, diff_ref, dy_ref, dyn_ref, wp_ref, sc_ref, dw_ref, lng_ref, lnb_ref,
         pw_ref) = ins
        da_ref, gwp_ref, gpw_ref, gdw_ref, gvec_ref = outs
        i = pl.program_id(0)
        exch.run(xrefs, i == 0, i == nt - 1)
        has_next = i < nt - 1
        cur = p_ref[...]
        nxt = jnp.where(has_next, pn_ref[...], 0.0)
        dyx = jnp.concatenate([dy_ref[...], jnp.where(has_next, dyn_ref[...], 0.0)], axis=0)
        row = lax.broadcasted_iota(jnp.int32, (n, 1), 0)
        in_seq = (row < ts) | has_next
        scale_v = sc_ref[...]

        cvx = jnp.concatenate([cv_ref[...], jnp.where(has_next, cvn_ref[...], 0.0)], axis=0)
        nrm, rstd = _layer_norm(cvx)
        z = nrm * lng_ref[...] + lnb_ref[...]
        sz = _sig(z)
        sw = z * sz
        gc = jnp.concatenate([cur[:, 1024:1280], nxt[:, 1024:1280]], axis=0)
        sgc = _sig(gc)
        yc = _mm(sw, pw_ref[...])
        dyc = dyx[:, 256:512]
        d_yc = dyc * (gc * sgc)
        d_gc = (dyc * yc * _dsilu(gc, sgc))[:ts]
        d_z = _mm(d_yc, pw_ref[...], _NT) * _dsilu(z, sz)
        d_n = d_z * lng_ref[...]
        d_cv = rstd * (d_n - jnp.mean(d_n, axis=-1, keepdims=True)
                       - nrm * jnp.mean(d_n * nrm, axis=-1, keepdims=True))
        d_cv = jnp.where(in_seq, d_cv, 0.0)
        _store_shifted(up_ref, d_cv, up=True)
        _anticausal_conv(up_ref, dw_ref, dhh_ref, ts)
        d_hh = dhh_ref[...]
        a_c, sb_c = cur[:, 512:768], _sig(cur[:, 768:1024])
        hh_ref[...] = a_c * sb_c
        d_a = d_hh * sb_c
        d_b = d_hh * a_c * sb_c * (1.0 - sb_c)
        d_cv_t = d_cv[:ts]
        _depthwise_conv_weight_grad(hh_ref, up_ref, gdw_acc_ref, ts)

        diff = diff_ref[...]
        raw = _mm(diff, wp_ref[...])
        gp = jnp.concatenate([cur[:, 256:512], nxt[:, 256:512]], axis=0)
        sgp = _sig(gp)
        dyp = dyx[:, 0:256]
        d_yp = dyp * (gp * sgp)
        d_gp = dyp[:ts] * (raw * scale_v) * _dsilu(gp, sgp)[:ts]
        d_raw = d_yp * scale_v
        d_diff = _mm(d_raw, wp_ref[...], _NT)
        w = jnp.where(in_seq, d_diff / _pool_count(i * ts, n), 0.0)
        d_u = _pool_diff_bwd(w, ts) - d_diff[:ts]

        da_ref[...] = jnp.concatenate([d_u, d_gp, d_a, d_b, d_gc], axis=-1).astype(da_ref.dtype)

        @pl.when(i == 0)
        def _():
            gwp_ref[...] = jnp.zeros_like(gwp_ref)
            gpw_ref[...] = jnp.zeros_like(gpw_ref)
            gdw_ref[...] = jnp.zeros_like(gdw_ref)
            gvec_ref[...] = jnp.zeros_like(gvec_ref)

        gwp_ref[...] += _mm(diff, d_raw[:ts], _TN)
        gpw_ref[...] += _mm(sw[:ts], d_yc[:ts], _TN)
        gdw_ref[...] += jnp.sum(gdw_acc_ref[...], axis=1)
        zero_row = jnp.zeros((1, 256), F32)
        gvec_ref[...] += jnp.concatenate([
            jnp.sum(d_yp[:ts] * raw, axis=0, keepdims=True),
            jnp.sum(d_cv_t, axis=0, keepdims=True),
            jnp.sum((d_z * nrm)[:ts], axis=0, keepdims=True),
            jnp.sum(d_z[:ts], axis=0, keepdims=True),
            zero_row, zero_row, zero_row, zero_row], axis=0)

    nxt_halo = lambda i: (jnp.minimum((i + 1) * hb, last_halo), 0)
    res = pl.pallas_call(
        body, name="poolconv_bwd_carrier" if exch.n else "poolconv_bwd", grid=(nt,),
        in_specs=[pl.BlockSpec((ts, HALF_IN), lambda i: (i, 0)), pl.BlockSpec((HALO, HALF_IN), nxt_halo),
                  pl.BlockSpec((ts, 256), lambda i: (i, 0)), pl.BlockSpec((HALO, 256), nxt_halo),
                  pl.BlockSpec((ts, 256), lambda i: (i, 0)),
                  pl.BlockSpec((ts, 512), lambda i: (i, 0)), pl.BlockSpec((HALO, 512), nxt_halo),
                  _full((256, 256)), scale[1], _full((CONV_TAPS_PAD, 256)), lng[1], lnb[1], _full((256, 256))]
        + [_ANY] * exch.n,
        out_specs=[pl.BlockSpec((ts, HALF_IN), lambda i: (i, 0)), _full((256, 256)), _full((256, 256)),
                   _full((CONV_TAPS_PAD, 256)), _full((8, 256))] + [_ANY] * exch.n,
        out_shape=[jax.ShapeDtypeStruct((s_len, HALF_IN), MXU_DTYPE), jax.ShapeDtypeStruct((256, 256), F32),
                   jax.ShapeDtypeStruct((256, 256), F32), jax.ShapeDtypeStruct((CONV_TAPS_PAD, 256), F32),
                   jax.ShapeDtypeStruct((8, 256), F32)] + exch.out_shapes(),
        scratch_shapes=[pltpu.VMEM((8, n, CONV_WIDTH), F32), pltpu.VMEM((ts, CONV_WIDTH), F32),
                        pltpu.VMEM((ts, CONV_WIDTH), F32), pltpu.VMEM((CONV_TAPS_PAD, 8, CONV_WIDTH), F32)]
        + exch.scratch(),
        compiler_params=_params(1),
    )(proj, proj, cv, cv, diff, dy, dy, wp, scale[0], dw, lng[0], lnb[0], pw, *exch.sources)
    return res[:5], res[5:]


DQ0, DKC0, DVC0, DKP0, DVP0, DGA0, DATTN_W = 0, 512, 640, 768, 896, 1024, 1536


def _attn_bwd(proj, ya, dy, sinks, carried=()):
    s_len = proj.shape[0]
    qb = _attn_blocks_per_step(s_len)
    ts = qb * BLOCK
    nt = s_len // ts
    exch = _Exchange(carried)

    def body(*refs):
        ins, (o_ref, gs_ref), (bias_ref,), xrefs = _split_refs(refs, 5, 2, exch)
        p_ref, kvp_ref, ya_ref, dy_ref, sinks_ref = ins
        i = pl.program_id(0)
        exch.run(xrefs, i == 0, i == nt - 1)

        @pl.when(i == 0)
        def _():
            _fill_attn_bias(bias_ref)
            gs_ref[...] = jnp.zeros_like(gs_ref)

        probs = [(b, kh) for b in range(qb) for kh in range(N_KV_HEADS)]
        kv_heads = range(N_KV_HEADS)
        scores, k_bds, v_bds, q2s, do2s, dyas, gas, sgas = {}, {}, {}, {}, {}, {}, {}, {}
        for b in range(qb):
            r0 = BLOCK * b
            kv_prev = kvp_ref[...] if b == 0 else p_ref[r0 - BLOCK:r0, 512:768]
            kv2 = jnp.concatenate([kv_prev, p_ref[r0:r0 + BLOCK, 512:768]], axis=0)
            k2, v2 = kv2[:, :BLOCK], kv2[:, BLOCK:]
            k2_swapped, v2_swapped = pltpu.roll(k2, HEAD_DIM, 1), pltpu.roll(v2, HEAD_DIM, 1)
            variant = jnp.where(i == 0, 0, 1) if b == 0 else 1
            q = p_ref[r0:r0 + BLOCK, 0:512]
            gas[b] = p_ref[r0:r0 + BLOCK, 768:1280]
            dyas[b] = dy_ref[r0:r0 + BLOCK, :]
            sgas[b] = _sig(gas[b])
            d_o = dyas[b] * (gas[b] * sgas[b])
            for kh in kv_heads:
                k_bds[b, kh] = _pair_block_matrix(k2, k2_swapped, kh)
                v_bds[b, kh] = _pair_block_matrix(v2, v2_swapped, kh)
                q2s[b, kh] = _pair_queries(q, kh)
                do2s[b, kh] = _pair_stack(d_o, kh)
                scores[b, kh] = _mm(q2s[b, kh], k_bds[b, kh], _NT) + bias_ref[variant, kh]
        ps, p_sinks, dps, dss, dqs, dks, dvs = {}, {}, {}, {}, {}, {}, {}
        d_sinks = [None] * N_Q_HEADS

        def softmax_stage(pr):
            ps[pr], p_sinks[pr] = _pair_softmax(scores[pr], pr[1], sinks_ref)

        def value_stage(pr):
            dps[pr] = _mm(do2s[pr], v_bds[pr], _NT)

        def score_grad_stage(pr):
            p, dp, kh = ps[pr], dps[pr], pr[1]
            ds_halves = []
            for a in range(2):
                cols = slice(2 * BLOCK * a, 2 * BLOCK * (a + 1))
                delta = jnp.sum(p[:, cols] * dp[:, cols], axis=-1, keepdims=True)
                ds_halves.append(p[:, cols] * (dp[:, cols] - delta))
                dsink = -p_sinks[pr][a] * delta
                for j in range(2):
                    part = jnp.sum(dsink[BLOCK * j: BLOCK * (j + 1)], axis=0, keepdims=True)
                    h = Q_PER_KV * kh + 2 * j + a
                    d_sinks[h] = part if d_sinks[h] is None else d_sinks[h] + part
            dss[pr] = jnp.concatenate(ds_halves, axis=-1)

        def operand_grad_stage(pr):
            dqs[pr] = _mm(dss[pr], k_bds[pr]) * SCALE
            dks[pr] = _fold_pair_rows(_mm(q2s[pr], dss[pr], _TN))
            dvs[pr] = _fold_pair_rows(_mm(do2s[pr], ps[pr], _TN))

        for stage in (softmax_stage, value_stage, score_grad_stage, operand_grad_stage):
            for pr in probs:
                stage(pr)
        for b in range(qb):
            r0 = BLOCK * b
            dk = jnp.concatenate([dks[b, kh] for kh in kv_heads], axis=0).T
            dv = jnp.concatenate([dvs[b, kh] for kh in kv_heads], axis=0).T
            d_ga = dyas[b] * ya_ref[r0:r0 + BLOCK, :] * _dsilu(gas[b], sgas[b])
            o_ref[r0:r0 + BLOCK, :] = jnp.concatenate(
                [_pair_unstack([dqs[b, kh] for kh in kv_heads]), dk[BLOCK:], dv[BLOCK:], dk[:BLOCK], dv[:BLOCK],
                 d_ga], axis=-1)
        gs_ref[...] += jnp.broadcast_to(jnp.concatenate(d_sinks, axis=0), gs_ref.shape)

    res = pl.pallas_call(
        body, name="attn_bwd_carrier" if exch.n else "attn_bwd", grid=(nt,),
        in_specs=[pl.BlockSpec((ts, HALF_IN), lambda i: (i, 1)),
                  pl.BlockSpec((BLOCK, 256), lambda i: (jnp.maximum(i * qb - 1, 0), 7)),
                  pl.BlockSpec((ts, 512), lambda i: (i, 0)), pl.BlockSpec((ts, 512), lambda i: (i, 1)),
                  pl.BlockSpec(memory_space=pltpu.SMEM)] + [_ANY] * exch.n,
        out_specs=[pl.BlockSpec((ts, DATTN_W), lambda i: (i, 0)), _full((N_Q_HEADS, 128))] + [_ANY] * exch.n,
        out_shape=[jax.ShapeDtypeStruct((s_len, DATTN_W), F32), jax.ShapeDtypeStruct((N_Q_HEADS, 128), F32)]
        + exch.out_shapes(),
        scratch_shapes=[pltpu.VMEM((2, N_KV_HEADS, PAIR_ROWS, PAIR_COLS), F32)] + exch.scratch(),
        compiler_params=_params(1),
    )(proj, proj, ya, dy, sinks, *exch.sources)
    return res[:2], res[2:]


def _in_bwd(da, dattn, x, dxo, g, w_t):
    s_len, d = x.shape
    ts = _row_tile(s_len, IN_BWD_TILE)
    bpt = ts // BLOCK
    nt = s_len // ts
    last_block = s_len // BLOCK - 1

    def body(da_ref, dat_ref, nxt_ref, x_ref, dxo_ref, g_ref, w_ref, dx_ref, dg_ref, gw_ref, acc_ref, stage_ref,
             stage_sem):
        i = pl.program_id(0)
        dat = dat_ref[...]
        nxt = jnp.where(i < nt - 1, nxt_ref[...], 0.0)
        shifted = jnp.concatenate([dat[BLOCK:, DKP0:DGA0], nxt], axis=0) if bpt > 1 else nxt
        dkv = dat[:, DKC0:DKP0] + shifted
        dproj = jnp.concatenate([da_ref[...], dat[:, DQ0:DKC0].astype(MXU_DTYPE), dkv.astype(MXU_DTYPE),
                                 dat[:, DGA0:DATTN_W].astype(MXU_DTYPE)], axis=-1)
        d_h = _mm(dproj, w_ref[...])
        xv = x_ref[...]
        gv = g_ref[...]
        r = lax.rsqrt(jnp.mean(xv * xv, axis=-1, keepdims=True) + EPS)
        xr = xv * r
        w = d_h * gv
        dx_ref[...] = dxo_ref[...] + r * (w - xr * jnp.mean(w * xr, axis=-1, keepdims=True))

        @pl.when(i == 0)
        def _():
            dg_ref[...] = jnp.zeros_like(dg_ref)
            acc_ref[...] = jnp.zeros_like(acc_ref)

        dg_ref[...] += jnp.sum(d_h * xr, axis=0, keepdims=True)
        acc_ref[...] += _mm(dproj, xr * gv, _TN)

        @pl.when(i == nt - 1)
        def _():
            stage_ref[...] = acc_ref[...].astype(stage_ref.dtype)
            out = pltpu.make_async_copy(stage_ref, gw_ref, stage_sem)
            out.start()
            out.wait()

    return pl.pallas_call(
        body, name="in_bwd", grid=(nt,),
        in_specs=[pl.BlockSpec((ts, HALF_IN), lambda i: (i, 0)),
                  pl.BlockSpec((ts, DATTN_W), lambda i: (i, 0)),
                  pl.BlockSpec((BLOCK, 256), lambda i: (jnp.minimum((i + 1) * bpt, last_block), 3)),
                  pl.BlockSpec((ts, d), lambda i: (i, 0)), pl.BlockSpec((ts, d), lambda i: (i, 0)),
                  g[1], pl.BlockSpec((D_IN, d), lambda i: (0, 0), pipeline_mode=pl.Buffered(1))],
        out_specs=[pl.BlockSpec((ts, d), lambda i: (i, 0)), _full((1, d)), _ANY],
        out_shape=[jax.ShapeDtypeStruct((s_len, d), F32), jax.ShapeDtypeStruct((1, d), F32),
                   jax.ShapeDtypeStruct((D_IN, d), EXCHANGE_DTYPE)],
        scratch_shapes=[pltpu.VMEM((D_IN, d), F32), pltpu.VMEM((D_IN, d), EXCHANGE_DTYPE), pltpu.SemaphoreType.DMA],
        compiler_params=_params(1),
    )(da, dattn, dattn, x, dxo, g[0], w_t)


def _in_bwd_dw(da, dattn, x, g):
    s_len, d = x.shape
    ts = _row_tile(s_len, IN_BWD_TILE)
    bpt = ts // BLOCK
    nt = s_len // ts
    last_block = s_len // BLOCK - 1

    def body(da_ref, dat_ref, nxt_ref, x_ref, g_ref, dp_ref, gw_ref, acc_ref, stage_ref, stage_sem):
        i = pl.program_id(0)
        dat = dat_ref[...]
        nxt = jnp.where(i < nt - 1, nxt_ref[...], 0.0)
        shifted = jnp.concatenate([dat[BLOCK:, DKP0:DGA0], nxt], axis=0) if bpt > 1 else nxt
        dkv = dat[:, DKC0:DKP0] + shifted
        dproj = jnp.concatenate([da_ref[...], dat[:, DQ0:DKC0].astype(MXU_DTYPE), dkv.astype(MXU_DTYPE),
                                 dat[:, DGA0:DATTN_W].astype(MXU_DTYPE)], axis=-1)
        dp_ref[...] = dproj
        xv = x_ref[...]
        r = lax.rsqrt(jnp.mean(xv * xv, axis=-1, keepdims=True) + EPS)

        @pl.when(i == 0)
        def _():
            acc_ref[...] = jnp.zeros_like(acc_ref)

        acc_ref[...] += _mm(dproj, xv * r * g_ref[...], _TN)

        @pl.when(i == nt - 1)
        def _():
            stage_ref[...] = acc_ref[...].astype(stage_ref.dtype)
            out = pltpu.make_async_copy(stage_ref, gw_ref, stage_sem)
            out.start()
            out.wait()

    return pl.pallas_call(
        body, name="in_bwd_dw", grid=(nt,),
        in_specs=[pl.BlockSpec((ts, HALF_IN), lambda i: (i, 0)),
                  pl.BlockSpec((ts, DATTN_W), lambda i: (i, 0)),
                  pl.BlockSpec((BLOCK, 256), lambda i: (jnp.minimum((i + 1) * bpt, last_block), 3)),
                  pl.BlockSpec((ts, d), lambda i: (i, 0)), g[1]],
        out_specs=[pl.BlockSpec((ts, D_IN), lambda i: (i, 0)), _ANY],
        out_shape=[jax.ShapeDtypeStruct((s_len, D_IN), MXU_DTYPE), jax.ShapeDtypeStruct((D_IN, d), EXCHANGE_DTYPE)],
        scratch_shapes=[pltpu.VMEM((D_IN, d), F32), pltpu.VMEM((D_IN, d), EXCHANGE_DTYPE), pltpu.SemaphoreType.DMA],
        compiler_params=_params(1),
    )(da, dattn, dattn, x, g[0])


def _in_bwd_dx(dproj, x, dxo, g, w_t, carried=()):
    s_len, d = x.shape
    ts = _row_tile(s_len, ROW_TILE)
    nt = s_len // ts
    exch = _Exchange(carried)

    def body(*refs):
        (dp_ref, x_ref, dxo_ref, g_ref, w_ref), (dx_ref, dg_ref), _, xrefs = _split_refs(refs, 5, 2, exch)
        i = pl.program_id(0)
        exch.run(xrefs, i == 0, i == nt - 1)
        d_h = _mm(dp_ref[...], w_ref[...])
        xv = x_ref[...]
        r = lax.rsqrt(jnp.mean(xv * xv, axis=-1, keepdims=True) + EPS)
        xr = xv * r
        w = d_h * g_ref[...]
        dx_ref[...] = dxo_ref[...] + r * (w - xr * jnp.mean(w * xr, axis=-1, keepdims=True))

        @pl.when(i == 0)
        def _():
            dg_ref[...] = jnp.zeros_like(dg_ref)

        dg_ref[...] += jnp.sum(d_h * xr, axis=0, keepdims=True)

    tile = pl.BlockSpec((ts, d), lambda i: (i, 0))
    res = pl.pallas_call(
        body, name="in_bwd_dx_carrier" if exch.n else "in_bwd_dx", grid=(nt,),
        in_specs=[pl.BlockSpec((ts, D_IN), lambda i: (i, 0)), tile, tile, g[1], _full((D_IN, d))]
        + [_ANY] * exch.n,
        out_specs=[tile, _full((1, d))] + [_ANY] * exch.n,
        out_shape=[jax.ShapeDtypeStruct((s_len, d), F32), jax.ShapeDtypeStruct((1, d), F32)] + exch.out_shapes(),
        scratch_shapes=exch.scratch(),
        compiler_params=_params(1),
    )(dproj, x, dxo, g[0], w_t, *exch.sources)
    return res[:2], res[2:]


def _sum_partials(p_ref):
    g = p_ref[0].astype(F32)
    for k in range(1, N_DEV):
        g = g + p_ref[k].astype(F32)
    return g


def _adamw_step(g, w, m, v):
    nm = ADAM_B1 * m + (1.0 - ADAM_B1) * g
    nv = ADAM_B2 * v + (1.0 - ADAM_B2) * (g * g)
    m_hat = nm / (1.0 - ADAM_B1 ** ADAM_STEP)
    v_hat = nv / (1.0 - ADAM_B2 ** ADAM_STEP)
    return -ADAM_LR * (m_hat / (jnp.sqrt(v_hat) + ADAM_EPS) + ADAM_WD * w), nm, nv


def _adamw_layers(parts, w, m, v, name):
    _, rows, n = w.shape
    tr = _row_tile(rows, 64)

    def body(*refs):
        p_refs = refs[:DEPTH]
        w_ref, m_ref, v_ref, g_ref, d_ref, nm_ref, nv_ref = refs[DEPTH:]
        for l in range(DEPTH):
            g = _sum_partials(p_refs[l])
            g_ref[l] = g
            d_ref[l], nm_ref[l], nv_ref[l] = _adamw_step(g, w_ref[l], m_ref[l], v_ref[l])

    tile = pl.BlockSpec((DEPTH, tr, n), lambda i: (0, i, 0))
    shape = jax.ShapeDtypeStruct(w.shape, F32)
    return pl.pallas_call(
        body, name=name, grid=(rows // tr,),
        in_specs=[pl.BlockSpec((N_DEV, tr, n), lambda i: (0, i, 0))] * DEPTH + [tile] * 3,
        out_specs=[tile] * 4, out_shape=[shape] * 4,
        compiler_params=_params(1),
    )(*parts, w, m, v)


def _sum_rows(parts):
    def body(p_ref, o_ref):
        o_ref[...] = _sum_partials(p_ref)

    vmem = pl.BlockSpec(memory_space=pltpu.VMEM)
    return pl.pallas_call(body, name="sum_replicated_partials", in_specs=[vmem], out_specs=vmem,
                          out_shape=jax.ShapeDtypeStruct(parts.shape[1:], F32))(parts)


def _adamw_tensors(gs, ws, ms, vs):
    n = len(ws)
    shapes = [w.shape for w in ws]
    two_d = lambda a: a.reshape(1, -1) if a.ndim == 1 else a
    ops = [two_d(a) for a in list(gs) + list(ws) + list(ms) + list(vs)]

    def body(*refs):
        ins, outs = refs[:4 * n], refs[4 * n:]
        for k in range(n):
            g, w, m, v = (ins[j * n + k][...] for j in range(4))
            outs[k][...], outs[n + k][...], outs[2 * n + k][...] = _adamw_step(g, w, m, v)

    vmem = pl.BlockSpec(memory_space=pltpu.VMEM)
    res = pl.pallas_call(
        body, name="adamw_replicated", in_specs=[vmem] * (4 * n), out_specs=[vmem] * (3 * n),
        out_shape=[jax.ShapeDtypeStruct(o.shape, F32) for o in ops[n:2 * n]] * 3,
    )(*ops)
    return [[res[j * n + k].reshape(shapes[k]) for k in range(n)] for j in range(3)]


def _pad_rows(a, mult):
    pad = (-a.shape[0]) % mult
    return a if pad == 0 else jnp.concatenate([a, jnp.zeros((pad, a.shape[1]), a.dtype)], axis=0)


def _dw_rows(conv_dw_l):
    return jnp.pad(jnp.swapaxes(conv_dw_l, 0, 1), ((0, 0), (0, CONV_TAPS_PAD - CONV_KERNEL)))


def _pack_small(pw_l, dw_l, d):
    rows = jnp.concatenate([pw_l.reshape(-1, d), _dw_rows(dw_l).reshape(-1, d)], axis=0)
    return _pad_rows(rows, 8)


def _small_slabs(g_pw, g_dw, d):
    a = g_pw.reshape(N_DEV, -1, d)
    b = jnp.swapaxes(g_dw, 0, 1).reshape(N_DEV, -1, d)
    used = a.shape[1] + b.shape[1]
    return jnp.concatenate([a, b, jnp.zeros((N_DEV, (-used) % 8, d), g_pw.dtype)], axis=1)


def _unpack_small(rows, d):
    c = CONV_WIDTH // N_DEV
    n_pw = c * CONV_WIDTH // d
    n_dw = c * CONV_TAPS_PAD // d
    pw = rows[:n_pw].reshape(c, CONV_WIDTH)
    dw = jnp.swapaxes(rows[n_pw:n_pw + n_dw].reshape(c, CONV_TAPS_PAD), 0, 1)[:CONV_KERNEL]
    return pw, dw


def _pack_replicated(ln_g, pool_w, pool_scale, conv_b, conv_ln_g, conv_ln_b, attn_sinks, final_g, scalar, d):
    sinks = jnp.pad(attn_sinks, ((0, 0), (0, 256 - N_Q_HEADS)))
    small = jnp.concatenate([pool_scale, conv_b, conv_ln_g, conv_ln_b, sinks], axis=0)
    small = _pad_rows(small, d // 256)
    last = jnp.pad(scalar.reshape(1, 1), ((0, 0), (0, d - 1)))
    return _pad_rows(jnp.concatenate([ln_g.reshape(-1, d), final_g.reshape(-1, d), pool_w.reshape(-1, d),
                                      small.reshape(-1, d), last], axis=0), 8)


def _unpack_replicated(rows, d):
    n_pool = DEPTH * 4 * POOL_GROUP * POOL_GROUP // d
    n_small = -(-5 * DEPTH * 256 // d)
    ln_g = rows[:DEPTH]
    final_g = rows[DEPTH]
    pool_w = rows[DEPTH + 1: DEPTH + 1 + n_pool].reshape(DEPTH, 4, POOL_GROUP, POOL_GROUP)
    small = rows[DEPTH + 1 + n_pool: DEPTH + 1 + n_pool + n_small].reshape(-1, 256)[: 5 * DEPTH]
    pool_scale, conv_b, conv_ln_g, conv_ln_b = (small[DEPTH * k: DEPTH * (k + 1)] for k in range(4))
    sinks = small[4 * DEPTH: 5 * DEPTH, :N_Q_HEADS]
    scalar = rows[DEPTH + 1 + n_pool + n_small, 0]
    return ln_g, pool_w, pool_scale, conv_b, conv_ln_g, conv_ln_b, sinks, final_g, scalar


def _block_diag(pool_w):
    wide = jnp.tile(pool_w.reshape(POOL_WIDTH, POOL_GROUP), (1, POOL_WIDTH // POOL_GROUP))
    rows = lax.broadcasted_iota(jnp.int32, wide.shape, 0) // POOL_GROUP
    cols = lax.broadcasted_iota(jnp.int32, wide.shape, 1) // POOL_GROUP
    return jnp.where(rows == cols, wide, jnp.zeros_like(wide))


def _diag_blocks(mat):
    return jnp.stack([mat[POOL_GROUP * gi: POOL_GROUP * (gi + 1), POOL_GROUP * gi: POOL_GROUP * (gi + 1)]
                      for gi in range(4)], axis=0)


def kernel(x, ln_g, w_in, pool_w, pool_scale, conv_dw, conv_b, conv_ln_g, conv_ln_b, conv_pw, attn_sinks, w_out, final_g, loss_target, m_ln_g, m_w_in, m_pool_w, m_pool_scale, m_conv_dw, m_conv_b, m_conv_ln_g, m_conv_ln_b, m_conv_pw, m_attn_sinks, m_w_out, m_final_g, v_ln_g, v_w_in, v_pool_w, v_pool_scale, v_conv_dw, v_conv_b, v_conv_ln_g, v_conv_ln_b, v_conv_pw, v_attn_sinks, v_w_out, v_final_g):
    x0 = x[0]
    d = x0.shape[1]
    row = lambda a: a.reshape(1, -1)
    vec = _layer_vec
    slabs = lambda a: a.reshape(N_DEV, a.shape[0] // N_DEV, d)
    c_shard = CONV_WIDTH // N_DEV

    w_in_rows = [jnp.swapaxes(w_in[l], 0, 1).astype(MXU_DTYPE) for l in range(DEPTH)]
    w_out_rows = [w_out[l].astype(MXU_DTYPE) for l in range(DEPTH)]
    per_word = 4 // jnp.dtype(MXU_DTYPE).itemsize
    dw_t = jnp.stack([_dw_rows(conv_dw[l]) for l in range(DEPTH)], axis=0)
    dw_bits = (lax.bitcast_convert_type(dw_t, MXU_DTYPE) if per_word > 1 else dw_t).reshape(-1, d)
    n_pw = DEPTH * c_shard * CONV_WIDTH // d
    conv_rows = _pad_rows(jnp.concatenate([conv_pw.reshape(-1, d).astype(MXU_DTYPE), dw_bits], axis=0), 16)
    w_in_t, w_out_f = [None] * DEPTH, [None] * DEPTH
    w_in_t[0] = _all_gather(w_in_rows[0], "w_in_all_gather")
    wp_bd = [_block_diag(pool_w[l]).astype(MXU_DTYPE) for l in range(DEPTH)]

    xs, projs, cvs, ys = [x0], [], [], []
    q4 = D_IN // N_DEV // 4
    w_in_1 = [w_in_rows[1][q4 * k: q4 * (k + 1)] for k in range(4)]
    for l in range(DEPTH):
        if l == 0:
            proj, got = _in_proj(xs[0], vec(ln_g, 0), w_in_t[0], carried=[w_out_rows[0], conv_rows, w_in_1[0]])
            w_in_1_got = [got[2]]
            w_out_f[0] = got[0].reshape(D_MIX, d)
            pw_all = got[1][:, :n_pw].reshape(N_DEV, DEPTH, c_shard, CONV_WIDTH)
            pw_f = [pw_all[:, k].reshape(CONV_WIDTH, CONV_WIDTH) for k in range(DEPTH)]
            bits = got[1][:, n_pw:n_pw + dw_bits.shape[0]].reshape(
                (N_DEV, DEPTH, c_shard, CONV_TAPS_PAD) + (per_word,) * (per_word > 1))
            dw_all = lax.bitcast_convert_type(bits, F32) if per_word > 1 else bits
            dw_f = [jnp.swapaxes(dw_all[:, k].reshape(CONV_WIDTH, CONV_TAPS_PAD), 0, 1) for k in range(DEPTH)]
        (y_pc, cv, diff), got = _poolconv_fwd(proj, wp_bd[l], vec(pool_scale, l), dw_f[l], vec(conv_b, l),
                                              vec(conv_ln_g, l), vec(conv_ln_b, l), pw_f[l],
                                              carried=[w_in_1[1], w_in_1[2]] if l == 0 else [])
        if l == 0:
            w_in_1_got += list(got)
        (y_at, ya), got = _attn_fwd(proj, attn_sinks[l], carried=[w_out_rows[1], w_in_1[3]] if l == 0 else [])
        if l == 0:
            w_out_f[1] = got[0].reshape(D_MIX, d)
            w_in_t[1] = jnp.concatenate(w_in_1_got + [got[1]], axis=1).reshape(D_IN, d)
        projs.append(proj)
        cvs.append((cv, diff))
        ys.append((y_pc, y_at, ya))
        if l < DEPTH - 1:
            x_next, proj = _out_in_proj(xs[l], y_pc, y_at, w_out_f[l], vec(ln_g, l + 1), w_in_t[l + 1])
            xs.append(x_next)

    l = DEPTH - 1
    sq, g_final, dx = _out_proj_loss(xs[l], ys[l][0], ys[l][1], w_out_f[l], loss_target[0], row(final_g))

    l = 1
    dy, g_wout1 = _out_bwd(dx, ys[l][0], ys[l][1], w_out_f[l])
    (da, g_wp1, g_pw1, g_dw1, g_vec1), _ = _poolconv_bwd(
        projs[l], *cvs[l], dy, wp_bd[l], vec(pool_scale, l), dw_f[l], vec(conv_ln_g, l), vec(conv_ln_b, l), pw_f[l])
    (dattn, gs1), _ = _attn_bwd(projs[l], ys[l][2], dy, attn_sinks[l])
    dx, g_ln1, g_win_t1 = _in_bwd(da, dattn, xs[l], dx, vec(ln_g, l), w_in_t[l])
    l = 0
    dy, g_wout0 = _out_bwd(dx, ys[l][0], ys[l][1], w_out_f[l])
    (da, g_wp0, g_pw0, g_dw0, g_vec0), (r_win1, r_small1) = _poolconv_bwd(
        projs[l], *cvs[l], dy, wp_bd[l], vec(pool_scale, l), dw_f[l], vec(conv_ln_g, l), vec(conv_ln_b, l), pw_f[l],
        carried=[slabs(g_win_t1), _small_slabs(g_pw1, g_dw1, d)])
    (dattn, gs0), (r_wout0, r_wout1, r_small0) = _attn_bwd(
        projs[l], ys[l][2], dy, attn_sinks[l],
        carried=[slabs(g_wout0), slabs(g_wout1), _small_slabs(g_pw0, g_dw0, d)])
    dproj, g_win_t0 = _in_bwd_dw(da, dattn, xs[l], vec(ln_g, l))
    (dx, g_ln0), (r_win0,) = _in_bwd_dx(dproj, xs[l], dx, vec(ln_g, l), w_in_t[l], carried=[slabs(g_win_t0)])
    grad_x = dx[None]

    gv = jnp.stack([g_vec0, g_vec1], axis=0)
    rep_part = _pack_replicated(
        jnp.concatenate([g_ln0, g_ln1], axis=0), jnp.stack([_diag_blocks(g_wp0), _diag_blocks(g_wp1)], axis=0),
        gv[:, 0], gv[:, 1], gv[:, 2], gv[:, 3], jnp.stack([gs0[:, 0], gs1[:, 0]], axis=0), g_final, sq[0, 0], d)
    (r_rep,) = _final_exchange([rep_part])

    t = lambda a: jnp.swapaxes(a, 1, 2)
    win = [t(o) for o in _adamw_layers([r_win0, r_win1], t(w_in), t(m_w_in), t(v_w_in), "adamw_w_in")]
    wout = _adamw_layers([r_wout0, r_wout1], w_out, m_w_out, v_w_out, "adamw_w_out")
    pack_s = lambda pw_, dw_: jnp.stack([_pack_small(pw_[l], dw_[l], d) for l in range(DEPTH)], axis=0)
    small = _adamw_layers([r_small0, r_small1], pack_s(conv_pw, conv_dw), pack_s(m_conv_pw, m_conv_dw),
                          pack_s(v_conv_pw, v_conv_dw), "adamw_conv")
    small = [[_unpack_small(o[l], d) for l in range(DEPTH)] for o in small]
    *rep_grads, sq_sum = _unpack_replicated(_sum_rows(r_rep), d)
    loss = 0.5 / d * sq_sum
    rep_w = [ln_g, pool_w, pool_scale, conv_b, conv_ln_g, conv_ln_b, attn_sinks, final_g]
    rep_m = [m_ln_g, m_pool_w, m_pool_scale, m_conv_b, m_conv_ln_g, m_conv_ln_b, m_attn_sinks, m_final_g]
    rep_v = [v_ln_g, v_pool_w, v_pool_scale, v_conv_b, v_conv_ln_g, v_conv_ln_b, v_attn_sinks, v_final_g]
    rep = [rep_grads] + _adamw_tensors(rep_grads, rep_w, rep_m, rep_v)

    outs = []
    for k in range(4):
        r_ln, r_pool, r_scale, r_cb, r_lng, r_lnb, r_sinks, r_final = rep[k]
        s_pw = jnp.stack([small[k][l][0] for l in range(DEPTH)], axis=0)
        s_dw = jnp.stack([small[k][l][1] for l in range(DEPTH)], axis=0)
        outs += [r_ln, win[k], r_pool, r_scale, s_dw, r_cb, r_lng, r_lnb, s_pw, r_sinks, wout[k], r_final]
    return (loss, grad_x, *outs)
```

```python
import jax
import jax.numpy as jnp
from jax import lax
from jax.experimental import pallas as pl
from jax.experimental.pallas import tpu as pltpu

F32 = jnp.float32
MXU_DTYPE = jnp.bfloat16
EXCHANGE_DTYPE = jnp.bfloat16

N_DEV = 8
DEPTH = 2
POOL_WIDTH = 256
POOL_GROUP = 64
CONV_WIDTH = 256
CONV_KERNEL = 31
CONV_TAPS_PAD = 32
HEAD_DIM = 64
N_KV_HEADS = 2
Q_PER_KV = 4
N_Q_HEADS = 8
BLOCK = 128
D_MIX = 1024
D_IN = 2560
HALF_IN = 1280
EPS = 1e-6
SCALE = HEAD_DIM ** -0.5
NEG = -1e30

ADAM_LR = 0.001
ADAM_B1 = 0.9
ADAM_B2 = 0.999
ADAM_EPS = 1e-08
ADAM_WD = 0.01
ADAM_STEP = 10

HALO = 32
ROW_TILE = 512
IN_BWD_TILE = 512
POOLCONV_BWD_TILE = 1024
VMEM_LIMIT = 56 * 1024 * 1024

_NN = (((1,), (0,)), ((), ()))
_NT = (((1,), (1,)), ((), ()))
_TN = (((0,), (0,)), ((), ()))
_ANY = pl.BlockSpec(memory_space=pl.ANY)


def _mm(a, b, dims=_NN):
    return lax.dot_general(a.astype(MXU_DTYPE), b.astype(MXU_DTYPE), dims, preferred_element_type=F32)


def _sig(x):
    return 1.0 / (1.0 + jnp.exp(-x))


def _dsilu(z, s):
    return s * (1.0 + z * (1.0 - s))


def _params(n_grid):
    return pltpu.CompilerParams(dimension_semantics=("arbitrary",) * n_grid, vmem_limit_bytes=VMEM_LIMIT)


def _row_tile(rows, cap):
    t = min(rows, cap)
    while rows % t or t % 8:
        t -= 8
    return t


def _full(shape):
    return pl.BlockSpec(shape, lambda i: (0,) * len(shape))


def _layer_vec(stacked, layer):
    arr = stacked.reshape(stacked.shape[0], 1, stacked.shape[-1])
    return arr, pl.BlockSpec((None, 1, arr.shape[-1]), lambda i: (layer, 0, 0))


def _mesh_pos():
    return lax.axis_index("x"), lax.axis_index("y"), lax.axis_index("c")


class _Exchange:
    def __init__(self, sources):
        self.sources = list(sources)
        self.n = len(self.sources)
        self.gather = [s.ndim == 2 for s in self.sources]

    def out_shapes(self):
        return [jax.ShapeDtypeStruct((N_DEV,) + s.shape[-2:], s.dtype) for s in self.sources]

    def scratch(self):
        if not self.n:
            return []
        return [pltpu.SemaphoreType.DMA((7 * self.n,)), pltpu.SemaphoreType.DMA((7 * self.n,)),
                pltpu.SemaphoreType.DMA((self.n,))]

    def copies(self, src_refs, dst_refs, sems):
        send_sems, recv_sems, local_sems = sems
        x, y, c = _mesh_pos()
        me = 4 * x + 2 * y + c
        out = []
        for a, (src, dst) in enumerate(zip(src_refs, dst_refs)):
            out.append(pltpu.make_async_copy(src if self.gather[a] else src.at[me], dst.at[me], local_sems.at[a]))
            for k in range(1, N_DEV):
                tx, ty, tc = x ^ ((k >> 2) & 1), y ^ ((k >> 1) & 1), c ^ (k & 1)
                out.append(pltpu.make_async_remote_copy(
                    src_ref=src if self.gather[a] else src.at[4 * tx + 2 * ty + tc], dst_ref=dst.at[me],
                    send_sem=send_sems.at[7 * a + k - 1], recv_sem=recv_sems.at[7 * a + k - 1],
                    device_id=(tx, ty, tc), device_id_type=pl.DeviceIdType.MESH))
        return out

    def run(self, refs, first, last):
        if not self.n:
            return
        src_refs, dst_refs, sems = refs

        @pl.when(first)
        def _():
            for cp in self.copies(src_refs, dst_refs, sems):
                cp.start()

        @pl.when(last)
        def _():
            for cp in self.copies(src_refs, dst_refs, sems):
                cp.wait()


def _split_refs(refs, n_in, n_out, exch):
    ins = refs[:n_in]
    srcs = refs[n_in:n_in + exch.n]
    outs = refs[n_in + exch.n:n_in + exch.n + n_out]
    dsts = refs[n_in + exch.n + n_out:n_in + 2 * exch.n + n_out]
    rest = refs[n_in + 2 * exch.n + n_out:]
    sems = rest[len(rest) - 3:] if exch.n else ()
    scratch = rest[:len(rest) - 3] if exch.n else rest
    return ins, outs, scratch, (srcs, dsts, sems)


def _final_all_reduce(part):
    exch = _Exchange([part])

    def body(src_ref, dst_ref, sum_ref, stage_ref, stage_sem, send_sems, recv_sems, local_sems):
        xrefs = ((src_ref,), (dst_ref,), (send_sems, recv_sems, local_sems))
        for cp in exch.copies(*xrefs):
            cp.start()
        for cp in exch.copies(*xrefs):
            cp.wait()
        landed = pltpu.make_async_copy(dst_ref, stage_ref, stage_sem)
        landed.start()
        landed.wait()
        sum_ref[...] = _sum_partials(stage_ref)

    gathered = exch.out_shapes()[0]
    return pl.pallas_call(
        body, name="final_all_reduce",
        out_shape=[gathered, jax.ShapeDtypeStruct(part.shape, F32)],
        in_specs=[_ANY], out_specs=[_ANY, pl.BlockSpec(memory_space=pltpu.VMEM)],
        scratch_shapes=[pltpu.VMEM(gathered.shape, gathered.dtype), pltpu.SemaphoreType.DMA] + exch.scratch(),
    )(part)[1]


def _all_gather(shard, name):
    m_per, n = shard.shape

    def body(x_ref, out_ref, send_sems, recv_sems, local_sem):
        x, y, c = _mesh_pos()
        me, sibling = (x, y, c), (x, y, 1 - c)
        chips = [(1 - x, y), (x, 1 - y), (1 - x, 1 - y)]

        def rows(px, py, pc):
            return out_ref.at[pl.ds((4 * px + 2 * py + pc) * m_per, m_per), :]

        def copy(k, block, to, src=None):
            return pltpu.make_async_remote_copy(
                src_ref=rows(*block) if src is None else src, dst_ref=rows(*block),
                send_sem=send_sems.at[k], recv_sem=recv_sems.at[k],
                device_id=to, device_id_type=pl.DeviceIdType.MESH)

        mine = pltpu.make_async_copy(x_ref, rows(*me), local_sem)
        mine.start()
        first = [copy(0, me, sibling, src=x_ref)]
        first += [copy(1 + j, me, (*chip, c), src=x_ref) for j, chip in enumerate(chips)]
        for cp in first:
            cp.start()
        passed = [copy(4 + j, (*chip, c), sibling) for j, chip in enumerate(chips)]
        for j, chip in enumerate(chips):
            copy(1 + j, (*chip, c), me).wait_recv()
            passed[j].start()
        copy(0, sibling, me).wait_recv()
        for j, chip in enumerate(chips):
            copy(4 + j, (*chip, 1 - c), me).wait_recv()
        for cp in first + passed:
            cp.wait_send()
        mine.wait()

    return pl.pallas_call(
        body, name=name,
        out_shape=jax.ShapeDtypeStruct((N_DEV * m_per, n), shard.dtype),
        in_specs=[pl.BlockSpec(memory_space=pltpu.VMEM)],
        out_specs=pl.BlockSpec(memory_space=pltpu.VMEM),
        scratch_shapes=[pltpu.SemaphoreType.DMA((7,)), pltpu.SemaphoreType.DMA((7,)), pltpu.SemaphoreType.DMA],
        compiler_params=pltpu.CompilerParams(vmem_limit_bytes=VMEM_LIMIT),
    )(shard)


def _by_group(lane, v2, v4, v8, v16):
    return jnp.where(lane < 64, v2, jnp.where(lane < 128, v4, jnp.where(lane < 192, v8, v16)))


def _pool_count(t0, n):
    lane = lax.broadcasted_iota(jnp.int32, (1, POOL_WIDTH), 1)
    t = (t0 + lax.broadcasted_iota(jnp.int32, (n, 1), 0)).astype(F32)
    wnd = _by_group(lane, 2.0, 4.0, 8.0, 16.0)
    return jnp.minimum(t + 1.0, wnd)


def _pool_diff(u_ext, t0, ts):
    lane = lax.broadcasted_iota(jnp.int32, (1, POOL_WIDTH), 1)
    s2 = u_ext + pltpu.roll(u_ext, 1, 0)
    s4 = s2 + pltpu.roll(s2, 2, 0)
    s8 = s4 + pltpu.roll(s4, 4, 0)
    s16 = s8 + pltpu.roll(s8, 8, 0)
    pooled = _by_group(lane, s2, s4, s8, s16)[HALO:]
    return pooled / _pool_count(t0, ts) - u_ext[HALO:]


def _pool_diff_bwd(w, ts):
    n = w.shape[0]
    lane = lax.broadcasted_iota(jnp.int32, (1, POOL_WIDTH), 1)
    f2 = w + pltpu.roll(w, n - 1, 0)
    f4 = f2 + pltpu.roll(f2, n - 2, 0)
    f8 = f4 + pltpu.roll(f4, n - 4, 0)
    f16 = f8 + pltpu.roll(f8, n - 8, 0)
    return _by_group(lane, f2, f4, f8, f16)[:ts]


CONV_CHUNK = 64


def _conv_taps():
    return [(8 * m + r, r, m) for r in range(8) for m in range(4) if 8 * m + r < CONV_KERNEL]


def _store_shifted(dst_ref, x, up):
    n = x.shape[0]
    for r in range(8):
        dst_ref[r] = x if r == 0 else pltpu.roll(x, n - r if up else r, 0)


def _anticausal_conv(src_ref, dw_ref, out_ref, n_out):
    def chunk(c, carry):
        t0 = pl.multiple_of(c * CONV_CHUNK, CONV_CHUNK)
        acc = None
        for d, r, m in _conv_taps():
            term = dw_ref[pl.ds(CONV_KERNEL - 1 - d, 1), :] * src_ref[r, pl.ds(t0 + 8 * m, CONV_CHUNK), :]
            acc = term if acc is None else acc + term
        out_ref[pl.ds(t0, CONV_CHUNK), :] = acc
        return carry

    lax.fori_loop(0, n_out // CONV_CHUNK, chunk, 0)


def _depthwise_conv_weight_grad(x_ref, dout_up_ref, acc_ref, n_rows):
    acc_ref[...] = jnp.zeros_like(acc_ref)

    def chunk(c, carry):
        t0 = pl.multiple_of(c * CONV_CHUNK, CONV_CHUNK)
        xv = x_ref[pl.ds(t0, CONV_CHUNK), :]
        for d, r, m in _conv_taps():
            prod = xv * dout_up_ref[r, pl.ds(t0 + 8 * m, CONV_CHUNK), :]
            acc_ref[CONV_KERNEL - 1 - d] += jnp.sum(prod.reshape(CONV_CHUNK // 8, 8, prod.shape[-1]), axis=0)
        return carry

    lax.fori_loop(0, n_rows // CONV_CHUNK, chunk, 0)


def _layer_norm(cv):
    mu = jnp.mean(cv, axis=-1, keepdims=True)
    xc = cv - mu
    var = jnp.mean(xc * xc, axis=-1, keepdims=True)
    rstd = lax.rsqrt(var + EPS)
    return xc * rstd, rstd


PAIR_ROWS = 2 * BLOCK
PAIR_COLS = 4 * BLOCK
ATTN_BLOCKS_PER_STEP = 4


def _attn_blocks_per_step(s_len):
    qb = ATTN_BLOCKS_PER_STEP
    while (s_len // BLOCK) % qb:
        qb //= 2
    return qb


def _pair_rows(v0, v1):
    r = lax.broadcasted_iota(jnp.int32, (PAIR_ROWS, 1), 0)
    return jnp.where(r < BLOCK, v0, v1)


def _fill_attn_bias(bias_ref):
    rows = lax.broadcasted_iota(jnp.int32, (PAIR_ROWS, PAIR_COLS), 0)
    cols = lax.broadcasted_iota(jnp.int32, (PAIR_ROWS, PAIR_COLS), 1)
    key = cols & (2 * BLOCK - 1)
    dist = BLOCK + (rows & (BLOCK - 1)) - key
    in_band = (dist >= 0) & (dist < BLOCK)
    distf = dist.astype(F32)
    second = cols >= 2 * BLOCK
    for kh in range(N_KV_HEADS):
        slope_of = lambda j, a: 2.0 ** -(Q_PER_KV * kh + 2 * j + a + 1)
        slope = jnp.where(rows < BLOCK, jnp.where(second, slope_of(0, 1), slope_of(0, 0)),
                          jnp.where(second, slope_of(1, 1), slope_of(1, 0)))
        bias = -slope * distf
        bias_ref[0, kh] = jnp.where(in_band & (key >= BLOCK), bias, NEG)
        bias_ref[1, kh] = jnp.where(in_band, bias, NEG)


def _pair_block_matrix(x, x_swapped, kh):
    lo = lax.broadcasted_iota(jnp.int32, (1, 2 * HEAD_DIM), 1) < HEAD_DIM
    in_lo, in_hi = (x, x_swapped) if kh == 0 else (x_swapped, x)
    return jnp.concatenate([jnp.where(lo, in_lo, 0.0), jnp.where(lo, 0.0, in_hi)], axis=0).astype(MXU_DTYPE)


def _pair_queries(q, kh):
    return (_pair_stack(q, kh) * SCALE).astype(MXU_DTYPE)


def _pair_stack(a, kh):
    return jnp.concatenate([a[:, 2 * BLOCK * kh: 2 * BLOCK * kh + BLOCK],
                            a[:, 2 * BLOCK * kh + BLOCK: 2 * BLOCK * (kh + 1)]], axis=0)


def _pair_unstack(parts):
    return jnp.concatenate([p[BLOCK * j: BLOCK * (j + 1)] for p in parts for j in range(2)], axis=-1)


def _pair_softmax(s, kh, sinks_ref):
    ps, p_sinks = [], []
    for a in range(2):
        sa = s[:, 2 * BLOCK * a: 2 * BLOCK * (a + 1)]
        sink = _pair_rows(sinks_ref[Q_PER_KV * kh + a], sinks_ref[Q_PER_KV * kh + 2 + a])
        m = jnp.maximum(jnp.max(sa, axis=-1, keepdims=True), sink)
        e = jnp.exp(sa - m)
        es = jnp.exp(sink - m)
        inv = 1.0 / (jnp.sum(e, axis=-1, keepdims=True) + es)
        ps.append(e * inv)
        p_sinks.append(es * inv)
    return jnp.concatenate(ps, axis=-1), p_sinks


def _fold_pair_rows(t):
    return t[:HEAD_DIM, :2 * BLOCK] + t[HEAD_DIM:, 2 * BLOCK:]


def _in_proj(x, g, w_t, carried=()):
    s_len, d = x.shape
    ts = _row_tile(s_len, ROW_TILE)
    nt = s_len // ts
    exch = _Exchange(carried)

    def body(*refs):
        (x_ref, g_ref, w_ref), (o_ref,), _, xrefs = _split_refs(refs, 3, 1, exch)
        i = pl.program_id(0)
        exch.run(xrefs, i == 0, i == nt - 1)
        xv = x_ref[...]
        r = lax.rsqrt(jnp.mean(xv * xv, axis=-1, keepdims=True) + EPS)
        o_ref[...] = _mm(xv * r * g_ref[...], w_ref[...], _NT)

    res = pl.pallas_call(
        body, name="in_proj_carrier" if exch.n else "in_proj", grid=(nt,),
        in_specs=[pl.BlockSpec((ts, d), lambda i: (i, 0)), g[1], _full((D_IN, d))] + [_ANY] * exch.n,
        out_specs=[pl.BlockSpec((ts, D_IN), lambda i: (i, 0))] + [_ANY] * exch.n,
        out_shape=[jax.ShapeDtypeStruct((s_len, D_IN), F32)] + exch.out_shapes(),
        scratch_shapes=exch.scratch(),
        compiler_params=_params(1),
    )(x, g[0], w_t, *exch.sources)
    return res[0], res[1:]


def _poolconv_fwd(proj, wp, scale, dw, cb, lng, lnb, pw, carried=()):
    s_len = proj.shape[0]
    ts = _row_tile(s_len, ROW_TILE)
    hb = ts // HALO
    nt = s_len // ts
    exch = _Exchange(carried)

    def body(*refs):
        ins, (y_ref, cv_ref, diff_ref), _, xrefs = _split_refs(refs, 9, 3, exch)
        p_ref, ph_ref, wp_ref, sc_ref, dw_ref, cb_ref, lng_ref, lnb_ref, pw_ref = ins
        i = pl.program_id(0)
        exch.run(xrefs, i == 0, i == nt - 1)
        halo = jnp.where(i > 0, ph_ref[...], 0.0)
        y_mix, cv, diff = _pool_conv_mixers(p_ref[...], halo, i * ts, ts, wp_ref, sc_ref, dw_ref, cb_ref, lng_ref,
                                            lnb_ref, pw_ref)
        y_ref[...] = y_mix.astype(y_ref.dtype)
        cv_ref[...] = cv
        diff_ref[...] = diff

    res = pl.pallas_call(
        body, name="poolconv_fwd_carrier" if exch.n else "poolconv_fwd", grid=(nt,),
        in_specs=[pl.BlockSpec((ts, HALF_IN), lambda i: (i, 0)),
                  pl.BlockSpec((HALO, HALF_IN), lambda i: (jnp.maximum(i * hb - 1, 0), 0)),
                  _full((256, 256)), scale[1], _full((CONV_TAPS_PAD, 256)), cb[1], lng[1], lnb[1],
                  _full((256, 256))]
        + [_ANY] * exch.n,
        out_specs=[pl.BlockSpec((ts, 512), lambda i: (i, 0)), pl.BlockSpec((ts, 256), lambda i: (i, 0)),
                   pl.BlockSpec((ts, 256), lambda i: (i, 0))] + [_ANY] * exch.n,
        out_shape=[jax.ShapeDtypeStruct((s_len, 512), MXU_DTYPE), jax.ShapeDtypeStruct((s_len, 256), F32),
                   jax.ShapeDtypeStruct((s_len, 256), MXU_DTYPE)] + exch.out_shapes(),
        scratch_shapes=exch.scratch(),
        compiler_params=_params(1),
    )(proj, proj, wp, scale[0], dw, cb[0], lng[0], lnb[0], pw, *exch.sources)
    return res[:3], res[3:]


def _pool_conv_mixers(cur, halo, t0, ts, wp_ref, sc_ref, dw_ref, cb_ref, lng_ref, lnb_ref, pw_ref):
    ext = jnp.concatenate([halo, cur], axis=0)
    diff = _pool_diff(ext[:, 0:256], t0, ts).astype(MXU_DTYPE)
    gp = cur[:, 256:512]
    y_pool = _mm(diff, wp_ref[...]) * sc_ref[...] * (gp * _sig(gp))
    hh = ext[:, 512:768] * _sig(ext[:, 768:1024])
    shifted = [hh if r == 0 else pltpu.roll(hh, r, 0) for r in range(8)]
    cv = cb_ref[...]
    for dist, r, m in _conv_taps():
        cv = cv + dw_ref[pl.ds(CONV_KERNEL - 1 - dist, 1), :] * shifted[r][HALO - 8 * m: HALO - 8 * m + ts]
    n, _ = _layer_norm(cv)
    z = n * lng_ref[...] + lnb_ref[...]
    gc = cur[:, 1024:1280]
    y_conv = _mm(z * _sig(z), pw_ref[...]) * (gc * _sig(gc))
    return jnp.concatenate([y_pool, y_conv], axis=-1), cv, diff


def _attn_fwd(proj, sinks, carried=()):
    s_len = proj.shape[0]
    qb = _attn_blocks_per_step(s_len)
    ts = qb * BLOCK
    nt = s_len // ts
    exch = _Exchange(carried)

    def body(*refs):
        (p_ref, kvp_ref, sinks_ref), (y_ref, ya_ref), (bias_ref,), xrefs = _split_refs(refs, 3, 2, exch)
        i = pl.program_id(0)
        exch.run(xrefs, i == 0, i == nt - 1)

        @pl.when(i == 0)
        def _():
            _fill_attn_bias(bias_ref)

        probs = [(b, kh) for b in range(qb) for kh in range(N_KV_HEADS)]
        scores, v_bds = {}, {}
        for b in range(qb):
            r0 = BLOCK * b
            kv_prev = kvp_ref[...] if b == 0 else p_ref[r0 - BLOCK:r0, 512:768]
            kv2 = jnp.concatenate([kv_prev, p_ref[r0:r0 + BLOCK, 512:768]], axis=0)
            k2, v2 = kv2[:, :BLOCK], kv2[:, BLOCK:]
            k2_swapped, v2_swapped = pltpu.roll(k2, HEAD_DIM, 1), pltpu.roll(v2, HEAD_DIM, 1)
            variant = jnp.where(i == 0, 0, 1) if b == 0 else 1
            q = p_ref[r0:r0 + BLOCK, 0:512]
            for kh in range(N_KV_HEADS):
                k_bd = _pair_block_matrix(k2, k2_swapped, kh)
                v_bds[b, kh] = _pair_block_matrix(v2, v2_swapped, kh)
                scores[b, kh] = _mm(_pair_queries(q, kh), k_bd, _NT) + bias_ref[variant, kh]
        ps = {pr: _pair_softmax(scores[pr], pr[1], sinks_ref)[0] for pr in probs}
        outs = {pr: _mm(ps[pr], v_bds[pr]) for pr in probs}
        for b in range(qb):
            r0 = BLOCK * b
            ga = p_ref[r0:r0 + BLOCK, 768:1280]
            ya = _pair_unstack([outs[b, kh] for kh in range(N_KV_HEADS)])
            ya_ref[r0:r0 + BLOCK, :] = ya
            y_ref[r0:r0 + BLOCK, :] = (ya * (ga * _sig(ga))).astype(y_ref.dtype)

    res = pl.pallas_call(
        body, name="attn_fwd_carrier" if exch.n else "attn_fwd", grid=(nt,),
        in_specs=[pl.BlockSpec((ts, HALF_IN), lambda i: (i, 1)),
                  pl.BlockSpec((BLOCK, 256), lambda i: (jnp.maximum(i * qb - 1, 0), 7)),
                  pl.BlockSpec(memory_space=pltpu.SMEM)] + [_ANY] * exch.n,
        out_specs=[pl.BlockSpec((ts, 512), lambda i: (i, 0)), pl.BlockSpec((ts, 512), lambda i: (i, 0))]
        + [_ANY] * exch.n,
        out_shape=[jax.ShapeDtypeStruct((s_len, 512), MXU_DTYPE), jax.ShapeDtypeStruct((s_len, 512), F32)]
        + exch.out_shapes(),
        scratch_shapes=[pltpu.VMEM((2, N_KV_HEADS, PAIR_ROWS, PAIR_COLS), F32)] + exch.scratch(),
        compiler_params=_params(1),
    )(proj, proj, sinks, *exch.sources)
    return res[:2], res[2:]


def _out_in_proj(x, y_pc, y_at, w_out, g_next, w_t_next):
    s_len, d = x.shape
    ts = _row_tile(s_len, ROW_TILE)

    def body(x_ref, a_ref, b_ref, w_ref, g_ref, wn_ref, o_ref, p_ref):
        y = jnp.concatenate([a_ref[...], b_ref[...]], axis=-1)
        xv = x_ref[...] + _mm(y, w_ref[...])
        o_ref[...] = xv
        r = lax.rsqrt(jnp.mean(xv * xv, axis=-1, keepdims=True) + EPS)
        p_ref[...] = _mm(xv * r * g_ref[...], wn_ref[...], _NT)

    tile = pl.BlockSpec((ts, d), lambda i: (i, 0))
    half = pl.BlockSpec((ts, 512), lambda i: (i, 0))
    return pl.pallas_call(
        body, name="out_in_proj", grid=(s_len // ts,),
        in_specs=[tile, half, half, _full((D_MIX, d)), g_next[1], _full((D_IN, d))],
        out_specs=[tile, pl.BlockSpec((ts, D_IN), lambda i: (i, 0))],
        out_shape=[jax.ShapeDtypeStruct((s_len, d), F32), jax.ShapeDtypeStruct((s_len, D_IN), F32)],
        compiler_params=_params(1),
    )(x, y_pc, y_at, w_out, g_next[0], w_t_next)


def _out_proj_loss(x, y_pc, y_at, w_out, target, g):
    s_len, d = x.shape
    ts = _row_tile(s_len, ROW_TILE)

    def body(x_ref, a_ref, b_ref, w_ref, t_ref, g_ref, sq_ref, dg_ref, dx_ref):
        i = pl.program_id(0)
        y = jnp.concatenate([a_ref[...], b_ref[...]], axis=-1)
        xv = x_ref[...] + _mm(y, w_ref[...])
        gv = g_ref[...]
        r = lax.rsqrt(jnp.mean(xv * xv, axis=-1, keepdims=True) + EPS)
        xr = xv * r
        err = xr * gv - t_ref[...]
        dout = err * (1.0 / d)
        w = dout * gv
        dx_ref[...] = r * (w - xr * jnp.mean(w * xr, axis=-1, keepdims=True))

        @pl.when(i == 0)
        def _():
            sq_ref[...] = jnp.zeros_like(sq_ref)
            dg_ref[...] = jnp.zeros_like(dg_ref)

        sq = jnp.sum(jnp.sum(err * err, axis=-1, keepdims=True), axis=0, keepdims=True)
        sq_ref[...] += jnp.broadcast_to(sq, sq_ref.shape)
        dg_ref[...] += jnp.sum(dout * xr, axis=0, keepdims=True)

    tile = pl.BlockSpec((ts, d), lambda i: (i, 0))
    half = pl.BlockSpec((ts, 512), lambda i: (i, 0))
    return pl.pallas_call(
        body, name="out_proj_loss", grid=(s_len // ts,),
        in_specs=[tile, half, half, _full((D_MIX, d)), tile, _full((1, d))],
        out_specs=[_full((1, 128)), _full((1, d)), tile],
        out_shape=[jax.ShapeDtypeStruct((1, 128), F32), jax.ShapeDtypeStruct((1, d), F32),
                   jax.ShapeDtypeStruct((s_len, d), F32)],
        compiler_params=_params(1),
    )(x, y_pc, y_at, w_out, target, g)


def _out_bwd(dxo, y_pc, y_at, w_out):
    s_len, d = dxo.shape
    ts = _row_tile(s_len, ROW_TILE)
    nt = s_len // ts

    def body(dx_ref, a_ref, b_ref, w_ref, dy_ref, gw_ref, acc_ref):
        i = pl.program_id(0)
        dxv = dx_ref[...].astype(MXU_DTYPE)
        dy_ref[...] = _mm(dxv, w_ref[...], _NT)

        @pl.when(i == 0)
        def _():
            acc_ref[...] = jnp.zeros_like(acc_ref)

        y = jnp.concatenate([a_ref[...], b_ref[...]], axis=-1)
        acc_ref[...] += _mm(y, dxv, _TN)

        @pl.when(i == nt - 1)
        def _():
            gw_ref[...] = acc_ref[...].astype(gw_ref.dtype)

    return pl.pallas_call(
        body, name="out_bwd", grid=(nt,),
        in_specs=[pl.BlockSpec((ts, d), lambda i: (i, 0)), pl.BlockSpec((ts, 512), lambda i: (i, 0)),
                  pl.BlockSpec((ts, 512), lambda i: (i, 0)), _full((D_MIX, d))],
        out_specs=[pl.BlockSpec((ts, D_MIX), lambda i: (i, 0)), _full((D_MIX, d))],
        out_shape=[jax.ShapeDtypeStruct((s_len, D_MIX), F32), jax.ShapeDtypeStruct((D_MIX, d), EXCHANGE_DTYPE)],
        scratch_shapes=[pltpu.VMEM((D_MIX, d), F32)],
        compiler_params=_params(1),
    )(dxo, y_pc, y_at, w_out)


def _poolconv_bwd(proj, cv, diff, dy, wp, scale, dw, lng, lnb, pw, carried=()):
    s_len = proj.shape[0]
    ts = _row_tile(s_len, POOLCONV_BWD_TILE)
    hb = ts // HALO
    nt = s_len // ts
    last_halo = s_len // HALO - 1
    n = ts + HALO
    exch = _Exchange(carried)

    def body(*refs):
        ins, outs, (up_ref, hh_ref, dhh_ref, gdw_acc_ref), xrefs = _split_refs(refs, 13, 5, exch)
        (p_ref, pn_ref, cv_ref, cvn_ref, diff_ref, dy_ref, dyn_ref, wp_ref, sc_ref, dw_ref, lng_ref, lnb_ref,
         pw_ref) = ins
        da_ref, gwp_ref, gpw_ref, gdw_ref, gvec_ref = outs
        i = pl.program_id(0)
        exch.run(xrefs, i == 0, i == nt - 1)
        has_next = i < nt - 1
        cur = p_ref[...]
        nxt = jnp.where(has_next, pn_ref[...], 0.0)
        dyx = jnp.concatenate([dy_ref[...], jnp.where(has_next, dyn_ref[...], 0.0)], axis=0)
        row = lax.broadcasted_iota(jnp.int32, (n, 1), 0)
        in_seq = (row < ts) | has_next
        scale_v = sc_ref[...]

        cvx = jnp.concatenate([cv_ref[...], jnp.where(has_next, cvn_ref[...], 0.0)], axis=0)
        nrm, rstd = _layer_norm(cvx)
        z = nrm * lng_ref[...] + lnb_ref[...]
        sz = _sig(z)
        sw = z * sz
        gc = jnp.concatenate([cur[:, 1024:1280], nxt[:, 1024:1280]], axis=0)
        sgc = _sig(gc)
        yc = _mm(sw, pw_ref[...])
        dyc = dyx[:, 256:512]
        d_yc = dyc * (gc * sgc)
        d_gc = (dyc * yc * _dsilu(gc, sgc))[:ts]
        d_z = _mm(d_yc, pw_ref[...], _NT) * _dsilu(z, sz)
        d_n = d_z * lng_ref[...]
        d_cv = rstd * (d_n - jnp.mean(d_n, axis=-1, keepdims=True)
                       - nrm * jnp.mean(d_n * nrm, axis=-1, keepdims=True))
        d_cv = jnp.where(in_seq, d_cv, 0.0)
        _store_shifted(up_ref, d_cv, up=True)
        _anticausal_conv(up_ref, dw_ref, dhh_ref, ts)
        d_hh = dhh_ref[...]
        a_c, sb_c = cur[:, 512:768], _sig(cur[:, 768:1024])
        hh_ref[...] = a_c * sb_c
        d_a = d_hh * sb_c
        d_b = d_hh * a_c * sb_c * (1.0 - sb_c)
        d_cv_t = d_cv[:ts]
        _depthwise_conv_weight_grad(hh_ref, up_ref, gdw_acc_ref, ts)

        diff = diff_ref[...]
        raw = _mm(diff, wp_ref[...])
        gp = jnp.concatenate([cur[:, 256:512], nxt[:, 256:512]], axis=0)
        sgp = _sig(gp)
        dyp = dyx[:, 0:256]
        d_yp = dyp * (gp * sgp)
        d_gp = dyp[:ts] * (raw * scale_v) * _dsilu(gp, sgp)[:ts]
        d_raw = d_yp * scale_v
        d_diff = _mm(d_raw, wp_ref[...], _NT)
        w = jnp.where(in_seq, d_diff / _pool_count(i * ts, n), 0.0)
        d_u = _pool_diff_bwd(w, ts) - d_diff[:ts]

        da_ref[...] = jnp.concatenate([d_u, d_gp, d_a, d_b, d_gc], axis=-1).astype(da_ref.dtype)

        @pl.when(i == 0)
        def _():
            gwp_ref[...] = jnp.zeros_like(gwp_ref)
            gpw_ref[...] = jnp.zeros_like(gpw_ref)
            gdw_ref[...] = jnp.zeros_like(gdw_ref)
            gvec_ref[...] = jnp.zeros_like(gvec_ref)

        gwp_ref[...] += _mm(diff, d_raw[:ts], _TN)
        gpw_ref[...] += _mm(sw[:ts], d_yc[:ts], _TN)
        gdw_ref[...] += jnp.sum(gdw_acc_ref[...], axis=1)
        zero_row = jnp.zeros((1, 256), F32)
        gvec_ref[...] += jnp.concatenate([
            jnp.sum(d_yp[:ts] * raw, axis=0, keepdims=True),
            jnp.sum(d_cv_t, axis=0, keepdims=True),
            jnp.sum((d_z * nrm)[:ts], axis=0, keepdims=True),
            jnp.sum(d_z[:ts], axis=0, keepdims=True),
            zero_row, zero_row, zero_row, zero_row], axis=0)

    nxt_halo = lambda i: (jnp.minimum((i + 1) * hb, last_halo), 0)
    res = pl.pallas_call(
        body, name="poolconv_bwd_carrier" if exch.n else "poolconv_bwd", grid=(nt,),
        in_specs=[pl.BlockSpec((ts, HALF_IN), lambda i: (i, 0)), pl.BlockSpec((HALO, HALF_IN), nxt_halo),
                  pl.BlockSpec((ts, 256), lambda i: (i, 0)), pl.BlockSpec((HALO, 256), nxt_halo),
                  pl.BlockSpec((ts, 256), lambda i: (i, 0)),
                  pl.BlockSpec((ts, 512), lambda i: (i, 0)), pl.BlockSpec((HALO, 512), nxt_halo),
                  _full((256, 256)), scale[1], _full((CONV_TAPS_PAD, 256)), lng[1], lnb[1], _full((256, 256))]
        + [_ANY] * exch.n,
        out_specs=[pl.BlockSpec((ts, HALF_IN), lambda i: (i, 0)), _full((256, 256)), _full((256, 256)),
                   _full((CONV_TAPS_PAD, 256)), _full((8, 256))] + [_ANY] * exch.n,
        out_shape=[jax.ShapeDtypeStruct((s_len, HALF_IN), MXU_DTYPE), jax.ShapeDtypeStruct((256, 256), F32),
                   jax.ShapeDtypeStruct((256, 256), F32), jax.ShapeDtypeStruct((CONV_TAPS_PAD, 256), F32),
                   jax.ShapeDtypeStruct((8, 256), F32)] + exch.out_shapes(),
        scratch_shapes=[pltpu.VMEM((8, n, CONV_WIDTH), F32), pltpu.VMEM((ts, CONV_WIDTH), F32),
                        pltpu.VMEM((ts, CONV_WIDTH), F32), pltpu.VMEM((CONV_TAPS_PAD, 8, CONV_WIDTH), F32)]
        + exch.scratch(),
        compiler_params=_params(1),
    )(proj, proj, cv, cv, diff, dy, dy, wp, scale[0], dw, lng[0], lnb[0], pw, *exch.sources)
    return res[:5], res[5:]


DQ0, DKC0, DVC0, DKP0, DVP0, DGA0, DATTN_W = 0, 512, 640, 768, 896, 1024, 1536


def _attn_bwd(proj, ya, dy, sinks, carried=()):
    s_len = proj.shape[0]
    qb = _attn_blocks_per_step(s_len)
    ts = qb * BLOCK
    nt = s_len // ts
    exch = _Exchange(carried)

    def body(*refs):
        ins, (o_ref, gs_ref), (bias_ref,), xrefs = _split_refs(refs, 5, 2, exch)
        p_ref, kvp_ref, ya_ref, dy_ref, sinks_ref = ins
        i = pl.program_id(0)
        exch.run(xrefs, i == 0, i == nt - 1)

        @pl.when(i == 0)
        def _():
            _fill_attn_bias(bias_ref)
            gs_ref[...] = jnp.zeros_like(gs_ref)

        probs = [(b, kh) for b in range(qb) for kh in range(N_KV_HEADS)]
        kv_heads = range(N_KV_HEADS)
        scores, k_bds, v_bds, q2s, do2s, dyas, gas, sgas = {}, {}, {}, {}, {}, {}, {}, {}
        for b in range(qb):
            r0 = BLOCK * b
            kv_prev = kvp_ref[...] if b == 0 else p_ref[r0 - BLOCK:r0, 512:768]
            kv2 = jnp.concatenate([kv_prev, p_ref[r0:r0 + BLOCK, 512:768]], axis=0)
            k2, v2 = kv2[:, :BLOCK], kv2[:, BLOCK:]
            k2_swapped, v2_swapped = pltpu.roll(k2, HEAD_DIM, 1), pltpu.roll(v2, HEAD_DIM, 1)
            variant = jnp.where(i == 0, 0, 1) if b == 0 else 1
            q = p_ref[r0:r0 + BLOCK, 0:512]
            gas[b] = p_ref[r0:r0 + BLOCK, 768:1280]
            dyas[b] = dy_ref[r0:r0 + BLOCK, :]
            sgas[b] = _sig(gas[b])
            d_o = dyas[b] * (gas[b] * sgas[b])
            for kh in kv_heads:
                k_bds[b, kh] = _pair_block_matrix(k2, k2_swapped, kh)
                v_bds[b, kh] = _pair_block_matrix(v2, v2_swapped, kh)
                q2s[b, kh] = _pair_queries(q, kh)
                do2s[b, kh] = _pair_stack(d_o, kh)
                scores[b, kh] = _mm(q2s[b, kh], k_bds[b, kh], _NT) + bias_ref[variant, kh]
        ps, p_sinks, dps, dss, dqs, dks, dvs = {}, {}, {}, {}, {}, {}, {}
        d_sinks = [None] * N_Q_HEADS

        def softmax_stage(pr):
            ps[pr], p_sinks[pr] = _pair_softmax(scores[pr], pr[1], sinks_ref)

        def value_stage(pr):
            dps[pr] = _mm(do2s[pr], v_bds[pr], _NT)

        def score_grad_stage(pr):
            p, dp, kh = ps[pr], dps[pr], pr[1]
            ds_halves = []
            for a in range(2):
                cols = slice(2 * BLOCK * a, 2 * BLOCK * (a + 1))
                delta = jnp.sum(p[:, cols] * dp[:, cols], axis=-1, keepdims=True)
                ds_halves.append(p[:, cols] * (dp[:, cols] - delta))
                dsink = -p_sinks[pr][a] * delta
                for j in range(2):
                    part = jnp.sum(dsink[BLOCK * j: BLOCK * (j + 1)], axis=0, keepdims=True)
                    h = Q_PER_KV * kh + 2 * j + a
                    d_sinks[h] = part if d_sinks[h] is None else d_sinks[h] + part
            dss[pr] = jnp.concatenate(ds_halves, axis=-1)

        def operand_grad_stage(pr):
            dqs[pr] = _mm(dss[pr], k_bds[pr]) * SCALE
            dks[pr] = _fold_pair_rows(_mm(q2s[pr], dss[pr], _TN))
            dvs[pr] = _fold_pair_rows(_mm(do2s[pr], ps[pr], _TN))

        for stage in (softmax_stage, value_stage, score_grad_stage, operand_grad_stage):
            for pr in probs:
                stage(pr)
        for b in range(qb):
            r0 = BLOCK * b
            dk = jnp.concatenate([dks[b, kh] for kh in kv_heads], axis=0).T
            dv = jnp.concatenate([dvs[b, kh] for kh in kv_heads], axis=0).T
            d_ga = dyas[b] * ya_ref[r0:r0 + BLOCK, :] * _dsilu(gas[b], sgas[b])
            o_ref[r0:r0 + BLOCK, :] = jnp.concatenate(
                [_pair_unstack([dqs[b, kh] for kh in kv_heads]), dk[BLOCK:], dv[BLOCK:], dk[:BLOCK], dv[:BLOCK],
                 d_ga], axis=-1)
        gs_ref[...] += jnp.broadcast_to(jnp.concatenate(d_sinks, axis=0), gs_ref.shape)

    res = pl.pallas_call(
        body, name="attn_bwd_carrier" if exch.n else "attn_bwd", grid=(nt,),
        in_specs=[pl.BlockSpec((ts, HALF_IN), lambda i: (i, 1)),
                  pl.BlockSpec((BLOCK, 256), lambda i: (jnp.maximum(i * qb - 1, 0), 7)),
                  pl.BlockSpec((ts, 512), lambda i: (i, 0)), pl.BlockSpec((ts, 512), lambda i: (i, 1)),
                  pl.BlockSpec(memory_space=pltpu.SMEM)] + [_ANY] * exch.n,
        out_specs=[pl.BlockSpec((ts, DATTN_W), lambda i: (i, 0)), _full((N_Q_HEADS, 128))] + [_ANY] * exch.n,
        out_shape=[jax.ShapeDtypeStruct((s_len, DATTN_W), F32), jax.ShapeDtypeStruct((N_Q_HEADS, 128), F32)]
        + exch.out_shapes(),
        scratch_shapes=[pltpu.VMEM((2, N_KV_HEADS, PAIR_ROWS, PAIR_COLS), F32)] + exch.scratch(),
        compiler_params=_params(1),
    )(proj, proj, ya, dy, sinks, *exch.sources)
    return res[:2], res[2:]


def _in_bwd(da, dattn, x, dxo, g, w_t):
    s_len, d = x.shape
    ts = _row_tile(s_len, IN_BWD_TILE)
    bpt = ts // BLOCK
    nt = s_len // ts
    last_block = s_len // BLOCK - 1

    def body(da_ref, dat_ref, nxt_ref, x_ref, dxo_ref, g_ref, w_ref, dx_ref, dg_ref, gw_ref, acc_ref, stage_ref,
             stage_sem):
        i = pl.program_id(0)
        dat = dat_ref[...]
        nxt = jnp.where(i < nt - 1, nxt_ref[...], 0.0)
        shifted = jnp.concatenate([dat[BLOCK:, DKP0:DGA0], nxt], axis=0) if bpt > 1 else nxt
        dkv = dat[:, DKC0:DKP0] + shifted
        dproj = jnp.concatenate([da_ref[...], dat[:, DQ0:DKC0].astype(MXU_DTYPE), dkv.astype(MXU_DTYPE),
                                 dat[:, DGA0:DATTN_W].astype(MXU_DTYPE)], axis=-1)
        d_h = _mm(dproj, w_ref[...])
        xv = x_ref[...]
        gv = g_ref[...]
        r = lax.rsqrt(jnp.mean(xv * xv, axis=-1, keepdims=True) + EPS)
        xr = xv * r
        w = d_h * gv
        dx_ref[...] = dxo_ref[...] + r * (w - xr * jnp.mean(w * xr, axis=-1, keepdims=True))

        @pl.when(i == 0)
        def _():
            dg_ref[...] = jnp.zeros_like(dg_ref)
            acc_ref[...] = jnp.zeros_like(acc_ref)

        dg_ref[...] += jnp.sum(d_h * xr, axis=0, keepdims=True)
        acc_ref[...] += _mm(dproj, xr * gv, _TN)

        @pl.when(i == nt - 1)
        def _():
            stage_ref[...] = acc_ref[...].astype(stage_ref.dtype)
            out = pltpu.make_async_copy(stage_ref, gw_ref, stage_sem)
            out.start()
            out.wait()

    return pl.pallas_call(
        body, name="in_bwd", grid=(nt,),
        in_specs=[pl.BlockSpec((ts, HALF_IN), lambda i: (i, 0)),
                  pl.BlockSpec((ts, DATTN_W), lambda i: (i, 0)),
                  pl.BlockSpec((BLOCK, 256), lambda i: (jnp.minimum((i + 1) * bpt, last_block), 3)),
                  pl.BlockSpec((ts, d), lambda i: (i, 0)), pl.BlockSpec((ts, d), lambda i: (i, 0)),
                  g[1], pl.BlockSpec((D_IN, d), lambda i: (0, 0), pipeline_mode=pl.Buffered(1))],
        out_specs=[pl.BlockSpec((ts, d), lambda i: (i, 0)), _full((1, d)), _ANY],
        out_shape=[jax.ShapeDtypeStruct((s_len, d), F32), jax.ShapeDtypeStruct((1, d), F32),
                   jax.ShapeDtypeStruct((D_IN, d), EXCHANGE_DTYPE)],
        scratch_shapes=[pltpu.VMEM((D_IN, d), F32), pltpu.VMEM((D_IN, d), EXCHANGE_DTYPE), pltpu.SemaphoreType.DMA],
        compiler_params=_params(1),
    )(da, dattn, dattn, x, dxo, g[0], w_t)


def _in_bwd_dw(da, dattn, x, g):
    s_len, d = x.shape
    ts = _row_tile(s_len, IN_BWD_TILE)
    bpt = ts // BLOCK
    nt = s_len // ts
    last_block = s_len // BLOCK - 1

    def body(da_ref, dat_ref, nxt_ref, x_ref, g_ref, dp_ref, gw_ref, acc_ref, stage_ref, stage_sem):
        i = pl.program_id(0)
        dat = dat_ref[...]
        nxt = jnp.where(i < nt - 1, nxt_ref[...], 0.0)
        shifted = jnp.concatenate([dat[BLOCK:, DKP0:DGA0], nxt], axis=0) if bpt > 1 else nxt
        dkv = dat[:, DKC0:DKP0] + shifted
        dproj = jnp.concatenate([da_ref[...], dat[:, DQ0:DKC0].astype(MXU_DTYPE), dkv.astype(MXU_DTYPE),
                                 dat[:, DGA0:DATTN_W].astype(MXU_DTYPE)], axis=-1)
        dp_ref[...] = dproj
        xv = x_ref[...]
        r = lax.rsqrt(jnp.mean(xv * xv, axis=-1, keepdims=True) + EPS)

        @pl.when(i == 0)
        def _():
            acc_ref[...] = jnp.zeros_like(acc_ref)

        acc_ref[...] += _mm(dproj, xv * r * g_ref[...], _TN)

        @pl.when(i == nt - 1)
        def _():
            stage_ref[...] = acc_ref[...].astype(stage_ref.dtype)
            out = pltpu.make_async_copy(stage_ref, gw_ref, stage_sem)
            out.start()
            out.wait()

    return pl.pallas_call(
        body, name="in_bwd_dw", grid=(nt,),
        in_specs=[pl.BlockSpec((ts, HALF_IN), lambda i: (i, 0)),
                  pl.BlockSpec((ts, DATTN_W), lambda i: (i, 0)),
                  pl.BlockSpec((BLOCK, 256), lambda i: (jnp.minimum((i + 1) * bpt, last_block), 3)),
                  pl.BlockSpec((ts, d), lambda i: (i, 0)), g[1]],
        out_specs=[pl.BlockSpec((ts, D_IN), lambda i: (i, 0)), _ANY],
        out_shape=[jax.ShapeDtypeStruct((s_len, D_IN), MXU_DTYPE), jax.ShapeDtypeStruct((D_IN, d), EXCHANGE_DTYPE)],
        scratch_shapes=[pltpu.VMEM((D_IN, d), F32), pltpu.VMEM((D_IN, d), EXCHANGE_DTYPE), pltpu.SemaphoreType.DMA],
        compiler_params=_params(1),
    )(da, dattn, dattn, x, g[0])


def _in_bwd_dx(dproj, x, dxo, g, w_t, carried=()):
    s_len, d = x.shape
    ts = _row_tile(s_len, ROW_TILE)
    nt = s_len // ts
    exch = _Exchange(carried)

    def body(*refs):
        (dp_ref, x_ref, dxo_ref, g_ref, w_ref), (dx_ref, dg_ref), _, xrefs = _split_refs(refs, 5, 2, exch)
        i = pl.program_id(0)
        exch.run(xrefs, i == 0, i == nt - 1)
        d_h = _mm(dp_ref[...], w_ref[...])
        xv = x_ref[...]
        r = lax.rsqrt(jnp.mean(xv * xv, axis=-1, keepdims=True) + EPS)
        xr = xv * r
        w = d_h * g_ref[...]
        dx_ref[...] = dxo_ref[...] + r * (w - xr * jnp.mean(w * xr, axis=-1, keepdims=True))

        @pl.when(i == 0)
        def _():
            dg_ref[...] = jnp.zeros_like(dg_ref)

        dg_ref[...] += jnp.sum(d_h * xr, axis=0, keepdims=True)

    tile = pl.BlockSpec((ts, d), lambda i: (i, 0))
    res = pl.pallas_call(
        body, name="in_bwd_dx_carrier" if exch.n else "in_bwd_dx", grid=(nt,),
        in_specs=[pl.BlockSpec((ts, D_IN), lambda i: (i, 0)), tile, tile, g[1], _full((D_IN, d))]
        + [_ANY] * exch.n,
        out_specs=[tile, _full((1, d))] + [_ANY] * exch.n,
        out_shape=[jax.ShapeDtypeStruct((s_len, d), F32), jax.ShapeDtypeStruct((1, d), F32)] + exch.out_shapes(),
        scratch_shapes=exch.scratch(),
        compiler_params=_params(1),
    )(dproj, x, dxo, g[0], w_t, *exch.sources)
    return res[:2], res[2:]


def _sum_partials(p_ref):
    g = p_ref[0].astype(F32)
    for k in range(1, N_DEV):
        g = g + p_ref[k].astype(F32)
    return g


def _adamw_step(g, w, m, v):
    nm = ADAM_B1 * m + (1.0 - ADAM_B1) * g
    nv = ADAM_B2 * v + (1.0 - ADAM_B2) * (g * g)
    m_hat = nm / (1.0 - ADAM_B1 ** ADAM_STEP)
    v_hat = nv / (1.0 - ADAM_B2 ** ADAM_STEP)
    return -ADAM_LR * (m_hat / (jnp.sqrt(v_hat) + ADAM_EPS) + ADAM_WD * w), nm, nv


def _adamw_layers(parts, w, m, v, name):
    _, rows, n = w.shape
    tr = _row_tile(rows, 64)

    def body(*refs):
        p_refs = refs[:DEPTH]
        w_ref, m_ref, v_ref, g_ref, d_ref, nm_ref, nv_ref = refs[DEPTH:]
        for l in range(DEPTH):
            g = _sum_partials(p_refs[l])
            g_ref[l] = g
            d_ref[l], nm_ref[l], nv_ref[l] = _adamw_step(g, w_ref[l], m_ref[l], v_ref[l])

    tile = pl.BlockSpec((DEPTH, tr, n), lambda i: (0, i, 0))
    shape = jax.ShapeDtypeStruct(w.shape, F32)
    return pl.pallas_call(
        body, name=name, grid=(rows // tr,),
        in_specs=[pl.BlockSpec((N_DEV, tr, n), lambda i: (0, i, 0))] * DEPTH + [tile] * 3,
        out_specs=[tile] * 4, out_shape=[shape] * 4,
        compiler_params=_params(1),
    )(*parts, w, m, v)


def _adamw_tensors(gs, ws, ms, vs):
    n = len(ws)
    shapes = [w.shape for w in ws]
    two_d = lambda a: a.reshape(1, -1) if a.ndim == 1 else a
    ops = [two_d(a) for a in list(gs) + list(ws) + list(ms) + list(vs)]

    def body(*refs):
        ins, outs = refs[:4 * n], refs[4 * n:]
        for k in range(n):
            g, w, m, v = (ins[j * n + k][...] for j in range(4))
            outs[k][...], outs[n + k][...], outs[2 * n + k][...] = _adamw_step(g, w, m, v)

    vmem = pl.BlockSpec(memory_space=pltpu.VMEM)
    res = pl.pallas_call(
        body, name="adamw_replicated", in_specs=[vmem] * (4 * n), out_specs=[vmem] * (3 * n),
        out_shape=[jax.ShapeDtypeStruct(o.shape, F32) for o in ops[n:2 * n]] * 3,
    )(*ops)
    return [[res[j * n + k].reshape(shapes[k]) for k in range(n)] for j in range(3)]


def _pad_rows(a, mult):
    pad = (-a.shape[0]) % mult
    return a if pad == 0 else jnp.concatenate([a, jnp.zeros((pad, a.shape[1]), a.dtype)], axis=0)


def _dw_rows(conv_dw_l):
    return jnp.pad(jnp.swapaxes(conv_dw_l, 0, 1), ((0, 0), (0, CONV_TAPS_PAD - CONV_KERNEL)))


def _pack_small(pw_l, dw_l, d):
    rows = jnp.concatenate([pw_l.reshape(-1, d), _dw_rows(dw_l).reshape(-1, d)], axis=0)
    return _pad_rows(rows, 8)


def _small_slabs(g_pw, g_dw, d):
    a = g_pw.reshape(N_DEV, -1, d)
    b = jnp.swapaxes(g_dw, 0, 1).reshape(N_DEV, -1, d)
    used = a.shape[1] + b.shape[1]
    return jnp.concatenate([a, b, jnp.zeros((N_DEV, (-used) % 8, d), g_pw.dtype)], axis=1)


def _unpack_small(rows, d):
    c = CONV_WIDTH // N_DEV
    n_pw = c * CONV_WIDTH // d
    n_dw = c * CONV_TAPS_PAD // d
    pw = rows[:n_pw].reshape(c, CONV_WIDTH)
    dw = jnp.swapaxes(rows[n_pw:n_pw + n_dw].reshape(c, CONV_TAPS_PAD), 0, 1)[:CONV_KERNEL]
    return pw, dw


def _pack_replicated(ln_g, pool_w, pool_scale, conv_b, conv_ln_g, conv_ln_b, attn_sinks, final_g, scalar, d):
    sinks = jnp.pad(attn_sinks, ((0, 0), (0, 256 - N_Q_HEADS)))
    small = jnp.concatenate([pool_scale, conv_b, conv_ln_g, conv_ln_b, sinks], axis=0)
    small = _pad_rows(small, d // 256)
    last = jnp.pad(scalar.reshape(1, 1), ((0, 0), (0, d - 1)))
    return _pad_rows(jnp.concatenate([ln_g.reshape(-1, d), final_g.reshape(-1, d), pool_w.reshape(-1, d),
                                      small.reshape(-1, d), last], axis=0), 8)


def _unpack_replicated(rows, d):
    n_pool = DEPTH * 4 * POOL_GROUP * POOL_GROUP // d
    n_small = -(-5 * DEPTH * 256 // d)
    ln_g = rows[:DEPTH]
    final_g = rows[DEPTH]
    pool_w = rows[DEPTH + 1: DEPTH + 1 + n_pool].reshape(DEPTH, 4, POOL_GROUP, POOL_GROUP)
    small = rows[DEPTH + 1 + n_pool: DEPTH + 1 + n_pool + n_small].reshape(-1, 256)[: 5 * DEPTH]
    pool_scale, conv_b, conv_ln_g, conv_ln_b = (small[DEPTH * k: DEPTH * (k + 1)] for k in range(4))
    sinks = small[4 * DEPTH: 5 * DEPTH, :N_Q_HEADS]
    scalar = rows[DEPTH + 1 + n_pool + n_small, 0]
    return ln_g, pool_w, pool_scale, conv_b, conv_ln_g, conv_ln_b, sinks, final_g, scalar


def _block_diag(pool_w):
    wide = jnp.tile(pool_w.reshape(POOL_WIDTH, POOL_GROUP), (1, POOL_WIDTH // POOL_GROUP))
    rows = lax.broadcasted_iota(jnp.int32, wide.shape, 0) // POOL_GROUP
    cols = lax.broadcasted_iota(jnp.int32, wide.shape, 1) // POOL_GROUP
    return jnp.where(rows == cols, wide, jnp.zeros_like(wide))


def _diag_blocks(mat):
    return jnp.stack([mat[POOL_GROUP * gi: POOL_GROUP * (gi + 1), POOL_GROUP * gi: POOL_GROUP * (gi + 1)]
                      for gi in range(4)], axis=0)


def kernel(x, ln_g, w_in, pool_w, pool_scale, conv_dw, conv_b, conv_ln_g, conv_ln_b, conv_pw, attn_sinks, w_out, final_g, loss_target, m_ln_g, m_w_in, m_pool_w, m_pool_scale, m_conv_dw, m_conv_b, m_conv_ln_g, m_conv_ln_b, m_conv_pw, m_attn_sinks, m_w_out, m_final_g, v_ln_g, v_w_in, v_pool_w, v_pool_scale, v_conv_dw, v_conv_b, v_conv_ln_g, v_conv_ln_b, v_conv_pw, v_attn_sinks, v_w_out, v_final_g):
    x0 = x[0]
    d = x0.shape[1]
    row = lambda a: a.reshape(1, -1)
    vec = _layer_vec
    slabs = lambda a: a.reshape(N_DEV, a.shape[0] // N_DEV, d)
    c_shard = CONV_WIDTH // N_DEV

    w_in_rows = [jnp.swapaxes(w_in[l], 0, 1).astype(MXU_DTYPE) for l in range(DEPTH)]
    w_out_rows = [w_out[l].astype(MXU_DTYPE) for l in range(DEPTH)]
    per_word = 4 // jnp.dtype(MXU_DTYPE).itemsize
    dw_t = jnp.stack([_dw_rows(conv_dw[l]) for l in range(DEPTH)], axis=0)
    dw_bits = (lax.bitcast_convert_type(dw_t, MXU_DTYPE) if per_word > 1 else dw_t).reshape(-1, d)
    n_pw = DEPTH * c_shard * CONV_WIDTH // d
    conv_rows = _pad_rows(jnp.concatenate([conv_pw.reshape(-1, d).astype(MXU_DTYPE), dw_bits], axis=0), 16)
    w_in_t, w_out_f = [None] * DEPTH, [None] * DEPTH
    w_in_t[0] = _all_gather(w_in_rows[0], "w_in_all_gather")
    wp_bd = [_block_diag(pool_w[l]).astype(MXU_DTYPE) for l in range(DEPTH)]

    xs, projs, cvs, ys = [x0], [], [], []
    q4 = D_IN // N_DEV // 4
    w_in_1 = [w_in_rows[1][q4 * k: q4 * (k + 1)] for k in range(4)]
    for l in range(DEPTH):
        if l == 0:
            proj, got = _in_proj(xs[0], vec(ln_g, 0), w_in_t[0], carried=[w_out_rows[0], conv_rows, w_in_1[0]])
            w_in_1_got = [got[2]]
            w_out_f[0] = got[0].reshape(D_MIX, d)
            pw_all = got[1][:, :n_pw].reshape(N_DEV, DEPTH, c_shard, CONV_WIDTH)
            pw_f = [pw_all[:, k].reshape(CONV_WIDTH, CONV_WIDTH) for k in range(DEPTH)]
            bits = got[1][:, n_pw:n_pw + dw_bits.shape[0]].reshape(
                (N_DEV, DEPTH, c_shard, CONV_TAPS_PAD) + (per_word,) * (per_word > 1))
            dw_all = lax.bitcast_convert_type(bits, F32) if per_word > 1 else bits
            dw_f = [jnp.swapaxes(dw_all[:, k].reshape(CONV_WIDTH, CONV_TAPS_PAD), 0, 1) for k in range(DEPTH)]
        (y_pc, cv, diff), got = _poolconv_fwd(proj, wp_bd[l], vec(pool_scale, l), dw_f[l], vec(conv_b, l),
                                              vec(conv_ln_g, l), vec(conv_ln_b, l), pw_f[l],
                                              carried=[w_in_1[1], w_in_1[2]] if l == 0 else [])
        if l == 0:
            w_in_1_got += list(got)
        (y_at, ya), got = _attn_fwd(proj, attn_sinks[l], carried=[w_out_rows[1], w_in_1[3]] if l == 0 else [])
        if l == 0:
            w_out_f[1] = got[0].reshape(D_MIX, d)
            w_in_t[1] = jnp.concatenate(w_in_1_got + [got[1]], axis=1).reshape(D_IN, d)
        projs.append(proj)
        cvs.append((cv, diff))
        ys.append((y_pc, y_at, ya))
        if l < DEPTH - 1:
            x_next, proj = _out_in_proj(xs[l], y_pc, y_at, w_out_f[l], vec(ln_g, l + 1), w_in_t[l + 1])
            xs.append(x_next)

    l = DEPTH - 1
    sq, g_final, dx = _out_proj_loss(xs[l], ys[l][0], ys[l][1], w_out_f[l], loss_target[0], row(final_g))

    l = 1
    dy, g_wout1 = _out_bwd(dx, ys[l][0], ys[l][1], w_out_f[l])
    (da, g_wp1, g_pw1, g_dw1, g_vec1), _ = _poolconv_bwd(
        projs[l], *cvs[l], dy, wp_bd[l], vec(pool_scale, l), dw_f[l], vec(conv_ln_g, l), vec(conv_ln_b, l), pw_f[l])
    (dattn, gs1), _ = _attn_bwd(projs[l], ys[l][2], dy, attn_sinks[l])
    dx, g_ln1, g_win_t1 = _in_bwd(da, dattn, xs[l], dx, vec(ln_g, l), w_in_t[l])
    l = 0
    dy, g_wout0 = _out_bwd(dx, ys[l][0], ys[l][1], w_out_f[l])
    (da, g_wp0, g_pw0, g_dw0, g_vec0), (r_win1, r_small1) = _poolconv_bwd(
        projs[l], *cvs[l], dy, wp_bd[l], vec(pool_scale, l), dw_f[l], vec(conv_ln_g, l), vec(conv_ln_b, l), pw_f[l],
        carried=[slabs(g_win_t1), _small_slabs(g_pw1, g_dw1, d)])
    (dattn, gs0), (r_wout0, r_wout1, r_small0) = _attn_bwd(
        projs[l], ys[l][2], dy, attn_sinks[l],
        carried=[slabs(g_wout0), slabs(g_wout1), _small_slabs(g_pw0, g_dw0, d)])
    dproj, g_win_t0 = _in_bwd_dw(da, dattn, xs[l], vec(ln_g, l))
    (dx, g_ln0), (r_win0,) = _in_bwd_dx(dproj, xs[l], dx, vec(ln_g, l), w_in_t[l], carried=[slabs(g_win_t0)])
    grad_x = dx[None]

    gv = jnp.stack([g_vec0, g_vec1], axis=0)
    rep_part = _pack_replicated(
        jnp.concatenate([g_ln0, g_ln1], axis=0), jnp.stack([_diag_blocks(g_wp0), _diag_blocks(g_wp1)], axis=0),
        gv[:, 0], gv[:, 1], gv[:, 2], gv[:, 3], jnp.stack([gs0[:, 0], gs1[:, 0]], axis=0), g_final, sq[0, 0], d)
    rep_sum = _final_all_reduce(rep_part)

    t = lambda a: jnp.swapaxes(a, 1, 2)
    win = [t(o) for o in _adamw_layers([r_win0, r_win1], t(w_in), t(m_w_in), t(v_w_in), "adamw_w_in")]
    wout = _adamw_layers([r_wout0, r_wout1], w_out, m_w_out, v_w_out, "adamw_w_out")
    pack_s = lambda pw_, dw_: jnp.stack([_pack_small(pw_[l], dw_[l], d) for l in range(DEPTH)], axis=0)
    small = _adamw_layers([r_small0, r_small1], pack_s(conv_pw, conv_dw), pack_s(m_conv_pw, m_conv_dw),
                          pack_s(v_conv_pw, v_conv_dw), "adamw_conv")
    small = [[_unpack_small(o[l], d) for l in range(DEPTH)] for o in small]
    *rep_grads, sq_sum = _unpack_replicated(rep_sum, d)
    loss = 0.5 / d * sq_sum
    rep_w = [ln_g, pool_w, pool_scale, conv_b, conv_ln_g, conv_ln_b, attn_sinks, final_g]
    rep_m = [m_ln_g, m_pool_w, m_pool_scale, m_conv_b, m_conv_ln_g, m_conv_ln_b, m_attn_sinks, m_final_g]
    rep_v = [v_ln_g, v_pool_w, v_pool_scale, v_conv_b, v_conv_ln_g, v_conv_ln_b, v_attn_sinks, v_final_g]
    rep = [rep_grads] + _adamw_tensors(rep_grads, rep_w, rep_m, rep_v)

    outs = []
    for k in range(4):
        r_ln, r_pool, r_scale, r_cb, r_lng, r_lnb, r_sinks, r_final = rep[k]
        s_pw = jnp.stack([small[k][l][0] for l in range(DEPTH)], axis=0)
        s_dw = jnp.stack([small[k][l][1] for l in range(DEPTH)], axis=0)
        outs += [r_ln, win[k], r_pool, r_scale, s_dw, r_cb, r_lng, r_lnb, s_pw, r_sinks, wout[k], r_final]
    return (loss, grad_x, *outs)
```

```python
import jax
import jax.numpy as jnp
from jax import lax
from jax.experimental import pallas as pl
from jax.experimental.pallas import tpu as pltpu

F32 = jnp.float32
MXU_DTYPE = jnp.bfloat16
EXCHANGE_DTYPE = jnp.bfloat16

N_DEV = 8
DEPTH = 2
POOL_WIDTH = 256
POOL_GROUP = 64
CONV_WIDTH = 256
CONV_KERNEL = 31
CONV_TAPS_PAD = 32
HEAD_DIM = 64
N_KV_HEADS = 2
Q_PER_KV = 4
N_Q_HEADS = 8
BLOCK = 128
D_MIX = 1024
D_IN = 2560
HALF_IN = 1280
EPS = 1e-6
SCALE = HEAD_DIM ** -0.5
NEG = -1e30

ADAM_LR = 0.001
ADAM_B1 = 0.9
ADAM_B2 = 0.999
ADAM_EPS = 1e-08
ADAM_WD = 0.01
ADAM_STEP = 10

HALO = 32
ROW_TILE = 512
IN_BWD_TILE = 512
POOLCONV_BWD_TILE = 1024
VMEM_LIMIT = 56 * 1024 * 1024

_NN = (((1,), (0,)), ((), ()))
_NT = (((1,), (1,)), ((), ()))
_TN = (((0,), (0,)), ((), ()))
_ANY = pl.BlockSpec(memory_space=pl.ANY)


def _mm(a, b, dims=_NN):
    return lax.dot_general(a.astype(MXU_DTYPE), b.astype(MXU_DTYPE), dims, preferred_element_type=F32)


def _sig(x):
    return 1.0 / (1.0 + jnp.exp(-x))


def _dsilu(z, s):
    return s * (1.0 + z * (1.0 - s))


def _params(n_grid):
    return pltpu.CompilerParams(dimension_semantics=("arbitrary",) * n_grid, vmem_limit_bytes=VMEM_LIMIT)


def _row_tile(rows, cap):
    t = min(rows, cap)
    while rows % t or t % 8:
        t -= 8
    return t


def _full(shape):
    return pl.BlockSpec(shape, lambda i: (0,) * len(shape))


def _layer_vec(stacked, layer):
    arr = stacked.reshape(stacked.shape[0], 1, stacked.shape[-1])
    return arr, pl.BlockSpec((None, 1, arr.shape[-1]), lambda i: (layer, 0, 0))


def _mesh_pos():
    return lax.axis_index("x"), lax.axis_index("y"), lax.axis_index("c")


class _Exchange:
    def __init__(self, sources):
        self.sources = list(sources)
        self.n = len(self.sources)
        self.gather = [s.ndim == 2 for s in self.sources]

    def out_shapes(self):
        return [jax.ShapeDtypeStruct((N_DEV,) + s.shape[-2:], s.dtype) for s in self.sources]

    def scratch(self):
        if not self.n:
            return []
        return [pltpu.SemaphoreType.DMA((7 * self.n,)), pltpu.SemaphoreType.DMA((7 * self.n,)),
                pltpu.SemaphoreType.DMA((self.n,))]

    def copies(self, src_refs, dst_refs, sems):
        send_sems, recv_sems, local_sems = sems
        x, y, c = _mesh_pos()
        me = 4 * x + 2 * y + c
        out = []
        for a, (src, dst) in enumerate(zip(src_refs, dst_refs)):
            out.append(pltpu.make_async_copy(src if self.gather[a] else src.at[me], dst.at[me], local_sems.at[a]))
            for k in range(1, N_DEV):
                tx, ty, tc = x ^ ((k >> 2) & 1), y ^ ((k >> 1) & 1), c ^ (k & 1)
                out.append(pltpu.make_async_remote_copy(
                    src_ref=src if self.gather[a] else src.at[4 * tx + 2 * ty + tc], dst_ref=dst.at[me],
                    send_sem=send_sems.at[7 * a + k - 1], recv_sem=recv_sems.at[7 * a + k - 1],
                    device_id=(tx, ty, tc), device_id_type=pl.DeviceIdType.MESH))
        return out

    def run(self, refs, first, last):
        if not self.n:
            return
        src_refs, dst_refs, sems = refs

        @pl.when(first)
        def _():
            for cp in self.copies(src_refs, dst_refs, sems):
                cp.start()

        @pl.when(last)
        def _():
            for cp in self.copies(src_refs, dst_refs, sems):
                cp.wait()


def _split_refs(refs, n_in, n_out, exch):
    ins = refs[:n_in]
    srcs = refs[n_in:n_in + exch.n]
    outs = refs[n_in + exch.n:n_in + exch.n + n_out]
    dsts = refs[n_in + exch.n + n_out:n_in + 2 * exch.n + n_out]
    rest = refs[n_in + 2 * exch.n + n_out:]
    sems = rest[len(rest) - 3:] if exch.n else ()
    scratch = rest[:len(rest) - 3] if exch.n else rest
    return ins, outs, scratch, (srcs, dsts, sems)


def _final_all_reduce(part):
    exch = _Exchange([part])

    def body(src_ref, sum_ref, dst_ref, send_sems, recv_sems, local_sems):
        xrefs = ((src_ref,), (dst_ref,), (send_sems, recv_sems, local_sems))
        for cp in exch.copies(*xrefs):
            cp.start()
        for cp in exch.copies(*xrefs):
            cp.wait()
        sum_ref[...] = _sum_partials(dst_ref)

    gathered = exch.out_shapes()[0]
    return pl.pallas_call(
        body, name="final_all_reduce",
        out_shape=jax.ShapeDtypeStruct(part.shape, F32),
        in_specs=[_ANY], out_specs=pl.BlockSpec(memory_space=pltpu.VMEM),
        scratch_shapes=[pltpu.VMEM(gathered.shape, gathered.dtype)] + exch.scratch(),
    )(part)


def _all_gather(shard, name):
    m_per, n = shard.shape

    def body(x_ref, out_ref, send_sems, recv_sems, local_sem):
        x, y, c = _mesh_pos()
        me, sibling = (x, y, c), (x, y, 1 - c)
        chips = [(1 - x, y), (x, 1 - y), (1 - x, 1 - y)]

        def rows(px, py, pc):
            return out_ref.at[pl.ds((4 * px + 2 * py + pc) * m_per, m_per), :]

        def copy(k, block, to, src=None):
            return pltpu.make_async_remote_copy(
                src_ref=rows(*block) if src is None else src, dst_ref=rows(*block),
                send_sem=send_sems.at[k], recv_sem=recv_sems.at[k],
                device_id=to, device_id_type=pl.DeviceIdType.MESH)

        mine = pltpu.make_async_copy(x_ref, rows(*me), local_sem)
        mine.start()
        first = [copy(0, me, sibling, src=x_ref)]
        first += [copy(1 + j, me, (*chip, c), src=x_ref) for j, chip in enumerate(chips)]
        for cp in first:
            cp.start()
        passed = [copy(4 + j, (*chip, c), sibling) for j, chip in enumerate(chips)]
        for j, chip in enumerate(chips):
            copy(1 + j, (*chip, c), me).wait_recv()
            passed[j].start()
        copy(0, sibling, me).wait_recv()
        for j, chip in enumerate(chips):
            copy(4 + j, (*chip, 1 - c), me).wait_recv()
        for cp in first + passed:
            cp.wait_send()
        mine.wait()

    return pl.pallas_call(
        body, name=name,
        out_shape=jax.ShapeDtypeStruct((N_DEV * m_per, n), shard.dtype),
        in_specs=[pl.BlockSpec(memory_space=pltpu.VMEM)],
        out_specs=pl.BlockSpec(memory_space=pltpu.VMEM),
        scratch_shapes=[pltpu.SemaphoreType.DMA((7,)), pltpu.SemaphoreType.DMA((7,)), pltpu.SemaphoreType.DMA],
        compiler_params=pltpu.CompilerParams(vmem_limit_bytes=VMEM_LIMIT),
    )(shard)


def _by_group(lane, v2, v4, v8, v16):
    return jnp.where(lane < 64, v2, jnp.where(lane < 128, v4, jnp.where(lane < 192, v8, v16)))


def _pool_count(t0, n):
    lane = lax.broadcasted_iota(jnp.int32, (1, POOL_WIDTH), 1)
    t = (t0 + lax.broadcasted_iota(jnp.int32, (n, 1), 0)).astype(F32)
    wnd = _by_group(lane, 2.0, 4.0, 8.0, 16.0)
    return jnp.minimum(t + 1.0, wnd)


def _pool_diff(u_ext, t0, ts):
    lane = lax.broadcasted_iota(jnp.int32, (1, POOL_WIDTH), 1)
    s2 = u_ext + pltpu.roll(u_ext, 1, 0)
    s4 = s2 + pltpu.roll(s2, 2, 0)
    s8 = s4 + pltpu.roll(s4, 4, 0)
    s16 = s8 + pltpu.roll(s8, 8, 0)
    pooled = _by_group(lane, s2, s4, s8, s16)[HALO:]
    return pooled / _pool_count(t0, ts) - u_ext[HALO:]


def _pool_diff_bwd(w, ts):
    n = w.shape[0]
    lane = lax.broadcasted_iota(jnp.int32, (1, POOL_WIDTH), 1)
    f2 = w + pltpu.roll(w, n - 1, 0)
    f4 = f2 + pltpu.roll(f2, n - 2, 0)
    f8 = f4 + pltpu.roll(f4, n - 4, 0)
    f16 = f8 + pltpu.roll(f8, n - 8, 0)
    return _by_group(lane, f2, f4, f8, f16)[:ts]


CONV_CHUNK = 64


def _conv_taps():
    return [(8 * m + r, r, m) for r in range(8) for m in range(4) if 8 * m + r < CONV_KERNEL]


def _store_shifted(dst_ref, x, up):
    n = x.shape[0]
    for r in range(8):
        dst_ref[r] = x if r == 0 else pltpu.roll(x, n - r if up else r, 0)


def _anticausal_conv(src_ref, dw_ref, out_ref, n_out):
    def chunk(c, carry):
        t0 = pl.multiple_of(c * CONV_CHUNK, CONV_CHUNK)
        acc = None
        for d, r, m in _conv_taps():
            term = dw_ref[pl.ds(CONV_KERNEL - 1 - d, 1), :] * src_ref[r, pl.ds(t0 + 8 * m, CONV_CHUNK), :]
            acc = term if acc is None else acc + term
        out_ref[pl.ds(t0, CONV_CHUNK), :] = acc
        return carry

    lax.fori_loop(0, n_out // CONV_CHUNK, chunk, 0)


def _depthwise_conv_weight_grad(x_ref, dout_up_ref, acc_ref, n_rows):
    acc_ref[...] = jnp.zeros_like(acc_ref)

    def chunk(c, carry):
        t0 = pl.multiple_of(c * CONV_CHUNK, CONV_CHUNK)
        xv = x_ref[pl.ds(t0, CONV_CHUNK), :]
        for d, r, m in _conv_taps():
            prod = xv * dout_up_ref[r, pl.ds(t0 + 8 * m, CONV_CHUNK), :]
            acc_ref[CONV_KERNEL - 1 - d] += jnp.sum(prod.reshape(CONV_CHUNK // 8, 8, prod.shape[-1]), axis=0)
        return carry

    lax.fori_loop(0, n_rows // CONV_CHUNK, chunk, 0)


def _layer_norm(cv):
    mu = jnp.mean(cv, axis=-1, keepdims=True)
    xc = cv - mu
    var = jnp.mean(xc * xc, axis=-1, keepdims=True)
    rstd = lax.rsqrt(var + EPS)
    return xc * rstd, rstd


PAIR_ROWS = 2 * BLOCK
PAIR_COLS = 4 * BLOCK
ATTN_BLOCKS_PER_STEP = 4


def _attn_blocks_per_step(s_len):
    qb = ATTN_BLOCKS_PER_STEP
    while (s_len // BLOCK) % qb:
        qb //= 2
    return qb


def _pair_rows(v0, v1):
    r = lax.broadcasted_iota(jnp.int32, (PAIR_ROWS, 1), 0)
    return jnp.where(r < BLOCK, v0, v1)


def _fill_attn_bias(bias_ref):
    rows = lax.broadcasted_iota(jnp.int32, (PAIR_ROWS, PAIR_COLS), 0)
    cols = lax.broadcasted_iota(jnp.int32, (PAIR_ROWS, PAIR_COLS), 1)
    key = cols & (2 * BLOCK - 1)
    dist = BLOCK + (rows & (BLOCK - 1)) - key
    in_band = (dist >= 0) & (dist < BLOCK)
    distf = dist.astype(F32)
    second = cols >= 2 * BLOCK
    for kh in range(N_KV_HEADS):
        slope_of = lambda j, a: 2.0 ** -(Q_PER_KV * kh + 2 * j + a + 1)
        slope = jnp.where(rows < BLOCK, jnp.where(second, slope_of(0, 1), slope_of(0, 0)),
                          jnp.where(second, slope_of(1, 1), slope_of(1, 0)))
        bias = -slope * distf
        bias_ref[0, kh] = jnp.where(in_band & (key >= BLOCK), bias, NEG)
        bias_ref[1, kh] = jnp.where(in_band, bias, NEG)


def _pair_block_matrix(x, x_swapped, kh):
    lo = lax.broadcasted_iota(jnp.int32, (1, 2 * HEAD_DIM), 1) < HEAD_DIM
    in_lo, in_hi = (x, x_swapped) if kh == 0 else (x_swapped, x)
    return jnp.concatenate([jnp.where(lo, in_lo, 0.0), jnp.where(lo, 0.0, in_hi)], axis=0).astype(MXU_DTYPE)


def _pair_queries(q, kh):
    return (_pair_stack(q, kh) * SCALE).astype(MXU_DTYPE)


def _pair_stack(a, kh):
    return jnp.concatenate([a[:, 2 * BLOCK * kh: 2 * BLOCK * kh + BLOCK],
                            a[:, 2 * BLOCK * kh + BLOCK: 2 * BLOCK * (kh + 1)]], axis=0)


def _pair_unstack(parts):
    return jnp.concatenate([p[BLOCK * j: BLOCK * (j + 1)] for p in parts for j in range(2)], axis=-1)


def _pair_softmax(s, kh, sinks_ref):
    ps, p_sinks = [], []
    for a in range(2):
        sa = s[:, 2 * BLOCK * a: 2 * BLOCK * (a + 1)]
        sink = _pair_rows(sinks_ref[Q_PER_KV * kh + a], sinks_ref[Q_PER_KV * kh + 2 + a])
        m = jnp.maximum(jnp.max(sa, axis=-1, keepdims=True), sink)
        e = jnp.exp(sa - m)
        es = jnp.exp(sink - m)
        inv = 1.0 / (jnp.sum(e, axis=-1, keepdims=True) + es)
        ps.append(e * inv)
        p_sinks.append(es * inv)
    return jnp.concatenate(ps, axis=-1), p_sinks


def _fold_pair_rows(t):
    return t[:HEAD_DIM, :2 * BLOCK] + t[HEAD_DIM:, 2 * BLOCK:]


def _in_proj(x, g, w_t, carried=()):
    s_len, d = x.shape
    ts = _row_tile(s_len, ROW_TILE)
    nt = s_len // ts
    exch = _Exchange(carried)

    def body(*refs):
        (x_ref, g_ref, w_ref), (o_ref,), _, xrefs = _split_refs(refs, 3, 1, exch)
        i = pl.program_id(0)
        exch.run(xrefs, i == 0, i == nt - 1)
        xv = x_ref[...]
        r = lax.rsqrt(jnp.mean(xv * xv, axis=-1, keepdims=True) + EPS)
        o_ref[...] = _mm(xv * r * g_ref[...], w_ref[...], _NT)

    res = pl.pallas_call(
        body, name="in_proj_carrier" if exch.n else "in_proj", grid=(nt,),
        in_specs=[pl.BlockSpec((ts, d), lambda i: (i, 0)), g[1], _full((D_IN, d))] + [_ANY] * exch.n,
        out_specs=[pl.BlockSpec((ts, D_IN), lambda i: (i, 0))] + [_ANY] * exch.n,
        out_shape=[jax.ShapeDtypeStruct((s_len, D_IN), F32)] + exch.out_shapes(),
        scratch_shapes=exch.scratch(),
        compiler_params=_params(1),
    )(x, g[0], w_t, *exch.sources)
    return res[0], res[1:]


def _poolconv_fwd(proj, wp, scale, dw, cb, lng, lnb, pw, carried=()):
    s_len = proj.shape[0]
    ts = _row_tile(s_len, ROW_TILE)
    hb = ts // HALO
    nt = s_len // ts
    exch = _Exchange(carried)

    def body(*refs):
        ins, (y_ref, cv_ref, diff_ref), _, xrefs = _split_refs(refs, 9, 3, exch)
        p_ref, ph_ref, wp_ref, sc_ref, dw_ref, cb_ref, lng_ref, lnb_ref, pw_ref = ins
        i = pl.program_id(0)
        exch.run(xrefs, i == 0, i == nt - 1)
        halo = jnp.where(i > 0, ph_ref[...], 0.0)
        y_mix, cv, diff = _pool_conv_mixers(p_ref[...], halo, i * ts, ts, wp_ref, sc_ref, dw_ref, cb_ref, lng_ref,
                                            lnb_ref, pw_ref)
        y_ref[...] = y_mix.astype(y_ref.dtype)
        cv_ref[...] = cv
        diff_ref[...] = diff

    res = pl.pallas_call(
        body, name="poolconv_fwd_carrier" if exch.n else "poolconv_fwd", grid=(nt,),
        in_specs=[pl.BlockSpec((ts, HALF_IN), lambda i: (i, 0)),
                  pl.BlockSpec((HALO, HALF_IN), lambda i: (jnp.maximum(i * hb - 1, 0), 0)),
                  _full((256, 256)), scale[1], _full((CONV_TAPS_PAD, 256)), cb[1], lng[1], lnb[1],
                  _full((256, 256))]
        + [_ANY] * exch.n,
        out_specs=[pl.BlockSpec((ts, 512), lambda i: (i, 0)), pl.BlockSpec((ts, 256), lambda i: (i, 0)),
                   pl.BlockSpec((ts, 256), lambda i: (i, 0))] + [_ANY] * exch.n,
        out_shape=[jax.ShapeDtypeStruct((s_len, 512), MXU_DTYPE), jax.ShapeDtypeStruct((s_len, 256), F32),
                   jax.ShapeDtypeStruct((s_len, 256), MXU_DTYPE)] + exch.out_shapes(),
        scratch_shapes=exch.scratch(),
        compiler_params=_params(1),
    )(proj, proj, wp, scale[0], dw, cb[0], lng[0], lnb[0], pw, *exch.sources)
    return res[:3], res[3:]


def _pool_conv_mixers(cur, halo, t0, ts, wp_ref, sc_ref, dw_ref, cb_ref, lng_ref, lnb_ref, pw_ref):
    ext = jnp.concatenate([halo, cur], axis=0)
    diff = _pool_diff(ext[:, 0:256], t0, ts).astype(MXU_DTYPE)
    gp = cur[:, 256:512]
    y_pool = _mm(diff, wp_ref[...]) * sc_ref[...] * (gp * _sig(gp))
    hh = ext[:, 512:768] * _sig(ext[:, 768:1024])
    shifted = [hh if r == 0 else pltpu.roll(hh, r, 0) for r in range(8)]
    cv = cb_ref[...]
    for dist, r, m in _conv_taps():
        cv = cv + dw_ref[pl.ds(CONV_KERNEL - 1 - dist, 1), :] * shifted[r][HALO - 8 * m: HALO - 8 * m + ts]
    n, _ = _layer_norm(cv)
    z = n * lng_ref[...] + lnb_ref[...]
    gc = cur[:, 1024:1280]
    y_conv = _mm(z * _sig(z), pw_ref[...]) * (gc * _sig(gc))
    return jnp.concatenate([y_pool, y_conv], axis=-1), cv, diff


def _attn_fwd(proj, sinks, carried=()):
    s_len = proj.shape[0]
    qb = _attn_blocks_per_step(s_len)
    ts = qb * BLOCK
    nt = s_len // ts
    exch = _Exchange(carried)

    def body(*refs):
        (p_ref, kvp_ref, sinks_ref), (y_ref, ya_ref), (bias_ref,), xrefs = _split_refs(refs, 3, 2, exch)
        i = pl.program_id(0)
        exch.run(xrefs, i == 0, i == nt - 1)

        @pl.when(i == 0)
        def _():
            _fill_attn_bias(bias_ref)

        probs = [(b, kh) for b in range(qb) for kh in range(N_KV_HEADS)]
        scores, v_bds = {}, {}
        for b in range(qb):
            r0 = BLOCK * b
            kv_prev = kvp_ref[...] if b == 0 else p_ref[r0 - BLOCK:r0, 512:768]
            kv2 = jnp.concatenate([kv_prev, p_ref[r0:r0 + BLOCK, 512:768]], axis=0)
            k2, v2 = kv2[:, :BLOCK], kv2[:, BLOCK:]
            k2_swapped, v2_swapped = pltpu.roll(k2, HEAD_DIM, 1), pltpu.roll(v2, HEAD_DIM, 1)
            variant = jnp.where(i == 0, 0, 1) if b == 0 else 1
            q = p_ref[r0:r0 + BLOCK, 0:512]
            for kh in range(N_KV_HEADS):
                k_bd = _pair_block_matrix(k2, k2_swapped, kh)
                v_bds[b, kh] = _pair_block_matrix(v2, v2_swapped, kh)
                scores[b, kh] = _mm(_pair_queries(q, kh), k_bd, _NT) + bias_ref[variant, kh]
        ps = {pr: _pair_softmax(scores[pr], pr[1], sinks_ref)[0] for pr in probs}
        outs = {pr: _mm(ps[pr], v_bds[pr]) for pr in probs}
        for b in range(qb):
            r0 = BLOCK * b
            ga = p_ref[r0:r0 + BLOCK, 768:1280]
            ya = _pair_unstack([outs[b, kh] for kh in range(N_KV_HEADS)])
            ya_ref[r0:r0 + BLOCK, :] = ya
            y_ref[r0:r0 + BLOCK, :] = (ya * (ga * _sig(ga))).astype(y_ref.dtype)

    res = pl.pallas_call(
        body, name="attn_fwd_carrier" if exch.n else "attn_fwd", grid=(nt,),
        in_specs=[pl.BlockSpec((ts, HALF_IN), lambda i: (i, 1)),
                  pl.BlockSpec((BLOCK, 256), lambda i: (jnp.maximum(i * qb - 1, 0), 7)),
                  pl.BlockSpec(memory_space=pltpu.SMEM)] + [_ANY] * exch.n,
        out_specs=[pl.BlockSpec((ts, 512), lambda i: (i, 0)), pl.BlockSpec((ts, 512), lambda i: (i, 0))]
        + [_ANY] * exch.n,
        out_shape=[jax.ShapeDtypeStruct((s_len, 512), MXU_DTYPE), jax.ShapeDtypeStruct((s_len, 512), F32)]
        + exch.out_shapes(),
        scratch_shapes=[pltpu.VMEM((2, N_KV_HEADS, PAIR_ROWS, PAIR_COLS), F32)] + exch.scratch(),
        compiler_params=_params(1),
    )(proj, proj, sinks, *exch.sources)
    return res[:2], res[2:]


def _out_in_proj(x, y_pc, y_at, w_out, g_next, w_t_next):
    s_len, d = x.shape
    ts = _row_tile(s_len, ROW_TILE)

    def body(x_ref, a_ref, b_ref, w_ref, g_ref, wn_ref, o_ref, p_ref):
        y = jnp.concatenate([a_ref[...], b_ref[...]], axis=-1)
        xv = x_ref[...] + _mm(y, w_ref[...])
        o_ref[...] = xv
        r = lax.rsqrt(jnp.mean(xv * xv, axis=-1, keepdims=True) + EPS)
        p_ref[...] = _mm(xv * r * g_ref[...], wn_ref[...], _NT)

    tile = pl.BlockSpec((ts, d), lambda i: (i, 0))
    half = pl.BlockSpec((ts, 512), lambda i: (i, 0))
    return pl.pallas_call(
        body, name="out_in_proj", grid=(s_len // ts,),
        in_specs=[tile, half, half, _full((D_MIX, d)), g_next[1], _full((D_IN, d))],
        out_specs=[tile, pl.BlockSpec((ts, D_IN), lambda i: (i, 0))],
        out_shape=[jax.ShapeDtypeStruct((s_len, d), F32), jax.ShapeDtypeStruct((s_len, D_IN), F32)],
        compiler_params=_params(1),
    )(x, y_pc, y_at, w_out, g_next[0], w_t_next)


def _out_proj_loss(x, y_pc, y_at, w_out, target, g):
    s_len, d = x.shape
    ts = _row_tile(s_len, ROW_TILE)

    def body(x_ref, a_ref, b_ref, w_ref, t_ref, g_ref, sq_ref, dg_ref, dx_ref):
        i = pl.program_id(0)
        y = jnp.concatenate([a_ref[...], b_ref[...]], axis=-1)
        xv = x_ref[...] + _mm(y, w_ref[...])
        gv = g_ref[...]
        r = lax.rsqrt(jnp.mean(xv * xv, axis=-1, keepdims=True) + EPS)
        xr = xv * r
        err = xr * gv - t_ref[...]
        dout = err * (1.0 / d)
        w = dout * gv
        dx_ref[...] = r * (w - xr * jnp.mean(w * xr, axis=-1, keepdims=True))

        @pl.when(i == 0)
        def _():
            sq_ref[...] = jnp.zeros_like(sq_ref)
            dg_ref[...] = jnp.zeros_like(dg_ref)

        sq = jnp.sum(jnp.sum(err * err, axis=-1, keepdims=True), axis=0, keepdims=True)
        sq_ref[...] += jnp.broadcast_to(sq, sq_ref.shape)
        dg_ref[...] += jnp.sum(dout * xr, axis=0, keepdims=True)

    tile = pl.BlockSpec((ts, d), lambda i: (i, 0))
    half = pl.BlockSpec((ts, 512), lambda i: (i, 0))
    return pl.pallas_call(
        body, name="out_proj_loss", grid=(s_len // ts,),
        in_specs=[tile, half, half, _full((D_MIX, d)), tile, _full((1, d))],
        out_specs=[_full((1, 128)), _full((1, d)), tile],
        out_shape=[jax.ShapeDtypeStruct((1, 128), F32), jax.ShapeDtypeStruct((1, d), F32),
                   jax.ShapeDtypeStruct((s_len, d), F32)],
        compiler_params=_params(1),
    )(x, y_pc, y_at, w_out, target, g)


def _out_bwd(dxo, y_pc, y_at, w_out):
    s_len, d = dxo.shape
    ts = _row_tile(s_len, ROW_TILE)
    nt = s_len // ts

    def body(dx_ref, a_ref, b_ref, w_ref, dy_ref, gw_ref, acc_ref):
        i = pl.program_id(0)
        dxv = dx_ref[...].astype(MXU_DTYPE)
        dy_ref[...] = _mm(dxv, w_ref[...], _NT)

        @pl.when(i == 0)
        def _():
            acc_ref[...] = jnp.zeros_like(acc_ref)

        y = jnp.concatenate([a_ref[...], b_ref[...]], axis=-1)
        acc_ref[...] += _mm(y, dxv, _TN)

        @pl.when(i == nt - 1)
        def _():
            gw_ref[...] = acc_ref[...].astype(gw_ref.dtype)

    return pl.pallas_call(
        body, name="out_bwd", grid=(nt,),
        in_specs=[pl.BlockSpec((ts, d), lambda i: (i, 0)), pl.BlockSpec((ts, 512), lambda i: (i, 0)),
                  pl.BlockSpec((ts, 512), lambda i: (i, 0)), _full((D_MIX, d))],
        out_specs=[pl.BlockSpec((ts, D_MIX), lambda i: (i, 0)), _full((D_MIX, d))],
        out_shape=[jax.ShapeDtypeStruct((s_len, D_MIX), F32), jax.ShapeDtypeStruct((D_MIX, d), EXCHANGE_DTYPE)],
        scratch_shapes=[pltpu.VMEM((D_MIX, d), F32)],
        compiler_params=_params(1),
    )(dxo, y_pc, y_at, w_out)


def _poolconv_bwd(proj, cv, diff, dy, wp, scale, dw, lng, lnb, pw, carried=()):
    s_len = proj.shape[0]
    ts = _row_tile(s_len, POOLCONV_BWD_TILE)
    hb = ts // HALO
    nt = s_len // ts
    last_halo = s_len // HALO - 1
    n = ts + HALO
    exch = _Exchange(carried)

    def body(*refs):
        ins, outs, (up_ref, hh_ref, dhh_ref, gdw_acc_ref), xrefs = _split_refs(refs, 13, 5, exch)
        (p_ref, pn_ref, cv_ref, cvn_ref, diff_ref, dy_ref, dyn_ref, wp_ref, sc_ref, dw_ref, lng_ref, lnb_ref,
         pw_ref) = ins
        da_ref, gwp_ref, gpw_ref, gdw_ref, gvec_ref = outs
        i = pl.program_id(0)
        exch.run(xrefs, i == 0, i == nt - 1)
        has_next = i < nt - 1
        cur = p_ref[...]
        nxt = jnp.where(has_next, pn_ref[...], 0.0)
        dyx = jnp.concatenate([dy_ref[...], jnp.where(has_next, dyn_ref[...], 0.0)], axis=0)
        row = lax.broadcasted_iota(jnp.int32, (n, 1), 0)
        in_seq = (row < ts) | has_next
        scale_v = sc_ref[...]

        cvx = jnp.concatenate([cv_ref[...], jnp.where(has_next, cvn_ref[...], 0.0)], axis=0)
        nrm, rstd = _layer_norm(cvx)
        z = nrm * lng_ref[...] + lnb_ref[...]
        sz = _sig(z)
        sw = z * sz
        gc = jnp.concatenate([cur[:, 1024:1280], nxt[:, 1024:1280]], axis=0)
        sgc = _sig(gc)
        yc = _mm(sw, pw_ref[...])
        dyc = dyx[:, 256:512]
        d_yc = dyc * (gc * sgc)
        d_gc = (dyc * yc * _dsilu(gc, sgc))[:ts]
        d_z = _mm(d_yc, pw_ref[...], _NT) * _dsilu(z, sz)
        d_n = d_z * lng_ref[...]
        d_cv = rstd * (d_n - jnp.mean(d_n, axis=-1, keepdims=True)
                       - nrm * jnp.mean(d_n * nrm, axis=-1, keepdims=True))
        d_cv = jnp.where(in_seq, d_cv, 0.0)
        _store_shifted(up_ref, d_cv, up=True)
        _anticausal_conv(up_ref, dw_ref, dhh_ref, ts)
        d_hh = dhh_ref[...]
        a_c, sb_c = cur[:, 512:768], _sig(cur[:, 768:1024])
        hh_ref[...] = a_c * sb_c
        d_a = d_hh * sb_c
        d_b = d_hh * a_c * sb_c * (1.0 - sb_c)
        d_cv_t = d_cv[:ts]
        _depthwise_conv_weight_grad(hh_ref, up_ref, gdw_acc_ref, ts)

        diff = diff_ref[...]
        raw = _mm(diff, wp_ref[...])
        gp = jnp.concatenate([cur[:, 256:512], nxt[:, 256:512]], axis=0)
        sgp = _sig(gp)
        dyp = dyx[:, 0:256]
        d_yp = dyp * (gp * sgp)
        d_gp = dyp[:ts] * (raw * scale_v) * _dsilu(gp, sgp)[:ts]
        d_raw = d_yp * scale_v
        d_diff = _mm(d_raw, wp_ref[...], _NT)
        w = jnp.where(in_seq, d_diff / _pool_count(i * ts, n), 0.0)
        d_u = _pool_diff_bwd(w, ts) - d_diff[:ts]

        da_ref[...] = jnp.concatenate([d_u, d_gp, d_a, d_b, d_gc], axis=-1).astype(da_ref.dtype)

        @pl.when(i == 0)
        def _():
            gwp_ref[...] = jnp.zeros_like(gwp_ref)
            gpw_ref[...] = jnp.zeros_like(gpw_ref)
            gdw_ref[...] = jnp.zeros_like(gdw_ref)
            gvec_ref[...] = jnp.zeros_like(gvec_ref)

        gwp_ref[...] += _mm(diff, d_raw[:ts], _TN)
        gpw_ref[...] += _mm(sw[:ts], d_yc[:ts], _TN)
        gdw_ref[...] += jnp.sum(gdw_acc_ref[...], axis=1)
        zero_row = jnp.zeros((1, 256), F32)
        gvec_ref[...] += jnp.concatenate([
            jnp.sum(d_yp[:ts] * raw, axis=0, keepdims=True),
            jnp.sum(d_cv_t, axis=0, keepdims=True),
            jnp.sum((d_z * nrm)[:ts], axis=0, keepdims=True),
            jnp.sum(d_z[:ts], axis=0, keepdims=True),
            zero_row, zero_row, zero_row, zero_row], axis=0)

    nxt_halo = lambda i: (jnp.minimum((i + 1) * hb, last_halo), 0)
    res = pl.pallas_call(
        body, name="poolconv_bwd_carrier" if exch.n else "poolconv_bwd", grid=(nt,),
        in_specs=[pl.BlockSpec((ts, HALF_IN), lambda i: (i, 0)), pl.BlockSpec((HALO, HALF_IN), nxt_halo),
                  pl.BlockSpec((ts, 256), lambda i: (i, 0)), pl.BlockSpec((HALO, 256), nxt_halo),
                  pl.BlockSpec((ts, 256), lambda i: (i, 0)),
                  pl.BlockSpec((ts, 512), lambda i: (i, 0)), pl.BlockSpec((HALO, 512), nxt_halo),
                  _full((256, 256)), scale[1], _full((CONV_TAPS_PAD, 256)), lng[1], lnb[1], _full((256, 256))]
        + [_ANY] * exch.n,
        out_specs=[pl.BlockSpec((ts, HALF_IN), lambda i: (i, 0)), _full((256, 256)), _full((256, 256)),
                   _full((CONV_TAPS_PAD, 256)), _full((8, 256))] + [_ANY] * exch.n,
        out_shape=[jax.ShapeDtypeStruct((s_len, HALF_IN), MXU_DTYPE), jax.ShapeDtypeStruct((256, 256), F32),
                   jax.ShapeDtypeStruct((256, 256), F32), jax.ShapeDtypeStruct((CONV_TAPS_PAD, 256), F32),
                   jax.ShapeDtypeStruct((8, 256), F32)] + exch.out_shapes(),
        scratch_shapes=[pltpu.VMEM((8, n, CONV_WIDTH), F32), pltpu.VMEM((ts, CONV_WIDTH), F32),
                        pltpu.VMEM((ts, CONV_WIDTH), F32), pltpu.VMEM((CONV_TAPS_PAD, 8, CONV_WIDTH), F32)]
        + exch.scratch(),
        compiler_params=_params(1),
    )(proj, proj, cv, cv, diff, dy, dy, wp, scale[0], dw, lng[0], lnb[0], pw, *exch.sources)
    return res[:5], res[5:]


DQ0, DKC0, DVC0, DKP0, DVP0, DGA0, DATTN_W = 0, 512, 640, 768, 896, 1024, 1536


def _attn_bwd(proj, ya, dy, sinks, carried=()):
    s_len = proj.shape[0]
    qb = _attn_blocks_per_step(s_len)
    ts = qb * BLOCK
    nt = s_len // ts
    exch = _Exchange(carried)

    def body(*refs):
        ins, (o_ref, gs_ref), (bias_ref,), xrefs = _split_refs(refs, 5, 2, exch)
        p_ref, kvp_ref, ya_ref, dy_ref, sinks_ref = ins
        i = pl.program_id(0)
        exch.run(xrefs, i == 0, i == nt - 1)

        @pl.when(i == 0)
        def _():
            _fill_attn_bias(bias_ref)
            gs_ref[...] = jnp.zeros_like(gs_ref)

        probs = [(b, kh) for b in range(qb) for kh in range(N_KV_HEADS)]
        kv_heads = range(N_KV_HEADS)
        scores, k_bds, v_bds, q2s, do2s, dyas, gas, sgas = {}, {}, {}, {}, {}, {}, {}, {}
        for b in range(qb):
            r0 = BLOCK * b
            kv_prev = kvp_ref[...] if b == 0 else p_ref[r0 - BLOCK:r0, 512:768]
            kv2 = jnp.concatenate([kv_prev, p_ref[r0:r0 + BLOCK, 512:768]], axis=0)
            k2, v2 = kv2[:, :BLOCK], kv2[:, BLOCK:]
            k2_swapped, v2_swapped = pltpu.roll(k2, HEAD_DIM, 1), pltpu.roll(v2, HEAD_DIM, 1)
            variant = jnp.where(i == 0, 0, 1) if b == 0 else 1
            q = p_ref[r0:r0 + BLOCK, 0:512]
            gas[b] = p_ref[r0:r0 + BLOCK, 768:1280]
            dyas[b] = dy_ref[r0:r0 + BLOCK, :]
            sgas[b] = _sig(gas[b])
            d_o = dyas[b] * (gas[b] * sgas[b])
            for kh in kv_heads:
                k_bds[b, kh] = _pair_block_matrix(k2, k2_swapped, kh)
                v_bds[b, kh] = _pair_block_matrix(v2, v2_swapped, kh)
                q2s[b, kh] = _pair_queries(q, kh)
                do2s[b, kh] = _pair_stack(d_o, kh)
                scores[b, kh] = _mm(q2s[b, kh], k_bds[b, kh], _NT) + bias_ref[variant, kh]
        ps, p_sinks, dps, dss, dqs, dks, dvs = {}, {}, {}, {}, {}, {}, {}
        d_sinks = [None] * N_Q_HEADS

        def softmax_stage(pr):
            ps[pr], p_sinks[pr] = _pair_softmax(scores[pr], pr[1], sinks_ref)

        def value_stage(pr):
            dps[pr] = _mm(do2s[pr], v_bds[pr], _NT)

        def score_grad_stage(pr):
            p, dp, kh = ps[pr], dps[pr], pr[1]
            ds_halves = []
            for a in range(2):
                cols = slice(2 * BLOCK * a, 2 * BLOCK * (a + 1))
                delta = jnp.sum(p[:, cols] * dp[:, cols], axis=-1, keepdims=True)
                ds_halves.append(p[:, cols] * (dp[:, cols] - delta))
                dsink = -p_sinks[pr][a] * delta
                for j in range(2):
                    part = jnp.sum(dsink[BLOCK * j: BLOCK * (j + 1)], axis=0, keepdims=True)
                    h = Q_PER_KV * kh + 2 * j + a
                    d_sinks[h] = part if d_sinks[h] is None else d_sinks[h] + part
            dss[pr] = jnp.concatenate(ds_halves, axis=-1)

        def operand_grad_stage(pr):
            dqs[pr] = _mm(dss[pr], k_bds[pr]) * SCALE
            dks[pr] = _fold_pair_rows(_mm(q2s[pr], dss[pr], _TN))
            dvs[pr] = _fold_pair_rows(_mm(do2s[pr], ps[pr], _TN))

        for stage in (softmax_stage, value_stage, score_grad_stage, operand_grad_stage):
            for pr in probs:
                stage(pr)
        for b in range(qb):
            r0 = BLOCK * b
            dk = jnp.concatenate([dks[b, kh] for kh in kv_heads], axis=0).T
            dv = jnp.concatenate([dvs[b, kh] for kh in kv_heads], axis=0).T
            d_ga = dyas[b] * ya_ref[r0:r0 + BLOCK, :] * _dsilu(gas[b], sgas[b])
            o_ref[r0:r0 + BLOCK, :] = jnp.concatenate(
                [_pair_unstack([dqs[b, kh] for kh in kv_heads]), dk[BLOCK:], dv[BLOCK:], dk[:BLOCK], dv[:BLOCK],
                 d_ga], axis=-1)
        gs_ref[...] += jnp.broadcast_to(jnp.concatenate(d_sinks, axis=0), gs_ref.shape)

    res = pl.pallas_call(
        body, name="attn_bwd_carrier" if exch.n else "attn_bwd", grid=(nt,),
        in_specs=[pl.BlockSpec((ts, HALF_IN), lambda i: (i, 1)),
                  pl.BlockSpec((BLOCK, 256), lambda i: (jnp.maximum(i * qb - 1, 0), 7)),
                  pl.BlockSpec((ts, 512), lambda i: (i, 0)), pl.BlockSpec((ts, 512), lambda i: (i, 1)),
                  pl.BlockSpec(memory_space=pltpu.SMEM)] + [_ANY] * exch.n,
        out_specs=[pl.BlockSpec((ts, DATTN_W), lambda i: (i, 0)), _full((N_Q_HEADS, 128))] + [_ANY] * exch.n,
        out_shape=[jax.ShapeDtypeStruct((s_len, DATTN_W), F32), jax.ShapeDtypeStruct((N_Q_HEADS, 128), F32)]
        + exch.out_shapes(),
        scratch_shapes=[pltpu.VMEM((2, N_KV_HEADS, PAIR_ROWS, PAIR_COLS), F32)] + exch.scratch(),
        compiler_params=_params(1),
    )(proj, proj, ya, dy, sinks, *exch.sources)
    return res[:2], res[2:]


def _in_bwd(da, dattn, x, dxo, g, w_t):
    s_len, d = x.shape
    ts = _row_tile(s_len, IN_BWD_TILE)
    bpt = ts // BLOCK
    nt = s_len // ts
    last_block = s_len // BLOCK - 1

    def body(da_ref, dat_ref, nxt_ref, x_ref, dxo_ref, g_ref, w_ref, dx_ref, dg_ref, gw_ref, acc_ref, stage_ref,
             stage_sem):
        i = pl.program_id(0)
        dat = dat_ref[...]
        nxt = jnp.where(i < nt - 1, nxt_ref[...], 0.0)
        shifted = jnp.concatenate([dat[BLOCK:, DKP0:DGA0], nxt], axis=0) if bpt > 1 else nxt
        dkv = dat[:, DKC0:DKP0] + shifted
        dproj = jnp.concatenate([da_ref[...], dat[:, DQ0:DKC0].astype(MXU_DTYPE), dkv.astype(MXU_DTYPE),
                                 dat[:, DGA0:DATTN_W].astype(MXU_DTYPE)], axis=-1)
        d_h = _mm(dproj, w_ref[...])
        xv = x_ref[...]
        gv = g_ref[...]
        r = lax.rsqrt(jnp.mean(xv * xv, axis=-1, keepdims=True) + EPS)
        xr = xv * r
        w = d_h * gv
        dx_ref[...] = dxo_ref[...] + r * (w - xr * jnp.mean(w * xr, axis=-1, keepdims=True))

        @pl.when(i == 0)
        def _():
            dg_ref[...] = jnp.zeros_like(dg_ref)
            acc_ref[...] = jnp.zeros_like(acc_ref)

        dg_ref[...] += jnp.sum(d_h * xr, axis=0, keepdims=True)
        acc_ref[...] += _mm(dproj, xr * gv, _TN)

        @pl.when(i == nt - 1)
        def _():
            stage_ref[...] = acc_ref[...].astype(stage_ref.dtype)
            out = pltpu.make_async_copy(stage_ref, gw_ref, stage_sem)
            out.start()
            out.wait()

    return pl.pallas_call(
        body, name="in_bwd", grid=(nt,),
        in_specs=[pl.BlockSpec((ts, HALF_IN), lambda i: (i, 0)),
                  pl.BlockSpec((ts, DATTN_W), lambda i: (i, 0)),
                  pl.BlockSpec((BLOCK, 256), lambda i: (jnp.minimum((i + 1) * bpt, last_block), 3)),
                  pl.BlockSpec((ts, d), lambda i: (i, 0)), pl.BlockSpec((ts, d), lambda i: (i, 0)),
                  g[1], pl.BlockSpec((D_IN, d), lambda i: (0, 0), pipeline_mode=pl.Buffered(1))],
        out_specs=[pl.BlockSpec((ts, d), lambda i: (i, 0)), _full((1, d)), _ANY],
        out_shape=[jax.ShapeDtypeStruct((s_len, d), F32), jax.ShapeDtypeStruct((1, d), F32),
                   jax.ShapeDtypeStruct((D_IN, d), EXCHANGE_DTYPE)],
        scratch_shapes=[pltpu.VMEM((D_IN, d), F32), pltpu.VMEM((D_IN, d), EXCHANGE_DTYPE), pltpu.SemaphoreType.DMA],
        compiler_params=_params(1),
    )(da, dattn, dattn, x, dxo, g[0], w_t)


def _in_bwd_dw(da, dattn, x, g):
    s_len, d = x.shape
    ts = _row_tile(s_len, IN_BWD_TILE)
    bpt = ts // BLOCK
    nt = s_len // ts
    last_block = s_len // BLOCK - 1

    def body(da_ref, dat_ref, nxt_ref, x_ref, g_ref, dp_ref, gw_ref, acc_ref, stage_ref, stage_sem):
        i = pl.program_id(0)
        dat = dat_ref[...]
        nxt = jnp.where(i < nt - 1, nxt_ref[...], 0.0)
        shifted = jnp.concatenate([dat[BLOCK:, DKP0:DGA0], nxt], axis=0) if bpt > 1 else nxt
        dkv = dat[:, DKC0:DKP0] + shifted
        dproj = jnp.concatenate([da_ref[...], dat[:, DQ0:DKC0].astype(MXU_DTYPE), dkv.astype(MXU_DTYPE),
                                 dat[:, DGA0:DATTN_W].astype(MXU_DTYPE)], axis=-1)
        dp_ref[...] = dproj
        xv = x_ref[...]
        r = lax.rsqrt(jnp.mean(xv * xv, axis=-1, keepdims=True) + EPS)

        @pl.when(i == 0)
        def _():
            acc_ref[...] = jnp.zeros_like(acc_ref)

        acc_ref[...] += _mm(dproj, xv * r * g_ref[...], _TN)

        @pl.when(i == nt - 1)
        def _():
            stage_ref[...] = acc_ref[...].astype(stage_ref.dtype)
            out = pltpu.make_async_copy(stage_ref, gw_ref, stage_sem)
            out.start()
            out.wait()

    return pl.pallas_call(
        body, name="in_bwd_dw", grid=(nt,),
        in_specs=[pl.BlockSpec((ts, HALF_IN), lambda i: (i, 0)),
                  pl.BlockSpec((ts, DATTN_W), lambda i: (i, 0)),
                  pl.BlockSpec((BLOCK, 256), lambda i: (jnp.minimum((i + 1) * bpt, last_block), 3)),
                  pl.BlockSpec((ts, d), lambda i: (i, 0)), g[1]],
        out_specs=[pl.BlockSpec((ts, D_IN), lambda i: (i, 0)), _ANY],
        out_shape=[jax.ShapeDtypeStruct((s_len, D_IN), MXU_DTYPE), jax.ShapeDtypeStruct((D_IN, d), EXCHANGE_DTYPE)],
        scratch_shapes=[pltpu.VMEM((D_IN, d), F32), pltpu.VMEM((D_IN, d), EXCHANGE_DTYPE), pltpu.SemaphoreType.DMA],
        compiler_params=_params(1),
    )(da, dattn, dattn, x, g[0])


def _in_bwd_dx(dproj, x, dxo, g, w_t, carried=()):
    s_len, d = x.shape
    ts = _row_tile(s_len, ROW_TILE)
    nt = s_len // ts
    exch = _Exchange(carried)

    def body(*refs):
        (dp_ref, x_ref, dxo_ref, g_ref, w_ref), (dx_ref, dg_ref), _, xrefs = _split_refs(refs, 5, 2, exch)
        i = pl.program_id(0)
        exch.run(xrefs, i == 0, i == nt - 1)
        d_h = _mm(dp_ref[...], w_ref[...])
        xv = x_ref[...]
        r = lax.rsqrt(jnp.mean(xv * xv, axis=-1, keepdims=True) + EPS)
        xr = xv * r
        w = d_h * g_ref[...]
        dx_ref[...] = dxo_ref[...] + r * (w - xr * jnp.mean(w * xr, axis=-1, keepdims=True))

        @pl.when(i == 0)
        def _():
            dg_ref[...] = jnp.zeros_like(dg_ref)

        dg_ref[...] += jnp.sum(d_h * xr, axis=0, keepdims=True)

    tile = pl.BlockSpec((ts, d), lambda i: (i, 0))
    res = pl.pallas_call(
        body, name="in_bwd_dx_carrier" if exch.n else "in_bwd_dx", grid=(nt,),
        in_specs=[pl.BlockSpec((ts, D_IN), lambda i: (i, 0)), tile, tile, g[1], _full((D_IN, d))]
        + [_ANY] * exch.n,
        out_specs=[tile, _full((1, d))] + [_ANY] * exch.n,
        out_shape=[jax.ShapeDtypeStruct((s_len, d), F32), jax.ShapeDtypeStruct((1, d), F32)] + exch.out_shapes(),
        scratch_shapes=exch.scratch(),
        compiler_params=_params(1),
    )(dproj, x, dxo, g[0], w_t, *exch.sources)
    return res[:2], res[2:]


def _sum_partials(p_ref):
    g = p_ref[0].astype(F32)
    for k in range(1, N_DEV):
        g = g + p_ref[k].astype(F32)
    return g


def _adamw_step(g, w, m, v):
    nm = ADAM_B1 * m + (1.0 - ADAM_B1) * g
    nv = ADAM_B2 * v + (1.0 - ADAM_B2) * (g * g)
    m_hat = nm / (1.0 - ADAM_B1 ** ADAM_STEP)
    v_hat = nv / (1.0 - ADAM_B2 ** ADAM_STEP)
    return -ADAM_LR * (m_hat / (jnp.sqrt(v_hat) + ADAM_EPS) + ADAM_WD * w), nm, nv


def _adamw_layers(parts, w, m, v, name):
    _, rows, n = w.shape
    tr = _row_tile(rows, 64)

    def body(*refs):
        p_refs = refs[:DEPTH]
        w_ref, m_ref, v_ref, g_ref, d_ref, nm_ref, nv_ref = refs[DEPTH:]
        for l in range(DEPTH):
            g = _sum_partials(p_refs[l])
            g_ref[l] = g
            d_ref[l], nm_ref[l], nv_ref[l] = _adamw_step(g, w_ref[l], m_ref[l], v_ref[l])

    tile = pl.BlockSpec((DEPTH, tr, n), lambda i: (0, i, 0))
    shape = jax.ShapeDtypeStruct(w.shape, F32)
    return pl.pallas_call(
        body, name=name, grid=(rows // tr,),
        in_specs=[pl.BlockSpec((N_DEV, tr, n), lambda i: (0, i, 0))] * DEPTH + [tile] * 3,
        out_specs=[tile] * 4, out_shape=[shape] * 4,
        compiler_params=_params(1),
    )(*parts, w, m, v)


def _adamw_tensors(gs, ws, ms, vs):
    n = len(ws)
    shapes = [w.shape for w in ws]
    two_d = lambda a: a.reshape(1, -1) if a.ndim == 1 else a
    ops = [two_d(a) for a in list(gs) + list(ws) + list(ms) + list(vs)]

    def body(*refs):
        ins, outs = refs[:4 * n], refs[4 * n:]
        for k in range(n):
            g, w, m, v = (ins[j * n + k][...] for j in range(4))
            outs[k][...], outs[n + k][...], outs[2 * n + k][...] = _adamw_step(g, w, m, v)

    vmem = pl.BlockSpec(memory_space=pltpu.VMEM)
    res = pl.pallas_call(
        body, name="adamw_replicated", in_specs=[vmem] * (4 * n), out_specs=[vmem] * (3 * n),
        out_shape=[jax.ShapeDtypeStruct(o.shape, F32) for o in ops[n:2 * n]] * 3,
    )(*ops)
    return [[res[j * n + k].reshape(shapes[k]) for k in range(n)] for j in range(3)]


def _pad_rows(a, mult):
    pad = (-a.shape[0]) % mult
    return a if pad == 0 else jnp.concatenate([a, jnp.zeros((pad, a.shape[1]), a.dtype)], axis=0)


def _dw_rows(conv_dw_l):
    return jnp.pad(jnp.swapaxes(conv_dw_l, 0, 1), ((0, 0), (0, CONV_TAPS_PAD - CONV_KERNEL)))


def _pack_small(pw_l, dw_l, d):
    rows = jnp.concatenate([pw_l.reshape(-1, d), _dw_rows(dw_l).reshape(-1, d)], axis=0)
    return _pad_rows(rows, 8)


def _small_slabs(g_pw, g_dw, d):
    a = g_pw.reshape(N_DEV, -1, d)
    b = jnp.swapaxes(g_dw, 0, 1).reshape(N_DEV, -1, d)
    used = a.shape[1] + b.shape[1]
    return jnp.concatenate([a, b, jnp.zeros((N_DEV, (-used) % 8, d), g_pw.dtype)], axis=1)


def _unpack_small(rows, d):
    c = CONV_WIDTH // N_DEV
    n_pw = c * CONV_WIDTH // d
    n_dw = c * CONV_TAPS_PAD // d
    pw = rows[:n_pw].reshape(c, CONV_WIDTH)
    dw = jnp.swapaxes(rows[n_pw:n_pw + n_dw].reshape(c, CONV_TAPS_PAD), 0, 1)[:CONV_KERNEL]
    return pw, dw


def _pack_replicated(ln_g, pool_w, pool_scale, conv_b, conv_ln_g, conv_ln_b, attn_sinks, final_g, scalar, d):
    sinks = jnp.pad(attn_sinks, ((0, 0), (0, 256 - N_Q_HEADS)))
    small = jnp.concatenate([pool_scale, conv_b, conv_ln_g, conv_ln_b, sinks], axis=0)
    small = _pad_rows(small, d // 256)
    last = jnp.pad(scalar.reshape(1, 1), ((0, 0), (0, d - 1)))
    return _pad_rows(jnp.concatenate([ln_g.reshape(-1, d), final_g.reshape(-1, d), pool_w.reshape(-1, d),
                                      small.reshape(-1, d), last], axis=0), 8)


def _unpack_replicated(rows, d):
    n_pool = DEPTH * 4 * POOL_GROUP * POOL_GROUP // d
    n_small = -(-5 * DEPTH * 256 // d)
    ln_g = rows[:DEPTH]
    final_g = rows[DEPTH]
    pool_w = rows[DEPTH + 1: DEPTH + 1 + n_pool].reshape(DEPTH, 4, POOL_GROUP, POOL_GROUP)
    small = rows[DEPTH + 1 + n_pool: DEPTH + 1 + n_pool + n_small].reshape(-1, 256)[: 5 * DEPTH]
    pool_scale, conv_b, conv_ln_g, conv_ln_b = (small[DEPTH * k: DEPTH * (k + 1)] for k in range(4))
    sinks = small[4 * DEPTH: 5 * DEPTH, :N_Q_HEADS]
    scalar = rows[DEPTH + 1 + n_pool + n_small, 0]
    return ln_g, pool_w, pool_scale, conv_b, conv_ln_g, conv_ln_b, sinks, final_g, scalar


def _block_diag(pool_w):
    wide = jnp.tile(pool_w.reshape(POOL_WIDTH, POOL_GROUP), (1, POOL_WIDTH // POOL_GROUP))
    rows = lax.broadcasted_iota(jnp.int32, wide.shape, 0) // POOL_GROUP
    cols = lax.broadcasted_iota(jnp.int32, wide.shape, 1) // POOL_GROUP
    return jnp.where(rows == cols, wide, jnp.zeros_like(wide))


def _diag_blocks(mat):
    return jnp.stack([mat[POOL_GROUP * gi: POOL_GROUP * (gi + 1), POOL_GROUP * gi: POOL_GROUP * (gi + 1)]
                      for gi in range(4)], axis=0)


def kernel(x, ln_g, w_in, pool_w, pool_scale, conv_dw, conv_b, conv_ln_g, conv_ln_b, conv_pw, attn_sinks, w_out, final_g, loss_target, m_ln_g, m_w_in, m_pool_w, m_pool_scale, m_conv_dw, m_conv_b, m_conv_ln_g, m_conv_ln_b, m_conv_pw, m_attn_sinks, m_w_out, m_final_g, v_ln_g, v_w_in, v_pool_w, v_pool_scale, v_conv_dw, v_conv_b, v_conv_ln_g, v_conv_ln_b, v_conv_pw, v_attn_sinks, v_w_out, v_final_g):
    x0 = x[0]
    d = x0.shape[1]
    row = lambda a: a.reshape(1, -1)
    vec = _layer_vec
    slabs = lambda a: a.reshape(N_DEV, a.shape[0] // N_DEV, d)
    c_shard = CONV_WIDTH // N_DEV

    w_in_rows = [jnp.swapaxes(w_in[l], 0, 1).astype(MXU_DTYPE) for l in range(DEPTH)]
    w_out_rows = [w_out[l].astype(MXU_DTYPE) for l in range(DEPTH)]
    per_word = 4 // jnp.dtype(MXU_DTYPE).itemsize
    dw_t = jnp.stack([_dw_rows(conv_dw[l]) for l in range(DEPTH)], axis=0)
    dw_bits = (lax.bitcast_convert_type(dw_t, MXU_DTYPE) if per_word > 1 else dw_t).reshape(-1, d)
    n_pw = DEPTH * c_shard * CONV_WIDTH // d
    conv_rows = _pad_rows(jnp.concatenate([conv_pw.reshape(-1, d).astype(MXU_DTYPE), dw_bits], axis=0), 16)
    w_in_t, w_out_f = [None] * DEPTH, [None] * DEPTH
    w_in_t[0] = _all_gather(w_in_rows[0], "w_in_all_gather")
    wp_bd = [_block_diag(pool_w[l]).astype(MXU_DTYPE) for l in range(DEPTH)]

    xs, projs, cvs, ys = [x0], [], [], []
    q4 = D_IN // N_DEV // 4
    w_in_1 = [w_in_rows[1][q4 * k: q4 * (k + 1)] for k in range(4)]
    for l in range(DEPTH):
        if l == 0:
            proj, got = _in_proj(xs[0], vec(ln_g, 0), w_in_t[0], carried=[w_out_rows[0], conv_rows, w_in_1[0]])
            w_in_1_got = [got[2]]
            w_out_f[0] = got[0].reshape(D_MIX, d)
            pw_all = got[1][:, :n_pw].reshape(N_DEV, DEPTH, c_shard, CONV_WIDTH)
            pw_f = [pw_all[:, k].reshape(CONV_WIDTH, CONV_WIDTH) for k in range(DEPTH)]
            bits = got[1][:, n_pw:n_pw + dw_bits.shape[0]].reshape(
                (N_DEV, DEPTH, c_shard, CONV_TAPS_PAD) + (per_word,) * (per_word > 1))
            dw_all = lax.bitcast_convert_type(bits, F32) if per_word > 1 else bits
            dw_f = [jnp.swapaxes(dw_all[:, k].reshape(CONV_WIDTH, CONV_TAPS_PAD), 0, 1) for k in range(DEPTH)]
        (y_pc, cv, diff), got = _poolconv_fwd(proj, wp_bd[l], vec(pool_scale, l), dw_f[l], vec(conv_b, l),
                                              vec(conv_ln_g, l), vec(conv_ln_b, l), pw_f[l],
                                              carried=[w_in_1[1], w_in_1[2]] if l == 0 else [])
        if l == 0:
            w_in_1_got += list(got)
        (y_at, ya), got = _attn_fwd(proj, attn_sinks[l], carried=[w_out_rows[1], w_in_1[3]] if l == 0 else [])
        if l == 0:
            w_out_f[1] = got[0].reshape(D_MIX, d)
            w_in_t[1] = jnp.concatenate(w_in_1_got + [got[1]], axis=1).reshape(D_IN, d)
        projs.append(proj)
        cvs.append((cv, diff))
        ys.append((y_pc, y_at, ya))
        if l < DEPTH - 1:
            x_next, proj = _out_in_proj(xs[l], y_pc, y_at, w_out_f[l], vec(ln_g, l + 1), w_in_t[l + 1])
            xs.append(x_next)

    l = DEPTH - 1
    sq, g_final, dx = _out_proj_loss(xs[l], ys[l][0], ys[l][1], w_out_f[l], loss_target[0], row(final_g))

    l = 1
    dy, g_wout1 = _out_bwd(dx, ys[l][0], ys[l][1], w_out_f[l])
    (da, g_wp1, g_pw1, g_dw1, g_vec1), _ = _poolconv_bwd(
        projs[l], *cvs[l], dy, wp_bd[l], vec(pool_scale, l), dw_f[l], vec(conv_ln_g, l), vec(conv_ln_b, l), pw_f[l])
    (dattn, gs1), _ = _attn_bwd(projs[l], ys[l][2], dy, attn_sinks[l])
    dx, g_ln1, g_win_t1 = _in_bwd(da, dattn, xs[l], dx, vec(ln_g, l), w_in_t[l])
    l = 0
    dy, g_wout0 = _out_bwd(dx, ys[l][0], ys[l][1], w_out_f[l])
    (da, g_wp0, g_pw0, g_dw0, g_vec0), (r_win1, r_small1) = _poolconv_bwd(
        projs[l], *cvs[l], dy, wp_bd[l], vec(pool_scale, l), dw_f[l], vec(conv_ln_g, l), vec(conv_ln_b, l), pw_f[l],
        carried=[slabs(g_win_t1), _small_slabs(g_pw1, g_dw1, d)])
    (dattn, gs0), (r_wout0, r_wout1, r_small0) = _attn_bwd(
        projs[l], ys[l][2], dy, attn_sinks[l],
        carried=[slabs(g_wout0), slabs(g_wout1), _small_slabs(g_pw0, g_dw0, d)])
    dproj, g_win_t0 = _in_bwd_dw(da, dattn, xs[l], vec(ln_g, l))
    (dx, g_ln0), (r_win0,) = _in_bwd_dx(dproj, xs[l], dx, vec(ln_g, l), w_in_t[l], carried=[slabs(g_win_t0)])
    grad_x = dx[None]

    gv = jnp.stack([g_vec0, g_vec1], axis=0)
    rep_part = _pack_replicated(
        jnp.concatenate([g_ln0, g_ln1], axis=0), jnp.stack([_diag_blocks(g_wp0), _diag_blocks(g_wp1)], axis=0),
        gv[:, 0], gv[:, 1], gv[:, 2], gv[:, 3], jnp.stack([gs0[:, 0], gs1[:, 0]], axis=0), g_final, sq[0, 0], d)
    rep_sum = _final_all_reduce(rep_part)

    t = lambda a: jnp.swapaxes(a, 1, 2)
    win = [t(o) for o in _adamw_layers([r_win0, r_win1], t(w_in), t(m_w_in), t(v_w_in), "adamw_w_in")]
    wout = _adamw_layers([r_wout0, r_wout1], w_out, m_w_out, v_w_out, "adamw_w_out")
    pack_s = lambda pw_, dw_: jnp.stack([_pack_small(pw_[l], dw_[l], d) for l in range(DEPTH)], axis=0)
    small = _adamw_layers([r_small0, r_small1], pack_s(conv_pw, conv_dw), pack_s(m_conv_pw, m_conv_dw),
                          pack_s(v_conv_pw, v_conv_dw), "adamw_conv")
    small = [[_unpack_small(o[l], d) for l in range(DEPTH)] for o in small]
    *rep_grads, sq_sum = _unpack_replicated(rep_sum, d)
    loss = 0.5 / d * sq_sum
    rep_w = [ln_g, pool_w, pool_scale, conv_b, conv_ln_g, conv_ln_b, attn_sinks, final_g]
    rep_m = [m_ln_g, m_pool_w, m_pool_scale, m_conv_b, m_conv_ln_g, m_conv_ln_b, m_attn_sinks, m_final_g]
    rep_v = [v_ln_g, v_pool_w, v_pool_scale, v_conv_b, v_conv_ln_g, v_conv_ln_b, v_attn_sinks, v_final_g]
    rep = [rep_grads] + _adamw_tensors(rep_grads, rep_w, rep_m, rep_v)

    outs = []
    for k in range(4):
        r_ln, r_pool, r_scale, r_cb, r_lng, r_lnb, r_sinks, r_final = rep[k]
        s_pw = jnp.stack([small[k][l][0] for l in range(DEPTH)], axis=0)
        s_dw = jnp.stack([small[k][l][1] for l in range(DEPTH)], axis=0)
        outs += [r_ln, win[k], r_pool, r_scale, s_dw, r_cb, r_lng, r_lnb, s_pw, r_sinks, wout[k], r_final]
    return (loss, grad_x, *outs)
```
